```python
import math
import jax, jax.numpy as jnp
from jax import lax
import numpy as np

D_MODEL = 1024
BATCH = 2
SEQ = 8192
DEPTH = 1

HEAD_DIM = 64
SWA_Q_HEADS = 8
SWA_KV_HEADS = 2
SWA_GROUP = SWA_Q_HEADS // SWA_KV_HEADS
WINDOW = 128
DIFF_HEADS = 4
DIFF_V_DIM = 2 * HEAD_DIM
MIX_A_WIDTH = SWA_Q_HEADS * HEAD_DIM
MIX_B_WIDTH = DIFF_HEADS * DIFF_V_DIM
MIX_WIDTH = MIX_A_WIDTH + MIX_B_WIDTH
A_Q = SWA_Q_HEADS * HEAD_DIM
A_K = SWA_KV_HEADS * HEAD_DIM
A_V = SWA_KV_HEADS * HEAD_DIM
B_Q = DIFF_HEADS * 2 * HEAD_DIM
B_K = DIFF_HEADS * 2 * HEAD_DIM
B_V = DIFF_HEADS * DIFF_V_DIM
IN_COLS = A_Q + A_K + A_V + B_Q + B_K + B_V
IN_SPLITS = [int(v) for v in np.cumsum([A_Q, A_K, A_V, B_Q, B_K])]
Q_BLOCK = 128
ROPE_THETA = 10000.0
N_EXPERTS = 32
TOP_K = 4
D_FF = D_MODEL
SWIGLU_ALPHA = 1.702
SWIGLU_LIMIT = 7.0
EXPERT_BLOCK = 128
EPS = 1e-5
N_MOD = 6

kernel_name = "hymba_swa_sink_diffattn_moe_adaln"


def rms_norm(x, g):
    xf = x.astype(jnp.float32)
    y = xf * lax.rsqrt(jnp.mean(xf * xf, axis=-1, keepdims=True) + EPS)
    return (y * g.astype(jnp.float32)).astype(x.dtype)


def modulate(h, shift, scale):
    return h * (1 + scale[:, None, :]) + shift[:, None, :]


def rope_tables(positions):
    inv = 1.0 / (ROPE_THETA ** (jnp.arange(0, HEAD_DIM, 2, dtype=jnp.float32) / HEAD_DIM))
    ang = positions.astype(jnp.float32)[..., None] * inv
    return jnp.cos(ang)[:, :, None, :], jnp.sin(ang)[:, :, None, :]


def apply_rope(t, cos, sin):
    tf = t.astype(jnp.float32)
    t1, t2 = jnp.split(tf, 2, axis=-1)
    out = jnp.concatenate([t1 * cos - t2 * sin, t2 * cos + t1 * sin], axis=-1)
    return out.astype(t.dtype)


def sliding_window_sink_attention(q, k, v, sinks):
    B, S, _, D = q.shape
    nb = S // WINDOW
    qb = q.reshape(B, nb, WINDOW, SWA_KV_HEADS, SWA_GROUP, D)

    def with_prev(t):
        tb = t.reshape(B, nb, WINDOW, SWA_KV_HEADS, D)
        prev = jnp.pad(tb, ((0, 0), (1, 0), (0, 0), (0, 0), (0, 0)))[:, :-1]
        return jnp.concatenate([prev, tb], axis=2)

    kb, vb = with_prev(k), with_prev(v)
    s = jnp.einsum('bnikgd,bnjkd->bnkgij', qb, kb,
                   preferred_element_type=jnp.float32) / math.sqrt(D)
    qi = jnp.arange(WINDOW) + WINDOW
    kj = jnp.arange(2 * WINDOW)
    diff = qi[:, None] - kj[None, :]
    band = (diff >= 0) & (diff < WINDOW)
    blk = jnp.arange(nb)[:, None, None]
    mask = band[None] & ((blk > 0) | (kj[None, None, :] >= WINDOW))
    s = jnp.where(mask[None, :, None, None], s, -jnp.inf)
    sink = sinks.astype(jnp.float32).reshape(SWA_KV_HEADS, SWA_GROUP)[None, None, :, :, None, None]
    m = jnp.maximum(jnp.max(s, axis=-1, keepdims=True), sink)
    p = jnp.exp(s - m)
    p = p / (jnp.sum(p, axis=-1, keepdims=True) + jnp.exp(sink - m))
    o = jnp.einsum('bnkgij,bnjkd->bnikgd', p.astype(v.dtype), vb)
    return o.reshape(B, S, SWA_Q_HEADS * D)


def differential_attention(q, k, v, lam, g_subln, lambda_init):
    B, S, H, _, D = q.shape
    nb = S // Q_BLOCK
    qb = jnp.moveaxis(q.reshape(B, nb, Q_BLOCK, H, 2, D), 1, 0)
    kpos = jnp.arange(S)
    scale = 1.0 / math.sqrt(D)

    def block(args):
        qblk, start = args
        s = jnp.einsum('bihmd,bjhmd->bhmij', qblk, k,
                       preferred_element_type=jnp.float32) * scale
        qpos = start + jnp.arange(Q_BLOCK)
        causal = kpos[None, :] <= qpos[:, None]
        p = jax.nn.softmax(jnp.where(causal, s, -jnp.inf), axis=-1)
        a = p[:, :, 0] - lam * p[:, :, 1]
        return jnp.einsum('bhij,bjhe->bihe', a.astype(v.dtype), v)

    o = lax.map(block, (qb, jnp.arange(nb) * Q_BLOCK))
    o = jnp.moveaxis(o, 0, 1).reshape(B, S, H, DIFF_V_DIM)
    o = rms_norm(o, g_subln) * (1.0 - lambda_init)
    return o.reshape(B, S, H * DIFF_V_DIM)


def swiglu_clamped(u):
    x_glu = jnp.minimum(u[..., ::2], SWIGLU_LIMIT)
    x_lin = jnp.clip(u[..., 1::2], -SWIGLU_LIMIT, SWIGLU_LIMIT)
    return x_glu * jax.nn.sigmoid(SWIGLU_ALPHA * x_glu) * (x_lin + 1)


def moe_ffn(h, w_router, b_router, w1, b1, w2, b2):
    B, S, D = h.shape
    N = B * S
    hf = h.reshape(N, D)
    logits = (hf @ w_router + b_router).astype(jnp.float32)
    top_val, top_idx = lax.top_k(logits, TOP_K)
    gate = jax.nn.softmax(top_val, axis=-1)
    flat_e = top_idx.reshape(-1)
    flat_tok = jnp.repeat(jnp.arange(N, dtype=jnp.int32), TOP_K)
    flat_w = gate.reshape(-1)
    order = jnp.argsort(flat_e)
    e_sorted, tok_sorted, w_sorted = flat_e[order], flat_tok[order], flat_w[order]
    counts = jnp.bincount(flat_e, length=N_EXPERTS)
    padded = (counts + EXPERT_BLOCK - 1) // EXPERT_BLOCK * EXPERT_BLOCK
    starts = jnp.cumsum(counts) - counts
    pends = jnp.cumsum(padded)
    pstarts = pends - padded
    dest = pstarts[e_sorted] + (jnp.arange(N * TOP_K) - starts[e_sorted])
    n_rows = (N * TOP_K + N_EXPERTS * (EXPERT_BLOCK - 1) + EXPERT_BLOCK - 1) // EXPERT_BLOCK * EXPERT_BLOCK
    n_blocks = n_rows // EXPERT_BLOCK
    row_tok = jnp.full((n_rows,), N, jnp.int32).at[dest].set(tok_sorted)
    row_w = jnp.zeros((n_rows,), jnp.float32).at[dest].set(w_sorted)
    block_e = jnp.clip(jnp.searchsorted(pends, jnp.arange(n_blocks) * EXPERT_BLOCK, side='right'),
                       0, N_EXPERTS - 1)
    x_rows = jnp.concatenate([hf, jnp.zeros((1, D), hf.dtype)], axis=0)[row_tok]
    x_rows = x_rows.reshape(n_blocks, EXPERT_BLOCK, D)

    def expert_block(args):
        xb, e = args
        u = xb @ w1[e] + b1[e]
        return swiglu_clamped(u) @ w2[e] + b2[e]

    y = lax.map(expert_block, (x_rows, block_e)).reshape(n_rows, D)
    y = y * row_w[:, None].astype(y.dtype)
    out = jax.ops.segment_sum(y, row_tok, num_segments=N + 1)[:N]
    return out.reshape(B, S, D)


def setup_inputs(seed: int = 0) -> dict:
    key = jax.random.key(seed)
    ks = jax.random.split(key, 26)
    nrm = lambda k, shape: jax.random.normal(k, shape, jnp.float32)
    L, D = DEPTH, D_MODEL
    offsets = jax.random.randint(ks[2], (BATCH, 1), 0, 4096, dtype=jnp.int32)
    positions = offsets + jnp.arange(SEQ, dtype=jnp.int32)[None, :]
    return {
        "x": nrm(ks[0], (BATCH, SEQ, D)),
        "c": nrm(ks[1], (BATCH, D)),
        "positions": positions,
        "w_ada": nrm(ks[3], (L, D, N_MOD * D)) * (0.5 * D ** -0.5),
        "b_ada": nrm(ks[4], (L, N_MOD * D)) * 0.02,
        "g_mix": 1.0 + 0.02 * nrm(ks[5], (L, D)),
        "w_in": nrm(ks[6], (L, D, IN_COLS)) * D ** -0.5,
        "b_in": nrm(ks[7], (L, IN_COLS)) * 0.02,
        "attn_sinks": nrm(ks[8], (L, SWA_Q_HEADS)) * 0.5,
        "lambda_q1": nrm(ks[9], (L, HEAD_DIM)) * 0.1,
        "lambda_k1": nrm(ks[10], (L, HEAD_DIM)) * 0.1,
        "lambda_q2": nrm(ks[11], (L, HEAD_DIM)) * 0.1,
        "lambda_k2": nrm(ks[12], (L, HEAD_DIM)) * 0.1,
        "g_subln": 1.0 + 0.02 * nrm(ks[13], (L, DIFF_V_DIM)),
        "w_out": nrm(ks[14], (L, MIX_WIDTH, D)) * MIX_WIDTH ** -0.5,
        "b_out": nrm(ks[15], (L, D)) * 0.02,
        "g_ffn": 1.0 + 0.02 * nrm(ks[16], (L, D)),
        "w_router": nrm(ks[17], (L, D, N_EXPERTS)) * D ** -0.5,
        "b_router": nrm(ks[18], (L, N_EXPERTS)) * 0.01,
        "w1": nrm(ks[19], (L, N_EXPERTS, D, 2 * D_FF)) * D ** -0.5,
        "b1": nrm(ks[20], (L, N_EXPERTS, 2 * D_FF)) * 0.02,
        "w2": nrm(ks[21], (L, N_EXPERTS, D_FF, D)) * D_FF ** -0.5,
        "b2": nrm(ks[22], (L, N_EXPERTS, D)) * 0.02,
        "g_final": 1.0 + 0.02 * nrm(ks[23], (D,)),
    }


def reference(x, c, positions, w_ada, b_ada, g_mix, w_in, b_in, attn_sinks,
              lambda_q1, lambda_k1, lambda_q2, lambda_k2, g_subln, w_out, b_out,
              g_ffn, w_router, b_router, w1, b1, w2, b2, g_final):
    B, S, D = x.shape
    cos, sin = rope_tables(positions)
    cond = jax.nn.silu(c)
    for layer in range(DEPTH):
        mod = cond @ w_ada[layer] + b_ada[layer]
        sh1, sc1, gt1, sh2, sc2, gt2 = jnp.split(mod, N_MOD, axis=-1)

        h = modulate(rms_norm(x, g_mix[layer]), sh1, sc1)
        proj = h @ w_in[layer] + b_in[layer]
        qa, ka, va, qd, kd, vd = jnp.split(proj, IN_SPLITS, axis=-1)
        qa = apply_rope(qa.reshape(B, S, SWA_Q_HEADS, HEAD_DIM), cos, sin)
        ka = apply_rope(ka.reshape(B, S, SWA_KV_HEADS, HEAD_DIM), cos, sin)
        va = va.reshape(B, S, SWA_KV_HEADS, HEAD_DIM)
        out_a = sliding_window_sink_attention(qa, ka, va, attn_sinks[layer])
        qd = apply_rope(qd.reshape(B, S, DIFF_HEADS * 2, HEAD_DIM), cos, sin)
        kd = apply_rope(kd.reshape(B, S, DIFF_HEADS * 2, HEAD_DIM), cos, sin)
        qd = qd.reshape(B, S, DIFF_HEADS, 2, HEAD_DIM)
        kd = kd.reshape(B, S, DIFF_HEADS, 2, HEAD_DIM)
        vd = vd.reshape(B, S, DIFF_HEADS, DIFF_V_DIM)
        lambda_init = 0.8 - 0.6 * math.exp(-0.3 * layer)
        lam = (jnp.exp(jnp.sum(lambda_q1[layer].astype(jnp.float32) * lambda_k1[layer].astype(jnp.float32)))
               - jnp.exp(jnp.sum(lambda_q2[layer].astype(jnp.float32) * lambda_k2[layer].astype(jnp.float32)))
               + lambda_init)
        out_b = differential_attention(qd, kd, vd, lam, g_subln[layer], lambda_init)
        mixed = jnp.concatenate([out_a, out_b], axis=-1)
        x = x + gt1[:, None, :] * (mixed @ w_out[layer] + b_out[layer])

        h = modulate(rms_norm(x, g_ffn[layer]), sh2, sc2)
        x = x + gt2[:, None, :] * moe_ffn(h, w_router[layer], b_router[layer],
                                          w1[layer], b1[layer], w2[layer], b2[layer])
    return rms_norm(x, g_final)
```

```python
import functools
import math

import jax
import jax.numpy as jnp
from jax import lax
from jax.experimental import pallas as pl
from jax.experimental.pallas import tpu as pltpu

HEAD_DIM = 64
SWA_Q_HEADS = 8
SWA_KV_HEADS = 2
SWA_GROUP = SWA_Q_HEADS // SWA_KV_HEADS
WINDOW = 128
DIFF_HEADS = 4
DIFF_V_DIM = 2 * HEAD_DIM
ROPE_THETA = 10000.0
N_EXPERTS = 32
TOP_K = 4
SWIGLU_ALPHA = 1.702
SWIGLU_LIMIT = 7.0
EPS = 1e-5
N_MOD = 6

LANES = 128
F32 = jnp.float32
BF16 = jnp.bfloat16
NEG_INF = float("-inf")

TM_PROJ = 512
TQ_SWA = 512
TQ_DIFF = 512
TM_OUT = 256
TM_ROWS = 256
BLOCK_ROWS = 256
VMEM_LIMIT = 48 * 1024 * 1024


def _cparams(sem, vmem=VMEM_LIMIT):
    return pltpu.CompilerParams(dimension_semantics=sem, vmem_limit_bytes=vmem)


def _adaln_kernel(ct_ref, w_ref, b_ref, o_ref):
    c = ct_ref[...]
    cond = c * jax.nn.sigmoid(c)
    w = w_ref[...]
    rows = [jnp.sum(w * cond[:, b:b + 1], axis=0, keepdims=True) for b in range(c.shape[1])]
    o_ref[...] = jnp.concatenate(rows, axis=0) + b_ref[...]


def _adaln(c, w_ada, b_ada):
    B, D = c.shape
    n_out = w_ada.shape[1]
    tn = 1024
    return pl.pallas_call(
        _adaln_kernel,
        out_shape=jax.ShapeDtypeStruct((B, n_out), F32),
        grid=(n_out // tn,),
        in_specs=[pl.BlockSpec((D, B), lambda j: (0, 0)),
                  pl.BlockSpec((D, tn), lambda j: (0, j)),
                  pl.BlockSpec((1, tn), lambda j: (0, j))],
        out_specs=pl.BlockSpec((B, tn), lambda j: (0, j)),
        compiler_params=_cparams(("arbitrary",)),
        name="adaln",
    )(c.T, w_ada, b_ada.reshape(1, n_out))


def _rms(x):
    return x * lax.rsqrt(jnp.mean(x * x, axis=-1, keepdims=True) + EPS)


def _inproj_kernel(x_ref, pos_ref, inv_ref, mod_ref, g_ref, w_ref, b_ref,
                   qa_ref, ka_ref, va_ref, qd_ref, kd_ref, vd_ref):
    x = x_ref[...]
    sh = mod_ref[0, 0:1, :]
    sc = mod_ref[0, 1:2, :]
    h = _rms(x) * g_ref[...] * (1.0 + sc) + sh
    proj = jnp.dot(h.astype(BF16), w_ref[...], preferred_element_type=F32) + b_ref[...]

    ang = pos_ref[...].astype(F32) * inv_ref[...]
    lane = lax.broadcasted_iota(jnp.int32, (1, LANES), 1)
    first_half = (lane & (HEAD_DIM - 1)) < (HEAD_DIM // 2)
    cos = jnp.cos(ang)
    sin = jnp.sin(ang)
    sin_signed = jnp.where(first_half, -sin, sin)

    def rope(t):
        partner = jnp.where(first_half,
                            pltpu.roll(t, LANES - HEAD_DIM // 2, axis=1),
                            pltpu.roll(t, HEAD_DIM // 2, axis=1))
        return t * cos + partner * sin_signed

    def emit(out_ref, col0, width, rotary, scale):
        for j in range(width // LANES):
            t = proj[:, col0 + j * LANES: col0 + (j + 1) * LANES]
            if rotary:
                t = rope(t)
            if scale != 1.0:
                t = t * scale
            out_ref[:, j * LANES:(j + 1) * LANES] = t.astype(out_ref.dtype)

    qk_scale = 1.0 / math.sqrt(HEAD_DIM)
    col = 0
    for out_ref, rotary, scale in ((qa_ref, True, qk_scale), (ka_ref, True, 1.0), (va_ref, False, 1.0),
                                   (qd_ref, True, qk_scale), (kd_ref, True, 1.0), (vd_ref, False, 1.0)):
        width = out_ref.shape[1]
        emit(out_ref, col, width, rotary, scale)
        col += width


def _inproj(x2, pos2, inv_lane, mod3, g_mix, w_ext, b_ext, S, widths):
    N, D = x2.shape
    tm = TM_PROJ
    C = w_ext.shape[1]
    tiles_per_seq = S // tm
    row = lambda i: (i, 0)
    return pl.pallas_call(
        _inproj_kernel,
        out_shape=[jax.ShapeDtypeStruct((N, w), BF16) for w in widths],
        grid=(N // tm,),
        in_specs=[pl.BlockSpec((tm, D), row),
                  pl.BlockSpec((tm, 1), row),
                  pl.BlockSpec((1, LANES), lambda i: (0, 0)),
                  pl.BlockSpec((1, N_MOD, D), lambda i: (i // tiles_per_seq, 0, 0)),
                  pl.BlockSpec((1, D), lambda i: (0, 0)),
                  pl.BlockSpec((D, C), lambda i: (0, 0)),
                  pl.BlockSpec((1, C), lambda i: (0, 0))],
        out_specs=[pl.BlockSpec((tm, w), row) for w in widths],
        compiler_params=_cparams(("arbitrary",)),
        name="inproj",
    )(x2, pos2, inv_lane, mod3, g_mix, w_ext, b_ext)


def _swa_kernel(sink_ref, q_ref, kc_ref, kp_ref, vc_ref, vp_ref, o_ref):
    i = pl.program_id(1)
    tq = q_ref.shape[0]
    lane = lax.broadcasted_iota(jnp.int32, (1, LANES), 1)
    lo = lane < HEAD_DIM
    qi = lax.broadcasted_iota(jnp.int32, (WINDOW, 2 * WINDOW), 0) + WINDOW
    kj = lax.broadcasted_iota(jnp.int32, (WINDOW, 2 * WINDOW), 1)
    band = (qi - kj >= 0) & (qi - kj < WINDOW)
    dn = (((1,), (1,)), ((), ()))
    for c in range(tq // WINDOW):
        if c == 0:
            kcat = jnp.concatenate([kp_ref[...], kc_ref[0:WINDOW, :]], axis=0)
            vcat = jnp.concatenate([vp_ref[...], vc_ref[0:WINDOW, :]], axis=0)
            mask = band & (kj >= jnp.where(i > 0, 0, WINDOW))
        else:
            kcat = kc_ref[(c - 1) * WINDOW:(c + 1) * WINDOW, :]
            vcat = vc_ref[(c - 1) * WINDOW:(c + 1) * WINDOW, :]
            mask = band
        for j in range(SWA_KV_HEADS):
            kj2 = kcat[:, j * LANES:(j + 1) * LANES]
            vj2 = vcat[:, j * LANES:(j + 1) * LANES]
            zero = jnp.zeros_like(kj2)
            k_halves = (jnp.where(lo, kj2, zero), jnp.where(lo, zero, kj2))
            v_halves = (jnp.where(lo, vj2, zero), jnp.where(lo, zero, vj2))
            for p in range(SWA_GROUP // 2):
                g = j * (SWA_GROUP // 2) + p
                q = q_ref[c * WINDOW:(c + 1) * WINDOW, g * LANES:(g + 1) * LANES]
                out = jnp.zeros((WINDOW, LANES), F32)
                for half in range(2):
                    sink = sink_ref[2 * g + half]
                    s = lax.dot_general(q, k_halves[half], dn, preferred_element_type=F32)
                    s = jnp.where(mask, s, NEG_INF)
                    m = jnp.maximum(jnp.max(s, axis=1, keepdims=True), sink)
                    e = jnp.exp(s - m)
                    denom = jnp.sum(e, axis=1, keepdims=True) + jnp.exp(sink - m)
                    pv = jnp.dot(e.astype(BF16), v_halves[half], preferred_element_type=F32)
                    out = out + pv / denom
                o_ref[c * WINDOW:(c + 1) * WINDOW, g * LANES:(g + 1) * LANES] = out.astype(o_ref.dtype)


def _swa(sinks, qa, ka2, va2, B, S):
    N = qa.shape[0]
    tq = TQ_SWA
    nq = S // tq
    wpt = tq // WINDOW
    wps = S // WINDOW
    cur = lambda b, i: (b * nq + i, 0)
    prev = lambda b, i: (b * wps + jnp.maximum(i * wpt - 1, 0), 0)
    return pl.pallas_call(
        _swa_kernel,
        out_shape=jax.ShapeDtypeStruct((N, qa.shape[1]), BF16),
        grid=(B, nq),
        in_specs=[pl.BlockSpec(memory_space=pltpu.SMEM),
                  pl.BlockSpec((tq, qa.shape[1]), cur),
                  pl.BlockSpec((tq, ka2.shape[1]), cur),
                  pl.BlockSpec((WINDOW, ka2.shape[1]), prev),
                  pl.BlockSpec((tq, va2.shape[1]), cur),
                  pl.BlockSpec((WINDOW, va2.shape[1]), prev)],
        out_specs=pl.BlockSpec((tq, qa.shape[1]), cur),
        compiler_params=_cparams(("arbitrary", "arbitrary")),
        name="swa",
    )(sinks, qa, ka2, ka2, va2, va2)


def _diff_kernel(lam_ref, g_ref, q_ref, k_ref, v_ref, o_ref, m_ref, l_ref, acc_ref, *, lambda_init):
    i = pl.program_id(2)
    tq = q_ref.shape[0]
    tk = tq
    q = q_ref[...]
    lane = lax.broadcasted_iota(jnp.int32, (1, LANES), 1)
    lo = lane < HEAD_DIM
    dn = (((1,), (1,)), ((), ()))
    m_ref[...] = jnp.full(m_ref.shape, NEG_INF, F32)
    l_ref[...] = jnp.zeros(l_ref.shape, F32)
    acc_ref[...] = jnp.zeros(acc_ref.shape, F32)

    def step(j, diagonal):
        start = pl.multiple_of(j * tk, tk)
        k = k_ref[pl.ds(start, tk), :]
        v = v_ref[pl.ds(start, tk), :]
        zero = jnp.zeros_like(k)
        for mp in range(2):
            km = jnp.where(lo, k, zero) if mp == 0 else jnp.where(lo, zero, k)
            s = lax.dot_general(q, km, dn, preferred_element_type=F32)
            if diagonal:
                r = lax.broadcasted_iota(jnp.int32, (tq, tk), 0)
                cidx = lax.broadcasted_iota(jnp.int32, (tq, tk), 1)
                s = jnp.where(cidx <= r, s, NEG_INF)
            m_prev = m_ref[mp]
            m_new = jnp.maximum(m_prev, jnp.max(s, axis=1, keepdims=True))
            alpha = jnp.exp(m_prev - m_new)
            p = jnp.exp(s - m_new)
            l_ref[mp] = alpha * l_ref[mp] + jnp.sum(p, axis=1, keepdims=True)
            acc_ref[mp] = alpha * acc_ref[mp] + jnp.dot(p.astype(BF16), v, preferred_element_type=F32)
            m_ref[mp] = m_new

    def body(j, carry):
        step(j, False)
        return carry

    lax.fori_loop(0, i, body, 0)
    step(i, True)

    lq1, lk1, lq2, lk2 = (lam_ref[r:r + 1, :] for r in range(4))
    lam = (jnp.exp(jnp.sum(lq1 * lk1, axis=1, keepdims=True))
           - jnp.exp(jnp.sum(lq2 * lk2, axis=1, keepdims=True)) + lambda_init)
    o = acc_ref[0] / l_ref[0] - lam * (acc_ref[1] / l_ref[1])
    o = _rms(o) * g_ref[...] * (1.0 - lambda_init)
    o_ref[...] = o.astype(o_ref.dtype)


def _diffattn(lam_vecs, g_subln, qd, kd, vd, B, S, lambda_init):
    N, C = qd.shape
    tq = TQ_DIFF
    nq = S // tq
    return pl.pallas_call(
        functools.partial(_diff_kernel, lambda_init=lambda_init),
        out_shape=jax.ShapeDtypeStruct((N, C), BF16),
        grid=(B, DIFF_HEADS, nq),
        in_specs=[pl.BlockSpec((4, HEAD_DIM), lambda b, h, i: (0, 0)),
                  pl.BlockSpec((1, DIFF_V_DIM), lambda b, h, i: (0, 0)),
                  pl.BlockSpec((tq, LANES), lambda b, h, i: (b * nq + i, h)),
                  pl.BlockSpec((S, LANES), lambda b, h, i: (b, h)),
                  pl.BlockSpec((S, LANES), lambda b, h, i: (b, h))],
        out_specs=pl.BlockSpec((tq, LANES), lambda b, h, i: (b * nq + i, h)),
        scratch_shapes=[pltpu.VMEM((2, tq, 1), F32), pltpu.VMEM((2, tq, 1), F32),
                        pltpu.VMEM((2, tq, DIFF_V_DIM), F32)],
        compiler_params=_cparams(("arbitrary", "arbitrary", "arbitrary")),
        name="diffattn",
    )(lam_vecs, g_subln, qd, kd, vd)


def _outproj_kernel(oa_ref, ob_ref, x_ref, mod_ref, wo_ref, bo_ref, g_ref, wr_ref, br_ref,
                    x1_ref, h2_ref, idx_ref, gate_ref, rank_ref, cnt_ref, carry_ref):
    i = pl.program_id(0)
    tm = x_ref.shape[0]
    half = oa_ref.shape[1]

    @pl.when(i == 0)
    def _():
        carry_ref[...] = jnp.zeros(carry_ref.shape, F32)

    gt1 = mod_ref[0, 2:3, :]
    sh2 = mod_ref[0, 3:4, :]
    sc2 = mod_ref[0, 4:5, :]
    mixed = (jnp.dot(oa_ref[...], wo_ref[0:half, :], preferred_element_type=F32)
             + jnp.dot(ob_ref[...], wo_ref[half:, :], preferred_element_type=F32) + bo_ref[...])
    x1 = x_ref[...] + gt1 * mixed
    x1_ref[...] = x1
    h2 = _rms(x1) * g_ref[...] * (1.0 + sc2) + sh2
    h2_ref[...] = h2

    dn = (((1,), (1,)), ((), ()))
    h_hi = h2.astype(BF16)
    h_lo = (h2 - h_hi.astype(F32)).astype(BF16)
    w = wr_ref[...]
    w_hi = w.astype(BF16)
    w_lo = (w - w_hi.astype(F32)).astype(BF16)
    logits = (lax.dot_general(w_hi, h_hi, dn, preferred_element_type=F32)
              + lax.dot_general(w_hi, h_lo, dn, preferred_element_type=F32)
              + lax.dot_general(w_lo, h_hi, dn, preferred_element_type=F32)
              + br_ref[...])

    eidx = lax.broadcasted_iota(jnp.int32, logits.shape, 0)
    vals = logits
    onehots, top_vals, top_idx = [], [], []
    for _k in range(TOP_K):
        mx = jnp.max(vals, axis=0, keepdims=True)
        sel = jnp.min(jnp.where(vals == mx, eidx, N_EXPERTS), axis=0, keepdims=True)
        oh = eidx == sel
        onehots.append(oh)
        top_vals.append(mx)
        top_idx.append(sel)
        vals = jnp.where(oh, NEG_INF, vals)
    exps = [jnp.exp(v - top_vals[0]) for v in top_vals]
    denom = exps[0] + exps[1] + exps[2] + exps[3]
    gate_ref[...] = jnp.concatenate([e / denom for e in exps], axis=0)
    idx_ref[...] = jnp.concatenate(top_idx, axis=0)

    member = (onehots[0] | onehots[1] | onehots[2] | onehots[3])
    member_f = member.astype(F32)
    t_src = lax.broadcasted_iota(jnp.int32, (tm, tm), 0)
    t_dst = lax.broadcasted_iota(jnp.int32, (tm, tm), 1)
    before = (t_src < t_dst).astype(BF16)
    prefix = jnp.dot(member.astype(BF16), before, preferred_element_type=F32) + carry_ref[...]
    ranks = [jnp.sum(jnp.where(oh, prefix, 0.0), axis=0, keepdims=True) for oh in onehots]
    rank_ref[...] = jnp.concatenate(ranks, axis=0).astype(jnp.int32)
    carry_ref[...] = carry_ref[...] + jnp.sum(member_f, axis=1, keepdims=True)
    cnt_ref[...] = jnp.broadcast_to(carry_ref[...], cnt_ref.shape)


def _outproj(out_a, out_b, x2, mod3, w_out, b_out, g_ffn, wr_t, br_col, S):
    N, D = x2.shape
    tm = TM_OUT
    tiles_per_seq = S // tm
    row = lambda i: (i, 0)
    colb = lambda i: (0, i)
    const = lambda i: (0, 0)
    return pl.pallas_call(
        _outproj_kernel,
        out_shape=[jax.ShapeDtypeStruct((N, D), F32), jax.ShapeDtypeStruct((N, D), F32),
                   jax.ShapeDtypeStruct((TOP_K, N), jnp.int32), jax.ShapeDtypeStruct((TOP_K, N), F32),
                   jax.ShapeDtypeStruct((TOP_K, N), jnp.int32), jax.ShapeDtypeStruct((N_EXPERTS, LANES), F32)],
        grid=(N // tm,),
        in_specs=[pl.BlockSpec((tm, out_a.shape[1]), row),
                  pl.BlockSpec((tm, out_b.shape[1]), row),
                  pl.BlockSpec((tm, D), row),
                  pl.BlockSpec((1, N_MOD, D), lambda i: (i // tiles_per_seq, 0, 0)),
                  pl.BlockSpec(w_out.shape, const),
                  pl.BlockSpec((1, D), const),
                  pl.BlockSpec((1, D), const),
                  pl.BlockSpec(wr_t.shape, const),
                  pl.BlockSpec((N_EXPERTS, 1), const)],
        out_specs=[pl.BlockSpec((tm, D), row), pl.BlockSpec((tm, D), row),
                   pl.BlockSpec((TOP_K, tm), colb), pl.BlockSpec((TOP_K, tm), colb),
                   pl.BlockSpec((TOP_K, tm), colb), pl.BlockSpec((N_EXPERTS, LANES), const)],
        scratch_shapes=[pltpu.VMEM((N_EXPERTS, 1), F32)],
        compiler_params=_cparams(("arbitrary",)),
        name="outproj_router",
    )(out_a, out_b, x2, mod3, w_out, b_out, g_ffn, wr_t, br_col)


def _dispatch_kernel(dest_ref, pad_ref, npad_ref, h_ref, rows_ref, zero_ref, sem, *, n_tokens):
    i = pl.program_id(0)
    tm = h_ref.shape[0]
    base = i * tm

    def issue(t, carry):
        for k in range(TOP_K):
            d = dest_ref[k * n_tokens + base + t]
            pltpu.make_async_copy(h_ref.at[pl.ds(t, 1)], rows_ref.at[pl.ds(d, 1)], sem).start()
        return carry

    lax.fori_loop(0, tm, issue, 0)

    for k in range(TOP_K):
        pltpu.make_async_copy(h_ref, rows_ref.at[pl.ds(0, tm)], sem).wait()

    @pl.when(i == pl.num_programs(0) - 1)
    def _():
        zero_ref[...] = jnp.zeros(zero_ref.shape, zero_ref.dtype)
        n_pad = npad_ref[0]
        n_used = npad_ref[1]
        bm = zero_ref.shape[0]
        n_blocks = rows_ref.shape[0] // bm
        zero_row = zero_ref.at[pl.ds(0, 1)]

        def issue_pad(j, carry):
            pltpu.make_async_copy(zero_row, rows_ref.at[pl.ds(pad_ref[j], 1)], sem).start()
            return carry

        def drain_pad(j, carry):
            pltpu.make_async_copy(zero_row, rows_ref.at[pl.ds(0, 1)], sem).wait()
            return carry

        def issue_block(j, carry):
            pltpu.make_async_copy(zero_ref, rows_ref.at[pl.ds(pl.multiple_of(j * bm, bm), bm)], sem).start()
            return carry

        def drain_block(j, carry):
            pltpu.make_async_copy(zero_ref, rows_ref.at[pl.ds(0, bm)], sem).wait()
            return carry

        lax.fori_loop(0, n_pad, issue_pad, 0)
        lax.fori_loop(n_used, n_blocks, issue_block, 0)
        lax.fori_loop(0, n_pad, drain_pad, 0)
        lax.fori_loop(n_used, n_blocks, drain_block, 0)


def _dispatch(dest_flat, pad_rows, n_pad, h2, n_rows):
    N, D = h2.shape
    tm = TM_ROWS
    return pl.pallas_call(
        functools.partial(_dispatch_kernel, n_tokens=N),
        out_shape=jax.ShapeDtypeStruct((n_rows, D), h2.dtype),
        grid_spec=pltpu.PrefetchScalarGridSpec(
            num_scalar_prefetch=3,
            grid=(N // tm,),
            in_specs=[pl.BlockSpec((tm, D), lambda i, *_: (i, 0))],
            out_specs=pl.BlockSpec(memory_space=pl.ANY),
            scratch_shapes=[pltpu.VMEM((BLOCK_ROWS, D), h2.dtype), pltpu.SemaphoreType.DMA]),
        compiler_params=_cparams(("arbitrary",)),
        name="dispatch",
    )(dest_flat, pad_rows, n_pad, h2)


def _experts_kernel(be_ref, nused_ref, x_ref, w1g_ref, w1l_ref, b1g_ref, b1l_ref, w2_ref, b2_ref, y_ref):
    i = pl.program_id(0)

    @pl.when(i < nused_ref[0])
    def _():
        x = x_ref[...].astype(BF16)
        ug = jnp.dot(x, w1g_ref[0], preferred_element_type=F32) + b1g_ref[0]
        ul = jnp.dot(x, w1l_ref[0], preferred_element_type=F32) + b1l_ref[0]
        glu = jnp.minimum(ug, SWIGLU_LIMIT)
        lin = jnp.clip(ul, -SWIGLU_LIMIT, SWIGLU_LIMIT)
        act = glu * jax.nn.sigmoid(SWIGLU_ALPHA * glu) * (lin + 1.0)
        y_ref[...] = jnp.dot(act.astype(BF16), w2_ref[0], preferred_element_type=F32) + b2_ref[0]

    @pl.when(i >= nused_ref[0])
    def _():
        y_ref[...] = jnp.zeros(y_ref.shape, y_ref.dtype)


def _experts(block_e, n_used, x_rows, w1g, w1l, b1g, b1l, w2b, b2):
    n_rows, D = x_rows.shape
    E, _, Fh = w1g.shape
    bm = BLOCK_ROWS
    wsel = lambda i, be, nu: (be[i], 0, 0)
    return pl.pallas_call(
        _experts_kernel,
        out_shape=jax.ShapeDtypeStruct((n_rows, D), F32),
        grid_spec=pltpu.PrefetchScalarGridSpec(
            num_scalar_prefetch=2,
            grid=(n_rows // bm,),
            in_specs=[pl.BlockSpec((bm, D), lambda i, be, nu: (i, 0)),
                      pl.BlockSpec((1, D, Fh), wsel),
                      pl.BlockSpec((1, D, Fh), wsel),
                      pl.BlockSpec((1, 1, Fh), wsel),
                      pl.BlockSpec((1, 1, Fh), wsel),
                      pl.BlockSpec((1, Fh, D), wsel),
                      pl.BlockSpec((1, 1, D), wsel)],
            out_specs=pl.BlockSpec((bm, D), lambda i, be, nu: (i, 0))),
        compiler_params=_cparams(("arbitrary",)),
        name="experts",
    )(block_e, n_used, x_rows, w1g, w1l, b1g, b1l, w2b, b2)


def _combine_kernel(dest_ref, x1_ref, gate_ref, mod_ref, g_ref, y_ref, o_ref, buf_ref, sem, *, n_tokens, final_norm):
    i = pl.program_id(0)
    tm = x1_ref.shape[0]
    base = i * tm

    def issue(t, carry):
        for k in range(TOP_K):
            d = dest_ref[k * n_tokens + base + t]
            pltpu.make_async_copy(y_ref.at[pl.ds(d, 1)], buf_ref.at[k, pl.ds(t, 1)], sem).start()
        return carry

    lax.fori_loop(0, tm, issue, 0)

    for k in range(TOP_K):
        pltpu.make_async_copy(y_ref.at[pl.ds(0, tm)], buf_ref.at[k], sem).wait()

    gate = gate_ref[...]
    moe = gate[:, 0:1] * buf_ref[0]
    for k in range(1, TOP_K):
        moe = moe + gate[:, k:k + 1] * buf_ref[k]
    gt2 = mod_ref[0, 5:6, :]
    x2 = x1_ref[...] + gt2 * moe
    o_ref[...] = _rms(x2) * g_ref[...] if final_norm else x2


def _combine(dest_flat, x1, gate_t, mod3, g_final, y_rows, S, final_norm):
    N, D = x1.shape
    tm = TM_ROWS
    tiles_per_seq = S // tm
    return pl.pallas_call(
        functools.partial(_combine_kernel, n_tokens=N, final_norm=final_norm),
        out_shape=jax.ShapeDtypeStruct((N, D), F32),
        grid_spec=pltpu.PrefetchScalarGridSpec(
            num_scalar_prefetch=1,
            grid=(N // tm,),
            in_specs=[pl.BlockSpec((tm, D), lambda i, d: (i, 0)),
                      pl.BlockSpec((tm, TOP_K), lambda i, d: (i, 0)),
                      pl.BlockSpec((1, N_MOD, D), lambda i, d: (i // tiles_per_seq, 0, 0)),
                      pl.BlockSpec((1, D), lambda i, d: (0, 0)),
                      pl.BlockSpec(memory_space=pl.ANY)],
            out_specs=pl.BlockSpec((tm, D), lambda i, d: (i, 0)),
            scratch_shapes=[pltpu.VMEM((TOP_K, tm, D), F32), pltpu.SemaphoreType.DMA]),
        compiler_params=_cparams(("arbitrary",)),
        name="combine",
    )(dest_flat, x1, gate_t, mod3, g_final, y_rows)


def _routing_tables(counts, idx, rank, n_rows):
    bm = BLOCK_ROWS
    counts = counts.astype(jnp.int32)
    padded = (counts + bm - 1) // bm * bm
    pends = jnp.cumsum(padded)
    pstarts = pends - padded
    dest = pstarts[idx] + rank
    n_blocks = n_rows // bm
    block_e = jnp.clip(jnp.searchsorted(pends, jnp.arange(n_blocks, dtype=jnp.int32) * bm, side='right'),
                       0, N_EXPERTS - 1).astype(jnp.int32)
    n_used = (pends[-1] // bm).astype(jnp.int32).reshape(1)
    pad_counts = padded - counts
    pad_ends = jnp.cumsum(pad_counts)
    max_pad = N_EXPERTS * (bm - 1)
    j = jnp.arange(max_pad, dtype=jnp.int32)
    e_j = jnp.clip(jnp.searchsorted(pad_ends, j, side='right'), 0, N_EXPERTS - 1)
    pad_rows = pstarts[e_j] + counts[e_j] + (j - (pad_ends - pad_counts)[e_j])
    n_pad = pad_ends[-1].astype(jnp.int32)
    pad_rows = jnp.where(j < n_pad, pad_rows, 0).astype(jnp.int32)
    fill_counts = jnp.stack([n_pad, n_used[0]])
    return dest.reshape(-1).astype(jnp.int32), block_e, n_used, pad_rows, fill_counts


def _extended_in_weights(w_in, b_in):
    a_q = SWA_Q_HEADS * HEAD_DIM
    a_kv = SWA_KV_HEADS * HEAD_DIM
    b_w = DIFF_HEADS * DIFF_V_DIM
    spans = [(0, a_q)]
    for base in (a_q, a_q + a_kv):
        for j in range(SWA_KV_HEADS):
            spans += [(base + j * HEAD_DIM, base + (j + 1) * HEAD_DIM)] * 2
    spans.append((a_q + 2 * a_kv, a_q + 2 * a_kv + 3 * b_w))
    w_ext = jnp.concatenate([w_in[:, lo:hi] for lo, hi in spans], axis=1).astype(BF16)
    b_ext = jnp.concatenate([b_in[lo:hi] for lo, hi in spans]).reshape(1, -1)
    widths = (a_q, 2 * a_kv, 2 * a_kv, b_w, b_w, b_w)
    return w_ext, b_ext, widths


def kernel(x, c, positions, w_ada, b_ada, g_mix, w_in, b_in, attn_sinks, lambda_q1, lambda_k1, lambda_q2,
           lambda_k2, g_subln, w_out, b_out, g_ffn, w_router, b_router, w1, b1, w2, b2, g_final):
    B, S, D = x.shape
    N = B * S
    depth = w_ada.shape[0]
    n_rows = (N * TOP_K + N_EXPERTS * (BLOCK_ROWS - 1) + BLOCK_ROWS - 1) // BLOCK_ROWS * BLOCK_ROWS

    inv = 1.0 / (ROPE_THETA ** (jnp.arange(0, HEAD_DIM, 2, dtype=F32) / HEAD_DIM))
    inv_lane = jnp.tile(inv, LANES // (HEAD_DIM // 2)).reshape(1, LANES)
    pos2 = positions.reshape(N, 1)
    xcur = x.reshape(N, D)

    for layer in range(depth):
        last = layer == depth - 1
        lambda_init = 0.8 - 0.6 * math.exp(-0.3 * layer)
        mod3 = _adaln(c, w_ada[layer], b_ada[layer]).reshape(B, N_MOD, D)

        w_ext, b_ext, widths = _extended_in_weights(w_in[layer], b_in[layer])
        qa, ka2, va2, qd, kd, vd = _inproj(xcur, pos2, inv_lane, mod3, g_mix[layer].reshape(1, D),
                                           w_ext, b_ext, S, widths)
        out_a = _swa(attn_sinks[layer], qa, ka2, va2, B, S)
        lam_vecs = jnp.stack([lambda_q1[layer], lambda_k1[layer], lambda_q2[layer], lambda_k2[layer]])
        out_b = _diffattn(lam_vecs, g_subln[layer].reshape(1, DIFF_V_DIM), qd, kd, vd, B, S, lambda_init)

        x1, h2, idx, gate, rank, counts = _outproj(
            out_a, out_b, xcur, mod3, w_out[layer].astype(BF16), b_out[layer].reshape(1, D),
            g_ffn[layer].reshape(1, D), w_router[layer].T, b_router[layer].reshape(N_EXPERTS, 1), S)

        dest, block_e, n_used, pad_rows, n_pad = _routing_tables(counts[:, 0], idx, rank, n_rows)
        x_rows = _dispatch(dest, pad_rows, n_pad, h2, n_rows)

        w1l = w1[layer]
        y_rows = _experts(block_e, n_used, x_rows,
                          w1l[:, :, 0::2].astype(BF16), w1l[:, :, 1::2].astype(BF16),
                          b1[layer][:, None, 0::2], b1[layer][:, None, 1::2],
                          w2[layer].astype(BF16), b2[layer][:, None, :])

        xcur = _combine(dest, x1, gate.T, mod3, g_final.reshape(1, D), y_rows, S, final_norm=last)
    return xcur.reshape(B, S, D)
```

```python
import functools
import math

import jax
import jax.numpy as jnp
from jax import lax
from jax.experimental import pallas as pl
from jax.experimental.pallas import tpu as pltpu

HEAD_DIM = 64
SWA_Q_HEADS = 8
SWA_KV_HEADS = 2
SWA_GROUP = SWA_Q_HEADS // SWA_KV_HEADS
WINDOW = 128
DIFF_HEADS = 4
DIFF_V_DIM = 2 * HEAD_DIM
ROPE_THETA = 10000.0
N_EXPERTS = 32
TOP_K = 4
SWIGLU_ALPHA = 1.702
SWIGLU_LIMIT = 7.0
EPS = 1e-5
N_MOD = 6

LANES = 128
F32 = jnp.float32
BF16 = jnp.bfloat16
NEG_INF = float("-inf")

TM_PROJ = 512
TQ_SWA = 512
TQ_DIFF = 512
TM_OUT = 256
TM_ROWS = 256
BLOCK_ROWS = 256
VMEM_LIMIT = 48 * 1024 * 1024


def _cparams(sem, vmem=VMEM_LIMIT):
    return pltpu.CompilerParams(dimension_semantics=sem, vmem_limit_bytes=vmem)


def _adaln_kernel(ct_ref, w_ref, b_ref, o_ref):
    c = ct_ref[...]
    cond = c * jax.nn.sigmoid(c)
    w = w_ref[...]
    rows = [jnp.sum(w * cond[:, b:b + 1], axis=0, keepdims=True) for b in range(c.shape[1])]
    o_ref[...] = jnp.concatenate(rows, axis=0) + b_ref[...]


def _adaln(c, w_ada, b_ada):
    B, D = c.shape
    n_out = w_ada.shape[1]
    tn = 1024
    return pl.pallas_call(
        _adaln_kernel,
        out_shape=jax.ShapeDtypeStruct((B, n_out), F32),
        grid=(n_out // tn,),
        in_specs=[pl.BlockSpec((D, B), lambda j: (0, 0)),
                  pl.BlockSpec((D, tn), lambda j: (0, j)),
                  pl.BlockSpec((1, tn), lambda j: (0, j))],
        out_specs=pl.BlockSpec((B, tn), lambda j: (0, j)),
        compiler_params=_cparams(("arbitrary",)),
        name="adaln",
    )(c.T, w_ada, b_ada.reshape(1, n_out))


def _rms(x):
    return x * lax.rsqrt(jnp.mean(x * x, axis=-1, keepdims=True) + EPS)


def _inproj_kernel(x_ref, pos_ref, inv_ref, mod_ref, g_ref, w_ref, b_ref,
                   qa_ref, ka_ref, va_ref, qd_ref, kd_ref, vd_ref):
    x = x_ref[...]
    sh = mod_ref[0, 0:1, :]
    sc = mod_ref[0, 1:2, :]
    h = _rms(x) * g_ref[...] * (1.0 + sc) + sh
    proj = jnp.dot(h.astype(BF16), w_ref[...], preferred_element_type=F32) + b_ref[...]

    ang = pos_ref[...].astype(F32) * inv_ref[...]
    lane = lax.broadcasted_iota(jnp.int32, (1, LANES), 1)
    first_half = (lane & (HEAD_DIM - 1)) < (HEAD_DIM // 2)
    cos = jnp.cos(ang)
    sin = jnp.sin(ang)
    sin_signed = jnp.where(first_half, -sin, sin)

    def rope(t):
        partner = jnp.where(first_half,
                            pltpu.roll(t, LANES - HEAD_DIM // 2, axis=1),
                            pltpu.roll(t, HEAD_DIM // 2, axis=1))
        return t * cos + partner * sin_signed

    def emit(out_ref, col0, width, rotary, scale, transposed):
        for j in range(width // LANES):
            t = proj[:, col0 + j * LANES: col0 + (j + 1) * LANES]
            if rotary:
                t = rope(t)
            if scale != 1.0:
                t = t * scale
            if transposed:
                out_ref[0, 0, j * LANES:(j + 1) * LANES, :] = t.T.astype(out_ref.dtype)
            else:
                out_ref[:, j * LANES:(j + 1) * LANES] = t.astype(out_ref.dtype)

    qk_scale = 1.0 / math.sqrt(HEAD_DIM)
    col = 0
    for out_ref, rotary, scale, transposed in (
            (qa_ref, True, qk_scale, False), (ka_ref, True, 1.0, False), (va_ref, False, 1.0, False),
            (qd_ref, True, qk_scale, True), (kd_ref, True, 1.0, False), (vd_ref, False, 1.0, True)):
        width = out_ref.shape[2] if transposed else out_ref.shape[1]
        emit(out_ref, col, width, rotary, scale, transposed)
        col += width


def _inproj(x2, pos2, inv_lane, mod3, g_mix, w_ext, b_ext, S, widths):
    N, D = x2.shape
    tm = TM_PROJ
    C = w_ext.shape[1]
    tiles_per_seq = S // tm
    row = lambda i: (i, 0)
    transposed = (False, False, False, True, False, True)
    out_shape, out_specs = [], []
    for w, tr in zip(widths, transposed):
        if tr:
            out_shape.append(jax.ShapeDtypeStruct((N // S, tiles_per_seq, w, tm), BF16))
            out_specs.append(pl.BlockSpec((1, 1, w, tm), lambda i: (i // tiles_per_seq, i % tiles_per_seq, 0, 0)))
        else:
            out_shape.append(jax.ShapeDtypeStruct((N, w), BF16))
            out_specs.append(pl.BlockSpec((tm, w), row))
    return pl.pallas_call(
        _inproj_kernel,
        out_shape=out_shape,
        grid=(N // tm,),
        in_specs=[pl.BlockSpec((tm, D), row),
                  pl.BlockSpec((tm, 1), row),
                  pl.BlockSpec((1, LANES), lambda i: (0, 0)),
                  pl.BlockSpec((1, N_MOD, D), lambda i: (i // tiles_per_seq, 0, 0)),
                  pl.BlockSpec((1, D), lambda i: (0, 0)),
                  pl.BlockSpec((D, C), lambda i: (0, 0)),
                  pl.BlockSpec((1, C), lambda i: (0, 0))],
        out_specs=out_specs,
        compiler_params=_cparams(("arbitrary",)),
        name="inproj",
    )(x2, pos2, inv_lane, mod3, g_mix, w_ext, b_ext)


def _swa_kernel(sink_ref, q_ref, kc_ref, kp_ref, vc_ref, vp_ref, o_ref):
    i = pl.program_id(1)
    tq = q_ref.shape[0]
    lane = lax.broadcasted_iota(jnp.int32, (1, LANES), 1)
    lo = lane < HEAD_DIM
    qi = lax.broadcasted_iota(jnp.int32, (WINDOW, 2 * WINDOW), 0) + WINDOW
    kj = lax.broadcasted_iota(jnp.int32, (WINDOW, 2 * WINDOW), 1)
    band = (qi - kj >= 0) & (qi - kj < WINDOW)
    dn = (((1,), (1,)), ((), ()))
    for c in range(tq // WINDOW):
        if c == 0:
            kcat = jnp.concatenate([kp_ref[...], kc_ref[0:WINDOW, :]], axis=0)
            vcat = jnp.concatenate([vp_ref[...], vc_ref[0:WINDOW, :]], axis=0)
            mask = band & (kj >= jnp.where(i > 0, 0, WINDOW))
        else:
            kcat = kc_ref[(c - 1) * WINDOW:(c + 1) * WINDOW, :]
            vcat = vc_ref[(c - 1) * WINDOW:(c + 1) * WINDOW, :]
            mask = band
        for j in range(SWA_KV_HEADS):
            kj2 = kcat[:, j * LANES:(j + 1) * LANES]
            vj2 = vcat[:, j * LANES:(j + 1) * LANES]
            zero = jnp.zeros_like(kj2)
            k_halves = (jnp.where(lo, kj2, zero), jnp.where(lo, zero, kj2))
            v_halves = (jnp.where(lo, vj2, zero), jnp.where(lo, zero, vj2))
            for p in range(SWA_GROUP // 2):
                g = j * (SWA_GROUP // 2) + p
                q = q_ref[c * WINDOW:(c + 1) * WINDOW, g * LANES:(g + 1) * LANES]
                out = jnp.zeros((WINDOW, LANES), F32)
                for half in range(2):
                    sink = sink_ref[2 * g + half]
                    s = lax.dot_general(q, k_halves[half], dn, preferred_element_type=F32)
                    s = jnp.where(mask, s, NEG_INF)
                    m = jnp.maximum(jnp.max(s, axis=1, keepdims=True), sink)
                    e = jnp.exp(s - m)
                    denom = jnp.sum(e, axis=1, keepdims=True) + jnp.exp(sink - m)
                    pv = jnp.dot(e.astype(BF16), v_halves[half], preferred_element_type=F32)
                    out = out + pv / denom
                o_ref[c * WINDOW:(c + 1) * WINDOW, g * LANES:(g + 1) * LANES] = out.astype(o_ref.dtype)


def _swa(sinks, qa, ka2, va2, B, S):
    N = qa.shape[0]
    tq = TQ_SWA
    nq = S // tq
    wpt = tq // WINDOW
    wps = S // WINDOW
    cur = lambda b, i: (b * nq + i, 0)
    prev = lambda b, i: (b * wps + jnp.maximum(i * wpt - 1, 0), 0)
    return pl.pallas_call(
        _swa_kernel,
        out_shape=jax.ShapeDtypeStruct((N, qa.shape[1]), BF16),
        grid=(B, nq),
        in_specs=[pl.BlockSpec(memory_space=pltpu.SMEM),
                  pl.BlockSpec((tq, qa.shape[1]), cur),
                  pl.BlockSpec((tq, ka2.shape[1]), cur),
                  pl.BlockSpec((WINDOW, ka2.shape[1]), prev),
                  pl.BlockSpec((tq, va2.shape[1]), cur),
                  pl.BlockSpec((WINDOW, va2.shape[1]), prev)],
        out_specs=pl.BlockSpec((tq, qa.shape[1]), cur),
        compiler_params=_cparams(("arbitrary", "arbitrary")),
        name="swa",
    )(sinks, qa, ka2, ka2, va2, va2)


def _diff_kernel(lam_ref, g_ref, qt_ref, k_ref, vt_ref, o_ref, m_ref, l_ref, acc_ref, *, lambda_init):
    i = pl.program_id(2)
    tq = qt_ref.shape[3]
    tk = vt_ref.shape[3]
    qt = qt_ref[0, 0]
    lane = lax.broadcasted_iota(jnp.int32, (1, LANES), 1)
    lo = lane < HEAD_DIM
    m_ref[...] = jnp.full(m_ref.shape, NEG_INF, F32)
    l_ref[...] = jnp.zeros(l_ref.shape, F32)
    acc_ref[...] = jnp.zeros(acc_ref.shape, F32)

    def step(j, diagonal):
        start = pl.multiple_of(j * tk, tk)
        k = k_ref[pl.ds(start, tk), :]
        vt = vt_ref[0, j]
        zero = jnp.zeros_like(k)
        for mp in range(2):
            km = jnp.where(lo, k, zero) if mp == 0 else jnp.where(lo, zero, k)
            s = jnp.dot(km, qt, preferred_element_type=F32)
            if diagonal:
                kpos = lax.broadcasted_iota(jnp.int32, (tk, tq), 0)
                qpos = lax.broadcasted_iota(jnp.int32, (tk, tq), 1)
                s = jnp.where(kpos <= qpos, s, NEG_INF)
            m_prev = m_ref[mp]
            m_new = jnp.maximum(m_prev, jnp.max(s, axis=0, keepdims=True))
            alpha = jnp.exp(m_prev - m_new)
            p = jnp.exp(s - m_new)
            l_ref[mp] = alpha * l_ref[mp] + jnp.sum(p, axis=0, keepdims=True)
            acc_ref[mp] = alpha * acc_ref[mp] + jnp.dot(vt, p.astype(BF16), preferred_element_type=F32)
            m_ref[mp] = m_new

    def body(j, carry):
        step(j, False)
        return carry

    lax.fori_loop(0, i, body, 0)
    step(i, True)

    lq1, lk1, lq2, lk2 = (lam_ref[r:r + 1, :] for r in range(4))
    lam = (jnp.exp(jnp.sum(lq1 * lk1, axis=1, keepdims=True))
           - jnp.exp(jnp.sum(lq2 * lk2, axis=1, keepdims=True)) + lambda_init)
    ot = acc_ref[0] / l_ref[0] - lam * (acc_ref[1] / l_ref[1])
    ot = ot * lax.rsqrt(jnp.mean(ot * ot, axis=0, keepdims=True) + EPS)
    o_ref[...] = (ot.T * g_ref[...] * (1.0 - lambda_init)).astype(o_ref.dtype)


def _diffattn(lam_vecs, g_subln, qdt, kd, vdt, B, S, lambda_init):
    N, C = kd.shape
    tq = TQ_DIFF
    nq = S // tq
    return pl.pallas_call(
        functools.partial(_diff_kernel, lambda_init=lambda_init),
        out_shape=jax.ShapeDtypeStruct((N, C), BF16),
        grid=(B, DIFF_HEADS, nq),
        in_specs=[pl.BlockSpec((4, HEAD_DIM), lambda b, h, i: (0, 0)),
                  pl.BlockSpec((1, DIFF_V_DIM), lambda b, h, i: (0, 0)),
                  pl.BlockSpec((1, 1, LANES, tq), lambda b, h, i: (b, i, h, 0)),
                  pl.BlockSpec((S, LANES), lambda b, h, i: (b, h)),
                  pl.BlockSpec((1, nq, LANES, tq), lambda b, h, i: (b, 0, h, 0))],
        out_specs=pl.BlockSpec((tq, LANES), lambda b, h, i: (b * nq + i, h)),
        scratch_shapes=[pltpu.VMEM((2, 1, tq), F32), pltpu.VMEM((2, 1, tq), F32),
                        pltpu.VMEM((2, DIFF_V_DIM, tq), F32)],
        compiler_params=_cparams(("arbitrary", "arbitrary", "arbitrary")),
        name="diffattn",
    )(lam_vecs, g_subln, qdt, kd, vdt)


def _outproj_kernel(oa_ref, ob_ref, x_ref, mod_ref, wo_ref, bo_ref, g_ref, wr_ref, br_ref,
                    x1_ref, h2_ref, idx_ref, gate_ref, rank_ref, cnt_ref, carry_ref):
    i = pl.program_id(0)
    tm = x_ref.shape[0]
    half = oa_ref.shape[1]

    @pl.when(i == 0)
    def _():
        carry_ref[...] = jnp.zeros(carry_ref.shape, F32)

    gt1 = mod_ref[0, 2:3, :]
    sh2 = mod_ref[0, 3:4, :]
    sc2 = mod_ref[0, 4:5, :]
    mixed = (jnp.dot(oa_ref[...], wo_ref[0:half, :], preferred_element_type=F32)
             + jnp.dot(ob_ref[...], wo_ref[half:, :], preferred_element_type=F32) + bo_ref[...])
    x1 = x_ref[...] + gt1 * mixed
    x1_ref[...] = x1
    h2 = _rms(x1) * g_ref[...] * (1.0 + sc2) + sh2
    h2_ref[...] = h2

    dn = (((1,), (1,)), ((), ()))
    h_hi = h2.astype(BF16)
    h_lo = (h2 - h_hi.astype(F32)).astype(BF16)
    w = wr_ref[...]
    w_hi = w.astype(BF16)
    w_lo = (w - w_hi.astype(F32)).astype(BF16)
    logits = (lax.dot_general(w_hi, h_hi, dn, preferred_element_type=F32)
              + lax.dot_general(w_hi, h_lo, dn, preferred_element_type=F32)
              + lax.dot_general(w_lo, h_hi, dn, preferred_element_type=F32)
              + br_ref[...])

    eidx = lax.broadcasted_iota(jnp.int32, logits.shape, 0)
    vals = logits
    onehots, top_vals, top_idx = [], [], []
    for _k in range(TOP_K):
        mx = jnp.max(vals, axis=0, keepdims=True)
        sel = jnp.min(jnp.where(vals == mx, eidx, N_EXPERTS), axis=0, keepdims=True)
        oh = eidx == sel
        onehots.append(oh)
        top_vals.append(mx)
        top_idx.append(sel)
        vals = jnp.where(oh, NEG_INF, vals)
    exps = [jnp.exp(v - top_vals[0]) for v in top_vals]
    denom = exps[0] + exps[1] + exps[2] + exps[3]
    gate_ref[...] = jnp.concatenate([e / denom for e in exps], axis=0)
    idx_ref[...] = jnp.concatenate(top_idx, axis=0)

    member = (onehots[0] | onehots[1] | onehots[2] | onehots[3])
    member_f = member.astype(F32)
    t_src = lax.broadcasted_iota(jnp.int32, (tm, tm), 0)
    t_dst = lax.broadcasted_iota(jnp.int32, (tm, tm), 1)
    before = (t_src < t_dst).astype(BF16)
    prefix = jnp.dot(member.astype(BF16), before, preferred_element_type=F32) + carry_ref[...]
    ranks = [jnp.sum(jnp.where(oh, prefix, 0.0), axis=0, keepdims=True) for oh in onehots]
    rank_ref[...] = jnp.concatenate(ranks, axis=0).astype(jnp.int32)
    carry_ref[...] = carry_ref[...] + jnp.sum(member_f, axis=1, keepdims=True)
    cnt_ref[...] = jnp.broadcast_to(carry_ref[...], cnt_ref.shape)


def _outproj(out_a, out_b, x2, mod3, w_out, b_out, g_ffn, wr_t, br_col, S):
    N, D = x2.shape
    tm = TM_OUT
    tiles_per_seq = S // tm
    row = lambda i: (i, 0)
    colb = lambda i: (0, i)
    const = lambda i: (0, 0)
    return pl.pallas_call(
        _outproj_kernel,
        out_shape=[jax.ShapeDtypeStruct((N, D), F32), jax.ShapeDtypeStruct((N, D), F32),
                   jax.ShapeDtypeStruct((TOP_K, N), jnp.int32), jax.ShapeDtypeStruct((TOP_K, N), F32),
                   jax.ShapeDtypeStruct((TOP_K, N), jnp.int32), jax.ShapeDtypeStruct((N_EXPERTS, LANES), F32)],
        grid=(N // tm,),
        in_specs=[pl.BlockSpec((tm, out_a.shape[1]), row),
                  pl.BlockSpec((tm, out_b.shape[1]), row),
                  pl.BlockSpec((tm, D), row),
                  pl.BlockSpec((1, N_MOD, D), lambda i: (i // tiles_per_seq, 0, 0)),
                  pl.BlockSpec(w_out.shape, const),
                  pl.BlockSpec((1, D), const),
                  pl.BlockSpec((1, D), const),
                  pl.BlockSpec(wr_t.shape, const),
                  pl.BlockSpec((N_EXPERTS, 1), const)],
        out_specs=[pl.BlockSpec((tm, D), row), pl.BlockSpec((tm, D), row),
                   pl.BlockSpec((TOP_K, tm), colb), pl.BlockSpec((TOP_K, tm), colb),
                   pl.BlockSpec((TOP_K, tm), colb), pl.BlockSpec((N_EXPERTS, LANES), const)],
        scratch_shapes=[pltpu.VMEM((N_EXPERTS, 1), F32)],
        compiler_params=_cparams(("arbitrary",)),
        name="outproj_router",
    )(out_a, out_b, x2, mod3, w_out, b_out, g_ffn, wr_t, br_col)


def _dispatch_kernel(dest_ref, pad_ref, npad_ref, h_ref, rows_ref, zero_ref, sem, *, n_tokens):
    i = pl.program_id(0)
    tm = h_ref.shape[0]
    base = i * tm

    def issue(t, carry):
        for k in range(TOP_K):
            d = dest_ref[k * n_tokens + base + t]
            pltpu.make_async_copy(h_ref.at[pl.ds(t, 1)], rows_ref.at[pl.ds(d, 1)], sem).start()
        return carry

    lax.fori_loop(0, tm, issue, 0)

    for k in range(TOP_K):
        pltpu.make_async_copy(h_ref, rows_ref.at[pl.ds(0, tm)], sem).wait()

    @pl.when(i == pl.num_programs(0) - 1)
    def _():
        zero_ref[...] = jnp.zeros(zero_ref.shape, zero_ref.dtype)
        n_pad = npad_ref[0]
        n_used = npad_ref[1]
        bm = zero_ref.shape[0]
        n_blocks = rows_ref.shape[0] // bm
        zero_row = zero_ref.at[pl.ds(0, 1)]

        def issue_pad(j, carry):
            pltpu.make_async_copy(zero_row, rows_ref.at[pl.ds(pad_ref[j], 1)], sem).start()
            return carry

        def drain_pad(j, carry):
            pltpu.make_async_copy(zero_row, rows_ref.at[pl.ds(0, 1)], sem).wait()
            return carry

        def issue_block(j, carry):
            pltpu.make_async_copy(zero_ref, rows_ref.at[pl.ds(pl.multiple_of(j * bm, bm), bm)], sem).start()
            return carry

        def drain_block(j, carry):
            pltpu.make_async_copy(zero_ref, rows_ref.at[pl.ds(0, bm)], sem).wait()
            return carry

        lax.fori_loop(0, n_pad, issue_pad, 0)
        lax.fori_loop(n_used, n_blocks, issue_block, 0)
        lax.fori_loop(0, n_pad, drain_pad, 0)
        lax.fori_loop(n_used, n_blocks, drain_block, 0)


def _dispatch(dest_flat, pad_rows, n_pad, h2, n_rows):
    N, D = h2.shape
    tm = TM_ROWS
    return pl.pallas_call(
        functools.partial(_dispatch_kernel, n_tokens=N),
        out_shape=jax.ShapeDtypeStruct((n_rows, D), h2.dtype),
        grid_spec=pltpu.PrefetchScalarGridSpec(
            num_scalar_prefetch=3,
            grid=(N // tm,),
            in_specs=[pl.BlockSpec((tm, D), lambda i, *_: (i, 0))],
            out_specs=pl.BlockSpec(memory_space=pl.ANY),
            scratch_shapes=[pltpu.VMEM((BLOCK_ROWS, D), h2.dtype), pltpu.SemaphoreType.DMA]),
        compiler_params=_cparams(("arbitrary",)),
        name="dispatch",
    )(dest_flat, pad_rows, n_pad, h2)


def _experts_kernel(be_ref, nused_ref, x_ref, w1_ref, b1g_ref, b1l_ref, w2_ref, b2_ref, y_ref,
                    w1g_s, w1l_s, w2_s, t_s):
    i = pl.program_id(0)
    used = i < nused_ref[0]
    new_expert = (i == 0) | (be_ref[i] != be_ref[jnp.maximum(i - 1, 0)])

    @pl.when(used & new_expert)
    def _():
        n_db, ch, _ = t_s.shape
        half = ch // 2
        for c in range(w1_ref.shape[2] // ch):
            t = w1_ref[0, :, c * ch:(c + 1) * ch].T
            for db in range(n_db):
                t_s[db] = t[:, db * LANES:(db + 1) * LANES]
            for db in range(n_db):
                rows = slice(c * half, (c + 1) * half)
                cols = slice(db * LANES, (db + 1) * LANES)
                w1g_s[rows, cols] = t_s[db, pl.ds(0, half, stride=2), :].astype(BF16)
                w1l_s[rows, cols] = t_s[db, pl.ds(1, half, stride=2), :].astype(BF16)
        w2_s[...] = w2_ref[0].astype(BF16)

    @pl.when(used)
    def _():
        dn = (((1,), (1,)), ((), ()))
        x = x_ref[...].astype(BF16)
        ug = lax.dot_general(x, w1g_s[...], dn, preferred_element_type=F32) + b1g_ref[0]
        ul = lax.dot_general(x, w1l_s[...], dn, preferred_element_type=F32) + b1l_ref[0]
        glu = jnp.minimum(ug, SWIGLU_LIMIT)
        lin = jnp.clip(ul, -SWIGLU_LIMIT, SWIGLU_LIMIT)
        act = glu * jax.nn.sigmoid(SWIGLU_ALPHA * glu) * (lin + 1.0)
        y_ref[...] = jnp.dot(act.astype(BF16), w2_s[...], preferred_element_type=F32) + b2_ref[0]

    @pl.when(jnp.logical_not(used))
    def _():
        y_ref[...] = jnp.zeros(y_ref.shape, y_ref.dtype)


def _experts(block_e, n_used, x_rows, w1, b1g, b1l, w2, b2):
    n_rows, D = x_rows.shape
    E, Fh, _ = w2.shape
    bm = BLOCK_ROWS
    chunk = 256
    wsel = lambda i, be, nu: (be[i], 0, 0)
    return pl.pallas_call(
        _experts_kernel,
        out_shape=jax.ShapeDtypeStruct((n_rows, D), F32),
        grid_spec=pltpu.PrefetchScalarGridSpec(
            num_scalar_prefetch=2,
            grid=(n_rows // bm,),
            in_specs=[pl.BlockSpec((bm, D), lambda i, be, nu: (i, 0)),
                      pl.BlockSpec((1, D, 2 * Fh), wsel),
                      pl.BlockSpec((1, 1, Fh), wsel),
                      pl.BlockSpec((1, 1, Fh), wsel),
                      pl.BlockSpec((1, Fh, D), wsel),
                      pl.BlockSpec((1, 1, D), wsel)],
            out_specs=pl.BlockSpec((bm, D), lambda i, be, nu: (i, 0)),
            scratch_shapes=[pltpu.VMEM((Fh, D), BF16), pltpu.VMEM((Fh, D), BF16), pltpu.VMEM((Fh, D), BF16),
                            pltpu.VMEM((D // LANES, chunk, LANES), F32)]),
        compiler_params=_cparams(("arbitrary",), 56 * 1024 * 1024),
        name="experts",
    )(block_e, n_used, x_rows, w1, b1g, b1l, w2, b2)


def _combine_kernel(dest_ref, x1_ref, gate_ref, mod_ref, g_ref, y_ref, o_ref, buf_ref, sem, *, n_tokens, final_norm):
    i = pl.program_id(0)
    tm = x1_ref.shape[0]
    base = i * tm

    def issue(t, carry):
        for k in range(TOP_K):
            d = dest_ref[k * n_tokens + base + t]
            pltpu.make_async_copy(y_ref.at[pl.ds(d, 1)], buf_ref.at[k, pl.ds(t, 1)], sem).start()
        return carry

    lax.fori_loop(0, tm, issue, 0)

    for k in range(TOP_K):
        pltpu.make_async_copy(y_ref.at[pl.ds(0, tm)], buf_ref.at[k], sem).wait()

    gate = gate_ref[...]
    moe = gate[:, 0:1] * buf_ref[0]
    for k in range(1, TOP_K):
        moe = moe + gate[:, k:k + 1] * buf_ref[k]
    gt2 = mod_ref[0, 5:6, :]
    x2 = x1_ref[...] + gt2 * moe
    o_ref[...] = _rms(x2) * g_ref[...] if final_norm else x2


def _combine(dest_flat, x1, gate_t, mod3, g_final, y_rows, S, final_norm):
    N, D = x1.shape
    tm = TM_ROWS
    tiles_per_seq = S // tm
    return pl.pallas_call(
        functools.partial(_combine_kernel, n_tokens=N, final_norm=final_norm),
        out_shape=jax.ShapeDtypeStruct((N, D), F32),
        grid_spec=pltpu.PrefetchScalarGridSpec(
            num_scalar_prefetch=1,
            grid=(N // tm,),
            in_specs=[pl.BlockSpec((tm, D), lambda i, d: (i, 0)),
                      pl.BlockSpec((tm, TOP_K), lambda i, d: (i, 0)),
                      pl.BlockSpec((1, N_MOD, D), lambda i, d: (i // tiles_per_seq, 0, 0)),
                      pl.BlockSpec((1, D), lambda i, d: (0, 0)),
                      pl.BlockSpec(memory_space=pl.ANY)],
            out_specs=pl.BlockSpec((tm, D), lambda i, d: (i, 0)),
            scratch_shapes=[pltpu.VMEM((TOP_K, tm, D), F32), pltpu.SemaphoreType.DMA]),
        compiler_params=_cparams(("arbitrary",)),
        name="combine",
    )(dest_flat, x1, gate_t, mod3, g_final, y_rows)


def _routing_tables(counts, idx, rank, n_rows):
    bm = BLOCK_ROWS
    counts = counts.astype(jnp.int32)
    padded = (counts + bm - 1) // bm * bm
    pends = jnp.cumsum(padded)
    pstarts = pends - padded
    experts = jnp.arange(N_EXPERTS, dtype=jnp.int32)
    lookup = lambda table, e: jnp.sum(jnp.where(e[..., None] == experts, table, 0), axis=-1)
    dest = lookup(pstarts, idx) + rank
    n_blocks = n_rows // bm
    block_start = jnp.arange(n_blocks, dtype=jnp.int32) * bm
    block_e = jnp.minimum(jnp.sum(pends <= block_start[:, None], axis=-1), N_EXPERTS - 1).astype(jnp.int32)
    n_used = (pends[-1] // bm).astype(jnp.int32).reshape(1)
    pad_counts = padded - counts
    pad_ends = jnp.cumsum(pad_counts)
    max_pad = N_EXPERTS * (bm - 1)
    j = jnp.arange(max_pad, dtype=jnp.int32)
    e_j = jnp.minimum(jnp.sum(pad_ends <= j[:, None], axis=-1), N_EXPERTS - 1).astype(jnp.int32)
    pad_rows = lookup(pstarts + counts - (pad_ends - pad_counts), e_j) + j
    n_pad = pad_ends[-1].astype(jnp.int32)
    pad_rows = jnp.where(j < n_pad, pad_rows, 0).astype(jnp.int32)
    fill_counts = jnp.stack([n_pad, n_used[0]])
    return dest.reshape(-1).astype(jnp.int32), block_e, n_used, pad_rows, fill_counts


def _extended_in_weights(w_in, b_in):
    a_q = SWA_Q_HEADS * HEAD_DIM
    a_kv = SWA_KV_HEADS * HEAD_DIM
    b_w = DIFF_HEADS * DIFF_V_DIM
    spans = [(0, a_q)]
    for base in (a_q, a_q + a_kv):
        for j in range(SWA_KV_HEADS):
            spans += [(base + j * HEAD_DIM, base + (j + 1) * HEAD_DIM)] * 2
    spans.append((a_q + 2 * a_kv, a_q + 2 * a_kv + 3 * b_w))
    w_ext = jnp.concatenate([w_in[:, lo:hi] for lo, hi in spans], axis=1).astype(BF16)
    b_ext = jnp.concatenate([b_in[lo:hi] for lo, hi in spans]).reshape(1, -1)
    widths = (a_q, 2 * a_kv, 2 * a_kv, b_w, b_w, b_w)
    return w_ext, b_ext, widths


def kernel(x, c, positions, w_ada, b_ada, g_mix, w_in, b_in, attn_sinks, lambda_q1, lambda_k1, lambda_q2,
           lambda_k2, g_subln, w_out, b_out, g_ffn, w_router, b_router, w1, b1, w2, b2, g_final):
    B, S, D = x.shape
    N = B * S
    depth = w_ada.shape[0]
    assert TM_PROJ == TQ_DIFF, "the input projection writes q/v transposed per attention tile"
    n_rows = (N * TOP_K + N_EXPERTS * (BLOCK_ROWS - 1) + BLOCK_ROWS - 1) // BLOCK_ROWS * BLOCK_ROWS

    inv = 1.0 / (ROPE_THETA ** (jnp.arange(0, HEAD_DIM, 2, dtype=F32) / HEAD_DIM))
    inv_lane = jnp.tile(inv, LANES // (HEAD_DIM // 2)).reshape(1, LANES)
    pos2 = positions.reshape(N, 1)
    xcur = x.reshape(N, D)

    for layer in range(depth):
        last = layer == depth - 1
        lambda_init = 0.8 - 0.6 * math.exp(-0.3 * layer)
        mod3 = _adaln(c, w_ada[layer], b_ada[layer]).reshape(B, N_MOD, D)

        w_ext, b_ext, widths = _extended_in_weights(w_in[layer], b_in[layer])
        qa, ka2, va2, qdt, kd, vdt = _inproj(xcur, pos2, inv_lane, mod3, g_mix[layer].reshape(1, D),
                                             w_ext, b_ext, S, widths)
        out_a = _swa(attn_sinks[layer], qa, ka2, va2, B, S)
        lam_vecs = jnp.stack([lambda_q1[layer], lambda_k1[layer], lambda_q2[layer], lambda_k2[layer]])
        out_b = _diffattn(lam_vecs, g_subln[layer].reshape(1, DIFF_V_DIM), qdt, kd, vdt, B, S, lambda_init)

        x1, h2, idx, gate, rank, counts = _outproj(
            out_a, out_b, xcur, mod3, w_out[layer].astype(BF16), b_out[layer].reshape(1, D),
            g_ffn[layer].reshape(1, D), w_router[layer].T, b_router[layer].reshape(N_EXPERTS, 1), S)

        dest, block_e, n_used, pad_rows, n_pad = _routing_tables(counts[:, 0], idx, rank, n_rows)
        x_rows = _dispatch(dest, pad_rows, n_pad, h2, n_rows)

        y_rows = _experts(block_e, n_used, x_rows, w1[layer],
                          b1[layer][:, None, 0::2], b1[layer][:, None, 1::2],
                          w2[layer], b2[layer][:, None, :])

        xcur = _combine(dest, x1, gate.T, mod3, g_final.reshape(1, D), y_rows, S, final_norm=last)
    return xcur.reshape(B, S, D)
```

```python
import functools
import math

import jax
import jax.numpy as jnp
from jax import lax
from jax.experimental import pallas as pl
from jax.experimental.pallas import tpu as pltpu

HEAD_DIM = 64
SWA_Q_HEADS = 8
SWA_KV_HEADS = 2
SWA_GROUP = SWA_Q_HEADS // SWA_KV_HEADS
WINDOW = 128
DIFF_HEADS = 4
DIFF_V_DIM = 2 * HEAD_DIM
ROPE_THETA = 10000.0
N_EXPERTS = 32
TOP_K = 4
SWIGLU_ALPHA = 1.702
SWIGLU_LIMIT = 7.0
EPS = 1e-5
N_MOD = 6

LANES = 128
F32 = jnp.float32
BF16 = jnp.bfloat16
NEG_INF = float("-inf")

TM_PROJ = 512
TQ_SWA = 512
TQ_DIFF = 512
VT_ROWS = DIFF_V_DIM + 16
TM_OUT = 256
TM_ROWS = 256
BLOCK_ROWS = 256
VMEM_LIMIT = 48 * 1024 * 1024


def _cparams(sem, vmem=VMEM_LIMIT):
    return pltpu.CompilerParams(dimension_semantics=sem, vmem_limit_bytes=vmem)


def _adaln_kernel(ct_ref, w_ref, b_ref, o_ref):
    c = ct_ref[...]
    cond = c * jax.nn.sigmoid(c)
    w = w_ref[...]
    rows = [jnp.sum(w * cond[:, b:b + 1], axis=0, keepdims=True) for b in range(c.shape[1])]
    o_ref[...] = jnp.concatenate(rows, axis=0) + b_ref[...]


def _adaln(c, w_ada, b_ada):
    B, D = c.shape
    n_out = w_ada.shape[1]
    tn = 1024
    return pl.pallas_call(
        _adaln_kernel,
        out_shape=jax.ShapeDtypeStruct((B, n_out), F32),
        grid=(n_out // tn,),
        in_specs=[pl.BlockSpec((D, B), lambda j: (0, 0)),
                  pl.BlockSpec((D, tn), lambda j: (0, j)),
                  pl.BlockSpec((1, tn), lambda j: (0, j))],
        out_specs=pl.BlockSpec((B, tn), lambda j: (0, j)),
        compiler_params=_cparams(("arbitrary",)),
        name="adaln",
    )(c.T, w_ada, b_ada.reshape(1, n_out))


def _rms(x):
    return x * lax.rsqrt(jnp.mean(x * x, axis=-1, keepdims=True) + EPS)


def _inproj_kernel(x_ref, pos_ref, inv_ref, mod_ref, g_ref, w_ref, b_ref,
                   qa_ref, ka_ref, va_ref, qd_ref, kd_ref, vd_ref):
    x = x_ref[...]
    sh = mod_ref[0, 0:1, :]
    sc = mod_ref[0, 1:2, :]
    h = _rms(x) * g_ref[...] * (1.0 + sc) + sh
    proj = jnp.dot(h.astype(BF16), w_ref[...], preferred_element_type=F32) + b_ref[...]

    ang = pos_ref[...].astype(F32) * inv_ref[...]
    lane = lax.broadcasted_iota(jnp.int32, (1, LANES), 1)
    first_half = (lane & (HEAD_DIM - 1)) < (HEAD_DIM // 2)
    cos = jnp.cos(ang)
    sin = jnp.sin(ang)
    sin_signed = jnp.where(first_half, -sin, sin)

    def rope(t):
        partner = jnp.where(first_half,
                            pltpu.roll(t, LANES - HEAD_DIM // 2, axis=1),
                            pltpu.roll(t, HEAD_DIM // 2, axis=1))
        return t * cos + partner * sin_signed

    def emit(out_ref, col0, width, rotary, scale, transposed):
        for j in range(width // LANES):
            t = proj[:, col0 + j * LANES: col0 + (j + 1) * LANES]
            if rotary:
                t = rope(t)
            if scale != 1.0:
                t = t * scale
            if transposed:
                rows = out_ref.shape[2] // (width // LANES)
                out_ref[0, 0, j * rows:j * rows + LANES, :] = t.T.astype(out_ref.dtype)
                if rows > LANES:
                    fill = lax.broadcasted_iota(jnp.int32, (rows - LANES, t.shape[0]), 0) == 0
                    out_ref[0, 0, j * rows + LANES:(j + 1) * rows, :] = fill.astype(out_ref.dtype)
            else:
                out_ref[:, j * LANES:(j + 1) * LANES] = t.astype(out_ref.dtype)

    swa_scale = 1.0 / math.sqrt(HEAD_DIM)
    diff_scale = math.log2(math.e) / math.sqrt(HEAD_DIM)
    col = 0
    for out_ref, width, rotary, scale, transposed in (
            (qa_ref, qa_ref.shape[1], True, swa_scale, False), (ka_ref, ka_ref.shape[1], True, 1.0, False),
            (va_ref, va_ref.shape[1], False, 1.0, False), (qd_ref, qd_ref.shape[2], True, diff_scale, True),
            (kd_ref, kd_ref.shape[1], True, 1.0, False),
            (vd_ref, vd_ref.shape[2] // VT_ROWS * LANES, False, 1.0, True)):
        emit(out_ref, col, width, rotary, scale, transposed)
        col += width


def _inproj(x2, pos2, inv_lane, mod3, g_mix, w_ext, b_ext, S, widths):
    N, D = x2.shape
    tm = TM_PROJ
    C = w_ext.shape[1]
    tiles_per_seq = S // tm
    row = lambda i: (i, 0)
    t_rows = (0, 0, 0, widths[3], 0, widths[5] // LANES * VT_ROWS)
    out_shape, out_specs = [], []
    for w, tr in zip(widths, t_rows):
        if tr:
            out_shape.append(jax.ShapeDtypeStruct((N // S, tiles_per_seq, tr, tm), BF16))
            out_specs.append(pl.BlockSpec((1, 1, tr, tm), lambda i: (i // tiles_per_seq, i % tiles_per_seq, 0, 0)))
        else:
            out_shape.append(jax.ShapeDtypeStruct((N, w), BF16))
            out_specs.append(pl.BlockSpec((tm, w), row))
    return pl.pallas_call(
        _inproj_kernel,
        out_shape=out_shape,
        grid=(N // tm,),
        in_specs=[pl.BlockSpec((tm, D), row),
                  pl.BlockSpec((tm, 1), row),
                  pl.BlockSpec((1, LANES), lambda i: (0, 0)),
                  pl.BlockSpec((1, N_MOD, D), lambda i: (i // tiles_per_seq, 0, 0)),
                  pl.BlockSpec((1, D), lambda i: (0, 0)),
                  pl.BlockSpec((D, C), lambda i: (0, 0)),
                  pl.BlockSpec((1, C), lambda i: (0, 0))],
        out_specs=out_specs,
        compiler_params=_cparams(("arbitrary",)),
        name="inproj",
    )(x2, pos2, inv_lane, mod3, g_mix, w_ext, b_ext)


def _swa_kernel(sink_ref, q_ref, kc_ref, kp_ref, vc_ref, vp_ref, o_ref):
    i = pl.program_id(1)
    tq = q_ref.shape[0]
    lane = lax.broadcasted_iota(jnp.int32, (1, LANES), 1)
    lo = lane < HEAD_DIM
    qi = lax.broadcasted_iota(jnp.int32, (WINDOW, 2 * WINDOW), 0) + WINDOW
    kj = lax.broadcasted_iota(jnp.int32, (WINDOW, 2 * WINDOW), 1)
    band = (qi - kj >= 0) & (qi - kj < WINDOW)
    dn = (((1,), (1,)), ((), ()))
    for c in range(tq // WINDOW):
        if c == 0:
            kcat = jnp.concatenate([kp_ref[...], kc_ref[0:WINDOW, :]], axis=0)
            vcat = jnp.concatenate([vp_ref[...], vc_ref[0:WINDOW, :]], axis=0)
            mask = band & (kj >= jnp.where(i > 0, 0, WINDOW))
        else:
            kcat = kc_ref[(c - 1) * WINDOW:(c + 1) * WINDOW, :]
            vcat = vc_ref[(c - 1) * WINDOW:(c + 1) * WINDOW, :]
            mask = band
        for j in range(SWA_KV_HEADS):
            kj2 = kcat[:, j * LANES:(j + 1) * LANES]
            vj2 = vcat[:, j * LANES:(j + 1) * LANES]
            zero = jnp.zeros_like(kj2)
            k_halves = (jnp.where(lo, kj2, zero), jnp.where(lo, zero, kj2))
            v_halves = (jnp.where(lo, vj2, zero), jnp.where(lo, zero, vj2))
            for p in range(SWA_GROUP // 2):
                g = j * (SWA_GROUP // 2) + p
                q = q_ref[c * WINDOW:(c + 1) * WINDOW, g * LANES:(g + 1) * LANES]
                out = jnp.zeros((WINDOW, LANES), F32)
                for half in range(2):
                    sink = sink_ref[2 * g + half]
                    s = lax.dot_general(q, k_halves[half], dn, preferred_element_type=F32)
                    s = jnp.where(mask, s, NEG_INF)
                    m = jnp.maximum(jnp.max(s, axis=1, keepdims=True), sink)
                    e = jnp.exp(s - m)
                    denom = jnp.sum(e, axis=1, keepdims=True) + jnp.exp(sink - m)
                    pv = jnp.dot(e.astype(BF16), v_halves[half], preferred_element_type=F32)
                    out = out + pv / denom
                o_ref[c * WINDOW:(c + 1) * WINDOW, g * LANES:(g + 1) * LANES] = out.astype(o_ref.dtype)


def _swa(sinks, qa, ka2, va2, B, S):
    N = qa.shape[0]
    tq = TQ_SWA
    nq = S // tq
    wpt = tq // WINDOW
    wps = S // WINDOW
    cur = lambda b, i: (b * nq + i, 0)
    prev = lambda b, i: (b * wps + jnp.maximum(i * wpt - 1, 0), 0)
    return pl.pallas_call(
        _swa_kernel,
        out_shape=jax.ShapeDtypeStruct((N, qa.shape[1]), BF16),
        grid=(B, nq),
        in_specs=[pl.BlockSpec(memory_space=pltpu.SMEM),
                  pl.BlockSpec((tq, qa.shape[1]), cur),
                  pl.BlockSpec((tq, ka2.shape[1]), cur),
                  pl.BlockSpec((WINDOW, ka2.shape[1]), prev),
                  pl.BlockSpec((tq, va2.shape[1]), cur),
                  pl.BlockSpec((WINDOW, va2.shape[1]), prev)],
        out_specs=pl.BlockSpec((tq, qa.shape[1]), cur),
        compiler_params=_cparams(("arbitrary", "arbitrary")),
        name="swa",
    )(sinks, qa, ka2, ka2, va2, va2)


def _diff_kernel(lam_ref, g_ref, qt_ref, k_ref, vt_ref, o_ref, sa_ref, sb_ref, m_ref, acc_ref, *, lambda_init):
    i = pl.program_id(2)
    tq = qt_ref.shape[3]
    tk = vt_ref.shape[3]
    qt = qt_ref[0, 0]
    lane = lax.broadcasted_iota(jnp.int32, (1, LANES), 1)
    lo = lane < HEAD_DIM
    m_ref[...] = jnp.full(m_ref.shape, NEG_INF, F32)
    acc_ref[...] = jnp.zeros(acc_ref.shape, F32)

    def scores(c, s_ref):
        k = k_ref[pl.ds(pl.multiple_of(c * tk, tk), tk), :]
        zero = jnp.zeros_like(k)
        s_ref[0] = jnp.dot(jnp.where(lo, k, zero), qt, preferred_element_type=F32)
        s_ref[1] = jnp.dot(jnp.where(lo, zero, k), qt, preferred_element_type=F32)

    def consume(c, s_ref, diagonal):
        vt = vt_ref[0, c]
        for mp in range(2):
            s = s_ref[mp]
            if diagonal:
                kpos = lax.broadcasted_iota(jnp.int32, (tk, tq), 0)
                qpos = lax.broadcasted_iota(jnp.int32, (tk, tq), 1)
                s = jnp.where(kpos <= qpos, s, NEG_INF)
            m_prev = m_ref[mp]
            m_new = jnp.maximum(m_prev, jnp.max(s, axis=0, keepdims=True))
            alpha = jnp.exp2(m_prev - m_new)
            p = jnp.exp2(s - m_new).astype(BF16)
            acc_ref[mp] = alpha * acc_ref[mp] + jnp.dot(vt, p, preferred_element_type=F32)
            m_ref[mp] = m_new

    scores(0, sa_ref)

    def pair(jj, carry):
        c = 2 * jj
        scores(c + 1, sb_ref)
        consume(c, sa_ref, False)
        scores(c + 2, sa_ref)
        consume(c + 1, sb_ref, False)
        return carry

    lax.fori_loop(0, lax.shift_right_logical(i, 1), pair, 0)

    @pl.when(i % 2 == 0)
    def _():
        consume(i, sa_ref, True)

    @pl.when(i % 2 == 1)
    def _():
        scores(i, sb_ref)
        consume(i - 1, sa_ref, False)
        consume(i, sb_ref, True)

    lq1, lk1, lq2, lk2 = (lam_ref[r:r + 1, :] for r in range(4))
    lam = (jnp.exp(jnp.sum(lq1 * lk1, axis=1, keepdims=True))
           - jnp.exp(jnp.sum(lq2 * lk2, axis=1, keepdims=True)) + lambda_init)
    d = DIFF_V_DIM
    ot = (acc_ref[0, 0:d, :] / acc_ref[0, d:d + 1, :]
          - lam * (acc_ref[1, 0:d, :] / acc_ref[1, d:d + 1, :]))
    ot = ot * lax.rsqrt(jnp.mean(ot * ot, axis=0, keepdims=True) + EPS)
    o_ref[...] = (ot.T * g_ref[...] * (1.0 - lambda_init)).astype(o_ref.dtype)


def _diffattn(lam_vecs, g_subln, qdt, kd, vdt, B, S, lambda_init):
    N, C = kd.shape
    tq = TQ_DIFF
    nq = S // tq
    return pl.pallas_call(
        functools.partial(_diff_kernel, lambda_init=lambda_init),
        out_shape=jax.ShapeDtypeStruct((N, C), BF16),
        grid=(B, DIFF_HEADS, nq),
        in_specs=[pl.BlockSpec((4, HEAD_DIM), lambda b, h, i: (0, 0)),
                  pl.BlockSpec((1, DIFF_V_DIM), lambda b, h, i: (0, 0)),
                  pl.BlockSpec((1, 1, LANES, tq), lambda b, h, i: (b, i, h, 0)),
                  pl.BlockSpec((S, LANES), lambda b, h, i: (b, h)),
                  pl.BlockSpec((1, nq, VT_ROWS, tq), lambda b, h, i: (b, 0, h, 0))],
        out_specs=pl.BlockSpec((tq, LANES), lambda b, h, i: (b * nq + i, h)),
        scratch_shapes=[pltpu.VMEM((2, tq, tq), F32), pltpu.VMEM((2, tq, tq), F32),
                        pltpu.VMEM((2, 1, tq), F32), pltpu.VMEM((2, VT_ROWS, tq), F32)],
        compiler_params=_cparams(("arbitrary", "arbitrary", "arbitrary")),
        name="diffattn",
    )(lam_vecs, g_subln, qdt, kd, vdt)


def _outproj_kernel(oa_ref, ob_ref, x_ref, mod_ref, wo_ref, bo_ref, g_ref, wr_ref, br_ref,
                    x1_ref, h2_ref, idx_ref, gate_ref, rank_ref, cnt_ref, carry_ref):
    i = pl.program_id(0)
    tm = x_ref.shape[0]
    half = oa_ref.shape[1]

    @pl.when(i == 0)
    def _():
        carry_ref[...] = jnp.zeros(carry_ref.shape, F32)

    gt1 = mod_ref[0, 2:3, :]
    sh2 = mod_ref[0, 3:4, :]
    sc2 = mod_ref[0, 4:5, :]
    mixed = (jnp.dot(oa_ref[...], wo_ref[0:half, :], preferred_element_type=F32)
             + jnp.dot(ob_ref[...], wo_ref[half:, :], preferred_element_type=F32) + bo_ref[...])
    x1 = x_ref[...] + gt1 * mixed
    x1_ref[...] = x1
    h2 = _rms(x1) * g_ref[...] * (1.0 + sc2) + sh2
    h2_ref[...] = h2

    dn = (((1,), (1,)), ((), ()))
    h_hi = h2.astype(BF16)
    h_lo = (h2 - h_hi.astype(F32)).astype(BF16)
    w = wr_ref[...]
    w_hi = w.astype(BF16)
    w_lo = (w - w_hi.astype(F32)).astype(BF16)
    logits = (lax.dot_general(w_hi, h_hi, dn, preferred_element_type=F32)
              + lax.dot_general(w_hi, h_lo, dn, preferred_element_type=F32)
              + lax.dot_general(w_lo, h_hi, dn, preferred_element_type=F32)
              + br_ref[...])

    eidx = lax.broadcasted_iota(jnp.int32, logits.shape, 0)
    vals = logits
    onehots, top_vals, top_idx = [], [], []
    for _k in range(TOP_K):
        mx = jnp.max(vals, axis=0, keepdims=True)
        sel = jnp.min(jnp.where(vals == mx, eidx, N_EXPERTS), axis=0, keepdims=True)
        oh = eidx == sel
        onehots.append(oh)
        top_vals.append(mx)
        top_idx.append(sel)
        vals = jnp.where(oh, NEG_INF, vals)
    exps = [jnp.exp(v - top_vals[0]) for v in top_vals]
    denom = exps[0] + exps[1] + exps[2] + exps[3]
    gate_ref[...] = jnp.concatenate([e / denom for e in exps], axis=0)
    idx_ref[...] = jnp.concatenate(top_idx, axis=0)

    member = (onehots[0] | onehots[1] | onehots[2] | onehots[3])
    member_f = member.astype(F32)
    t_src = lax.broadcasted_iota(jnp.int32, (tm, tm), 0)
    t_dst = lax.broadcasted_iota(jnp.int32, (tm, tm), 1)
    before = (t_src < t_dst).astype(BF16)
    prefix = jnp.dot(member.astype(BF16), before, preferred_element_type=F32) + carry_ref[...]
    ranks = [jnp.sum(jnp.where(oh, prefix, 0.0), axis=0, keepdims=True) for oh in onehots]
    rank_ref[...] = jnp.concatenate(ranks, axis=0).astype(jnp.int32)
    carry_ref[...] = carry_ref[...] + jnp.sum(member_f, axis=1, keepdims=True)
    cnt_ref[...] = jnp.broadcast_to(carry_ref[...], cnt_ref.shape)


def _outproj(out_a, out_b, x2, mod3, w_out, b_out, g_ffn, wr_t, br_col, S):
    N, D = x2.shape
    tm = TM_OUT
    tiles_per_seq = S // tm
    row = lambda i: (i, 0)
    colb = lambda i: (0, i)
    const = lambda i: (0, 0)
    return pl.pallas_call(
        _outproj_kernel,
        out_shape=[jax.ShapeDtypeStruct((N, D), F32), jax.ShapeDtypeStruct((N, D), F32),
                   jax.ShapeDtypeStruct((TOP_K, N), jnp.int32), jax.ShapeDtypeStruct((TOP_K, N), F32),
                   jax.ShapeDtypeStruct((TOP_K, N), jnp.int32), jax.ShapeDtypeStruct((N_EXPERTS, LANES), F32)],
        grid=(N // tm,),
        in_specs=[pl.BlockSpec((tm, out_a.shape[1]), row),
                  pl.BlockSpec((tm, out_b.shape[1]), row),
                  pl.BlockSpec((tm, D), row),
                  pl.BlockSpec((1, N_MOD, D), lambda i: (i // tiles_per_seq, 0, 0)),
                  pl.BlockSpec(w_out.shape, const),
                  pl.BlockSpec((1, D), const),
                  pl.BlockSpec((1, D), const),
                  pl.BlockSpec(wr_t.shape, const),
                  pl.BlockSpec((N_EXPERTS, 1), const)],
        out_specs=[pl.BlockSpec((tm, D), row), pl.BlockSpec((tm, D), row),
                   pl.BlockSpec((TOP_K, tm), colb), pl.BlockSpec((TOP_K, tm), colb),
                   pl.BlockSpec((TOP_K, tm), colb), pl.BlockSpec((N_EXPERTS, LANES), const)],
        scratch_shapes=[pltpu.VMEM((N_EXPERTS, 1), F32)],
        compiler_params=_cparams(("arbitrary",)),
        name="outproj_router",
    )(out_a, out_b, x2, mod3, w_out, b_out, g_ffn, wr_t, br_col)


def _dispatch_kernel(dest_ref, pad_ref, npad_ref, h_ref, rows_ref, zero_ref, sem, *, n_tokens):
    i = pl.program_id(0)
    tm = h_ref.shape[0]
    base = i * tm

    def issue(t, carry):
        for k in range(TOP_K):
            d = dest_ref[k * n_tokens + base + t]
            pltpu.make_async_copy(h_ref.at[pl.ds(t, 1)], rows_ref.at[pl.ds(d, 1)], sem).start(priority=k % 2)
        return carry

    lax.fori_loop(0, tm, issue, 0, unroll=2)

    for k in range(TOP_K):
        pltpu.make_async_copy(h_ref, rows_ref.at[pl.ds(0, tm)], sem).wait()

    @pl.when(i == pl.num_programs(0) - 1)
    def _():
        zero_ref[...] = jnp.zeros(zero_ref.shape, zero_ref.dtype)
        n_pad = npad_ref[0]
        n_used = npad_ref[1]
        bm = zero_ref.shape[0]
        n_blocks = rows_ref.shape[0] // bm
        zero_row = zero_ref.at[pl.ds(0, 1)]

        def issue_pad(j, carry):
            pltpu.make_async_copy(zero_row, rows_ref.at[pl.ds(pad_ref[j], 1)], sem).start()
            return carry

        def drain_pad(j, carry):
            pltpu.make_async_copy(zero_row, rows_ref.at[pl.ds(0, 1)], sem).wait()
            return carry

        def issue_block(j, carry):
            pltpu.make_async_copy(zero_ref, rows_ref.at[pl.ds(pl.multiple_of(j * bm, bm), bm)], sem).start()
            return carry

        def drain_block(j, carry):
            pltpu.make_async_copy(zero_ref, rows_ref.at[pl.ds(0, bm)], sem).wait()
            return carry

        lax.fori_loop(0, n_pad, issue_pad, 0)
        lax.fori_loop(n_used, n_blocks, issue_block, 0)
        lax.fori_loop(0, n_pad, drain_pad, 0)
        lax.fori_loop(n_used, n_blocks, drain_block, 0)


def _dispatch(dest_flat, pad_rows, n_pad, h2, n_rows):
    N, D = h2.shape
    tm = TM_ROWS
    return pl.pallas_call(
        functools.partial(_dispatch_kernel, n_tokens=N),
        out_shape=jax.ShapeDtypeStruct((n_rows, D), h2.dtype),
        grid_spec=pltpu.PrefetchScalarGridSpec(
            num_scalar_prefetch=3,
            grid=(N // tm,),
            in_specs=[pl.BlockSpec((tm, D), lambda i, *_: (i, 0))],
            out_specs=pl.BlockSpec(memory_space=pl.ANY),
            scratch_shapes=[pltpu.VMEM((BLOCK_ROWS, D), h2.dtype), pltpu.SemaphoreType.DMA]),
        compiler_params=_cparams(("arbitrary",)),
        name="dispatch",
    )(dest_flat, pad_rows, n_pad, h2)


def _experts_kernel(be_ref, nused_ref, x_ref, w1_ref, b1g_ref, b1l_ref, w2_ref, b2_ref, y_ref,
                    w1g_s, w1l_s, w2_s, t_s):
    i = pl.program_id(0)
    used = i < nused_ref[0]
    new_expert = (i == 0) | (be_ref[i] != be_ref[jnp.maximum(i - 1, 0)])

    @pl.when(used & new_expert)
    def _():
        n_db, ch, _ = t_s.shape
        half = ch // 2
        for c in range(w1_ref.shape[2] // ch):
            t = w1_ref[0, :, c * ch:(c + 1) * ch].T
            for db in range(n_db):
                t_s[db] = t[:, db * LANES:(db + 1) * LANES]
            for db in range(n_db):
                rows = slice(c * half, (c + 1) * half)
                cols = slice(db * LANES, (db + 1) * LANES)
                w1g_s[rows, cols] = t_s[db, pl.ds(0, half, stride=2), :].astype(BF16)
                w1l_s[rows, cols] = t_s[db, pl.ds(1, half, stride=2), :].astype(BF16)
        w2_s[...] = w2_ref[0].astype(BF16)

    @pl.when(used)
    def _():
        dn = (((1,), (1,)), ((), ()))
        x = x_ref[...].astype(BF16)
        ug = lax.dot_general(x, w1g_s[...], dn, preferred_element_type=F32) + b1g_ref[0]
        ul = lax.dot_general(x, w1l_s[...], dn, preferred_element_type=F32) + b1l_ref[0]
        glu = jnp.minimum(ug, SWIGLU_LIMIT)
        lin = jnp.clip(ul, -SWIGLU_LIMIT, SWIGLU_LIMIT)
        act = glu * jax.nn.sigmoid(SWIGLU_ALPHA * glu) * (lin + 1.0)
        y_ref[...] = jnp.dot(act.astype(BF16), w2_s[...], preferred_element_type=F32) + b2_ref[0]

    @pl.when(jnp.logical_not(used))
    def _():
        y_ref[...] = jnp.zeros(y_ref.shape, y_ref.dtype)


def _experts(block_e, n_used, x_rows, w1, b1g, b1l, w2, b2):
    n_rows, D = x_rows.shape
    E, Fh, _ = w2.shape
    bm = BLOCK_ROWS
    chunk = 256
    wsel = lambda i, be, nu: (be[i], 0, 0)
    return pl.pallas_call(
        _experts_kernel,
        out_shape=jax.ShapeDtypeStruct((n_rows, D), F32),
        grid_spec=pltpu.PrefetchScalarGridSpec(
            num_scalar_prefetch=2,
            grid=(n_rows // bm,),
            in_specs=[pl.BlockSpec((bm, D), lambda i, be, nu: (i, 0)),
                      pl.BlockSpec((1, D, 2 * Fh), wsel),
                      pl.BlockSpec((1, 1, Fh), wsel),
                      pl.BlockSpec((1, 1, Fh), wsel),
                      pl.BlockSpec((1, Fh, D), wsel),
                      pl.BlockSpec((1, 1, D), wsel)],
            out_specs=pl.BlockSpec((bm, D), lambda i, be, nu: (i, 0)),
            scratch_shapes=[pltpu.VMEM((Fh, D), BF16), pltpu.VMEM((Fh, D), BF16), pltpu.VMEM((Fh, D), BF16),
                            pltpu.VMEM((D // LANES, chunk, LANES), F32)]),
        compiler_params=_cparams(("arbitrary",), 56 * 1024 * 1024),
        name="experts",
    )(block_e, n_used, x_rows, w1, b1g, b1l, w2, b2)


def _combine_kernel(dest_ref, x1_ref, gate_ref, mod_ref, g_ref, y_ref, o_ref, buf_ref, sem, *, n_tokens, final_norm):
    i = pl.program_id(0)
    tm = x1_ref.shape[0]
    base = i * tm

    def issue(t, carry):
        for k in range(TOP_K):
            d = dest_ref[k * n_tokens + base + t]
            pltpu.make_async_copy(y_ref.at[pl.ds(d, 1)], buf_ref.at[k, pl.ds(t, 1)], sem).start(priority=k % 2)
        return carry

    lax.fori_loop(0, tm, issue, 0, unroll=2)

    for k in range(TOP_K):
        pltpu.make_async_copy(y_ref.at[pl.ds(0, tm)], buf_ref.at[k], sem).wait()

    gate = gate_ref[...]
    moe = gate[:, 0:1] * buf_ref[0]
    for k in range(1, TOP_K):
        moe = moe + gate[:, k:k + 1] * buf_ref[k]
    gt2 = mod_ref[0, 5:6, :]
    x2 = x1_ref[...] + gt2 * moe
    o_ref[...] = _rms(x2) * g_ref[...] if final_norm else x2


def _combine(dest_flat, x1, gate_t, mod3, g_final, y_rows, S, final_norm):
    N, D = x1.shape
    tm = TM_ROWS
    tiles_per_seq = S // tm
    return pl.pallas_call(
        functools.partial(_combine_kernel, n_tokens=N, final_norm=final_norm),
        out_shape=jax.ShapeDtypeStruct((N, D), F32),
        grid_spec=pltpu.PrefetchScalarGridSpec(
            num_scalar_prefetch=1,
            grid=(N // tm,),
            in_specs=[pl.BlockSpec((tm, D), lambda i, d: (i, 0)),
                      pl.BlockSpec((tm, TOP_K), lambda i, d: (i, 0)),
                      pl.BlockSpec((1, N_MOD, D), lambda i, d: (i // tiles_per_seq, 0, 0)),
                      pl.BlockSpec((1, D), lambda i, d: (0, 0)),
                      pl.BlockSpec(memory_space=pl.ANY)],
            out_specs=pl.BlockSpec((tm, D), lambda i, d: (i, 0)),
            scratch_shapes=[pltpu.VMEM((TOP_K, tm, D), F32), pltpu.SemaphoreType.DMA]),
        compiler_params=_cparams(("arbitrary",)),
        name="combine",
    )(dest_flat, x1, gate_t, mod3, g_final, y_rows)


def _routing_tables(counts, idx, rank, n_rows):
    bm = BLOCK_ROWS
    counts = counts.astype(jnp.int32)
    padded = (counts + bm - 1) // bm * bm
    pends = jnp.cumsum(padded)
    pstarts = pends - padded
    experts = jnp.arange(N_EXPERTS, dtype=jnp.int32)
    lookup = lambda table, e: jnp.sum(jnp.where(e[..., None] == experts, table, 0), axis=-1)
    dest = lookup(pstarts, idx) + rank
    n_blocks = n_rows // bm
    block_start = jnp.arange(n_blocks, dtype=jnp.int32) * bm
    block_e = jnp.minimum(jnp.sum(pends <= block_start[:, None], axis=-1), N_EXPERTS - 1).astype(jnp.int32)
    n_used = (pends[-1] // bm).astype(jnp.int32).reshape(1)
    pad_counts = padded - counts
    pad_ends = jnp.cumsum(pad_counts)
    max_pad = N_EXPERTS * (bm - 1)
    j = jnp.arange(max_pad, dtype=jnp.int32)
    e_j = jnp.minimum(jnp.sum(pad_ends <= j[:, None], axis=-1), N_EXPERTS - 1).astype(jnp.int32)
    pad_rows = lookup(pstarts + counts - (pad_ends - pad_counts), e_j) + j
    n_pad = pad_ends[-1].astype(jnp.int32)
    pad_rows = jnp.where(j < n_pad, pad_rows, 0).astype(jnp.int32)
    fill_counts = jnp.stack([n_pad, n_used[0]])
    return dest.reshape(-1).astype(jnp.int32), block_e, n_used, pad_rows, fill_counts


def _extended_in_weights(w_in, b_in):
    a_q = SWA_Q_HEADS * HEAD_DIM
    a_kv = SWA_KV_HEADS * HEAD_DIM
    b_w = DIFF_HEADS * DIFF_V_DIM
    spans = [(0, a_q)]
    for base in (a_q, a_q + a_kv):
        for j in range(SWA_KV_HEADS):
            spans += [(base + j * HEAD_DIM, base + (j + 1) * HEAD_DIM)] * 2
    spans.append((a_q + 2 * a_kv, a_q + 2 * a_kv + 3 * b_w))
    w_ext = jnp.concatenate([w_in[:, lo:hi] for lo, hi in spans], axis=1).astype(BF16)
    b_ext = jnp.concatenate([b_in[lo:hi] for lo, hi in spans]).reshape(1, -1)
    widths = (a_q, 2 * a_kv, 2 * a_kv, b_w, b_w, b_w)
    return w_ext, b_ext, widths


def kernel(x, c, positions, w_ada, b_ada, g_mix, w_in, b_in, attn_sinks, lambda_q1, lambda_k1, lambda_q2,
           lambda_k2, g_subln, w_out, b_out, g_ffn, w_router, b_router, w1, b1, w2, b2, g_final):
    B, S, D = x.shape
    N = B * S
    depth = w_ada.shape[0]
    assert TM_PROJ == TQ_DIFF, "the input projection writes q/v transposed per attention tile"
    n_rows = (N * TOP_K + N_EXPERTS * (BLOCK_ROWS - 1) + BLOCK_ROWS - 1) // BLOCK_ROWS * BLOCK_ROWS

    inv = 1.0 / (ROPE_THETA ** (jnp.arange(0, HEAD_DIM, 2, dtype=F32) / HEAD_DIM))
    inv_lane = jnp.tile(inv, LANES // (HEAD_DIM // 2)).reshape(1, LANES)
    pos2 = positions.reshape(N, 1)
    xcur = x.reshape(N, D)

    for layer in range(depth):
        last = layer == depth - 1
        lambda_init = 0.8 - 0.6 * math.exp(-0.3 * layer)
        mod3 = _adaln(c, w_ada[layer], b_ada[layer]).reshape(B, N_MOD, D)

        w_ext, b_ext, widths = _extended_in_weights(w_in[layer], b_in[layer])
        qa, ka2, va2, qdt, kd, vdt = _inproj(xcur, pos2, inv_lane, mod3, g_mix[layer].reshape(1, D),
                                             w_ext, b_ext, S, widths)
        out_a = _swa(attn_sinks[layer], qa, ka2, va2, B, S)
        lam_vecs = jnp.stack([lambda_q1[layer], lambda_k1[layer], lambda_q2[layer], lambda_k2[layer]])
        out_b = _diffattn(lam_vecs, g_subln[layer].reshape(1, DIFF_V_DIM), qdt, kd, vdt, B, S, lambda_init)

        x1, h2, idx, gate, rank, counts = _outproj(
            out_a, out_b, xcur, mod3, w_out[layer].astype(BF16), b_out[layer].reshape(1, D),
            g_ffn[layer].reshape(1, D), w_router[layer].T, b_router[layer].reshape(N_EXPERTS, 1), S)

        dest, block_e, n_used, pad_rows, n_pad = _routing_tables(counts[:, 0], idx, rank, n_rows)
        x_rows = _dispatch(dest, pad_rows, n_pad, h2, n_rows)

        y_rows = _experts(block_e, n_used, x_rows, w1[layer],
                          b1[layer][:, None, 0::2], b1[layer][:, None, 1::2],
                          w2[layer], b2[layer][:, None, :])

        xcur = _combine(dest, x1, gate.T, mod3, g_final.reshape(1, D), y_rows, S, final_norm=last)
    return xcur.reshape(B, S, D)
```

```python
import functools
import math

import jax
import jax.numpy as jnp
from jax import lax
from jax.experimental import pallas as pl
from jax.experimental.pallas import tpu as pltpu

HEAD_DIM = 64
SWA_Q_HEADS = 8
SWA_KV_HEADS = 2
SWA_GROUP = SWA_Q_HEADS // SWA_KV_HEADS
WINDOW = 128
DIFF_HEADS = 4
DIFF_V_DIM = 2 * HEAD_DIM
ROPE_THETA = 10000.0
N_EXPERTS = 32
TOP_K = 4
SWIGLU_ALPHA = 1.702
SWIGLU_LIMIT = 7.0
EPS = 1e-5
N_MOD = 6

LANES = 128
LANE_SHIFT = LANES.bit_length() - 1
TOP_K_SHIFT = TOP_K.bit_length() - 1
assert 1 << LANE_SHIFT == LANES and 1 << TOP_K_SHIFT == TOP_K
F32 = jnp.float32
BF16 = jnp.bfloat16
NEG_INF = float("-inf")

TM_PROJ = 512
TQ_SWA = 512
TQ_DIFF = 512
VT_ROWS = DIFF_V_DIM + 16
TM_OUT = 256
TM_ROWS = 256
BLOCK_ROWS = 256
VMEM_LIMIT = 48 * 1024 * 1024


def _cparams(sem, vmem=VMEM_LIMIT):
    return pltpu.CompilerParams(dimension_semantics=sem, vmem_limit_bytes=vmem)


def _adaln_kernel(ct_ref, w_ref, b_ref, o_ref):
    c = ct_ref[...]
    cond = c * jax.nn.sigmoid(c)
    w = w_ref[...]
    rows = [jnp.sum(w * cond[:, b:b + 1], axis=0, keepdims=True) for b in range(c.shape[1])]
    o_ref[...] = jnp.concatenate(rows, axis=0) + b_ref[...]


def _adaln(c, w_ada, b_ada):
    B, D = c.shape
    n_out = w_ada.shape[1]
    tn = 1024
    return pl.pallas_call(
        _adaln_kernel,
        out_shape=jax.ShapeDtypeStruct((B, n_out), F32),
        grid=(n_out // tn,),
        in_specs=[pl.BlockSpec((D, B), lambda j: (0, 0)),
                  pl.BlockSpec((D, tn), lambda j: (0, j)),
                  pl.BlockSpec((1, tn), lambda j: (0, j))],
        out_specs=pl.BlockSpec((B, tn), lambda j: (0, j)),
        compiler_params=_cparams(("arbitrary",)),
        name="adaln",
    )(c.T, w_ada, b_ada.reshape(1, n_out))


def _rms(x):
    return x * lax.rsqrt(jnp.mean(x * x, axis=-1, keepdims=True) + EPS)


def _inproj_kernel(x_ref, pos_ref, inv_ref, mod_ref, g_ref, w_ref, b_ref,
                   qa_ref, ka_ref, va_ref, qd_ref, kd_ref, vd_ref):
    x = x_ref[...]
    sh = mod_ref[0, 0:1, :]
    sc = mod_ref[0, 1:2, :]
    h = _rms(x) * g_ref[...] * (1.0 + sc) + sh
    proj = jnp.dot(h.astype(BF16), w_ref[...], preferred_element_type=F32) + b_ref[...]

    ang = pos_ref[...].astype(F32) * inv_ref[...]
    lane = lax.broadcasted_iota(jnp.int32, (1, LANES), 1)
    first_half = (lane & (HEAD_DIM - 1)) < (HEAD_DIM // 2)
    cos = jnp.cos(ang)
    sin = jnp.sin(ang)
    sin_signed = jnp.where(first_half, -sin, sin)

    def rope(t):
        partner = jnp.where(first_half,
                            pltpu.roll(t, LANES - HEAD_DIM // 2, axis=1),
                            pltpu.roll(t, HEAD_DIM // 2, axis=1))
        return t * cos + partner * sin_signed

    def emit(out_ref, col0, width, rotary, scale, transposed):
        for j in range(width // LANES):
            t = proj[:, col0 + j * LANES: col0 + (j + 1) * LANES]
            if rotary:
                t = rope(t)
            if scale != 1.0:
                t = t * scale
            if transposed:
                rows = out_ref.shape[2] // (width // LANES)
                out_ref[0, 0, j * rows:j * rows + LANES, :] = t.T.astype(out_ref.dtype)
                if rows > LANES:
                    fill = lax.broadcasted_iota(jnp.int32, (rows - LANES, t.shape[0]), 0) == 0
                    out_ref[0, 0, j * rows + LANES:(j + 1) * rows, :] = fill.astype(out_ref.dtype)
            else:
                out_ref[:, j * LANES:(j + 1) * LANES] = t.astype(out_ref.dtype)

    swa_scale = 1.0 / math.sqrt(HEAD_DIM)
    diff_scale = math.log2(math.e) / math.sqrt(HEAD_DIM)
    col = 0
    for out_ref, width, rotary, scale, transposed in (
            (qa_ref, qa_ref.shape[1], True, swa_scale, False), (ka_ref, ka_ref.shape[1], True, 1.0, False),
            (va_ref, va_ref.shape[1], False, 1.0, False), (qd_ref, qd_ref.shape[2], True, diff_scale, True),
            (kd_ref, kd_ref.shape[1], True, 1.0, False),
            (vd_ref, vd_ref.shape[2] // VT_ROWS * LANES, False, 1.0, True)):
        emit(out_ref, col, width, rotary, scale, transposed)
        col += width


def _inproj(x2, pos2, inv_lane, mod3, g_mix, w_ext, b_ext, S, widths):
    N, D = x2.shape
    tm = TM_PROJ
    C = w_ext.shape[1]
    tiles_per_seq = S // tm
    row = lambda i: (i, 0)
    t_rows = (0, 0, 0, widths[3], 0, widths[5] // LANES * VT_ROWS)
    out_shape, out_specs = [], []
    for w, tr in zip(widths, t_rows):
        if tr:
            out_shape.append(jax.ShapeDtypeStruct((N // S, tiles_per_seq, tr, tm), BF16))
            out_specs.append(pl.BlockSpec((1, 1, tr, tm), lambda i: (i // tiles_per_seq, i % tiles_per_seq, 0, 0)))
        else:
            out_shape.append(jax.ShapeDtypeStruct((N, w), BF16))
            out_specs.append(pl.BlockSpec((tm, w), row))
    return pl.pallas_call(
        _inproj_kernel,
        out_shape=out_shape,
        grid=(N // tm,),
        in_specs=[pl.BlockSpec((tm, D), row),
                  pl.BlockSpec((tm, 1), row),
                  pl.BlockSpec((1, LANES), lambda i: (0, 0)),
                  pl.BlockSpec((1, N_MOD, D), lambda i: (i // tiles_per_seq, 0, 0)),
                  pl.BlockSpec((1, D), lambda i: (0, 0)),
                  pl.BlockSpec((D, C), lambda i: (0, 0)),
                  pl.BlockSpec((1, C), lambda i: (0, 0))],
        out_specs=out_specs,
        compiler_params=_cparams(("arbitrary",)),
        name="inproj",
    )(x2, pos2, inv_lane, mod3, g_mix, w_ext, b_ext)


def _swa_kernel(sink_ref, q_ref, kc_ref, kp_ref, vc_ref, vp_ref, o_ref):
    i = pl.program_id(1)
    tq = q_ref.shape[0]
    lane = lax.broadcasted_iota(jnp.int32, (1, LANES), 1)
    lo = lane < HEAD_DIM
    qi = lax.broadcasted_iota(jnp.int32, (WINDOW, 2 * WINDOW), 0) + WINDOW
    kj = lax.broadcasted_iota(jnp.int32, (WINDOW, 2 * WINDOW), 1)
    band = (qi - kj >= 0) & (qi - kj < WINDOW)
    dn = (((1,), (1,)), ((), ()))
    for c in range(tq // WINDOW):
        if c == 0:
            kcat = jnp.concatenate([kp_ref[...], kc_ref[0:WINDOW, :]], axis=0)
            vcat = jnp.concatenate([vp_ref[...], vc_ref[0:WINDOW, :]], axis=0)
            mask = band & (kj >= jnp.where(i > 0, 0, WINDOW))
        else:
            kcat = kc_ref[(c - 1) * WINDOW:(c + 1) * WINDOW, :]
            vcat = vc_ref[(c - 1) * WINDOW:(c + 1) * WINDOW, :]
            mask = band
        for j in range(SWA_KV_HEADS):
            kj2 = kcat[:, j * LANES:(j + 1) * LANES]
            vj2 = vcat[:, j * LANES:(j + 1) * LANES]
            zero = jnp.zeros_like(kj2)
            k_halves = (jnp.where(lo, kj2, zero), jnp.where(lo, zero, kj2))
            v_halves = (jnp.where(lo, vj2, zero), jnp.where(lo, zero, vj2))
            for p in range(SWA_GROUP // 2):
                g = j * (SWA_GROUP // 2) + p
                q = q_ref[c * WINDOW:(c + 1) * WINDOW, g * LANES:(g + 1) * LANES]
                out = jnp.zeros((WINDOW, LANES), F32)
                for half in range(2):
                    sink = sink_ref[2 * g + half]
                    s = lax.dot_general(q, k_halves[half], dn, preferred_element_type=F32)
                    s = jnp.where(mask, s, NEG_INF)
                    m = jnp.maximum(jnp.max(s, axis=1, keepdims=True), sink)
                    e = jnp.exp(s - m)
                    denom = jnp.sum(e, axis=1, keepdims=True) + jnp.exp(sink - m)
                    pv = jnp.dot(e.astype(BF16), v_halves[half], preferred_element_type=F32)
                    out = out + pv / denom
                o_ref[c * WINDOW:(c + 1) * WINDOW, g * LANES:(g + 1) * LANES] = out.astype(o_ref.dtype)


def _swa(sinks, qa, ka2, va2, B, S):
    N = qa.shape[0]
    tq = TQ_SWA
    nq = S // tq
    wpt = tq // WINDOW
    wps = S // WINDOW
    cur = lambda b, i: (b * nq + i, 0)
    prev = lambda b, i: (b * wps + jnp.maximum(i * wpt - 1, 0), 0)
    return pl.pallas_call(
        _swa_kernel,
        out_shape=jax.ShapeDtypeStruct((N, qa.shape[1]), BF16),
        grid=(B, nq),
        in_specs=[pl.BlockSpec(memory_space=pltpu.SMEM),
                  pl.BlockSpec((tq, qa.shape[1]), cur),
                  pl.BlockSpec((tq, ka2.shape[1]), cur),
                  pl.BlockSpec((WINDOW, ka2.shape[1]), prev),
                  pl.BlockSpec((tq, va2.shape[1]), cur),
                  pl.BlockSpec((WINDOW, va2.shape[1]), prev)],
        out_specs=pl.BlockSpec((tq, qa.shape[1]), cur),
        compiler_params=_cparams(("arbitrary", "arbitrary")),
        name="swa",
    )(sinks, qa, ka2, ka2, va2, va2)


def _diff_kernel(lam_ref, g_ref, qt_ref, k_ref, vt_ref, o_ref, sa_ref, sb_ref, m_ref, acc_ref, *, lambda_init):
    i = pl.program_id(2)
    tq = qt_ref.shape[3]
    tk = vt_ref.shape[3]
    qt = qt_ref[0, 0]
    lane = lax.broadcasted_iota(jnp.int32, (1, LANES), 1)
    lo = lane < HEAD_DIM
    m_ref[...] = jnp.full(m_ref.shape, NEG_INF, F32)
    acc_ref[...] = jnp.zeros(acc_ref.shape, F32)

    def scores(c, s_ref):
        k = k_ref[pl.ds(pl.multiple_of(c * tk, tk), tk), :]
        zero = jnp.zeros_like(k)
        s_ref[0] = jnp.dot(jnp.where(lo, k, zero), qt, preferred_element_type=F32)
        s_ref[1] = jnp.dot(jnp.where(lo, zero, k), qt, preferred_element_type=F32)

    def consume(c, s_ref, diagonal):
        vt = vt_ref[0, c]
        for mp in range(2):
            s = s_ref[mp]
            if diagonal:
                kpos = lax.broadcasted_iota(jnp.int32, (tk, tq), 0)
                qpos = lax.broadcasted_iota(jnp.int32, (tk, tq), 1)
                s = jnp.where(kpos <= qpos, s, NEG_INF)
            m_prev = m_ref[mp]
            m_new = jnp.maximum(m_prev, jnp.max(s, axis=0, keepdims=True))
            alpha = jnp.exp2(m_prev - m_new)
            p = jnp.exp2(s - m_new).astype(BF16)
            acc_ref[mp] = alpha * acc_ref[mp] + jnp.dot(vt, p, preferred_element_type=F32)
            m_ref[mp] = m_new

    scores(0, sa_ref)

    def pair(jj, carry):
        c = 2 * jj
        scores(c + 1, sb_ref)
        consume(c, sa_ref, False)
        scores(c + 2, sa_ref)
        consume(c + 1, sb_ref, False)
        return carry

    lax.fori_loop(0, lax.shift_right_logical(i, 1), pair, 0)

    @pl.when(i % 2 == 0)
    def _():
        consume(i, sa_ref, True)

    @pl.when(i % 2 == 1)
    def _():
        scores(i, sb_ref)
        consume(i - 1, sa_ref, False)
        consume(i, sb_ref, True)

    lq1, lk1, lq2, lk2 = (lam_ref[r:r + 1, :] for r in range(4))
    lam = (jnp.exp(jnp.sum(lq1 * lk1, axis=1, keepdims=True))
           - jnp.exp(jnp.sum(lq2 * lk2, axis=1, keepdims=True)) + lambda_init)
    d = DIFF_V_DIM
    ot = (acc_ref[0, 0:d, :] / acc_ref[0, d:d + 1, :]
          - lam * (acc_ref[1, 0:d, :] / acc_ref[1, d:d + 1, :]))
    ot = ot * lax.rsqrt(jnp.mean(ot * ot, axis=0, keepdims=True) + EPS)
    o_ref[...] = (ot.T * g_ref[...] * (1.0 - lambda_init)).astype(o_ref.dtype)


def _diffattn(lam_vecs, g_subln, qdt, kd, vdt, B, S, lambda_init):
    N, C = kd.shape
    tq = TQ_DIFF
    nq = S // tq
    return pl.pallas_call(
        functools.partial(_diff_kernel, lambda_init=lambda_init),
        out_shape=jax.ShapeDtypeStruct((N, C), BF16),
        grid=(B, DIFF_HEADS, nq),
        in_specs=[pl.BlockSpec((4, HEAD_DIM), lambda b, h, i: (0, 0)),
                  pl.BlockSpec((1, DIFF_V_DIM), lambda b, h, i: (0, 0)),
                  pl.BlockSpec((1, 1, LANES, tq), lambda b, h, i: (b, i, h, 0)),
                  pl.BlockSpec((S, LANES), lambda b, h, i: (b, h)),
                  pl.BlockSpec((1, nq, VT_ROWS, tq), lambda b, h, i: (b, 0, h, 0))],
        out_specs=pl.BlockSpec((tq, LANES), lambda b, h, i: (b * nq + i, h)),
        scratch_shapes=[pltpu.VMEM((2, tq, tq), F32), pltpu.VMEM((2, tq, tq), F32),
                        pltpu.VMEM((2, 1, tq), F32), pltpu.VMEM((2, VT_ROWS, tq), F32)],
        compiler_params=_cparams(("arbitrary", "arbitrary", "arbitrary")),
        name="diffattn",
    )(lam_vecs, g_subln, qdt, kd, vdt)


def _outproj_kernel(oa_ref, ob_ref, x_ref, mod_ref, wo_ref, bo_ref, g_ref, wr_ref, br_ref,
                    x1_ref, h2_ref, idx_ref, gate_ref, rank_ref, cnt_ref, carry_ref):
    i = pl.program_id(0)
    tm = x_ref.shape[0]
    half = oa_ref.shape[1]

    @pl.when(i == 0)
    def _():
        carry_ref[...] = jnp.zeros(carry_ref.shape, F32)

    gt1 = mod_ref[0, 2:3, :]
    sh2 = mod_ref[0, 3:4, :]
    sc2 = mod_ref[0, 4:5, :]
    mixed = (jnp.dot(oa_ref[...], wo_ref[0:half, :], preferred_element_type=F32)
             + jnp.dot(ob_ref[...], wo_ref[half:, :], preferred_element_type=F32) + bo_ref[...])
    x1 = x_ref[...] + gt1 * mixed
    x1_ref[...] = x1
    h2 = _rms(x1) * g_ref[...] * (1.0 + sc2) + sh2
    row_tiles = h2.shape[1] // LANES
    for s in range(row_tiles):
        h2_ref[pl.ds(s, tm, stride=row_tiles), :] = h2[:, s * LANES:(s + 1) * LANES]

    dn = (((1,), (1,)), ((), ()))
    h_hi = h2.astype(BF16)
    h_lo = (h2 - h_hi.astype(F32)).astype(BF16)
    w = wr_ref[...]
    w_hi = w.astype(BF16)
    w_lo = (w - w_hi.astype(F32)).astype(BF16)
    logits = (lax.dot_general(w_hi, h_hi, dn, preferred_element_type=F32)
              + lax.dot_general(w_hi, h_lo, dn, preferred_element_type=F32)
              + lax.dot_general(w_lo, h_hi, dn, preferred_element_type=F32)
              + br_ref[...])

    eidx = lax.broadcasted_iota(jnp.int32, logits.shape, 0)
    vals = logits
    onehots, top_vals, top_idx = [], [], []
    for _k in range(TOP_K):
        mx = jnp.max(vals, axis=0, keepdims=True)
        sel = jnp.min(jnp.where(vals == mx, eidx, N_EXPERTS), axis=0, keepdims=True)
        oh = eidx == sel
        onehots.append(oh)
        top_vals.append(mx)
        top_idx.append(sel)
        vals = jnp.where(oh, NEG_INF, vals)
    exps = [jnp.exp(v - top_vals[0]) for v in top_vals]
    denom = exps[0] + exps[1] + exps[2] + exps[3]
    gate_ref[...] = jnp.concatenate([e / denom for e in exps], axis=0)
    idx_ref[...] = jnp.concatenate(top_idx, axis=0)

    member = (onehots[0] | onehots[1] | onehots[2] | onehots[3])
    member_f = member.astype(F32)
    t_src = lax.broadcasted_iota(jnp.int32, (tm, tm), 0)
    t_dst = lax.broadcasted_iota(jnp.int32, (tm, tm), 1)
    before = (t_src < t_dst).astype(BF16)
    prefix = jnp.dot(member.astype(BF16), before, preferred_element_type=F32) + carry_ref[...]
    ranks = [jnp.sum(jnp.where(oh, prefix, 0.0), axis=0, keepdims=True) for oh in onehots]
    rank_ref[...] = jnp.concatenate(ranks, axis=0).astype(jnp.int32)
    carry_ref[...] = carry_ref[...] + jnp.sum(member_f, axis=1, keepdims=True)
    cnt_ref[...] = jnp.broadcast_to(carry_ref[...], cnt_ref.shape)


def _outproj(out_a, out_b, x2, mod3, w_out, b_out, g_ffn, wr_t, br_col, S):
    N, D = x2.shape
    tm = TM_OUT
    tiles_per_seq = S // tm
    row = lambda i: (i, 0)
    colb = lambda i: (0, i)
    const = lambda i: (0, 0)
    return pl.pallas_call(
        _outproj_kernel,
        out_shape=[jax.ShapeDtypeStruct((N, D), F32), jax.ShapeDtypeStruct((N * D // LANES, LANES), F32),
                   jax.ShapeDtypeStruct((TOP_K, N), jnp.int32), jax.ShapeDtypeStruct((TOP_K, N), F32),
                   jax.ShapeDtypeStruct((TOP_K, N), jnp.int32), jax.ShapeDtypeStruct((N_EXPERTS, LANES), F32)],
        grid=(N // tm,),
        in_specs=[pl.BlockSpec((tm, out_a.shape[1]), row),
                  pl.BlockSpec((tm, out_b.shape[1]), row),
                  pl.BlockSpec((tm, D), row),
                  pl.BlockSpec((1, N_MOD, D), lambda i: (i // tiles_per_seq, 0, 0)),
                  pl.BlockSpec(w_out.shape, const),
                  pl.BlockSpec((1, D), const),
                  pl.BlockSpec((1, D), const),
                  pl.BlockSpec(wr_t.shape, const),
                  pl.BlockSpec((N_EXPERTS, 1), const)],
        out_specs=[pl.BlockSpec((tm, D), row), pl.BlockSpec((tm * D // LANES, LANES), row),
                   pl.BlockSpec((TOP_K, tm), colb), pl.BlockSpec((TOP_K, tm), colb),
                   pl.BlockSpec((TOP_K, tm), colb), pl.BlockSpec((N_EXPERTS, LANES), const)],
        scratch_shapes=[pltpu.VMEM((N_EXPERTS, 1), F32)],
        compiler_params=_cparams(("arbitrary",)),
        name="outproj_router",
    )(out_a, out_b, x2, mod3, w_out, b_out, g_ffn, wr_t, br_col)


def _slotmap_kernel(dest_ref, pad_ref, slot_ref, *, n_tokens, steps):
    j = pl.program_id(0)
    per_step = n_tokens // steps
    pad_per_step = pad_ref.shape[0] // steps

    def mark(p, carry):
        slot_ref[pad_ref[j * pad_per_step + p]] = jnp.int32(-1)
        return carry

    lax.fori_loop(0, pad_per_step, mark, 0, unroll=16)

    for k in range(TOP_K):
        def place(u, carry):
            t = j * per_step + u
            slot_ref[dest_ref[k * n_tokens + t]] = t * TOP_K + k
            return carry

        lax.fori_loop(0, per_step, place, 0, unroll=16)


def _slotmap(dest, pad_rows, n_rows, n_tokens):
    steps = 64
    assert n_tokens % steps == 0 and pad_rows.shape[0] % steps == 0
    return pl.pallas_call(
        functools.partial(_slotmap_kernel, n_tokens=n_tokens, steps=steps),
        out_shape=jax.ShapeDtypeStruct((n_rows,), jnp.int32),
        grid=(steps,),
        in_specs=[pl.BlockSpec(memory_space=pltpu.SMEM), pl.BlockSpec(memory_space=pltpu.SMEM)],
        out_specs=pl.BlockSpec(memory_space=pltpu.SMEM),
        compiler_params=pltpu.CompilerParams(dimension_semantics=("arbitrary",)),
        name="slotmap",
    )(dest, pad_rows).reshape(n_rows // LANES, LANES)


def _experts_kernel(slot_ref, be_ref, nused_ref, h_ref, w1_ref, b1g_ref, b1l_ref, w2_ref, b2_ref, y_ref,
                    w1g_s, w1l_s, w2_s, t_s, xbuf0, xbuf1, ybuf0, ybuf1, gsem, ssem, *, n_slots):
    i = pl.program_id(0)
    n_used = nused_ref[0]
    used = i < n_used
    new_expert = (i == 0) | (be_ref[i] != be_ref[jnp.maximum(i - 1, 0)])
    rt = w2_s.shape[1] // LANES
    xbufs, ybufs = (xbuf0, xbuf1), (ybuf0, ybuf1)
    bm = xbuf0.shape[0] // rt
    groups = bm // LANES

    def tile(ref, row):
        return ref.at[pl.ds(pl.multiple_of(row * rt, rt), rt)]

    def gather(blk, buf):
        for r in range(bm):
            s = slot_ref[blk * groups + r // LANES, r % LANES]
            tok = lax.shift_right_logical(jnp.maximum(s, 0), TOP_K_SHIFT)
            pltpu.make_async_copy(tile(h_ref, tok), xbufs[buf].at[pl.ds(r * rt, rt)], gsem).start(priority=0)

    def scatter(blk, buf, valid):
        for r in range(bm):
            s = slot_ref[blk * groups + r // LANES, r % LANES]
            spare = n_slots + buf * bm + r
            dst = jnp.where(valid & (s >= 0), s, spare)
            pltpu.make_async_copy(ybufs[buf].at[pl.ds(r * rt, rt)], tile(y_ref, dst), ssem).start(priority=1)

    def wait_gather(buf):
        pltpu.make_async_copy(h_ref.at[pl.ds(0, bm * rt)], xbufs[buf], gsem).wait()

    def wait_scatter(buf):
        pltpu.make_async_copy(ybufs[buf], y_ref.at[pl.ds(0, bm * rt)], ssem).wait()

    @pl.when(i == 0)
    def _():
        ybuf1[...] = jnp.zeros(ybuf1.shape, ybuf1.dtype)
        fill = pltpu.make_async_copy(ybuf1, y_ref.at[pl.ds(n_slots * rt, bm * rt)], ssem)
        fill.start()
        fill.wait()
        gather(0, 0)

    @pl.when(used & new_expert)
    def _():
        n_db, ch, _ = t_s.shape
        half = ch // 2
        for c in range(w1_ref.shape[2] // ch):
            t = w1_ref[0, :, c * ch:(c + 1) * ch].T
            for db in range(n_db):
                t_s[db] = t[:, db * LANES:(db + 1) * LANES]
            for db in range(n_db):
                rows = slice(c * half, (c + 1) * half)
                cols = slice(db * LANES, (db + 1) * LANES)
                w1g_s[rows, cols] = t_s[db, pl.ds(0, half, stride=2), :].astype(BF16)
                w1l_s[rows, cols] = t_s[db, pl.ds(1, half, stride=2), :].astype(BF16)
        w2_s[...] = w2_ref[0].astype(BF16)

    def block(cur):
        nxt = 1 - cur
        wait_gather(cur)

        @pl.when(i >= 1)
        def _():
            wait_scatter(cur)

        gather(jnp.minimum(i + 1, n_used - 1), nxt)
        scatter(jnp.maximum(i - 1, 0), nxt, i >= 1)
        dn = (((1,), (1,)), ((), ()))
        x = jnp.concatenate([xbufs[cur][pl.ds(s, bm, stride=rt), :] for s in range(rt)], axis=1).astype(BF16)
        ug = lax.dot_general(x, w1g_s[...], dn, preferred_element_type=F32) + b1g_ref[0]
        ul = lax.dot_general(x, w1l_s[...], dn, preferred_element_type=F32) + b1l_ref[0]
        glu = jnp.minimum(ug, SWIGLU_LIMIT)
        lin = jnp.clip(ul, -SWIGLU_LIMIT, SWIGLU_LIMIT)
        act = glu * jax.nn.sigmoid(SWIGLU_ALPHA * glu) * (lin + 1.0)
        y = jnp.dot(act.astype(BF16), w2_s[...], preferred_element_type=F32) + b2_ref[0]
        for s in range(rt):
            ybufs[cur][pl.ds(s, bm, stride=rt), :] = y[:, s * LANES:(s + 1) * LANES]

        @pl.when(i == n_used - 1)
        def _():
            scatter(i, cur, True)
            wait_gather(nxt)
            wait_scatter(nxt)
            wait_scatter(cur)

    @pl.when(used & (i % 2 == 0))
    def _():
        block(0)

    @pl.when(used & (i % 2 == 1))
    def _():
        block(1)


def _experts(slot2d, block_e, n_used, h2_tiles, w1, b1g, b1l, w2, b2):
    E, Fh, D = w2.shape
    rt = D // LANES
    N = h2_tiles.shape[0] // rt
    bm = BLOCK_ROWS
    n_blocks = slot2d.shape[0] * LANES // bm
    n_slots = N * TOP_K
    chunk = 256
    wsel = lambda i, sl, be, nu: (be[i], 0, 0)
    return pl.pallas_call(
        functools.partial(_experts_kernel, n_slots=n_slots),
        out_shape=jax.ShapeDtypeStruct(((n_slots + 2 * bm) * rt, LANES), F32),
        grid_spec=pltpu.PrefetchScalarGridSpec(
            num_scalar_prefetch=3,
            grid=(n_blocks,),
            in_specs=[pl.BlockSpec(memory_space=pl.ANY),
                      pl.BlockSpec((1, D, 2 * Fh), wsel),
                      pl.BlockSpec((1, 1, Fh), wsel),
                      pl.BlockSpec((1, 1, Fh), wsel),
                      pl.BlockSpec((1, Fh, D), wsel),
                      pl.BlockSpec((1, 1, D), wsel)],
            out_specs=pl.BlockSpec(memory_space=pl.ANY),
            scratch_shapes=[pltpu.VMEM((Fh, D), BF16), pltpu.VMEM((Fh, D), BF16), pltpu.VMEM((Fh, D), BF16),
                            pltpu.VMEM((D // LANES, chunk, LANES), F32),
                            pltpu.VMEM((bm * rt, LANES), F32), pltpu.VMEM((bm * rt, LANES), F32),
                            pltpu.VMEM((bm * rt, LANES), F32), pltpu.VMEM((bm * rt, LANES), F32),
                            pltpu.SemaphoreType.DMA, pltpu.SemaphoreType.DMA]),
        compiler_params=_cparams(("arbitrary",), 56 * 1024 * 1024),
        name="experts",
    )(slot2d, block_e, n_used, h2_tiles, w1, b1g, b1l, w2, b2)


def _combine_kernel(x1_ref, gate_ref, mod_ref, g_ref, y_ref, o_ref, *, final_norm):
    tm, D = x1_ref.shape
    rt = D // LANES
    gate = gate_ref[...]
    cols = []
    for s in range(rt):
        col = gate[:, 0:1] * y_ref[pl.ds(s, tm, stride=TOP_K * rt), :]
        for k in range(1, TOP_K):
            col = col + gate[:, k:k + 1] * y_ref[pl.ds(k * rt + s, tm, stride=TOP_K * rt), :]
        cols.append(col)
    moe = jnp.concatenate(cols, axis=1)
    gt2 = mod_ref[0, 5:6, :]
    x2 = x1_ref[...] + gt2 * moe
    o_ref[...] = _rms(x2) * g_ref[...] if final_norm else x2


def _combine(x1, gate_t, mod3, g_final, y_tok, S, final_norm):
    N, D = x1.shape
    tm = TM_ROWS
    tiles_per_seq = S // tm
    row = lambda i: (i, 0)
    return pl.pallas_call(
        functools.partial(_combine_kernel, final_norm=final_norm),
        out_shape=jax.ShapeDtypeStruct((N, D), F32),
        grid=(N // tm,),
        in_specs=[pl.BlockSpec((tm, D), row),
                  pl.BlockSpec((tm, TOP_K), row),
                  pl.BlockSpec((1, N_MOD, D), lambda i: (i // tiles_per_seq, 0, 0)),
                  pl.BlockSpec((1, D), lambda i: (0, 0)),
                  pl.BlockSpec((tm * TOP_K * D // LANES, LANES), row)],
        out_specs=pl.BlockSpec((tm, D), row),
        compiler_params=_cparams(("arbitrary",)),
        name="combine",
    )(x1, gate_t, mod3, g_final, y_tok)


def _routing_tables(counts, idx, rank, n_rows):
    bm = BLOCK_ROWS
    counts = counts.astype(jnp.int32)
    padded = (counts + bm - 1) // bm * bm
    pends = jnp.cumsum(padded)
    pstarts = pends - padded
    experts = jnp.arange(N_EXPERTS, dtype=jnp.int32)
    dest = jnp.sum(jnp.where(idx[..., None] == experts, pstarts, 0), axis=-1) + rank
    n_blocks = n_rows // bm
    block_start = jnp.arange(n_blocks, dtype=jnp.int32) * bm
    block_e = jnp.minimum(jnp.sum(pends <= block_start[:, None], axis=-1), N_EXPERTS - 1).astype(jnp.int32)
    n_used = (pends[-1] // bm).astype(jnp.int32).reshape(1)
    seg_start = jnp.concatenate([pstarts + counts, pends[-1:]])
    seg_len = jnp.concatenate([padded - counts, n_rows - pends[-1:]])
    seg_end = jnp.cumsum(seg_len)
    j = jnp.arange(n_rows - idx.size, dtype=jnp.int32)
    in_seg = (j[:, None] >= seg_end - seg_len) & (j[:, None] < seg_end)
    pad_rows = jnp.sum(jnp.where(in_seg, seg_start - (seg_end - seg_len) + j[:, None], 0), axis=-1)
    return dest.astype(jnp.int32).reshape(-1), pad_rows.astype(jnp.int32), block_e, n_used


def _extended_in_weights(w_in, b_in):
    a_q = SWA_Q_HEADS * HEAD_DIM
    a_kv = SWA_KV_HEADS * HEAD_DIM
    b_w = DIFF_HEADS * DIFF_V_DIM
    spans = [(0, a_q)]
    for base in (a_q, a_q + a_kv):
        for j in range(SWA_KV_HEADS):
            spans += [(base + j * HEAD_DIM, base + (j + 1) * HEAD_DIM)] * 2
    spans.append((a_q + 2 * a_kv, a_q + 2 * a_kv + 3 * b_w))
    w_ext = jnp.concatenate([w_in[:, lo:hi] for lo, hi in spans], axis=1).astype(BF16)
    b_ext = jnp.concatenate([b_in[lo:hi] for lo, hi in spans]).reshape(1, -1)
    widths = (a_q, 2 * a_kv, 2 * a_kv, b_w, b_w, b_w)
    return w_ext, b_ext, widths


def kernel(x, c, positions, w_ada, b_ada, g_mix, w_in, b_in, attn_sinks, lambda_q1, lambda_k1, lambda_q2,
           lambda_k2, g_subln, w_out, b_out, g_ffn, w_router, b_router, w1, b1, w2, b2, g_final):
    B, S, D = x.shape
    N = B * S
    depth = w_ada.shape[0]
    assert TM_PROJ == TQ_DIFF, "the input projection writes q/v transposed per attention tile"
    n_rows = (N * TOP_K + N_EXPERTS * (BLOCK_ROWS - 1) + BLOCK_ROWS - 1) // BLOCK_ROWS * BLOCK_ROWS

    inv = 1.0 / (ROPE_THETA ** (jnp.arange(0, HEAD_DIM, 2, dtype=F32) / HEAD_DIM))
    inv_lane = jnp.tile(inv, LANES // (HEAD_DIM // 2)).reshape(1, LANES)
    pos2 = positions.reshape(N, 1)
    xcur = x.reshape(N, D)

    for layer in range(depth):
        last = layer == depth - 1
        lambda_init = 0.8 - 0.6 * math.exp(-0.3 * layer)
        mod3 = _adaln(c, w_ada[layer], b_ada[layer]).reshape(B, N_MOD, D)

        w_ext, b_ext, widths = _extended_in_weights(w_in[layer], b_in[layer])
        qa, ka2, va2, qdt, kd, vdt = _inproj(xcur, pos2, inv_lane, mod3, g_mix[layer].reshape(1, D),
                                             w_ext, b_ext, S, widths)
        out_a = _swa(attn_sinks[layer], qa, ka2, va2, B, S)
        lam_vecs = jnp.stack([lambda_q1[layer], lambda_k1[layer], lambda_q2[layer], lambda_k2[layer]])
        out_b = _diffattn(lam_vecs, g_subln[layer].reshape(1, DIFF_V_DIM), qdt, kd, vdt, B, S, lambda_init)

        x1, h2, idx, gate, rank, counts = _outproj(
            out_a, out_b, xcur, mod3, w_out[layer].astype(BF16), b_out[layer].reshape(1, D),
            g_ffn[layer].reshape(1, D), w_router[layer].T, b_router[layer].reshape(N_EXPERTS, 1), S)

        dest, pad_rows, block_e, n_used = _routing_tables(counts[:, 0], idx, rank, n_rows)
        slot2d = _slotmap(dest, pad_rows, n_rows, N)

        y_slots = _experts(slot2d, block_e, n_used, h2, w1[layer],
                           b1[layer][:, None, 0::2], b1[layer][:, None, 1::2],
                           w2[layer], b2[layer][:, None, :])
        xcur = _combine(x1, gate.T, mod3, g_final.reshape(1, D), y_slots, S, final_norm=last)
    return xcur.reshape(B, S, D)
```

```python
import functools
import math

import jax
import jax.numpy as jnp
from jax import lax
from jax.experimental import pallas as pl
from jax.experimental.pallas import tpu as pltpu

HEAD_DIM = 64
SWA_Q_HEADS = 8
SWA_KV_HEADS = 2
SWA_GROUP = SWA_Q_HEADS // SWA_KV_HEADS
WINDOW = 128
DIFF_HEADS = 4
DIFF_V_DIM = 2 * HEAD_DIM
ROPE_THETA = 10000.0
N_EXPERTS = 32
TOP_K = 4
SWIGLU_ALPHA = 1.702
SWIGLU_LIMIT = 7.0
EPS = 1e-5
N_MOD = 6

LANES = 128
LANE_SHIFT = LANES.bit_length() - 1
TOP_K_SHIFT = TOP_K.bit_length() - 1
assert 1 << LANE_SHIFT == LANES and 1 << TOP_K_SHIFT == TOP_K
F32 = jnp.float32
BF16 = jnp.bfloat16
NEG_INF = float("-inf")

TM_PROJ = 512
TQ_SWA = 512
TQ_DIFF = 512
VT_ROWS = DIFF_V_DIM + 16
TM_OUT = 256
TM_ROWS = 256
BLOCK_ROWS = 256
VMEM_LIMIT = 48 * 1024 * 1024


def _cparams(sem, vmem=VMEM_LIMIT):
    return pltpu.CompilerParams(dimension_semantics=sem, vmem_limit_bytes=vmem)


def _adaln_kernel(ct_ref, w_ref, b_ref, o_ref):
    c = ct_ref[...]
    cond = c * jax.nn.sigmoid(c)
    w = w_ref[...]
    rows = [jnp.sum(w * cond[:, b:b + 1], axis=0, keepdims=True) for b in range(c.shape[1])]
    o_ref[...] = jnp.concatenate(rows, axis=0) + b_ref[...]


def _adaln(c, w_ada, b_ada):
    B, D = c.shape
    n_out = w_ada.shape[1]
    tn = 1024
    return pl.pallas_call(
        _adaln_kernel,
        out_shape=jax.ShapeDtypeStruct((B, n_out), F32),
        grid=(n_out // tn,),
        in_specs=[pl.BlockSpec((D, B), lambda j: (0, 0)),
                  pl.BlockSpec((D, tn), lambda j: (0, j)),
                  pl.BlockSpec((1, tn), lambda j: (0, j))],
        out_specs=pl.BlockSpec((B, tn), lambda j: (0, j)),
        compiler_params=_cparams(("arbitrary",)),
        name="adaln",
    )(c.T, w_ada, b_ada.reshape(1, n_out))


def _rms(x):
    return x * lax.rsqrt(jnp.mean(x * x, axis=-1, keepdims=True) + EPS)


def _inproj_kernel(x_ref, pos_ref, inv_ref, mod_ref, g_ref, w_ref, b_ref,
                   qa_ref, ka_ref, va_ref, qd_ref, kd_ref, vd_ref):
    x = x_ref[...]
    sh = mod_ref[0, 0:1, :]
    sc = mod_ref[0, 1:2, :]
    h = _rms(x) * g_ref[...] * (1.0 + sc) + sh
    proj = jnp.dot(h.astype(BF16), w_ref[...], preferred_element_type=F32) + b_ref[...]

    ang = pos_ref[...].astype(F32) * inv_ref[...]
    lane = lax.broadcasted_iota(jnp.int32, (1, LANES), 1)
    first_half = (lane & (HEAD_DIM - 1)) < (HEAD_DIM // 2)
    cos = jnp.cos(ang)
    sin = jnp.sin(ang)
    sin_signed = jnp.where(first_half, -sin, sin)

    def rope(t):
        partner = jnp.where(first_half,
                            pltpu.roll(t, LANES - HEAD_DIM // 2, axis=1),
                            pltpu.roll(t, HEAD_DIM // 2, axis=1))
        return t * cos + partner * sin_signed

    def emit(out_ref, col0, width, rotary, scale, transposed):
        for j in range(width // LANES):
            t = proj[:, col0 + j * LANES: col0 + (j + 1) * LANES]
            if rotary:
                t = rope(t)
            if scale != 1.0:
                t = t * scale
            if transposed:
                rows = out_ref.shape[2] // (width // LANES)
                out_ref[0, 0, j * rows:j * rows + LANES, :] = t.T.astype(out_ref.dtype)
                if rows > LANES:
                    fill = lax.broadcasted_iota(jnp.int32, (rows - LANES, t.shape[0]), 0) == 0
                    out_ref[0, 0, j * rows + LANES:(j + 1) * rows, :] = fill.astype(out_ref.dtype)
            else:
                out_ref[:, j * LANES:(j + 1) * LANES] = t.astype(out_ref.dtype)

    swa_scale = 1.0 / math.sqrt(HEAD_DIM)
    diff_scale = math.log2(math.e) / math.sqrt(HEAD_DIM)
    col = 0
    for out_ref, width, rotary, scale, transposed in (
            (qa_ref, qa_ref.shape[1], True, swa_scale, False), (ka_ref, ka_ref.shape[1], True, 1.0, False),
            (va_ref, va_ref.shape[1], False, 1.0, False), (qd_ref, qd_ref.shape[2], True, diff_scale, True),
            (kd_ref, kd_ref.shape[1], True, 1.0, False),
            (vd_ref, vd_ref.shape[2] // VT_ROWS * LANES, False, 1.0, True)):
        emit(out_ref, col, width, rotary, scale, transposed)
        col += width


def _inproj(x2, pos2, inv_lane, mod3, g_mix, w_ext, b_ext, S, widths):
    N, D = x2.shape
    tm = TM_PROJ
    C = w_ext.shape[1]
    tiles_per_seq = S // tm
    row = lambda i: (i, 0)
    t_rows = (0, 0, 0, widths[3], 0, widths[5] // LANES * VT_ROWS)
    out_shape, out_specs = [], []
    for w, tr in zip(widths, t_rows):
        if tr:
            out_shape.append(jax.ShapeDtypeStruct((N // S, tiles_per_seq, tr, tm), BF16))
            out_specs.append(pl.BlockSpec((1, 1, tr, tm), lambda i: (i // tiles_per_seq, i % tiles_per_seq, 0, 0)))
        else:
            out_shape.append(jax.ShapeDtypeStruct((N, w), BF16))
            out_specs.append(pl.BlockSpec((tm, w), row))
    return pl.pallas_call(
        _inproj_kernel,
        out_shape=out_shape,
        grid=(N // tm,),
        in_specs=[pl.BlockSpec((tm, D), row),
                  pl.BlockSpec((tm, 1), row),
                  pl.BlockSpec((1, LANES), lambda i: (0, 0)),
                  pl.BlockSpec((1, N_MOD, D), lambda i: (i // tiles_per_seq, 0, 0)),
                  pl.BlockSpec((1, D), lambda i: (0, 0)),
                  pl.BlockSpec((D, C), lambda i: (0, 0)),
                  pl.BlockSpec((1, C), lambda i: (0, 0))],
        out_specs=out_specs,
        compiler_params=_cparams(("arbitrary",)),
        name="inproj",
    )(x2, pos2, inv_lane, mod3, g_mix, w_ext, b_ext)


def _swa_kernel(sink_ref, q_ref, kc_ref, kp_ref, vc_ref, vp_ref, o_ref):
    i = pl.program_id(1)
    tq = q_ref.shape[0]
    lane = lax.broadcasted_iota(jnp.int32, (1, LANES), 1)
    lo = lane < HEAD_DIM
    qi = lax.broadcasted_iota(jnp.int32, (WINDOW, 2 * WINDOW), 0) + WINDOW
    kj = lax.broadcasted_iota(jnp.int32, (WINDOW, 2 * WINDOW), 1)
    band = (qi - kj >= 0) & (qi - kj < WINDOW)
    dn = (((1,), (1,)), ((), ()))
    for c in range(tq // WINDOW):
        if c == 0:
            kcat = jnp.concatenate([kp_ref[...], kc_ref[0:WINDOW, :]], axis=0)
            vcat = jnp.concatenate([vp_ref[...], vc_ref[0:WINDOW, :]], axis=0)
            mask = band & (kj >= jnp.where(i > 0, 0, WINDOW))
        else:
            kcat = kc_ref[(c - 1) * WINDOW:(c + 1) * WINDOW, :]
            vcat = vc_ref[(c - 1) * WINDOW:(c + 1) * WINDOW, :]
            mask = band
        for j in range(SWA_KV_HEADS):
            kj2 = kcat[:, j * LANES:(j + 1) * LANES]
            vj2 = vcat[:, j * LANES:(j + 1) * LANES]
            zero = jnp.zeros_like(kj2)
            k_halves = (jnp.where(lo, kj2, zero), jnp.where(lo, zero, kj2))
            v_halves = (jnp.where(lo, vj2, zero), jnp.where(lo, zero, vj2))
            for p in range(SWA_GROUP // 2):
                g = j * (SWA_GROUP // 2) + p
                q = q_ref[c * WINDOW:(c + 1) * WINDOW, g * LANES:(g + 1) * LANES]
                out = jnp.zeros((WINDOW, LANES), F32)
                for half in range(2):
                    sink = sink_ref[2 * g + half]
                    s = lax.dot_general(q, k_halves[half], dn, preferred_element_type=F32)
                    s = jnp.where(mask, s, NEG_INF)
                    m = jnp.maximum(jnp.max(s, axis=1, keepdims=True), sink)
                    e = jnp.exp(s - m)
                    denom = jnp.sum(e, axis=1, keepdims=True) + jnp.exp(sink - m)
                    pv = jnp.dot(e.astype(BF16), v_halves[half], preferred_element_type=F32)
                    out = out + pv / denom
                o_ref[c * WINDOW:(c + 1) * WINDOW, g * LANES:(g + 1) * LANES] = out.astype(o_ref.dtype)


def _swa(sinks, qa, ka2, va2, B, S):
    N = qa.shape[0]
    tq = TQ_SWA
    nq = S // tq
    wpt = tq // WINDOW
    wps = S // WINDOW
    cur = lambda b, i: (b * nq + i, 0)
    prev = lambda b, i: (b * wps + jnp.maximum(i * wpt - 1, 0), 0)
    return pl.pallas_call(
        _swa_kernel,
        out_shape=jax.ShapeDtypeStruct((N, qa.shape[1]), BF16),
        grid=(B, nq),
        in_specs=[pl.BlockSpec(memory_space=pltpu.SMEM),
                  pl.BlockSpec((tq, qa.shape[1]), cur),
                  pl.BlockSpec((tq, ka2.shape[1]), cur),
                  pl.BlockSpec((WINDOW, ka2.shape[1]), prev),
                  pl.BlockSpec((tq, va2.shape[1]), cur),
                  pl.BlockSpec((WINDOW, va2.shape[1]), prev)],
        out_specs=pl.BlockSpec((tq, qa.shape[1]), cur),
        compiler_params=_cparams(("arbitrary", "arbitrary")),
        name="swa",
    )(sinks, qa, ka2, ka2, va2, va2)


def _diff_kernel(lam_ref, g_ref, qt_ref, k_ref, vt_ref, o_ref, sa_ref, sb_ref, m_ref, acc_ref, *, lambda_init):
    i = pl.program_id(2)
    tq = qt_ref.shape[3]
    tk = vt_ref.shape[3]
    qt = qt_ref[0, 0]
    lane = lax.broadcasted_iota(jnp.int32, (1, LANES), 1)
    lo = lane < HEAD_DIM
    m_ref[...] = jnp.full(m_ref.shape, NEG_INF, F32)
    acc_ref[...] = jnp.zeros(acc_ref.shape, F32)

    def scores(c, s_ref):
        k = k_ref[pl.ds(pl.multiple_of(c * tk, tk), tk), :]
        zero = jnp.zeros_like(k)
        s_ref[0] = jnp.dot(jnp.where(lo, k, zero), qt, preferred_element_type=F32)
        s_ref[1] = jnp.dot(jnp.where(lo, zero, k), qt, preferred_element_type=F32)

    def consume(c, s_ref, diagonal):
        vt = vt_ref[0, c]
        for mp in range(2):
            s = s_ref[mp]
            if diagonal:
                kpos = lax.broadcasted_iota(jnp.int32, (tk, tq), 0)
                qpos = lax.broadcasted_iota(jnp.int32, (tk, tq), 1)
                s = jnp.where(kpos <= qpos, s, NEG_INF)
            m_prev = m_ref[mp]
            m_new = jnp.maximum(m_prev, jnp.max(s, axis=0, keepdims=True))
            alpha = jnp.exp2(m_prev - m_new)
            p = jnp.exp2(s - m_new).astype(BF16)
            acc_ref[mp] = alpha * acc_ref[mp] + jnp.dot(vt, p, preferred_element_type=F32)
            m_ref[mp] = m_new

    scores(0, sa_ref)

    def pair(jj, carry):
        c = 2 * jj
        scores(c + 1, sb_ref)
        consume(c, sa_ref, False)
        scores(c + 2, sa_ref)
        consume(c + 1, sb_ref, False)
        return carry

    lax.fori_loop(0, lax.shift_right_logical(i, 1), pair, 0)

    @pl.when(i % 2 == 0)
    def _():
        consume(i, sa_ref, True)

    @pl.when(i % 2 == 1)
    def _():
        scores(i, sb_ref)
        consume(i - 1, sa_ref, False)
        consume(i, sb_ref, True)

    lq1, lk1, lq2, lk2 = (lam_ref[r:r + 1, :] for r in range(4))
    lam = (jnp.exp(jnp.sum(lq1 * lk1, axis=1, keepdims=True))
           - jnp.exp(jnp.sum(lq2 * lk2, axis=1, keepdims=True)) + lambda_init)
    d = DIFF_V_DIM
    ot = (acc_ref[0, 0:d, :] / acc_ref[0, d:d + 1, :]
          - lam * (acc_ref[1, 0:d, :] / acc_ref[1, d:d + 1, :]))
    ot = ot * lax.rsqrt(jnp.mean(ot * ot, axis=0, keepdims=True) + EPS)
    o_ref[...] = (ot.T * g_ref[...] * (1.0 - lambda_init)).astype(o_ref.dtype)


def _diffattn(lam_vecs, g_subln, qdt, kd, vdt, B, S, lambda_init):
    N, C = kd.shape
    tq = TQ_DIFF
    nq = S // tq
    return pl.pallas_call(
        functools.partial(_diff_kernel, lambda_init=lambda_init),
        out_shape=jax.ShapeDtypeStruct((N, C), BF16),
        grid=(B, DIFF_HEADS, nq),
        in_specs=[pl.BlockSpec((4, HEAD_DIM), lambda b, h, i: (0, 0)),
                  pl.BlockSpec((1, DIFF_V_DIM), lambda b, h, i: (0, 0)),
                  pl.BlockSpec((1, 1, LANES, tq), lambda b, h, i: (b, i, h, 0)),
                  pl.BlockSpec((S, LANES), lambda b, h, i: (b, h)),
                  pl.BlockSpec((1, nq, VT_ROWS, tq), lambda b, h, i: (b, 0, h, 0))],
        out_specs=pl.BlockSpec((tq, LANES), lambda b, h, i: (b * nq + i, h)),
        scratch_shapes=[pltpu.VMEM((2, tq, tq), F32), pltpu.VMEM((2, tq, tq), F32),
                        pltpu.VMEM((2, 1, tq), F32), pltpu.VMEM((2, VT_ROWS, tq), F32)],
        compiler_params=_cparams(("arbitrary", "arbitrary", "arbitrary")),
        name="diffattn",
    )(lam_vecs, g_subln, qdt, kd, vdt)


def _outproj_kernel(oa_ref, ob_ref, x_ref, mod_ref, wo_ref, bo_ref, g_ref, wr_ref, br_ref,
                    x1_ref, h2_ref, idx_ref, gate_ref, rank_ref, cnt_ref, carry_ref):
    i = pl.program_id(0)
    tm = x_ref.shape[0]
    half = oa_ref.shape[1]

    @pl.when(i == 0)
    def _():
        carry_ref[...] = jnp.zeros(carry_ref.shape, F32)

    gt1 = mod_ref[0, 2:3, :]
    sh2 = mod_ref[0, 3:4, :]
    sc2 = mod_ref[0, 4:5, :]
    mixed = (jnp.dot(oa_ref[...], wo_ref[0:half, :], preferred_element_type=F32)
             + jnp.dot(ob_ref[...], wo_ref[half:, :], preferred_element_type=F32) + bo_ref[...])
    x1 = x_ref[...] + gt1 * mixed
    x1_ref[...] = x1
    h2 = _rms(x1) * g_ref[...] * (1.0 + sc2) + sh2
    row_tiles = h2.shape[1] // LANES
    for s in range(row_tiles):
        h2_ref[pl.ds(s, tm, stride=row_tiles), :] = h2[:, s * LANES:(s + 1) * LANES]

    dn = (((1,), (1,)), ((), ()))
    h_hi = h2.astype(BF16)
    h_lo = (h2 - h_hi.astype(F32)).astype(BF16)
    w = wr_ref[...]
    w_hi = w.astype(BF16)
    w_lo = (w - w_hi.astype(F32)).astype(BF16)
    logits = (lax.dot_general(w_hi, h_hi, dn, preferred_element_type=F32)
              + lax.dot_general(w_hi, h_lo, dn, preferred_element_type=F32)
              + lax.dot_general(w_lo, h_hi, dn, preferred_element_type=F32)
              + br_ref[...])

    eidx = lax.broadcasted_iota(jnp.int32, logits.shape, 0)
    vals = logits
    onehots, top_vals, top_idx = [], [], []
    for _k in range(TOP_K):
        mx = jnp.max(vals, axis=0, keepdims=True)
        sel = jnp.min(jnp.where(vals == mx, eidx, N_EXPERTS), axis=0, keepdims=True)
        oh = eidx == sel
        onehots.append(oh)
        top_vals.append(mx)
        top_idx.append(sel)
        vals = jnp.where(oh, NEG_INF, vals)
    exps = [jnp.exp(v - top_vals[0]) for v in top_vals]
    denom = exps[0] + exps[1] + exps[2] + exps[3]
    gate_ref[...] = jnp.concatenate([e / denom for e in exps], axis=0)
    idx_ref[...] = jnp.concatenate(top_idx, axis=0)

    member = (onehots[0] | onehots[1] | onehots[2] | onehots[3])
    member_f = member.astype(F32)
    t_src = lax.broadcasted_iota(jnp.int32, (tm, tm), 0)
    t_dst = lax.broadcasted_iota(jnp.int32, (tm, tm), 1)
    before = (t_src < t_dst).astype(BF16)
    prefix = jnp.dot(member.astype(BF16), before, preferred_element_type=F32) + carry_ref[...]
    ranks = [jnp.sum(jnp.where(oh, prefix, 0.0), axis=0, keepdims=True) for oh in onehots]
    rank_ref[...] = jnp.concatenate(ranks, axis=0).astype(jnp.int32)
    carry_ref[...] = carry_ref[...] + jnp.sum(member_f, axis=1, keepdims=True)
    cnt_ref[...] = jnp.broadcast_to(carry_ref[...], cnt_ref.shape)


def _outproj(out_a, out_b, x2, mod3, w_out, b_out, g_ffn, wr_t, br_col, S):
    N, D = x2.shape
    tm = TM_OUT
    tiles_per_seq = S // tm
    row = lambda i: (i, 0)
    colb = lambda i: (0, i)
    const = lambda i: (0, 0)
    return pl.pallas_call(
        _outproj_kernel,
        out_shape=[jax.ShapeDtypeStruct((N, D), F32), jax.ShapeDtypeStruct((N * D // LANES, LANES), F32),
                   jax.ShapeDtypeStruct((TOP_K, N), jnp.int32), jax.ShapeDtypeStruct((TOP_K, N), F32),
                   jax.ShapeDtypeStruct((TOP_K, N), jnp.int32), jax.ShapeDtypeStruct((N_EXPERTS, LANES), F32)],
        grid=(N // tm,),
        in_specs=[pl.BlockSpec((tm, out_a.shape[1]), row),
                  pl.BlockSpec((tm, out_b.shape[1]), row),
                  pl.BlockSpec((tm, D), row),
                  pl.BlockSpec((1, N_MOD, D), lambda i: (i // tiles_per_seq, 0, 0)),
                  pl.BlockSpec(w_out.shape, const),
                  pl.BlockSpec((1, D), const),
                  pl.BlockSpec((1, D), const),
                  pl.BlockSpec(wr_t.shape, const),
                  pl.BlockSpec((N_EXPERTS, 1), const)],
        out_specs=[pl.BlockSpec((tm, D), row), pl.BlockSpec((tm * D // LANES, LANES), row),
                   pl.BlockSpec((TOP_K, tm), colb), pl.BlockSpec((TOP_K, tm), colb),
                   pl.BlockSpec((TOP_K, tm), colb), pl.BlockSpec((N_EXPERTS, LANES), const)],
        scratch_shapes=[pltpu.VMEM((N_EXPERTS, 1), F32)],
        compiler_params=_cparams(("arbitrary",)),
        name="outproj_router",
    )(out_a, out_b, x2, mod3, w_out, b_out, g_ffn, wr_t, br_col)


def _slotmap_kernel(dest_ref, pad_ref, slot_ref, *, n_tokens, steps):
    j = pl.program_id(0)
    per_step = n_tokens // steps
    pad_per_step = pad_ref.shape[0] // steps

    def mark(p, carry):
        slot_ref[pad_ref[j * pad_per_step + p]] = jnp.int32(-1)
        return carry

    lax.fori_loop(0, pad_per_step, mark, 0, unroll=16)

    for k in range(TOP_K):
        def place(u, carry):
            t = j * per_step + u
            slot_ref[dest_ref[k * n_tokens + t]] = t * TOP_K + k
            return carry

        lax.fori_loop(0, per_step, place, 0, unroll=16)


def _slotmap(dest, pad_rows, n_rows, n_tokens):
    steps = 64
    assert n_tokens % steps == 0 and pad_rows.shape[0] % steps == 0
    return pl.pallas_call(
        functools.partial(_slotmap_kernel, n_tokens=n_tokens, steps=steps),
        out_shape=jax.ShapeDtypeStruct((n_rows,), jnp.int32),
        grid=(steps,),
        in_specs=[pl.BlockSpec(memory_space=pltpu.SMEM), pl.BlockSpec(memory_space=pltpu.SMEM)],
        out_specs=pl.BlockSpec(memory_space=pltpu.SMEM),
        compiler_params=pltpu.CompilerParams(dimension_semantics=("arbitrary",)),
        name="slotmap",
    )(dest, pad_rows).reshape(n_rows // LANES, LANES)


def _experts_kernel(slot_ref, be_ref, nused_ref, h_ref, w1_ref, b1g_ref, b1l_ref, w2_ref, b2_ref, y_ref,
                    w1g_s, w1l_s, w2_s, t_s, xbuf0, xbuf1, gsem):
    i = pl.program_id(0)
    n_used = nused_ref[0]
    used = i < n_used
    new_expert = (i == 0) | (be_ref[i] != be_ref[jnp.maximum(i - 1, 0)])
    rt = w2_s.shape[1] // LANES
    xbufs = (xbuf0, xbuf1)
    bm = xbuf0.shape[0] // rt
    groups = bm // LANES

    def gather(blk, buf):
        for r in range(bm):
            s = slot_ref[blk * groups + r // LANES, r % LANES]
            tok = lax.shift_right_logical(jnp.maximum(s, 0), TOP_K_SHIFT)
            pltpu.make_async_copy(h_ref.at[pl.ds(pl.multiple_of(tok * rt, rt), rt)],
                                  xbufs[buf].at[pl.ds(r * rt, rt)], gsem).start(priority=r % 2)

    def wait_gather(buf):
        pltpu.make_async_copy(h_ref.at[pl.ds(0, bm * rt)], xbufs[buf], gsem).wait()

    @pl.when(i == 0)
    def _():
        gather(0, 0)

    @pl.when(used & new_expert)
    def _():
        n_db, ch, _ = t_s.shape
        half = ch // 2
        for c in range(w1_ref.shape[2] // ch):
            t = w1_ref[0, :, c * ch:(c + 1) * ch].T
            for db in range(n_db):
                t_s[db] = t[:, db * LANES:(db + 1) * LANES]
            for db in range(n_db):
                rows = slice(c * half, (c + 1) * half)
                cols = slice(db * LANES, (db + 1) * LANES)
                w1g_s[rows, cols] = t_s[db, pl.ds(0, half, stride=2), :].astype(BF16)
                w1l_s[rows, cols] = t_s[db, pl.ds(1, half, stride=2), :].astype(BF16)
        w2_s[...] = w2_ref[0].astype(BF16)

    def block(cur):
        nxt = 1 - cur
        wait_gather(cur)
        gather(jnp.minimum(i + 1, n_used - 1), nxt)
        dn = (((1,), (1,)), ((), ()))
        x = jnp.concatenate([xbufs[cur][pl.ds(s, bm, stride=rt), :] for s in range(rt)], axis=1).astype(BF16)
        ug = lax.dot_general(x, w1g_s[...], dn, preferred_element_type=F32) + b1g_ref[0]
        ul = lax.dot_general(x, w1l_s[...], dn, preferred_element_type=F32) + b1l_ref[0]
        glu = jnp.minimum(ug, SWIGLU_LIMIT)
        lin = jnp.clip(ul, -SWIGLU_LIMIT, SWIGLU_LIMIT)
        act = glu * jax.nn.sigmoid(SWIGLU_ALPHA * glu) * (lin + 1.0)
        y_ref[...] = jnp.dot(act.astype(BF16), w2_s[...], preferred_element_type=F32) + b2_ref[0]

        @pl.when(i == n_used - 1)
        def _():
            wait_gather(nxt)

    @pl.when(used & (i % 2 == 0))
    def _():
        block(0)

    @pl.when(used & (i % 2 == 1))
    def _():
        block(1)

    @pl.when(jnp.logical_not(used))
    def _():
        y_ref[...] = jnp.zeros(y_ref.shape, y_ref.dtype)


def _experts(slot2d, block_e, n_used, h2_tiles, w1, b1g, b1l, w2, b2):
    E, Fh, D = w2.shape
    rt = D // LANES
    bm = BLOCK_ROWS
    n_rows = slot2d.shape[0] * LANES
    n_blocks = n_rows // bm
    chunk = 256
    wsel = lambda i, sl, be, nu: (be[i], 0, 0)
    return pl.pallas_call(
        _experts_kernel,
        out_shape=jax.ShapeDtypeStruct((n_rows, D), F32),
        grid_spec=pltpu.PrefetchScalarGridSpec(
            num_scalar_prefetch=3,
            grid=(n_blocks,),
            in_specs=[pl.BlockSpec(memory_space=pl.ANY),
                      pl.BlockSpec((1, D, 2 * Fh), wsel),
                      pl.BlockSpec((1, 1, Fh), wsel),
                      pl.BlockSpec((1, 1, Fh), wsel),
                      pl.BlockSpec((1, Fh, D), wsel),
                      pl.BlockSpec((1, 1, D), wsel)],
            out_specs=pl.BlockSpec((bm, D), lambda i, sl, be, nu: (i, 0)),
            scratch_shapes=[pltpu.VMEM((Fh, D), BF16), pltpu.VMEM((Fh, D), BF16), pltpu.VMEM((Fh, D), BF16),
                            pltpu.VMEM((D // LANES, chunk, LANES), F32),
                            pltpu.VMEM((bm * rt, LANES), F32), pltpu.VMEM((bm * rt, LANES), F32),
                            pltpu.SemaphoreType.DMA]),
        compiler_params=_cparams(("arbitrary",), 56 * 1024 * 1024),
        name="experts",
    )(slot2d, block_e, n_used, h2_tiles, w1, b1g, b1l, w2, b2)


def _combine_kernel(dest_ref, x1_ref, gate_ref, mod_ref, g_ref, y_ref, o_ref, buf_ref, sem, *, n_tokens, final_norm):
    i = pl.program_id(0)
    tm = x1_ref.shape[0]
    base = i * tm

    def issue(t, carry):
        for k in range(TOP_K):
            d = dest_ref[k * n_tokens + base + t]
            pltpu.make_async_copy(y_ref.at[pl.ds(d, 1)], buf_ref.at[k, pl.ds(t, 1)], sem).start(priority=k % 2)
        return carry

    lax.fori_loop(0, tm, issue, 0, unroll=2)

    for k in range(TOP_K):
        pltpu.make_async_copy(y_ref.at[pl.ds(0, tm)], buf_ref.at[k], sem).wait()

    gate = gate_ref[...]
    moe = gate[:, 0:1] * buf_ref[0]
    for k in range(1, TOP_K):
        moe = moe + gate[:, k:k + 1] * buf_ref[k]
    gt2 = mod_ref[0, 5:6, :]
    x2 = x1_ref[...] + gt2 * moe
    o_ref[...] = _rms(x2) * g_ref[...] if final_norm else x2


def _combine(dest_flat, x1, gate_t, mod3, g_final, y_rows, S, final_norm):
    N, D = x1.shape
    tm = TM_ROWS
    tiles_per_seq = S // tm
    return pl.pallas_call(
        functools.partial(_combine_kernel, n_tokens=N, final_norm=final_norm),
        out_shape=jax.ShapeDtypeStruct((N, D), F32),
        grid_spec=pltpu.PrefetchScalarGridSpec(
            num_scalar_prefetch=1,
            grid=(N // tm,),
            in_specs=[pl.BlockSpec((tm, D), lambda i, d: (i, 0)),
                      pl.BlockSpec((tm, TOP_K), lambda i, d: (i, 0)),
                      pl.BlockSpec((1, N_MOD, D), lambda i, d: (i // tiles_per_seq, 0, 0)),
                      pl.BlockSpec((1, D), lambda i, d: (0, 0)),
                      pl.BlockSpec(memory_space=pl.ANY)],
            out_specs=pl.BlockSpec((tm, D), lambda i, d: (i, 0)),
            scratch_shapes=[pltpu.VMEM((TOP_K, tm, D), F32), pltpu.SemaphoreType.DMA]),
        compiler_params=_cparams(("arbitrary",)),
        name="combine",
    )(dest_flat, x1, gate_t, mod3, g_final, y_rows)


def _routing_tables(counts, idx, rank, n_rows):
    bm = BLOCK_ROWS
    counts = counts.astype(jnp.int32)
    padded = (counts + bm - 1) // bm * bm
    pends = jnp.cumsum(padded)
    pstarts = pends - padded
    experts = jnp.arange(N_EXPERTS, dtype=jnp.int32)
    dest = jnp.sum(jnp.where(idx[..., None] == experts, pstarts, 0), axis=-1) + rank
    n_blocks = n_rows // bm
    block_start = jnp.arange(n_blocks, dtype=jnp.int32) * bm
    block_e = jnp.minimum(jnp.sum(pends <= block_start[:, None], axis=-1), N_EXPERTS - 1).astype(jnp.int32)
    n_used = (pends[-1] // bm).astype(jnp.int32).reshape(1)
    seg_start = jnp.concatenate([pstarts + counts, pends[-1:]])
    seg_len = jnp.concatenate([padded - counts, n_rows - pends[-1:]])
    seg_end = jnp.cumsum(seg_len)
    j = jnp.arange(n_rows - idx.size, dtype=jnp.int32)
    in_seg = (j[:, None] >= seg_end - seg_len) & (j[:, None] < seg_end)
    pad_rows = jnp.sum(jnp.where(in_seg, seg_start - (seg_end - seg_len) + j[:, None], 0), axis=-1)
    return dest.astype(jnp.int32).reshape(-1), pad_rows.astype(jnp.int32), block_e, n_used


def _extended_in_weights(w_in, b_in):
    a_q = SWA_Q_HEADS * HEAD_DIM
    a_kv = SWA_KV_HEADS * HEAD_DIM
    b_w = DIFF_HEADS * DIFF_V_DIM
    spans = [(0, a_q)]
    for base in (a_q, a_q + a_kv):
        for j in range(SWA_KV_HEADS):
            spans += [(base + j * HEAD_DIM, base + (j + 1) * HEAD_DIM)] * 2
    spans.append((a_q + 2 * a_kv, a_q + 2 * a_kv + 3 * b_w))
    w_ext = jnp.concatenate([w_in[:, lo:hi] for lo, hi in spans], axis=1).astype(BF16)
    b_ext = jnp.concatenate([b_in[lo:hi] for lo, hi in spans]).reshape(1, -1)
    widths = (a_q, 2 * a_kv, 2 * a_kv, b_w, b_w, b_w)
    return w_ext, b_ext, widths


def kernel(x, c, positions, w_ada, b_ada, g_mix, w_in, b_in, attn_sinks, lambda_q1, lambda_k1, lambda_q2,
           lambda_k2, g_subln, w_out, b_out, g_ffn, w_router, b_router, w1, b1, w2, b2, g_final):
    B, S, D = x.shape
    N = B * S
    depth = w_ada.shape[0]
    assert TM_PROJ == TQ_DIFF, "the input projection writes q/v transposed per attention tile"
    n_rows = (N * TOP_K + N_EXPERTS * (BLOCK_ROWS - 1) + BLOCK_ROWS - 1) // BLOCK_ROWS * BLOCK_ROWS

    inv = 1.0 / (ROPE_THETA ** (jnp.arange(0, HEAD_DIM, 2, dtype=F32) / HEAD_DIM))
    inv_lane = jnp.tile(inv, LANES // (HEAD_DIM // 2)).reshape(1, LANES)
    pos2 = positions.reshape(N, 1)
    xcur = x.reshape(N, D)

    for layer in range(depth):
        last = layer == depth - 1
        lambda_init = 0.8 - 0.6 * math.exp(-0.3 * layer)
        mod3 = _adaln(c, w_ada[layer], b_ada[layer]).reshape(B, N_MOD, D)

        w_ext, b_ext, widths = _extended_in_weights(w_in[layer], b_in[layer])
        qa, ka2, va2, qdt, kd, vdt = _inproj(xcur, pos2, inv_lane, mod3, g_mix[layer].reshape(1, D),
                                             w_ext, b_ext, S, widths)
        out_a = _swa(attn_sinks[layer], qa, ka2, va2, B, S)
        lam_vecs = jnp.stack([lambda_q1[layer], lambda_k1[layer], lambda_q2[layer], lambda_k2[layer]])
        out_b = _diffattn(lam_vecs, g_subln[layer].reshape(1, DIFF_V_DIM), qdt, kd, vdt, B, S, lambda_init)

        x1, h2, idx, gate, rank, counts = _outproj(
            out_a, out_b, xcur, mod3, w_out[layer].astype(BF16), b_out[layer].reshape(1, D),
            g_ffn[layer].reshape(1, D), w_router[layer].T, b_router[layer].reshape(N_EXPERTS, 1), S)

        dest, pad_rows, block_e, n_used = _routing_tables(counts[:, 0], idx, rank, n_rows)
        slot2d = _slotmap(dest, pad_rows, n_rows, N)

        y_rows = _experts(slot2d, block_e, n_used, h2, w1[layer],
                          b1[layer][:, None, 0::2], b1[layer][:, None, 1::2],
                          w2[layer], b2[layer][:, None, :])
        xcur = _combine(dest, x1, gate.T, mod3, g_final.reshape(1, D), y_rows, S, final_norm=last)
    return xcur.reshape(B, S, D)
```

```python
import functools
import math

import jax
import jax.numpy as jnp
from jax import lax
from jax.experimental import pallas as pl
from jax.experimental.pallas import tpu as pltpu
from jax.experimental.pallas import tpu_sc as plsc

HEAD_DIM = 64
SWA_Q_HEADS = 8
SWA_KV_HEADS = 2
SWA_GROUP = SWA_Q_HEADS // SWA_KV_HEADS
WINDOW = 128
DIFF_HEADS = 4
DIFF_V_DIM = 2 * HEAD_DIM
ROPE_THETA = 10000.0
N_EXPERTS = 32
TOP_K = 4
SWIGLU_ALPHA = 1.702
SWIGLU_LIMIT = 7.0
EPS = 1e-5
N_MOD = 6

LANES = 128
LANE_SHIFT = LANES.bit_length() - 1
TOP_K_SHIFT = TOP_K.bit_length() - 1
assert 1 << LANE_SHIFT == LANES and 1 << TOP_K_SHIFT == TOP_K
F32 = jnp.float32
BF16 = jnp.bfloat16
NEG_INF = float("-inf")

TM_PROJ = 512
TQ_SWA = 512
TQ_DIFF = 512
VT_ROWS = DIFF_V_DIM + 16
TM_OUT = 256
TM_ROWS = 256
BLOCK_ROWS = 256
VMEM_LIMIT = 48 * 1024 * 1024


def _cparams(sem, vmem=VMEM_LIMIT):
    return pltpu.CompilerParams(dimension_semantics=sem, vmem_limit_bytes=vmem)


def _adaln_kernel(ct_ref, w_ref, b_ref, o_ref):
    c = ct_ref[...]
    cond = c * jax.nn.sigmoid(c)
    w = w_ref[...]
    rows = [jnp.sum(w * cond[:, b:b + 1], axis=0, keepdims=True) for b in range(c.shape[1])]
    o_ref[...] = jnp.concatenate(rows, axis=0) + b_ref[...]


def _adaln(c, w_ada, b_ada):
    B, D = c.shape
    n_out = w_ada.shape[1]
    tn = 1024
    return pl.pallas_call(
        _adaln_kernel,
        out_shape=jax.ShapeDtypeStruct((B, n_out), F32),
        grid=(n_out // tn,),
        in_specs=[pl.BlockSpec((D, B), lambda j: (0, 0)),
                  pl.BlockSpec((D, tn), lambda j: (0, j)),
                  pl.BlockSpec((1, tn), lambda j: (0, j))],
        out_specs=pl.BlockSpec((B, tn), lambda j: (0, j)),
        compiler_params=_cparams(("arbitrary",)),
        name="adaln",
    )(c.T, w_ada, b_ada.reshape(1, n_out))


def _rms(x):
    return x * lax.rsqrt(jnp.mean(x * x, axis=-1, keepdims=True) + EPS)


def _inproj_kernel(x_ref, pos_ref, inv_ref, mod_ref, g_ref, w_ref, b_ref,
                   qa_ref, ka_ref, va_ref, qd_ref, kd_ref, vd_ref):
    x = x_ref[...]
    sh = mod_ref[0, 0:1, :]
    sc = mod_ref[0, 1:2, :]
    h = _rms(x) * g_ref[...] * (1.0 + sc) + sh
    proj = jnp.dot(h.astype(BF16), w_ref[...], preferred_element_type=F32) + b_ref[...]

    ang = pos_ref[...].astype(F32) * inv_ref[...]
    lane = lax.broadcasted_iota(jnp.int32, (1, LANES), 1)
    first_half = (lane & (HEAD_DIM - 1)) < (HEAD_DIM // 2)
    cos = jnp.cos(ang)
    sin = jnp.sin(ang)
    sin_signed = jnp.where(first_half, -sin, sin)

    def rope(t):
        partner = jnp.where(first_half,
                            pltpu.roll(t, LANES - HEAD_DIM // 2, axis=1),
                            pltpu.roll(t, HEAD_DIM // 2, axis=1))
        return t * cos + partner * sin_signed

    def emit(out_ref, col0, width, rotary, scale, transposed):
        for j in range(width // LANES):
            t = proj[:, col0 + j * LANES: col0 + (j + 1) * LANES]
            if rotary:
                t = rope(t)
            if scale != 1.0:
                t = t * scale
            if transposed:
                rows = out_ref.shape[2] // (width // LANES)
                out_ref[0, 0, j * rows:j * rows + LANES, :] = t.T.astype(out_ref.dtype)
                if rows > LANES:
                    fill = lax.broadcasted_iota(jnp.int32, (rows - LANES, t.shape[0]), 0) == 0
                    out_ref[0, 0, j * rows + LANES:(j + 1) * rows, :] = fill.astype(out_ref.dtype)
            else:
                out_ref[:, j * LANES:(j + 1) * LANES] = t.astype(out_ref.dtype)

    swa_scale = 1.0 / math.sqrt(HEAD_DIM)
    diff_scale = math.log2(math.e) / math.sqrt(HEAD_DIM)
    col = 0
    for out_ref, width, rotary, scale, transposed in (
            (qa_ref, qa_ref.shape[1], True, swa_scale, False), (ka_ref, ka_ref.shape[1], True, 1.0, False),
            (va_ref, va_ref.shape[1], False, 1.0, False), (qd_ref, qd_ref.shape[2], True, diff_scale, True),
            (kd_ref, kd_ref.shape[1], True, 1.0, False),
            (vd_ref, vd_ref.shape[2] // VT_ROWS * LANES, False, 1.0, True)):
        emit(out_ref, col, width, rotary, scale, transposed)
        col += width


def _inproj(x2, pos2, inv_lane, mod3, g_mix, w_ext, b_ext, S, widths):
    N, D = x2.shape
    tm = TM_PROJ
    C = w_ext.shape[1]
    tiles_per_seq = S // tm
    row = lambda i: (i, 0)
    t_rows = (0, 0, 0, widths[3], 0, widths[5] // LANES * VT_ROWS)
    out_shape, out_specs = [], []
    for w, tr in zip(widths, t_rows):
        if tr:
            out_shape.append(jax.ShapeDtypeStruct((N // S, tiles_per_seq, tr, tm), BF16))
            out_specs.append(pl.BlockSpec((1, 1, tr, tm), lambda i: (i // tiles_per_seq, i % tiles_per_seq, 0, 0)))
        else:
            out_shape.append(jax.ShapeDtypeStruct((N, w), BF16))
            out_specs.append(pl.BlockSpec((tm, w), row))
    return pl.pallas_call(
        _inproj_kernel,
        out_shape=out_shape,
        grid=(N // tm,),
        in_specs=[pl.BlockSpec((tm, D), row),
                  pl.BlockSpec((tm, 1), row),
                  pl.BlockSpec((1, LANES), lambda i: (0, 0)),
                  pl.BlockSpec((1, N_MOD, D), lambda i: (i // tiles_per_seq, 0, 0)),
                  pl.BlockSpec((1, D), lambda i: (0, 0)),
                  pl.BlockSpec((D, C), lambda i: (0, 0)),
                  pl.BlockSpec((1, C), lambda i: (0, 0))],
        out_specs=out_specs,
        compiler_params=_cparams(("arbitrary",)),
        name="inproj",
    )(x2, pos2, inv_lane, mod3, g_mix, w_ext, b_ext)


def _swa_kernel(sink_ref, q_ref, kc_ref, kp_ref, vc_ref, vp_ref, o_ref):
    i = pl.program_id(1)
    tq = q_ref.shape[0]
    lane = lax.broadcasted_iota(jnp.int32, (1, LANES), 1)
    lo = lane < HEAD_DIM
    qi = lax.broadcasted_iota(jnp.int32, (WINDOW, 2 * WINDOW), 0) + WINDOW
    kj = lax.broadcasted_iota(jnp.int32, (WINDOW, 2 * WINDOW), 1)
    band = (qi - kj >= 0) & (qi - kj < WINDOW)
    dn = (((1,), (1,)), ((), ()))
    for c in range(tq // WINDOW):
        if c == 0:
            kcat = jnp.concatenate([kp_ref[...], kc_ref[0:WINDOW, :]], axis=0)
            vcat = jnp.concatenate([vp_ref[...], vc_ref[0:WINDOW, :]], axis=0)
            mask = band & (kj >= jnp.where(i > 0, 0, WINDOW))
        else:
            kcat = kc_ref[(c - 1) * WINDOW:(c + 1) * WINDOW, :]
            vcat = vc_ref[(c - 1) * WINDOW:(c + 1) * WINDOW, :]
            mask = band
        for j in range(SWA_KV_HEADS):
            kj2 = kcat[:, j * LANES:(j + 1) * LANES]
            vj2 = vcat[:, j * LANES:(j + 1) * LANES]
            zero = jnp.zeros_like(kj2)
            k_halves = (jnp.where(lo, kj2, zero), jnp.where(lo, zero, kj2))
            v_halves = (jnp.where(lo, vj2, zero), jnp.where(lo, zero, vj2))
            for p in range(SWA_GROUP // 2):
                g = j * (SWA_GROUP // 2) + p
                q = q_ref[c * WINDOW:(c + 1) * WINDOW, g * LANES:(g + 1) * LANES]
                out = jnp.zeros((WINDOW, LANES), F32)
                for half in range(2):
                    sink = sink_ref[2 * g + half]
                    s = lax.dot_general(q, k_halves[half], dn, preferred_element_type=F32)
                    s = jnp.where(mask, s, NEG_INF)
                    m = jnp.maximum(jnp.max(s, axis=1, keepdims=True), sink)
                    e = jnp.exp(s - m)
                    denom = jnp.sum(e, axis=1, keepdims=True) + jnp.exp(sink - m)
                    pv = jnp.dot(e.astype(BF16), v_halves[half], preferred_element_type=F32)
                    out = out + pv / denom
                o_ref[c * WINDOW:(c + 1) * WINDOW, g * LANES:(g + 1) * LANES] = out.astype(o_ref.dtype)


def _swa(sinks, qa, ka2, va2, B, S):
    N = qa.shape[0]
    tq = TQ_SWA
    nq = S // tq
    wpt = tq // WINDOW
    wps = S // WINDOW
    cur = lambda b, i: (b * nq + i, 0)
    prev = lambda b, i: (b * wps + jnp.maximum(i * wpt - 1, 0), 0)
    return pl.pallas_call(
        _swa_kernel,
        out_shape=jax.ShapeDtypeStruct((N, qa.shape[1]), BF16),
        grid=(B, nq),
        in_specs=[pl.BlockSpec(memory_space=pltpu.SMEM),
                  pl.BlockSpec((tq, qa.shape[1]), cur),
                  pl.BlockSpec((tq, ka2.shape[1]), cur),
                  pl.BlockSpec((WINDOW, ka2.shape[1]), prev),
                  pl.BlockSpec((tq, va2.shape[1]), cur),
                  pl.BlockSpec((WINDOW, va2.shape[1]), prev)],
        out_specs=pl.BlockSpec((tq, qa.shape[1]), cur),
        compiler_params=_cparams(("arbitrary", "arbitrary")),
        name="swa",
    )(sinks, qa, ka2, ka2, va2, va2)


def _diff_kernel(lam_ref, g_ref, qt_ref, k_ref, vt_ref, o_ref, sa_ref, sb_ref, m_ref, acc_ref, *, lambda_init):
    i = pl.program_id(2)
    tq = qt_ref.shape[3]
    tk = vt_ref.shape[3]
    qt = qt_ref[0, 0]
    lane = lax.broadcasted_iota(jnp.int32, (1, LANES), 1)
    lo = lane < HEAD_DIM
    m_ref[...] = jnp.full(m_ref.shape, NEG_INF, F32)
    acc_ref[...] = jnp.zeros(acc_ref.shape, F32)

    def scores(c, s_ref):
        k = k_ref[pl.ds(pl.multiple_of(c * tk, tk), tk), :]
        zero = jnp.zeros_like(k)
        s_ref[0] = jnp.dot(jnp.where(lo, k, zero), qt, preferred_element_type=F32)
        s_ref[1] = jnp.dot(jnp.where(lo, zero, k), qt, preferred_element_type=F32)

    def consume(c, s_ref, diagonal):
        vt = vt_ref[0, c]
        for mp in range(2):
            s = s_ref[mp]
            if diagonal:
                kpos = lax.broadcasted_iota(jnp.int32, (tk, tq), 0)
                qpos = lax.broadcasted_iota(jnp.int32, (tk, tq), 1)
                s = jnp.where(kpos <= qpos, s, NEG_INF)
            m_prev = m_ref[mp]
            m_new = jnp.maximum(m_prev, jnp.max(s, axis=0, keepdims=True))
            alpha = jnp.exp2(m_prev - m_new)
            p = jnp.exp2(s - m_new).astype(BF16)
            acc_ref[mp] = alpha * acc_ref[mp] + jnp.dot(vt, p, preferred_element_type=F32)
            m_ref[mp] = m_new

    scores(0, sa_ref)

    def pair(jj, carry):
        c = 2 * jj
        scores(c + 1, sb_ref)
        consume(c, sa_ref, False)
        scores(c + 2, sa_ref)
        consume(c + 1, sb_ref, False)
        return carry

    lax.fori_loop(0, lax.shift_right_logical(i, 1), pair, 0)

    @pl.when(i % 2 == 0)
    def _():
        consume(i, sa_ref, True)

    @pl.when(i % 2 == 1)
    def _():
        scores(i, sb_ref)
        consume(i - 1, sa_ref, False)
        consume(i, sb_ref, True)

    lq1, lk1, lq2, lk2 = (lam_ref[r:r + 1, :] for r in range(4))
    lam = (jnp.exp(jnp.sum(lq1 * lk1, axis=1, keepdims=True))
           - jnp.exp(jnp.sum(lq2 * lk2, axis=1, keepdims=True)) + lambda_init)
    d = DIFF_V_DIM
    ot = (acc_ref[0, 0:d, :] / acc_ref[0, d:d + 1, :]
          - lam * (acc_ref[1, 0:d, :] / acc_ref[1, d:d + 1, :]))
    ot = ot * lax.rsqrt(jnp.mean(ot * ot, axis=0, keepdims=True) + EPS)
    o_ref[...] = (ot.T * g_ref[...] * (1.0 - lambda_init)).astype(o_ref.dtype)


def _diffattn(lam_vecs, g_subln, qdt, kd, vdt, B, S, lambda_init):
    N, C = kd.shape
    tq = TQ_DIFF
    nq = S // tq
    return pl.pallas_call(
        functools.partial(_diff_kernel, lambda_init=lambda_init),
        out_shape=jax.ShapeDtypeStruct((N, C), BF16),
        grid=(B, DIFF_HEADS, nq),
        in_specs=[pl.BlockSpec((4, HEAD_DIM), lambda b, h, i: (0, 0)),
                  pl.BlockSpec((1, DIFF_V_DIM), lambda b, h, i: (0, 0)),
                  pl.BlockSpec((1, 1, LANES, tq), lambda b, h, i: (b, i, h, 0)),
                  pl.BlockSpec((S, LANES), lambda b, h, i: (b, h)),
                  pl.BlockSpec((1, nq, VT_ROWS, tq), lambda b, h, i: (b, 0, h, 0))],
        out_specs=pl.BlockSpec((tq, LANES), lambda b, h, i: (b * nq + i, h)),
        scratch_shapes=[pltpu.VMEM((2, tq, tq), F32), pltpu.VMEM((2, tq, tq), F32),
                        pltpu.VMEM((2, 1, tq), F32), pltpu.VMEM((2, VT_ROWS, tq), F32)],
        compiler_params=_cparams(("arbitrary", "arbitrary", "arbitrary")),
        name="diffattn",
    )(lam_vecs, g_subln, qdt, kd, vdt)


def _outproj_kernel(oa_ref, ob_ref, x_ref, mod_ref, wo_ref, bo_ref, g_ref, wr_ref, br_ref,
                    x1_ref, h2_ref, idx_ref, gate_ref, rank_ref, cnt_ref, carry_ref):
    i = pl.program_id(0)
    tm = x_ref.shape[0]
    half = oa_ref.shape[1]

    @pl.when(i == 0)
    def _():
        carry_ref[...] = jnp.zeros(carry_ref.shape, F32)

    gt1 = mod_ref[0, 2:3, :]
    sh2 = mod_ref[0, 3:4, :]
    sc2 = mod_ref[0, 4:5, :]
    mixed = (jnp.dot(oa_ref[...], wo_ref[0:half, :], preferred_element_type=F32)
             + jnp.dot(ob_ref[...], wo_ref[half:, :], preferred_element_type=F32) + bo_ref[...])
    x1 = x_ref[...] + gt1 * mixed
    x1_ref[...] = x1
    h2 = _rms(x1) * g_ref[...] * (1.0 + sc2) + sh2
    row_tiles = h2.shape[1] // LANES
    for s in range(row_tiles):
        h2_ref[pl.ds(s, tm, stride=row_tiles), :] = h2[:, s * LANES:(s + 1) * LANES]

    dn = (((1,), (1,)), ((), ()))
    h_hi = h2.astype(BF16)
    h_lo = (h2 - h_hi.astype(F32)).astype(BF16)
    w = wr_ref[...]
    w_hi = w.astype(BF16)
    w_lo = (w - w_hi.astype(F32)).astype(BF16)
    logits = (lax.dot_general(w_hi, h_hi, dn, preferred_element_type=F32)
              + lax.dot_general(w_hi, h_lo, dn, preferred_element_type=F32)
              + lax.dot_general(w_lo, h_hi, dn, preferred_element_type=F32)
              + br_ref[...])

    eidx = lax.broadcasted_iota(jnp.int32, logits.shape, 0)
    vals = logits
    onehots, top_vals, top_idx = [], [], []
    for _k in range(TOP_K):
        mx = jnp.max(vals, axis=0, keepdims=True)
        sel = jnp.min(jnp.where(vals == mx, eidx, N_EXPERTS), axis=0, keepdims=True)
        oh = eidx == sel
        onehots.append(oh)
        top_vals.append(mx)
        top_idx.append(sel)
        vals = jnp.where(oh, NEG_INF, vals)
    exps = [jnp.exp(v - top_vals[0]) for v in top_vals]
    denom = exps[0] + exps[1] + exps[2] + exps[3]
    gate_ref[...] = jnp.concatenate([e / denom for e in exps], axis=0)
    idx_ref[...] = jnp.concatenate(top_idx, axis=0)

    member = (onehots[0] | onehots[1] | onehots[2] | onehots[3])
    member_f = member.astype(F32)
    t_src = lax.broadcasted_iota(jnp.int32, (tm, tm), 0)
    t_dst = lax.broadcasted_iota(jnp.int32, (tm, tm), 1)
    before = (t_src < t_dst).astype(BF16)
    prefix = jnp.dot(member.astype(BF16), before, preferred_element_type=F32) + carry_ref[...]
    ranks = [jnp.sum(jnp.where(oh, prefix, 0.0), axis=0, keepdims=True) for oh in onehots]
    rank_ref[...] = jnp.concatenate(ranks, axis=0).astype(jnp.int32)
    carry_ref[...] = carry_ref[...] + jnp.sum(member_f, axis=1, keepdims=True)
    cnt_ref[...] = jnp.broadcast_to(carry_ref[...], cnt_ref.shape)


def _outproj(out_a, out_b, x2, mod3, w_out, b_out, g_ffn, wr_t, br_col, S):
    N, D = x2.shape
    tm = TM_OUT
    tiles_per_seq = S // tm
    row = lambda i: (i, 0)
    colb = lambda i: (0, i)
    const = lambda i: (0, 0)
    return pl.pallas_call(
        _outproj_kernel,
        out_shape=[jax.ShapeDtypeStruct((N, D), F32), jax.ShapeDtypeStruct((N * D // LANES, LANES), F32),
                   jax.ShapeDtypeStruct((TOP_K, N), jnp.int32), jax.ShapeDtypeStruct((TOP_K, N), F32),
                   jax.ShapeDtypeStruct((TOP_K, N), jnp.int32), jax.ShapeDtypeStruct((N_EXPERTS, LANES), F32)],
        grid=(N // tm,),
        in_specs=[pl.BlockSpec((tm, out_a.shape[1]), row),
                  pl.BlockSpec((tm, out_b.shape[1]), row),
                  pl.BlockSpec((tm, D), row),
                  pl.BlockSpec((1, N_MOD, D), lambda i: (i // tiles_per_seq, 0, 0)),
                  pl.BlockSpec(w_out.shape, const),
                  pl.BlockSpec((1, D), const),
                  pl.BlockSpec((1, D), const),
                  pl.BlockSpec(wr_t.shape, const),
                  pl.BlockSpec((N_EXPERTS, 1), const)],
        out_specs=[pl.BlockSpec((tm, D), row), pl.BlockSpec((tm * D // LANES, LANES), row),
                   pl.BlockSpec((TOP_K, tm), colb), pl.BlockSpec((TOP_K, tm), colb),
                   pl.BlockSpec((TOP_K, tm), colb), pl.BlockSpec((N_EXPERTS, LANES), const)],
        scratch_shapes=[pltpu.VMEM((N_EXPERTS, 1), F32)],
        compiler_params=_cparams(("arbitrary",)),
        name="outproj_router",
    )(out_a, out_b, x2, mod3, w_out, b_out, g_ffn, wr_t, br_col)


def _slotmap_kernel(dest_ref, pad_ref, slot_ref, *, n_tokens, steps):
    j = pl.program_id(0)
    per_step = n_tokens // steps
    pad_per_step = pad_ref.shape[0] // steps

    def mark(p, carry):
        slot_ref[pad_ref[j * pad_per_step + p]] = jnp.int32(-1)
        return carry

    lax.fori_loop(0, pad_per_step, mark, 0, unroll=16)

    for k in range(TOP_K):
        def place(u, carry):
            t = j * per_step + u
            slot_ref[dest_ref[k * n_tokens + t]] = t * TOP_K + k
            return carry

        lax.fori_loop(0, per_step, place, 0, unroll=16)


def _slotmap(dest, pad_rows, n_rows, n_tokens):
    steps = 64
    assert n_tokens % steps == 0 and pad_rows.shape[0] % steps == 0
    return pl.pallas_call(
        functools.partial(_slotmap_kernel, n_tokens=n_tokens, steps=steps),
        out_shape=jax.ShapeDtypeStruct((n_rows,), jnp.int32),
        grid=(steps,),
        in_specs=[pl.BlockSpec(memory_space=pltpu.SMEM), pl.BlockSpec(memory_space=pltpu.SMEM)],
        out_specs=pl.BlockSpec(memory_space=pltpu.SMEM),
        compiler_params=pltpu.CompilerParams(dimension_semantics=("arbitrary",)),
        name="slotmap",
    )(dest, pad_rows).reshape(n_rows // LANES, LANES)


def _experts_kernel(slot_ref, be_ref, nused_ref, h_ref, w1_ref, b1g_ref, b1l_ref, w2_ref, b2_ref, y_ref,
                    w1g_s, w1l_s, w2_s, t_s, xbuf0, xbuf1, gsem):
    i = pl.program_id(0)
    n_used = nused_ref[0]
    used = i < n_used
    new_expert = (i == 0) | (be_ref[i] != be_ref[jnp.maximum(i - 1, 0)])
    rt = w2_s.shape[1] // LANES
    xbufs = (xbuf0, xbuf1)
    bm = xbuf0.shape[0] // rt
    groups = bm // LANES

    def gather(blk, buf):
        for r in range(bm):
            s = slot_ref[blk * groups + r // LANES, r % LANES]
            tok = lax.shift_right_logical(jnp.maximum(s, 0), TOP_K_SHIFT)
            pltpu.make_async_copy(h_ref.at[pl.ds(pl.multiple_of(tok * rt, rt), rt)],
                                  xbufs[buf].at[pl.ds(r * rt, rt)], gsem).start(priority=r % 2)

    def wait_gather(buf):
        pltpu.make_async_copy(h_ref.at[pl.ds(0, bm * rt)], xbufs[buf], gsem).wait()

    @pl.when(i == 0)
    def _():
        gather(0, 0)

    @pl.when(used & new_expert)
    def _():
        n_db, ch, _ = t_s.shape
        half = ch // 2
        for c in range(w1_ref.shape[2] // ch):
            t = w1_ref[0, :, c * ch:(c + 1) * ch].T
            for db in range(n_db):
                t_s[db] = t[:, db * LANES:(db + 1) * LANES]
            for db in range(n_db):
                rows = slice(c * half, (c + 1) * half)
                cols = slice(db * LANES, (db + 1) * LANES)
                w1g_s[rows, cols] = t_s[db, pl.ds(0, half, stride=2), :].astype(BF16)
                w1l_s[rows, cols] = t_s[db, pl.ds(1, half, stride=2), :].astype(BF16)
        w2_s[...] = w2_ref[0].astype(BF16)

    def block(cur):
        nxt = 1 - cur
        wait_gather(cur)
        gather(jnp.minimum(i + 1, n_used - 1), nxt)
        dn = (((1,), (1,)), ((), ()))
        x = jnp.concatenate([xbufs[cur][pl.ds(s, bm, stride=rt), :] for s in range(rt)], axis=1).astype(BF16)
        ug = lax.dot_general(x, w1g_s[...], dn, preferred_element_type=F32) + b1g_ref[0]
        ul = lax.dot_general(x, w1l_s[...], dn, preferred_element_type=F32) + b1l_ref[0]
        glu = jnp.minimum(ug, SWIGLU_LIMIT)
        lin = jnp.clip(ul, -SWIGLU_LIMIT, SWIGLU_LIMIT)
        act = glu * jax.nn.sigmoid(SWIGLU_ALPHA * glu) * (lin + 1.0)
        y_ref[...] = jnp.dot(act.astype(BF16), w2_s[...], preferred_element_type=F32) + b2_ref[0]

        @pl.when(i == n_used - 1)
        def _():
            wait_gather(nxt)

    @pl.when(used & (i % 2 == 0))
    def _():
        block(0)

    @pl.when(used & (i % 2 == 1))
    def _():
        block(1)

    @pl.when(jnp.logical_not(used))
    def _():
        y_ref[...] = jnp.zeros(y_ref.shape, y_ref.dtype)


def _experts(slot2d, block_e, n_used, h2_tiles, w1, b1g, b1l, w2, b2):
    E, Fh, D = w2.shape
    rt = D // LANES
    bm = BLOCK_ROWS
    n_rows = slot2d.shape[0] * LANES
    n_blocks = n_rows // bm
    chunk = 256
    wsel = lambda i, sl, be, nu: (be[i], 0, 0)
    return pl.pallas_call(
        _experts_kernel,
        out_shape=jax.ShapeDtypeStruct((n_rows, D), F32),
        grid_spec=pltpu.PrefetchScalarGridSpec(
            num_scalar_prefetch=3,
            grid=(n_blocks,),
            in_specs=[pl.BlockSpec(memory_space=pl.ANY),
                      pl.BlockSpec((1, D, 2 * Fh), wsel),
                      pl.BlockSpec((1, 1, Fh), wsel),
                      pl.BlockSpec((1, 1, Fh), wsel),
                      pl.BlockSpec((1, Fh, D), wsel),
                      pl.BlockSpec((1, 1, D), wsel)],
            out_specs=pl.BlockSpec((bm, D), lambda i, sl, be, nu: (i, 0)),
            scratch_shapes=[pltpu.VMEM((Fh, D), BF16), pltpu.VMEM((Fh, D), BF16), pltpu.VMEM((Fh, D), BF16),
                            pltpu.VMEM((D // LANES, chunk, LANES), F32),
                            pltpu.VMEM((bm * rt, LANES), F32), pltpu.VMEM((bm * rt, LANES), F32),
                            pltpu.SemaphoreType.DMA]),
        compiler_params=_cparams(("arbitrary",), 56 * 1024 * 1024),
        name="experts",
    )(slot2d, block_e, n_used, h2_tiles, w1, b1g, b1l, w2, b2)


SC_CORES = 2
SC_SUBCORES = 16
SC_CHUNK = 32


def _sc_gather_rows(table, idx):
    M = idx.shape[0]
    D = table.shape[1]
    workers = SC_CORES * SC_SUBCORES
    per_worker = M // workers
    assert M % workers == 0 and per_worker % SC_CHUNK == 0
    mesh = plsc.VectorSubcoreMesh(core_axis_name="c", subcore_axis_name="s")

    @functools.partial(
        pl.kernel, mesh=mesh,
        out_type=jax.ShapeDtypeStruct((M, D), table.dtype),
        scratch_types=[pltpu.VMEM((per_worker,), jnp.int32),
                       pltpu.VMEM((SC_CHUNK, D), table.dtype),
                       pltpu.SemaphoreType.DMA],
        name="sc_gather_rows")
    def gather(table_hbm, idx_hbm, out_hbm, idx_v, rows_v, sem):
        wid = lax.axis_index("s") * SC_CORES + lax.axis_index("c")
        base = wid * per_worker
        pltpu.sync_copy(idx_hbm.at[pl.ds(base, per_worker)], idx_v)

        @pl.loop(0, per_worker // SC_CHUNK)
        def _(j):
            off = pl.multiple_of(j * SC_CHUNK, SC_CHUNK)
            pltpu.async_copy(table_hbm.at[idx_v.at[pl.ds(off, SC_CHUNK)]], rows_v, sem).wait()
            pltpu.sync_copy(rows_v, out_hbm.at[pl.ds(base + off, SC_CHUNK)])

    return gather(table, idx)


def _combine_kernel(x1_ref, gate_ref, mod_ref, g_ref, y_ref, o_ref, *, final_norm):
    gate = gate_ref[...]
    moe = gate[:, 0:1] * y_ref[0]
    for k in range(1, TOP_K):
        moe = moe + gate[:, k:k + 1] * y_ref[k]
    gt2 = mod_ref[0, 5:6, :]
    x2 = x1_ref[...] + gt2 * moe
    o_ref[...] = _rms(x2) * g_ref[...] if final_norm else x2


def _combine(x1, gate_t, mod3, g_final, y_tok, S, final_norm):
    N, D = x1.shape
    tm = TM_ROWS
    tiles_per_seq = S // tm
    row = lambda i: (i, 0)
    return pl.pallas_call(
        functools.partial(_combine_kernel, final_norm=final_norm),
        out_shape=jax.ShapeDtypeStruct((N, D), F32),
        grid=(N // tm,),
        in_specs=[pl.BlockSpec((tm, D), row),
                  pl.BlockSpec((tm, TOP_K), row),
                  pl.BlockSpec((1, N_MOD, D), lambda i: (i // tiles_per_seq, 0, 0)),
                  pl.BlockSpec((1, D), lambda i: (0, 0)),
                  pl.BlockSpec((TOP_K, tm, D), lambda i: (0, i, 0))],
        out_specs=pl.BlockSpec((tm, D), row),
        compiler_params=_cparams(("arbitrary",)),
        name="combine",
    )(x1, gate_t, mod3, g_final, y_tok)


def _routing_tables(counts, idx, rank, n_rows):
    bm = BLOCK_ROWS
    counts = counts.astype(jnp.int32)
    padded = (counts + bm - 1) // bm * bm
    pends = jnp.cumsum(padded)
    pstarts = pends - padded
    experts = jnp.arange(N_EXPERTS, dtype=jnp.int32)
    dest = jnp.sum(jnp.where(idx[..., None] == experts, pstarts, 0), axis=-1) + rank
    n_blocks = n_rows // bm
    block_start = jnp.arange(n_blocks, dtype=jnp.int32) * bm
    block_e = jnp.minimum(jnp.sum(pends <= block_start[:, None], axis=-1), N_EXPERTS - 1).astype(jnp.int32)
    n_used = (pends[-1] // bm).astype(jnp.int32).reshape(1)
    seg_start = jnp.concatenate([pstarts + counts, pends[-1:]])
    seg_len = jnp.concatenate([padded - counts, n_rows - pends[-1:]])
    seg_end = jnp.cumsum(seg_len)
    j = jnp.arange(n_rows - idx.size, dtype=jnp.int32)
    in_seg = (j[:, None] >= seg_end - seg_len) & (j[:, None] < seg_end)
    pad_rows = jnp.sum(jnp.where(in_seg, seg_start - (seg_end - seg_len) + j[:, None], 0), axis=-1)
    return dest.astype(jnp.int32).reshape(-1), pad_rows.astype(jnp.int32), block_e, n_used


def _extended_in_weights(w_in, b_in):
    a_q = SWA_Q_HEADS * HEAD_DIM
    a_kv = SWA_KV_HEADS * HEAD_DIM
    b_w = DIFF_HEADS * DIFF_V_DIM
    spans = [(0, a_q)]
    for base in (a_q, a_q + a_kv):
        for j in range(SWA_KV_HEADS):
            spans += [(base + j * HEAD_DIM, base + (j + 1) * HEAD_DIM)] * 2
    spans.append((a_q + 2 * a_kv, a_q + 2 * a_kv + 3 * b_w))
    w_ext = jnp.concatenate([w_in[:, lo:hi] for lo, hi in spans], axis=1).astype(BF16)
    b_ext = jnp.concatenate([b_in[lo:hi] for lo, hi in spans]).reshape(1, -1)
    widths = (a_q, 2 * a_kv, 2 * a_kv, b_w, b_w, b_w)
    return w_ext, b_ext, widths


def kernel(x, c, positions, w_ada, b_ada, g_mix, w_in, b_in, attn_sinks, lambda_q1, lambda_k1, lambda_q2,
           lambda_k2, g_subln, w_out, b_out, g_ffn, w_router, b_router, w1, b1, w2, b2, g_final):
    B, S, D = x.shape
    N = B * S
    depth = w_ada.shape[0]
    assert TM_PROJ == TQ_DIFF, "the input projection writes q/v transposed per attention tile"
    n_rows = (N * TOP_K + N_EXPERTS * (BLOCK_ROWS - 1) + BLOCK_ROWS - 1) // BLOCK_ROWS * BLOCK_ROWS

    inv = 1.0 / (ROPE_THETA ** (jnp.arange(0, HEAD_DIM, 2, dtype=F32) / HEAD_DIM))
    inv_lane = jnp.tile(inv, LANES // (HEAD_DIM // 2)).reshape(1, LANES)
    pos2 = positions.reshape(N, 1)
    xcur = x.reshape(N, D)

    for layer in range(depth):
        last = layer == depth - 1
        lambda_init = 0.8 - 0.6 * math.exp(-0.3 * layer)
        mod3 = _adaln(c, w_ada[layer], b_ada[layer]).reshape(B, N_MOD, D)

        w_ext, b_ext, widths = _extended_in_weights(w_in[layer], b_in[layer])
        qa, ka2, va2, qdt, kd, vdt = _inproj(xcur, pos2, inv_lane, mod3, g_mix[layer].reshape(1, D),
                                             w_ext, b_ext, S, widths)
        out_a = _swa(attn_sinks[layer], qa, ka2, va2, B, S)
        lam_vecs = jnp.stack([lambda_q1[layer], lambda_k1[layer], lambda_q2[layer], lambda_k2[layer]])
        out_b = _diffattn(lam_vecs, g_subln[layer].reshape(1, DIFF_V_DIM), qdt, kd, vdt, B, S, lambda_init)

        x1, h2, idx, gate, rank, counts = _outproj(
            out_a, out_b, xcur, mod3, w_out[layer].astype(BF16), b_out[layer].reshape(1, D),
            g_ffn[layer].reshape(1, D), w_router[layer].T, b_router[layer].reshape(N_EXPERTS, 1), S)

        dest, pad_rows, block_e, n_used = _routing_tables(counts[:, 0], idx, rank, n_rows)
        slot2d = _slotmap(dest, pad_rows, n_rows, N)

        y_rows = _experts(slot2d, block_e, n_used, h2, w1[layer],
                          b1[layer][:, None, 0::2], b1[layer][:, None, 1::2],
                          w2[layer], b2[layer][:, None, :])
        y_tok = _sc_gather_rows(y_rows, dest).reshape(TOP_K, N, D)
        xcur = _combine(x1, gate.T, mod3, g_final.reshape(1, D), y_tok, S, final_norm=last)
    return xcur.reshape(B, S, D)
```

```python
import functools
import math

import jax
import jax.numpy as jnp
from jax import lax
from jax.experimental import pallas as pl
from jax.experimental.pallas import tpu as pltpu
from jax.experimental.pallas import tpu_sc as plsc

HEAD_DIM = 64
SWA_Q_HEADS = 8
SWA_KV_HEADS = 2
SWA_GROUP = SWA_Q_HEADS // SWA_KV_HEADS
WINDOW = 128
DIFF_HEADS = 4
DIFF_V_DIM = 2 * HEAD_DIM
ROPE_THETA = 10000.0
N_EXPERTS = 32
TOP_K = 4
SWIGLU_ALPHA = 1.702
SWIGLU_LIMIT = 7.0
EPS = 1e-5
N_MOD = 6

LANES = 128
LANE_SHIFT = LANES.bit_length() - 1
TOP_K_SHIFT = TOP_K.bit_length() - 1
assert 1 << LANE_SHIFT == LANES and 1 << TOP_K_SHIFT == TOP_K
F32 = jnp.float32
BF16 = jnp.bfloat16
NEG_INF = float("-inf")

TM_PROJ = 512
TQ_SWA = 512
TQ_DIFF = 512
VT_ROWS = DIFF_V_DIM + 16
TM_OUT = 256
TM_ROWS = 256
BLOCK_ROWS = 256
VMEM_LIMIT = 48 * 1024 * 1024


def _cparams(sem, vmem=VMEM_LIMIT):
    return pltpu.CompilerParams(dimension_semantics=sem, vmem_limit_bytes=vmem)


def _adaln_kernel(ct_ref, w_ref, b_ref, o_ref):
    c = ct_ref[...]
    cond = c * jax.nn.sigmoid(c)
    w = w_ref[...]
    rows = [jnp.sum(w * cond[:, b:b + 1], axis=0, keepdims=True) for b in range(c.shape[1])]
    o_ref[...] = jnp.concatenate(rows, axis=0) + b_ref[...]


def _adaln(c, w_ada, b_ada):
    B, D = c.shape
    n_out = w_ada.shape[1]
    tn = 1024
    return pl.pallas_call(
        _adaln_kernel,
        out_shape=jax.ShapeDtypeStruct((B, n_out), F32),
        grid=(n_out // tn,),
        in_specs=[pl.BlockSpec((D, B), lambda j: (0, 0)),
                  pl.BlockSpec((D, tn), lambda j: (0, j)),
                  pl.BlockSpec((1, tn), lambda j: (0, j))],
        out_specs=pl.BlockSpec((B, tn), lambda j: (0, j)),
        compiler_params=_cparams(("arbitrary",)),
        name="adaln",
    )(c.T, w_ada, b_ada.reshape(1, n_out))


def _rms(x):
    return x * lax.rsqrt(jnp.mean(x * x, axis=-1, keepdims=True) + EPS)


def _inproj_kernel(x_ref, pos_ref, inv_ref, mod_ref, g_ref, w_ref, b_ref,
                   qa_ref, ka_ref, va_ref, qd_ref, kd_ref, vd_ref):
    x = x_ref[...]
    sh = mod_ref[0, 0:1, :]
    sc = mod_ref[0, 1:2, :]
    h = _rms(x) * g_ref[...] * (1.0 + sc) + sh
    proj = jnp.dot(h.astype(BF16), w_ref[...], preferred_element_type=F32) + b_ref[...]

    ang = pos_ref[...].astype(F32) * inv_ref[...]
    lane = lax.broadcasted_iota(jnp.int32, (1, LANES), 1)
    first_half = (lane & (HEAD_DIM - 1)) < (HEAD_DIM // 2)
    cos = jnp.cos(ang)
    sin = jnp.sin(ang)
    sin_signed = jnp.where(first_half, -sin, sin)

    def rope(t):
        partner = jnp.where(first_half,
                            pltpu.roll(t, LANES - HEAD_DIM // 2, axis=1),
                            pltpu.roll(t, HEAD_DIM // 2, axis=1))
        return t * cos + partner * sin_signed

    def emit(out_ref, col0, width, rotary, scale, transposed):
        for j in range(width // LANES):
            t = proj[:, col0 + j * LANES: col0 + (j + 1) * LANES]
            if rotary:
                t = rope(t)
            if scale != 1.0:
                t = t * scale
            if transposed:
                rows = out_ref.shape[2] // (width // LANES)
                out_ref[0, 0, j * rows:j * rows + LANES, :] = t.T.astype(out_ref.dtype)
                if rows > LANES:
                    fill = lax.broadcasted_iota(jnp.int32, (rows - LANES, t.shape[0]), 0) == 0
                    out_ref[0, 0, j * rows + LANES:(j + 1) * rows, :] = fill.astype(out_ref.dtype)
            else:
                out_ref[:, j * LANES:(j + 1) * LANES] = t.astype(out_ref.dtype)

    swa_scale = 1.0 / math.sqrt(HEAD_DIM)
    diff_scale = math.log2(math.e) / math.sqrt(HEAD_DIM)
    col = 0
    for out_ref, width, rotary, scale, transposed in (
            (qa_ref, qa_ref.shape[1], True, swa_scale, False), (ka_ref, ka_ref.shape[1], True, 1.0, False),
            (va_ref, va_ref.shape[1], False, 1.0, False), (qd_ref, qd_ref.shape[2], True, diff_scale, True),
            (kd_ref, kd_ref.shape[1], True, 1.0, False),
            (vd_ref, vd_ref.shape[2] // VT_ROWS * LANES, False, 1.0, True)):
        emit(out_ref, col, width, rotary, scale, transposed)
        col += width


def _inproj(x2, pos2, inv_lane, mod3, g_mix, w_ext, b_ext, S, widths):
    N, D = x2.shape
    tm = TM_PROJ
    C = w_ext.shape[1]
    tiles_per_seq = S // tm
    row = lambda i: (i, 0)
    t_rows = (0, 0, 0, widths[3], 0, widths[5] // LANES * VT_ROWS)
    out_shape, out_specs = [], []
    for w, tr in zip(widths, t_rows):
        if tr:
            out_shape.append(jax.ShapeDtypeStruct((N // S, tiles_per_seq, tr, tm), BF16))
            out_specs.append(pl.BlockSpec((1, 1, tr, tm), lambda i: (i // tiles_per_seq, i % tiles_per_seq, 0, 0)))
        else:
            out_shape.append(jax.ShapeDtypeStruct((N, w), BF16))
            out_specs.append(pl.BlockSpec((tm, w), row))
    return pl.pallas_call(
        _inproj_kernel,
        out_shape=out_shape,
        grid=(N // tm,),
        in_specs=[pl.BlockSpec((tm, D), row),
                  pl.BlockSpec((tm, 1), row),
                  pl.BlockSpec((1, LANES), lambda i: (0, 0)),
                  pl.BlockSpec((1, N_MOD, D), lambda i: (i // tiles_per_seq, 0, 0)),
                  pl.BlockSpec((1, D), lambda i: (0, 0)),
                  pl.BlockSpec((D, C), lambda i: (0, 0)),
                  pl.BlockSpec((1, C), lambda i: (0, 0))],
        out_specs=out_specs,
        compiler_params=_cparams(("arbitrary",)),
        name="inproj",
    )(x2, pos2, inv_lane, mod3, g_mix, w_ext, b_ext)


def _swa_kernel(sink_ref, q_ref, kc_ref, kp_ref, vc_ref, vp_ref, o_ref):
    i = pl.program_id(1)
    tq = q_ref.shape[0]
    lane = lax.broadcasted_iota(jnp.int32, (1, LANES), 1)
    lo = lane < HEAD_DIM
    qi = lax.broadcasted_iota(jnp.int32, (WINDOW, 2 * WINDOW), 0) + WINDOW
    kj = lax.broadcasted_iota(jnp.int32, (WINDOW, 2 * WINDOW), 1)
    band = (qi - kj >= 0) & (qi - kj < WINDOW)
    dn = (((1,), (1,)), ((), ()))
    for c in range(tq // WINDOW):
        if c == 0:
            kcat = jnp.concatenate([kp_ref[...], kc_ref[0:WINDOW, :]], axis=0)
            vcat = jnp.concatenate([vp_ref[...], vc_ref[0:WINDOW, :]], axis=0)
            mask = band & (kj >= jnp.where(i > 0, 0, WINDOW))
        else:
            kcat = kc_ref[(c - 1) * WINDOW:(c + 1) * WINDOW, :]
            vcat = vc_ref[(c - 1) * WINDOW:(c + 1) * WINDOW, :]
            mask = band
        for j in range(SWA_KV_HEADS):
            kj2 = kcat[:, j * LANES:(j + 1) * LANES]
            vj2 = vcat[:, j * LANES:(j + 1) * LANES]
            zero = jnp.zeros_like(kj2)
            k_halves = (jnp.where(lo, kj2, zero), jnp.where(lo, zero, kj2))
            v_halves = (jnp.where(lo, vj2, zero), jnp.where(lo, zero, vj2))
            for p in range(SWA_GROUP // 2):
                g = j * (SWA_GROUP // 2) + p
                q = q_ref[c * WINDOW:(c + 1) * WINDOW, g * LANES:(g + 1) * LANES]
                out = jnp.zeros((WINDOW, LANES), F32)
                for half in range(2):
                    sink = sink_ref[2 * g + half]
                    s = lax.dot_general(q, k_halves[half], dn, preferred_element_type=F32)
                    s = jnp.where(mask, s, NEG_INF)
                    m = jnp.maximum(jnp.max(s, axis=1, keepdims=True), sink)
                    e = jnp.exp(s - m)
                    denom = jnp.sum(e, axis=1, keepdims=True) + jnp.exp(sink - m)
                    pv = jnp.dot(e.astype(BF16), v_halves[half], preferred_element_type=F32)
                    out = out + pv / denom
                o_ref[c * WINDOW:(c + 1) * WINDOW, g * LANES:(g + 1) * LANES] = out.astype(o_ref.dtype)


def _swa(sinks, qa, ka2, va2, B, S):
    N = qa.shape[0]
    tq = TQ_SWA
    nq = S // tq
    wpt = tq // WINDOW
    wps = S // WINDOW
    cur = lambda b, i: (b * nq + i, 0)
    prev = lambda b, i: (b * wps + jnp.maximum(i * wpt - 1, 0), 0)
    return pl.pallas_call(
        _swa_kernel,
        out_shape=jax.ShapeDtypeStruct((N, qa.shape[1]), BF16),
        grid=(B, nq),
        in_specs=[pl.BlockSpec(memory_space=pltpu.SMEM),
                  pl.BlockSpec((tq, qa.shape[1]), cur),
                  pl.BlockSpec((tq, ka2.shape[1]), cur),
                  pl.BlockSpec((WINDOW, ka2.shape[1]), prev),
                  pl.BlockSpec((tq, va2.shape[1]), cur),
                  pl.BlockSpec((WINDOW, va2.shape[1]), prev)],
        out_specs=pl.BlockSpec((tq, qa.shape[1]), cur),
        compiler_params=_cparams(("arbitrary", "arbitrary")),
        name="swa",
    )(sinks, qa, ka2, ka2, va2, va2)


def _diff_kernel(lam_ref, g_ref, qt_ref, k_ref, vt_ref, o_ref, sa_ref, sb_ref, m_ref, acc_ref, *, lambda_init):
    i = pl.program_id(2)
    tq = qt_ref.shape[3]
    tk = vt_ref.shape[3]
    qt = qt_ref[0, 0]
    lane = lax.broadcasted_iota(jnp.int32, (1, LANES), 1)
    lo = lane < HEAD_DIM
    m_ref[...] = jnp.full(m_ref.shape, NEG_INF, F32)
    acc_ref[...] = jnp.zeros(acc_ref.shape, F32)

    def scores(c, s_ref):
        k = k_ref[pl.ds(pl.multiple_of(c * tk, tk), tk), :]
        zero = jnp.zeros_like(k)
        s_ref[0] = jnp.dot(jnp.where(lo, k, zero), qt, preferred_element_type=F32)
        s_ref[1] = jnp.dot(jnp.where(lo, zero, k), qt, preferred_element_type=F32)

    def consume(c, s_ref, diagonal):
        vt = vt_ref[0, c]
        for mp in range(2):
            s = s_ref[mp]
            if diagonal:
                kpos = lax.broadcasted_iota(jnp.int32, (tk, tq), 0)
                qpos = lax.broadcasted_iota(jnp.int32, (tk, tq), 1)
                s = jnp.where(kpos <= qpos, s, NEG_INF)
            m_prev = m_ref[mp]
            m_new = jnp.maximum(m_prev, jnp.max(s, axis=0, keepdims=True))
            alpha = jnp.exp2(m_prev - m_new)
            p = jnp.exp2(s - m_new).astype(BF16)
            acc_ref[mp] = alpha * acc_ref[mp] + jnp.dot(vt, p, preferred_element_type=F32)
            m_ref[mp] = m_new

    scores(0, sa_ref)

    def pair(jj, carry):
        c = 2 * jj
        scores(c + 1, sb_ref)
        consume(c, sa_ref, False)
        scores(c + 2, sa_ref)
        consume(c + 1, sb_ref, False)
        return carry

    lax.fori_loop(0, lax.shift_right_logical(i, 1), pair, 0)

    @pl.when(i % 2 == 0)
    def _():
        consume(i, sa_ref, True)

    @pl.when(i % 2 == 1)
    def _():
        scores(i, sb_ref)
        consume(i - 1, sa_ref, False)
        consume(i, sb_ref, True)

    lq1, lk1, lq2, lk2 = (lam_ref[r:r + 1, :] for r in range(4))
    lam = (jnp.exp(jnp.sum(lq1 * lk1, axis=1, keepdims=True))
           - jnp.exp(jnp.sum(lq2 * lk2, axis=1, keepdims=True)) + lambda_init)
    d = DIFF_V_DIM
    ot = (acc_ref[0, 0:d, :] / acc_ref[0, d:d + 1, :]
          - lam * (acc_ref[1, 0:d, :] / acc_ref[1, d:d + 1, :]))
    ot = ot * lax.rsqrt(jnp.mean(ot * ot, axis=0, keepdims=True) + EPS)
    o_ref[...] = (ot.T * g_ref[...] * (1.0 - lambda_init)).astype(o_ref.dtype)


def _diffattn(lam_vecs, g_subln, qdt, kd, vdt, B, S, lambda_init):
    N, C = kd.shape
    tq = TQ_DIFF
    nq = S // tq
    return pl.pallas_call(
        functools.partial(_diff_kernel, lambda_init=lambda_init),
        out_shape=jax.ShapeDtypeStruct((N, C), BF16),
        grid=(B, DIFF_HEADS, nq),
        in_specs=[pl.BlockSpec((4, HEAD_DIM), lambda b, h, i: (0, 0)),
                  pl.BlockSpec((1, DIFF_V_DIM), lambda b, h, i: (0, 0)),
                  pl.BlockSpec((1, 1, LANES, tq), lambda b, h, i: (b, i, h, 0)),
                  pl.BlockSpec((S, LANES), lambda b, h, i: (b, h)),
                  pl.BlockSpec((1, nq, VT_ROWS, tq), lambda b, h, i: (b, 0, h, 0))],
        out_specs=pl.BlockSpec((tq, LANES), lambda b, h, i: (b * nq + i, h)),
        scratch_shapes=[pltpu.VMEM((2, tq, tq), F32), pltpu.VMEM((2, tq, tq), F32),
                        pltpu.VMEM((2, 1, tq), F32), pltpu.VMEM((2, VT_ROWS, tq), F32)],
        compiler_params=_cparams(("arbitrary", "arbitrary", "arbitrary")),
        name="diffattn",
    )(lam_vecs, g_subln, qdt, kd, vdt)


def _outproj_kernel(oa_ref, ob_ref, x_ref, mod_ref, wo_ref, bo_ref, g_ref, wr_ref, br_ref,
                    x1_ref, h2_ref, idx_ref, gate_ref, rank_ref, cnt_ref, carry_ref):
    i = pl.program_id(0)
    tm = x_ref.shape[0]
    half = oa_ref.shape[1]

    @pl.when(i == 0)
    def _():
        carry_ref[...] = jnp.zeros(carry_ref.shape, F32)

    gt1 = mod_ref[0, 2:3, :]
    sh2 = mod_ref[0, 3:4, :]
    sc2 = mod_ref[0, 4:5, :]
    mixed = (jnp.dot(oa_ref[...], wo_ref[0:half, :], preferred_element_type=F32)
             + jnp.dot(ob_ref[...], wo_ref[half:, :], preferred_element_type=F32) + bo_ref[...])
    x1 = x_ref[...] + gt1 * mixed
    x1_ref[...] = x1
    h2 = _rms(x1) * g_ref[...] * (1.0 + sc2) + sh2
    dn = (((1,), (1,)), ((), ()))
    h_hi = h2.astype(BF16)
    bits = lax.bitcast_convert_type(h_hi.astype(F32), jnp.int32)
    d_half = h2.shape[1] // 2
    h2_ref[...] = (lax.shift_right_logical(bits[:, :d_half], 16) | (bits[:, d_half:] & jnp.int32(-65536)))
    h_lo = (h2 - h_hi.astype(F32)).astype(BF16)
    w = wr_ref[...]
    w_hi = w.astype(BF16)
    w_lo = (w - w_hi.astype(F32)).astype(BF16)
    logits = (lax.dot_general(w_hi, h_hi, dn, preferred_element_type=F32)
              + lax.dot_general(w_hi, h_lo, dn, preferred_element_type=F32)
              + lax.dot_general(w_lo, h_hi, dn, preferred_element_type=F32)
              + br_ref[...])

    eidx = lax.broadcasted_iota(jnp.int32, logits.shape, 0)
    vals = logits
    onehots, top_vals, top_idx = [], [], []
    for _k in range(TOP_K):
        mx = jnp.max(vals, axis=0, keepdims=True)
        sel = jnp.min(jnp.where(vals == mx, eidx, N_EXPERTS), axis=0, keepdims=True)
        oh = eidx == sel
        onehots.append(oh)
        top_vals.append(mx)
        top_idx.append(sel)
        vals = jnp.where(oh, NEG_INF, vals)
    exps = [jnp.exp(v - top_vals[0]) for v in top_vals]
    denom = exps[0] + exps[1] + exps[2] + exps[3]
    gate_ref[...] = jnp.concatenate([e / denom for e in exps], axis=0)
    idx_ref[...] = jnp.concatenate(top_idx, axis=0)

    member = (onehots[0] | onehots[1] | onehots[2] | onehots[3])
    member_f = member.astype(F32)
    t_src = lax.broadcasted_iota(jnp.int32, (tm, tm), 0)
    t_dst = lax.broadcasted_iota(jnp.int32, (tm, tm), 1)
    before = (t_src < t_dst).astype(BF16)
    prefix = jnp.dot(member.astype(BF16), before, preferred_element_type=F32) + carry_ref[...]
    ranks = [jnp.sum(jnp.where(oh, prefix, 0.0), axis=0, keepdims=True) for oh in onehots]
    rank_ref[...] = jnp.concatenate(ranks, axis=0).astype(jnp.int32)
    carry_ref[...] = carry_ref[...] + jnp.sum(member_f, axis=1, keepdims=True)
    cnt_ref[...] = jnp.broadcast_to(carry_ref[...], cnt_ref.shape)


def _outproj(out_a, out_b, x2, mod3, w_out, b_out, g_ffn, wr_t, br_col, S):
    N, D = x2.shape
    tm = TM_OUT
    tiles_per_seq = S // tm
    row = lambda i: (i, 0)
    colb = lambda i: (0, i)
    const = lambda i: (0, 0)
    return pl.pallas_call(
        _outproj_kernel,
        out_shape=[jax.ShapeDtypeStruct((N, D), F32), jax.ShapeDtypeStruct((N, D // 2), jnp.int32),
                   jax.ShapeDtypeStruct((TOP_K, N), jnp.int32), jax.ShapeDtypeStruct((TOP_K, N), F32),
                   jax.ShapeDtypeStruct((TOP_K, N), jnp.int32), jax.ShapeDtypeStruct((N_EXPERTS, LANES), F32)],
        grid=(N // tm,),
        in_specs=[pl.BlockSpec((tm, out_a.shape[1]), row),
                  pl.BlockSpec((tm, out_b.shape[1]), row),
                  pl.BlockSpec((tm, D), row),
                  pl.BlockSpec((1, N_MOD, D), lambda i: (i // tiles_per_seq, 0, 0)),
                  pl.BlockSpec(w_out.shape, const),
                  pl.BlockSpec((1, D), const),
                  pl.BlockSpec((1, D), const),
                  pl.BlockSpec(wr_t.shape, const),
                  pl.BlockSpec((N_EXPERTS, 1), const)],
        out_specs=[pl.BlockSpec((tm, D), row), pl.BlockSpec((tm, D // 2), row),
                   pl.BlockSpec((TOP_K, tm), colb), pl.BlockSpec((TOP_K, tm), colb),
                   pl.BlockSpec((TOP_K, tm), colb), pl.BlockSpec((N_EXPERTS, LANES), const)],
        scratch_shapes=[pltpu.VMEM((N_EXPERTS, 1), F32)],
        compiler_params=_cparams(("arbitrary",)),
        name="outproj_router",
    )(out_a, out_b, x2, mod3, w_out, b_out, g_ffn, wr_t, br_col)


def _experts_kernel(first_ref, nblk_ref, x_ref, w1_ref, b1g_ref, b1l_ref, w2_ref, b2_ref, y_ref,
                    w1g_s, w1l_s, w2_s, t_s, xbuf, ybuf, xsem, ysem):
    e = pl.program_id(0)
    first = first_ref[e]
    n_blk = nblk_ref[e]
    bm = xbuf.shape[1]

    def x_copy(j, slot):
        rows = pl.ds(pl.multiple_of((first + j) * bm, bm), bm)
        return pltpu.make_async_copy(x_ref.at[rows], xbuf.at[slot], xsem.at[slot])

    def y_copy(j, slot):
        rows = pl.ds(pl.multiple_of((first + j) * bm, bm), bm)
        return pltpu.make_async_copy(ybuf.at[slot], y_ref.at[rows], ysem.at[slot])

    @pl.when(n_blk > 0)
    def _():
        x_copy(0, 0).start()
        n_db, ch, _ = t_s.shape
        half = ch // 2
        for c in range(w1_ref.shape[2] // ch):
            t = w1_ref[0, :, c * ch:(c + 1) * ch].T
            for db in range(n_db):
                t_s[db] = t[:, db * LANES:(db + 1) * LANES]
            for db in range(n_db):
                rows = slice(c * half, (c + 1) * half)
                cols = slice(db * LANES, (db + 1) * LANES)
                w1g_s[rows, cols] = t_s[db, pl.ds(0, half, stride=2), :].astype(BF16)
                w1l_s[rows, cols] = t_s[db, pl.ds(1, half, stride=2), :].astype(BF16)
        w2_s[...] = w2_ref[0].astype(BF16)

    def block(j, carry):
        slot = j & 1
        x_copy(j, slot).wait()

        @pl.when(j + 1 < n_blk)
        def _():
            x_copy(j + 1, 1 - slot).start()

        @pl.when(j >= 2)
        def _():
            y_copy(j - 2, slot).wait()

        dn = (((1,), (1,)), ((), ()))
        w = xbuf[slot]
        x = jnp.concatenate([lax.bitcast_convert_type(lax.shift_left(w, 16), F32),
                             lax.bitcast_convert_type(w & jnp.int32(-65536), F32)], axis=1).astype(BF16)
        ug = lax.dot_general(x, w1g_s[...], dn, preferred_element_type=F32) + b1g_ref[0]
        ul = lax.dot_general(x, w1l_s[...], dn, preferred_element_type=F32) + b1l_ref[0]
        glu = jnp.minimum(ug, SWIGLU_LIMIT)
        lin = jnp.clip(ul, -SWIGLU_LIMIT, SWIGLU_LIMIT)
        act = glu * jax.nn.sigmoid(SWIGLU_ALPHA * glu) * (lin + 1.0)
        ybuf[slot] = jnp.dot(act.astype(BF16), w2_s[...], preferred_element_type=F32) + b2_ref[0]
        y_copy(j, slot).start()
        return carry

    lax.fori_loop(0, n_blk, block, 0)

    @pl.when(n_blk >= 2)
    def _():
        y_copy(n_blk - 2, n_blk & 1).wait()

    @pl.when(n_blk >= 1)
    def _():
        y_copy(n_blk - 1, (n_blk - 1) & 1).wait()


def _experts(first_blk, n_blk, x_rows, w1, b1g, b1l, w2, b2):
    E, Fh, D = w2.shape
    bm = BLOCK_ROWS
    n_rows = x_rows.shape[0]
    chunk = 256
    wsel = lambda e, fb, nb: (e, 0, 0)
    return pl.pallas_call(
        _experts_kernel,
        out_shape=jax.ShapeDtypeStruct((n_rows, D), F32),
        grid_spec=pltpu.PrefetchScalarGridSpec(
            num_scalar_prefetch=2,
            grid=(E,),
            in_specs=[pl.BlockSpec(memory_space=pl.ANY),
                      pl.BlockSpec((1, D, 2 * Fh), wsel),
                      pl.BlockSpec((1, 1, Fh), wsel),
                      pl.BlockSpec((1, 1, Fh), wsel),
                      pl.BlockSpec((1, Fh, D), wsel),
                      pl.BlockSpec((1, 1, D), wsel)],
            out_specs=pl.BlockSpec(memory_space=pl.ANY),
            scratch_shapes=[pltpu.VMEM((Fh, D), BF16), pltpu.VMEM((Fh, D), BF16), pltpu.VMEM((Fh, D), BF16),
                            pltpu.VMEM((D // LANES, chunk, LANES), F32),
                            pltpu.VMEM((2, bm, D // 2), jnp.int32), pltpu.VMEM((2, bm, D), F32),
                            pltpu.SemaphoreType.DMA((2,)), pltpu.SemaphoreType.DMA((2,))]),
        compiler_params=_cparams(("arbitrary",), 56 * 1024 * 1024),
        name="experts",
    )(first_blk, n_blk, x_rows, w1, b1g, b1l, w2, b2)


SC_CORES = 2
SC_SUBCORES = 16
SC_CHUNK = 32


def _sc_gather_rows(table, idx):
    M = idx.shape[0]
    D = table.shape[1]
    workers = SC_CORES * SC_SUBCORES
    per_worker = M // workers
    assert M % workers == 0 and per_worker % SC_CHUNK == 0
    mesh = plsc.VectorSubcoreMesh(core_axis_name="c", subcore_axis_name="s")

    @functools.partial(
        pl.kernel, mesh=mesh,
        out_type=jax.ShapeDtypeStruct((M, D), table.dtype),
        scratch_types=[pltpu.VMEM((per_worker,), jnp.int32),
                       pltpu.VMEM((SC_CHUNK, D), table.dtype),
                       pltpu.SemaphoreType.DMA],
        name="sc_gather_rows")
    def gather(table_hbm, idx_hbm, out_hbm, idx_v, rows_v, sem):
        wid = lax.axis_index("s") * SC_CORES + lax.axis_index("c")
        base = wid * per_worker
        pltpu.sync_copy(idx_hbm.at[pl.ds(base, per_worker)], idx_v)

        @pl.loop(0, per_worker // SC_CHUNK)
        def _(j):
            off = pl.multiple_of(j * SC_CHUNK, SC_CHUNK)
            pltpu.async_copy(table_hbm.at[idx_v.at[pl.ds(off, SC_CHUNK)]], rows_v, sem).wait()
            pltpu.sync_copy(rows_v, out_hbm.at[pl.ds(base + off, SC_CHUNK)])

    return gather(table, idx)


SC_SCATTER_CHUNK = 128


def _sc_scatter_rows(rows, dest, n_rows):
    N, W = rows.shape
    workers = SC_CORES * SC_SUBCORES
    per_worker = N // workers
    chunks = per_worker // SC_SCATTER_CHUNK
    assert N % workers == 0 and per_worker % SC_SCATTER_CHUNK == 0
    dest3 = dest.reshape(TOP_K, N // SC_SCATTER_CHUNK, SC_SCATTER_CHUNK)
    mesh = plsc.VectorSubcoreMesh(core_axis_name="c", subcore_axis_name="s")

    @functools.partial(
        pl.kernel, mesh=mesh,
        out_type=jax.ShapeDtypeStruct((n_rows, W), rows.dtype),
        scratch_types=[pltpu.VMEM((TOP_K, chunks, SC_SCATTER_CHUNK), jnp.int32),
                       pltpu.VMEM((SC_SCATTER_CHUNK, W), rows.dtype)],
        name="sc_scatter_rows")
    def scatter(rows_hbm, dest_hbm, out_hbm, idx_v, rows_v):
        wid = lax.axis_index("s") * SC_CORES + lax.axis_index("c")
        for k in range(TOP_K):
            pltpu.sync_copy(dest_hbm.at[k, pl.ds(wid * chunks, chunks)], idx_v.at[k])

        @pl.loop(0, chunks)
        def _(j):
            start = pl.multiple_of(wid * per_worker + j * SC_SCATTER_CHUNK, SC_SCATTER_CHUNK)
            pltpu.sync_copy(rows_hbm.at[pl.ds(start, SC_SCATTER_CHUNK)], rows_v)
            for k in range(TOP_K):
                pltpu.sync_copy(rows_v, out_hbm.at[idx_v.at[k, j]])

    return scatter(rows, dest3)


def _combine_kernel(x1_ref, gate_ref, mod_ref, g_ref, y_ref, o_ref, *, final_norm):
    gate = gate_ref[...]
    moe = gate[:, 0:1] * y_ref[0]
    for k in range(1, TOP_K):
        moe = moe + gate[:, k:k + 1] * y_ref[k]
    gt2 = mod_ref[0, 5:6, :]
    x2 = x1_ref[...] + gt2 * moe
    o_ref[...] = _rms(x2) * g_ref[...] if final_norm else x2


def _combine(x1, gate_t, mod3, g_final, y_tok, S, final_norm):
    N, D = x1.shape
    tm = TM_ROWS
    tiles_per_seq = S // tm
    row = lambda i: (i, 0)
    return pl.pallas_call(
        functools.partial(_combine_kernel, final_norm=final_norm),
        out_shape=jax.ShapeDtypeStruct((N, D), F32),
        grid=(N // tm,),
        in_specs=[pl.BlockSpec((tm, D), row),
                  pl.BlockSpec((tm, TOP_K), row),
                  pl.BlockSpec((1, N_MOD, D), lambda i: (i // tiles_per_seq, 0, 0)),
                  pl.BlockSpec((1, D), lambda i: (0, 0)),
                  pl.BlockSpec((TOP_K, tm, D), lambda i: (0, i, 0))],
        out_specs=pl.BlockSpec((tm, D), row),
        compiler_params=_cparams(("arbitrary",)),
        name="combine",
    )(x1, gate_t, mod3, g_final, y_tok)


def _routing_tables(counts, idx, rank, n_rows):
    bm = BLOCK_ROWS
    counts = counts.astype(jnp.int32)
    padded = (counts + bm - 1) // bm * bm
    pends = jnp.cumsum(padded)
    pstarts = pends - padded
    experts = jnp.arange(N_EXPERTS, dtype=jnp.int32)
    dest = jnp.sum(jnp.where(idx[..., None] == experts, pstarts, 0), axis=-1) + rank
    return dest.astype(jnp.int32), (pstarts // bm).astype(jnp.int32), (padded // bm).astype(jnp.int32)


def _extended_in_weights(w_in, b_in):
    a_q = SWA_Q_HEADS * HEAD_DIM
    a_kv = SWA_KV_HEADS * HEAD_DIM
    b_w = DIFF_HEADS * DIFF_V_DIM
    spans = [(0, a_q)]
    for base in (a_q, a_q + a_kv):
        for j in range(SWA_KV_HEADS):
            spans += [(base + j * HEAD_DIM, base + (j + 1) * HEAD_DIM)] * 2
    spans.append((a_q + 2 * a_kv, a_q + 2 * a_kv + 3 * b_w))
    w_ext = jnp.concatenate([w_in[:, lo:hi] for lo, hi in spans], axis=1).astype(BF16)
    b_ext = jnp.concatenate([b_in[lo:hi] for lo, hi in spans]).reshape(1, -1)
    widths = (a_q, 2 * a_kv, 2 * a_kv, b_w, b_w, b_w)
    return w_ext, b_ext, widths


def kernel(x, c, positions, w_ada, b_ada, g_mix, w_in, b_in, attn_sinks, lambda_q1, lambda_k1, lambda_q2,
           lambda_k2, g_subln, w_out, b_out, g_ffn, w_router, b_router, w1, b1, w2, b2, g_final):
    B, S, D = x.shape
    N = B * S
    depth = w_ada.shape[0]
    assert TM_PROJ == TQ_DIFF, "the input projection writes q/v transposed per attention tile"
    n_rows = (N * TOP_K + N_EXPERTS * (BLOCK_ROWS - 1) + BLOCK_ROWS - 1) // BLOCK_ROWS * BLOCK_ROWS

    inv = 1.0 / (ROPE_THETA ** (jnp.arange(0, HEAD_DIM, 2, dtype=F32) / HEAD_DIM))
    inv_lane = jnp.tile(inv, LANES // (HEAD_DIM // 2)).reshape(1, LANES)
    pos2 = positions.reshape(N, 1)
    xcur = x.reshape(N, D)

    for layer in range(depth):
        last = layer == depth - 1
        lambda_init = 0.8 - 0.6 * math.exp(-0.3 * layer)
        mod3 = _adaln(c, w_ada[layer], b_ada[layer]).reshape(B, N_MOD, D)

        w_ext, b_ext, widths = _extended_in_weights(w_in[layer], b_in[layer])
        qa, ka2, va2, qdt, kd, vdt = _inproj(xcur, pos2, inv_lane, mod3, g_mix[layer].reshape(1, D),
                                             w_ext, b_ext, S, widths)
        out_a = _swa(attn_sinks[layer], qa, ka2, va2, B, S)
        lam_vecs = jnp.stack([lambda_q1[layer], lambda_k1[layer], lambda_q2[layer], lambda_k2[layer]])
        out_b = _diffattn(lam_vecs, g_subln[layer].reshape(1, DIFF_V_DIM), qdt, kd, vdt, B, S, lambda_init)

        x1, h2, idx, gate, rank, counts = _outproj(
            out_a, out_b, xcur, mod3, w_out[layer].astype(BF16), b_out[layer].reshape(1, D),
            g_ffn[layer].reshape(1, D), w_router[layer].T, b_router[layer].reshape(N_EXPERTS, 1), S)

        dest, first_blk, n_blk = _routing_tables(counts[:, 0], idx, rank, n_rows)
        x_rows = _sc_scatter_rows(h2, dest, n_rows)
        y_rows = _experts(first_blk, n_blk, x_rows, w1[layer],
                          b1[layer][:, None, 0::2], b1[layer][:, None, 1::2],
                          w2[layer], b2[layer][:, None, :])
        y_tok = _sc_gather_rows(y_rows, dest.reshape(-1)).reshape(TOP_K, N, D)
        xcur = _combine(x1, gate.T, mod3, g_final.reshape(1, D), y_tok, S, final_norm=last)
    return xcur.reshape(B, S, D)
```

```python
import functools
import math

import jax
import jax.numpy as jnp
from jax import lax
from jax.experimental import pallas as pl
from jax.experimental.pallas import tpu as pltpu
from jax.experimental.pallas import tpu_sc as plsc

HEAD_DIM = 64
SWA_Q_HEADS = 8
SWA_KV_HEADS = 2
SWA_GROUP = SWA_Q_HEADS // SWA_KV_HEADS
WINDOW = 128
DIFF_HEADS = 4
DIFF_V_DIM = 2 * HEAD_DIM
ROPE_THETA = 10000.0
N_EXPERTS = 32
TOP_K = 4
SWIGLU_ALPHA = 1.702
SWIGLU_LIMIT = 7.0
EPS = 1e-5
N_MOD = 6

LANES = 128
LANE_SHIFT = LANES.bit_length() - 1
TOP_K_SHIFT = TOP_K.bit_length() - 1
assert 1 << LANE_SHIFT == LANES and 1 << TOP_K_SHIFT == TOP_K
F32 = jnp.float32
BF16 = jnp.bfloat16
NEG_INF = float("-inf")

TM_PROJ = 512
TQ_SWA = 512
TQ_DIFF = 512
VT_ROWS = DIFF_V_DIM + 16
TM_OUT = 256
TM_ROWS = 256
BLOCK_ROWS = 256
VMEM_LIMIT = 48 * 1024 * 1024


def _cparams(sem, vmem=VMEM_LIMIT):
    return pltpu.CompilerParams(dimension_semantics=sem, vmem_limit_bytes=vmem)


def _adaln_kernel(ct_ref, w_ref, b_ref, o_ref):
    c = ct_ref[...]
    cond = c * jax.nn.sigmoid(c)
    w = w_ref[...]
    rows = [jnp.sum(w * cond[:, b:b + 1], axis=0, keepdims=True) for b in range(c.shape[1])]
    o_ref[...] = jnp.concatenate(rows, axis=0) + b_ref[...]


def _adaln(c, w_ada, b_ada):
    B, D = c.shape
    n_out = w_ada.shape[1]
    tn = 1024
    return pl.pallas_call(
        _adaln_kernel,
        out_shape=jax.ShapeDtypeStruct((B, n_out), F32),
        grid=(n_out // tn,),
        in_specs=[pl.BlockSpec((D, B), lambda j: (0, 0)),
                  pl.BlockSpec((D, tn), lambda j: (0, j)),
                  pl.BlockSpec((1, tn), lambda j: (0, j))],
        out_specs=pl.BlockSpec((B, tn), lambda j: (0, j)),
        compiler_params=_cparams(("arbitrary",)),
        name="adaln",
    )(c.T, w_ada, b_ada.reshape(1, n_out))


def _rms(x):
    return x * lax.rsqrt(jnp.mean(x * x, axis=-1, keepdims=True) + EPS)


HIGH_HALF = -65536


def _pack_bf16_pairs(v):
    bits = lax.bitcast_convert_type(v.astype(BF16).astype(F32), jnp.int32)
    half = v.shape[1] // 2
    return lax.shift_right_logical(bits[:, :half], 16) | (bits[:, half:] & jnp.int32(HIGH_HALF))


def _unpack_bf16_pairs(w):
    return jnp.concatenate([lax.bitcast_convert_type(lax.shift_left(w, 16), F32),
                            lax.bitcast_convert_type(w & jnp.int32(HIGH_HALF), F32)], axis=1)


def _inproj_kernel(x_ref, pos_ref, inv_ref, mod_ref, g_ref, w_ref, b_ref,
                   qa_ref, ka_ref, va_ref, qd_ref, kd_ref, vd_ref):
    x = x_ref[...]
    sh = mod_ref[0, 0:1, :]
    sc = mod_ref[0, 1:2, :]
    h = _rms(x) * g_ref[...] * (1.0 + sc) + sh
    proj = jnp.dot(h.astype(BF16), w_ref[...], preferred_element_type=F32) + b_ref[...]

    lane = lax.broadcasted_iota(jnp.int32, (1, LANES), 1)
    first_half = (lane & (HEAD_DIM - 1)) < (HEAD_DIM // 2)
    n_freq = HEAD_DIM // 2
    groups = LANES // n_freq
    tm = x.shape[0]
    rows = tm // groups
    group = lax.shift_right_logical(lane, n_freq.bit_length() - 1)
    pos = pos_ref[...].astype(F32)
    pos_q = pos[0:rows]
    for g in range(1, groups):
        pos_q = jnp.where(group == g, pos[g * rows:(g + 1) * rows], pos_q)
    ang_q = pos_q * inv_ref[...]

    def spread(table_q):
        blocks = []
        for g in range(groups):
            only = jnp.where(group == g, table_q, 0.0)
            full = only
            for r in range(1, groups):
                full = full + pltpu.roll(only, r * n_freq, axis=1)
            blocks.append(full)
        return jnp.concatenate(blocks, axis=0)

    cos = spread(jnp.cos(ang_q))
    sin = spread(jnp.sin(ang_q))
    sin_signed = jnp.where(first_half, -sin, sin)

    def rope(t):
        partner = jnp.where(first_half,
                            pltpu.roll(t, LANES - HEAD_DIM // 2, axis=1),
                            pltpu.roll(t, HEAD_DIM // 2, axis=1))
        return t * cos + partner * sin_signed

    def emit(out_ref, col0, width, rotary, scale, transposed):
        for j in range(width // LANES):
            t = proj[:, col0 + j * LANES: col0 + (j + 1) * LANES]
            if rotary:
                t = rope(t)
            if scale != 1.0:
                t = t * scale
            if transposed:
                rows = out_ref.shape[2] // (width // LANES)
                out_ref[0, 0, j * rows:j * rows + LANES, :] = t.T.astype(out_ref.dtype)
                if rows > LANES:
                    fill = lax.broadcasted_iota(jnp.int32, (rows - LANES, t.shape[0]), 0) == 0
                    out_ref[0, 0, j * rows + LANES:(j + 1) * rows, :] = fill.astype(out_ref.dtype)
            else:
                out_ref[:, j * LANES:(j + 1) * LANES] = t.astype(out_ref.dtype)

    swa_scale = 1.0 / math.sqrt(HEAD_DIM)
    diff_scale = math.log2(math.e) / math.sqrt(HEAD_DIM)
    col = 0
    for out_ref, width, rotary, scale, transposed in (
            (qa_ref, qa_ref.shape[1], True, swa_scale, False), (ka_ref, ka_ref.shape[1], True, 1.0, False),
            (va_ref, va_ref.shape[1], False, 1.0, False), (qd_ref, qd_ref.shape[2], True, diff_scale, True),
            (kd_ref, kd_ref.shape[1], True, 1.0, False),
            (vd_ref, vd_ref.shape[2] // VT_ROWS * LANES, False, 1.0, True)):
        emit(out_ref, col, width, rotary, scale, transposed)
        col += width


def _inproj(x2, pos2, inv_lane, mod3, g_mix, w_ext, b_ext, S, widths):
    N, D = x2.shape
    tm = TM_PROJ
    C = w_ext.shape[1]
    tiles_per_seq = S // tm
    row = lambda i: (i, 0)
    t_rows = (0, 0, 0, widths[3], 0, widths[5] // LANES * VT_ROWS)
    out_shape, out_specs = [], []
    for w, tr in zip(widths, t_rows):
        if tr:
            out_shape.append(jax.ShapeDtypeStruct((N // S, tiles_per_seq, tr, tm), BF16))
            out_specs.append(pl.BlockSpec((1, 1, tr, tm), lambda i: (i // tiles_per_seq, i % tiles_per_seq, 0, 0)))
        else:
            out_shape.append(jax.ShapeDtypeStruct((N, w), BF16))
            out_specs.append(pl.BlockSpec((tm, w), row))
    return pl.pallas_call(
        _inproj_kernel,
        out_shape=out_shape,
        grid=(N // tm,),
        in_specs=[pl.BlockSpec((tm, D), row),
                  pl.BlockSpec((tm, 1), row),
                  pl.BlockSpec((1, LANES), lambda i: (0, 0)),
                  pl.BlockSpec((1, N_MOD, D), lambda i: (i // tiles_per_seq, 0, 0)),
                  pl.BlockSpec((1, D), lambda i: (0, 0)),
                  pl.BlockSpec((D, C), lambda i: (0, 0)),
                  pl.BlockSpec((1, C), lambda i: (0, 0))],
        out_specs=out_specs,
        compiler_params=_cparams(("arbitrary",)),
        name="inproj",
    )(x2, pos2, inv_lane, mod3, g_mix, w_ext, b_ext)


def _swa_kernel(sink_ref, q_ref, kc_ref, kp_ref, vc_ref, vp_ref, o_ref):
    i = pl.program_id(1)
    tq = q_ref.shape[0]
    lane = lax.broadcasted_iota(jnp.int32, (1, LANES), 1)
    lo = lane < HEAD_DIM
    qi = lax.broadcasted_iota(jnp.int32, (WINDOW, 2 * WINDOW), 0) + WINDOW
    kj = lax.broadcasted_iota(jnp.int32, (WINDOW, 2 * WINDOW), 1)
    band = (qi - kj >= 0) & (qi - kj < WINDOW)
    dn = (((1,), (1,)), ((), ()))
    for c in range(tq // WINDOW):
        if c == 0:
            kcat = jnp.concatenate([kp_ref[...], kc_ref[0:WINDOW, :]], axis=0)
            vcat = jnp.concatenate([vp_ref[...], vc_ref[0:WINDOW, :]], axis=0)
            mask = band & (kj >= jnp.where(i > 0, 0, WINDOW))
        else:
            kcat = kc_ref[(c - 1) * WINDOW:(c + 1) * WINDOW, :]
            vcat = vc_ref[(c - 1) * WINDOW:(c + 1) * WINDOW, :]
            mask = band
        for j in range(SWA_KV_HEADS):
            kj2 = kcat[:, j * LANES:(j + 1) * LANES]
            vj2 = vcat[:, j * LANES:(j + 1) * LANES]
            zero = jnp.zeros_like(kj2)
            k_halves = (jnp.where(lo, kj2, zero), jnp.where(lo, zero, kj2))
            v_halves = (jnp.where(lo, vj2, zero), jnp.where(lo, zero, vj2))
            for p in range(SWA_GROUP // 2):
                g = j * (SWA_GROUP // 2) + p
                q = q_ref[c * WINDOW:(c + 1) * WINDOW, g * LANES:(g + 1) * LANES]
                out = jnp.zeros((WINDOW, LANES), F32)
                for half in range(2):
                    sink = sink_ref[2 * g + half]
                    s = lax.dot_general(q, k_halves[half], dn, preferred_element_type=F32)
                    s = jnp.where(mask, s, NEG_INF)
                    m = jnp.maximum(jnp.max(s, axis=1, keepdims=True), sink)
                    e = jnp.exp(s - m)
                    denom = jnp.sum(e, axis=1, keepdims=True) + jnp.exp(sink - m)
                    pv = jnp.dot(e.astype(BF16), v_halves[half], preferred_element_type=F32)
                    out = out + pv / denom
                o_ref[c * WINDOW:(c + 1) * WINDOW, g * LANES:(g + 1) * LANES] = out.astype(o_ref.dtype)


def _swa(sinks, qa, ka2, va2, B, S):
    N = qa.shape[0]
    tq = TQ_SWA
    nq = S // tq
    wpt = tq // WINDOW
    wps = S // WINDOW
    cur = lambda b, i: (b * nq + i, 0)
    prev = lambda b, i: (b * wps + jnp.maximum(i * wpt - 1, 0), 0)
    return pl.pallas_call(
        _swa_kernel,
        out_shape=jax.ShapeDtypeStruct((N, qa.shape[1]), BF16),
        grid=(B, nq),
        in_specs=[pl.BlockSpec(memory_space=pltpu.SMEM),
                  pl.BlockSpec((tq, qa.shape[1]), cur),
                  pl.BlockSpec((tq, ka2.shape[1]), cur),
                  pl.BlockSpec((WINDOW, ka2.shape[1]), prev),
                  pl.BlockSpec((tq, va2.shape[1]), cur),
                  pl.BlockSpec((WINDOW, va2.shape[1]), prev)],
        out_specs=pl.BlockSpec((tq, qa.shape[1]), cur),
        compiler_params=_cparams(("arbitrary", "arbitrary")),
        name="swa",
    )(sinks, qa, ka2, ka2, va2, va2)


def _diff_kernel(lam_ref, g_ref, qt_ref, k_ref, vt_ref, o_ref, sa_ref, sb_ref, m_ref, acc_ref, *, lambda_init):
    i = pl.program_id(2)
    tq = qt_ref.shape[3]
    tk = vt_ref.shape[3]
    qt = qt_ref[0, 0]
    lane = lax.broadcasted_iota(jnp.int32, (1, LANES), 1)
    lo = lane < HEAD_DIM
    m_ref[...] = jnp.full(m_ref.shape, NEG_INF, F32)
    acc_ref[...] = jnp.zeros(acc_ref.shape, F32)

    def scores(c, s_ref):
        k = k_ref[pl.ds(pl.multiple_of(c * tk, tk), tk), :]
        zero = jnp.zeros_like(k)
        s_ref[0] = jnp.dot(jnp.where(lo, k, zero), qt, preferred_element_type=F32)
        s_ref[1] = jnp.dot(jnp.where(lo, zero, k), qt, preferred_element_type=F32)

    def consume(c, s_ref, diagonal):
        vt = vt_ref[0, c]
        for mp in range(2):
            s = s_ref[mp]
            if diagonal:
                kpos = lax.broadcasted_iota(jnp.int32, (tk, tq), 0)
                qpos = lax.broadcasted_iota(jnp.int32, (tk, tq), 1)
                s = jnp.where(kpos <= qpos, s, NEG_INF)
            m_prev = m_ref[mp]
            m_new = jnp.maximum(m_prev, jnp.max(s, axis=0, keepdims=True))
            alpha = jnp.exp2(m_prev - m_new)
            p = jnp.exp2(s - m_new).astype(BF16)
            acc_ref[mp] = alpha * acc_ref[mp] + jnp.dot(vt, p, preferred_element_type=F32)
            m_ref[mp] = m_new

    scores(0, sa_ref)

    def pair(jj, carry):
        c = 2 * jj
        scores(c + 1, sb_ref)
        consume(c, sa_ref, False)
        scores(c + 2, sa_ref)
        consume(c + 1, sb_ref, False)
        return carry

    lax.fori_loop(0, lax.shift_right_logical(i, 1), pair, 0)

    @pl.when(i % 2 == 0)
    def _():
        consume(i, sa_ref, True)

    @pl.when(i % 2 == 1)
    def _():
        scores(i, sb_ref)
        consume(i - 1, sa_ref, False)
        consume(i, sb_ref, True)

    lq1, lk1, lq2, lk2 = (lam_ref[r:r + 1, :] for r in range(4))
    lam = (jnp.exp(jnp.sum(lq1 * lk1, axis=1, keepdims=True))
           - jnp.exp(jnp.sum(lq2 * lk2, axis=1, keepdims=True)) + lambda_init)
    d = DIFF_V_DIM
    ot = (acc_ref[0, 0:d, :] / acc_ref[0, d:d + 1, :]
          - lam * (acc_ref[1, 0:d, :] / acc_ref[1, d:d + 1, :]))
    ot = ot * lax.rsqrt(jnp.mean(ot * ot, axis=0, keepdims=True) + EPS)
    o_ref[...] = (ot.T * g_ref[...] * (1.0 - lambda_init)).astype(o_ref.dtype)


def _diffattn(lam_vecs, g_subln, qdt, kd, vdt, B, S, lambda_init):
    N, C = kd.shape
    tq = TQ_DIFF
    nq = S // tq
    return pl.pallas_call(
        functools.partial(_diff_kernel, lambda_init=lambda_init),
        out_shape=jax.ShapeDtypeStruct((N, C), BF16),
        grid=(B, DIFF_HEADS, nq),
        in_specs=[pl.BlockSpec((4, HEAD_DIM), lambda b, h, i: (0, 0)),
                  pl.BlockSpec((1, DIFF_V_DIM), lambda b, h, i: (0, 0)),
                  pl.BlockSpec((1, 1, LANES, tq), lambda b, h, i: (b, i, h, 0)),
                  pl.BlockSpec((S, LANES), lambda b, h, i: (b, h)),
                  pl.BlockSpec((1, nq, VT_ROWS, tq), lambda b, h, i: (b, 0, h, 0))],
        out_specs=pl.BlockSpec((tq, LANES), lambda b, h, i: (b * nq + i, h)),
        scratch_shapes=[pltpu.VMEM((2, tq, tq), F32), pltpu.VMEM((2, tq, tq), F32),
                        pltpu.VMEM((2, 1, tq), F32), pltpu.VMEM((2, VT_ROWS, tq), F32)],
        compiler_params=_cparams(("arbitrary", "arbitrary", "arbitrary")),
        name="diffattn",
    )(lam_vecs, g_subln, qdt, kd, vdt)


def _outproj_kernel(oa_ref, ob_ref, x_ref, mod_ref, wo_ref, bo_ref, g_ref, wr_ref, br_ref,
                    x1_ref, h2_ref, idx_ref, gate_ref, rank_ref, cnt_ref, carry_ref):
    i = pl.program_id(0)
    tm = x_ref.shape[0]
    half = oa_ref.shape[1]

    @pl.when(i == 0)
    def _():
        carry_ref[...] = jnp.zeros(carry_ref.shape, F32)

    gt1 = mod_ref[0, 2:3, :]
    sh2 = mod_ref[0, 3:4, :]
    sc2 = mod_ref[0, 4:5, :]
    mixed = (jnp.dot(oa_ref[...], wo_ref[0:half, :], preferred_element_type=F32)
             + jnp.dot(ob_ref[...], wo_ref[half:, :], preferred_element_type=F32) + bo_ref[...])
    x1 = x_ref[...] + gt1 * mixed
    x1_ref[...] = x1
    h2 = _rms(x1) * g_ref[...] * (1.0 + sc2) + sh2
    dn = (((1,), (1,)), ((), ()))
    h_hi = h2.astype(BF16)
    h2_ref[...] = _pack_bf16_pairs(h2)
    h_lo = (h2 - h_hi.astype(F32)).astype(BF16)
    w = wr_ref[...]
    w_hi = w.astype(BF16)
    w_lo = (w - w_hi.astype(F32)).astype(BF16)
    logits = (lax.dot_general(w_hi, h_hi, dn, preferred_element_type=F32)
              + lax.dot_general(w_hi, h_lo, dn, preferred_element_type=F32)
              + lax.dot_general(w_lo, h_hi, dn, preferred_element_type=F32)
              + br_ref[...])

    eidx = lax.broadcasted_iota(jnp.int32, logits.shape, 0)
    vals = logits
    onehots, top_vals, top_idx = [], [], []
    for _k in range(TOP_K):
        mx = jnp.max(vals, axis=0, keepdims=True)
        sel = jnp.min(jnp.where(vals == mx, eidx, N_EXPERTS), axis=0, keepdims=True)
        oh = eidx == sel
        onehots.append(oh)
        top_vals.append(mx)
        top_idx.append(sel)
        vals = jnp.where(oh, NEG_INF, vals)
    exps = [jnp.exp(v - top_vals[0]) for v in top_vals]
    denom = exps[0] + exps[1] + exps[2] + exps[3]
    gate_ref[...] = jnp.concatenate([e / denom for e in exps], axis=0)
    idx_ref[...] = jnp.concatenate(top_idx, axis=0)

    member = (onehots[0] | onehots[1] | onehots[2] | onehots[3])
    member_f = member.astype(F32)
    t_src = lax.broadcasted_iota(jnp.int32, (tm, tm), 0)
    t_dst = lax.broadcasted_iota(jnp.int32, (tm, tm), 1)
    before = (t_src < t_dst).astype(BF16)
    prefix = jnp.dot(member.astype(BF16), before, preferred_element_type=F32) + carry_ref[...]
    ranks = [jnp.sum(jnp.where(oh, prefix, 0.0), axis=0, keepdims=True) for oh in onehots]
    rank_ref[...] = jnp.concatenate(ranks, axis=0).astype(jnp.int32)
    carry_ref[...] = carry_ref[...] + jnp.sum(member_f, axis=1, keepdims=True)
    cnt_ref[...] = jnp.broadcast_to(carry_ref[...], cnt_ref.shape)


def _outproj(out_a, out_b, x2, mod3, w_out, b_out, g_ffn, wr_t, br_col, S):
    N, D = x2.shape
    tm = TM_OUT
    tiles_per_seq = S // tm
    row = lambda i: (i, 0)
    colb = lambda i: (0, i)
    const = lambda i: (0, 0)
    return pl.pallas_call(
        _outproj_kernel,
        out_shape=[jax.ShapeDtypeStruct((N, D), F32), jax.ShapeDtypeStruct((N, D // 2), jnp.int32),
                   jax.ShapeDtypeStruct((TOP_K, N), jnp.int32), jax.ShapeDtypeStruct((TOP_K, N), F32),
                   jax.ShapeDtypeStruct((TOP_K, N), jnp.int32), jax.ShapeDtypeStruct((N_EXPERTS, LANES), F32)],
        grid=(N // tm,),
        in_specs=[pl.BlockSpec((tm, out_a.shape[1]), row),
                  pl.BlockSpec((tm, out_b.shape[1]), row),
                  pl.BlockSpec((tm, D), row),
                  pl.BlockSpec((1, N_MOD, D), lambda i: (i // tiles_per_seq, 0, 0)),
                  pl.BlockSpec(w_out.shape, const),
                  pl.BlockSpec((1, D), const),
                  pl.BlockSpec((1, D), const),
                  pl.BlockSpec(wr_t.shape, const),
                  pl.BlockSpec((N_EXPERTS, 1), const)],
        out_specs=[pl.BlockSpec((tm, D), row), pl.BlockSpec((tm, D // 2), row),
                   pl.BlockSpec((TOP_K, tm), colb), pl.BlockSpec((TOP_K, tm), colb),
                   pl.BlockSpec((TOP_K, tm), colb), pl.BlockSpec((N_EXPERTS, LANES), const)],
        scratch_shapes=[pltpu.VMEM((N_EXPERTS, 1), F32)],
        compiler_params=_cparams(("arbitrary",)),
        name="outproj_router",
    )(out_a, out_b, x2, mod3, w_out, b_out, g_ffn, wr_t, br_col)


def _experts_kernel(first_ref, nblk_ref, x_ref, w1_ref, b1g_ref, b1l_ref, w2_ref, b2_ref, y_ref,
                    w1g_s, w1l_s, w2_s, t_s, xbuf, ybuf, xsem, ysem):
    e = pl.program_id(0)
    first = first_ref[e]
    n_blk = nblk_ref[e]
    bm = xbuf.shape[1]

    def x_copy(j, slot):
        rows = pl.ds(pl.multiple_of((first + j) * bm, bm), bm)
        return pltpu.make_async_copy(x_ref.at[rows], xbuf.at[slot], xsem.at[slot])

    def y_copy(j, slot):
        rows = pl.ds(pl.multiple_of((first + j) * bm, bm), bm)
        return pltpu.make_async_copy(ybuf.at[slot], y_ref.at[rows], ysem.at[slot])

    @pl.when(n_blk > 0)
    def _():
        x_copy(0, 0).start()
        n_db, ch, _ = t_s.shape
        half = ch // 2
        for c in range(w1_ref.shape[2] // ch):
            t = w1_ref[0, :, c * ch:(c + 1) * ch].T
            for db in range(n_db):
                t_s[db] = t[:, db * LANES:(db + 1) * LANES]
            for db in range(n_db):
                rows = slice(c * half, (c + 1) * half)
                cols = slice(db * LANES, (db + 1) * LANES)
                w1g_s[rows, cols] = t_s[db, pl.ds(0, half, stride=2), :].astype(BF16)
                w1l_s[rows, cols] = t_s[db, pl.ds(1, half, stride=2), :].astype(BF16)
        w2_s[...] = w2_ref[0].astype(BF16)

    def block(j, carry):
        slot = j & 1
        x_copy(j, slot).wait()

        @pl.when(j + 1 < n_blk)
        def _():
            x_copy(j + 1, 1 - slot).start()

        @pl.when(j >= 2)
        def _():
            y_copy(j - 2, slot).wait()

        dn = (((1,), (1,)), ((), ()))
        x = _unpack_bf16_pairs(xbuf[slot]).astype(BF16)
        ug = lax.dot_general(x, w1g_s[...], dn, preferred_element_type=F32) + b1g_ref[0]
        ul = lax.dot_general(x, w1l_s[...], dn, preferred_element_type=F32) + b1l_ref[0]
        glu = jnp.minimum(ug, SWIGLU_LIMIT)
        lin = jnp.clip(ul, -SWIGLU_LIMIT, SWIGLU_LIMIT)
        act = glu * jax.nn.sigmoid(SWIGLU_ALPHA * glu) * (lin + 1.0)
        y = jnp.dot(act.astype(BF16), w2_s[...], preferred_element_type=F32) + b2_ref[0]
        ybuf[slot] = _pack_bf16_pairs(y)
        y_copy(j, slot).start()
        return carry

    lax.fori_loop(0, n_blk, block, 0)

    @pl.when(n_blk >= 2)
    def _():
        y_copy(n_blk - 2, n_blk & 1).wait()

    @pl.when(n_blk >= 1)
    def _():
        y_copy(n_blk - 1, (n_blk - 1) & 1).wait()


def _experts(first_blk, n_blk, x_rows, w1, b1g, b1l, w2, b2):
    E, Fh, D = w2.shape
    bm = BLOCK_ROWS
    n_rows = x_rows.shape[0]
    chunk = 256
    wsel = lambda e, fb, nb: (e, 0, 0)
    return pl.pallas_call(
        _experts_kernel,
        out_shape=jax.ShapeDtypeStruct((n_rows, D // 2), jnp.int32),
        grid_spec=pltpu.PrefetchScalarGridSpec(
            num_scalar_prefetch=2,
            grid=(E,),
            in_specs=[pl.BlockSpec(memory_space=pl.ANY),
                      pl.BlockSpec((1, D, 2 * Fh), wsel),
                      pl.BlockSpec((1, 1, Fh), wsel),
                      pl.BlockSpec((1, 1, Fh), wsel),
                      pl.BlockSpec((1, Fh, D), wsel),
                      pl.BlockSpec((1, 1, D), wsel)],
            out_specs=pl.BlockSpec(memory_space=pl.ANY),
            scratch_shapes=[pltpu.VMEM((Fh, D), BF16), pltpu.VMEM((Fh, D), BF16), pltpu.VMEM((Fh, D), BF16),
                            pltpu.VMEM((D // LANES, chunk, LANES), F32),
                            pltpu.VMEM((2, bm, D // 2), jnp.int32), pltpu.VMEM((2, bm, D // 2), jnp.int32),
                            pltpu.SemaphoreType.DMA((2,)), pltpu.SemaphoreType.DMA((2,))]),
        compiler_params=_cparams(("arbitrary",), 56 * 1024 * 1024),
        name="experts",
    )(first_blk, n_blk, x_rows, w1, b1g, b1l, w2, b2)


SC_CORES = 2
SC_SUBCORES = 16
SC_CHUNK = 64


def _sc_gather_rows(table, idx):
    M = idx.shape[0]
    D = table.shape[1]
    workers = SC_CORES * SC_SUBCORES
    per_worker = M // workers
    n_chunks = per_worker // SC_CHUNK
    assert M % workers == 0 and per_worker % (2 * SC_CHUNK) == 0
    mesh = plsc.VectorSubcoreMesh(core_axis_name="c", subcore_axis_name="s")

    @functools.partial(
        pl.kernel, mesh=mesh,
        out_type=jax.ShapeDtypeStruct((M, D), table.dtype),
        scratch_types=[pltpu.VMEM((per_worker,), jnp.int32),
                       pltpu.VMEM((SC_CHUNK, D), table.dtype), pltpu.VMEM((SC_CHUNK, D), table.dtype),
                       pltpu.SemaphoreType.DMA, pltpu.SemaphoreType.DMA],
        name="sc_gather_rows")
    def gather(table_hbm, idx_hbm, out_hbm, idx_v, rows0, rows1, sem0, sem1):
        wid = lax.axis_index("s") * SC_CORES + lax.axis_index("c")
        base = wid * per_worker
        pltpu.sync_copy(idx_hbm.at[pl.ds(base, per_worker)], idx_v)

        def fetch(c, buf, sem):
            off = pl.multiple_of(c * SC_CHUNK, SC_CHUNK)
            return pltpu.make_async_copy(table_hbm.at[idx_v.at[pl.ds(off, SC_CHUNK)]], buf, sem)

        def flush(c, buf):
            off = pl.multiple_of(c * SC_CHUNK, SC_CHUNK)
            pltpu.sync_copy(buf, out_hbm.at[pl.ds(base + off, SC_CHUNK)])

        fetch(0, rows0, sem0).start()

        @pl.loop(0, n_chunks // 2)
        def _(jj):
            c = 2 * jj
            fetch(c + 1, rows1, sem1).start()
            fetch(c, rows0, sem0).wait()
            flush(c, rows0)

            @pl.when(c + 2 < n_chunks)
            def _():
                fetch(c + 2, rows0, sem0).start()

            fetch(c + 1, rows1, sem1).wait()
            flush(c + 1, rows1)

    return gather(table, idx)


SC_SCATTER_CHUNK = 128


def _sc_scatter_rows(rows, dest, n_rows):
    N, W = rows.shape
    workers = SC_CORES * SC_SUBCORES
    per_worker = N // workers
    chunks = per_worker // SC_SCATTER_CHUNK
    assert N % workers == 0 and per_worker % SC_SCATTER_CHUNK == 0
    dest3 = dest.reshape(TOP_K, N // SC_SCATTER_CHUNK, SC_SCATTER_CHUNK)
    mesh = plsc.VectorSubcoreMesh(core_axis_name="c", subcore_axis_name="s")

    @functools.partial(
        pl.kernel, mesh=mesh,
        out_type=jax.ShapeDtypeStruct((n_rows, W), rows.dtype),
        scratch_types=[pltpu.VMEM((TOP_K, chunks, SC_SCATTER_CHUNK), jnp.int32),
                       pltpu.VMEM((SC_SCATTER_CHUNK, W), rows.dtype)],
        name="sc_scatter_rows")
    def scatter(rows_hbm, dest_hbm, out_hbm, idx_v, rows_v):
        wid = lax.axis_index("s") * SC_CORES + lax.axis_index("c")
        for k in range(TOP_K):
            pltpu.sync_copy(dest_hbm.at[k, pl.ds(wid * chunks, chunks)], idx_v.at[k])

        @pl.loop(0, chunks)
        def _(j):
            start = pl.multiple_of(wid * per_worker + j * SC_SCATTER_CHUNK, SC_SCATTER_CHUNK)
            pltpu.sync_copy(rows_hbm.at[pl.ds(start, SC_SCATTER_CHUNK)], rows_v)
            for k in range(TOP_K):
                pltpu.sync_copy(rows_v, out_hbm.at[idx_v.at[k, j]])

    return scatter(rows, dest3)


def _combine_kernel(x1_ref, gate_ref, mod_ref, g_ref, y_ref, o_ref, *, final_norm):
    gate = gate_ref[...]
    moe = gate[:, 0:1] * _unpack_bf16_pairs(y_ref[0])
    for k in range(1, TOP_K):
        moe = moe + gate[:, k:k + 1] * _unpack_bf16_pairs(y_ref[k])
    gt2 = mod_ref[0, 5:6, :]
    x2 = x1_ref[...] + gt2 * moe
    o_ref[...] = _rms(x2) * g_ref[...] if final_norm else x2


def _combine(x1, gate_t, mod3, g_final, y_tok, S, final_norm):
    N, D = x1.shape
    tm = TM_ROWS
    tiles_per_seq = S // tm
    row = lambda i: (i, 0)
    return pl.pallas_call(
        functools.partial(_combine_kernel, final_norm=final_norm),
        out_shape=jax.ShapeDtypeStruct((N, D), F32),
        grid=(N // tm,),
        in_specs=[pl.BlockSpec((tm, D), row),
                  pl.BlockSpec((tm, TOP_K), row),
                  pl.BlockSpec((1, N_MOD, D), lambda i: (i // tiles_per_seq, 0, 0)),
                  pl.BlockSpec((1, D), lambda i: (0, 0)),
                  pl.BlockSpec((TOP_K, tm, D // 2), lambda i: (0, i, 0))],
        out_specs=pl.BlockSpec((tm, D), row),
        compiler_params=_cparams(("arbitrary",)),
        name="combine",
    )(x1, gate_t, mod3, g_final, y_tok)


def _routing_tables(counts, idx, rank, n_rows):
    bm = BLOCK_ROWS
    counts = counts.astype(jnp.int32)
    padded = (counts + bm - 1) // bm * bm
    pends = jnp.cumsum(padded)
    pstarts = pends - padded
    experts = jnp.arange(N_EXPERTS, dtype=jnp.int32)
    dest = jnp.sum(jnp.where(idx[..., None] == experts, pstarts, 0), axis=-1) + rank
    return dest.astype(jnp.int32), (pstarts // bm).astype(jnp.int32), (padded // bm).astype(jnp.int32)


def _extended_in_weights(w_in, b_in):
    a_q = SWA_Q_HEADS * HEAD_DIM
    a_kv = SWA_KV_HEADS * HEAD_DIM
    b_w = DIFF_HEADS * DIFF_V_DIM
    spans = [(0, a_q)]
    for base in (a_q, a_q + a_kv):
        for j in range(SWA_KV_HEADS):
            spans += [(base + j * HEAD_DIM, base + (j + 1) * HEAD_DIM)] * 2
    spans.append((a_q + 2 * a_kv, a_q + 2 * a_kv + 3 * b_w))
    w_ext = jnp.concatenate([w_in[:, lo:hi] for lo, hi in spans], axis=1).astype(BF16)
    b_ext = jnp.concatenate([b_in[lo:hi] for lo, hi in spans]).reshape(1, -1)
    widths = (a_q, 2 * a_kv, 2 * a_kv, b_w, b_w, b_w)
    return w_ext, b_ext, widths


def kernel(x, c, positions, w_ada, b_ada, g_mix, w_in, b_in, attn_sinks, lambda_q1, lambda_k1, lambda_q2,
           lambda_k2, g_subln, w_out, b_out, g_ffn, w_router, b_router, w1, b1, w2, b2, g_final):
    B, S, D = x.shape
    N = B * S
    depth = w_ada.shape[0]
    assert TM_PROJ == TQ_DIFF, "the input projection writes q/v transposed per attention tile"
    n_rows = (N * TOP_K + N_EXPERTS * (BLOCK_ROWS - 1) + BLOCK_ROWS - 1) // BLOCK_ROWS * BLOCK_ROWS

    inv = 1.0 / (ROPE_THETA ** (jnp.arange(0, HEAD_DIM, 2, dtype=F32) / HEAD_DIM))
    inv_lane = jnp.tile(inv, LANES // (HEAD_DIM // 2)).reshape(1, LANES)
    pos2 = positions.reshape(N, 1)
    xcur = x.reshape(N, D)

    for layer in range(depth):
        last = layer == depth - 1
        lambda_init = 0.8 - 0.6 * math.exp(-0.3 * layer)
        mod3 = _adaln(c, w_ada[layer], b_ada[layer]).reshape(B, N_MOD, D)

        w_ext, b_ext, widths = _extended_in_weights(w_in[layer], b_in[layer])
        qa, ka2, va2, qdt, kd, vdt = _inproj(xcur, pos2, inv_lane, mod3, g_mix[layer].reshape(1, D),
                                             w_ext, b_ext, S, widths)
        out_a = _swa(attn_sinks[layer], qa, ka2, va2, B, S)
        lam_vecs = jnp.stack([lambda_q1[layer], lambda_k1[layer], lambda_q2[layer], lambda_k2[layer]])
        out_b = _diffattn(lam_vecs, g_subln[layer].reshape(1, DIFF_V_DIM), qdt, kd, vdt, B, S, lambda_init)

        x1, h2, idx, gate, rank, counts = _outproj(
            out_a, out_b, xcur, mod3, w_out[layer].astype(BF16), b_out[layer].reshape(1, D),
            g_ffn[layer].reshape(1, D), w_router[layer].T, b_router[layer].reshape(N_EXPERTS, 1), S)

        dest, first_blk, n_blk = _routing_tables(counts[:, 0], idx, rank, n_rows)
        x_rows = _sc_scatter_rows(h2, dest, n_rows)
        y_rows = _experts(first_blk, n_blk, x_rows, w1[layer],
                          b1[layer][:, None, 0::2], b1[layer][:, None, 1::2],
                          w2[layer], b2[layer][:, None, :])
        y_tok = _sc_gather_rows(y_rows, dest.reshape(-1)).reshape(TOP_K, N, D // 2)
        xcur = _combine(x1, gate.T, mod3, g_final.reshape(1, D), y_tok, S, final_norm=last)
    return xcur.reshape(B, S, D)
```

```python
import functools
import math

import jax
import jax.numpy as jnp
from jax import lax
from jax.experimental import pallas as pl
from jax.experimental.pallas import tpu as pltpu
from jax.experimental.pallas import tpu_sc as plsc

HEAD_DIM = 64
SWA_Q_HEADS = 8
SWA_KV_HEADS = 2
SWA_GROUP = SWA_Q_HEADS // SWA_KV_HEADS
WINDOW = 128
DIFF_HEADS = 4
DIFF_V_DIM = 2 * HEAD_DIM
ROPE_THETA = 10000.0
N_EXPERTS = 32
TOP_K = 4
SWIGLU_ALPHA = 1.702
SWIGLU_LIMIT = 7.0
EPS = 1e-5
N_MOD = 6

LANES = 128
LANE_SHIFT = LANES.bit_length() - 1
TOP_K_SHIFT = TOP_K.bit_length() - 1
assert 1 << LANE_SHIFT == LANES and 1 << TOP_K_SHIFT == TOP_K
F32 = jnp.float32
BF16 = jnp.bfloat16
NEG_INF = float("-inf")

TM_PROJ = 512
TQ_SWA = 512
TQ_DIFF = 512
VT_ROWS = DIFF_V_DIM + 16
TM_OUT = 512
TM_ROWS = 256
BLOCK_ROWS = 512
VMEM_LIMIT = 48 * 1024 * 1024


def _cparams(sem, vmem=VMEM_LIMIT):
    return pltpu.CompilerParams(dimension_semantics=sem, vmem_limit_bytes=vmem)


def _adaln_kernel(ct_ref, w_ref, b_ref, o_ref):
    c = ct_ref[...]
    cond = c * jax.nn.sigmoid(c)
    w = w_ref[...]
    rows = [jnp.sum(w * cond[:, b:b + 1], axis=0, keepdims=True) for b in range(c.shape[1])]
    o_ref[...] = jnp.concatenate(rows, axis=0) + b_ref[...]


def _adaln(c, w_ada, b_ada):
    B, D = c.shape
    n_out = w_ada.shape[1]
    tn = 1024
    return pl.pallas_call(
        _adaln_kernel,
        out_shape=jax.ShapeDtypeStruct((B, n_out), F32),
        grid=(n_out // tn,),
        in_specs=[pl.BlockSpec((D, B), lambda j: (0, 0)),
                  pl.BlockSpec((D, tn), lambda j: (0, j)),
                  pl.BlockSpec((1, tn), lambda j: (0, j))],
        out_specs=pl.BlockSpec((B, tn), lambda j: (0, j)),
        compiler_params=_cparams(("arbitrary",)),
        name="adaln",
    )(c.T, w_ada, b_ada.reshape(1, n_out))


def _rms(x):
    return x * lax.rsqrt(jnp.mean(x * x, axis=-1, keepdims=True) + EPS)


HIGH_HALF = -65536


def _pack_bf16_pairs(v):
    bits = lax.bitcast_convert_type(v.astype(BF16).astype(F32), jnp.int32)
    half = v.shape[1] // 2
    return lax.shift_right_logical(bits[:, :half], 16) | (bits[:, half:] & jnp.int32(HIGH_HALF))


def _unpack_bf16_pairs(w):
    return jnp.concatenate([lax.bitcast_convert_type(lax.shift_left(w, 16), F32),
                            lax.bitcast_convert_type(w & jnp.int32(HIGH_HALF), F32)], axis=1)


def _inproj_kernel(x_ref, pos_ref, inv_ref, mod_ref, g_ref, w_ref, b_ref,
                   qa_ref, ka_ref, va_ref, qd_ref, kd_ref, vd_ref):
    x = x_ref[...]
    sh = mod_ref[0, 0:1, :]
    sc = mod_ref[0, 1:2, :]
    h = _rms(x) * g_ref[...] * (1.0 + sc) + sh
    proj = jnp.dot(h.astype(BF16), w_ref[...], preferred_element_type=F32) + b_ref[...]

    lane = lax.broadcasted_iota(jnp.int32, (1, LANES), 1)
    first_half = (lane & (HEAD_DIM - 1)) < (HEAD_DIM // 2)
    n_freq = HEAD_DIM // 2
    groups = LANES // n_freq
    tm = x.shape[0]
    rows = tm // groups
    group = lax.shift_right_logical(lane, n_freq.bit_length() - 1)
    pos = pos_ref[...].astype(F32)
    pos_q = pos[0:rows]
    for g in range(1, groups):
        pos_q = jnp.where(group == g, pos[g * rows:(g + 1) * rows], pos_q)
    ang_q = pos_q * inv_ref[...]

    def spread(table_q):
        blocks = []
        for g in range(groups):
            only = jnp.where(group == g, table_q, 0.0)
            full = only
            for r in range(1, groups):
                full = full + pltpu.roll(only, r * n_freq, axis=1)
            blocks.append(full)
        return jnp.concatenate(blocks, axis=0)

    cos = spread(jnp.cos(ang_q))
    sin = spread(jnp.sin(ang_q))
    sin_signed = jnp.where(first_half, -sin, sin)

    def rope(t):
        partner = jnp.where(first_half,
                            pltpu.roll(t, LANES - HEAD_DIM // 2, axis=1),
                            pltpu.roll(t, HEAD_DIM // 2, axis=1))
        return t * cos + partner * sin_signed

    def emit(out_ref, col0, width, rotary, scale, transposed):
        for j in range(width // LANES):
            t = proj[:, col0 + j * LANES: col0 + (j + 1) * LANES]
            if rotary:
                t = rope(t)
            if scale != 1.0:
                t = t * scale
            if transposed:
                rows = out_ref.shape[2] // (width // LANES)
                out_ref[0, 0, j * rows:j * rows + LANES, :] = t.T.astype(out_ref.dtype)
                if rows > LANES:
                    fill = lax.broadcasted_iota(jnp.int32, (rows - LANES, t.shape[0]), 0) == 0
                    out_ref[0, 0, j * rows + LANES:(j + 1) * rows, :] = fill.astype(out_ref.dtype)
            else:
                out_ref[:, j * LANES:(j + 1) * LANES] = t.astype(out_ref.dtype)

    swa_scale = 1.0 / math.sqrt(HEAD_DIM)
    diff_scale = math.log2(math.e) / math.sqrt(HEAD_DIM)
    col = 0
    for out_ref, width, rotary, scale, transposed in (
            (qa_ref, qa_ref.shape[1], True, swa_scale, False), (ka_ref, ka_ref.shape[1], True, 1.0, False),
            (va_ref, va_ref.shape[1], False, 1.0, False), (qd_ref, qd_ref.shape[2], True, diff_scale, True),
            (kd_ref, kd_ref.shape[1], True, 1.0, False),
            (vd_ref, vd_ref.shape[2] // VT_ROWS * LANES, False, 1.0, True)):
        emit(out_ref, col, width, rotary, scale, transposed)
        col += width


def _inproj(x2, pos2, inv_lane, mod3, g_mix, w_ext, b_ext, S, widths):
    N, D = x2.shape
    tm = TM_PROJ
    C = w_ext.shape[1]
    tiles_per_seq = S // tm
    row = lambda i: (i, 0)
    t_rows = (0, 0, 0, widths[3], 0, widths[5] // LANES * VT_ROWS)
    out_shape, out_specs = [], []
    for w, tr in zip(widths, t_rows):
        if tr:
            out_shape.append(jax.ShapeDtypeStruct((N // S, tiles_per_seq, tr, tm), BF16))
            out_specs.append(pl.BlockSpec((1, 1, tr, tm), lambda i: (i // tiles_per_seq, i % tiles_per_seq, 0, 0)))
        else:
            out_shape.append(jax.ShapeDtypeStruct((N, w), BF16))
            out_specs.append(pl.BlockSpec((tm, w), row))
    return pl.pallas_call(
        _inproj_kernel,
        out_shape=out_shape,
        grid=(N // tm,),
        in_specs=[pl.BlockSpec((tm, D), row),
                  pl.BlockSpec((tm, 1), row),
                  pl.BlockSpec((1, LANES), lambda i: (0, 0)),
                  pl.BlockSpec((1, N_MOD, D), lambda i: (i // tiles_per_seq, 0, 0)),
                  pl.BlockSpec((1, D), lambda i: (0, 0)),
                  pl.BlockSpec((D, C), lambda i: (0, 0)),
                  pl.BlockSpec((1, C), lambda i: (0, 0))],
        out_specs=out_specs,
        compiler_params=_cparams(("arbitrary",)),
        name="inproj",
    )(x2, pos2, inv_lane, mod3, g_mix, w_ext, b_ext)


def _swa_kernel(sink_ref, q_ref, kc_ref, kp_ref, vc_ref, vp_ref, o_ref):
    i = pl.program_id(1)
    tq = q_ref.shape[0]
    lane = lax.broadcasted_iota(jnp.int32, (1, LANES), 1)
    lo = lane < HEAD_DIM
    qi = lax.broadcasted_iota(jnp.int32, (WINDOW, 2 * WINDOW), 0) + WINDOW
    kj = lax.broadcasted_iota(jnp.int32, (WINDOW, 2 * WINDOW), 1)
    band = (qi - kj >= 0) & (qi - kj < WINDOW)
    dn = (((1,), (1,)), ((), ()))
    for c in range(tq // WINDOW):
        if c == 0:
            kcat = jnp.concatenate([kp_ref[...], kc_ref[0:WINDOW, :]], axis=0)
            vcat = jnp.concatenate([vp_ref[...], vc_ref[0:WINDOW, :]], axis=0)
            mask = band & (kj >= jnp.where(i > 0, 0, WINDOW))
        else:
            kcat = kc_ref[(c - 1) * WINDOW:(c + 1) * WINDOW, :]
            vcat = vc_ref[(c - 1) * WINDOW:(c + 1) * WINDOW, :]
            mask = band
        for j in range(SWA_KV_HEADS):
            kj2 = kcat[:, j * LANES:(j + 1) * LANES]
            vj2 = vcat[:, j * LANES:(j + 1) * LANES]
            zero = jnp.zeros_like(kj2)
            k_halves = (jnp.where(lo, kj2, zero), jnp.where(lo, zero, kj2))
            v_halves = (jnp.where(lo, vj2, zero), jnp.where(lo, zero, vj2))
            for p in range(SWA_GROUP // 2):
                g = j * (SWA_GROUP // 2) + p
                q = q_ref[c * WINDOW:(c + 1) * WINDOW, g * LANES:(g + 1) * LANES]
                out = jnp.zeros((WINDOW, LANES), F32)
                for half in range(2):
                    sink = sink_ref[2 * g + half]
                    s = lax.dot_general(q, k_halves[half], dn, preferred_element_type=F32)
                    s = jnp.where(mask, s, NEG_INF)
                    m = jnp.maximum(jnp.max(s, axis=1, keepdims=True), sink)
                    e = jnp.exp(s - m)
                    denom = jnp.sum(e, axis=1, keepdims=True) + jnp.exp(sink - m)
                    pv = jnp.dot(e.astype(BF16), v_halves[half], preferred_element_type=F32)
                    out = out + pv / denom
                o_ref[c * WINDOW:(c + 1) * WINDOW, g * LANES:(g + 1) * LANES] = out.astype(o_ref.dtype)


def _swa(sinks, qa, ka2, va2, B, S):
    N = qa.shape[0]
    tq = TQ_SWA
    nq = S // tq
    wpt = tq // WINDOW
    wps = S // WINDOW
    cur = lambda b, i: (b * nq + i, 0)
    prev = lambda b, i: (b * wps + jnp.maximum(i * wpt - 1, 0), 0)
    return pl.pallas_call(
        _swa_kernel,
        out_shape=jax.ShapeDtypeStruct((N, qa.shape[1]), BF16),
        grid=(B, nq),
        in_specs=[pl.BlockSpec(memory_space=pltpu.SMEM),
                  pl.BlockSpec((tq, qa.shape[1]), cur),
                  pl.BlockSpec((tq, ka2.shape[1]), cur),
                  pl.BlockSpec((WINDOW, ka2.shape[1]), prev),
                  pl.BlockSpec((tq, va2.shape[1]), cur),
                  pl.BlockSpec((WINDOW, va2.shape[1]), prev)],
        out_specs=pl.BlockSpec((tq, qa.shape[1]), cur),
        compiler_params=_cparams(("arbitrary", "arbitrary")),
        name="swa",
    )(sinks, qa, ka2, ka2, va2, va2)


def _diff_kernel(lam_ref, g_ref, qt_ref, k_ref, vt_ref, o_ref, sa_ref, sb_ref, m_ref, acc_ref, *, lambda_init):
    i = pl.program_id(2)
    tq = qt_ref.shape[3]
    tk = vt_ref.shape[3]
    qt = qt_ref[0, 0]
    lane = lax.broadcasted_iota(jnp.int32, (1, LANES), 1)
    lo = lane < HEAD_DIM
    m_ref[...] = jnp.full(m_ref.shape, NEG_INF, F32)
    acc_ref[...] = jnp.zeros(acc_ref.shape, F32)

    def scores(c, s_ref):
        k = k_ref[pl.ds(pl.multiple_of(c * tk, tk), tk), :]
        zero = jnp.zeros_like(k)
        s_ref[0] = jnp.dot(jnp.where(lo, k, zero), qt, preferred_element_type=F32)
        s_ref[1] = jnp.dot(jnp.where(lo, zero, k), qt, preferred_element_type=F32)

    def consume(c, s_ref, diagonal):
        vt = vt_ref[0, c]
        for mp in range(2):
            s = s_ref[mp]
            if diagonal:
                kpos = lax.broadcasted_iota(jnp.int32, (tk, tq), 0)
                qpos = lax.broadcasted_iota(jnp.int32, (tk, tq), 1)
                s = jnp.where(kpos <= qpos, s, NEG_INF)
            m_prev = m_ref[mp]
            m_new = jnp.maximum(m_prev, jnp.max(s, axis=0, keepdims=True))
            alpha = jnp.exp2(m_prev - m_new)
            p = jnp.exp2(s - m_new).astype(BF16)
            acc_ref[mp] = alpha * acc_ref[mp] + jnp.dot(vt, p, preferred_element_type=F32)
            m_ref[mp] = m_new

    scores(0, sa_ref)

    def pair(jj, carry):
        c = 2 * jj
        scores(c + 1, sb_ref)
        consume(c, sa_ref, False)
        scores(c + 2, sa_ref)
        consume(c + 1, sb_ref, False)
        return carry

    lax.fori_loop(0, lax.shift_right_logical(i, 1), pair, 0)

    @pl.when(i % 2 == 0)
    def _():
        consume(i, sa_ref, True)

    @pl.when(i % 2 == 1)
    def _():
        scores(i, sb_ref)
        consume(i - 1, sa_ref, False)
        consume(i, sb_ref, True)

    lq1, lk1, lq2, lk2 = (lam_ref[r:r + 1, :] for r in range(4))
    lam = (jnp.exp(jnp.sum(lq1 * lk1, axis=1, keepdims=True))
           - jnp.exp(jnp.sum(lq2 * lk2, axis=1, keepdims=True)) + lambda_init)
    d = DIFF_V_DIM
    ot = (acc_ref[0, 0:d, :] / acc_ref[0, d:d + 1, :]
          - lam * (acc_ref[1, 0:d, :] / acc_ref[1, d:d + 1, :]))
    ot = ot * lax.rsqrt(jnp.mean(ot * ot, axis=0, keepdims=True) + EPS)
    o_ref[...] = (ot.T * g_ref[...] * (1.0 - lambda_init)).astype(o_ref.dtype)


def _diffattn(lam_vecs, g_subln, qdt, kd, vdt, B, S, lambda_init):
    N, C = kd.shape
    tq = TQ_DIFF
    nq = S // tq
    return pl.pallas_call(
        functools.partial(_diff_kernel, lambda_init=lambda_init),
        out_shape=jax.ShapeDtypeStruct((N, C), BF16),
        grid=(B, DIFF_HEADS, nq),
        in_specs=[pl.BlockSpec((4, HEAD_DIM), lambda b, h, i: (0, 0)),
                  pl.BlockSpec((1, DIFF_V_DIM), lambda b, h, i: (0, 0)),
                  pl.BlockSpec((1, 1, LANES, tq), lambda b, h, i: (b, i, h, 0)),
                  pl.BlockSpec((S, LANES), lambda b, h, i: (b, h)),
                  pl.BlockSpec((1, nq, VT_ROWS, tq), lambda b, h, i: (b, 0, h, 0))],
        out_specs=pl.BlockSpec((tq, LANES), lambda b, h, i: (b * nq + i, h)),
        scratch_shapes=[pltpu.VMEM((2, tq, tq), F32), pltpu.VMEM((2, tq, tq), F32),
                        pltpu.VMEM((2, 1, tq), F32), pltpu.VMEM((2, VT_ROWS, tq), F32)],
        compiler_params=_cparams(("arbitrary", "arbitrary", "arbitrary")),
        name="diffattn",
    )(lam_vecs, g_subln, qdt, kd, vdt)


def _outproj_kernel(oa_ref, ob_ref, x_ref, mod_ref, wo_ref, bo_ref, g_ref, wr_ref, br_ref,
                    x1_ref, h2_ref, idx_ref, gate_ref, rank_ref, cnt_ref, carry_ref):
    i = pl.program_id(0)
    tm = x_ref.shape[0]
    half = oa_ref.shape[1]

    @pl.when(i == 0)
    def _():
        carry_ref[...] = jnp.zeros(carry_ref.shape, F32)

    gt1 = mod_ref[0, 2:3, :]
    sh2 = mod_ref[0, 3:4, :]
    sc2 = mod_ref[0, 4:5, :]
    mixed = (jnp.dot(oa_ref[...], wo_ref[0:half, :], preferred_element_type=F32)
             + jnp.dot(ob_ref[...], wo_ref[half:, :], preferred_element_type=F32) + bo_ref[...])
    x1 = x_ref[...] + gt1 * mixed
    x1_ref[...] = x1
    h2 = _rms(x1) * g_ref[...] * (1.0 + sc2) + sh2
    dn = (((1,), (1,)), ((), ()))
    h_hi = h2.astype(BF16)
    h2_ref[...] = _pack_bf16_pairs(h2)
    h_lo = (h2 - h_hi.astype(F32)).astype(BF16)
    w = wr_ref[...]
    w_hi = w.astype(BF16)
    w_lo = (w - w_hi.astype(F32)).astype(BF16)
    logits = (lax.dot_general(w_hi, h_hi, dn, preferred_element_type=F32)
              + lax.dot_general(w_hi, h_lo, dn, preferred_element_type=F32)
              + lax.dot_general(w_lo, h_hi, dn, preferred_element_type=F32)
              + br_ref[...])

    eidx = lax.broadcasted_iota(jnp.int32, logits.shape, 0)
    vals = logits
    onehots, top_vals, top_idx = [], [], []
    for _k in range(TOP_K):
        mx = jnp.max(vals, axis=0, keepdims=True)
        sel = jnp.min(jnp.where(vals == mx, eidx, N_EXPERTS), axis=0, keepdims=True)
        oh = eidx == sel
        onehots.append(oh)
        top_vals.append(mx)
        top_idx.append(sel)
        vals = jnp.where(oh, NEG_INF, vals)
    exps = [jnp.exp(v - top_vals[0]) for v in top_vals]
    denom = exps[0] + exps[1] + exps[2] + exps[3]
    gate_ref[...] = jnp.concatenate([e / denom for e in exps], axis=0)
    idx_ref[...] = jnp.concatenate(top_idx, axis=0)

    member = (onehots[0] | onehots[1] | onehots[2] | onehots[3])
    member_f = member.astype(F32)
    t_src = lax.broadcasted_iota(jnp.int32, (tm, tm), 0)
    t_dst = lax.broadcasted_iota(jnp.int32, (tm, tm), 1)
    before = (t_src < t_dst).astype(BF16)
    prefix = jnp.dot(member.astype(BF16), before, preferred_element_type=F32) + carry_ref[...]
    ranks = [jnp.sum(jnp.where(oh, prefix, 0.0), axis=0, keepdims=True) for oh in onehots]
    rank_ref[...] = jnp.concatenate(ranks, axis=0).astype(jnp.int32)
    carry_ref[...] = carry_ref[...] + jnp.sum(member_f, axis=1, keepdims=True)
    cnt_ref[...] = jnp.broadcast_to(carry_ref[...], cnt_ref.shape)


def _outproj(out_a, out_b, x2, mod3, w_out, b_out, g_ffn, wr_t, br_col, S):
    N, D = x2.shape
    tm = TM_OUT
    tiles_per_seq = S // tm
    row = lambda i: (i, 0)
    colb = lambda i: (0, i)
    const = lambda i: (0, 0)
    return pl.pallas_call(
        _outproj_kernel,
        out_shape=[jax.ShapeDtypeStruct((N, D), F32), jax.ShapeDtypeStruct((N, D // 2), jnp.int32),
                   jax.ShapeDtypeStruct((TOP_K, N), jnp.int32), jax.ShapeDtypeStruct((TOP_K, N), F32),
                   jax.ShapeDtypeStruct((TOP_K, N), jnp.int32), jax.ShapeDtypeStruct((N_EXPERTS, LANES), F32)],
        grid=(N // tm,),
        in_specs=[pl.BlockSpec((tm, out_a.shape[1]), row),
                  pl.BlockSpec((tm, out_b.shape[1]), row),
                  pl.BlockSpec((tm, D), row),
                  pl.BlockSpec((1, N_MOD, D), lambda i: (i // tiles_per_seq, 0, 0)),
                  pl.BlockSpec(w_out.shape, const),
                  pl.BlockSpec((1, D), const),
                  pl.BlockSpec((1, D), const),
                  pl.BlockSpec(wr_t.shape, const),
                  pl.BlockSpec((N_EXPERTS, 1), const)],
        out_specs=[pl.BlockSpec((tm, D), row), pl.BlockSpec((tm, D // 2), row),
                   pl.BlockSpec((TOP_K, tm), colb), pl.BlockSpec((TOP_K, tm), colb),
                   pl.BlockSpec((TOP_K, tm), colb), pl.BlockSpec((N_EXPERTS, LANES), const)],
        scratch_shapes=[pltpu.VMEM((N_EXPERTS, 1), F32)],
        compiler_params=_cparams(("arbitrary",)),
        name="outproj_router",
    )(out_a, out_b, x2, mod3, w_out, b_out, g_ffn, wr_t, br_col)


def _experts_kernel(first_ref, nblk_ref, x_ref, w1_ref, b1g_ref, b1l_ref, w2_ref, b2_ref, y_ref,
                    w1g_s, w1l_s, w2_s, t_s, xbuf, ybuf, xsem, ysem):
    e = pl.program_id(0)
    first = first_ref[e]
    n_blk = nblk_ref[e]
    bm = xbuf.shape[1]

    def x_copy(j, slot):
        rows = pl.ds(pl.multiple_of((first + j) * bm, bm), bm)
        return pltpu.make_async_copy(x_ref.at[rows], xbuf.at[slot], xsem.at[slot])

    def y_copy(j, slot):
        rows = pl.ds(pl.multiple_of((first + j) * bm, bm), bm)
        return pltpu.make_async_copy(ybuf.at[slot], y_ref.at[rows], ysem.at[slot])

    @pl.when(n_blk > 0)
    def _():
        x_copy(0, 0).start()
        n_db, ch, _ = t_s.shape
        half = ch // 2
        for c in range(w1_ref.shape[2] // ch):
            t = w1_ref[0, :, c * ch:(c + 1) * ch].T
            for db in range(n_db):
                t_s[db] = t[:, db * LANES:(db + 1) * LANES]
            for db in range(n_db):
                rows = slice(c * half, (c + 1) * half)
                cols = slice(db * LANES, (db + 1) * LANES)
                w1g_s[rows, cols] = t_s[db, pl.ds(0, half, stride=2), :].astype(BF16)
                w1l_s[rows, cols] = t_s[db, pl.ds(1, half, stride=2), :].astype(BF16)
        w2_s[...] = w2_ref[0].astype(BF16)

    def block(j, carry):
        slot = j & 1
        x_copy(j, slot).wait()

        @pl.when(j + 1 < n_blk)
        def _():
            x_copy(j + 1, 1 - slot).start()

        @pl.when(j >= 2)
        def _():
            y_copy(j - 2, slot).wait()

        dn = (((1,), (1,)), ((), ()))
        x = _unpack_bf16_pairs(xbuf[slot]).astype(BF16)
        ug = lax.dot_general(x, w1g_s[...], dn, preferred_element_type=F32) + b1g_ref[0]
        ul = lax.dot_general(x, w1l_s[...], dn, preferred_element_type=F32) + b1l_ref[0]
        glu = jnp.minimum(ug, SWIGLU_LIMIT)
        lin = jnp.clip(ul, -SWIGLU_LIMIT, SWIGLU_LIMIT)
        act = glu * jax.nn.sigmoid(SWIGLU_ALPHA * glu) * (lin + 1.0)
        y = jnp.dot(act.astype(BF16), w2_s[...], preferred_element_type=F32) + b2_ref[0]
        ybuf[slot] = _pack_bf16_pairs(y)
        y_copy(j, slot).start()
        return carry

    lax.fori_loop(0, n_blk, block, 0)

    @pl.when(n_blk >= 2)
    def _():
        y_copy(n_blk - 2, n_blk & 1).wait()

    @pl.when(n_blk >= 1)
    def _():
        y_copy(n_blk - 1, (n_blk - 1) & 1).wait()


def _experts(first_blk, n_blk, x_rows, w1, b1g, b1l, w2, b2):
    E, Fh, D = w2.shape
    bm = BLOCK_ROWS
    n_rows = x_rows.shape[0]
    chunk = 256
    wsel = lambda e, fb, nb: (e, 0, 0)
    return pl.pallas_call(
        _experts_kernel,
        out_shape=jax.ShapeDtypeStruct((n_rows, D // 2), jnp.int32),
        grid_spec=pltpu.PrefetchScalarGridSpec(
            num_scalar_prefetch=2,
            grid=(E,),
            in_specs=[pl.BlockSpec(memory_space=pl.ANY),
                      pl.BlockSpec((1, D, 2 * Fh), wsel),
                      pl.BlockSpec((1, 1, Fh), wsel),
                      pl.BlockSpec((1, 1, Fh), wsel),
                      pl.BlockSpec((1, Fh, D), wsel),
                      pl.BlockSpec((1, 1, D), wsel)],
            out_specs=pl.BlockSpec(memory_space=pl.ANY),
            scratch_shapes=[pltpu.VMEM((Fh, D), BF16), pltpu.VMEM((Fh, D), BF16), pltpu.VMEM((Fh, D), BF16),
                            pltpu.VMEM((D // LANES, chunk, LANES), F32),
                            pltpu.VMEM((2, bm, D // 2), jnp.int32), pltpu.VMEM((2, bm, D // 2), jnp.int32),
                            pltpu.SemaphoreType.DMA((2,)), pltpu.SemaphoreType.DMA((2,))]),
        compiler_params=_cparams(("arbitrary",), 56 * 1024 * 1024),
        name="experts",
    )(first_blk, n_blk, x_rows, w1, b1g, b1l, w2, b2)


SC_CORES = 2
SC_SUBCORES = 16
SC_CHUNK = 64


def _sc_gather_rows(table, idx):
    M = idx.shape[0]
    D = table.shape[1]
    workers = SC_CORES * SC_SUBCORES
    per_worker = M // workers
    n_chunks = per_worker // SC_CHUNK
    assert M % workers == 0 and per_worker % (2 * SC_CHUNK) == 0
    mesh = plsc.VectorSubcoreMesh(core_axis_name="c", subcore_axis_name="s")

    @functools.partial(
        pl.kernel, mesh=mesh,
        out_type=jax.ShapeDtypeStruct((M, D), table.dtype),
        scratch_types=[pltpu.VMEM((per_worker,), jnp.int32),
                       pltpu.VMEM((SC_CHUNK, D), table.dtype), pltpu.VMEM((SC_CHUNK, D), table.dtype),
                       pltpu.SemaphoreType.DMA, pltpu.SemaphoreType.DMA],
        name="sc_gather_rows")
    def gather(table_hbm, idx_hbm, out_hbm, idx_v, rows0, rows1, sem0, sem1):
        wid = lax.axis_index("s") * SC_CORES + lax.axis_index("c")
        base = wid * per_worker
        pltpu.sync_copy(idx_hbm.at[pl.ds(base, per_worker)], idx_v)

        def fetch(c, buf, sem):
            off = pl.multiple_of(c * SC_CHUNK, SC_CHUNK)
            return pltpu.make_async_copy(table_hbm.at[idx_v.at[pl.ds(off, SC_CHUNK)]], buf, sem)

        def flush(c, buf):
            off = pl.multiple_of(c * SC_CHUNK, SC_CHUNK)
            pltpu.sync_copy(buf, out_hbm.at[pl.ds(base + off, SC_CHUNK)])

        fetch(0, rows0, sem0).start()

        @pl.loop(0, n_chunks // 2)
        def _(jj):
            c = 2 * jj
            fetch(c + 1, rows1, sem1).start()
            fetch(c, rows0, sem0).wait()
            flush(c, rows0)

            @pl.when(c + 2 < n_chunks)
            def _():
                fetch(c + 2, rows0, sem0).start()

            fetch(c + 1, rows1, sem1).wait()
            flush(c + 1, rows1)

    return gather(table, idx)


SC_SCATTER_CHUNK = 128


def _sc_scatter_rows(rows, dest, n_rows):
    N, W = rows.shape
    workers = SC_CORES * SC_SUBCORES
    per_worker = N // workers
    chunks = per_worker // SC_SCATTER_CHUNK
    assert N % workers == 0 and per_worker % SC_SCATTER_CHUNK == 0
    dest3 = dest.reshape(TOP_K, N // SC_SCATTER_CHUNK, SC_SCATTER_CHUNK)
    mesh = plsc.VectorSubcoreMesh(core_axis_name="c", subcore_axis_name="s")

    @functools.partial(
        pl.kernel, mesh=mesh,
        out_type=jax.ShapeDtypeStruct((n_rows, W), rows.dtype),
        scratch_types=[pltpu.VMEM((TOP_K, chunks, SC_SCATTER_CHUNK), jnp.int32),
                       pltpu.VMEM((SC_SCATTER_CHUNK, W), rows.dtype)],
        name="sc_scatter_rows")
    def scatter(rows_hbm, dest_hbm, out_hbm, idx_v, rows_v):
        wid = lax.axis_index("s") * SC_CORES + lax.axis_index("c")
        for k in range(TOP_K):
            pltpu.sync_copy(dest_hbm.at[k, pl.ds(wid * chunks, chunks)], idx_v.at[k])

        @pl.loop(0, chunks)
        def _(j):
            start = pl.multiple_of(wid * per_worker + j * SC_SCATTER_CHUNK, SC_SCATTER_CHUNK)
            pltpu.sync_copy(rows_hbm.at[pl.ds(start, SC_SCATTER_CHUNK)], rows_v)
            for k in range(TOP_K):
                pltpu.sync_copy(rows_v, out_hbm.at[idx_v.at[k, j]])

    return scatter(rows, dest3)


def _combine_kernel(x1_ref, gate_ref, mod_ref, g_ref, y_ref, o_ref, *, final_norm):
    gate = gate_ref[...]
    moe = gate[:, 0:1] * _unpack_bf16_pairs(y_ref[0])
    for k in range(1, TOP_K):
        moe = moe + gate[:, k:k + 1] * _unpack_bf16_pairs(y_ref[k])
    gt2 = mod_ref[0, 5:6, :]
    x2 = x1_ref[...] + gt2 * moe
    o_ref[...] = _rms(x2) * g_ref[...] if final_norm else x2


def _combine(x1, gate_t, mod3, g_final, y_tok, S, final_norm):
    N, D = x1.shape
    tm = TM_ROWS
    tiles_per_seq = S // tm
    row = lambda i: (i, 0)
    return pl.pallas_call(
        functools.partial(_combine_kernel, final_norm=final_norm),
        out_shape=jax.ShapeDtypeStruct((N, D), F32),
        grid=(N // tm,),
        in_specs=[pl.BlockSpec((tm, D), row),
                  pl.BlockSpec((tm, TOP_K), row),
                  pl.BlockSpec((1, N_MOD, D), lambda i: (i // tiles_per_seq, 0, 0)),
                  pl.BlockSpec((1, D), lambda i: (0, 0)),
                  pl.BlockSpec((TOP_K, tm, D // 2), lambda i: (0, i, 0))],
        out_specs=pl.BlockSpec((tm, D), row),
        compiler_params=_cparams(("arbitrary",)),
        name="combine",
    )(x1, gate_t, mod3, g_final, y_tok)


def _routing_tables(counts, idx, rank, n_rows):
    bm = BLOCK_ROWS
    counts = counts.astype(jnp.int32)
    padded = (counts + bm - 1) // bm * bm
    pends = jnp.cumsum(padded)
    pstarts = pends - padded
    experts = jnp.arange(N_EXPERTS, dtype=jnp.int32)
    dest = jnp.sum(jnp.where(idx[..., None] == experts, pstarts, 0), axis=-1) + rank
    return dest.astype(jnp.int32), (pstarts // bm).astype(jnp.int32), (padded // bm).astype(jnp.int32)


def _extended_in_weights(w_in, b_in):
    a_q = SWA_Q_HEADS * HEAD_DIM
    a_kv = SWA_KV_HEADS * HEAD_DIM
    b_w = DIFF_HEADS * DIFF_V_DIM
    spans = [(0, a_q)]
    for base in (a_q, a_q + a_kv):
        for j in range(SWA_KV_HEADS):
            spans += [(base + j * HEAD_DIM, base + (j + 1) * HEAD_DIM)] * 2
    spans.append((a_q + 2 * a_kv, a_q + 2 * a_kv + 3 * b_w))
    w_ext = jnp.concatenate([w_in[:, lo:hi] for lo, hi in spans], axis=1).astype(BF16)
    b_ext = jnp.concatenate([b_in[lo:hi] for lo, hi in spans]).reshape(1, -1)
    widths = (a_q, 2 * a_kv, 2 * a_kv, b_w, b_w, b_w)
    return w_ext, b_ext, widths


def kernel(x, c, positions, w_ada, b_ada, g_mix, w_in, b_in, attn_sinks, lambda_q1, lambda_k1, lambda_q2,
           lambda_k2, g_subln, w_out, b_out, g_ffn, w_router, b_router, w1, b1, w2, b2, g_final):
    B, S, D = x.shape
    N = B * S
    depth = w_ada.shape[0]
    assert TM_PROJ == TQ_DIFF, "the input projection writes q/v transposed per attention tile"
    n_rows = (N * TOP_K + N_EXPERTS * (BLOCK_ROWS - 1) + BLOCK_ROWS - 1) // BLOCK_ROWS * BLOCK_ROWS

    inv = 1.0 / (ROPE_THETA ** (jnp.arange(0, HEAD_DIM, 2, dtype=F32) / HEAD_DIM))
    inv_lane = jnp.tile(inv, LANES // (HEAD_DIM // 2)).reshape(1, LANES)
    pos2 = positions.reshape(N, 1)
    xcur = x.reshape(N, D)

    for layer in range(depth):
        last = layer == depth - 1
        lambda_init = 0.8 - 0.6 * math.exp(-0.3 * layer)
        mod3 = _adaln(c, w_ada[layer], b_ada[layer]).reshape(B, N_MOD, D)

        w_ext, b_ext, widths = _extended_in_weights(w_in[layer], b_in[layer])
        qa, ka2, va2, qdt, kd, vdt = _inproj(xcur, pos2, inv_lane, mod3, g_mix[layer].reshape(1, D),
                                             w_ext, b_ext, S, widths)
        out_a = _swa(attn_sinks[layer], qa, ka2, va2, B, S)
        lam_vecs = jnp.stack([lambda_q1[layer], lambda_k1[layer], lambda_q2[layer], lambda_k2[layer]])
        out_b = _diffattn(lam_vecs, g_subln[layer].reshape(1, DIFF_V_DIM), qdt, kd, vdt, B, S, lambda_init)

        x1, h2, idx, gate, rank, counts = _outproj(
            out_a, out_b, xcur, mod3, w_out[layer].astype(BF16), b_out[layer].reshape(1, D),
            g_ffn[layer].reshape(1, D), w_router[layer].T, b_router[layer].reshape(N_EXPERTS, 1), S)

        dest, first_blk, n_blk = _routing_tables(counts[:, 0], idx, rank, n_rows)
        x_rows = _sc_scatter_rows(h2, dest, n_rows)
        y_rows = _experts(first_blk, n_blk, x_rows, w1[layer],
                          b1[layer][:, None, 0::2], b1[layer][:, None, 1::2],
                          w2[layer], b2[layer][:, None, :])
        y_tok = _sc_gather_rows(y_rows, dest.reshape(-1)).reshape(TOP_K, N, D // 2)
        xcur = _combine(x1, gate.T, mod3, g_final.reshape(1, D), y_tok, S, final_norm=last)
    return xcur.reshape(B, S, D)
```

```python
import functools
import math

import jax
import jax.numpy as jnp
from jax import lax
from jax.experimental import pallas as pl
from jax.experimental.pallas import tpu as pltpu
from jax.experimental.pallas import tpu_sc as plsc

HEAD_DIM = 64
SWA_Q_HEADS = 8
SWA_KV_HEADS = 2
SWA_GROUP = SWA_Q_HEADS // SWA_KV_HEADS
WINDOW = 128
DIFF_HEADS = 4
DIFF_V_DIM = 2 * HEAD_DIM
ROPE_THETA = 10000.0
N_EXPERTS = 32
TOP_K = 4
SWIGLU_ALPHA = 1.702
SWIGLU_LIMIT = 7.0
EPS = 1e-5
N_MOD = 6

LANES = 128
LANE_SHIFT = LANES.bit_length() - 1
TOP_K_SHIFT = TOP_K.bit_length() - 1
assert 1 << LANE_SHIFT == LANES and 1 << TOP_K_SHIFT == TOP_K
F32 = jnp.float32
BF16 = jnp.bfloat16
NEG_INF = float("-inf")

TM_PROJ = 512
TQ_SWA = 512
TQ_DIFF = 512
VT_ROWS = DIFF_V_DIM + 16
TM_OUT = 512
TM_ROWS = 256
BLOCK_ROWS = 512
VMEM_LIMIT = 48 * 1024 * 1024


def _cparams(sem, vmem=VMEM_LIMIT):
    return pltpu.CompilerParams(dimension_semantics=sem, vmem_limit_bytes=vmem)


def _adaln_kernel(ct_ref, w_ref, b_ref, o_ref):
    c = ct_ref[...]
    cond = c * jax.nn.sigmoid(c)
    w = w_ref[...]
    rows = [jnp.sum(w * cond[:, b:b + 1], axis=0, keepdims=True) for b in range(c.shape[1])]
    o_ref[...] = jnp.concatenate(rows, axis=0) + b_ref[...]


def _adaln(c, w_ada, b_ada):
    B, D = c.shape
    n_out = w_ada.shape[1]
    tn = 1024
    return pl.pallas_call(
        _adaln_kernel,
        out_shape=jax.ShapeDtypeStruct((B, n_out), F32),
        grid=(n_out // tn,),
        in_specs=[pl.BlockSpec((D, B), lambda j: (0, 0)),
                  pl.BlockSpec((D, tn), lambda j: (0, j)),
                  pl.BlockSpec((1, tn), lambda j: (0, j))],
        out_specs=pl.BlockSpec((B, tn), lambda j: (0, j)),
        compiler_params=_cparams(("arbitrary",)),
        name="adaln",
    )(c.T, w_ada, b_ada.reshape(1, n_out))


def _rms(x):
    return x * lax.rsqrt(jnp.mean(x * x, axis=-1, keepdims=True) + EPS)


HIGH_HALF = -65536


def _pack_bf16_pairs(v):
    bits = lax.bitcast_convert_type(v.astype(BF16).astype(F32), jnp.int32)
    half = v.shape[1] // 2
    return lax.shift_right_logical(bits[:, :half], 16) | (bits[:, half:] & jnp.int32(HIGH_HALF))


def _unpack_bf16_pairs(w):
    return jnp.concatenate([lax.bitcast_convert_type(lax.shift_left(w, 16), F32),
                            lax.bitcast_convert_type(w & jnp.int32(HIGH_HALF), F32)], axis=1)


def _inproj_kernel(x_ref, pos_ref, inv_ref, mod_ref, g_ref, w_ref, b_ref,
                   qa_ref, ka_ref, va_ref, qd_ref, kd_ref, vd_ref):
    x = x_ref[...]
    sh = mod_ref[0, 0:1, :]
    sc = mod_ref[0, 1:2, :]
    h = _rms(x) * g_ref[...] * (1.0 + sc) + sh
    proj = jnp.dot(h.astype(BF16), w_ref[...], preferred_element_type=F32) + b_ref[...]

    lane = lax.broadcasted_iota(jnp.int32, (1, LANES), 1)
    first_half = (lane & (HEAD_DIM - 1)) < (HEAD_DIM // 2)
    n_freq = HEAD_DIM // 2
    groups = LANES // n_freq
    tm = x.shape[0]
    rows = tm // groups
    group = lax.shift_right_logical(lane, n_freq.bit_length() - 1)
    pos = pos_ref[...].astype(F32)
    pos_q = pos[0:rows]
    for g in range(1, groups):
        pos_q = jnp.where(group == g, pos[g * rows:(g + 1) * rows], pos_q)
    ang_q = pos_q * inv_ref[...]

    def spread(table_q):
        blocks = []
        for g in range(groups):
            only = jnp.where(group == g, table_q, 0.0)
            full = only
            for r in range(1, groups):
                full = full + pltpu.roll(only, r * n_freq, axis=1)
            blocks.append(full)
        return jnp.concatenate(blocks, axis=0)

    cos = spread(jnp.cos(ang_q))
    sin = spread(jnp.sin(ang_q))
    sin_signed = jnp.where(first_half, -sin, sin)

    def rope(t):
        partner = jnp.where(first_half,
                            pltpu.roll(t, LANES - HEAD_DIM // 2, axis=1),
                            pltpu.roll(t, HEAD_DIM // 2, axis=1))
        return t * cos + partner * sin_signed

    def emit(out_ref, col0, width, rotary, scale, transposed):
        for j in range(width // LANES):
            t = proj[:, col0 + j * LANES: col0 + (j + 1) * LANES]
            if rotary:
                t = rope(t)
            if scale != 1.0:
                t = t * scale
            if transposed:
                rows = out_ref.shape[2] // (width // LANES)
                out_ref[0, 0, j * rows:j * rows + LANES, :] = t.T.astype(out_ref.dtype)
                if rows > LANES:
                    fill = lax.broadcasted_iota(jnp.int32, (rows - LANES, t.shape[0]), 0) == 0
                    out_ref[0, 0, j * rows + LANES:(j + 1) * rows, :] = fill.astype(out_ref.dtype)
            else:
                out_ref[:, j * LANES:(j + 1) * LANES] = t.astype(out_ref.dtype)

    swa_scale = 1.0 / math.sqrt(HEAD_DIM)
    diff_scale = math.log2(math.e) / math.sqrt(HEAD_DIM)
    col = 0
    for out_ref, width, rotary, scale, transposed in (
            (qa_ref, qa_ref.shape[1], True, swa_scale, False), (ka_ref, ka_ref.shape[1], True, 1.0, False),
            (va_ref, va_ref.shape[1], False, 1.0, False), (qd_ref, qd_ref.shape[2], True, diff_scale, True),
            (kd_ref, kd_ref.shape[1], True, 1.0, False),
            (vd_ref, vd_ref.shape[2] // VT_ROWS * LANES, False, 1.0, True)):
        emit(out_ref, col, width, rotary, scale, transposed)
        col += width


def _inproj(x2, pos2, inv_lane, mod3, g_mix, w_ext, b_ext, S, widths):
    N, D = x2.shape
    tm = TM_PROJ
    C = w_ext.shape[1]
    tiles_per_seq = S // tm
    row = lambda i: (i, 0)
    t_rows = (0, 0, 0, widths[3], 0, widths[5] // LANES * VT_ROWS)
    out_shape, out_specs = [], []
    for w, tr in zip(widths, t_rows):
        if tr:
            out_shape.append(jax.ShapeDtypeStruct((N // S, tiles_per_seq, tr, tm), BF16))
            out_specs.append(pl.BlockSpec((1, 1, tr, tm), lambda i: (i // tiles_per_seq, i % tiles_per_seq, 0, 0)))
        else:
            out_shape.append(jax.ShapeDtypeStruct((N, w), BF16))
            out_specs.append(pl.BlockSpec((tm, w), row))
    return pl.pallas_call(
        _inproj_kernel,
        out_shape=out_shape,
        grid=(N // tm,),
        in_specs=[pl.BlockSpec((tm, D), row),
                  pl.BlockSpec((tm, 1), row),
                  pl.BlockSpec((1, LANES), lambda i: (0, 0)),
                  pl.BlockSpec((1, N_MOD, D), lambda i: (i // tiles_per_seq, 0, 0)),
                  pl.BlockSpec((1, D), lambda i: (0, 0)),
                  pl.BlockSpec((D, C), lambda i: (0, 0)),
                  pl.BlockSpec((1, C), lambda i: (0, 0))],
        out_specs=out_specs,
        compiler_params=_cparams(("arbitrary",)),
        name="inproj",
    )(x2, pos2, inv_lane, mod3, g_mix, w_ext, b_ext)


def _swa_kernel(sink_ref, q_ref, kc_ref, kp_ref, vc_ref, vp_ref, o_ref):
    i = pl.program_id(1)
    tq = q_ref.shape[0]
    lane = lax.broadcasted_iota(jnp.int32, (1, LANES), 1)
    lo = lane < HEAD_DIM
    qi = lax.broadcasted_iota(jnp.int32, (WINDOW, 2 * WINDOW), 0) + WINDOW
    kj = lax.broadcasted_iota(jnp.int32, (WINDOW, 2 * WINDOW), 1)
    band = (qi - kj >= 0) & (qi - kj < WINDOW)
    dn = (((1,), (1,)), ((), ()))
    for c in range(tq // WINDOW):
        if c == 0:
            kcat = jnp.concatenate([kp_ref[...], kc_ref[0:WINDOW, :]], axis=0)
            vcat = jnp.concatenate([vp_ref[...], vc_ref[0:WINDOW, :]], axis=0)
            mask = band & (kj >= jnp.where(i > 0, 0, WINDOW))
        else:
            kcat = kc_ref[(c - 1) * WINDOW:(c + 1) * WINDOW, :]
            vcat = vc_ref[(c - 1) * WINDOW:(c + 1) * WINDOW, :]
            mask = band
        for j in range(SWA_KV_HEADS):
            kj2 = kcat[:, j * LANES:(j + 1) * LANES]
            vj2 = vcat[:, j * LANES:(j + 1) * LANES]
            zero = jnp.zeros_like(kj2)
            k_halves = (jnp.where(lo, kj2, zero), jnp.where(lo, zero, kj2))
            v_halves = (jnp.where(lo, vj2, zero), jnp.where(lo, zero, vj2))
            for p in range(SWA_GROUP // 2):
                g = j * (SWA_GROUP // 2) + p
                q = q_ref[c * WINDOW:(c + 1) * WINDOW, g * LANES:(g + 1) * LANES]
                out = jnp.zeros((WINDOW, LANES), F32)
                for half in range(2):
                    sink = sink_ref[2 * g + half]
                    s = lax.dot_general(q, k_halves[half], dn, preferred_element_type=F32)
                    s = jnp.where(mask, s, NEG_INF)
                    m = jnp.maximum(jnp.max(s, axis=1, keepdims=True), sink)
                    e = jnp.exp(s - m)
                    denom = jnp.sum(e, axis=1, keepdims=True) + jnp.exp(sink - m)
                    pv = jnp.dot(e.astype(BF16), v_halves[half], preferred_element_type=F32)
                    out = out + pv / denom
                o_ref[c * WINDOW:(c + 1) * WINDOW, g * LANES:(g + 1) * LANES] = out.astype(o_ref.dtype)


def _swa(sinks, qa, ka2, va2, B, S):
    N = qa.shape[0]
    tq = TQ_SWA
    nq = S // tq
    wpt = tq // WINDOW
    wps = S // WINDOW
    cur = lambda b, i: (b * nq + i, 0)
    prev = lambda b, i: (b * wps + jnp.maximum(i * wpt - 1, 0), 0)
    return pl.pallas_call(
        _swa_kernel,
        out_shape=jax.ShapeDtypeStruct((N, qa.shape[1]), BF16),
        grid=(B, nq),
        in_specs=[pl.BlockSpec(memory_space=pltpu.SMEM),
                  pl.BlockSpec((tq, qa.shape[1]), cur),
                  pl.BlockSpec((tq, ka2.shape[1]), cur),
                  pl.BlockSpec((WINDOW, ka2.shape[1]), prev),
                  pl.BlockSpec((tq, va2.shape[1]), cur),
                  pl.BlockSpec((WINDOW, va2.shape[1]), prev)],
        out_specs=pl.BlockSpec((tq, qa.shape[1]), cur),
        compiler_params=_cparams(("arbitrary", "arbitrary")),
        name="swa",
    )(sinks, qa, ka2, ka2, va2, va2)


def _diff_kernel(lam_ref, g_ref, qt_ref, k_ref, vt_ref, o_ref, sa_ref, sb_ref, m_ref, acc_ref, *, lambda_init):
    i = pl.program_id(2)
    tq = qt_ref.shape[3]
    tk = vt_ref.shape[3]
    qt = qt_ref[0, 0]
    lane = lax.broadcasted_iota(jnp.int32, (1, LANES), 1)
    lo = lane < HEAD_DIM
    m_ref[...] = jnp.full(m_ref.shape, NEG_INF, F32)
    acc_ref[...] = jnp.zeros(acc_ref.shape, F32)

    def scores(c, s_ref):
        k = k_ref[pl.ds(pl.multiple_of(c * tk, tk), tk), :]
        zero = jnp.zeros_like(k)
        s_ref[0] = jnp.dot(jnp.where(lo, k, zero), qt, preferred_element_type=F32)
        s_ref[1] = jnp.dot(jnp.where(lo, zero, k), qt, preferred_element_type=F32)

    def consume(c, s_ref, diagonal):
        vt = vt_ref[0, c]
        for mp in range(2):
            s = s_ref[mp]
            if diagonal:
                kpos = lax.broadcasted_iota(jnp.int32, (tk, tq), 0)
                qpos = lax.broadcasted_iota(jnp.int32, (tk, tq), 1)
                s = jnp.where(kpos <= qpos, s, NEG_INF)
            m_prev = m_ref[mp]
            m_new = jnp.maximum(m_prev, jnp.max(s, axis=0, keepdims=True))
            alpha = jnp.exp2(m_prev - m_new)
            p = jnp.exp2(s - m_new).astype(BF16)
            acc_ref[mp] = alpha * acc_ref[mp] + jnp.dot(vt, p, preferred_element_type=F32)
            m_ref[mp] = m_new

    scores(0, sa_ref)

    def pair(jj, carry):
        c = 2 * jj
        scores(c + 1, sb_ref)
        consume(c, sa_ref, False)
        scores(c + 2, sa_ref)
        consume(c + 1, sb_ref, False)
        return carry

    lax.fori_loop(0, lax.shift_right_logical(i, 1), pair, 0)

    @pl.when(i % 2 == 0)
    def _():
        consume(i, sa_ref, True)

    @pl.when(i % 2 == 1)
    def _():
        scores(i, sb_ref)
        consume(i - 1, sa_ref, False)
        consume(i, sb_ref, True)

    lq1, lk1, lq2, lk2 = (lam_ref[r:r + 1, :] for r in range(4))
    lam = (jnp.exp(jnp.sum(lq1 * lk1, axis=1, keepdims=True))
           - jnp.exp(jnp.sum(lq2 * lk2, axis=1, keepdims=True)) + lambda_init)
    d = DIFF_V_DIM
    ot = (acc_ref[0, 0:d, :] / acc_ref[0, d:d + 1, :]
          - lam * (acc_ref[1, 0:d, :] / acc_ref[1, d:d + 1, :]))
    ot = ot * lax.rsqrt(jnp.mean(ot * ot, axis=0, keepdims=True) + EPS)
    o_ref[...] = (ot.T * g_ref[...] * (1.0 - lambda_init)).astype(o_ref.dtype)


def _diffattn(lam_vecs, g_subln, qdt, kd, vdt, B, S, lambda_init):
    N, C = kd.shape
    tq = TQ_DIFF
    nq = S // tq
    return pl.pallas_call(
        functools.partial(_diff_kernel, lambda_init=lambda_init),
        out_shape=jax.ShapeDtypeStruct((N, C), BF16),
        grid=(B, DIFF_HEADS, nq),
        in_specs=[pl.BlockSpec((4, HEAD_DIM), lambda b, h, i: (0, 0)),
                  pl.BlockSpec((1, DIFF_V_DIM), lambda b, h, i: (0, 0)),
                  pl.BlockSpec((1, 1, LANES, tq), lambda b, h, i: (b, i, h, 0)),
                  pl.BlockSpec((S, LANES), lambda b, h, i: (b, h)),
                  pl.BlockSpec((1, nq, VT_ROWS, tq), lambda b, h, i: (b, 0, h, 0))],
        out_specs=pl.BlockSpec((tq, LANES), lambda b, h, i: (b * nq + i, h)),
        scratch_shapes=[pltpu.VMEM((2, tq, tq), F32), pltpu.VMEM((2, tq, tq), F32),
                        pltpu.VMEM((2, 1, tq), F32), pltpu.VMEM((2, VT_ROWS, tq), F32)],
        compiler_params=_cparams(("arbitrary", "arbitrary", "arbitrary")),
        name="diffattn",
    )(lam_vecs, g_subln, qdt, kd, vdt)


def _outproj_kernel(oa_ref, ob_ref, x_ref, mod_ref, wo_ref, bo_ref, g_ref, wr_ref, br_ref,
                    x1_ref, h2_ref, idx_ref, gate_ref, rank_ref, cnt_ref, carry_ref):
    i = pl.program_id(0)
    tm = x_ref.shape[0]
    half = oa_ref.shape[1]

    @pl.when(i == 0)
    def _():
        carry_ref[...] = jnp.zeros(carry_ref.shape, F32)

    gt1 = mod_ref[0, 2:3, :]
    sh2 = mod_ref[0, 3:4, :]
    sc2 = mod_ref[0, 4:5, :]
    mixed = (jnp.dot(oa_ref[...], wo_ref[0:half, :], preferred_element_type=F32)
             + jnp.dot(ob_ref[...], wo_ref[half:, :], preferred_element_type=F32) + bo_ref[...])
    x1 = x_ref[...] + gt1 * mixed
    x1_ref[...] = x1
    h2 = _rms(x1) * g_ref[...] * (1.0 + sc2) + sh2
    dn = (((1,), (1,)), ((), ()))
    h_hi = h2.astype(BF16)
    h2_ref[...] = _pack_bf16_pairs(h2)
    h_lo = (h2 - h_hi.astype(F32)).astype(BF16)
    w = wr_ref[...]
    w_hi = w.astype(BF16)
    w_lo = (w - w_hi.astype(F32)).astype(BF16)
    logits = (lax.dot_general(w_hi, h_hi, dn, preferred_element_type=F32)
              + lax.dot_general(w_hi, h_lo, dn, preferred_element_type=F32)
              + lax.dot_general(w_lo, h_hi, dn, preferred_element_type=F32)
              + br_ref[...])

    eidx = lax.broadcasted_iota(jnp.int32, logits.shape, 0)
    vals = logits
    onehots, top_vals, top_idx = [], [], []
    for _k in range(TOP_K):
        mx = jnp.max(vals, axis=0, keepdims=True)
        sel = jnp.min(jnp.where(vals == mx, eidx, N_EXPERTS), axis=0, keepdims=True)
        oh = eidx == sel
        onehots.append(oh)
        top_vals.append(mx)
        top_idx.append(sel)
        vals = jnp.where(oh, NEG_INF, vals)
    exps = [jnp.exp(v - top_vals[0]) for v in top_vals]
    denom = exps[0] + exps[1] + exps[2] + exps[3]
    gate_ref[...] = jnp.concatenate([e / denom for e in exps], axis=0)
    idx_ref[...] = jnp.concatenate(top_idx, axis=0)

    member = (onehots[0] | onehots[1] | onehots[2] | onehots[3])
    member_f = member.astype(F32)
    t_src = lax.broadcasted_iota(jnp.int32, (tm, tm), 0)
    t_dst = lax.broadcasted_iota(jnp.int32, (tm, tm), 1)
    before = (t_src < t_dst).astype(BF16)
    prefix = jnp.dot(member.astype(BF16), before, preferred_element_type=F32) + carry_ref[...]
    ranks = [jnp.sum(jnp.where(oh, prefix, 0.0), axis=0, keepdims=True) for oh in onehots]
    rank_ref[...] = jnp.concatenate(ranks, axis=0).astype(jnp.int32)
    carry_ref[...] = carry_ref[...] + jnp.sum(member_f, axis=1, keepdims=True)
    cnt_ref[...] = jnp.broadcast_to(carry_ref[...], cnt_ref.shape)


def _outproj(out_a, out_b, x2, mod3, w_out, b_out, g_ffn, wr_t, br_col, S):
    N, D = x2.shape
    tm = TM_OUT
    tiles_per_seq = S // tm
    row = lambda i: (i, 0)
    colb = lambda i: (0, i)
    const = lambda i: (0, 0)
    return pl.pallas_call(
        _outproj_kernel,
        out_shape=[jax.ShapeDtypeStruct((N, D), F32), jax.ShapeDtypeStruct((N, D // 2), jnp.int32),
                   jax.ShapeDtypeStruct((TOP_K, N), jnp.int32), jax.ShapeDtypeStruct((TOP_K, N), F32),
                   jax.ShapeDtypeStruct((TOP_K, N), jnp.int32), jax.ShapeDtypeStruct((N_EXPERTS, LANES), F32)],
        grid=(N // tm,),
        in_specs=[pl.BlockSpec((tm, out_a.shape[1]), row),
                  pl.BlockSpec((tm, out_b.shape[1]), row),
                  pl.BlockSpec((tm, D), row),
                  pl.BlockSpec((1, N_MOD, D), lambda i: (i // tiles_per_seq, 0, 0)),
                  pl.BlockSpec(w_out.shape, const),
                  pl.BlockSpec((1, D), const),
                  pl.BlockSpec((1, D), const),
                  pl.BlockSpec(wr_t.shape, const),
                  pl.BlockSpec((N_EXPERTS, 1), const)],
        out_specs=[pl.BlockSpec((tm, D), row), pl.BlockSpec((tm, D // 2), row),
                   pl.BlockSpec((TOP_K, tm), colb), pl.BlockSpec((TOP_K, tm), colb),
                   pl.BlockSpec((TOP_K, tm), colb), pl.BlockSpec((N_EXPERTS, LANES), const)],
        scratch_shapes=[pltpu.VMEM((N_EXPERTS, 1), F32)],
        compiler_params=_cparams(("arbitrary",)),
        name="outproj_router",
    )(out_a, out_b, x2, mod3, w_out, b_out, g_ffn, wr_t, br_col)


def _experts_kernel(first_ref, nblk_ref, x_ref, w1_ref, b1g_ref, b1l_ref, w2_ref, b2_ref, y_ref,
                    w1g_s, w1l_s, w2_s, xbuf, ybuf, xsem, ysem):
    e = pl.program_id(0)
    first = first_ref[e]
    n_blk = nblk_ref[e]
    bm = xbuf.shape[1]

    def x_copy(j, slot):
        rows = pl.ds(pl.multiple_of((first + j) * bm, bm), bm)
        return pltpu.make_async_copy(x_ref.at[rows], xbuf.at[slot], xsem.at[slot])

    def y_copy(j, slot):
        rows = pl.ds(pl.multiple_of((first + j) * bm, bm), bm)
        return pltpu.make_async_copy(ybuf.at[slot], y_ref.at[rows], ysem.at[slot])

    @pl.when(n_blk > 0)
    def _():
        x_copy(0, 0).start()
        ch = 256
        half = ch // 2
        for c in range(w1_ref.shape[2] // ch):
            t = w1_ref[0, :, c * ch:(c + 1) * ch].astype(BF16).T
            pairs = pltpu.bitcast(t, jnp.int32)
            rows = slice(c * half, (c + 1) * half)
            w1g_s[rows, :] = lax.bitcast_convert_type(lax.shift_left(pairs, 16), F32).astype(BF16)
            w1l_s[rows, :] = lax.bitcast_convert_type(pairs & jnp.int32(HIGH_HALF), F32).astype(BF16)
        w2_s[...] = w2_ref[0].astype(BF16)

    def block(j, carry):
        slot = j & 1
        x_copy(j, slot).wait()

        @pl.when(j + 1 < n_blk)
        def _():
            x_copy(j + 1, 1 - slot).start()

        @pl.when(j >= 2)
        def _():
            y_copy(j - 2, slot).wait()

        dn = (((1,), (1,)), ((), ()))
        x = _unpack_bf16_pairs(xbuf[slot]).astype(BF16)
        ug = lax.dot_general(x, w1g_s[...], dn, preferred_element_type=F32) + b1g_ref[0]
        ul = lax.dot_general(x, w1l_s[...], dn, preferred_element_type=F32) + b1l_ref[0]
        glu = jnp.minimum(ug, SWIGLU_LIMIT)
        lin = jnp.clip(ul, -SWIGLU_LIMIT, SWIGLU_LIMIT)
        act = glu * jax.nn.sigmoid(SWIGLU_ALPHA * glu) * (lin + 1.0)
        y = jnp.dot(act.astype(BF16), w2_s[...], preferred_element_type=F32) + b2_ref[0]
        ybuf[slot] = _pack_bf16_pairs(y)
        y_copy(j, slot).start()
        return carry

    lax.fori_loop(0, n_blk, block, 0)

    @pl.when(n_blk >= 2)
    def _():
        y_copy(n_blk - 2, n_blk & 1).wait()

    @pl.when(n_blk >= 1)
    def _():
        y_copy(n_blk - 1, (n_blk - 1) & 1).wait()


def _experts(first_blk, n_blk, x_rows, w1, b1g, b1l, w2, b2):
    E, Fh, D = w2.shape
    bm = BLOCK_ROWS
    n_rows = x_rows.shape[0]
    wsel = lambda e, fb, nb: (e, 0, 0)
    return pl.pallas_call(
        _experts_kernel,
        out_shape=jax.ShapeDtypeStruct((n_rows, D // 2), jnp.int32),
        grid_spec=pltpu.PrefetchScalarGridSpec(
            num_scalar_prefetch=2,
            grid=(E,),
            in_specs=[pl.BlockSpec(memory_space=pl.ANY),
                      pl.BlockSpec((1, D, 2 * Fh), wsel),
                      pl.BlockSpec((1, 1, Fh), wsel),
                      pl.BlockSpec((1, 1, Fh), wsel),
                      pl.BlockSpec((1, Fh, D), wsel),
                      pl.BlockSpec((1, 1, D), wsel)],
            out_specs=pl.BlockSpec(memory_space=pl.ANY),
            scratch_shapes=[pltpu.VMEM((Fh, D), BF16), pltpu.VMEM((Fh, D), BF16), pltpu.VMEM((Fh, D), BF16),
                            pltpu.VMEM((2, bm, D // 2), jnp.int32), pltpu.VMEM((2, bm, D // 2), jnp.int32),
                            pltpu.SemaphoreType.DMA((2,)), pltpu.SemaphoreType.DMA((2,))]),
        compiler_params=_cparams(("arbitrary",), 56 * 1024 * 1024),
        name="experts",
    )(first_blk, n_blk, x_rows, w1, b1g, b1l, w2, b2)


SC_CORES = 2
SC_SUBCORES = 16
SC_CHUNK = 64


def _sc_gather_rows(table, idx):
    M = idx.shape[0]
    D = table.shape[1]
    workers = SC_CORES * SC_SUBCORES
    per_worker = M // workers
    n_chunks = per_worker // SC_CHUNK
    assert M % workers == 0 and per_worker % (2 * SC_CHUNK) == 0
    mesh = plsc.VectorSubcoreMesh(core_axis_name="c", subcore_axis_name="s")

    @functools.partial(
        pl.kernel, mesh=mesh,
        out_type=jax.ShapeDtypeStruct((M, D), table.dtype),
        scratch_types=[pltpu.VMEM((per_worker,), jnp.int32),
                       pltpu.VMEM((SC_CHUNK, D), table.dtype), pltpu.VMEM((SC_CHUNK, D), table.dtype),
                       pltpu.SemaphoreType.DMA, pltpu.SemaphoreType.DMA],
        name="sc_gather_rows")
    def gather(table_hbm, idx_hbm, out_hbm, idx_v, rows0, rows1, sem0, sem1):
        wid = lax.axis_index("s") * SC_CORES + lax.axis_index("c")
        base = wid * per_worker
        pltpu.sync_copy(idx_hbm.at[pl.ds(base, per_worker)], idx_v)

        def fetch(c, buf, sem):
            off = pl.multiple_of(c * SC_CHUNK, SC_CHUNK)
            return pltpu.make_async_copy(table_hbm.at[idx_v.at[pl.ds(off, SC_CHUNK)]], buf, sem)

        def flush(c, buf):
            off = pl.multiple_of(c * SC_CHUNK, SC_CHUNK)
            pltpu.sync_copy(buf, out_hbm.at[pl.ds(base + off, SC_CHUNK)])

        fetch(0, rows0, sem0).start()

        @pl.loop(0, n_chunks // 2)
        def _(jj):
            c = 2 * jj
            fetch(c + 1, rows1, sem1).start()
            fetch(c, rows0, sem0).wait()
            flush(c, rows0)

            @pl.when(c + 2 < n_chunks)
            def _():
                fetch(c + 2, rows0, sem0).start()

            fetch(c + 1, rows1, sem1).wait()
            flush(c + 1, rows1)

    return gather(table, idx)


SC_SCATTER_CHUNK = 128


def _sc_scatter_rows(rows, dest, n_rows):
    N, W = rows.shape
    workers = SC_CORES * SC_SUBCORES
    per_worker = N // workers
    chunks = per_worker // SC_SCATTER_CHUNK
    assert N % workers == 0 and per_worker % SC_SCATTER_CHUNK == 0
    dest3 = dest.reshape(TOP_K, N // SC_SCATTER_CHUNK, SC_SCATTER_CHUNK)
    mesh = plsc.VectorSubcoreMesh(core_axis_name="c", subcore_axis_name="s")

    @functools.partial(
        pl.kernel, mesh=mesh,
        out_type=jax.ShapeDtypeStruct((n_rows, W), rows.dtype),
        scratch_types=[pltpu.VMEM((TOP_K, chunks, SC_SCATTER_CHUNK), jnp.int32),
                       pltpu.VMEM((SC_SCATTER_CHUNK, W), rows.dtype)],
        name="sc_scatter_rows")
    def scatter(rows_hbm, dest_hbm, out_hbm, idx_v, rows_v):
        wid = lax.axis_index("s") * SC_CORES + lax.axis_index("c")
        for k in range(TOP_K):
            pltpu.sync_copy(dest_hbm.at[k, pl.ds(wid * chunks, chunks)], idx_v.at[k])

        @pl.loop(0, chunks)
        def _(j):
            start = pl.multiple_of(wid * per_worker + j * SC_SCATTER_CHUNK, SC_SCATTER_CHUNK)
            pltpu.sync_copy(rows_hbm.at[pl.ds(start, SC_SCATTER_CHUNK)], rows_v)
            for k in range(TOP_K):
                pltpu.sync_copy(rows_v, out_hbm.at[idx_v.at[k, j]])

    return scatter(rows, dest3)


def _combine_kernel(x1_ref, gate_ref, mod_ref, g_ref, y_ref, o_ref, *, final_norm):
    gate = gate_ref[...]
    moe = gate[:, 0:1] * _unpack_bf16_pairs(y_ref[0])
    for k in range(1, TOP_K):
        moe = moe + gate[:, k:k + 1] * _unpack_bf16_pairs(y_ref[k])
    gt2 = mod_ref[0, 5:6, :]
    x2 = x1_ref[...] + gt2 * moe
    o_ref[...] = _rms(x2) * g_ref[...] if final_norm else x2


def _combine(x1, gate_t, mod3, g_final, y_tok, S, final_norm):
    N, D = x1.shape
    tm = TM_ROWS
    tiles_per_seq = S // tm
    row = lambda i: (i, 0)
    return pl.pallas_call(
        functools.partial(_combine_kernel, final_norm=final_norm),
        out_shape=jax.ShapeDtypeStruct((N, D), F32),
        grid=(N // tm,),
        in_specs=[pl.BlockSpec((tm, D), row),
                  pl.BlockSpec((tm, TOP_K), row),
                  pl.BlockSpec((1, N_MOD, D), lambda i: (i // tiles_per_seq, 0, 0)),
                  pl.BlockSpec((1, D), lambda i: (0, 0)),
                  pl.BlockSpec((TOP_K, tm, D // 2), lambda i: (0, i, 0))],
        out_specs=pl.BlockSpec((tm, D), row),
        compiler_params=_cparams(("arbitrary",)),
        name="combine",
    )(x1, gate_t, mod3, g_final, y_tok)


def _routing_tables(counts, idx, rank, n_rows):
    bm = BLOCK_ROWS
    counts = counts.astype(jnp.int32)
    padded = (counts + bm - 1) // bm * bm
    pends = jnp.cumsum(padded)
    pstarts = pends - padded
    experts = jnp.arange(N_EXPERTS, dtype=jnp.int32)
    dest = jnp.sum(jnp.where(idx[..., None] == experts, pstarts, 0), axis=-1) + rank
    return dest.astype(jnp.int32), (pstarts // bm).astype(jnp.int32), (padded // bm).astype(jnp.int32)


def _extended_in_weights(w_in, b_in):
    a_q = SWA_Q_HEADS * HEAD_DIM
    a_kv = SWA_KV_HEADS * HEAD_DIM
    b_w = DIFF_HEADS * DIFF_V_DIM
    spans = [(0, a_q)]
    for base in (a_q, a_q + a_kv):
        for j in range(SWA_KV_HEADS):
            spans += [(base + j * HEAD_DIM, base + (j + 1) * HEAD_DIM)] * 2
    spans.append((a_q + 2 * a_kv, a_q + 2 * a_kv + 3 * b_w))
    w_ext = jnp.concatenate([w_in[:, lo:hi] for lo, hi in spans], axis=1).astype(BF16)
    b_ext = jnp.concatenate([b_in[lo:hi] for lo, hi in spans]).reshape(1, -1)
    widths = (a_q, 2 * a_kv, 2 * a_kv, b_w, b_w, b_w)
    return w_ext, b_ext, widths


def kernel(x, c, positions, w_ada, b_ada, g_mix, w_in, b_in, attn_sinks, lambda_q1, lambda_k1, lambda_q2,
           lambda_k2, g_subln, w_out, b_out, g_ffn, w_router, b_router, w1, b1, w2, b2, g_final):
    B, S, D = x.shape
    N = B * S
    depth = w_ada.shape[0]
    assert TM_PROJ == TQ_DIFF, "the input projection writes q/v transposed per attention tile"
    n_rows = (N * TOP_K + N_EXPERTS * (BLOCK_ROWS - 1) + BLOCK_ROWS - 1) // BLOCK_ROWS * BLOCK_ROWS

    inv = 1.0 / (ROPE_THETA ** (jnp.arange(0, HEAD_DIM, 2, dtype=F32) / HEAD_DIM))
    inv_lane = jnp.tile(inv, LANES // (HEAD_DIM // 2)).reshape(1, LANES)
    pos2 = positions.reshape(N, 1)
    xcur = x.reshape(N, D)

    for layer in range(depth):
        last = layer == depth - 1
        lambda_init = 0.8 - 0.6 * math.exp(-0.3 * layer)
        mod3 = _adaln(c, w_ada[layer], b_ada[layer]).reshape(B, N_MOD, D)

        w_ext, b_ext, widths = _extended_in_weights(w_in[layer], b_in[layer])
        qa, ka2, va2, qdt, kd, vdt = _inproj(xcur, pos2, inv_lane, mod3, g_mix[layer].reshape(1, D),
                                             w_ext, b_ext, S, widths)
        out_a = _swa(attn_sinks[layer], qa, ka2, va2, B, S)
        lam_vecs = jnp.stack([lambda_q1[layer], lambda_k1[layer], lambda_q2[layer], lambda_k2[layer]])
        out_b = _diffattn(lam_vecs, g_subln[layer].reshape(1, DIFF_V_DIM), qdt, kd, vdt, B, S, lambda_init)

        x1, h2, idx, gate, rank, counts = _outproj(
            out_a, out_b, xcur, mod3, w_out[layer].astype(BF16), b_out[layer].reshape(1, D),
            g_ffn[layer].reshape(1, D), w_router[layer].T, b_router[layer].reshape(N_EXPERTS, 1), S)

        dest, first_blk, n_blk = _routing_tables(counts[:, 0], idx, rank, n_rows)
        x_rows = _sc_scatter_rows(h2, dest, n_rows)
        y_rows = _experts(first_blk, n_blk, x_rows, w1[layer],
                          b1[layer][:, None, 0::2], b1[layer][:, None, 1::2],
                          w2[layer], b2[layer][:, None, :])
        y_tok = _sc_gather_rows(y_rows, dest.reshape(-1)).reshape(TOP_K, N, D // 2)
        xcur = _combine(x1, gate.T, mod3, g_final.reshape(1, D), y_tok, S, final_norm=last)
    return xcur.reshape(B, S, D)
```

```python
import functools
import math

import jax
import jax.numpy as jnp
from jax import lax
from jax.experimental import pallas as pl
from jax.experimental.pallas import tpu as pltpu
from jax.experimental.pallas import tpu_sc as plsc

HEAD_DIM = 64
SWA_Q_HEADS = 8
SWA_KV_HEADS = 2
SWA_GROUP = SWA_Q_HEADS // SWA_KV_HEADS
WINDOW = 128
DIFF_HEADS = 4
DIFF_V_DIM = 2 * HEAD_DIM
ROPE_THETA = 10000.0
N_EXPERTS = 32
TOP_K = 4
SWIGLU_ALPHA = 1.702
SWIGLU_LIMIT = 7.0
EPS = 1e-5
N_MOD = 6

LANES = 128
LANE_SHIFT = LANES.bit_length() - 1
TOP_K_SHIFT = TOP_K.bit_length() - 1
assert 1 << LANE_SHIFT == LANES and 1 << TOP_K_SHIFT == TOP_K
F32 = jnp.float32
BF16 = jnp.bfloat16
NEG_INF = float("-inf")

TM_PROJ = 512
TQ_SWA = 512
TQ_DIFF = 512
VT_ROWS = DIFF_V_DIM + 16
TM_OUT = 512
TM_ROWS = 256
BLOCK_ROWS = 512
VMEM_LIMIT = 48 * 1024 * 1024


def _cparams(sem, vmem=VMEM_LIMIT):
    return pltpu.CompilerParams(dimension_semantics=sem, vmem_limit_bytes=vmem)


def _adaln_kernel(ct_ref, w_ref, b_ref, o_ref):
    c = ct_ref[...]
    cond = c * jax.nn.sigmoid(c)
    w = w_ref[...]
    rows = [jnp.sum(w * cond[:, b:b + 1], axis=0, keepdims=True) for b in range(c.shape[1])]
    o_ref[...] = jnp.concatenate(rows, axis=0) + b_ref[...]


def _adaln(c, w_ada, b_ada):
    B, D = c.shape
    n_out = w_ada.shape[1]
    tn = 1024
    return pl.pallas_call(
        _adaln_kernel,
        out_shape=jax.ShapeDtypeStruct((B, n_out), F32),
        grid=(n_out // tn,),
        in_specs=[pl.BlockSpec((D, B), lambda j: (0, 0)),
                  pl.BlockSpec((D, tn), lambda j: (0, j)),
                  pl.BlockSpec((1, tn), lambda j: (0, j))],
        out_specs=pl.BlockSpec((B, tn), lambda j: (0, j)),
        compiler_params=_cparams(("arbitrary",)),
        name="adaln",
    )(c.T, w_ada, b_ada.reshape(1, n_out))


def _rms(x):
    return x * lax.rsqrt(jnp.mean(x * x, axis=-1, keepdims=True) + EPS)


HIGH_HALF = -65536


def _pack_bf16_pairs(v):
    bits = lax.bitcast_convert_type(v.astype(BF16).astype(F32), jnp.int32)
    half = v.shape[1] // 2
    return lax.shift_right_logical(bits[:, :half], 16) | (bits[:, half:] & jnp.int32(HIGH_HALF))


def _unpack_bf16_pairs(w):
    return jnp.concatenate([lax.bitcast_convert_type(lax.shift_left(w, 16), F32),
                            lax.bitcast_convert_type(w & jnp.int32(HIGH_HALF), F32)], axis=1)


def _inproj_kernel(x_ref, pos_ref, inv_ref, mod_ref, g_ref, w_ref, b_ref,
                   qa_ref, ka_ref, va_ref, qd_ref, kd_ref, vd_ref):
    x = x_ref[...]
    sh = mod_ref[0, 0:1, :]
    sc = mod_ref[0, 1:2, :]
    h = _rms(x) * g_ref[...] * (1.0 + sc) + sh
    proj = jnp.dot(h.astype(BF16), w_ref[...], preferred_element_type=F32) + b_ref[...]

    lane = lax.broadcasted_iota(jnp.int32, (1, LANES), 1)
    first_half = (lane & (HEAD_DIM - 1)) < (HEAD_DIM // 2)
    n_freq = HEAD_DIM // 2
    groups = LANES // n_freq
    tm = x.shape[0]
    rows = tm // groups
    group = lax.shift_right_logical(lane, n_freq.bit_length() - 1)
    pos = pos_ref[...].astype(F32)
    pos_q = pos[0:rows]
    for g in range(1, groups):
        pos_q = jnp.where(group == g, pos[g * rows:(g + 1) * rows], pos_q)
    ang_q = pos_q * inv_ref[...]

    def spread(table_q):
        blocks = []
        for g in range(groups):
            only = jnp.where(group == g, table_q, 0.0)
            full = only
            for r in range(1, groups):
                full = full + pltpu.roll(only, r * n_freq, axis=1)
            blocks.append(full)
        return jnp.concatenate(blocks, axis=0)

    cos = spread(jnp.cos(ang_q))
    sin = spread(jnp.sin(ang_q))
    sin_signed = jnp.where(first_half, -sin, sin)

    def rope(t):
        partner = jnp.where(first_half,
                            pltpu.roll(t, LANES - HEAD_DIM // 2, axis=1),
                            pltpu.roll(t, HEAD_DIM // 2, axis=1))
        return t * cos + partner * sin_signed

    def emit(out_ref, col0, width, rotary, scale, transposed):
        for j in range(width // LANES):
            t = proj[:, col0 + j * LANES: col0 + (j + 1) * LANES]
            if rotary:
                t = rope(t)
            if scale != 1.0:
                t = t * scale
            if transposed:
                rows = out_ref.shape[2] // (width // LANES)
                out_ref[0, 0, j * rows:j * rows + LANES, :] = t.T.astype(out_ref.dtype)
                if rows > LANES:
                    fill = lax.broadcasted_iota(jnp.int32, (rows - LANES, t.shape[0]), 0) == 0
                    out_ref[0, 0, j * rows + LANES:(j + 1) * rows, :] = fill.astype(out_ref.dtype)
            else:
                out_ref[:, j * LANES:(j + 1) * LANES] = t.astype(out_ref.dtype)

    swa_scale = 1.0 / math.sqrt(HEAD_DIM)
    diff_scale = math.log2(math.e) / math.sqrt(HEAD_DIM)
    col = 0
    for out_ref, width, rotary, scale, transposed in (
            (qa_ref, qa_ref.shape[1], True, swa_scale, False), (ka_ref, ka_ref.shape[1], True, 1.0, False),
            (va_ref, va_ref.shape[1], False, 1.0, False), (qd_ref, qd_ref.shape[2], True, diff_scale, True),
            (kd_ref, kd_ref.shape[1], True, 1.0, False),
            (vd_ref, vd_ref.shape[2] // VT_ROWS * LANES, False, 1.0, True)):
        emit(out_ref, col, width, rotary, scale, transposed)
        col += width


def _inproj(x2, pos2, inv_lane, mod3, g_mix, w_ext, b_ext, S, widths):
    N, D = x2.shape
    tm = TM_PROJ
    C = w_ext.shape[1]
    tiles_per_seq = S // tm
    row = lambda i: (i, 0)
    t_rows = (0, 0, 0, widths[3], 0, widths[5] // LANES * VT_ROWS)
    out_shape, out_specs = [], []
    for w, tr in zip(widths, t_rows):
        if tr:
            out_shape.append(jax.ShapeDtypeStruct((N // S, tiles_per_seq, tr, tm), BF16))
            out_specs.append(pl.BlockSpec((1, 1, tr, tm), lambda i: (i // tiles_per_seq, i % tiles_per_seq, 0, 0)))
        else:
            out_shape.append(jax.ShapeDtypeStruct((N, w), BF16))
            out_specs.append(pl.BlockSpec((tm, w), row))
    return pl.pallas_call(
        _inproj_kernel,
        out_shape=out_shape,
        grid=(N // tm,),
        in_specs=[pl.BlockSpec((tm, D), row),
                  pl.BlockSpec((tm, 1), row),
                  pl.BlockSpec((1, LANES), lambda i: (0, 0)),
                  pl.BlockSpec((1, N_MOD, D), lambda i: (i // tiles_per_seq, 0, 0)),
                  pl.BlockSpec((1, D), lambda i: (0, 0)),
                  pl.BlockSpec((D, C), lambda i: (0, 0)),
                  pl.BlockSpec((1, C), lambda i: (0, 0))],
        out_specs=out_specs,
        compiler_params=_cparams(("arbitrary",)),
        name="inproj",
    )(x2, pos2, inv_lane, mod3, g_mix, w_ext, b_ext)


def _swa_kernel(sink_ref, q_ref, kc_ref, kp_ref, vc_ref, vp_ref, o_ref):
    i = pl.program_id(1)
    tq = q_ref.shape[0]
    lane = lax.broadcasted_iota(jnp.int32, (1, LANES), 1)
    lo = lane < HEAD_DIM
    qi = lax.broadcasted_iota(jnp.int32, (WINDOW, 2 * WINDOW), 0) + WINDOW
    kj = lax.broadcasted_iota(jnp.int32, (WINDOW, 2 * WINDOW), 1)
    band = (qi - kj >= 0) & (qi - kj < WINDOW)
    dn = (((1,), (1,)), ((), ()))
    for c in range(tq // WINDOW):
        if c == 0:
            kcat = jnp.concatenate([kp_ref[...], kc_ref[0:WINDOW, :]], axis=0)
            vcat = jnp.concatenate([vp_ref[...], vc_ref[0:WINDOW, :]], axis=0)
            mask = band & (kj >= jnp.where(i > 0, 0, WINDOW))
        else:
            kcat = kc_ref[(c - 1) * WINDOW:(c + 1) * WINDOW, :]
            vcat = vc_ref[(c - 1) * WINDOW:(c + 1) * WINDOW, :]
            mask = band
        for j in range(SWA_KV_HEADS):
            kj2 = kcat[:, j * LANES:(j + 1) * LANES]
            vj2 = vcat[:, j * LANES:(j + 1) * LANES]
            zero = jnp.zeros_like(kj2)
            k_halves = (jnp.where(lo, kj2, zero), jnp.where(lo, zero, kj2))
            v_halves = (jnp.where(lo, vj2, zero), jnp.where(lo, zero, vj2))
            for p in range(SWA_GROUP // 2):
                g = j * (SWA_GROUP // 2) + p
                q = q_ref[c * WINDOW:(c + 1) * WINDOW, g * LANES:(g + 1) * LANES]
                out = jnp.zeros((WINDOW, LANES), F32)
                for half in range(2):
                    sink = sink_ref[2 * g + half]
                    s = lax.dot_general(q, k_halves[half], dn, preferred_element_type=F32)
                    s = jnp.where(mask, s, NEG_INF)
                    m = jnp.maximum(jnp.max(s, axis=1, keepdims=True), sink)
                    e = jnp.exp(s - m)
                    denom = jnp.sum(e, axis=1, keepdims=True) + jnp.exp(sink - m)
                    pv = jnp.dot(e.astype(BF16), v_halves[half], preferred_element_type=F32)
                    out = out + pv / denom
                o_ref[c * WINDOW:(c + 1) * WINDOW, g * LANES:(g + 1) * LANES] = out.astype(o_ref.dtype)


def _swa(sinks, qa, ka2, va2, B, S):
    N = qa.shape[0]
    tq = TQ_SWA
    nq = S // tq
    wpt = tq // WINDOW
    wps = S // WINDOW
    cur = lambda b, i: (b * nq + i, 0)
    prev = lambda b, i: (b * wps + jnp.maximum(i * wpt - 1, 0), 0)
    return pl.pallas_call(
        _swa_kernel,
        out_shape=jax.ShapeDtypeStruct((N, qa.shape[1]), BF16),
        grid=(B, nq),
        in_specs=[pl.BlockSpec(memory_space=pltpu.SMEM),
                  pl.BlockSpec((tq, qa.shape[1]), cur),
                  pl.BlockSpec((tq, ka2.shape[1]), cur),
                  pl.BlockSpec((WINDOW, ka2.shape[1]), prev),
                  pl.BlockSpec((tq, va2.shape[1]), cur),
                  pl.BlockSpec((WINDOW, va2.shape[1]), prev)],
        out_specs=pl.BlockSpec((tq, qa.shape[1]), cur),
        compiler_params=_cparams(("arbitrary", "arbitrary")),
        name="swa",
    )(sinks, qa, ka2, ka2, va2, va2)


def _diff_kernel(lam_ref, g_ref, qt_ref, k_ref, vt_ref, o_ref, sa_ref, sb_ref, m_ref, acc_ref, *, lambda_init):
    i = pl.program_id(2)
    tq = qt_ref.shape[3]
    tk = vt_ref.shape[3]
    qt = qt_ref[0, 0]
    lane = lax.broadcasted_iota(jnp.int32, (1, LANES), 1)
    lo = lane < HEAD_DIM
    m_ref[...] = jnp.full(m_ref.shape, NEG_INF, F32)
    acc_ref[...] = jnp.zeros(acc_ref.shape, F32)

    def scores(c, s_ref):
        k = k_ref[pl.ds(pl.multiple_of(c * tk, tk), tk), :]
        zero = jnp.zeros_like(k)
        s_ref[0] = jnp.dot(jnp.where(lo, k, zero), qt, preferred_element_type=F32)
        s_ref[1] = jnp.dot(jnp.where(lo, zero, k), qt, preferred_element_type=F32)

    def consume(c, s_ref, diagonal):
        vt = vt_ref[0, c]
        for mp in range(2):
            s = s_ref[mp]
            if diagonal:
                kpos = lax.broadcasted_iota(jnp.int32, (tk, tq), 0)
                qpos = lax.broadcasted_iota(jnp.int32, (tk, tq), 1)
                s = jnp.where(kpos <= qpos, s, NEG_INF)
            m_prev = m_ref[mp]
            m_new = jnp.maximum(m_prev, jnp.max(s, axis=0, keepdims=True))
            alpha = jnp.exp2(m_prev - m_new)
            p = jnp.exp2(s - m_new).astype(BF16)
            acc_ref[mp] = alpha * acc_ref[mp] + jnp.dot(vt, p, preferred_element_type=F32)
            m_ref[mp] = m_new

    scores(0, sa_ref)

    def pair(jj, carry):
        c = 2 * jj
        scores(c + 1, sb_ref)
        consume(c, sa_ref, False)
        scores(c + 2, sa_ref)
        consume(c + 1, sb_ref, False)
        return carry

    lax.fori_loop(0, lax.shift_right_logical(i, 1), pair, 0)

    @pl.when(i % 2 == 0)
    def _():
        consume(i, sa_ref, True)

    @pl.when(i % 2 == 1)
    def _():
        scores(i, sb_ref)
        consume(i - 1, sa_ref, False)
        consume(i, sb_ref, True)

    lq1, lk1, lq2, lk2 = (lam_ref[r:r + 1, :] for r in range(4))
    lam = (jnp.exp(jnp.sum(lq1 * lk1, axis=1, keepdims=True))
           - jnp.exp(jnp.sum(lq2 * lk2, axis=1, keepdims=True)) + lambda_init)
    d = DIFF_V_DIM
    ot = (acc_ref[0, 0:d, :] / acc_ref[0, d:d + 1, :]
          - lam * (acc_ref[1, 0:d, :] / acc_ref[1, d:d + 1, :]))
    ot = ot * lax.rsqrt(jnp.mean(ot * ot, axis=0, keepdims=True) + EPS)
    o_ref[...] = (ot.T * g_ref[...] * (1.0 - lambda_init)).astype(o_ref.dtype)


def _diffattn(lam_vecs, g_subln, qdt, kd, vdt, B, S, lambda_init):
    N, C = kd.shape
    tq = TQ_DIFF
    nq = S // tq
    return pl.pallas_call(
        functools.partial(_diff_kernel, lambda_init=lambda_init),
        out_shape=jax.ShapeDtypeStruct((N, C), BF16),
        grid=(B, DIFF_HEADS, nq),
        in_specs=[pl.BlockSpec((4, HEAD_DIM), lambda b, h, i: (0, 0)),
                  pl.BlockSpec((1, DIFF_V_DIM), lambda b, h, i: (0, 0)),
                  pl.BlockSpec((1, 1, LANES, tq), lambda b, h, i: (b, i, h, 0)),
                  pl.BlockSpec((S, LANES), lambda b, h, i: (b, h)),
                  pl.BlockSpec((1, nq, VT_ROWS, tq), lambda b, h, i: (b, 0, h, 0))],
        out_specs=pl.BlockSpec((tq, LANES), lambda b, h, i: (b * nq + i, h)),
        scratch_shapes=[pltpu.VMEM((2, tq, tq), F32), pltpu.VMEM((2, tq, tq), F32),
                        pltpu.VMEM((2, 1, tq), F32), pltpu.VMEM((2, VT_ROWS, tq), F32)],
        compiler_params=_cparams(("arbitrary", "arbitrary", "arbitrary")),
        name="diffattn",
    )(lam_vecs, g_subln, qdt, kd, vdt)


def _outproj_kernel(oa_ref, ob_ref, x_ref, mod_ref, wo_ref, bo_ref, g_ref, wr_ref, br_ref,
                    x1_ref, h2_ref, idx_ref, gate_ref, rank_ref, cnt_ref, carry_ref):
    i = pl.program_id(0)
    tm = x_ref.shape[0]
    half = oa_ref.shape[1]

    @pl.when(i == 0)
    def _():
        carry_ref[...] = jnp.zeros(carry_ref.shape, F32)

    gt1 = mod_ref[0, 2:3, :]
    sh2 = mod_ref[0, 3:4, :]
    sc2 = mod_ref[0, 4:5, :]
    mixed = (jnp.dot(oa_ref[...], wo_ref[0:half, :], preferred_element_type=F32)
             + jnp.dot(ob_ref[...], wo_ref[half:, :], preferred_element_type=F32) + bo_ref[...])
    x1 = x_ref[...] + gt1 * mixed
    x1_ref[...] = x1
    h2 = _rms(x1) * g_ref[...] * (1.0 + sc2) + sh2
    dn = (((1,), (1,)), ((), ()))
    h_hi = h2.astype(BF16)
    h2_ref[...] = _pack_bf16_pairs(h2)
    h_lo = (h2 - h_hi.astype(F32)).astype(BF16)
    w = wr_ref[...]
    w_hi = w.astype(BF16)
    w_lo = (w - w_hi.astype(F32)).astype(BF16)
    logits = (lax.dot_general(w_hi, h_hi, dn, preferred_element_type=F32)
              + lax.dot_general(w_hi, h_lo, dn, preferred_element_type=F32)
              + lax.dot_general(w_lo, h_hi, dn, preferred_element_type=F32)
              + br_ref[...])

    eidx = lax.broadcasted_iota(jnp.int32, logits.shape, 0)
    vals = logits
    onehots, top_vals, top_idx = [], [], []
    for _k in range(TOP_K):
        mx = jnp.max(vals, axis=0, keepdims=True)
        sel = jnp.min(jnp.where(vals == mx, eidx, N_EXPERTS), axis=0, keepdims=True)
        oh = eidx == sel
        onehots.append(oh)
        top_vals.append(mx)
        top_idx.append(sel)
        vals = jnp.where(oh, NEG_INF, vals)
    exps = [jnp.exp(v - top_vals[0]) for v in top_vals]
    denom = exps[0] + exps[1] + exps[2] + exps[3]
    gate_ref[...] = jnp.concatenate([e / denom for e in exps], axis=0)
    idx_ref[...] = jnp.concatenate(top_idx, axis=0)

    member = (onehots[0] | onehots[1] | onehots[2] | onehots[3])
    member_f = member.astype(F32)
    t_src = lax.broadcasted_iota(jnp.int32, (tm, tm), 0)
    t_dst = lax.broadcasted_iota(jnp.int32, (tm, tm), 1)
    before = (t_src < t_dst).astype(BF16)
    prefix = jnp.dot(member.astype(BF16), before, preferred_element_type=F32) + carry_ref[...]
    ranks = [jnp.sum(jnp.where(oh, prefix, 0.0), axis=0, keepdims=True) for oh in onehots]
    rank_ref[...] = jnp.concatenate(ranks, axis=0).astype(jnp.int32)
    carry_ref[...] = carry_ref[...] + jnp.sum(member_f, axis=1, keepdims=True)
    cnt_ref[...] = jnp.broadcast_to(carry_ref[...], cnt_ref.shape)


def _outproj(out_a, out_b, x2, mod3, w_out, b_out, g_ffn, wr_t, br_col, S):
    N, D = x2.shape
    tm = TM_OUT
    tiles_per_seq = S // tm
    row = lambda i: (i, 0)
    colb = lambda i: (0, i)
    const = lambda i: (0, 0)
    return pl.pallas_call(
        _outproj_kernel,
        out_shape=[jax.ShapeDtypeStruct((N, D), F32), jax.ShapeDtypeStruct((N, D // 2), jnp.int32),
                   jax.ShapeDtypeStruct((TOP_K, N), jnp.int32), jax.ShapeDtypeStruct((TOP_K, N), F32),
                   jax.ShapeDtypeStruct((TOP_K, N), jnp.int32), jax.ShapeDtypeStruct((N_EXPERTS, LANES), F32)],
        grid=(N // tm,),
        in_specs=[pl.BlockSpec((tm, out_a.shape[1]), row),
                  pl.BlockSpec((tm, out_b.shape[1]), row),
                  pl.BlockSpec((tm, D), row),
                  pl.BlockSpec((1, N_MOD, D), lambda i: (i // tiles_per_seq, 0, 0)),
                  pl.BlockSpec(w_out.shape, const),
                  pl.BlockSpec((1, D), const),
                  pl.BlockSpec((1, D), const),
                  pl.BlockSpec(wr_t.shape, const),
                  pl.BlockSpec((N_EXPERTS, 1), const)],
        out_specs=[pl.BlockSpec((tm, D), row), pl.BlockSpec((tm, D // 2), row),
                   pl.BlockSpec((TOP_K, tm), colb), pl.BlockSpec((TOP_K, tm), colb),
                   pl.BlockSpec((TOP_K, tm), colb), pl.BlockSpec((N_EXPERTS, LANES), const)],
        scratch_shapes=[pltpu.VMEM((N_EXPERTS, 1), F32)],
        compiler_params=_cparams(("arbitrary",)),
        name="outproj_router",
    )(out_a, out_b, x2, mod3, w_out, b_out, g_ffn, wr_t, br_col)


def _experts_kernel(first_ref, nblk_ref, x_ref, w1_ref, b1g_ref, b1l_ref, w2_ref, b2_ref, y_ref,
                    w1g_s, w1l_s, w2_s, xbuf, ybuf, xsem, ysem):
    e = pl.program_id(0)
    first = first_ref[e]
    n_blk = nblk_ref[e]
    bm = xbuf.shape[1]

    def x_copy(j, slot):
        rows = pl.ds(pl.multiple_of((first + j) * bm, bm), bm)
        return pltpu.make_async_copy(x_ref.at[rows], xbuf.at[slot], xsem.at[slot])

    def y_copy(j, slot):
        rows = pl.ds(pl.multiple_of((first + j) * bm, bm), bm)
        return pltpu.make_async_copy(ybuf.at[slot], y_ref.at[rows], ysem.at[slot])

    @pl.when(n_blk > 0)
    def _():
        x_copy(0, 0).start()
        ch = 256
        half = ch // 2
        for c in range(w1_ref.shape[2] // ch):
            t = w1_ref[0, :, c * ch:(c + 1) * ch].astype(BF16).T
            pairs = pltpu.bitcast(t, jnp.int32)
            cols = slice(c * half, (c + 1) * half)
            w1g_s[:, cols] = lax.bitcast_convert_type(lax.shift_left(pairs, 16), F32).astype(BF16).T
            w1l_s[:, cols] = lax.bitcast_convert_type(pairs & jnp.int32(HIGH_HALF), F32).astype(BF16).T
        w2_s[...] = w2_ref[0].astype(BF16)

    def block(j, carry):
        slot = j & 1
        x_copy(j, slot).wait()

        @pl.when(j + 1 < n_blk)
        def _():
            x_copy(j + 1, 1 - slot).start()

        @pl.when(j >= 2)
        def _():
            y_copy(j - 2, slot).wait()

        x = _unpack_bf16_pairs(xbuf[slot]).astype(BF16)
        ug = jnp.dot(x, w1g_s[...], preferred_element_type=F32) + b1g_ref[0]
        ul = jnp.dot(x, w1l_s[...], preferred_element_type=F32) + b1l_ref[0]
        glu = jnp.minimum(ug, SWIGLU_LIMIT)
        lin = jnp.clip(ul, -SWIGLU_LIMIT, SWIGLU_LIMIT)
        act = glu * jax.nn.sigmoid(SWIGLU_ALPHA * glu) * (lin + 1.0)
        y = jnp.dot(act.astype(BF16), w2_s[...], preferred_element_type=F32) + b2_ref[0]
        ybuf[slot] = _pack_bf16_pairs(y)
        y_copy(j, slot).start()
        return carry

    lax.fori_loop(0, n_blk, block, 0)

    @pl.when(n_blk >= 2)
    def _():
        y_copy(n_blk - 2, n_blk & 1).wait()

    @pl.when(n_blk >= 1)
    def _():
        y_copy(n_blk - 1, (n_blk - 1) & 1).wait()


def _experts(first_blk, n_blk, x_rows, w1, b1g, b1l, w2, b2):
    E, Fh, D = w2.shape
    bm = BLOCK_ROWS
    n_rows = x_rows.shape[0]
    wsel = lambda e, fb, nb: (e, 0, 0)
    return pl.pallas_call(
        _experts_kernel,
        out_shape=jax.ShapeDtypeStruct((n_rows, D // 2), jnp.int32),
        grid_spec=pltpu.PrefetchScalarGridSpec(
            num_scalar_prefetch=2,
            grid=(E,),
            in_specs=[pl.BlockSpec(memory_space=pl.ANY),
                      pl.BlockSpec((1, D, 2 * Fh), wsel),
                      pl.BlockSpec((1, 1, Fh), wsel),
                      pl.BlockSpec((1, 1, Fh), wsel),
                      pl.BlockSpec((1, Fh, D), wsel),
                      pl.BlockSpec((1, 1, D), wsel)],
            out_specs=pl.BlockSpec(memory_space=pl.ANY),
            scratch_shapes=[pltpu.VMEM((D, Fh), BF16), pltpu.VMEM((D, Fh), BF16), pltpu.VMEM((Fh, D), BF16),
                            pltpu.VMEM((2, bm, D // 2), jnp.int32), pltpu.VMEM((2, bm, D // 2), jnp.int32),
                            pltpu.SemaphoreType.DMA((2,)), pltpu.SemaphoreType.DMA((2,))]),
        compiler_params=_cparams(("arbitrary",), 56 * 1024 * 1024),
        name="experts",
    )(first_blk, n_blk, x_rows, w1, b1g, b1l, w2, b2)


SC_CORES = 2
SC_SUBCORES = 16
SC_CHUNK = 64


def _sc_gather_rows(table, idx):
    M = idx.shape[0]
    D = table.shape[1]
    workers = SC_CORES * SC_SUBCORES
    per_worker = M // workers
    n_chunks = per_worker // SC_CHUNK
    assert M % workers == 0 and per_worker % (2 * SC_CHUNK) == 0
    mesh = plsc.VectorSubcoreMesh(core_axis_name="c", subcore_axis_name="s")

    @functools.partial(
        pl.kernel, mesh=mesh,
        out_type=jax.ShapeDtypeStruct((M, D), table.dtype),
        scratch_types=[pltpu.VMEM((per_worker,), jnp.int32),
                       pltpu.VMEM((SC_CHUNK, D), table.dtype), pltpu.VMEM((SC_CHUNK, D), table.dtype),
                       pltpu.SemaphoreType.DMA, pltpu.SemaphoreType.DMA],
        name="sc_gather_rows")
    def gather(table_hbm, idx_hbm, out_hbm, idx_v, rows0, rows1, sem0, sem1):
        wid = lax.axis_index("s") * SC_CORES + lax.axis_index("c")
        base = wid * per_worker
        pltpu.sync_copy(idx_hbm.at[pl.ds(base, per_worker)], idx_v)

        def fetch(c, buf, sem):
            off = pl.multiple_of(c * SC_CHUNK, SC_CHUNK)
            return pltpu.make_async_copy(table_hbm.at[idx_v.at[pl.ds(off, SC_CHUNK)]], buf, sem)

        def flush(c, buf):
            off = pl.multiple_of(c * SC_CHUNK, SC_CHUNK)
            pltpu.sync_copy(buf, out_hbm.at[pl.ds(base + off, SC_CHUNK)])

        fetch(0, rows0, sem0).start()

        @pl.loop(0, n_chunks // 2)
        def _(jj):
            c = 2 * jj
            fetch(c + 1, rows1, sem1).start()
            fetch(c, rows0, sem0).wait()
            flush(c, rows0)

            @pl.when(c + 2 < n_chunks)
            def _():
                fetch(c + 2, rows0, sem0).start()

            fetch(c + 1, rows1, sem1).wait()
            flush(c + 1, rows1)

    return gather(table, idx)


SC_SCATTER_CHUNK = 128


def _sc_scatter_rows(rows, dest, n_rows):
    N, W = rows.shape
    workers = SC_CORES * SC_SUBCORES
    per_worker = N // workers
    chunks = per_worker // SC_SCATTER_CHUNK
    assert N % workers == 0 and per_worker % SC_SCATTER_CHUNK == 0
    dest3 = dest.reshape(TOP_K, N // SC_SCATTER_CHUNK, SC_SCATTER_CHUNK)
    mesh = plsc.VectorSubcoreMesh(core_axis_name="c", subcore_axis_name="s")

    @functools.partial(
        pl.kernel, mesh=mesh,
        out_type=jax.ShapeDtypeStruct((n_rows, W), rows.dtype),
        scratch_types=[pltpu.VMEM((TOP_K, chunks, SC_SCATTER_CHUNK), jnp.int32),
                       pltpu.VMEM((SC_SCATTER_CHUNK, W), rows.dtype)],
        name="sc_scatter_rows")
    def scatter(rows_hbm, dest_hbm, out_hbm, idx_v, rows_v):
        wid = lax.axis_index("s") * SC_CORES + lax.axis_index("c")
        for k in range(TOP_K):
            pltpu.sync_copy(dest_hbm.at[k, pl.ds(wid * chunks, chunks)], idx_v.at[k])

        @pl.loop(0, chunks)
        def _(j):
            start = pl.multiple_of(wid * per_worker + j * SC_SCATTER_CHUNK, SC_SCATTER_CHUNK)
            pltpu.sync_copy(rows_hbm.at[pl.ds(start, SC_SCATTER_CHUNK)], rows_v)
            for k in range(TOP_K):
                pltpu.sync_copy(rows_v, out_hbm.at[idx_v.at[k, j]])

    return scatter(rows, dest3)


def _combine_kernel(x1_ref, gate_ref, mod_ref, g_ref, y_ref, o_ref, *, final_norm):
    gate = gate_ref[...]
    moe = gate[:, 0:1] * _unpack_bf16_pairs(y_ref[0])
    for k in range(1, TOP_K):
        moe = moe + gate[:, k:k + 1] * _unpack_bf16_pairs(y_ref[k])
    gt2 = mod_ref[0, 5:6, :]
    x2 = x1_ref[...] + gt2 * moe
    o_ref[...] = _rms(x2) * g_ref[...] if final_norm else x2


def _combine(x1, gate_t, mod3, g_final, y_tok, S, final_norm):
    N, D = x1.shape
    tm = TM_ROWS
    tiles_per_seq = S // tm
    row = lambda i: (i, 0)
    return pl.pallas_call(
        functools.partial(_combine_kernel, final_norm=final_norm),
        out_shape=jax.ShapeDtypeStruct((N, D), F32),
        grid=(N // tm,),
        in_specs=[pl.BlockSpec((tm, D), row),
                  pl.BlockSpec((tm, TOP_K), row),
                  pl.BlockSpec((1, N_MOD, D), lambda i: (i // tiles_per_seq, 0, 0)),
                  pl.BlockSpec((1, D), lambda i: (0, 0)),
                  pl.BlockSpec((TOP_K, tm, D // 2), lambda i: (0, i, 0))],
        out_specs=pl.BlockSpec((tm, D), row),
        compiler_params=_cparams(("arbitrary",)),
        name="combine",
    )(x1, gate_t, mod3, g_final, y_tok)


def _routing_tables(counts, idx, rank, n_rows):
    bm = BLOCK_ROWS
    counts = counts.astype(jnp.int32)
    padded = (counts + bm - 1) // bm * bm
    pends = jnp.cumsum(padded)
    pstarts = pends - padded
    experts = jnp.arange(N_EXPERTS, dtype=jnp.int32)
    dest = jnp.sum(jnp.where(idx[..., None] == experts, pstarts, 0), axis=-1) + rank
    return dest.astype(jnp.int32), (pstarts // bm).astype(jnp.int32), (padded // bm).astype(jnp.int32)


def _extended_in_weights(w_in, b_in):
    a_q = SWA_Q_HEADS * HEAD_DIM
    a_kv = SWA_KV_HEADS * HEAD_DIM
    b_w = DIFF_HEADS * DIFF_V_DIM
    spans = [(0, a_q)]
    for base in (a_q, a_q + a_kv):
        for j in range(SWA_KV_HEADS):
            spans += [(base + j * HEAD_DIM, base + (j + 1) * HEAD_DIM)] * 2
    spans.append((a_q + 2 * a_kv, a_q + 2 * a_kv + 3 * b_w))
    w_ext = jnp.concatenate([w_in[:, lo:hi] for lo, hi in spans], axis=1).astype(BF16)
    b_ext = jnp.concatenate([b_in[lo:hi] for lo, hi in spans]).reshape(1, -1)
    widths = (a_q, 2 * a_kv, 2 * a_kv, b_w, b_w, b_w)
    return w_ext, b_ext, widths


def kernel(x, c, positions, w_ada, b_ada, g_mix, w_in, b_in, attn_sinks, lambda_q1, lambda_k1, lambda_q2,
           lambda_k2, g_subln, w_out, b_out, g_ffn, w_router, b_router, w1, b1, w2, b2, g_final):
    B, S, D = x.shape
    N = B * S
    depth = w_ada.shape[0]
    assert TM_PROJ == TQ_DIFF, "the input projection writes q/v transposed per attention tile"
    n_rows = (N * TOP_K + N_EXPERTS * (BLOCK_ROWS - 1) + BLOCK_ROWS - 1) // BLOCK_ROWS * BLOCK_ROWS

    inv = 1.0 / (ROPE_THETA ** (jnp.arange(0, HEAD_DIM, 2, dtype=F32) / HEAD_DIM))
    inv_lane = jnp.tile(inv, LANES // (HEAD_DIM // 2)).reshape(1, LANES)
    pos2 = positions.reshape(N, 1)
    xcur = x.reshape(N, D)

    for layer in range(depth):
        last = layer == depth - 1
        lambda_init = 0.8 - 0.6 * math.exp(-0.3 * layer)
        mod3 = _adaln(c, w_ada[layer], b_ada[layer]).reshape(B, N_MOD, D)

        w_ext, b_ext, widths = _extended_in_weights(w_in[layer], b_in[layer])
        qa, ka2, va2, qdt, kd, vdt = _inproj(xcur, pos2, inv_lane, mod3, g_mix[layer].reshape(1, D),
                                             w_ext, b_ext, S, widths)
        out_a = _swa(attn_sinks[layer], qa, ka2, va2, B, S)
        lam_vecs = jnp.stack([lambda_q1[layer], lambda_k1[layer], lambda_q2[layer], lambda_k2[layer]])
        out_b = _diffattn(lam_vecs, g_subln[layer].reshape(1, DIFF_V_DIM), qdt, kd, vdt, B, S, lambda_init)

        x1, h2, idx, gate, rank, counts = _outproj(
            out_a, out_b, xcur, mod3, w_out[layer].astype(BF16), b_out[layer].reshape(1, D),
            g_ffn[layer].reshape(1, D), w_router[layer].T, b_router[layer].reshape(N_EXPERTS, 1), S)

        dest, first_blk, n_blk = _routing_tables(counts[:, 0], idx, rank, n_rows)
        x_rows = _sc_scatter_rows(h2, dest, n_rows)
        y_rows = _experts(first_blk, n_blk, x_rows, w1[layer],
                          b1[layer][:, None, 0::2], b1[layer][:, None, 1::2],
                          w2[layer], b2[layer][:, None, :])
        y_tok = _sc_gather_rows(y_rows, dest.reshape(-1)).reshape(TOP_K, N, D // 2)
        xcur = _combine(x1, gate.T, mod3, g_final.reshape(1, D), y_tok, S, final_norm=last)
    return xcur.reshape(B, S, D)
```

```python
import functools
import math

import jax
import jax.numpy as jnp
from jax import lax
from jax.experimental import pallas as pl
from jax.experimental.pallas import tpu as pltpu
from jax.experimental.pallas import tpu_sc as plsc

HEAD_DIM = 64
SWA_Q_HEADS = 8
SWA_KV_HEADS = 2
SWA_GROUP = SWA_Q_HEADS // SWA_KV_HEADS
WINDOW = 128
DIFF_HEADS = 4
DIFF_V_DIM = 2 * HEAD_DIM
ROPE_THETA = 10000.0
N_EXPERTS = 32
TOP_K = 4
SWIGLU_ALPHA = 1.702
SWIGLU_LIMIT = 7.0
EPS = 1e-5
N_MOD = 6

LANES = 128
LANE_SHIFT = LANES.bit_length() - 1
TOP_K_SHIFT = TOP_K.bit_length() - 1
assert 1 << LANE_SHIFT == LANES and 1 << TOP_K_SHIFT == TOP_K
F32 = jnp.float32
BF16 = jnp.bfloat16
NEG_INF = float("-inf")

TM_PROJ = 1024
TQ_SWA = 512
TQ_DIFF = 1024
VT_ROWS = DIFF_V_DIM + 16
TM_OUT = 512
TM_ROWS = 256
BLOCK_ROWS = 512
VMEM_LIMIT = 48 * 1024 * 1024


def _cparams(sem, vmem=VMEM_LIMIT):
    return pltpu.CompilerParams(dimension_semantics=sem, vmem_limit_bytes=vmem)


def _adaln_kernel(ct_ref, w_ref, b_ref, o_ref):
    c = ct_ref[...]
    cond = c * jax.nn.sigmoid(c)
    w = w_ref[...]
    rows = [jnp.sum(w * cond[:, b:b + 1], axis=0, keepdims=True) for b in range(c.shape[1])]
    o_ref[...] = jnp.concatenate(rows, axis=0) + b_ref[...]


def _adaln(c, w_ada, b_ada):
    B, D = c.shape
    n_out = w_ada.shape[1]
    tn = 1024
    return pl.pallas_call(
        _adaln_kernel,
        out_shape=jax.ShapeDtypeStruct((B, n_out), F32),
        grid=(n_out // tn,),
        in_specs=[pl.BlockSpec((D, B), lambda j: (0, 0)),
                  pl.BlockSpec((D, tn), lambda j: (0, j)),
                  pl.BlockSpec((1, tn), lambda j: (0, j))],
        out_specs=pl.BlockSpec((B, tn), lambda j: (0, j)),
        compiler_params=_cparams(("arbitrary",)),
        name="adaln",
    )(c.T, w_ada, b_ada.reshape(1, n_out))


def _rms(x):
    return x * lax.rsqrt(jnp.mean(x * x, axis=-1, keepdims=True) + EPS)


HIGH_HALF = -65536


def _pack_bf16_pairs(v):
    bits = lax.bitcast_convert_type(v.astype(BF16).astype(F32), jnp.int32)
    half = v.shape[1] // 2
    return lax.shift_right_logical(bits[:, :half], 16) | (bits[:, half:] & jnp.int32(HIGH_HALF))


def _unpack_bf16_pairs(w):
    return jnp.concatenate([lax.bitcast_convert_type(lax.shift_left(w, 16), F32),
                            lax.bitcast_convert_type(w & jnp.int32(HIGH_HALF), F32)], axis=1)


def _inproj_kernel(x_ref, pos_ref, inv_ref, mod_ref, g_ref, w_ref, b_ref,
                   qa_ref, ka_ref, va_ref, qd_ref, kd_ref, vd_ref):
    x = x_ref[...]
    sh = mod_ref[0, 0:1, :]
    sc = mod_ref[0, 1:2, :]
    h = _rms(x) * g_ref[...] * (1.0 + sc) + sh
    proj = jnp.dot(h.astype(BF16), w_ref[...], preferred_element_type=F32) + b_ref[...]

    lane = lax.broadcasted_iota(jnp.int32, (1, LANES), 1)
    first_half = (lane & (HEAD_DIM - 1)) < (HEAD_DIM // 2)
    n_freq = HEAD_DIM // 2
    groups = LANES // n_freq
    tm = x.shape[0]
    rows = tm // groups
    group = lax.shift_right_logical(lane, n_freq.bit_length() - 1)
    pos = pos_ref[...].astype(F32)
    pos_q = pos[0:rows]
    for g in range(1, groups):
        pos_q = jnp.where(group == g, pos[g * rows:(g + 1) * rows], pos_q)
    ang_q = pos_q * inv_ref[...]

    def spread(table_q):
        blocks = []
        for g in range(groups):
            only = jnp.where(group == g, table_q, 0.0)
            full = only
            for r in range(1, groups):
                full = full + pltpu.roll(only, r * n_freq, axis=1)
            blocks.append(full)
        return jnp.concatenate(blocks, axis=0)

    cos = spread(jnp.cos(ang_q))
    sin = spread(jnp.sin(ang_q))
    sin_signed = jnp.where(first_half, -sin, sin)

    def rope(t):
        partner = jnp.where(first_half,
                            pltpu.roll(t, LANES - HEAD_DIM // 2, axis=1),
                            pltpu.roll(t, HEAD_DIM // 2, axis=1))
        return t * cos + partner * sin_signed

    def emit(out_ref, col0, width, rotary, scale, transposed):
        for j in range(width // LANES):
            t = proj[:, col0 + j * LANES: col0 + (j + 1) * LANES]
            if rotary:
                t = rope(t)
            if scale != 1.0:
                t = t * scale
            if transposed:
                rows = out_ref.shape[2] // (width // LANES)
                out_ref[0, 0, j * rows:j * rows + LANES, :] = t.T.astype(out_ref.dtype)
                if rows > LANES:
                    fill = lax.broadcasted_iota(jnp.int32, (rows - LANES, t.shape[0]), 0) == 0
                    out_ref[0, 0, j * rows + LANES:(j + 1) * rows, :] = fill.astype(out_ref.dtype)
            else:
                out_ref[:, j * LANES:(j + 1) * LANES] = t.astype(out_ref.dtype)

    swa_scale = 1.0 / math.sqrt(HEAD_DIM)
    diff_scale = math.log2(math.e) / math.sqrt(HEAD_DIM)
    col = 0
    for out_ref, width, rotary, scale, transposed in (
            (qa_ref, qa_ref.shape[1], True, swa_scale, False), (ka_ref, ka_ref.shape[1], True, 1.0, False),
            (va_ref, va_ref.shape[1], False, 1.0, False), (qd_ref, qd_ref.shape[2], True, diff_scale, True),
            (kd_ref, kd_ref.shape[1], True, 1.0, False),
            (vd_ref, vd_ref.shape[2] // VT_ROWS * LANES, False, 1.0, True)):
        emit(out_ref, col, width, rotary, scale, transposed)
        col += width


def _inproj(x2, pos2, inv_lane, mod3, g_mix, w_ext, b_ext, S, widths):
    N, D = x2.shape
    tm = TM_PROJ
    C = w_ext.shape[1]
    tiles_per_seq = S // tm
    row = lambda i: (i, 0)
    t_rows = (0, 0, 0, widths[3], 0, widths[5] // LANES * VT_ROWS)
    out_shape, out_specs = [], []
    for w, tr in zip(widths, t_rows):
        if tr:
            out_shape.append(jax.ShapeDtypeStruct((N // S, tiles_per_seq, tr, tm), BF16))
            out_specs.append(pl.BlockSpec((1, 1, tr, tm), lambda i: (i // tiles_per_seq, i % tiles_per_seq, 0, 0)))
        else:
            out_shape.append(jax.ShapeDtypeStruct((N, w), BF16))
            out_specs.append(pl.BlockSpec((tm, w), row))
    return pl.pallas_call(
        _inproj_kernel,
        out_shape=out_shape,
        grid=(N // tm,),
        in_specs=[pl.BlockSpec((tm, D), row),
                  pl.BlockSpec((tm, 1), row),
                  pl.BlockSpec((1, LANES), lambda i: (0, 0)),
                  pl.BlockSpec((1, N_MOD, D), lambda i: (i // tiles_per_seq, 0, 0)),
                  pl.BlockSpec((1, D), lambda i: (0, 0)),
                  pl.BlockSpec((D, C), lambda i: (0, 0)),
                  pl.BlockSpec((1, C), lambda i: (0, 0))],
        out_specs=out_specs,
        compiler_params=_cparams(("arbitrary",)),
        name="inproj",
    )(x2, pos2, inv_lane, mod3, g_mix, w_ext, b_ext)


def _swa_kernel(sink_ref, q_ref, kc_ref, kp_ref, vc_ref, vp_ref, o_ref):
    i = pl.program_id(1)
    tq = q_ref.shape[0]
    lane = lax.broadcasted_iota(jnp.int32, (1, LANES), 1)
    lo = lane < HEAD_DIM
    qi = lax.broadcasted_iota(jnp.int32, (WINDOW, 2 * WINDOW), 0) + WINDOW
    kj = lax.broadcasted_iota(jnp.int32, (WINDOW, 2 * WINDOW), 1)
    band = (qi - kj >= 0) & (qi - kj < WINDOW)
    dn = (((1,), (1,)), ((), ()))
    for c in range(tq // WINDOW):
        if c == 0:
            kcat = jnp.concatenate([kp_ref[...], kc_ref[0:WINDOW, :]], axis=0)
            vcat = jnp.concatenate([vp_ref[...], vc_ref[0:WINDOW, :]], axis=0)
            mask = band & (kj >= jnp.where(i > 0, 0, WINDOW))
        else:
            kcat = kc_ref[(c - 1) * WINDOW:(c + 1) * WINDOW, :]
            vcat = vc_ref[(c - 1) * WINDOW:(c + 1) * WINDOW, :]
            mask = band
        for j in range(SWA_KV_HEADS):
            kj2 = kcat[:, j * LANES:(j + 1) * LANES]
            vj2 = vcat[:, j * LANES:(j + 1) * LANES]
            zero = jnp.zeros_like(kj2)
            k_halves = (jnp.where(lo, kj2, zero), jnp.where(lo, zero, kj2))
            v_halves = (jnp.where(lo, vj2, zero), jnp.where(lo, zero, vj2))
            for p in range(SWA_GROUP // 2):
                g = j * (SWA_GROUP // 2) + p
                q = q_ref[c * WINDOW:(c + 1) * WINDOW, g * LANES:(g + 1) * LANES]
                out = jnp.zeros((WINDOW, LANES), F32)
                for half in range(2):
                    sink = sink_ref[2 * g + half]
                    s = lax.dot_general(q, k_halves[half], dn, preferred_element_type=F32)
                    s = jnp.where(mask, s, NEG_INF)
                    m = jnp.maximum(jnp.max(s, axis=1, keepdims=True), sink)
                    e = jnp.exp(s - m)
                    denom = jnp.sum(e, axis=1, keepdims=True) + jnp.exp(sink - m)
                    pv = jnp.dot(e.astype(BF16), v_halves[half], preferred_element_type=F32)
                    out = out + pv / denom
                o_ref[c * WINDOW:(c + 1) * WINDOW, g * LANES:(g + 1) * LANES] = out.astype(o_ref.dtype)


def _swa(sinks, qa, ka2, va2, B, S):
    N = qa.shape[0]
    tq = TQ_SWA
    nq = S // tq
    wpt = tq // WINDOW
    wps = S // WINDOW
    cur = lambda b, i: (b * nq + i, 0)
    prev = lambda b, i: (b * wps + jnp.maximum(i * wpt - 1, 0), 0)
    return pl.pallas_call(
        _swa_kernel,
        out_shape=jax.ShapeDtypeStruct((N, qa.shape[1]), BF16),
        grid=(B, nq),
        in_specs=[pl.BlockSpec(memory_space=pltpu.SMEM),
                  pl.BlockSpec((tq, qa.shape[1]), cur),
                  pl.BlockSpec((tq, ka2.shape[1]), cur),
                  pl.BlockSpec((WINDOW, ka2.shape[1]), prev),
                  pl.BlockSpec((tq, va2.shape[1]), cur),
                  pl.BlockSpec((WINDOW, va2.shape[1]), prev)],
        out_specs=pl.BlockSpec((tq, qa.shape[1]), cur),
        compiler_params=_cparams(("arbitrary", "arbitrary")),
        name="swa",
    )(sinks, qa, ka2, ka2, va2, va2)


def _diff_kernel(lam_ref, g_ref, qt_ref, k_ref, vt_ref, o_ref, sa_ref, sb_ref, m_ref, acc_ref, *, lambda_init):
    i = pl.program_id(2)
    tq = qt_ref.shape[3]
    tk = vt_ref.shape[3]
    qt = qt_ref[0, 0]
    lane = lax.broadcasted_iota(jnp.int32, (1, LANES), 1)
    lo = lane < HEAD_DIM
    m_ref[...] = jnp.full(m_ref.shape, NEG_INF, F32)
    acc_ref[...] = jnp.zeros(acc_ref.shape, F32)

    def scores(c, s_ref):
        k = k_ref[pl.ds(pl.multiple_of(c * tk, tk), tk), :]
        zero = jnp.zeros_like(k)
        s_ref[0] = jnp.dot(jnp.where(lo, k, zero), qt, preferred_element_type=F32)
        s_ref[1] = jnp.dot(jnp.where(lo, zero, k), qt, preferred_element_type=F32)

    def consume(c, s_ref, diagonal):
        vt = vt_ref[0, c]
        for mp in range(2):
            s = s_ref[mp]
            if diagonal:
                kpos = lax.broadcasted_iota(jnp.int32, (tk, tq), 0)
                qpos = lax.broadcasted_iota(jnp.int32, (tk, tq), 1)
                s = jnp.where(kpos <= qpos, s, NEG_INF)
            m_prev = m_ref[mp]
            m_new = jnp.maximum(m_prev, jnp.max(s, axis=0, keepdims=True))
            alpha = jnp.exp2(m_prev - m_new)
            p = jnp.exp2(s - m_new).astype(BF16)
            acc_ref[mp] = alpha * acc_ref[mp] + jnp.dot(vt, p, preferred_element_type=F32)
            m_ref[mp] = m_new

    scores(0, sa_ref)

    def pair(jj, carry):
        c = 2 * jj
        scores(c + 1, sb_ref)
        consume(c, sa_ref, False)
        scores(c + 2, sa_ref)
        consume(c + 1, sb_ref, False)
        return carry

    lax.fori_loop(0, lax.shift_right_logical(i, 1), pair, 0)

    @pl.when(i % 2 == 0)
    def _():
        consume(i, sa_ref, True)

    @pl.when(i % 2 == 1)
    def _():
        scores(i, sb_ref)
        consume(i - 1, sa_ref, False)
        consume(i, sb_ref, True)

    lq1, lk1, lq2, lk2 = (lam_ref[r:r + 1, :] for r in range(4))
    lam = (jnp.exp(jnp.sum(lq1 * lk1, axis=1, keepdims=True))
           - jnp.exp(jnp.sum(lq2 * lk2, axis=1, keepdims=True)) + lambda_init)
    d = DIFF_V_DIM
    ot = (acc_ref[0, 0:d, :] / acc_ref[0, d:d + 1, :]
          - lam * (acc_ref[1, 0:d, :] / acc_ref[1, d:d + 1, :]))
    ot = ot * lax.rsqrt(jnp.mean(ot * ot, axis=0, keepdims=True) + EPS)
    o_ref[...] = (ot.T * g_ref[...] * (1.0 - lambda_init)).astype(o_ref.dtype)


def _diffattn(lam_vecs, g_subln, qdt, kd, vdt, B, S, lambda_init):
    N, C = kd.shape
    tq = TQ_DIFF
    nq = S // tq
    return pl.pallas_call(
        functools.partial(_diff_kernel, lambda_init=lambda_init),
        out_shape=jax.ShapeDtypeStruct((N, C), BF16),
        grid=(B, DIFF_HEADS, nq),
        in_specs=[pl.BlockSpec((4, HEAD_DIM), lambda b, h, i: (0, 0)),
                  pl.BlockSpec((1, DIFF_V_DIM), lambda b, h, i: (0, 0)),
                  pl.BlockSpec((1, 1, LANES, tq), lambda b, h, i: (b, i, h, 0)),
                  pl.BlockSpec((S, LANES), lambda b, h, i: (b, h)),
                  pl.BlockSpec((1, nq, VT_ROWS, tq), lambda b, h, i: (b, 0, h, 0))],
        out_specs=pl.BlockSpec((tq, LANES), lambda b, h, i: (b * nq + i, h)),
        scratch_shapes=[pltpu.VMEM((2, tq, tq), F32), pltpu.VMEM((2, tq, tq), F32),
                        pltpu.VMEM((2, 1, tq), F32), pltpu.VMEM((2, VT_ROWS, tq), F32)],
        compiler_params=_cparams(("arbitrary", "arbitrary", "arbitrary")),
        name="diffattn",
    )(lam_vecs, g_subln, qdt, kd, vdt)


def _outproj_kernel(oa_ref, ob_ref, x_ref, mod_ref, wo_ref, bo_ref, g_ref, wr_ref, br_ref,
                    x1_ref, h2_ref, idx_ref, gate_ref, rank_ref, cnt_ref, carry_ref):
    i = pl.program_id(0)
    tm = x_ref.shape[0]
    half = oa_ref.shape[1]

    @pl.when(i == 0)
    def _():
        carry_ref[...] = jnp.zeros(carry_ref.shape, F32)

    gt1 = mod_ref[0, 2:3, :]
    sh2 = mod_ref[0, 3:4, :]
    sc2 = mod_ref[0, 4:5, :]
    mixed = (jnp.dot(oa_ref[...], wo_ref[0:half, :], preferred_element_type=F32)
             + jnp.dot(ob_ref[...], wo_ref[half:, :], preferred_element_type=F32) + bo_ref[...])
    x1 = x_ref[...] + gt1 * mixed
    x1_ref[...] = x1
    h2 = _rms(x1) * g_ref[...] * (1.0 + sc2) + sh2
    dn = (((1,), (1,)), ((), ()))
    h_hi = h2.astype(BF16)
    h2_ref[...] = _pack_bf16_pairs(h2)
    h_lo = (h2 - h_hi.astype(F32)).astype(BF16)
    w = wr_ref[...]
    w_hi = w.astype(BF16)
    w_lo = (w - w_hi.astype(F32)).astype(BF16)
    logits = (lax.dot_general(w_hi, h_hi, dn, preferred_element_type=F32)
              + lax.dot_general(w_hi, h_lo, dn, preferred_element_type=F32)
              + lax.dot_general(w_lo, h_hi, dn, preferred_element_type=F32)
              + br_ref[...])

    eidx = lax.broadcasted_iota(jnp.int32, logits.shape, 0)
    vals = logits
    onehots, top_vals, top_idx = [], [], []
    for _k in range(TOP_K):
        mx = jnp.max(vals, axis=0, keepdims=True)
        sel = jnp.min(jnp.where(vals == mx, eidx, N_EXPERTS), axis=0, keepdims=True)
        oh = eidx == sel
        onehots.append(oh)
        top_vals.append(mx)
        top_idx.append(sel)
        vals = jnp.where(oh, NEG_INF, vals)
    exps = [jnp.exp(v - top_vals[0]) for v in top_vals]
    denom = exps[0] + exps[1] + exps[2] + exps[3]
    gate_ref[...] = jnp.concatenate([e / denom for e in exps], axis=0)
    idx_ref[...] = jnp.concatenate(top_idx, axis=0)

    member = (onehots[0] | onehots[1] | onehots[2] | onehots[3])
    member_f = member.astype(F32)
    t_src = lax.broadcasted_iota(jnp.int32, (tm, tm), 0)
    t_dst = lax.broadcasted_iota(jnp.int32, (tm, tm), 1)
    before = (t_src < t_dst).astype(BF16)
    prefix = jnp.dot(member.astype(BF16), before, preferred_element_type=F32) + carry_ref[...]
    ranks = [jnp.sum(jnp.where(oh, prefix, 0.0), axis=0, keepdims=True) for oh in onehots]
    rank_ref[...] = jnp.concatenate(ranks, axis=0).astype(jnp.int32)
    carry_ref[...] = carry_ref[...] + jnp.sum(member_f, axis=1, keepdims=True)
    cnt_ref[...] = jnp.broadcast_to(carry_ref[...], cnt_ref.shape)


def _outproj(out_a, out_b, x2, mod3, w_out, b_out, g_ffn, wr_t, br_col, S):
    N, D = x2.shape
    tm = TM_OUT
    tiles_per_seq = S // tm
    row = lambda i: (i, 0)
    colb = lambda i: (0, i)
    const = lambda i: (0, 0)
    return pl.pallas_call(
        _outproj_kernel,
        out_shape=[jax.ShapeDtypeStruct((N, D), F32), jax.ShapeDtypeStruct((N, D // 2), jnp.int32),
                   jax.ShapeDtypeStruct((TOP_K, N), jnp.int32), jax.ShapeDtypeStruct((TOP_K, N), F32),
                   jax.ShapeDtypeStruct((TOP_K, N), jnp.int32), jax.ShapeDtypeStruct((N_EXPERTS, LANES), F32)],
        grid=(N // tm,),
        in_specs=[pl.BlockSpec((tm, out_a.shape[1]), row),
                  pl.BlockSpec((tm, out_b.shape[1]), row),
                  pl.BlockSpec((tm, D), row),
                  pl.BlockSpec((1, N_MOD, D), lambda i: (i // tiles_per_seq, 0, 0)),
                  pl.BlockSpec(w_out.shape, const),
                  pl.BlockSpec((1, D), const),
                  pl.BlockSpec((1, D), const),
                  pl.BlockSpec(wr_t.shape, const),
                  pl.BlockSpec((N_EXPERTS, 1), const)],
        out_specs=[pl.BlockSpec((tm, D), row), pl.BlockSpec((tm, D // 2), row),
                   pl.BlockSpec((TOP_K, tm), colb), pl.BlockSpec((TOP_K, tm), colb),
                   pl.BlockSpec((TOP_K, tm), colb), pl.BlockSpec((N_EXPERTS, LANES), const)],
        scratch_shapes=[pltpu.VMEM((N_EXPERTS, 1), F32)],
        compiler_params=_cparams(("arbitrary",)),
        name="outproj_router",
    )(out_a, out_b, x2, mod3, w_out, b_out, g_ffn, wr_t, br_col)


def _experts_kernel(first_ref, nblk_ref, x_ref, w1_ref, b1g_ref, b1l_ref, w2_ref, b2_ref, y_ref,
                    w1g_s, w1l_s, w2_s, xbuf, ybuf, xsem, ysem):
    e = pl.program_id(0)
    first = first_ref[e]
    n_blk = nblk_ref[e]
    bm = xbuf.shape[1]

    def x_copy(j, slot):
        rows = pl.ds(pl.multiple_of((first + j) * bm, bm), bm)
        return pltpu.make_async_copy(x_ref.at[rows], xbuf.at[slot], xsem.at[slot])

    def y_copy(j, slot):
        rows = pl.ds(pl.multiple_of((first + j) * bm, bm), bm)
        return pltpu.make_async_copy(ybuf.at[slot], y_ref.at[rows], ysem.at[slot])

    @pl.when(n_blk > 0)
    def _():
        x_copy(0, 0).start(priority=1)
        ch = 256
        half = ch // 2
        for c in range(w1_ref.shape[2] // ch):
            t = w1_ref[0, :, c * ch:(c + 1) * ch].astype(BF16).T
            pairs = pltpu.bitcast(t, jnp.int32)
            cols = slice(c * half, (c + 1) * half)
            w1g_s[:, cols] = lax.bitcast_convert_type(lax.shift_left(pairs, 16), F32).astype(BF16).T
            w1l_s[:, cols] = lax.bitcast_convert_type(pairs & jnp.int32(HIGH_HALF), F32).astype(BF16).T
        w2_s[...] = w2_ref[0].astype(BF16)

    def block(j, carry):
        slot = j & 1
        x_copy(j, slot).wait()

        @pl.when(j + 1 < n_blk)
        def _():
            x_copy(j + 1, 1 - slot).start(priority=1)

        @pl.when(j >= 2)
        def _():
            y_copy(j - 2, slot).wait()

        x = _unpack_bf16_pairs(xbuf[slot]).astype(BF16)
        ug = jnp.dot(x, w1g_s[...], preferred_element_type=F32) + b1g_ref[0]
        ul = jnp.dot(x, w1l_s[...], preferred_element_type=F32) + b1l_ref[0]
        glu = jnp.minimum(ug, SWIGLU_LIMIT)
        lin = jnp.clip(ul, -SWIGLU_LIMIT, SWIGLU_LIMIT)
        act = glu * jax.nn.sigmoid(SWIGLU_ALPHA * glu) * (lin + 1.0)
        y = jnp.dot(act.astype(BF16), w2_s[...], preferred_element_type=F32) + b2_ref[0]
        ybuf[slot] = _pack_bf16_pairs(y)
        y_copy(j, slot).start(priority=1)
        return carry

    lax.fori_loop(0, n_blk, block, 0)

    @pl.when(n_blk >= 2)
    def _():
        y_copy(n_blk - 2, n_blk & 1).wait()

    @pl.when(n_blk >= 1)
    def _():
        y_copy(n_blk - 1, (n_blk - 1) & 1).wait()


def _experts(first_blk, n_blk, x_rows, w1, b1g, b1l, w2, b2):
    E, Fh, D = w2.shape
    bm = BLOCK_ROWS
    n_rows = x_rows.shape[0]
    wsel = lambda e, fb, nb: (e, 0, 0)
    return pl.pallas_call(
        _experts_kernel,
        out_shape=jax.ShapeDtypeStruct((n_rows, D // 2), jnp.int32),
        grid_spec=pltpu.PrefetchScalarGridSpec(
            num_scalar_prefetch=2,
            grid=(E,),
            in_specs=[pl.BlockSpec(memory_space=pl.ANY),
                      pl.BlockSpec((1, D, 2 * Fh), wsel),
                      pl.BlockSpec((1, 1, Fh), wsel),
                      pl.BlockSpec((1, 1, Fh), wsel),
                      pl.BlockSpec((1, Fh, D), wsel),
                      pl.BlockSpec((1, 1, D), wsel)],
            out_specs=pl.BlockSpec(memory_space=pl.ANY),
            scratch_shapes=[pltpu.VMEM((D, Fh), BF16), pltpu.VMEM((D, Fh), BF16), pltpu.VMEM((Fh, D), BF16),
                            pltpu.VMEM((2, bm, D // 2), jnp.int32), pltpu.VMEM((2, bm, D // 2), jnp.int32),
                            pltpu.SemaphoreType.DMA((2,)), pltpu.SemaphoreType.DMA((2,))]),
        compiler_params=_cparams(("arbitrary",), 56 * 1024 * 1024),
        name="experts",
    )(first_blk, n_blk, x_rows, w1, b1g, b1l, w2, b2)


SC_CORES = 2
SC_SUBCORES = 16
SC_CHUNK = 64


def _sc_gather_rows(table, idx):
    M = idx.shape[0]
    D = table.shape[1]
    workers = SC_CORES * SC_SUBCORES
    per_worker = M // workers
    n_chunks = per_worker // SC_CHUNK
    assert M % workers == 0 and per_worker % (2 * SC_CHUNK) == 0
    mesh = plsc.VectorSubcoreMesh(core_axis_name="c", subcore_axis_name="s")

    @functools.partial(
        pl.kernel, mesh=mesh,
        out_type=jax.ShapeDtypeStruct((M, D), table.dtype),
        scratch_types=[pltpu.VMEM((per_worker,), jnp.int32),
                       pltpu.VMEM((SC_CHUNK, D), table.dtype), pltpu.VMEM((SC_CHUNK, D), table.dtype),
                       pltpu.SemaphoreType.DMA, pltpu.SemaphoreType.DMA],
        name="sc_gather_rows")
    def gather(table_hbm, idx_hbm, out_hbm, idx_v, rows0, rows1, sem0, sem1):
        wid = lax.axis_index("s") * SC_CORES + lax.axis_index("c")
        base = wid * per_worker
        pltpu.sync_copy(idx_hbm.at[pl.ds(base, per_worker)], idx_v)

        def fetch(c, buf, sem):
            off = pl.multiple_of(c * SC_CHUNK, SC_CHUNK)
            return pltpu.make_async_copy(table_hbm.at[idx_v.at[pl.ds(off, SC_CHUNK)]], buf, sem)

        def flush(c, buf):
            off = pl.multiple_of(c * SC_CHUNK, SC_CHUNK)
            pltpu.sync_copy(buf, out_hbm.at[pl.ds(base + off, SC_CHUNK)])

        fetch(0, rows0, sem0).start()

        @pl.loop(0, n_chunks // 2)
        def _(jj):
            c = 2 * jj
            fetch(c + 1, rows1, sem1).start()
            fetch(c, rows0, sem0).wait()
            flush(c, rows0)

            @pl.when(c + 2 < n_chunks)
            def _():
                fetch(c + 2, rows0, sem0).start()

            fetch(c + 1, rows1, sem1).wait()
            flush(c + 1, rows1)

    return gather(table, idx)


SC_SCATTER_CHUNK = 128


def _sc_scatter_rows(rows, dest, n_rows):
    N, W = rows.shape
    workers = SC_CORES * SC_SUBCORES
    per_worker = N // workers
    chunks = per_worker // SC_SCATTER_CHUNK
    assert N % workers == 0 and per_worker % SC_SCATTER_CHUNK == 0
    dest3 = dest.reshape(TOP_K, N // SC_SCATTER_CHUNK, SC_SCATTER_CHUNK)
    mesh = plsc.VectorSubcoreMesh(core_axis_name="c", subcore_axis_name="s")

    @functools.partial(
        pl.kernel, mesh=mesh,
        out_type=jax.ShapeDtypeStruct((n_rows, W), rows.dtype),
        scratch_types=[pltpu.VMEM((TOP_K, chunks, SC_SCATTER_CHUNK), jnp.int32),
                       pltpu.VMEM((SC_SCATTER_CHUNK, W), rows.dtype)],
        name="sc_scatter_rows")
    def scatter(rows_hbm, dest_hbm, out_hbm, idx_v, rows_v):
        wid = lax.axis_index("s") * SC_CORES + lax.axis_index("c")
        for k in range(TOP_K):
            pltpu.sync_copy(dest_hbm.at[k, pl.ds(wid * chunks, chunks)], idx_v.at[k])

        @pl.loop(0, chunks)
        def _(j):
            start = pl.multiple_of(wid * per_worker + j * SC_SCATTER_CHUNK, SC_SCATTER_CHUNK)
            pltpu.sync_copy(rows_hbm.at[pl.ds(start, SC_SCATTER_CHUNK)], rows_v)
            for k in range(TOP_K):
                pltpu.sync_copy(rows_v, out_hbm.at[idx_v.at[k, j]])

    return scatter(rows, dest3)


def _combine_kernel(x1_ref, gate_ref, mod_ref, g_ref, y_ref, o_ref, *, final_norm):
    gate = gate_ref[...]
    moe = gate[:, 0:1] * _unpack_bf16_pairs(y_ref[0])
    for k in range(1, TOP_K):
        moe = moe + gate[:, k:k + 1] * _unpack_bf16_pairs(y_ref[k])
    gt2 = mod_ref[0, 5:6, :]
    x2 = x1_ref[...] + gt2 * moe
    o_ref[...] = _rms(x2) * g_ref[...] if final_norm else x2


def _combine(x1, gate_t, mod3, g_final, y_tok, S, final_norm):
    N, D = x1.shape
    tm = TM_ROWS
    tiles_per_seq = S // tm
    row = lambda i: (i, 0)
    return pl.pallas_call(
        functools.partial(_combine_kernel, final_norm=final_norm),
        out_shape=jax.ShapeDtypeStruct((N, D), F32),
        grid=(N // tm,),
        in_specs=[pl.BlockSpec((tm, D), row),
                  pl.BlockSpec((tm, TOP_K), row),
                  pl.BlockSpec((1, N_MOD, D), lambda i: (i // tiles_per_seq, 0, 0)),
                  pl.BlockSpec((1, D), lambda i: (0, 0)),
                  pl.BlockSpec((TOP_K, tm, D // 2), lambda i: (0, i, 0))],
        out_specs=pl.BlockSpec((tm, D), row),
        compiler_params=_cparams(("arbitrary",)),
        name="combine",
    )(x1, gate_t, mod3, g_final, y_tok)


def _routing_tables(counts, idx, rank, n_rows):
    bm = BLOCK_ROWS
    counts = counts.astype(jnp.int32)
    padded = (counts + bm - 1) // bm * bm
    pends = jnp.cumsum(padded)
    pstarts = pends - padded
    experts = jnp.arange(N_EXPERTS, dtype=jnp.int32)
    dest = jnp.sum(jnp.where(idx[..., None] == experts, pstarts, 0), axis=-1) + rank
    return dest.astype(jnp.int32), (pstarts // bm).astype(jnp.int32), (padded // bm).astype(jnp.int32)


def _extended_in_weights(w_in, b_in):
    a_q = SWA_Q_HEADS * HEAD_DIM
    a_kv = SWA_KV_HEADS * HEAD_DIM
    b_w = DIFF_HEADS * DIFF_V_DIM
    spans = [(0, a_q)]
    for base in (a_q, a_q + a_kv):
        for j in range(SWA_KV_HEADS):
            spans += [(base + j * HEAD_DIM, base + (j + 1) * HEAD_DIM)] * 2
    spans.append((a_q + 2 * a_kv, a_q + 2 * a_kv + 3 * b_w))
    w_ext = jnp.concatenate([w_in[:, lo:hi] for lo, hi in spans], axis=1).astype(BF16)
    b_ext = jnp.concatenate([b_in[lo:hi] for lo, hi in spans]).reshape(1, -1)
    widths = (a_q, 2 * a_kv, 2 * a_kv, b_w, b_w, b_w)
    return w_ext, b_ext, widths


def kernel(x, c, positions, w_ada, b_ada, g_mix, w_in, b_in, attn_sinks, lambda_q1, lambda_k1, lambda_q2,
           lambda_k2, g_subln, w_out, b_out, g_ffn, w_router, b_router, w1, b1, w2, b2, g_final):
    B, S, D = x.shape
    N = B * S
    depth = w_ada.shape[0]
    assert TM_PROJ == TQ_DIFF, "the input projection writes q/v transposed per attention tile"
    n_rows = (N * TOP_K + N_EXPERTS * (BLOCK_ROWS - 1) + BLOCK_ROWS - 1) // BLOCK_ROWS * BLOCK_ROWS

    inv = 1.0 / (ROPE_THETA ** (jnp.arange(0, HEAD_DIM, 2, dtype=F32) / HEAD_DIM))
    inv_lane = jnp.tile(inv, LANES // (HEAD_DIM // 2)).reshape(1, LANES)
    pos2 = positions.reshape(N, 1)
    xcur = x.reshape(N, D)

    for layer in range(depth):
        last = layer == depth - 1
        lambda_init = 0.8 - 0.6 * math.exp(-0.3 * layer)
        mod3 = _adaln(c, w_ada[layer], b_ada[layer]).reshape(B, N_MOD, D)

        w_ext, b_ext, widths = _extended_in_weights(w_in[layer], b_in[layer])
        qa, ka2, va2, qdt, kd, vdt = _inproj(xcur, pos2, inv_lane, mod3, g_mix[layer].reshape(1, D),
                                             w_ext, b_ext, S, widths)
        out_a = _swa(attn_sinks[layer], qa, ka2, va2, B, S)
        lam_vecs = jnp.stack([lambda_q1[layer], lambda_k1[layer], lambda_q2[layer], lambda_k2[layer]])
        out_b = _diffattn(lam_vecs, g_subln[layer].reshape(1, DIFF_V_DIM), qdt, kd, vdt, B, S, lambda_init)

        x1, h2, idx, gate, rank, counts = _outproj(
            out_a, out_b, xcur, mod3, w_out[layer].astype(BF16), b_out[layer].reshape(1, D),
            g_ffn[layer].reshape(1, D), w_router[layer].T, b_router[layer].reshape(N_EXPERTS, 1), S)

        dest, first_blk, n_blk = _routing_tables(counts[:, 0], idx, rank, n_rows)
        x_rows = _sc_scatter_rows(h2, dest, n_rows)
        y_rows = _experts(first_blk, n_blk, x_rows, w1[layer],
                          b1[layer][:, None, 0::2], b1[layer][:, None, 1::2],
                          w2[layer], b2[layer][:, None, :])
        y_tok = _sc_gather_rows(y_rows, dest.reshape(-1)).reshape(TOP_K, N, D // 2)
        xcur = _combine(x1, gate.T, mod3, g_final.reshape(1, D), y_tok, S, final_norm=last)
    return xcur.reshape(B, S, D)
```

```python
import functools
import math

import jax
import jax.numpy as jnp
from jax import lax
from jax.experimental import pallas as pl
from jax.experimental.pallas import tpu as pltpu
from jax.experimental.pallas import tpu_sc as plsc

HEAD_DIM = 64
SWA_Q_HEADS = 8
SWA_KV_HEADS = 2
SWA_GROUP = SWA_Q_HEADS // SWA_KV_HEADS
WINDOW = 128
DIFF_HEADS = 4
DIFF_V_DIM = 2 * HEAD_DIM
ROPE_THETA = 10000.0
N_EXPERTS = 32
TOP_K = 4
SWIGLU_ALPHA = 1.702
SWIGLU_LIMIT = 7.0
EPS = 1e-5
N_MOD = 6

LANES = 128
LANE_SHIFT = LANES.bit_length() - 1
TOP_K_SHIFT = TOP_K.bit_length() - 1
assert 1 << LANE_SHIFT == LANES and 1 << TOP_K_SHIFT == TOP_K
F32 = jnp.float32
BF16 = jnp.bfloat16
NEG_INF = float("-inf")

TM_PROJ = 1024
TQ_SWA = 512
VT_ROWS = DIFF_V_DIM + 16
TM_OUT = 512
TM_ROWS = 256
BLOCK_ROWS = 512
VMEM_LIMIT = 48 * 1024 * 1024


def _cparams(sem, vmem=VMEM_LIMIT):
    return pltpu.CompilerParams(dimension_semantics=sem, vmem_limit_bytes=vmem)


def _adaln_kernel(ct_ref, w_ref, b_ref, o_ref):
    c = ct_ref[...]
    cond = c * jax.nn.sigmoid(c)
    w = w_ref[...]
    rows = [jnp.sum(w * cond[:, b:b + 1], axis=0, keepdims=True) for b in range(c.shape[1])]
    o_ref[...] = jnp.concatenate(rows, axis=0) + b_ref[...]


def _adaln(c, w_ada, b_ada):
    B, D = c.shape
    n_out = w_ada.shape[1]
    tn = 1024
    return pl.pallas_call(
        _adaln_kernel,
        out_shape=jax.ShapeDtypeStruct((B, n_out), F32),
        grid=(n_out // tn,),
        in_specs=[pl.BlockSpec((D, B), lambda j: (0, 0)),
                  pl.BlockSpec((D, tn), lambda j: (0, j)),
                  pl.BlockSpec((1, tn), lambda j: (0, j))],
        out_specs=pl.BlockSpec((B, tn), lambda j: (0, j)),
        compiler_params=_cparams(("arbitrary",)),
        name="adaln",
    )(c.T, w_ada, b_ada.reshape(1, n_out))


def _rms(x):
    return x * lax.rsqrt(jnp.mean(x * x, axis=-1, keepdims=True) + EPS)


HIGH_HALF = -65536


def _pack_bf16_pairs(v):
    bits = lax.bitcast_convert_type(v.astype(BF16).astype(F32), jnp.int32)
    half = v.shape[1] // 2
    return lax.shift_right_logical(bits[:, :half], 16) | (bits[:, half:] & jnp.int32(HIGH_HALF))


def _unpack_bf16_pairs(w):
    return jnp.concatenate([lax.bitcast_convert_type(lax.shift_left(w, 16), F32),
                            lax.bitcast_convert_type(w & jnp.int32(HIGH_HALF), F32)], axis=1)


def _inproj_kernel(x_ref, pos_ref, inv_ref, mod_ref, g_ref, w_ref, b_ref,
                   qa_ref, ka_ref, va_ref, qd_ref, kd_ref, vd_ref):
    x = x_ref[...]
    sh = mod_ref[0, 0:1, :]
    sc = mod_ref[0, 1:2, :]
    h = _rms(x) * g_ref[...] * (1.0 + sc) + sh
    proj = jnp.dot(h.astype(BF16), w_ref[...], preferred_element_type=F32) + b_ref[...]

    lane = lax.broadcasted_iota(jnp.int32, (1, LANES), 1)
    first_half = (lane & (HEAD_DIM - 1)) < (HEAD_DIM // 2)
    n_freq = HEAD_DIM // 2
    groups = LANES // n_freq
    tm = x.shape[0]
    rows = tm // groups
    group = lax.shift_right_logical(lane, n_freq.bit_length() - 1)
    pos = pos_ref[...].astype(F32)
    pos_q = pos[0:rows]
    for g in range(1, groups):
        pos_q = jnp.where(group == g, pos[g * rows:(g + 1) * rows], pos_q)
    ang_q = pos_q * inv_ref[...]

    def spread(table_q):
        blocks = []
        for g in range(groups):
            only = jnp.where(group == g, table_q, 0.0)
            full = only
            for r in range(1, groups):
                full = full + pltpu.roll(only, r * n_freq, axis=1)
            blocks.append(full)
        return jnp.concatenate(blocks, axis=0)

    cos = spread(jnp.cos(ang_q))
    sin = spread(jnp.sin(ang_q))
    sin_signed = jnp.where(first_half, -sin, sin)

    def rope(t):
        partner = jnp.where(first_half,
                            pltpu.roll(t, LANES - HEAD_DIM // 2, axis=1),
                            pltpu.roll(t, HEAD_DIM // 2, axis=1))
        return t * cos + partner * sin_signed

    def emit(out_ref, col0, width, rotary, scale, transposed):
        for j in range(width // LANES):
            t = proj[:, col0 + j * LANES: col0 + (j + 1) * LANES]
            if rotary:
                t = rope(t)
            if scale != 1.0:
                t = t * scale
            if transposed:
                rows = out_ref.shape[2] // (width // LANES)
                out_ref[0, 0, j * rows:j * rows + LANES, :] = t.T.astype(out_ref.dtype)
                if rows > LANES:
                    fill = lax.broadcasted_iota(jnp.int32, (rows - LANES, t.shape[0]), 0) == 0
                    out_ref[0, 0, j * rows + LANES:(j + 1) * rows, :] = fill.astype(out_ref.dtype)
            else:
                out_ref[:, j * LANES:(j + 1) * LANES] = t.astype(out_ref.dtype)

    swa_scale = 1.0 / math.sqrt(HEAD_DIM)
    diff_scale = math.log2(math.e) / math.sqrt(HEAD_DIM)
    col = 0
    for out_ref, width, rotary, scale, transposed in (
            (qa_ref, qa_ref.shape[1], True, swa_scale, False), (ka_ref, ka_ref.shape[1], True, 1.0, False),
            (va_ref, va_ref.shape[1], False, 1.0, False), (qd_ref, qd_ref.shape[2], True, diff_scale, True),
            (kd_ref, kd_ref.shape[1], True, 1.0, False),
            (vd_ref, vd_ref.shape[2] // VT_ROWS * LANES, False, 1.0, True)):
        emit(out_ref, col, width, rotary, scale, transposed)
        col += width


def _inproj(x2, pos2, inv_lane, mod3, g_mix, w_ext, b_ext, S, widths):
    N, D = x2.shape
    tm = TM_PROJ
    C = w_ext.shape[1]
    tiles_per_seq = S // tm
    row = lambda i: (i, 0)
    t_rows = (0, 0, 0, widths[3], 0, widths[5] // LANES * VT_ROWS)
    out_shape, out_specs = [], []
    for w, tr in zip(widths, t_rows):
        if tr:
            out_shape.append(jax.ShapeDtypeStruct((N // S, tiles_per_seq, tr, tm), BF16))
            out_specs.append(pl.BlockSpec((1, 1, tr, tm), lambda i: (i // tiles_per_seq, i % tiles_per_seq, 0, 0)))
        else:
            out_shape.append(jax.ShapeDtypeStruct((N, w), BF16))
            out_specs.append(pl.BlockSpec((tm, w), row))
    return pl.pallas_call(
        _inproj_kernel,
        out_shape=out_shape,
        grid=(N // tm,),
        in_specs=[pl.BlockSpec((tm, D), row),
                  pl.BlockSpec((tm, 1), row),
                  pl.BlockSpec((1, LANES), lambda i: (0, 0)),
                  pl.BlockSpec((1, N_MOD, D), lambda i: (i // tiles_per_seq, 0, 0)),
                  pl.BlockSpec((1, D), lambda i: (0, 0)),
                  pl.BlockSpec((D, C), lambda i: (0, 0)),
                  pl.BlockSpec((1, C), lambda i: (0, 0))],
        out_specs=out_specs,
        compiler_params=_cparams(("arbitrary",)),
        name="inproj",
    )(x2, pos2, inv_lane, mod3, g_mix, w_ext, b_ext)


def _swa_kernel(sink_ref, q_ref, kc_ref, kp_ref, vc_ref, vp_ref, o_ref):
    i = pl.program_id(1)
    tq = q_ref.shape[0]
    lane = lax.broadcasted_iota(jnp.int32, (1, LANES), 1)
    lo = lane < HEAD_DIM
    qi = lax.broadcasted_iota(jnp.int32, (WINDOW, 2 * WINDOW), 0) + WINDOW
    kj = lax.broadcasted_iota(jnp.int32, (WINDOW, 2 * WINDOW), 1)
    band = (qi - kj >= 0) & (qi - kj < WINDOW)
    dn = (((1,), (1,)), ((), ()))
    for c in range(tq // WINDOW):
        if c == 0:
            kcat = jnp.concatenate([kp_ref[...], kc_ref[0:WINDOW, :]], axis=0)
            vcat = jnp.concatenate([vp_ref[...], vc_ref[0:WINDOW, :]], axis=0)
            mask = band & (kj >= jnp.where(i > 0, 0, WINDOW))
        else:
            kcat = kc_ref[(c - 1) * WINDOW:(c + 1) * WINDOW, :]
            vcat = vc_ref[(c - 1) * WINDOW:(c + 1) * WINDOW, :]
            mask = band
        for j in range(SWA_KV_HEADS):
            kj2 = kcat[:, j * LANES:(j + 1) * LANES]
            vj2 = vcat[:, j * LANES:(j + 1) * LANES]
            zero = jnp.zeros_like(kj2)
            k_halves = (jnp.where(lo, kj2, zero), jnp.where(lo, zero, kj2))
            v_halves = (jnp.where(lo, vj2, zero), jnp.where(lo, zero, vj2))
            for p in range(SWA_GROUP // 2):
                g = j * (SWA_GROUP // 2) + p
                q = q_ref[c * WINDOW:(c + 1) * WINDOW, g * LANES:(g + 1) * LANES]
                out = jnp.zeros((WINDOW, LANES), F32)
                for half in range(2):
                    sink = sink_ref[2 * g + half]
                    s = lax.dot_general(q, k_halves[half], dn, preferred_element_type=F32)
                    s = jnp.where(mask, s, NEG_INF)
                    m = jnp.maximum(jnp.max(s, axis=1, keepdims=True), sink)
                    e = jnp.exp(s - m)
                    denom = jnp.sum(e, axis=1, keepdims=True) + jnp.exp(sink - m)
                    pv = jnp.dot(e.astype(BF16), v_halves[half], preferred_element_type=F32)
                    out = out + pv / denom
                o_ref[c * WINDOW:(c + 1) * WINDOW, g * LANES:(g + 1) * LANES] = out.astype(o_ref.dtype)


def _swa(sinks, qa, ka2, va2, B, S):
    N = qa.shape[0]
    tq = TQ_SWA
    nq = S // tq
    wpt = tq // WINDOW
    wps = S // WINDOW
    cur = lambda b, i: (b * nq + i, 0)
    prev = lambda b, i: (b * wps + jnp.maximum(i * wpt - 1, 0), 0)
    return pl.pallas_call(
        _swa_kernel,
        out_shape=jax.ShapeDtypeStruct((N, qa.shape[1]), BF16),
        grid=(B, nq),
        in_specs=[pl.BlockSpec(memory_space=pltpu.SMEM),
                  pl.BlockSpec((tq, qa.shape[1]), cur),
                  pl.BlockSpec((tq, ka2.shape[1]), cur),
                  pl.BlockSpec((WINDOW, ka2.shape[1]), prev),
                  pl.BlockSpec((tq, va2.shape[1]), cur),
                  pl.BlockSpec((WINDOW, va2.shape[1]), prev)],
        out_specs=pl.BlockSpec((tq, qa.shape[1]), cur),
        compiler_params=_cparams(("arbitrary", "arbitrary")),
        name="swa",
    )(sinks, qa, ka2, ka2, va2, va2)


def _diff_kernel(lam_ref, g_ref, qt_ref, k_ref, vt_ref, o_ref, sa_ref, sb_ref, m_ref, acc_ref, *, lambda_init):
    i = pl.program_id(2)
    tq = qt_ref.shape[3]
    tk = vt_ref.shape[3]
    qt = qt_ref[0, 0]
    lane = lax.broadcasted_iota(jnp.int32, (1, LANES), 1)
    lo = lane < HEAD_DIM
    m_ref[...] = jnp.full(m_ref.shape, NEG_INF, F32)
    acc_ref[...] = jnp.zeros(acc_ref.shape, F32)

    def scores(c, s_ref):
        k = k_ref[pl.ds(pl.multiple_of(c * tk, tk), tk), :]
        zero = jnp.zeros_like(k)
        s_ref[0] = jnp.dot(jnp.where(lo, k, zero), qt, preferred_element_type=F32)
        s_ref[1] = jnp.dot(jnp.where(lo, zero, k), qt, preferred_element_type=F32)

    def consume(c, s_ref, first_key):
        vt = vt_ref[0, c]
        for mp in range(2):
            s = s_ref[mp]
            if first_key is not None:
                kpos = lax.broadcasted_iota(jnp.int32, (tk, tq), 0) + first_key
                qpos = lax.broadcasted_iota(jnp.int32, (tk, tq), 1)
                s = jnp.where(kpos <= qpos, s, NEG_INF)
            m_prev = m_ref[mp]
            m_new = jnp.maximum(m_prev, jnp.max(s, axis=0, keepdims=True))
            alpha = jnp.exp2(m_prev - m_new)
            p = jnp.exp2(s - m_new).astype(BF16)
            acc_ref[mp] = alpha * acc_ref[mp] + jnp.dot(vt, p, preferred_element_type=F32)
            m_ref[mp] = m_new

    scores(0, sa_ref)

    def pair(jj, carry):
        c = 2 * jj
        scores(c + 1, sb_ref)
        consume(c, sa_ref, None)
        scores(c + 2, sa_ref)
        consume(c + 1, sb_ref, None)
        return carry

    lax.fori_loop(0, lax.shift_right_logical(i, 1), pair, 0)

    @pl.when(i % 2 == 0)
    def _():
        consume(i, sa_ref, 0)

    @pl.when(i % 2 == 1)
    def _():
        scores(i, sb_ref)
        consume(i - 1, sa_ref, None)
        consume(i, sb_ref, 0)

    lq1, lk1, lq2, lk2 = (lam_ref[r:r + 1, :] for r in range(4))
    lam = (jnp.exp(jnp.sum(lq1 * lk1, axis=1, keepdims=True))
           - jnp.exp(jnp.sum(lq2 * lk2, axis=1, keepdims=True)) + lambda_init)
    d = DIFF_V_DIM
    ot = (acc_ref[0, 0:d, :] / acc_ref[0, d:d + 1, :]
          - lam * (acc_ref[1, 0:d, :] / acc_ref[1, d:d + 1, :]))
    ot = ot * lax.rsqrt(jnp.mean(ot * ot, axis=0, keepdims=True) + EPS)
    o_ref[...] = (ot.T * g_ref[...] * (1.0 - lambda_init)).astype(o_ref.dtype)


def _diffattn(lam_vecs, g_subln, qdt, kd, vdt, B, S, lambda_init):
    N, C = kd.shape
    tq = qdt.shape[3]
    nq = S // tq
    return pl.pallas_call(
        functools.partial(_diff_kernel, lambda_init=lambda_init),
        out_shape=jax.ShapeDtypeStruct((N, C), BF16),
        grid=(B, DIFF_HEADS, nq),
        in_specs=[pl.BlockSpec((4, HEAD_DIM), lambda b, h, i: (0, 0)),
                  pl.BlockSpec((1, DIFF_V_DIM), lambda b, h, i: (0, 0)),
                  pl.BlockSpec((1, 1, LANES, tq), lambda b, h, i: (b, i, h, 0)),
                  pl.BlockSpec((S, LANES), lambda b, h, i: (b, h)),
                  pl.BlockSpec((1, nq, VT_ROWS, tq), lambda b, h, i: (b, 0, h, 0))],
        out_specs=pl.BlockSpec((tq, LANES), lambda b, h, i: (b * nq + i, h)),
        scratch_shapes=[pltpu.VMEM((2, tq, tq), F32), pltpu.VMEM((2, tq, tq), F32),
                        pltpu.VMEM((2, 1, tq), F32), pltpu.VMEM((2, VT_ROWS, tq), F32)],
        compiler_params=_cparams(("arbitrary", "arbitrary", "arbitrary")),
        name="diffattn",
    )(lam_vecs, g_subln, qdt, kd, vdt)


def _outproj_kernel(oa_ref, ob_ref, x_ref, mod_ref, wo_ref, bo_ref, g_ref, wr_ref, br_ref,
                    x1_ref, h2_ref, idx_ref, gate_ref, rank_ref, cnt_ref, carry_ref):
    i = pl.program_id(0)
    tm = x_ref.shape[0]
    half = oa_ref.shape[1]

    @pl.when(i == 0)
    def _():
        carry_ref[...] = jnp.zeros(carry_ref.shape, F32)

    gt1 = mod_ref[0, 2:3, :]
    sh2 = mod_ref[0, 3:4, :]
    sc2 = mod_ref[0, 4:5, :]
    mixed = (jnp.dot(oa_ref[...], wo_ref[0:half, :], preferred_element_type=F32)
             + jnp.dot(ob_ref[...], wo_ref[half:, :], preferred_element_type=F32) + bo_ref[...])
    x1 = x_ref[...] + gt1 * mixed
    x1_ref[...] = x1
    h2 = _rms(x1) * g_ref[...] * (1.0 + sc2) + sh2
    dn = (((1,), (1,)), ((), ()))
    h_hi = h2.astype(BF16)
    h2_ref[...] = _pack_bf16_pairs(h2)
    h_lo = (h2 - h_hi.astype(F32)).astype(BF16)
    w = wr_ref[...]
    w_hi = w.astype(BF16)
    w_lo = (w - w_hi.astype(F32)).astype(BF16)
    logits = (lax.dot_general(w_hi, h_hi, dn, preferred_element_type=F32)
              + lax.dot_general(w_hi, h_lo, dn, preferred_element_type=F32)
              + lax.dot_general(w_lo, h_hi, dn, preferred_element_type=F32)
              + br_ref[...])

    eidx = lax.broadcasted_iota(jnp.int32, logits.shape, 0)
    vals = logits
    onehots, top_vals, top_idx = [], [], []
    for _k in range(TOP_K):
        mx = jnp.max(vals, axis=0, keepdims=True)
        sel = jnp.min(jnp.where(vals == mx, eidx, N_EXPERTS), axis=0, keepdims=True)
        oh = eidx == sel
        onehots.append(oh)
        top_vals.append(mx)
        top_idx.append(sel)
        vals = jnp.where(oh, NEG_INF, vals)
    exps = [jnp.exp(v - top_vals[0]) for v in top_vals]
    denom = exps[0] + exps[1] + exps[2] + exps[3]
    gate_ref[...] = jnp.concatenate([e / denom for e in exps], axis=0)
    idx_ref[...] = jnp.concatenate(top_idx, axis=0)

    member = (onehots[0] | onehots[1] | onehots[2] | onehots[3])
    member_f = member.astype(F32)
    t_src = lax.broadcasted_iota(jnp.int32, (tm, tm), 0)
    t_dst = lax.broadcasted_iota(jnp.int32, (tm, tm), 1)
    before = (t_src < t_dst).astype(BF16)
    prefix = jnp.dot(member.astype(BF16), before, preferred_element_type=F32) + carry_ref[...]
    ranks = [jnp.sum(jnp.where(oh, prefix, 0.0), axis=0, keepdims=True) for oh in onehots]
    rank_ref[...] = jnp.concatenate(ranks, axis=0).astype(jnp.int32)
    carry_ref[...] = carry_ref[...] + jnp.sum(member_f, axis=1, keepdims=True)
    cnt_ref[...] = jnp.broadcast_to(carry_ref[...], cnt_ref.shape)


def _outproj(out_a, out_b, x2, mod3, w_out, b_out, g_ffn, wr_t, br_col, S):
    N, D = x2.shape
    tm = TM_OUT
    tiles_per_seq = S // tm
    row = lambda i: (i, 0)
    colb = lambda i: (0, i)
    const = lambda i: (0, 0)
    return pl.pallas_call(
        _outproj_kernel,
        out_shape=[jax.ShapeDtypeStruct((N, D), F32), jax.ShapeDtypeStruct((N, D // 2), jnp.int32),
                   jax.ShapeDtypeStruct((TOP_K, N), jnp.int32), jax.ShapeDtypeStruct((TOP_K, N), F32),
                   jax.ShapeDtypeStruct((TOP_K, N), jnp.int32), jax.ShapeDtypeStruct((N_EXPERTS, LANES), F32)],
        grid=(N // tm,),
        in_specs=[pl.BlockSpec((tm, out_a.shape[1]), row),
                  pl.BlockSpec((tm, out_b.shape[1]), row),
                  pl.BlockSpec((tm, D), row),
                  pl.BlockSpec((1, N_MOD, D), lambda i: (i // tiles_per_seq, 0, 0)),
                  pl.BlockSpec(w_out.shape, const),
                  pl.BlockSpec((1, D), const),
                  pl.BlockSpec((1, D), const),
                  pl.BlockSpec(wr_t.shape, const),
                  pl.BlockSpec((N_EXPERTS, 1), const)],
        out_specs=[pl.BlockSpec((tm, D), row), pl.BlockSpec((tm, D // 2), row),
                   pl.BlockSpec((TOP_K, tm), colb), pl.BlockSpec((TOP_K, tm), colb),
                   pl.BlockSpec((TOP_K, tm), colb), pl.BlockSpec((N_EXPERTS, LANES), const)],
        scratch_shapes=[pltpu.VMEM((N_EXPERTS, 1), F32)],
        compiler_params=_cparams(("arbitrary",)),
        name="outproj_router",
    )(out_a, out_b, x2, mod3, w_out, b_out, g_ffn, wr_t, br_col)


def _experts_kernel(first_ref, nblk_ref, x_ref, w1_ref, b1g_ref, b1l_ref, w2_ref, b2_ref, y_ref,
                    w1g_s, w1l_s, w2_s, xbuf, ybuf, xsem, ysem):
    e = pl.program_id(0)
    first = first_ref[e]
    n_blk = nblk_ref[e]
    bm = xbuf.shape[1]

    def x_copy(j, slot):
        rows = pl.ds(pl.multiple_of((first + j) * bm, bm), bm)
        return pltpu.make_async_copy(x_ref.at[rows], xbuf.at[slot], xsem.at[slot])

    def y_copy(j, slot):
        rows = pl.ds(pl.multiple_of((first + j) * bm, bm), bm)
        return pltpu.make_async_copy(ybuf.at[slot], y_ref.at[rows], ysem.at[slot])

    def swiglu(ug, ul):
        glu = jnp.minimum(ug, SWIGLU_LIMIT)
        lin = jnp.clip(ul, -SWIGLU_LIMIT, SWIGLU_LIMIT)
        return (glu * jax.nn.sigmoid(SWIGLU_ALPHA * glu) * (lin + 1.0)).astype(BF16)

    @pl.when(n_blk > 0)
    def _():
        x_copy(0, 0).start(priority=1)
        ch = 512
        half = ch // 2
        groups = w1_ref.shape[2] // ch

        def relayout(c):
            t = w1_ref[0, :, c * ch:(c + 1) * ch].astype(BF16).T
            pairs = pltpu.bitcast(t, jnp.int32)
            g = lax.bitcast_convert_type(lax.shift_left(pairs, 16), F32).astype(BF16).T
            l = lax.bitcast_convert_type(pairs & jnp.int32(HIGH_HALF), F32).astype(BF16).T
            return g, l, w2_ref[0, c * half:(c + 1) * half, :].astype(BF16)

        parts = relayout(0)
        x_copy(0, 0).wait()

        @pl.when(n_blk > 1)
        def _():
            x_copy(1, 1).start(priority=1)

        x = _unpack_bf16_pairs(xbuf[0]).astype(BF16)
        y = b2_ref[0]
        for c in range(groups):
            g, l, w2c = parts if c == 0 else relayout(c)
            cols = slice(c * half, (c + 1) * half)
            w1g_s[:, cols] = g
            w1l_s[:, cols] = l
            w2_s[cols, :] = w2c
            ug = jnp.dot(x, g, preferred_element_type=F32) + b1g_ref[0][:, cols]
            ul = jnp.dot(x, l, preferred_element_type=F32) + b1l_ref[0][:, cols]
            y = y + jnp.dot(swiglu(ug, ul), w2c, preferred_element_type=F32)
        ybuf[0] = _pack_bf16_pairs(y)
        y_copy(0, 0).start(priority=1)

    def block(j, carry):
        slot = j & 1
        x_copy(j, slot).wait()

        @pl.when(j + 1 < n_blk)
        def _():
            x_copy(j + 1, 1 - slot).start(priority=1)

        @pl.when(j >= 2)
        def _():
            y_copy(j - 2, slot).wait()

        x = _unpack_bf16_pairs(xbuf[slot]).astype(BF16)
        ug = jnp.dot(x, w1g_s[...], preferred_element_type=F32) + b1g_ref[0]
        ul = jnp.dot(x, w1l_s[...], preferred_element_type=F32) + b1l_ref[0]
        y = jnp.dot(swiglu(ug, ul), w2_s[...], preferred_element_type=F32) + b2_ref[0]
        ybuf[slot] = _pack_bf16_pairs(y)
        y_copy(j, slot).start(priority=1)
        return carry

    lax.fori_loop(1, n_blk, block, 0)

    @pl.when(n_blk >= 2)
    def _():
        y_copy(n_blk - 2, n_blk & 1).wait()

    @pl.when(n_blk >= 1)
    def _():
        y_copy(n_blk - 1, (n_blk - 1) & 1).wait()


def _experts(first_blk, n_blk, x_rows, w1, b1g, b1l, w2, b2):
    E, Fh, D = w2.shape
    bm = BLOCK_ROWS
    n_rows = x_rows.shape[0]
    wsel = lambda e, fb, nb: (e, 0, 0)
    return pl.pallas_call(
        _experts_kernel,
        out_shape=jax.ShapeDtypeStruct((n_rows, D // 2), jnp.int32),
        grid_spec=pltpu.PrefetchScalarGridSpec(
            num_scalar_prefetch=2,
            grid=(E,),
            in_specs=[pl.BlockSpec(memory_space=pl.ANY),
                      pl.BlockSpec((1, D, 2 * Fh), wsel),
                      pl.BlockSpec((1, 1, Fh), wsel),
                      pl.BlockSpec((1, 1, Fh), wsel),
                      pl.BlockSpec((1, Fh, D), wsel),
                      pl.BlockSpec((1, 1, D), wsel)],
            out_specs=pl.BlockSpec(memory_space=pl.ANY),
            scratch_shapes=[pltpu.VMEM((D, Fh), BF16), pltpu.VMEM((D, Fh), BF16), pltpu.VMEM((Fh, D), BF16),
                            pltpu.VMEM((2, bm, D // 2), jnp.int32), pltpu.VMEM((2, bm, D // 2), jnp.int32),
                            pltpu.SemaphoreType.DMA((2,)), pltpu.SemaphoreType.DMA((2,))]),
        compiler_params=_cparams(("arbitrary",), 56 * 1024 * 1024),
        name="experts",
    )(first_blk, n_blk, x_rows, w1, b1g, b1l, w2, b2)


SC_CORES = 2
SC_SUBCORES = 16
SC_CHUNK = 64


def _sc_gather_rows(table, idx):
    M = idx.shape[0]
    D = table.shape[1]
    workers = SC_CORES * SC_SUBCORES
    per_worker = M // workers
    n_chunks = per_worker // SC_CHUNK
    assert M % workers == 0 and per_worker % (2 * SC_CHUNK) == 0
    mesh = plsc.VectorSubcoreMesh(core_axis_name="c", subcore_axis_name="s")

    @functools.partial(
        pl.kernel, mesh=mesh,
        out_type=jax.ShapeDtypeStruct((M, D), table.dtype),
        scratch_types=[pltpu.VMEM((per_worker,), jnp.int32),
                       pltpu.VMEM((SC_CHUNK, D), table.dtype), pltpu.VMEM((SC_CHUNK, D), table.dtype),
                       pltpu.SemaphoreType.DMA, pltpu.SemaphoreType.DMA],
        name="sc_gather_rows")
    def gather(table_hbm, idx_hbm, out_hbm, idx_v, rows0, rows1, sem0, sem1):
        wid = lax.axis_index("s") * SC_CORES + lax.axis_index("c")
        base = wid * per_worker
        pltpu.sync_copy(idx_hbm.at[pl.ds(base, per_worker)], idx_v)

        def fetch(c, buf, sem):
            off = pl.multiple_of(c * SC_CHUNK, SC_CHUNK)
            return pltpu.make_async_copy(table_hbm.at[idx_v.at[pl.ds(off, SC_CHUNK)]], buf, sem)

        def flush(c, buf):
            off = pl.multiple_of(c * SC_CHUNK, SC_CHUNK)
            pltpu.sync_copy(buf, out_hbm.at[pl.ds(base + off, SC_CHUNK)])

        fetch(0, rows0, sem0).start()

        @pl.loop(0, n_chunks // 2)
        def _(jj):
            c = 2 * jj
            fetch(c + 1, rows1, sem1).start()
            fetch(c, rows0, sem0).wait()
            flush(c, rows0)

            @pl.when(c + 2 < n_chunks)
            def _():
                fetch(c + 2, rows0, sem0).start()

            fetch(c + 1, rows1, sem1).wait()
            flush(c + 1, rows1)

    return gather(table, idx)


SC_SCATTER_CHUNK = 128


def _sc_scatter_rows(rows, dest, n_rows):
    N, W = rows.shape
    workers = SC_CORES * SC_SUBCORES
    per_worker = N // workers
    chunks = per_worker // SC_SCATTER_CHUNK
    assert N % workers == 0 and per_worker % SC_SCATTER_CHUNK == 0
    dest3 = dest.reshape(TOP_K, N // SC_SCATTER_CHUNK, SC_SCATTER_CHUNK)
    mesh = plsc.VectorSubcoreMesh(core_axis_name="c", subcore_axis_name="s")

    @functools.partial(
        pl.kernel, mesh=mesh,
        out_type=jax.ShapeDtypeStruct((n_rows, W), rows.dtype),
        scratch_types=[pltpu.VMEM((TOP_K, chunks, SC_SCATTER_CHUNK), jnp.int32),
                       pltpu.VMEM((SC_SCATTER_CHUNK, W), rows.dtype)],
        name="sc_scatter_rows")
    def scatter(rows_hbm, dest_hbm, out_hbm, idx_v, rows_v):
        wid = lax.axis_index("s") * SC_CORES + lax.axis_index("c")
        for k in range(TOP_K):
            pltpu.sync_copy(dest_hbm.at[k, pl.ds(wid * chunks, chunks)], idx_v.at[k])

        @pl.loop(0, chunks)
        def _(j):
            start = pl.multiple_of(wid * per_worker + j * SC_SCATTER_CHUNK, SC_SCATTER_CHUNK)
            pltpu.sync_copy(rows_hbm.at[pl.ds(start, SC_SCATTER_CHUNK)], rows_v)
            for k in range(TOP_K):
                pltpu.sync_copy(rows_v, out_hbm.at[idx_v.at[k, j]])

    return scatter(rows, dest3)


def _combine_kernel(x1_ref, gate_ref, mod_ref, g_ref, y_ref, o_ref, *, final_norm):
    gate = gate_ref[...]
    moe = gate[:, 0:1] * _unpack_bf16_pairs(y_ref[0])
    for k in range(1, TOP_K):
        moe = moe + gate[:, k:k + 1] * _unpack_bf16_pairs(y_ref[k])
    gt2 = mod_ref[0, 5:6, :]
    x2 = x1_ref[...] + gt2 * moe
    o_ref[...] = _rms(x2) * g_ref[...] if final_norm else x2


def _combine(x1, gate_t, mod3, g_final, y_tok, S, final_norm):
    N, D = x1.shape
    tm = TM_ROWS
    tiles_per_seq = S // tm
    row = lambda i: (i, 0)
    return pl.pallas_call(
        functools.partial(_combine_kernel, final_norm=final_norm),
        out_shape=jax.ShapeDtypeStruct((N, D), F32),
        grid=(N // tm,),
        in_specs=[pl.BlockSpec((tm, D), row),
                  pl.BlockSpec((tm, TOP_K), row),
                  pl.BlockSpec((1, N_MOD, D), lambda i: (i // tiles_per_seq, 0, 0)),
                  pl.BlockSpec((1, D), lambda i: (0, 0)),
                  pl.BlockSpec((TOP_K, tm, D // 2), lambda i: (0, i, 0))],
        out_specs=pl.BlockSpec((tm, D), row),
        compiler_params=_cparams(("arbitrary",)),
        name="combine",
    )(x1, gate_t, mod3, g_final, y_tok)


def _routing_tables(counts, idx, rank, n_rows):
    bm = BLOCK_ROWS
    counts = counts.astype(jnp.int32)
    padded = (counts + bm - 1) // bm * bm
    pends = jnp.cumsum(padded)
    pstarts = pends - padded
    experts = jnp.arange(N_EXPERTS, dtype=jnp.int32)
    dest = jnp.sum(jnp.where(idx[..., None] == experts, pstarts, 0), axis=-1) + rank
    return dest.astype(jnp.int32), (pstarts // bm).astype(jnp.int32), (padded // bm).astype(jnp.int32)


def _extended_in_weights(w_in, b_in):
    a_q = SWA_Q_HEADS * HEAD_DIM
    a_kv = SWA_KV_HEADS * HEAD_DIM
    b_w = DIFF_HEADS * DIFF_V_DIM
    spans = [(0, a_q)]
    for base in (a_q, a_q + a_kv):
        for j in range(SWA_KV_HEADS):
            spans += [(base + j * HEAD_DIM, base + (j + 1) * HEAD_DIM)] * 2
    spans.append((a_q + 2 * a_kv, a_q + 2 * a_kv + 3 * b_w))
    w_ext = jnp.concatenate([w_in[:, lo:hi] for lo, hi in spans], axis=1).astype(BF16)
    b_ext = jnp.concatenate([b_in[lo:hi] for lo, hi in spans]).reshape(1, -1)
    widths = (a_q, 2 * a_kv, 2 * a_kv, b_w, b_w, b_w)
    return w_ext, b_ext, widths


def kernel(x, c, positions, w_ada, b_ada, g_mix, w_in, b_in, attn_sinks, lambda_q1, lambda_k1, lambda_q2,
           lambda_k2, g_subln, w_out, b_out, g_ffn, w_router, b_router, w1, b1, w2, b2, g_final):
    B, S, D = x.shape
    N = B * S
    depth = w_ada.shape[0]
    n_rows = (N * TOP_K + N_EXPERTS * (BLOCK_ROWS - 1) + BLOCK_ROWS - 1) // BLOCK_ROWS * BLOCK_ROWS

    inv = 1.0 / (ROPE_THETA ** (jnp.arange(0, HEAD_DIM, 2, dtype=F32) / HEAD_DIM))
    inv_lane = jnp.tile(inv, LANES // (HEAD_DIM // 2)).reshape(1, LANES)
    pos2 = positions.reshape(N, 1)
    xcur = x.reshape(N, D)

    for layer in range(depth):
        last = layer == depth - 1
        lambda_init = 0.8 - 0.6 * math.exp(-0.3 * layer)
        mod3 = _adaln(c, w_ada[layer], b_ada[layer]).reshape(B, N_MOD, D)

        w_ext, b_ext, widths = _extended_in_weights(w_in[layer], b_in[layer])
        qa, ka2, va2, qdt, kd, vdt = _inproj(xcur, pos2, inv_lane, mod3, g_mix[layer].reshape(1, D),
                                             w_ext, b_ext, S, widths)
        out_a = _swa(attn_sinks[layer], qa, ka2, va2, B, S)
        lam_vecs = jnp.stack([lambda_q1[layer], lambda_k1[layer], lambda_q2[layer], lambda_k2[layer]])
        out_b = _diffattn(lam_vecs, g_subln[layer].reshape(1, DIFF_V_DIM), qdt, kd, vdt, B, S, lambda_init)

        x1, h2, idx, gate, rank, counts = _outproj(
            out_a, out_b, xcur, mod3, w_out[layer].astype(BF16), b_out[layer].reshape(1, D),
            g_ffn[layer].reshape(1, D), w_router[layer].T, b_router[layer].reshape(N_EXPERTS, 1), S)

        dest, first_blk, n_blk = _routing_tables(counts[:, 0], idx, rank, n_rows)
        x_rows = _sc_scatter_rows(h2, dest, n_rows)
        y_rows = _experts(first_blk, n_blk, x_rows, w1[layer],
                          b1[layer][:, None, 0::2], b1[layer][:, None, 1::2],
                          w2[layer], b2[layer][:, None, :])
        y_tok = _sc_gather_rows(y_rows, dest.reshape(-1)).reshape(TOP_K, N, D // 2)
        xcur = _combine(x1, gate.T, mod3, g_final.reshape(1, D), y_tok, S, final_norm=last)
    return xcur.reshape(B, S, D)
```

```python
import functools
import math

import jax
import jax.numpy as jnp
from jax import lax
from jax.experimental import pallas as pl
from jax.experimental.pallas import tpu as pltpu
from jax.experimental.pallas import tpu_sc as plsc

HEAD_DIM = 64
SWA_Q_HEADS = 8
SWA_KV_HEADS = 2
SWA_GROUP = SWA_Q_HEADS // SWA_KV_HEADS
WINDOW = 128
DIFF_HEADS = 4
DIFF_V_DIM = 2 * HEAD_DIM
ROPE_THETA = 10000.0
N_EXPERTS = 32
TOP_K = 4
SWIGLU_ALPHA = 1.702
SWIGLU_LIMIT = 7.0
EPS = 1e-5
N_MOD = 6

LANES = 128
F32 = jnp.float32
BF16 = jnp.bfloat16
NEG_INF = float("-inf")

TM_PROJ = 1024
TQ_SWA = 512
VT_ROWS = DIFF_V_DIM + 16
TM_OUT = 512
TM_ROWS = 256
BLOCK_ROWS = 512
VMEM_LIMIT = 48 * 1024 * 1024


def _cparams(sem, vmem=VMEM_LIMIT):
    return pltpu.CompilerParams(dimension_semantics=sem, vmem_limit_bytes=vmem)


def _adaln_kernel(ct_ref, w_ref, b_ref, o_ref):
    c = ct_ref[...]
    cond = c * jax.nn.sigmoid(c)
    w = w_ref[...]
    rows = [jnp.sum(w * cond[:, b:b + 1], axis=0, keepdims=True) for b in range(c.shape[1])]
    o_ref[...] = jnp.concatenate(rows, axis=0) + b_ref[...]


def _adaln(c, w_ada, b_ada):
    B, D = c.shape
    n_out = w_ada.shape[1]
    tn = 1024
    return pl.pallas_call(
        _adaln_kernel,
        out_shape=jax.ShapeDtypeStruct((B, n_out), F32),
        grid=(n_out // tn,),
        in_specs=[pl.BlockSpec((D, B), lambda j: (0, 0)),
                  pl.BlockSpec((D, tn), lambda j: (0, j)),
                  pl.BlockSpec((1, tn), lambda j: (0, j))],
        out_specs=pl.BlockSpec((B, tn), lambda j: (0, j)),
        compiler_params=_cparams(("arbitrary",)),
        name="adaln",
    )(c.T, w_ada, b_ada.reshape(1, n_out))


def _rms(x):
    return x * lax.rsqrt(jnp.mean(x * x, axis=-1, keepdims=True) + EPS)


HIGH_HALF = -65536


def _pack_bf16_pairs(v):
    bits = lax.bitcast_convert_type(v.astype(BF16).astype(F32), jnp.int32)
    half = v.shape[1] // 2
    return lax.shift_right_logical(bits[:, :half], 16) | (bits[:, half:] & jnp.int32(HIGH_HALF))


def _unpack_bf16_pairs(w):
    return jnp.concatenate([lax.bitcast_convert_type(lax.shift_left(w, 16), F32),
                            lax.bitcast_convert_type(w & jnp.int32(HIGH_HALF), F32)], axis=1)


def _inproj_kernel(x_ref, pos_ref, inv_ref, mod_ref, g_ref, w_ref, b_ref,
                   qa_ref, ka_ref, va_ref, qd_ref, kd_ref, vd_ref):
    x = x_ref[...]
    sh = mod_ref[0, 0:1, :]
    sc = mod_ref[0, 1:2, :]
    h = _rms(x) * g_ref[...] * (1.0 + sc) + sh
    proj = jnp.dot(h.astype(BF16), w_ref[...], preferred_element_type=F32) + b_ref[...]

    lane = lax.broadcasted_iota(jnp.int32, (1, LANES), 1)
    first_half = (lane & (HEAD_DIM - 1)) < (HEAD_DIM // 2)
    n_freq = HEAD_DIM // 2
    groups = LANES // n_freq
    tm = x.shape[0]
    rows = tm // groups
    group = lax.shift_right_logical(lane, n_freq.bit_length() - 1)
    pos = pos_ref[...].astype(F32)
    pos_q = pos[0:rows]
    for g in range(1, groups):
        pos_q = jnp.where(group == g, pos[g * rows:(g + 1) * rows], pos_q)
    ang_q = pos_q * inv_ref[...]

    def spread(table_q):
        blocks = []
        for g in range(groups):
            only = jnp.where(group == g, table_q, 0.0)
            full = only
            for r in range(1, groups):
                full = full + pltpu.roll(only, r * n_freq, axis=1)
            blocks.append(full)
        return jnp.concatenate(blocks, axis=0)

    cos = spread(jnp.cos(ang_q))
    sin = spread(jnp.sin(ang_q))
    sin_signed = jnp.where(first_half, -sin, sin)

    def rope(t):
        partner = jnp.where(first_half,
                            pltpu.roll(t, LANES - HEAD_DIM // 2, axis=1),
                            pltpu.roll(t, HEAD_DIM // 2, axis=1))
        return t * cos + partner * sin_signed

    def emit(out_ref, col0, width, rotary, scale, transposed):
        for j in range(width // LANES):
            t = proj[:, col0 + j * LANES: col0 + (j + 1) * LANES]
            if rotary:
                t = rope(t)
            if scale != 1.0:
                t = t * scale
            if transposed:
                rows = out_ref.shape[2] // (width // LANES)
                out_ref[0, 0, j * rows:j * rows + LANES, :] = t.T.astype(out_ref.dtype)
                if rows > LANES:
                    fill = lax.broadcasted_iota(jnp.int32, (rows - LANES, t.shape[0]), 0) == 0
                    out_ref[0, 0, j * rows + LANES:(j + 1) * rows, :] = fill.astype(out_ref.dtype)
            else:
                out_ref[:, j * LANES:(j + 1) * LANES] = t.astype(out_ref.dtype)

    swa_scale = 1.0 / math.sqrt(HEAD_DIM)
    diff_scale = math.log2(math.e) / math.sqrt(HEAD_DIM)
    col = 0
    for out_ref, width, rotary, scale, transposed in (
            (qa_ref, qa_ref.shape[1], True, swa_scale, False), (ka_ref, ka_ref.shape[1], True, 1.0, False),
            (va_ref, va_ref.shape[1], False, 1.0, False), (qd_ref, qd_ref.shape[2], True, diff_scale, True),
            (kd_ref, kd_ref.shape[1], True, 1.0, False),
            (vd_ref, vd_ref.shape[2] // VT_ROWS * LANES, False, 1.0, True)):
        emit(out_ref, col, width, rotary, scale, transposed)
        col += width


def _inproj(x2, pos2, inv_lane, mod3, g_mix, w_ext, b_ext, S, widths):
    N, D = x2.shape
    tm = TM_PROJ
    C = w_ext.shape[1]
    tiles_per_seq = S // tm
    row = lambda i: (i, 0)
    t_rows = (0, 0, 0, widths[3], 0, widths[5] // LANES * VT_ROWS)
    out_shape, out_specs = [], []
    for w, tr in zip(widths, t_rows):
        if tr:
            out_shape.append(jax.ShapeDtypeStruct((N // S, tiles_per_seq, tr, tm), BF16))
            out_specs.append(pl.BlockSpec((1, 1, tr, tm), lambda i: (i // tiles_per_seq, i % tiles_per_seq, 0, 0)))
        else:
            out_shape.append(jax.ShapeDtypeStruct((N, w), BF16))
            out_specs.append(pl.BlockSpec((tm, w), row))
    return pl.pallas_call(
        _inproj_kernel,
        out_shape=out_shape,
        grid=(N // tm,),
        in_specs=[pl.BlockSpec((tm, D), row),
                  pl.BlockSpec((tm, 1), row),
                  pl.BlockSpec((1, LANES), lambda i: (0, 0)),
                  pl.BlockSpec((1, N_MOD, D), lambda i: (i // tiles_per_seq, 0, 0)),
                  pl.BlockSpec((1, D), lambda i: (0, 0)),
                  pl.BlockSpec((D, C), lambda i: (0, 0)),
                  pl.BlockSpec((1, C), lambda i: (0, 0))],
        out_specs=out_specs,
        compiler_params=_cparams(("arbitrary",)),
        name="inproj",
    )(x2, pos2, inv_lane, mod3, g_mix, w_ext, b_ext)


def _swa_kernel(sink_ref, q_ref, kc_ref, kp_ref, vc_ref, vp_ref, o_ref):
    i = pl.program_id(1)
    tq = q_ref.shape[0]
    lane = lax.broadcasted_iota(jnp.int32, (1, LANES), 1)
    lo = lane < HEAD_DIM
    qi = lax.broadcasted_iota(jnp.int32, (WINDOW, 2 * WINDOW), 0) + WINDOW
    kj = lax.broadcasted_iota(jnp.int32, (WINDOW, 2 * WINDOW), 1)
    band = (qi - kj >= 0) & (qi - kj < WINDOW)
    dn = (((1,), (1,)), ((), ()))
    for c in range(tq // WINDOW):
        if c == 0:
            kcat = jnp.concatenate([kp_ref[...], kc_ref[0:WINDOW, :]], axis=0)
            vcat = jnp.concatenate([vp_ref[...], vc_ref[0:WINDOW, :]], axis=0)
            mask = band & (kj >= jnp.where(i > 0, 0, WINDOW))
        else:
            kcat = kc_ref[(c - 1) * WINDOW:(c + 1) * WINDOW, :]
            vcat = vc_ref[(c - 1) * WINDOW:(c + 1) * WINDOW, :]
            mask = band
        for j in range(SWA_KV_HEADS):
            kj2 = kcat[:, j * LANES:(j + 1) * LANES]
            vj2 = vcat[:, j * LANES:(j + 1) * LANES]
            zero = jnp.zeros_like(kj2)
            k_halves = (jnp.where(lo, kj2, zero), jnp.where(lo, zero, kj2))
            v_halves = (jnp.where(lo, vj2, zero), jnp.where(lo, zero, vj2))
            for p in range(SWA_GROUP // 2):
                g = j * (SWA_GROUP // 2) + p
                q = q_ref[c * WINDOW:(c + 1) * WINDOW, g * LANES:(g + 1) * LANES]
                out = jnp.zeros((WINDOW, LANES), F32)
                for half in range(2):
                    sink = sink_ref[2 * g + half]
                    s = lax.dot_general(q, k_halves[half], dn, preferred_element_type=F32)
                    s = jnp.where(mask, s, NEG_INF)
                    m = jnp.maximum(jnp.max(s, axis=1, keepdims=True), sink)
                    e = jnp.exp(s - m)
                    denom = jnp.sum(e, axis=1, keepdims=True) + jnp.exp(sink - m)
                    pv = jnp.dot(e.astype(BF16), v_halves[half], preferred_element_type=F32)
                    out = out + pv / denom
                o_ref[c * WINDOW:(c + 1) * WINDOW, g * LANES:(g + 1) * LANES] = out.astype(o_ref.dtype)


def _swa(sinks, qa, ka2, va2, B, S):
    N = qa.shape[0]
    tq = TQ_SWA
    nq = S // tq
    wpt = tq // WINDOW
    wps = S // WINDOW
    cur = lambda b, i: (b * nq + i, 0)
    prev = lambda b, i: (b * wps + jnp.maximum(i * wpt - 1, 0), 0)
    return pl.pallas_call(
        _swa_kernel,
        out_shape=jax.ShapeDtypeStruct((N, qa.shape[1]), BF16),
        grid=(B, nq),
        in_specs=[pl.BlockSpec(memory_space=pltpu.SMEM),
                  pl.BlockSpec((tq, qa.shape[1]), cur),
                  pl.BlockSpec((tq, ka2.shape[1]), cur),
                  pl.BlockSpec((WINDOW, ka2.shape[1]), prev),
                  pl.BlockSpec((tq, va2.shape[1]), cur),
                  pl.BlockSpec((WINDOW, va2.shape[1]), prev)],
        out_specs=pl.BlockSpec((tq, qa.shape[1]), cur),
        compiler_params=_cparams(("arbitrary", "arbitrary")),
        name="swa",
    )(sinks, qa, ka2, ka2, va2, va2)


def _diff_kernel(lam_ref, g_ref, qt_ref, k_ref, vt_ref, o_ref, sa_ref, sb_ref, m_ref, acc_ref, *, lambda_init):
    i = pl.program_id(2)
    tq = qt_ref.shape[3]
    tk = vt_ref.shape[3]
    qt = qt_ref[0, 0]
    lane = lax.broadcasted_iota(jnp.int32, (1, LANES), 1)
    lo = lane < HEAD_DIM
    m_ref[...] = jnp.full(m_ref.shape, NEG_INF, F32)
    acc_ref[...] = jnp.zeros(acc_ref.shape, F32)

    def scores(c, s_ref):
        k = k_ref[pl.ds(pl.multiple_of(c * tk, tk), tk), :]
        zero = jnp.zeros_like(k)
        s_ref[0] = jnp.dot(jnp.where(lo, k, zero), qt, preferred_element_type=F32)
        s_ref[1] = jnp.dot(jnp.where(lo, zero, k), qt, preferred_element_type=F32)

    def consume(c, s_ref, first_key):
        vt = vt_ref[0, c]
        for mp in range(2):
            s = s_ref[mp]
            if first_key is not None:
                kpos = lax.broadcasted_iota(jnp.int32, (tk, tq), 0) + first_key
                qpos = lax.broadcasted_iota(jnp.int32, (tk, tq), 1)
                s = jnp.where(kpos <= qpos, s, NEG_INF)
            m_prev = m_ref[mp]
            m_new = jnp.maximum(m_prev, jnp.max(s, axis=0, keepdims=True))
            alpha = jnp.exp2(m_prev - m_new)
            p = jnp.exp2(s - m_new).astype(BF16)
            acc_ref[mp] = alpha * acc_ref[mp] + jnp.dot(vt, p, preferred_element_type=F32)
            m_ref[mp] = m_new

    scores(0, sa_ref)

    def pair(jj, carry):
        c = 2 * jj
        scores(c + 1, sb_ref)
        consume(c, sa_ref, None)
        scores(c + 2, sa_ref)
        consume(c + 1, sb_ref, None)
        return carry

    lax.fori_loop(0, lax.shift_right_logical(i, 1), pair, 0)

    @pl.when(i % 2 == 0)
    def _():
        consume(i, sa_ref, 0)

    @pl.when(i % 2 == 1)
    def _():
        scores(i, sb_ref)
        consume(i - 1, sa_ref, None)
        consume(i, sb_ref, 0)

    lq1, lk1, lq2, lk2 = (lam_ref[r:r + 1, :] for r in range(4))
    lam = (jnp.exp(jnp.sum(lq1 * lk1, axis=1, keepdims=True))
           - jnp.exp(jnp.sum(lq2 * lk2, axis=1, keepdims=True)) + lambda_init)
    d = DIFF_V_DIM
    ot = (acc_ref[0, 0:d, :] / acc_ref[0, d:d + 1, :]
          - lam * (acc_ref[1, 0:d, :] / acc_ref[1, d:d + 1, :]))
    ot = ot * lax.rsqrt(jnp.mean(ot * ot, axis=0, keepdims=True) + EPS)
    o_ref[...] = (ot.T * g_ref[...] * (1.0 - lambda_init)).astype(o_ref.dtype)


def _diffattn(lam_vecs, g_subln, qdt, kd, vdt, B, S, lambda_init):
    N, C = kd.shape
    tq = qdt.shape[3]
    nq = S // tq
    return pl.pallas_call(
        functools.partial(_diff_kernel, lambda_init=lambda_init),
        out_shape=jax.ShapeDtypeStruct((N, C), BF16),
        grid=(B, DIFF_HEADS, nq),
        in_specs=[pl.BlockSpec((4, HEAD_DIM), lambda b, h, i: (0, 0)),
                  pl.BlockSpec((1, DIFF_V_DIM), lambda b, h, i: (0, 0)),
                  pl.BlockSpec((1, 1, LANES, tq), lambda b, h, i: (b, i, h, 0)),
                  pl.BlockSpec((S, LANES), lambda b, h, i: (b, h)),
                  pl.BlockSpec((1, nq, VT_ROWS, tq), lambda b, h, i: (b, 0, h, 0))],
        out_specs=pl.BlockSpec((tq, LANES), lambda b, h, i: (b * nq + i, h)),
        scratch_shapes=[pltpu.VMEM((2, tq, tq), F32), pltpu.VMEM((2, tq, tq), F32),
                        pltpu.VMEM((2, 1, tq), F32), pltpu.VMEM((2, VT_ROWS, tq), F32)],
        compiler_params=_cparams(("arbitrary", "arbitrary", "arbitrary")),
        name="diffattn",
    )(lam_vecs, g_subln, qdt, kd, vdt)


def _outproj_kernel(oa_ref, ob_ref, x_ref, mod_ref, wo_ref, bo_ref, g_ref, wr_ref, br_ref,
                    x1_ref, h2_ref, idx_ref, gate_ref, rank_ref, cnt_ref, carry_ref):
    i = pl.program_id(0)
    tm = x_ref.shape[0]
    half = oa_ref.shape[1]

    @pl.when(i == 0)
    def _():
        carry_ref[...] = jnp.zeros(carry_ref.shape, F32)

    gt1 = mod_ref[0, 2:3, :]
    sh2 = mod_ref[0, 3:4, :]
    sc2 = mod_ref[0, 4:5, :]
    mixed = (jnp.dot(oa_ref[...], wo_ref[0:half, :], preferred_element_type=F32)
             + jnp.dot(ob_ref[...], wo_ref[half:, :], preferred_element_type=F32) + bo_ref[...])
    x1 = x_ref[...] + gt1 * mixed
    x1_ref[...] = x1
    h2 = _rms(x1) * g_ref[...] * (1.0 + sc2) + sh2
    dn = (((1,), (1,)), ((), ()))
    h_hi = h2.astype(BF16)
    h2_ref[...] = _pack_bf16_pairs(h2)
    h_lo = (h2 - h_hi.astype(F32)).astype(BF16)
    w = wr_ref[...]
    w_hi = w.astype(BF16)
    w_lo = (w - w_hi.astype(F32)).astype(BF16)
    logits = (lax.dot_general(w_hi, h_hi, dn, preferred_element_type=F32)
              + lax.dot_general(w_hi, h_lo, dn, preferred_element_type=F32)
              + lax.dot_general(w_lo, h_hi, dn, preferred_element_type=F32)
              + br_ref[...])

    eidx = lax.broadcasted_iota(jnp.int32, logits.shape, 0)
    vals = logits
    onehots, top_vals, top_idx = [], [], []
    for _k in range(TOP_K):
        mx = jnp.max(vals, axis=0, keepdims=True)
        sel = jnp.min(jnp.where(vals == mx, eidx, N_EXPERTS), axis=0, keepdims=True)
        oh = eidx == sel
        onehots.append(oh)
        top_vals.append(mx)
        top_idx.append(sel)
        vals = jnp.where(oh, NEG_INF, vals)
    exps = [jnp.exp(v - top_vals[0]) for v in top_vals]
    denom = exps[0] + exps[1] + exps[2] + exps[3]
    gate_ref[...] = jnp.concatenate([e / denom for e in exps], axis=0)
    idx_ref[...] = jnp.concatenate(top_idx, axis=0)

    member = (onehots[0] | onehots[1] | onehots[2] | onehots[3])
    member_f = member.astype(F32)
    t_src = lax.broadcasted_iota(jnp.int32, (tm, tm), 0)
    t_dst = lax.broadcasted_iota(jnp.int32, (tm, tm), 1)
    before = (t_src < t_dst).astype(BF16)
    prefix = jnp.dot(member.astype(BF16), before, preferred_element_type=F32) + carry_ref[...]
    ranks = [jnp.sum(jnp.where(oh, prefix, 0.0), axis=0, keepdims=True) for oh in onehots]
    rank_ref[...] = jnp.concatenate(ranks, axis=0).astype(jnp.int32)
    carry_ref[...] = carry_ref[...] + jnp.sum(member_f, axis=1, keepdims=True)
    cnt_ref[...] = jnp.broadcast_to(carry_ref[...], cnt_ref.shape)


def _outproj(out_a, out_b, x2, mod3, w_out, b_out, g_ffn, wr_t, br_col, S):
    N, D = x2.shape
    tm = TM_OUT
    tiles_per_seq = S // tm
    row = lambda i: (i, 0)
    colb = lambda i: (0, i)
    const = lambda i: (0, 0)
    return pl.pallas_call(
        _outproj_kernel,
        out_shape=[jax.ShapeDtypeStruct((N, D), F32), jax.ShapeDtypeStruct((N, D // 2), jnp.int32),
                   jax.ShapeDtypeStruct((TOP_K, N), jnp.int32), jax.ShapeDtypeStruct((TOP_K, N), F32),
                   jax.ShapeDtypeStruct((TOP_K, N), jnp.int32), jax.ShapeDtypeStruct((N_EXPERTS, LANES), F32)],
        grid=(N // tm,),
        in_specs=[pl.BlockSpec((tm, out_a.shape[1]), row),
                  pl.BlockSpec((tm, out_b.shape[1]), row),
                  pl.BlockSpec((tm, D), row),
                  pl.BlockSpec((1, N_MOD, D), lambda i: (i // tiles_per_seq, 0, 0)),
                  pl.BlockSpec(w_out.shape, const),
                  pl.BlockSpec((1, D), const),
                  pl.BlockSpec((1, D), const),
                  pl.BlockSpec(wr_t.shape, const),
                  pl.BlockSpec((N_EXPERTS, 1), const)],
        out_specs=[pl.BlockSpec((tm, D), row), pl.BlockSpec((tm, D // 2), row),
                   pl.BlockSpec((TOP_K, tm), colb), pl.BlockSpec((TOP_K, tm), colb),
                   pl.BlockSpec((TOP_K, tm), colb), pl.BlockSpec((N_EXPERTS, LANES), const)],
        scratch_shapes=[pltpu.VMEM((N_EXPERTS, 1), F32)],
        compiler_params=_cparams(("arbitrary",)),
        name="outproj_router",
    )(out_a, out_b, x2, mod3, w_out, b_out, g_ffn, wr_t, br_col)


def _experts_kernel(first_ref, nblk_ref, x_ref, w1_ref, b1g_ref, b1l_ref, w2_ref, b2_ref, y_ref,
                    w1g_s, w1l_s, w2_s, xbuf, ybuf, xsem, ysem):
    e = pl.program_id(0)
    first = first_ref[e]
    n_blk = nblk_ref[e]
    bm = xbuf.shape[1]

    def x_copy(j, slot):
        rows = pl.ds(pl.multiple_of((first + j) * bm, bm), bm)
        return pltpu.make_async_copy(x_ref.at[rows], xbuf.at[slot], xsem.at[slot])

    def y_copy(j, slot):
        rows = pl.ds(pl.multiple_of((first + j) * bm, bm), bm)
        return pltpu.make_async_copy(ybuf.at[slot], y_ref.at[rows], ysem.at[slot])

    def swiglu(ug, ul):
        glu = jnp.minimum(ug, SWIGLU_LIMIT)
        lin = jnp.clip(ul, -SWIGLU_LIMIT, SWIGLU_LIMIT)
        return (glu * jax.nn.sigmoid(SWIGLU_ALPHA * glu) * (lin + 1.0)).astype(BF16)

    @pl.when(n_blk > 0)
    def _():
        x_copy(0, 0).start(priority=1)
        ch = 256
        half = ch // 2
        for c in range(w1_ref.shape[2] // ch):
            t = w1_ref[0, :, c * ch:(c + 1) * ch].astype(BF16).T
            pairs = pltpu.bitcast(t, jnp.int32)
            cols = slice(c * half, (c + 1) * half)
            w1g_s[:, cols] = lax.bitcast_convert_type(lax.shift_left(pairs, 16), F32).astype(BF16).T
            w1l_s[:, cols] = lax.bitcast_convert_type(pairs & jnp.int32(HIGH_HALF), F32).astype(BF16).T
        w2_s[...] = w2_ref[0].astype(BF16)

    def block(j, carry):
        slot = j & 1
        x_copy(j, slot).wait()

        @pl.when(j + 1 < n_blk)
        def _():
            x_copy(j + 1, 1 - slot).start(priority=1)

        @pl.when(j >= 2)
        def _():
            y_copy(j - 2, slot).wait()

        x = _unpack_bf16_pairs(xbuf[slot]).astype(BF16)
        ug = jnp.dot(x, w1g_s[...], preferred_element_type=F32) + b1g_ref[0]
        ul = jnp.dot(x, w1l_s[...], preferred_element_type=F32) + b1l_ref[0]
        y = jnp.dot(swiglu(ug, ul), w2_s[...], preferred_element_type=F32) + b2_ref[0]
        ybuf[slot] = _pack_bf16_pairs(y)
        y_copy(j, slot).start(priority=1)
        return carry

    lax.fori_loop(0, n_blk, block, 0)

    @pl.when(n_blk >= 2)
    def _():
        y_copy(n_blk - 2, n_blk & 1).wait()

    @pl.when(n_blk >= 1)
    def _():
        y_copy(n_blk - 1, (n_blk - 1) & 1).wait()


def _experts(first_blk, n_blk, x_rows, w1, b1g, b1l, w2, b2):
    E, Fh, D = w2.shape
    bm = BLOCK_ROWS
    n_rows = x_rows.shape[0]
    wsel = lambda e, fb, nb: (e, 0, 0)
    return pl.pallas_call(
        _experts_kernel,
        out_shape=jax.ShapeDtypeStruct((n_rows, D // 2), jnp.int32),
        grid_spec=pltpu.PrefetchScalarGridSpec(
            num_scalar_prefetch=2,
            grid=(E,),
            in_specs=[pl.BlockSpec(memory_space=pl.ANY),
                      pl.BlockSpec((1, D, 2 * Fh), wsel),
                      pl.BlockSpec((1, 1, Fh), wsel),
                      pl.BlockSpec((1, 1, Fh), wsel),
                      pl.BlockSpec((1, Fh, D), wsel),
                      pl.BlockSpec((1, 1, D), wsel)],
            out_specs=pl.BlockSpec(memory_space=pl.ANY),
            scratch_shapes=[pltpu.VMEM((D, Fh), BF16), pltpu.VMEM((D, Fh), BF16), pltpu.VMEM((Fh, D), BF16),
                            pltpu.VMEM((2, bm, D // 2), jnp.int32), pltpu.VMEM((2, bm, D // 2), jnp.int32),
                            pltpu.SemaphoreType.DMA((2,)), pltpu.SemaphoreType.DMA((2,))]),
        compiler_params=_cparams(("arbitrary",), 56 * 1024 * 1024),
        name="experts",
    )(first_blk, n_blk, x_rows, w1, b1g, b1l, w2, b2)


SC_CORES = 2
SC_SUBCORES = 16
SC_CHUNK = 64


def _sc_gather_rows(table, idx):
    M = idx.shape[0]
    D = table.shape[1]
    workers = SC_CORES * SC_SUBCORES
    per_worker = M // workers
    n_chunks = per_worker // SC_CHUNK
    assert M % workers == 0 and per_worker % (2 * SC_CHUNK) == 0
    mesh = plsc.VectorSubcoreMesh(core_axis_name="c", subcore_axis_name="s")

    @functools.partial(
        pl.kernel, mesh=mesh,
        out_type=jax.ShapeDtypeStruct((M, D), table.dtype),
        scratch_types=[pltpu.VMEM((per_worker,), jnp.int32),
                       pltpu.VMEM((SC_CHUNK, D), table.dtype), pltpu.VMEM((SC_CHUNK, D), table.dtype),
                       pltpu.SemaphoreType.DMA, pltpu.SemaphoreType.DMA],
        name="sc_gather_rows")
    def gather(table_hbm, idx_hbm, out_hbm, idx_v, rows0, rows1, sem0, sem1):
        wid = lax.axis_index("s") * SC_CORES + lax.axis_index("c")
        base = wid * per_worker
        pltpu.sync_copy(idx_hbm.at[pl.ds(base, per_worker)], idx_v)

        def fetch(c, buf, sem):
            off = pl.multiple_of(c * SC_CHUNK, SC_CHUNK)
            return pltpu.make_async_copy(table_hbm.at[idx_v.at[pl.ds(off, SC_CHUNK)]], buf, sem)

        def flush(c, buf):
            off = pl.multiple_of(c * SC_CHUNK, SC_CHUNK)
            pltpu.sync_copy(buf, out_hbm.at[pl.ds(base + off, SC_CHUNK)])

        fetch(0, rows0, sem0).start()

        @pl.loop(0, n_chunks // 2)
        def _(jj):
            c = 2 * jj
            fetch(c + 1, rows1, sem1).start()
            fetch(c, rows0, sem0).wait()
            flush(c, rows0)

            @pl.when(c + 2 < n_chunks)
            def _():
                fetch(c + 2, rows0, sem0).start()

            fetch(c + 1, rows1, sem1).wait()
            flush(c + 1, rows1)

    return gather(table, idx)


SC_SCATTER_CHUNK = 128


def _sc_scatter_rows(rows, dest, n_rows):
    N, W = rows.shape
    workers = SC_CORES * SC_SUBCORES
    per_worker = N // workers
    chunks = per_worker // SC_SCATTER_CHUNK
    assert N % workers == 0 and per_worker % SC_SCATTER_CHUNK == 0
    dest3 = dest.reshape(TOP_K, N // SC_SCATTER_CHUNK, SC_SCATTER_CHUNK)
    mesh = plsc.VectorSubcoreMesh(core_axis_name="c", subcore_axis_name="s")

    @functools.partial(
        pl.kernel, mesh=mesh,
        out_type=jax.ShapeDtypeStruct((n_rows, W), rows.dtype),
        scratch_types=[pltpu.VMEM((TOP_K, chunks, SC_SCATTER_CHUNK), jnp.int32),
                       pltpu.VMEM((SC_SCATTER_CHUNK, W), rows.dtype)],
        name="sc_scatter_rows")
    def scatter(rows_hbm, dest_hbm, out_hbm, idx_v, rows_v):
        wid = lax.axis_index("s") * SC_CORES + lax.axis_index("c")
        for k in range(TOP_K):
            pltpu.sync_copy(dest_hbm.at[k, pl.ds(wid * chunks, chunks)], idx_v.at[k])

        @pl.loop(0, chunks)
        def _(j):
            start = pl.multiple_of(wid * per_worker + j * SC_SCATTER_CHUNK, SC_SCATTER_CHUNK)
            pltpu.sync_copy(rows_hbm.at[pl.ds(start, SC_SCATTER_CHUNK)], rows_v)
            for k in range(TOP_K):
                pltpu.sync_copy(rows_v, out_hbm.at[idx_v.at[k, j]])

    return scatter(rows, dest3)


def _combine_kernel(x1_ref, gate_ref, mod_ref, g_ref, y_ref, o_ref, *, final_norm):
    gate = gate_ref[...]
    moe = gate[:, 0:1] * _unpack_bf16_pairs(y_ref[0])
    for k in range(1, TOP_K):
        moe = moe + gate[:, k:k + 1] * _unpack_bf16_pairs(y_ref[k])
    gt2 = mod_ref[0, 5:6, :]
    x2 = x1_ref[...] + gt2 * moe
    o_ref[...] = _rms(x2) * g_ref[...] if final_norm else x2


def _combine(x1, gate_t, mod3, g_final, y_tok, S, final_norm):
    N, D = x1.shape
    tm = TM_ROWS
    tiles_per_seq = S // tm
    row = lambda i: (i, 0)
    return pl.pallas_call(
        functools.partial(_combine_kernel, final_norm=final_norm),
        out_shape=jax.ShapeDtypeStruct((N, D), F32),
        grid=(N // tm,),
        in_specs=[pl.BlockSpec((tm, D), row),
                  pl.BlockSpec((tm, TOP_K), row),
                  pl.BlockSpec((1, N_MOD, D), lambda i: (i // tiles_per_seq, 0, 0)),
                  pl.BlockSpec((1, D), lambda i: (0, 0)),
                  pl.BlockSpec((TOP_K, tm, D // 2), lambda i: (0, i, 0))],
        out_specs=pl.BlockSpec((tm, D), row),
        compiler_params=_cparams(("arbitrary",)),
        name="combine",
    )(x1, gate_t, mod3, g_final, y_tok)


def _routing_tables(counts, idx, rank):
    bm = BLOCK_ROWS
    counts = counts.astype(jnp.int32)
    padded = (counts + bm - 1) // bm * bm
    pends = jnp.cumsum(padded)
    pstarts = pends - padded
    experts = jnp.arange(N_EXPERTS, dtype=jnp.int32)
    dest = jnp.sum(jnp.where(idx[..., None] == experts, pstarts, 0), axis=-1) + rank
    return dest.astype(jnp.int32), (pstarts // bm).astype(jnp.int32), (padded // bm).astype(jnp.int32)


def _extended_in_weights(w_in, b_in):
    a_q = SWA_Q_HEADS * HEAD_DIM
    a_kv = SWA_KV_HEADS * HEAD_DIM
    b_w = DIFF_HEADS * DIFF_V_DIM
    spans = [(0, a_q)]
    for base in (a_q, a_q + a_kv):
        for j in range(SWA_KV_HEADS):
            spans += [(base + j * HEAD_DIM, base + (j + 1) * HEAD_DIM)] * 2
    spans.append((a_q + 2 * a_kv, a_q + 2 * a_kv + 3 * b_w))
    w_ext = jnp.concatenate([w_in[:, lo:hi] for lo, hi in spans], axis=1).astype(BF16)
    b_ext = jnp.concatenate([b_in[lo:hi] for lo, hi in spans]).reshape(1, -1)
    widths = (a_q, 2 * a_kv, 2 * a_kv, b_w, b_w, b_w)
    return w_ext, b_ext, widths


def kernel(x, c, positions, w_ada, b_ada, g_mix, w_in, b_in, attn_sinks, lambda_q1, lambda_k1, lambda_q2,
           lambda_k2, g_subln, w_out, b_out, g_ffn, w_router, b_router, w1, b1, w2, b2, g_final):
    B, S, D = x.shape
    N = B * S
    depth = w_ada.shape[0]
    n_rows = (N * TOP_K + N_EXPERTS * (BLOCK_ROWS - 1) + BLOCK_ROWS - 1) // BLOCK_ROWS * BLOCK_ROWS

    inv = 1.0 / (ROPE_THETA ** (jnp.arange(0, HEAD_DIM, 2, dtype=F32) / HEAD_DIM))
    inv_lane = jnp.tile(inv, LANES // (HEAD_DIM // 2)).reshape(1, LANES)
    pos2 = positions.reshape(N, 1)
    xcur = x.reshape(N, D)

    for layer in range(depth):
        last = layer == depth - 1
        lambda_init = 0.8 - 0.6 * math.exp(-0.3 * layer)
        mod3 = _adaln(c, w_ada[layer], b_ada[layer]).reshape(B, N_MOD, D)

        w_ext, b_ext, widths = _extended_in_weights(w_in[layer], b_in[layer])
        qa, ka2, va2, qdt, kd, vdt = _inproj(xcur, pos2, inv_lane, mod3, g_mix[layer].reshape(1, D),
                                             w_ext, b_ext, S, widths)
        out_a = _swa(attn_sinks[layer], qa, ka2, va2, B, S)
        lam_vecs = jnp.stack([lambda_q1[layer], lambda_k1[layer], lambda_q2[layer], lambda_k2[layer]])
        out_b = _diffattn(lam_vecs, g_subln[layer].reshape(1, DIFF_V_DIM), qdt, kd, vdt, B, S, lambda_init)

        x1, h2, idx, gate, rank, counts = _outproj(
            out_a, out_b, xcur, mod3, w_out[layer].astype(BF16), b_out[layer].reshape(1, D),
            g_ffn[layer].reshape(1, D), w_router[layer].T, b_router[layer].reshape(N_EXPERTS, 1), S)

        dest, first_blk, n_blk = _routing_tables(counts[:, 0], idx, rank)
        x_rows = _sc_scatter_rows(h2, dest, n_rows)
        y_rows = _experts(first_blk, n_blk, x_rows, w1[layer],
                          b1[layer][:, None, 0::2], b1[layer][:, None, 1::2],
                          w2[layer], b2[layer][:, None, :])
        y_tok = _sc_gather_rows(y_rows, dest.reshape(-1)).reshape(TOP_K, N, D // 2)
        xcur = _combine(x1, gate.T, mod3, g_final.reshape(1, D), y_tok, S, final_norm=last)
    return xcur.reshape(B, S, D)
```

```python
import functools
import math

import jax
import jax.numpy as jnp
from jax import lax
from jax.experimental import pallas as pl
from jax.experimental.pallas import tpu as pltpu
from jax.experimental.pallas import tpu_sc as plsc

HEAD_DIM = 64
SWA_Q_HEADS = 8
SWA_KV_HEADS = 2
SWA_GROUP = SWA_Q_HEADS // SWA_KV_HEADS
WINDOW = 128
DIFF_HEADS = 4
DIFF_V_DIM = 2 * HEAD_DIM
ROPE_THETA = 10000.0
N_EXPERTS = 32
TOP_K = 4
SWIGLU_ALPHA = 1.702
SWIGLU_LIMIT = 7.0
EPS = 1e-5
N_MOD = 6

LANES = 128
F32 = jnp.float32
BF16 = jnp.bfloat16
NEG_INF = float("-inf")

TM_PROJ = 1024
TQ_SWA = 512
VT_ROWS = DIFF_V_DIM + 16
TM_OUT = 512
TM_ROWS = 256
BLOCK_ROWS = 512
VMEM_LIMIT = 48 * 1024 * 1024


def _cparams(sem, vmem=VMEM_LIMIT):
    return pltpu.CompilerParams(dimension_semantics=sem, vmem_limit_bytes=vmem)


def _adaln_kernel(ct_ref, w_ref, b_ref, o_ref):
    c = ct_ref[...]
    cond = c * jax.nn.sigmoid(c)
    w = w_ref[...]
    rows = [jnp.sum(w * cond[:, b:b + 1], axis=0, keepdims=True) for b in range(c.shape[1])]
    o_ref[...] = jnp.concatenate(rows, axis=0) + b_ref[...]


def _adaln(c, w_ada, b_ada):
    B, D = c.shape
    n_out = w_ada.shape[1]
    tn = 1024
    return pl.pallas_call(
        _adaln_kernel,
        out_shape=jax.ShapeDtypeStruct((B, n_out), F32),
        grid=(n_out // tn,),
        in_specs=[pl.BlockSpec((D, B), lambda j: (0, 0)),
                  pl.BlockSpec((D, tn), lambda j: (0, j)),
                  pl.BlockSpec((1, tn), lambda j: (0, j))],
        out_specs=pl.BlockSpec((B, tn), lambda j: (0, j)),
        compiler_params=_cparams(("arbitrary",)),
        name="adaln",
    )(c.T, w_ada, b_ada.reshape(1, n_out))


def _rms(x):
    return x * lax.rsqrt(jnp.mean(x * x, axis=-1, keepdims=True) + EPS)


HIGH_HALF = -65536


def _pack_bf16_pairs(v):
    bits = lax.bitcast_convert_type(v.astype(BF16).astype(F32), jnp.int32)
    half = v.shape[1] // 2
    return lax.shift_right_logical(bits[:, :half], 16) | (bits[:, half:] & jnp.int32(HIGH_HALF))


def _unpack_bf16_pairs(w):
    return jnp.concatenate([lax.bitcast_convert_type(lax.shift_left(w, 16), F32),
                            lax.bitcast_convert_type(w & jnp.int32(HIGH_HALF), F32)], axis=1)


def _inproj_kernel(x_ref, pos_ref, inv_ref, mod_ref, g_ref, w_ref, b_ref,
                   qa_ref, ka_ref, va_ref, qd_ref, kd_ref, vd_ref):
    x = x_ref[...]
    sh = mod_ref[0, 0:1, :]
    sc = mod_ref[0, 1:2, :]
    h = _rms(x) * g_ref[...] * (1.0 + sc) + sh
    proj = jnp.dot(h.astype(BF16), w_ref[...], preferred_element_type=F32) + b_ref[...]

    lane = lax.broadcasted_iota(jnp.int32, (1, LANES), 1)
    first_half = (lane & (HEAD_DIM - 1)) < (HEAD_DIM // 2)
    n_freq = HEAD_DIM // 2
    groups = LANES // n_freq
    tm = x.shape[0]
    rows = tm // groups
    group = lax.shift_right_logical(lane, n_freq.bit_length() - 1)
    pos = pos_ref[...].astype(F32)
    pos_q = pos[0:rows]
    for g in range(1, groups):
        pos_q = jnp.where(group == g, pos[g * rows:(g + 1) * rows], pos_q)
    ang_q = pos_q * inv_ref[...]

    def spread(table_q):
        blocks = []
        for g in range(groups):
            only = jnp.where(group == g, table_q, 0.0)
            full = only
            for r in range(1, groups):
                full = full + pltpu.roll(only, r * n_freq, axis=1)
            blocks.append(full)
        return jnp.concatenate(blocks, axis=0)

    cos = spread(jnp.cos(ang_q))
    sin = spread(jnp.sin(ang_q))
    sin_signed = jnp.where(first_half, -sin, sin)

    def rope(t):
        partner = jnp.where(first_half,
                            pltpu.roll(t, LANES - HEAD_DIM // 2, axis=1),
                            pltpu.roll(t, HEAD_DIM // 2, axis=1))
        return t * cos + partner * sin_signed

    def emit(out_ref, col0, width, rotary, scale, transposed):
        for j in range(width // LANES):
            t = proj[:, col0 + j * LANES: col0 + (j + 1) * LANES]
            if rotary:
                t = rope(t)
            if scale != 1.0:
                t = t * scale
            if transposed:
                rows = out_ref.shape[2] // (width // LANES)
                out_ref[0, 0, j * rows:j * rows + LANES, :] = t.T.astype(out_ref.dtype)
                if rows > LANES:
                    fill = lax.broadcasted_iota(jnp.int32, (rows - LANES, t.shape[0]), 0) == 0
                    out_ref[0, 0, j * rows + LANES:(j + 1) * rows, :] = fill.astype(out_ref.dtype)
            else:
                out_ref[:, j * LANES:(j + 1) * LANES] = t.astype(out_ref.dtype)

    swa_scale = 1.0 / math.sqrt(HEAD_DIM)
    diff_scale = math.log2(math.e) / math.sqrt(HEAD_DIM)
    col = 0
    for out_ref, width, rotary, scale, transposed in (
            (qa_ref, qa_ref.shape[1], True, swa_scale, False), (ka_ref, ka_ref.shape[1], True, 1.0, False),
            (va_ref, va_ref.shape[1], False, 1.0, False), (qd_ref, qd_ref.shape[2], True, diff_scale, True),
            (kd_ref, kd_ref.shape[1], True, 1.0, False),
            (vd_ref, vd_ref.shape[2] // VT_ROWS * LANES, False, 1.0, True)):
        emit(out_ref, col, width, rotary, scale, transposed)
        col += width


def _inproj(x2, pos2, inv_lane, mod3, g_mix, w_ext, b_ext, S, widths):
    N, D = x2.shape
    tm = TM_PROJ
    C = w_ext.shape[1]
    tiles_per_seq = S // tm
    row = lambda i: (i, 0)
    t_rows = (0, 0, 0, widths[3], 0, widths[5] // LANES * VT_ROWS)
    out_shape, out_specs = [], []
    for w, tr in zip(widths, t_rows):
        if tr:
            out_shape.append(jax.ShapeDtypeStruct((N // S, tiles_per_seq, tr, tm), BF16))
            out_specs.append(pl.BlockSpec((1, 1, tr, tm), lambda i: (i // tiles_per_seq, i % tiles_per_seq, 0, 0)))
        else:
            out_shape.append(jax.ShapeDtypeStruct((N, w), BF16))
            out_specs.append(pl.BlockSpec((tm, w), row))
    return pl.pallas_call(
        _inproj_kernel,
        out_shape=out_shape,
        grid=(N // tm,),
        in_specs=[pl.BlockSpec((tm, D), row),
                  pl.BlockSpec((tm, 1), row),
                  pl.BlockSpec((1, LANES), lambda i: (0, 0)),
                  pl.BlockSpec((1, N_MOD, D), lambda i: (i // tiles_per_seq, 0, 0)),
                  pl.BlockSpec((1, D), lambda i: (0, 0)),
                  pl.BlockSpec((D, C), lambda i: (0, 0)),
                  pl.BlockSpec((1, C), lambda i: (0, 0))],
        out_specs=out_specs,
        compiler_params=_cparams(("arbitrary",)),
        name="inproj",
    )(x2, pos2, inv_lane, mod3, g_mix, w_ext, b_ext)


def _swa_kernel(sink_ref, q_ref, kc_ref, kp_ref, vc_ref, vp_ref, o_ref):
    i = pl.program_id(1)
    tq = q_ref.shape[0]
    lane = lax.broadcasted_iota(jnp.int32, (1, LANES), 1)
    lo = lane < HEAD_DIM
    qi = lax.broadcasted_iota(jnp.int32, (WINDOW, 2 * WINDOW), 0) + WINDOW
    kj = lax.broadcasted_iota(jnp.int32, (WINDOW, 2 * WINDOW), 1)
    band = (qi - kj >= 0) & (qi - kj < WINDOW)
    dn = (((1,), (1,)), ((), ()))
    for c in range(tq // WINDOW):
        if c == 0:
            kcat = jnp.concatenate([kp_ref[...], kc_ref[0:WINDOW, :]], axis=0)
            vcat = jnp.concatenate([vp_ref[...], vc_ref[0:WINDOW, :]], axis=0)
            mask = band & (kj >= jnp.where(i > 0, 0, WINDOW))
        else:
            kcat = kc_ref[(c - 1) * WINDOW:(c + 1) * WINDOW, :]
            vcat = vc_ref[(c - 1) * WINDOW:(c + 1) * WINDOW, :]
            mask = band
        for j in range(SWA_KV_HEADS):
            kj2 = kcat[:, j * LANES:(j + 1) * LANES]
            vj2 = vcat[:, j * LANES:(j + 1) * LANES]
            zero = jnp.zeros_like(kj2)
            k_halves = (jnp.where(lo, kj2, zero), jnp.where(lo, zero, kj2))
            v_halves = (jnp.where(lo, vj2, zero), jnp.where(lo, zero, vj2))
            for p in range(SWA_GROUP // 2):
                g = j * (SWA_GROUP // 2) + p
                q = q_ref[c * WINDOW:(c + 1) * WINDOW, g * LANES:(g + 1) * LANES]
                out = jnp.zeros((WINDOW, LANES), F32)
                for half in range(2):
                    sink = sink_ref[2 * g + half]
                    s = lax.dot_general(q, k_halves[half], dn, preferred_element_type=F32)
                    s = jnp.where(mask, s, NEG_INF)
                    m = jnp.maximum(jnp.max(s, axis=1, keepdims=True), sink)
                    e = jnp.exp(s - m)
                    denom = jnp.sum(e, axis=1, keepdims=True) + jnp.exp(sink - m)
                    pv = jnp.dot(e.astype(BF16), v_halves[half], preferred_element_type=F32)
                    out = out + pv / denom
                o_ref[c * WINDOW:(c + 1) * WINDOW, g * LANES:(g + 1) * LANES] = out.astype(o_ref.dtype)


def _swa(sinks, qa, ka2, va2, B, S):
    N = qa.shape[0]
    tq = TQ_SWA
    nq = S // tq
    wpt = tq // WINDOW
    wps = S // WINDOW
    cur = lambda b, i: (b * nq + i, 0)
    prev = lambda b, i: (b * wps + jnp.maximum(i * wpt - 1, 0), 0)
    return pl.pallas_call(
        _swa_kernel,
        out_shape=jax.ShapeDtypeStruct((N, qa.shape[1]), BF16),
        grid=(B, nq),
        in_specs=[pl.BlockSpec(memory_space=pltpu.SMEM),
                  pl.BlockSpec((tq, qa.shape[1]), cur),
                  pl.BlockSpec((tq, ka2.shape[1]), cur),
                  pl.BlockSpec((WINDOW, ka2.shape[1]), prev),
                  pl.BlockSpec((tq, va2.shape[1]), cur),
                  pl.BlockSpec((WINDOW, va2.shape[1]), prev)],
        out_specs=pl.BlockSpec((tq, qa.shape[1]), cur),
        compiler_params=_cparams(("arbitrary", "arbitrary")),
        name="swa",
    )(sinks, qa, ka2, ka2, va2, va2)


def _diff_kernel(lam_ref, g_ref, qt_ref, k_ref, vt_ref, o_ref, sa_ref, sb_ref, m_ref, acc_ref, *, lambda_init):
    i = pl.program_id(2)
    tq = qt_ref.shape[3]
    tk = vt_ref.shape[3]
    qt = qt_ref[0, 0]
    lane = lax.broadcasted_iota(jnp.int32, (1, LANES), 1)
    lo = lane < HEAD_DIM
    m_ref[...] = jnp.full(m_ref.shape, NEG_INF, F32)
    acc_ref[...] = jnp.zeros(acc_ref.shape, F32)

    def scores(c, s_ref):
        k = k_ref[pl.ds(pl.multiple_of(c * tk, tk), tk), :]
        zero = jnp.zeros_like(k)
        s_ref[0] = jnp.dot(jnp.where(lo, k, zero), qt, preferred_element_type=F32)
        s_ref[1] = jnp.dot(jnp.where(lo, zero, k), qt, preferred_element_type=F32)

    def consume(c, s_ref, first_key):
        vt = vt_ref[0, c]
        for mp in range(2):
            s = s_ref[mp]
            if first_key is not None:
                kpos = lax.broadcasted_iota(jnp.int32, (tk, tq), 0) + first_key
                qpos = lax.broadcasted_iota(jnp.int32, (tk, tq), 1)
                s = jnp.where(kpos <= qpos, s, NEG_INF)
            m_prev = m_ref[mp]
            m_new = jnp.maximum(m_prev, jnp.max(s, axis=0, keepdims=True))
            alpha = jnp.exp2(m_prev - m_new)
            p = jnp.exp2(s - m_new).astype(BF16)
            acc_ref[mp] = alpha * acc_ref[mp] + jnp.dot(vt, p, preferred_element_type=F32)
            m_ref[mp] = m_new

    scores(0, sa_ref)

    def pair(jj, carry):
        c = 2 * jj
        scores(c + 1, sb_ref)
        consume(c, sa_ref, None)
        scores(c + 2, sa_ref)
        consume(c + 1, sb_ref, None)
        return carry

    lax.fori_loop(0, lax.shift_right_logical(i, 1), pair, 0)

    @pl.when(i % 2 == 0)
    def _():
        consume(i, sa_ref, 0)

    @pl.when(i % 2 == 1)
    def _():
        scores(i, sb_ref)
        consume(i - 1, sa_ref, None)
        consume(i, sb_ref, 0)

    lq1, lk1, lq2, lk2 = (lam_ref[r:r + 1, :] for r in range(4))
    lam = (jnp.exp(jnp.sum(lq1 * lk1, axis=1, keepdims=True))
           - jnp.exp(jnp.sum(lq2 * lk2, axis=1, keepdims=True)) + lambda_init)
    d = DIFF_V_DIM
    ot = (acc_ref[0, 0:d, :] / acc_ref[0, d:d + 1, :]
          - lam * (acc_ref[1, 0:d, :] / acc_ref[1, d:d + 1, :]))
    ot = ot * lax.rsqrt(jnp.mean(ot * ot, axis=0, keepdims=True) + EPS)
    o_ref[...] = (ot.T * g_ref[...] * (1.0 - lambda_init)).astype(o_ref.dtype)


def _diffattn(lam_vecs, g_subln, qdt, kd, vdt, B, S, lambda_init):
    N, C = kd.shape
    tq = qdt.shape[3]
    nq = S // tq
    return pl.pallas_call(
        functools.partial(_diff_kernel, lambda_init=lambda_init),
        out_shape=jax.ShapeDtypeStruct((N, C), BF16),
        grid=(B, DIFF_HEADS, nq),
        in_specs=[pl.BlockSpec((4, HEAD_DIM), lambda b, h, i: (0, 0)),
                  pl.BlockSpec((1, DIFF_V_DIM), lambda b, h, i: (0, 0)),
                  pl.BlockSpec((1, 1, LANES, tq), lambda b, h, i: (b, i, h, 0)),
                  pl.BlockSpec((S, LANES), lambda b, h, i: (b, h)),
                  pl.BlockSpec((1, nq, VT_ROWS, tq), lambda b, h, i: (b, 0, h, 0))],
        out_specs=pl.BlockSpec((tq, LANES), lambda b, h, i: (b * nq + i, h)),
        scratch_shapes=[pltpu.VMEM((2, tq, tq), F32), pltpu.VMEM((2, tq, tq), F32),
                        pltpu.VMEM((2, 1, tq), F32), pltpu.VMEM((2, VT_ROWS, tq), F32)],
        compiler_params=_cparams(("arbitrary", "arbitrary", "arbitrary")),
        name="diffattn",
    )(lam_vecs, g_subln, qdt, kd, vdt)


def _outproj_kernel(oa_ref, ob_ref, x_ref, mod_ref, wo_ref, bo_ref, g_ref, wr_ref, br_ref,
                    x1_ref, h2_ref, idx_ref, gate_ref, rank_ref, cnt_ref, carry_ref):
    i = pl.program_id(0)
    tm = x_ref.shape[0]
    half = oa_ref.shape[1]

    @pl.when(i == 0)
    def _():
        carry_ref[...] = jnp.zeros(carry_ref.shape, F32)

    gt1 = mod_ref[0, 2:3, :]
    sh2 = mod_ref[0, 3:4, :]
    sc2 = mod_ref[0, 4:5, :]
    mixed = (jnp.dot(oa_ref[...], wo_ref[0:half, :], preferred_element_type=F32)
             + jnp.dot(ob_ref[...], wo_ref[half:, :], preferred_element_type=F32) + bo_ref[...])
    x1 = x_ref[...] + gt1 * mixed
    x1_ref[...] = x1
    h2 = _rms(x1) * g_ref[...] * (1.0 + sc2) + sh2
    dn = (((1,), (1,)), ((), ()))
    h_hi = h2.astype(BF16)
    h2_ref[...] = _pack_bf16_pairs(h2)
    h_lo = (h2 - h_hi.astype(F32)).astype(BF16)
    w = wr_ref[...]
    w_hi = w.astype(BF16)
    w_lo = (w - w_hi.astype(F32)).astype(BF16)
    logits = (lax.dot_general(w_hi, h_hi, dn, preferred_element_type=F32)
              + lax.dot_general(w_hi, h_lo, dn, preferred_element_type=F32)
              + lax.dot_general(w_lo, h_hi, dn, preferred_element_type=F32)
              + br_ref[...])

    eidx = lax.broadcasted_iota(jnp.int32, logits.shape, 0)
    vals = logits
    onehots, top_vals, top_idx = [], [], []
    for _k in range(TOP_K):
        mx = jnp.max(vals, axis=0, keepdims=True)
        sel = jnp.min(jnp.where(vals == mx, eidx, N_EXPERTS), axis=0, keepdims=True)
        oh = eidx == sel
        onehots.append(oh)
        top_vals.append(mx)
        top_idx.append(sel)
        vals = jnp.where(oh, NEG_INF, vals)
    exps = [jnp.exp(v - top_vals[0]) for v in top_vals]
    denom = exps[0] + exps[1] + exps[2] + exps[3]
    gate_ref[...] = jnp.concatenate([e / denom for e in exps], axis=0)
    idx_ref[...] = jnp.concatenate(top_idx, axis=0)

    member = (onehots[0] | onehots[1] | onehots[2] | onehots[3])
    member_f = member.astype(F32)
    t_src = lax.broadcasted_iota(jnp.int32, (tm, tm), 0)
    t_dst = lax.broadcasted_iota(jnp.int32, (tm, tm), 1)
    before = (t_src < t_dst).astype(BF16)
    prefix = jnp.dot(member.astype(BF16), before, preferred_element_type=F32) + carry_ref[...]
    ranks = [jnp.sum(jnp.where(oh, prefix, 0.0), axis=0, keepdims=True) for oh in onehots]
    rank_ref[...] = jnp.concatenate(ranks, axis=0).astype(jnp.int32)
    carry_ref[...] = carry_ref[...] + jnp.sum(member_f, axis=1, keepdims=True)
    cnt_ref[...] = jnp.broadcast_to(carry_ref[...], cnt_ref.shape)


def _outproj(out_a, out_b, x2, mod3, w_out, b_out, g_ffn, wr_t, br_col, S):
    N, D = x2.shape
    tm = TM_OUT
    tiles_per_seq = S // tm
    row = lambda i: (i, 0)
    colb = lambda i: (0, i)
    const = lambda i: (0, 0)
    return pl.pallas_call(
        _outproj_kernel,
        out_shape=[jax.ShapeDtypeStruct((N, D), F32), jax.ShapeDtypeStruct((N, D // 2), jnp.int32),
                   jax.ShapeDtypeStruct((TOP_K, N), jnp.int32), jax.ShapeDtypeStruct((TOP_K, N), F32),
                   jax.ShapeDtypeStruct((TOP_K, N), jnp.int32), jax.ShapeDtypeStruct((N_EXPERTS, LANES), F32)],
        grid=(N // tm,),
        in_specs=[pl.BlockSpec((tm, out_a.shape[1]), row),
                  pl.BlockSpec((tm, out_b.shape[1]), row),
                  pl.BlockSpec((tm, D), row),
                  pl.BlockSpec((1, N_MOD, D), lambda i: (i // tiles_per_seq, 0, 0)),
                  pl.BlockSpec(w_out.shape, const),
                  pl.BlockSpec((1, D), const),
                  pl.BlockSpec((1, D), const),
                  pl.BlockSpec(wr_t.shape, const),
                  pl.BlockSpec((N_EXPERTS, 1), const)],
        out_specs=[pl.BlockSpec((tm, D), row), pl.BlockSpec((tm, D // 2), row),
                   pl.BlockSpec((TOP_K, tm), colb), pl.BlockSpec((TOP_K, tm), colb),
                   pl.BlockSpec((TOP_K, tm), colb), pl.BlockSpec((N_EXPERTS, LANES), const)],
        scratch_shapes=[pltpu.VMEM((N_EXPERTS, 1), F32)],
        compiler_params=_cparams(("arbitrary",)),
        name="outproj_router",
    )(out_a, out_b, x2, mod3, w_out, b_out, g_ffn, wr_t, br_col)


def _experts_kernel(first_ref, nblk_ref, x_ref, w1_ref, b1g_ref, b1l_ref, w2_ref, b2_ref, y_ref,
                    w1g_s, w1l_s, w2_s, xbuf, ybuf, xsem, ysem):
    e = pl.program_id(0)
    first = first_ref[e]
    n_blk = nblk_ref[e]
    bm = xbuf.shape[1]

    def x_copy(j, slot):
        rows = pl.ds(pl.multiple_of((first + j) * bm, bm), bm)
        return pltpu.make_async_copy(x_ref.at[rows], xbuf.at[slot], xsem.at[slot])

    def y_copy(j, slot):
        rows = pl.ds(pl.multiple_of((first + j) * bm, bm), bm)
        return pltpu.make_async_copy(ybuf.at[slot], y_ref.at[rows], ysem.at[slot])

    def swiglu(ug, ul):
        glu = jnp.minimum(ug, SWIGLU_LIMIT)
        lin = jnp.clip(ul, -SWIGLU_LIMIT, SWIGLU_LIMIT)
        return (glu * jax.nn.sigmoid(SWIGLU_ALPHA * glu) * (lin + 1.0)).astype(BF16)

    @pl.when(n_blk > 0)
    def _():
        x_copy(0, 0).start(priority=1)
        ch = 256
        half = ch // 2
        for c in range(w1_ref.shape[2] // ch):
            t = w1_ref[0, :, c * ch:(c + 1) * ch].astype(BF16).T
            pairs = pltpu.bitcast(t, jnp.int32)
            cols = slice(c * half, (c + 1) * half)
            w1g_s[:, cols] = lax.bitcast_convert_type(lax.shift_left(pairs, 16), F32).astype(BF16).T
            w1l_s[:, cols] = lax.bitcast_convert_type(pairs & jnp.int32(HIGH_HALF), F32).astype(BF16).T
        w2_s[...] = w2_ref[0].astype(BF16)

    def block(j, carry):
        slot = j & 1
        x_copy(j, slot).wait()

        @pl.when(j + 1 < n_blk)
        def _():
            x_copy(j + 1, 1 - slot).start(priority=1)

        @pl.when(j >= 2)
        def _():
            y_copy(j - 2, slot).wait()

        x = _unpack_bf16_pairs(xbuf[slot]).astype(BF16)
        ug = jnp.dot(x, w1g_s[...], preferred_element_type=F32) + b1g_ref[0]
        ul = jnp.dot(x, w1l_s[...], preferred_element_type=F32) + b1l_ref[0]
        y = jnp.dot(swiglu(ug, ul), w2_s[...], preferred_element_type=F32) + b2_ref[0]
        ybuf[slot] = _pack_bf16_pairs(y)
        y_copy(j, slot).start(priority=1)
        return carry

    lax.fori_loop(0, n_blk, block, 0)

    @pl.when(n_blk >= 2)
    def _():
        y_copy(n_blk - 2, n_blk & 1).wait()

    @pl.when(n_blk >= 1)
    def _():
        y_copy(n_blk - 1, (n_blk - 1) & 1).wait()


def _experts(first_blk, n_blk, x_rows, w1, b1g, b1l, w2, b2):
    E, Fh, D = w2.shape
    bm = BLOCK_ROWS
    n_rows = x_rows.shape[0]
    wsel = lambda e, fb, nb: (e, 0, 0)
    return pl.pallas_call(
        _experts_kernel,
        out_shape=jax.ShapeDtypeStruct((n_rows, D // 2), jnp.int32),
        grid_spec=pltpu.PrefetchScalarGridSpec(
            num_scalar_prefetch=2,
            grid=(E,),
            in_specs=[pl.BlockSpec(memory_space=pl.ANY),
                      pl.BlockSpec((1, D, 2 * Fh), wsel),
                      pl.BlockSpec((1, 1, Fh), wsel),
                      pl.BlockSpec((1, 1, Fh), wsel),
                      pl.BlockSpec((1, Fh, D), wsel),
                      pl.BlockSpec((1, 1, D), wsel)],
            out_specs=pl.BlockSpec(memory_space=pl.ANY),
            scratch_shapes=[pltpu.VMEM((D, Fh), BF16), pltpu.VMEM((D, Fh), BF16), pltpu.VMEM((Fh, D), BF16),
                            pltpu.VMEM((2, bm, D // 2), jnp.int32), pltpu.VMEM((2, bm, D // 2), jnp.int32),
                            pltpu.SemaphoreType.DMA((2,)), pltpu.SemaphoreType.DMA((2,))]),
        compiler_params=_cparams(("arbitrary",), 56 * 1024 * 1024),
        name="experts",
    )(first_blk, n_blk, x_rows, w1, b1g, b1l, w2, b2)


SC_CORES = 2
SC_SUBCORES = 16
SC_CHUNK = 64


def _sc_gather_rows(table, idx):
    M = idx.shape[0]
    D = table.shape[1]
    workers = SC_CORES * SC_SUBCORES
    per_worker = M // workers
    n_chunks = per_worker // SC_CHUNK
    assert M % workers == 0 and per_worker % (2 * SC_CHUNK) == 0
    mesh = plsc.VectorSubcoreMesh(core_axis_name="c", subcore_axis_name="s")

    @functools.partial(
        pl.kernel, mesh=mesh,
        out_type=jax.ShapeDtypeStruct((M, D), table.dtype),
        scratch_types=[pltpu.VMEM((per_worker,), jnp.int32),
                       pltpu.VMEM((SC_CHUNK, D), table.dtype), pltpu.VMEM((SC_CHUNK, D), table.dtype),
                       pltpu.SemaphoreType.DMA, pltpu.SemaphoreType.DMA],
        name="sc_gather_rows")
    def gather(table_hbm, idx_hbm, out_hbm, idx_v, rows0, rows1, sem0, sem1):
        wid = lax.axis_index("s") * SC_CORES + lax.axis_index("c")
        base = wid * per_worker
        pltpu.sync_copy(idx_hbm.at[pl.ds(base, per_worker)], idx_v)

        def fetch(c, buf, sem):
            off = pl.multiple_of(c * SC_CHUNK, SC_CHUNK)
            return pltpu.make_async_copy(table_hbm.at[idx_v.at[pl.ds(off, SC_CHUNK)]], buf, sem)

        def flush(c, buf):
            off = pl.multiple_of(c * SC_CHUNK, SC_CHUNK)
            pltpu.sync_copy(buf, out_hbm.at[pl.ds(base + off, SC_CHUNK)])

        fetch(0, rows0, sem0).start()

        @pl.loop(0, n_chunks // 2)
        def _(jj):
            c = 2 * jj
            fetch(c + 1, rows1, sem1).start()
            fetch(c, rows0, sem0).wait()
            flush(c, rows0)

            @pl.when(c + 2 < n_chunks)
            def _():
                fetch(c + 2, rows0, sem0).start()

            fetch(c + 1, rows1, sem1).wait()
            flush(c + 1, rows1)

    return gather(table, idx)


SC_SCATTER_CHUNK = 128


def _sc_scatter_rows(rows, dest, n_rows):
    N, W = rows.shape
    workers = SC_CORES * SC_SUBCORES
    per_worker = N // workers
    chunks = per_worker // SC_SCATTER_CHUNK
    assert N % workers == 0 and per_worker % SC_SCATTER_CHUNK == 0
    dest3 = dest.reshape(TOP_K, N // SC_SCATTER_CHUNK, SC_SCATTER_CHUNK)
    mesh = plsc.VectorSubcoreMesh(core_axis_name="c", subcore_axis_name="s")

    @functools.partial(
        pl.kernel, mesh=mesh,
        out_type=jax.ShapeDtypeStruct((n_rows, W), rows.dtype),
        scratch_types=[pltpu.VMEM((TOP_K, chunks, SC_SCATTER_CHUNK), jnp.int32),
                       pltpu.VMEM((SC_SCATTER_CHUNK, W), rows.dtype)],
        name="sc_scatter_rows")
    def scatter(rows_hbm, dest_hbm, out_hbm, idx_v, rows_v):
        wid = lax.axis_index("s") * SC_CORES + lax.axis_index("c")
        for k in range(TOP_K):
            pltpu.sync_copy(dest_hbm.at[k, pl.ds(wid * chunks, chunks)], idx_v.at[k])

        @pl.loop(0, chunks)
        def _(j):
            start = pl.multiple_of(wid * per_worker + j * SC_SCATTER_CHUNK, SC_SCATTER_CHUNK)
            pltpu.sync_copy(rows_hbm.at[pl.ds(start, SC_SCATTER_CHUNK)], rows_v)
            for k in range(TOP_K):
                pltpu.sync_copy(rows_v, out_hbm.at[idx_v.at[k, j]])

    return scatter(rows, dest3)


def _combine_kernel(x1_ref, gate_ref, mod_ref, g_ref, y_ref, *rest, final_norm):
    o_ref = rest[-1]
    gate = gate_ref[...]
    moe = gate[:, 0:1] * _unpack_bf16_pairs(y_ref[0])
    for k in range(1, TOP_K):
        moe = moe + gate[:, k:k + 1] * _unpack_bf16_pairs(y_ref[k])
    gt2 = mod_ref[0, 5:6, :]
    x2 = x1_ref[...] + gt2 * moe
    o_ref[...] = _rms(x2) * g_ref[...] if final_norm else x2


def _combine(x1, gate_t, mod3, g_final, y_tok, S, final_norm, seq, prev_out):
    N, D = x1.shape
    tm = TM_ROWS
    off = seq * (S // tm)
    row = lambda i: (i + off, 0)
    in_specs = [pl.BlockSpec((tm, D), row),
                pl.BlockSpec((tm, TOP_K), row),
                pl.BlockSpec((1, N_MOD, D), lambda i: (seq, 0, 0)),
                pl.BlockSpec((1, D), lambda i: (0, 0)),
                pl.BlockSpec((TOP_K, tm, D // 2), lambda i: (0, i, 0))]
    args = [x1, gate_t, mod3, g_final, y_tok]
    aliases = {}
    if prev_out is not None:
        in_specs.append(pl.BlockSpec(memory_space=pl.ANY))
        args.append(prev_out)
        aliases = {len(args) - 1: 0}
    return pl.pallas_call(
        functools.partial(_combine_kernel, final_norm=final_norm),
        out_shape=jax.ShapeDtypeStruct((N, D), F32),
        grid=(S // tm,),
        in_specs=in_specs,
        out_specs=pl.BlockSpec((tm, D), row),
        input_output_aliases=aliases,
        compiler_params=_cparams(("arbitrary",)),
        name="combine",
    )(*args)


def _routing_tables(counts, idx, rank):
    bm = BLOCK_ROWS
    counts = counts.astype(jnp.int32)
    padded = (counts + bm - 1) // bm * bm
    pends = jnp.cumsum(padded)
    pstarts = pends - padded
    experts = jnp.arange(N_EXPERTS, dtype=jnp.int32)
    dest = jnp.sum(jnp.where(idx[..., None] == experts, pstarts, 0), axis=-1) + rank
    return dest.astype(jnp.int32), (pstarts // bm).astype(jnp.int32), (padded // bm).astype(jnp.int32)


def _extended_in_weights(w_in, b_in):
    a_q = SWA_Q_HEADS * HEAD_DIM
    a_kv = SWA_KV_HEADS * HEAD_DIM
    b_w = DIFF_HEADS * DIFF_V_DIM
    spans = [(0, a_q)]
    for base in (a_q, a_q + a_kv):
        for j in range(SWA_KV_HEADS):
            spans += [(base + j * HEAD_DIM, base + (j + 1) * HEAD_DIM)] * 2
    spans.append((a_q + 2 * a_kv, a_q + 2 * a_kv + 3 * b_w))
    w_ext = jnp.concatenate([w_in[:, lo:hi] for lo, hi in spans], axis=1).astype(BF16)
    b_ext = jnp.concatenate([b_in[lo:hi] for lo, hi in spans]).reshape(1, -1)
    widths = (a_q, 2 * a_kv, 2 * a_kv, b_w, b_w, b_w)
    return w_ext, b_ext, widths


def kernel(x, c, positions, w_ada, b_ada, g_mix, w_in, b_in, attn_sinks, lambda_q1, lambda_k1, lambda_q2,
           lambda_k2, g_subln, w_out, b_out, g_ffn, w_router, b_router, w1, b1, w2, b2, g_final):
    B, S, D = x.shape
    N = B * S
    depth = w_ada.shape[0]
    n_rows = (N * TOP_K + N_EXPERTS * (BLOCK_ROWS - 1) + BLOCK_ROWS - 1) // BLOCK_ROWS * BLOCK_ROWS

    inv = 1.0 / (ROPE_THETA ** (jnp.arange(0, HEAD_DIM, 2, dtype=F32) / HEAD_DIM))
    inv_lane = jnp.tile(inv, LANES // (HEAD_DIM // 2)).reshape(1, LANES)
    pos2 = positions.reshape(N, 1)
    xcur = x.reshape(N, D)

    for layer in range(depth):
        last = layer == depth - 1
        lambda_init = 0.8 - 0.6 * math.exp(-0.3 * layer)
        mod3 = _adaln(c, w_ada[layer], b_ada[layer]).reshape(B, N_MOD, D)

        w_ext, b_ext, widths = _extended_in_weights(w_in[layer], b_in[layer])
        qa, ka2, va2, qdt, kd, vdt = _inproj(xcur, pos2, inv_lane, mod3, g_mix[layer].reshape(1, D),
                                             w_ext, b_ext, S, widths)
        out_a = _swa(attn_sinks[layer], qa, ka2, va2, B, S)
        lam_vecs = jnp.stack([lambda_q1[layer], lambda_k1[layer], lambda_q2[layer], lambda_k2[layer]])
        out_b = _diffattn(lam_vecs, g_subln[layer].reshape(1, DIFF_V_DIM), qdt, kd, vdt, B, S, lambda_init)

        x1, h2, idx, gate, rank, counts = _outproj(
            out_a, out_b, xcur, mod3, w_out[layer].astype(BF16), b_out[layer].reshape(1, D),
            g_ffn[layer].reshape(1, D), w_router[layer].T, b_router[layer].reshape(N_EXPERTS, 1), S)

        dest, first_blk, n_blk = _routing_tables(counts[:, 0], idx, rank)
        x_rows = _sc_scatter_rows(h2, dest, n_rows)
        y_rows = _experts(first_blk, n_blk, x_rows, w1[layer],
                          b1[layer][:, None, 0::2], b1[layer][:, None, 1::2],
                          w2[layer], b2[layer][:, None, :])
        xnext = None
        for seq in range(B):
            dest_seq = dest[:, seq * S:(seq + 1) * S].reshape(-1)
            y_tok = _sc_gather_rows(y_rows, dest_seq).reshape(TOP_K, S, D // 2)
            xnext = _combine(x1, gate.T, mod3, g_final.reshape(1, D), y_tok, S, last, seq, xnext)
        xcur = xnext
    return xcur.reshape(B, S, D)
```

```python
import functools
import math

import jax
import jax.numpy as jnp
from jax import lax
from jax.experimental import pallas as pl
from jax.experimental.pallas import tpu as pltpu
from jax.experimental.pallas import tpu_sc as plsc

HEAD_DIM = 64
SWA_Q_HEADS = 8
SWA_KV_HEADS = 2
SWA_GROUP = SWA_Q_HEADS // SWA_KV_HEADS
WINDOW = 128
DIFF_HEADS = 4
DIFF_V_DIM = 2 * HEAD_DIM
ROPE_THETA = 10000.0
N_EXPERTS = 32
TOP_K = 4
SWIGLU_ALPHA = 1.702
SWIGLU_LIMIT = 7.0
EPS = 1e-5
N_MOD = 6

LANES = 128
F32 = jnp.float32
BF16 = jnp.bfloat16
NEG_INF = float("-inf")

TM_PROJ = 1024
TQ_SWA = 512
VT_ROWS = DIFF_V_DIM + 16
TM_OUT = 512
TM_ROWS = 256
BLOCK_ROWS = 512
VMEM_LIMIT = 48 * 1024 * 1024


def _cparams(sem, vmem=VMEM_LIMIT):
    return pltpu.CompilerParams(dimension_semantics=sem, vmem_limit_bytes=vmem)


def _adaln_kernel(ct_ref, w_ref, b_ref, o_ref):
    c = ct_ref[...]
    cond = c * jax.nn.sigmoid(c)
    w = w_ref[...]
    rows = [jnp.sum(w * cond[:, b:b + 1], axis=0, keepdims=True) for b in range(c.shape[1])]
    o_ref[...] = jnp.concatenate(rows, axis=0) + b_ref[...]


def _adaln(c, w_ada, b_ada):
    B, D = c.shape
    n_out = w_ada.shape[1]
    tn = 1024
    return pl.pallas_call(
        _adaln_kernel,
        out_shape=jax.ShapeDtypeStruct((B, n_out), F32),
        grid=(n_out // tn,),
        in_specs=[pl.BlockSpec((D, B), lambda j: (0, 0)),
                  pl.BlockSpec((D, tn), lambda j: (0, j)),
                  pl.BlockSpec((1, tn), lambda j: (0, j))],
        out_specs=pl.BlockSpec((B, tn), lambda j: (0, j)),
        compiler_params=_cparams(("arbitrary",)),
        name="adaln",
    )(c.T, w_ada, b_ada.reshape(1, n_out))


def _rms(x):
    return x * lax.rsqrt(jnp.mean(x * x, axis=-1, keepdims=True) + EPS)


HIGH_HALF = -65536


def _pack_bf16_pairs(v):
    bits = lax.bitcast_convert_type(v.astype(BF16).astype(F32), jnp.int32)
    half = v.shape[1] // 2
    return lax.shift_right_logical(bits[:, :half], 16) | (bits[:, half:] & jnp.int32(HIGH_HALF))


def _unpack_bf16_pairs(w):
    return jnp.concatenate([lax.bitcast_convert_type(lax.shift_left(w, 16), F32),
                            lax.bitcast_convert_type(w & jnp.int32(HIGH_HALF), F32)], axis=1)


def _inproj_kernel(x_ref, pos_ref, inv_ref, mod_ref, g_ref, w_ref, b_ref,
                   qa_ref, ka_ref, va_ref, qd_ref, kd_ref, vd_ref):
    x = x_ref[...]
    sh = mod_ref[0, 0:1, :]
    sc = mod_ref[0, 1:2, :]
    h = _rms(x) * g_ref[...] * (1.0 + sc) + sh
    proj = jnp.dot(h.astype(BF16), w_ref[...], preferred_element_type=F32) + b_ref[...]

    lane = lax.broadcasted_iota(jnp.int32, (1, LANES), 1)
    first_half = (lane & (HEAD_DIM - 1)) < (HEAD_DIM // 2)
    n_freq = HEAD_DIM // 2
    groups = LANES // n_freq
    tm = x.shape[0]
    rows = tm // groups
    group = lax.shift_right_logical(lane, n_freq.bit_length() - 1)
    pos = pos_ref[...].astype(F32)
    pos_q = pos[0:rows]
    for g in range(1, groups):
        pos_q = jnp.where(group == g, pos[g * rows:(g + 1) * rows], pos_q)
    ang_q = pos_q * inv_ref[...]

    def spread(table_q):
        blocks = []
        for g in range(groups):
            only = jnp.where(group == g, table_q, 0.0)
            full = only
            for r in range(1, groups):
                full = full + pltpu.roll(only, r * n_freq, axis=1)
            blocks.append(full)
        return jnp.concatenate(blocks, axis=0)

    cos = spread(jnp.cos(ang_q))
    sin = spread(jnp.sin(ang_q))
    sin_signed = jnp.where(first_half, -sin, sin)

    def rope(t):
        partner = jnp.where(first_half,
                            pltpu.roll(t, LANES - HEAD_DIM // 2, axis=1),
                            pltpu.roll(t, HEAD_DIM // 2, axis=1))
        return t * cos + partner * sin_signed

    def emit(out_ref, col0, width, rotary, scale, transposed):
        for j in range(width // LANES):
            t = proj[:, col0 + j * LANES: col0 + (j + 1) * LANES]
            if rotary:
                t = rope(t)
            if scale != 1.0:
                t = t * scale
            if transposed:
                rows = out_ref.shape[2] // (width // LANES)
                out_ref[0, 0, j * rows:j * rows + LANES, :] = t.T.astype(out_ref.dtype)
                if rows > LANES:
                    fill = lax.broadcasted_iota(jnp.int32, (rows - LANES, t.shape[0]), 0) == 0
                    out_ref[0, 0, j * rows + LANES:(j + 1) * rows, :] = fill.astype(out_ref.dtype)
            else:
                out_ref[:, j * LANES:(j + 1) * LANES] = t.astype(out_ref.dtype)

    swa_scale = 1.0 / math.sqrt(HEAD_DIM)
    diff_scale = math.log2(math.e) / math.sqrt(HEAD_DIM)
    col = 0
    for out_ref, width, rotary, scale, transposed in (
            (qa_ref, qa_ref.shape[1], True, swa_scale, False), (ka_ref, ka_ref.shape[1], True, 1.0, False),
            (va_ref, va_ref.shape[1], False, 1.0, False), (qd_ref, qd_ref.shape[2], True, diff_scale, True),
            (kd_ref, kd_ref.shape[1], True, 1.0, False),
            (vd_ref, vd_ref.shape[2] // VT_ROWS * LANES, False, 1.0, True)):
        emit(out_ref, col, width, rotary, scale, transposed)
        col += width


def _inproj(x2, pos2, inv_lane, mod3, g_mix, w_ext, b_ext, S, widths):
    N, D = x2.shape
    tm = TM_PROJ
    C = w_ext.shape[1]
    tiles_per_seq = S // tm
    row = lambda i: (i, 0)
    t_rows = (0, 0, 0, widths[3], 0, widths[5] // LANES * VT_ROWS)
    out_shape, out_specs = [], []
    for w, tr in zip(widths, t_rows):
        if tr:
            out_shape.append(jax.ShapeDtypeStruct((N // S, tiles_per_seq, tr, tm), BF16))
            out_specs.append(pl.BlockSpec((1, 1, tr, tm), lambda i: (i // tiles_per_seq, i % tiles_per_seq, 0, 0)))
        else:
            out_shape.append(jax.ShapeDtypeStruct((N, w), BF16))
            out_specs.append(pl.BlockSpec((tm, w), row))
    return pl.pallas_call(
        _inproj_kernel,
        out_shape=out_shape,
        grid=(N // tm,),
        in_specs=[pl.BlockSpec((tm, D), row),
                  pl.BlockSpec((tm, 1), row),
                  pl.BlockSpec((1, LANES), lambda i: (0, 0)),
                  pl.BlockSpec((1, N_MOD, D), lambda i: (i // tiles_per_seq, 0, 0)),
                  pl.BlockSpec((1, D), lambda i: (0, 0)),
                  pl.BlockSpec((D, C), lambda i: (0, 0)),
                  pl.BlockSpec((1, C), lambda i: (0, 0))],
        out_specs=out_specs,
        compiler_params=_cparams(("arbitrary",)),
        name="inproj",
    )(x2, pos2, inv_lane, mod3, g_mix, w_ext, b_ext)


def _swa_kernel(sink_ref, q_ref, kc_ref, kp_ref, vc_ref, vp_ref, o_ref):
    i = pl.program_id(1)
    tq = q_ref.shape[0]
    lane = lax.broadcasted_iota(jnp.int32, (1, LANES), 1)
    lo = lane < HEAD_DIM
    qi = lax.broadcasted_iota(jnp.int32, (WINDOW, 2 * WINDOW), 0) + WINDOW
    kj = lax.broadcasted_iota(jnp.int32, (WINDOW, 2 * WINDOW), 1)
    band = (qi - kj >= 0) & (qi - kj < WINDOW)
    dn = (((1,), (1,)), ((), ()))
    for c in range(tq // WINDOW):
        if c == 0:
            kcat = jnp.concatenate([kp_ref[...], kc_ref[0:WINDOW, :]], axis=0)
            vcat = jnp.concatenate([vp_ref[...], vc_ref[0:WINDOW, :]], axis=0)
            mask = band & (kj >= jnp.where(i > 0, 0, WINDOW))
        else:
            kcat = kc_ref[(c - 1) * WINDOW:(c + 1) * WINDOW, :]
            vcat = vc_ref[(c - 1) * WINDOW:(c + 1) * WINDOW, :]
            mask = band
        for j in range(SWA_KV_HEADS):
            kj2 = kcat[:, j * LANES:(j + 1) * LANES]
            vj2 = vcat[:, j * LANES:(j + 1) * LANES]
            zero = jnp.zeros_like(kj2)
            k_halves = (jnp.where(lo, kj2, zero), jnp.where(lo, zero, kj2))
            v_halves = (jnp.where(lo, vj2, zero), jnp.where(lo, zero, vj2))
            for p in range(SWA_GROUP // 2):
                g = j * (SWA_GROUP // 2) + p
                q = q_ref[c * WINDOW:(c + 1) * WINDOW, g * LANES:(g + 1) * LANES]
                out = jnp.zeros((WINDOW, LANES), F32)
                for half in range(2):
                    sink = sink_ref[2 * g + half]
                    s = lax.dot_general(q, k_halves[half], dn, preferred_element_type=F32)
                    s = jnp.where(mask, s, NEG_INF)
                    m = jnp.maximum(jnp.max(s, axis=1, keepdims=True), sink)
                    e = jnp.exp(s - m)
                    denom = jnp.sum(e, axis=1, keepdims=True) + jnp.exp(sink - m)
                    pv = jnp.dot(e.astype(BF16), v_halves[half], preferred_element_type=F32)
                    out = out + pv / denom
                o_ref[c * WINDOW:(c + 1) * WINDOW, g * LANES:(g + 1) * LANES] = out.astype(o_ref.dtype)


def _swa(sinks, qa, ka2, va2, B, S):
    N = qa.shape[0]
    tq = TQ_SWA
    nq = S // tq
    wpt = tq // WINDOW
    wps = S // WINDOW
    cur = lambda b, i: (b * nq + i, 0)
    prev = lambda b, i: (b * wps + jnp.maximum(i * wpt - 1, 0), 0)
    return pl.pallas_call(
        _swa_kernel,
        out_shape=jax.ShapeDtypeStruct((N, qa.shape[1]), BF16),
        grid=(B, nq),
        in_specs=[pl.BlockSpec(memory_space=pltpu.SMEM),
                  pl.BlockSpec((tq, qa.shape[1]), cur),
                  pl.BlockSpec((tq, ka2.shape[1]), cur),
                  pl.BlockSpec((WINDOW, ka2.shape[1]), prev),
                  pl.BlockSpec((tq, va2.shape[1]), cur),
                  pl.BlockSpec((WINDOW, va2.shape[1]), prev)],
        out_specs=pl.BlockSpec((tq, qa.shape[1]), cur),
        compiler_params=_cparams(("arbitrary", "arbitrary")),
        name="swa",
    )(sinks, qa, ka2, ka2, va2, va2)


def _diff_kernel(lam_ref, g_ref, qt_ref, k_ref, vt_ref, o_ref, sa_ref, sb_ref, pa_ref, pb_ref, al_ref, m_ref, acc_ref,
                 *, lambda_init):
    i = pl.program_id(2)
    tq = qt_ref.shape[3]
    tk = vt_ref.shape[3]
    qt = qt_ref[0, 0]
    lane = lax.broadcasted_iota(jnp.int32, (1, LANES), 1)
    lo = lane < HEAD_DIM
    m_ref[...] = jnp.full(m_ref.shape, NEG_INF, F32)
    acc_ref[...] = jnp.zeros(acc_ref.shape, F32)

    @pl.when((pl.program_id(0) == 0) & (pl.program_id(1) == 0) & (i == 0))
    def _():
        pb_ref[...] = jnp.zeros(pb_ref.shape, pb_ref.dtype)
        al_ref[...] = jnp.ones(al_ref.shape, F32)

    def scores(c, s_ref):
        k = k_ref[pl.ds(pl.multiple_of(c * tk, tk), tk), :]
        zero = jnp.zeros_like(k)
        s_ref[0] = jnp.dot(jnp.where(lo, k, zero), qt, preferred_element_type=F32)
        s_ref[1] = jnp.dot(jnp.where(lo, zero, k), qt, preferred_element_type=F32)

    def softmax(s_ref, p_ref, slot, first_key):
        for mp in range(2):
            s = s_ref[mp]
            if first_key is not None:
                kpos = lax.broadcasted_iota(jnp.int32, (tk, tq), 0) + first_key
                qpos = lax.broadcasted_iota(jnp.int32, (tk, tq), 1)
                s = jnp.where(kpos <= qpos, s, NEG_INF)
            m_prev = m_ref[mp]
            m_new = jnp.maximum(m_prev, jnp.max(s, axis=0, keepdims=True))
            al_ref[slot, mp] = jnp.exp2(m_prev - m_new)
            p_ref[mp] = jnp.exp2(s - m_new).astype(BF16)
            m_ref[mp] = m_new

    def pv(c, p_ref, slot, valid=None):
        vt = vt_ref[0, jnp.maximum(c, 0)]
        for mp in range(2):
            alpha = al_ref[slot, mp]
            add = jnp.dot(vt, p_ref[mp], preferred_element_type=F32)
            if valid is not None:
                alpha = jnp.where(valid, alpha, 1.0)
                add = jnp.where(valid, add, 0.0)
            acc_ref[mp] = alpha * acc_ref[mp] + add

    scores(0, sa_ref)

    def pair(jj, carry):
        c = 2 * jj
        scores(c + 1, sb_ref)
        softmax(sa_ref, pa_ref, 0, None)
        pv(c - 1, pb_ref, 1, jj > 0)
        scores(c + 2, sa_ref)
        softmax(sb_ref, pb_ref, 1, None)
        pv(c, pa_ref, 0)
        return carry

    lax.fori_loop(0, lax.shift_right_logical(i, 1), pair, 0)

    @pl.when(i % 2 == 0)
    def _():
        pv(i - 1, pb_ref, 1, i >= 1)
        softmax(sa_ref, pa_ref, 0, 0)
        pv(i, pa_ref, 0)

    @pl.when(i % 2 == 1)
    def _():
        scores(i, sb_ref)
        softmax(sa_ref, pa_ref, 0, None)
        pv(i - 2, pb_ref, 1, i >= 2)
        softmax(sb_ref, pb_ref, 1, 0)
        pv(i - 1, pa_ref, 0)
        pv(i, pb_ref, 1)

    lq1, lk1, lq2, lk2 = (lam_ref[r:r + 1, :] for r in range(4))
    lam = (jnp.exp(jnp.sum(lq1 * lk1, axis=1, keepdims=True))
           - jnp.exp(jnp.sum(lq2 * lk2, axis=1, keepdims=True)) + lambda_init)
    d = DIFF_V_DIM
    ot = (acc_ref[0, 0:d, :] / acc_ref[0, d:d + 1, :]
          - lam * (acc_ref[1, 0:d, :] / acc_ref[1, d:d + 1, :]))
    ot = ot * lax.rsqrt(jnp.mean(ot * ot, axis=0, keepdims=True) + EPS)
    o_ref[...] = (ot.T * g_ref[...] * (1.0 - lambda_init)).astype(o_ref.dtype)


def _diffattn(lam_vecs, g_subln, qdt, kd, vdt, B, S, lambda_init):
    N, C = kd.shape
    tq = qdt.shape[3]
    nq = S // tq
    return pl.pallas_call(
        functools.partial(_diff_kernel, lambda_init=lambda_init),
        out_shape=jax.ShapeDtypeStruct((N, C), BF16),
        grid=(B, DIFF_HEADS, nq),
        in_specs=[pl.BlockSpec((4, HEAD_DIM), lambda b, h, i: (0, 0)),
                  pl.BlockSpec((1, DIFF_V_DIM), lambda b, h, i: (0, 0)),
                  pl.BlockSpec((1, 1, LANES, tq), lambda b, h, i: (b, i, h, 0)),
                  pl.BlockSpec((S, LANES), lambda b, h, i: (b, h)),
                  pl.BlockSpec((1, nq, VT_ROWS, tq), lambda b, h, i: (b, 0, h, 0))],
        out_specs=pl.BlockSpec((tq, LANES), lambda b, h, i: (b * nq + i, h)),
        scratch_shapes=[pltpu.VMEM((2, tq, tq), F32), pltpu.VMEM((2, tq, tq), F32),
                        pltpu.VMEM((2, tq, tq), BF16), pltpu.VMEM((2, tq, tq), BF16),
                        pltpu.VMEM((2, 2, 1, tq), F32),
                        pltpu.VMEM((2, 1, tq), F32), pltpu.VMEM((2, VT_ROWS, tq), F32)],
        compiler_params=_cparams(("arbitrary", "arbitrary", "arbitrary")),
        name="diffattn",
    )(lam_vecs, g_subln, qdt, kd, vdt)


def _outproj_kernel(oa_ref, ob_ref, x_ref, mod_ref, wo_ref, bo_ref, g_ref, wr_ref, br_ref,
                    x1_ref, h2_ref, idx_ref, gate_ref, rank_ref, cnt_ref, carry_ref):
    i = pl.program_id(0)
    tm = x_ref.shape[0]
    half = oa_ref.shape[1]

    @pl.when(i == 0)
    def _():
        carry_ref[...] = jnp.zeros(carry_ref.shape, F32)

    gt1 = mod_ref[0, 2:3, :]
    sh2 = mod_ref[0, 3:4, :]
    sc2 = mod_ref[0, 4:5, :]
    mixed = (jnp.dot(oa_ref[...], wo_ref[0:half, :], preferred_element_type=F32)
             + jnp.dot(ob_ref[...], wo_ref[half:, :], preferred_element_type=F32) + bo_ref[...])
    x1 = x_ref[...] + gt1 * mixed
    x1_ref[...] = x1
    h2 = _rms(x1) * g_ref[...] * (1.0 + sc2) + sh2
    dn = (((1,), (1,)), ((), ()))
    h_hi = h2.astype(BF16)
    h2_ref[...] = _pack_bf16_pairs(h2)
    h_lo = (h2 - h_hi.astype(F32)).astype(BF16)
    w = wr_ref[...]
    w_hi = w.astype(BF16)
    w_lo = (w - w_hi.astype(F32)).astype(BF16)
    logits = (lax.dot_general(w_hi, h_hi, dn, preferred_element_type=F32)
              + lax.dot_general(w_hi, h_lo, dn, preferred_element_type=F32)
              + lax.dot_general(w_lo, h_hi, dn, preferred_element_type=F32)
              + br_ref[...])

    eidx = lax.broadcasted_iota(jnp.int32, logits.shape, 0)
    vals = logits
    onehots, top_vals, top_idx = [], [], []
    for _k in range(TOP_K):
        mx = jnp.max(vals, axis=0, keepdims=True)
        sel = jnp.min(jnp.where(vals == mx, eidx, N_EXPERTS), axis=0, keepdims=True)
        oh = eidx == sel
        onehots.append(oh)
        top_vals.append(mx)
        top_idx.append(sel)
        vals = jnp.where(oh, NEG_INF, vals)
    exps = [jnp.exp(v - top_vals[0]) for v in top_vals]
    denom = exps[0] + exps[1] + exps[2] + exps[3]
    gate_ref[...] = jnp.concatenate([e / denom for e in exps], axis=0)
    idx_ref[...] = jnp.concatenate(top_idx, axis=0)

    member = (onehots[0] | onehots[1] | onehots[2] | onehots[3])
    member_f = member.astype(F32)
    t_src = lax.broadcasted_iota(jnp.int32, (tm, tm), 0)
    t_dst = lax.broadcasted_iota(jnp.int32, (tm, tm), 1)
    before = (t_src < t_dst).astype(BF16)
    prefix = jnp.dot(member.astype(BF16), before, preferred_element_type=F32) + carry_ref[...]
    ranks = [jnp.sum(jnp.where(oh, prefix, 0.0), axis=0, keepdims=True) for oh in onehots]
    rank_ref[...] = jnp.concatenate(ranks, axis=0).astype(jnp.int32)
    carry_ref[...] = carry_ref[...] + jnp.sum(member_f, axis=1, keepdims=True)
    cnt_ref[...] = jnp.broadcast_to(carry_ref[...], cnt_ref.shape)


def _outproj(out_a, out_b, x2, mod3, w_out, b_out, g_ffn, wr_t, br_col, S):
    N, D = x2.shape
    tm = TM_OUT
    tiles_per_seq = S // tm
    row = lambda i: (i, 0)
    colb = lambda i: (0, i)
    const = lambda i: (0, 0)
    return pl.pallas_call(
        _outproj_kernel,
        out_shape=[jax.ShapeDtypeStruct((N, D), F32), jax.ShapeDtypeStruct((N, D // 2), jnp.int32),
                   jax.ShapeDtypeStruct((TOP_K, N), jnp.int32), jax.ShapeDtypeStruct((TOP_K, N), F32),
                   jax.ShapeDtypeStruct((TOP_K, N), jnp.int32), jax.ShapeDtypeStruct((N_EXPERTS, LANES), F32)],
        grid=(N // tm,),
        in_specs=[pl.BlockSpec((tm, out_a.shape[1]), row),
                  pl.BlockSpec((tm, out_b.shape[1]), row),
                  pl.BlockSpec((tm, D), row),
                  pl.BlockSpec((1, N_MOD, D), lambda i: (i // tiles_per_seq, 0, 0)),
                  pl.BlockSpec(w_out.shape, const),
                  pl.BlockSpec((1, D), const),
                  pl.BlockSpec((1, D), const),
                  pl.BlockSpec(wr_t.shape, const),
                  pl.BlockSpec((N_EXPERTS, 1), const)],
        out_specs=[pl.BlockSpec((tm, D), row), pl.BlockSpec((tm, D // 2), row),
                   pl.BlockSpec((TOP_K, tm), colb), pl.BlockSpec((TOP_K, tm), colb),
                   pl.BlockSpec((TOP_K, tm), colb), pl.BlockSpec((N_EXPERTS, LANES), const)],
        scratch_shapes=[pltpu.VMEM((N_EXPERTS, 1), F32)],
        compiler_params=_cparams(("arbitrary",)),
        name="outproj_router",
    )(out_a, out_b, x2, mod3, w_out, b_out, g_ffn, wr_t, br_col)


def _experts_kernel(first_ref, nblk_ref, x_ref, w1_ref, b1g_ref, b1l_ref, w2_ref, b2_ref, y_ref,
                    w1g_s, w1l_s, w2_s, xbuf, ybuf, xsem, ysem):
    e = pl.program_id(0)
    first = first_ref[e]
    n_blk = nblk_ref[e]
    bm = xbuf.shape[1]

    def x_copy(j, slot):
        rows = pl.ds(pl.multiple_of((first + j) * bm, bm), bm)
        return pltpu.make_async_copy(x_ref.at[rows], xbuf.at[slot], xsem.at[slot])

    def y_copy(j, slot):
        rows = pl.ds(pl.multiple_of((first + j) * bm, bm), bm)
        return pltpu.make_async_copy(ybuf.at[slot], y_ref.at[rows], ysem.at[slot])

    def swiglu(ug, ul):
        glu = jnp.minimum(ug, SWIGLU_LIMIT)
        lin = jnp.clip(ul, -SWIGLU_LIMIT, SWIGLU_LIMIT)
        return (glu * jax.nn.sigmoid(SWIGLU_ALPHA * glu) * (lin + 1.0)).astype(BF16)

    @pl.when(n_blk > 0)
    def _():
        x_copy(0, 0).start(priority=1)
        ch = 256
        half = ch // 2
        for c in range(w1_ref.shape[2] // ch):
            t = w1_ref[0, :, c * ch:(c + 1) * ch].astype(BF16).T
            pairs = pltpu.bitcast(t, jnp.int32)
            cols = slice(c * half, (c + 1) * half)
            w1g_s[:, cols] = lax.bitcast_convert_type(lax.shift_left(pairs, 16), F32).astype(BF16).T
            w1l_s[:, cols] = lax.bitcast_convert_type(pairs & jnp.int32(HIGH_HALF), F32).astype(BF16).T
        w2_s[...] = w2_ref[0].astype(BF16)

    def block(j, carry):
        slot = j & 1
        x_copy(j, slot).wait()

        @pl.when(j + 1 < n_blk)
        def _():
            x_copy(j + 1, 1 - slot).start(priority=1)

        @pl.when(j >= 2)
        def _():
            y_copy(j - 2, slot).wait()

        x = _unpack_bf16_pairs(xbuf[slot]).astype(BF16)
        ug = jnp.dot(x, w1g_s[...], preferred_element_type=F32) + b1g_ref[0]
        ul = jnp.dot(x, w1l_s[...], preferred_element_type=F32) + b1l_ref[0]
        y = jnp.dot(swiglu(ug, ul), w2_s[...], preferred_element_type=F32) + b2_ref[0]
        ybuf[slot] = _pack_bf16_pairs(y)
        y_copy(j, slot).start(priority=1)
        return carry

    lax.fori_loop(0, n_blk, block, 0)

    @pl.when(n_blk >= 2)
    def _():
        y_copy(n_blk - 2, n_blk & 1).wait()

    @pl.when(n_blk >= 1)
    def _():
        y_copy(n_blk - 1, (n_blk - 1) & 1).wait()


def _experts(first_blk, n_blk, x_rows, w1, b1g, b1l, w2, b2):
    E, Fh, D = w2.shape
    bm = BLOCK_ROWS
    n_rows = x_rows.shape[0]
    wsel = lambda e, fb, nb: (e, 0, 0)
    return pl.pallas_call(
        _experts_kernel,
        out_shape=jax.ShapeDtypeStruct((n_rows, D // 2), jnp.int32),
        grid_spec=pltpu.PrefetchScalarGridSpec(
            num_scalar_prefetch=2,
            grid=(E,),
            in_specs=[pl.BlockSpec(memory_space=pl.ANY),
                      pl.BlockSpec((1, D, 2 * Fh), wsel),
                      pl.BlockSpec((1, 1, Fh), wsel),
                      pl.BlockSpec((1, 1, Fh), wsel),
                      pl.BlockSpec((1, Fh, D), wsel),
                      pl.BlockSpec((1, 1, D), wsel)],
            out_specs=pl.BlockSpec(memory_space=pl.ANY),
            scratch_shapes=[pltpu.VMEM((D, Fh), BF16), pltpu.VMEM((D, Fh), BF16), pltpu.VMEM((Fh, D), BF16),
                            pltpu.VMEM((2, bm, D // 2), jnp.int32), pltpu.VMEM((2, bm, D // 2), jnp.int32),
                            pltpu.SemaphoreType.DMA((2,)), pltpu.SemaphoreType.DMA((2,))]),
        compiler_params=_cparams(("arbitrary",), 56 * 1024 * 1024),
        name="experts",
    )(first_blk, n_blk, x_rows, w1, b1g, b1l, w2, b2)


SC_CORES = 2
SC_SUBCORES = 16
SC_CHUNK = 64


def _sc_gather_rows(table, idx):
    M = idx.shape[0]
    D = table.shape[1]
    workers = SC_CORES * SC_SUBCORES
    per_worker = M // workers
    n_chunks = per_worker // SC_CHUNK
    assert M % workers == 0 and per_worker % (2 * SC_CHUNK) == 0
    mesh = plsc.VectorSubcoreMesh(core_axis_name="c", subcore_axis_name="s")

    @functools.partial(
        pl.kernel, mesh=mesh,
        out_type=jax.ShapeDtypeStruct((M, D), table.dtype),
        scratch_types=[pltpu.VMEM((per_worker,), jnp.int32),
                       pltpu.VMEM((SC_CHUNK, D), table.dtype), pltpu.VMEM((SC_CHUNK, D), table.dtype),
                       pltpu.SemaphoreType.DMA, pltpu.SemaphoreType.DMA],
        name="sc_gather_rows")
    def gather(table_hbm, idx_hbm, out_hbm, idx_v, rows0, rows1, sem0, sem1):
        wid = lax.axis_index("s") * SC_CORES + lax.axis_index("c")
        base = wid * per_worker
        pltpu.sync_copy(idx_hbm.at[pl.ds(base, per_worker)], idx_v)

        def fetch(c, buf, sem):
            off = pl.multiple_of(c * SC_CHUNK, SC_CHUNK)
            return pltpu.make_async_copy(table_hbm.at[idx_v.at[pl.ds(off, SC_CHUNK)]], buf, sem)

        def flush(c, buf):
            off = pl.multiple_of(c * SC_CHUNK, SC_CHUNK)
            pltpu.sync_copy(buf, out_hbm.at[pl.ds(base + off, SC_CHUNK)])

        fetch(0, rows0, sem0).start()

        @pl.loop(0, n_chunks // 2)
        def _(jj):
            c = 2 * jj
            fetch(c + 1, rows1, sem1).start()
            fetch(c, rows0, sem0).wait()
            flush(c, rows0)

            @pl.when(c + 2 < n_chunks)
            def _():
                fetch(c + 2, rows0, sem0).start()

            fetch(c + 1, rows1, sem1).wait()
            flush(c + 1, rows1)

    return gather(table, idx)


SC_SCATTER_CHUNK = 128


def _sc_scatter_rows(rows, dest, n_rows):
    N, W = rows.shape
    workers = SC_CORES * SC_SUBCORES
    per_worker = N // workers
    chunks = per_worker // SC_SCATTER_CHUNK
    assert N % workers == 0 and per_worker % SC_SCATTER_CHUNK == 0
    dest3 = dest.reshape(TOP_K, N // SC_SCATTER_CHUNK, SC_SCATTER_CHUNK)
    mesh = plsc.VectorSubcoreMesh(core_axis_name="c", subcore_axis_name="s")

    @functools.partial(
        pl.kernel, mesh=mesh,
        out_type=jax.ShapeDtypeStruct((n_rows, W), rows.dtype),
        scratch_types=[pltpu.VMEM((TOP_K, chunks, SC_SCATTER_CHUNK), jnp.int32),
                       pltpu.VMEM((SC_SCATTER_CHUNK, W), rows.dtype)],
        name="sc_scatter_rows")
    def scatter(rows_hbm, dest_hbm, out_hbm, idx_v, rows_v):
        wid = lax.axis_index("s") * SC_CORES + lax.axis_index("c")
        for k in range(TOP_K):
            pltpu.sync_copy(dest_hbm.at[k, pl.ds(wid * chunks, chunks)], idx_v.at[k])

        @pl.loop(0, chunks)
        def _(j):
            start = pl.multiple_of(wid * per_worker + j * SC_SCATTER_CHUNK, SC_SCATTER_CHUNK)
            pltpu.sync_copy(rows_hbm.at[pl.ds(start, SC_SCATTER_CHUNK)], rows_v)
            for k in range(TOP_K):
                pltpu.sync_copy(rows_v, out_hbm.at[idx_v.at[k, j]])

    return scatter(rows, dest3)


def _combine_kernel(x1_ref, gate_ref, mod_ref, g_ref, y_ref, o_ref, *, final_norm):
    gate = gate_ref[...]
    moe = gate[:, 0:1] * _unpack_bf16_pairs(y_ref[0])
    for k in range(1, TOP_K):
        moe = moe + gate[:, k:k + 1] * _unpack_bf16_pairs(y_ref[k])
    gt2 = mod_ref[0, 5:6, :]
    x2 = x1_ref[...] + gt2 * moe
    o_ref[...] = _rms(x2) * g_ref[...] if final_norm else x2


def _combine(x1, gate_t, mod3, g_final, y_tok, S, final_norm):
    N, D = x1.shape
    tm = TM_ROWS
    tiles_per_seq = S // tm
    row = lambda i: (i, 0)
    return pl.pallas_call(
        functools.partial(_combine_kernel, final_norm=final_norm),
        out_shape=jax.ShapeDtypeStruct((N, D), F32),
        grid=(N // tm,),
        in_specs=[pl.BlockSpec((tm, D), row),
                  pl.BlockSpec((tm, TOP_K), row),
                  pl.BlockSpec((1, N_MOD, D), lambda i: (i // tiles_per_seq, 0, 0)),
                  pl.BlockSpec((1, D), lambda i: (0, 0)),
                  pl.BlockSpec((TOP_K, tm, D // 2), lambda i: (0, i, 0))],
        out_specs=pl.BlockSpec((tm, D), row),
        compiler_params=_cparams(("arbitrary",)),
        name="combine",
    )(x1, gate_t, mod3, g_final, y_tok)


def _routing_tables(counts, idx, rank):
    bm = BLOCK_ROWS
    counts = counts.astype(jnp.int32)
    padded = (counts + bm - 1) // bm * bm
    pends = jnp.cumsum(padded)
    pstarts = pends - padded
    experts = jnp.arange(N_EXPERTS, dtype=jnp.int32)
    dest = jnp.sum(jnp.where(idx[..., None] == experts, pstarts, 0), axis=-1) + rank
    return dest.astype(jnp.int32), (pstarts // bm).astype(jnp.int32), (padded // bm).astype(jnp.int32)


def _extended_in_weights(w_in, b_in):
    a_q = SWA_Q_HEADS * HEAD_DIM
    a_kv = SWA_KV_HEADS * HEAD_DIM
    b_w = DIFF_HEADS * DIFF_V_DIM
    spans = [(0, a_q)]
    for base in (a_q, a_q + a_kv):
        for j in range(SWA_KV_HEADS):
            spans += [(base + j * HEAD_DIM, base + (j + 1) * HEAD_DIM)] * 2
    spans.append((a_q + 2 * a_kv, a_q + 2 * a_kv + 3 * b_w))
    w_ext = jnp.concatenate([w_in[:, lo:hi] for lo, hi in spans], axis=1).astype(BF16)
    b_ext = jnp.concatenate([b_in[lo:hi] for lo, hi in spans]).reshape(1, -1)
    widths = (a_q, 2 * a_kv, 2 * a_kv, b_w, b_w, b_w)
    return w_ext, b_ext, widths


def kernel(x, c, positions, w_ada, b_ada, g_mix, w_in, b_in, attn_sinks, lambda_q1, lambda_k1, lambda_q2,
           lambda_k2, g_subln, w_out, b_out, g_ffn, w_router, b_router, w1, b1, w2, b2, g_final):
    B, S, D = x.shape
    N = B * S
    depth = w_ada.shape[0]
    n_rows = (N * TOP_K + N_EXPERTS * (BLOCK_ROWS - 1) + BLOCK_ROWS - 1) // BLOCK_ROWS * BLOCK_ROWS

    inv = 1.0 / (ROPE_THETA ** (jnp.arange(0, HEAD_DIM, 2, dtype=F32) / HEAD_DIM))
    inv_lane = jnp.tile(inv, LANES // (HEAD_DIM // 2)).reshape(1, LANES)
    pos2 = positions.reshape(N, 1)
    xcur = x.reshape(N, D)

    for layer in range(depth):
        last = layer == depth - 1
        lambda_init = 0.8 - 0.6 * math.exp(-0.3 * layer)
        mod3 = _adaln(c, w_ada[layer], b_ada[layer]).reshape(B, N_MOD, D)

        w_ext, b_ext, widths = _extended_in_weights(w_in[layer], b_in[layer])
        qa, ka2, va2, qdt, kd, vdt = _inproj(xcur, pos2, inv_lane, mod3, g_mix[layer].reshape(1, D),
                                             w_ext, b_ext, S, widths)
        out_a = _swa(attn_sinks[layer], qa, ka2, va2, B, S)
        lam_vecs = jnp.stack([lambda_q1[layer], lambda_k1[layer], lambda_q2[layer], lambda_k2[layer]])
        out_b = _diffattn(lam_vecs, g_subln[layer].reshape(1, DIFF_V_DIM), qdt, kd, vdt, B, S, lambda_init)

        x1, h2, idx, gate, rank, counts = _outproj(
            out_a, out_b, xcur, mod3, w_out[layer].astype(BF16), b_out[layer].reshape(1, D),
            g_ffn[layer].reshape(1, D), w_router[layer].T, b_router[layer].reshape(N_EXPERTS, 1), S)

        dest, first_blk, n_blk = _routing_tables(counts[:, 0], idx, rank)
        x_rows = _sc_scatter_rows(h2, dest, n_rows)
        y_rows = _experts(first_blk, n_blk, x_rows, w1[layer],
                          b1[layer][:, None, 0::2], b1[layer][:, None, 1::2],
                          w2[layer], b2[layer][:, None, :])
        y_tok = _sc_gather_rows(y_rows, dest.reshape(-1)).reshape(TOP_K, N, D // 2)
        xcur = _combine(x1, gate.T, mod3, g_final.reshape(1, D), y_tok, S, final_norm=last)
    return xcur.reshape(B, S, D)
```

```python
import functools
import math

import jax
import jax.numpy as jnp
from jax import lax
from jax.experimental import pallas as pl
from jax.experimental.pallas import tpu as pltpu
from jax.experimental.pallas import tpu_sc as plsc

HEAD_DIM = 64
SWA_Q_HEADS = 8
SWA_KV_HEADS = 2
SWA_GROUP = SWA_Q_HEADS // SWA_KV_HEADS
WINDOW = 128
DIFF_HEADS = 4
DIFF_V_DIM = 2 * HEAD_DIM
ROPE_THETA = 10000.0
N_EXPERTS = 32
TOP_K = 4
SWIGLU_ALPHA = 1.702
SWIGLU_LIMIT = 7.0
EPS = 1e-5
N_MOD = 6

LANES = 128
F32 = jnp.float32
BF16 = jnp.bfloat16
NEG_INF = float("-inf")

TM_PROJ = 1024
TQ_SWA = 512
VT_ROWS = DIFF_V_DIM + 16
TM_OUT = 512
TM_ROWS = 256
BLOCK_ROWS = 512
VMEM_LIMIT = 48 * 1024 * 1024


def _cparams(sem, vmem=VMEM_LIMIT):
    return pltpu.CompilerParams(dimension_semantics=sem, vmem_limit_bytes=vmem)


def _adaln_kernel(ct_ref, w_ref, b_ref, o_ref):
    c = ct_ref[...]
    cond = c * jax.nn.sigmoid(c)
    w = w_ref[...]
    rows = [jnp.sum(w * cond[:, b:b + 1], axis=0, keepdims=True) for b in range(c.shape[1])]
    o_ref[...] = jnp.concatenate(rows, axis=0) + b_ref[...]


def _adaln(c, w_ada, b_ada):
    B, D = c.shape
    n_out = w_ada.shape[1]
    tn = 1024
    return pl.pallas_call(
        _adaln_kernel,
        out_shape=jax.ShapeDtypeStruct((B, n_out), F32),
        grid=(n_out // tn,),
        in_specs=[pl.BlockSpec((D, B), lambda j: (0, 0)),
                  pl.BlockSpec((D, tn), lambda j: (0, j)),
                  pl.BlockSpec((1, tn), lambda j: (0, j))],
        out_specs=pl.BlockSpec((B, tn), lambda j: (0, j)),
        compiler_params=_cparams(("arbitrary",)),
        name="adaln",
    )(c.T, w_ada, b_ada.reshape(1, n_out))


def _rms(x):
    return x * lax.rsqrt(jnp.mean(x * x, axis=-1, keepdims=True) + EPS)


HIGH_HALF = -65536


def _pack_bf16_pairs(v):
    bits = lax.bitcast_convert_type(v.astype(BF16).astype(F32), jnp.int32)
    half = v.shape[1] // 2
    return lax.shift_right_logical(bits[:, :half], 16) | (bits[:, half:] & jnp.int32(HIGH_HALF))


def _unpack_bf16_pairs(w):
    return jnp.concatenate([lax.bitcast_convert_type(lax.shift_left(w, 16), F32),
                            lax.bitcast_convert_type(w & jnp.int32(HIGH_HALF), F32)], axis=1)


def _inproj_kernel(x_ref, pos_ref, inv_ref, mod_ref, g_ref, w_ref, b_ref,
                   qa_ref, ka_ref, va_ref, qd_ref, kd_ref, vd_ref):
    x = x_ref[...]
    sh = mod_ref[0, 0:1, :]
    sc = mod_ref[0, 1:2, :]
    h = _rms(x) * g_ref[...] * (1.0 + sc) + sh
    proj = jnp.dot(h.astype(BF16), w_ref[...], preferred_element_type=F32) + b_ref[...]

    lane = lax.broadcasted_iota(jnp.int32, (1, LANES), 1)
    first_half = (lane & (HEAD_DIM - 1)) < (HEAD_DIM // 2)
    n_freq = HEAD_DIM // 2
    groups = LANES // n_freq
    tm = x.shape[0]
    rows = tm // groups
    group = lax.shift_right_logical(lane, n_freq.bit_length() - 1)
    pos = pos_ref[...].astype(F32)
    pos_q = pos[0:rows]
    for g in range(1, groups):
        pos_q = jnp.where(group == g, pos[g * rows:(g + 1) * rows], pos_q)
    ang_q = pos_q * inv_ref[...]

    def spread(table_q):
        blocks = []
        for g in range(groups):
            only = jnp.where(group == g, table_q, 0.0)
            full = only
            for r in range(1, groups):
                full = full + pltpu.roll(only, r * n_freq, axis=1)
            blocks.append(full)
        return jnp.concatenate(blocks, axis=0)

    cos = spread(jnp.cos(ang_q))
    sin = spread(jnp.sin(ang_q))
    sin_signed = jnp.where(first_half, -sin, sin)

    def rope(t):
        partner = jnp.where(first_half,
                            pltpu.roll(t, LANES - HEAD_DIM // 2, axis=1),
                            pltpu.roll(t, HEAD_DIM // 2, axis=1))
        return t * cos + partner * sin_signed

    def emit(out_ref, col0, width, rotary, scale, transposed):
        for j in range(width // LANES):
            t = proj[:, col0 + j * LANES: col0 + (j + 1) * LANES]
            if rotary:
                t = rope(t)
            if scale != 1.0:
                t = t * scale
            if transposed:
                rows = out_ref.shape[2] // (width // LANES)
                out_ref[0, 0, j * rows:j * rows + LANES, :] = t.T.astype(out_ref.dtype)
                if rows > LANES:
                    fill = lax.broadcasted_iota(jnp.int32, (rows - LANES, t.shape[0]), 0) == 0
                    out_ref[0, 0, j * rows + LANES:(j + 1) * rows, :] = fill.astype(out_ref.dtype)
            else:
                out_ref[:, j * LANES:(j + 1) * LANES] = t.astype(out_ref.dtype)

    swa_scale = 1.0 / math.sqrt(HEAD_DIM)
    diff_scale = math.log2(math.e) / math.sqrt(HEAD_DIM)
    col = 0
    for out_ref, width, rotary, scale, transposed in (
            (qa_ref, qa_ref.shape[1], True, swa_scale, False), (ka_ref, ka_ref.shape[1], True, 1.0, False),
            (va_ref, va_ref.shape[1], False, 1.0, False), (qd_ref, qd_ref.shape[2], True, diff_scale, True),
            (kd_ref, kd_ref.shape[1], True, 1.0, False),
            (vd_ref, vd_ref.shape[2] // VT_ROWS * LANES, False, 1.0, True)):
        emit(out_ref, col, width, rotary, scale, transposed)
        col += width


def _inproj(x2, pos2, inv_lane, mod3, g_mix, w_ext, b_ext, S, widths):
    N, D = x2.shape
    tm = TM_PROJ
    C = w_ext.shape[1]
    tiles_per_seq = S // tm
    row = lambda i: (i, 0)
    t_rows = (0, 0, 0, widths[3], 0, widths[5] // LANES * VT_ROWS)
    out_shape, out_specs = [], []
    for w, tr in zip(widths, t_rows):
        if tr:
            out_shape.append(jax.ShapeDtypeStruct((N // S, tiles_per_seq, tr, tm), BF16))
            out_specs.append(pl.BlockSpec((1, 1, tr, tm), lambda i: (i // tiles_per_seq, i % tiles_per_seq, 0, 0)))
        else:
            out_shape.append(jax.ShapeDtypeStruct((N, w), BF16))
            out_specs.append(pl.BlockSpec((tm, w), row))
    return pl.pallas_call(
        _inproj_kernel,
        out_shape=out_shape,
        grid=(N // tm,),
        in_specs=[pl.BlockSpec((tm, D), row),
                  pl.BlockSpec((tm, 1), row),
                  pl.BlockSpec((1, LANES), lambda i: (0, 0)),
                  pl.BlockSpec((1, N_MOD, D), lambda i: (i // tiles_per_seq, 0, 0)),
                  pl.BlockSpec((1, D), lambda i: (0, 0)),
                  pl.BlockSpec((D, C), lambda i: (0, 0)),
                  pl.BlockSpec((1, C), lambda i: (0, 0))],
        out_specs=out_specs,
        compiler_params=_cparams(("arbitrary",)),
        name="inproj",
    )(x2, pos2, inv_lane, mod3, g_mix, w_ext, b_ext)


def _swa_kernel(sink_ref, q_ref, kc_ref, kp_ref, vc_ref, vp_ref, o_ref):
    i = pl.program_id(1)
    tq = q_ref.shape[0]
    lane = lax.broadcasted_iota(jnp.int32, (1, LANES), 1)
    lo = lane < HEAD_DIM
    qi = lax.broadcasted_iota(jnp.int32, (WINDOW, 2 * WINDOW), 0) + WINDOW
    kj = lax.broadcasted_iota(jnp.int32, (WINDOW, 2 * WINDOW), 1)
    band = (qi - kj >= 0) & (qi - kj < WINDOW)
    dn = (((1,), (1,)), ((), ()))
    for c in range(tq // WINDOW):
        if c == 0:
            kcat = jnp.concatenate([kp_ref[...], kc_ref[0:WINDOW, :]], axis=0)
            vcat = jnp.concatenate([vp_ref[...], vc_ref[0:WINDOW, :]], axis=0)
            mask = band & (kj >= jnp.where(i > 0, 0, WINDOW))
        else:
            kcat = kc_ref[(c - 1) * WINDOW:(c + 1) * WINDOW, :]
            vcat = vc_ref[(c - 1) * WINDOW:(c + 1) * WINDOW, :]
            mask = band
        for j in range(SWA_KV_HEADS):
            kj2 = kcat[:, j * LANES:(j + 1) * LANES]
            vj2 = vcat[:, j * LANES:(j + 1) * LANES]
            zero = jnp.zeros_like(kj2)
            k_halves = (jnp.where(lo, kj2, zero), jnp.where(lo, zero, kj2))
            v_halves = (jnp.where(lo, vj2, zero), jnp.where(lo, zero, vj2))
            for p in range(SWA_GROUP // 2):
                g = j * (SWA_GROUP // 2) + p
                q = q_ref[c * WINDOW:(c + 1) * WINDOW, g * LANES:(g + 1) * LANES]
                out = jnp.zeros((WINDOW, LANES), F32)
                for half in range(2):
                    sink = sink_ref[2 * g + half]
                    s = lax.dot_general(q, k_halves[half], dn, preferred_element_type=F32)
                    s = jnp.where(mask, s, NEG_INF)
                    m = jnp.maximum(jnp.max(s, axis=1, keepdims=True), sink)
                    e = jnp.exp(s - m)
                    denom = jnp.sum(e, axis=1, keepdims=True) + jnp.exp(sink - m)
                    pv = jnp.dot(e.astype(BF16), v_halves[half], preferred_element_type=F32)
                    out = out + pv / denom
                o_ref[c * WINDOW:(c + 1) * WINDOW, g * LANES:(g + 1) * LANES] = out.astype(o_ref.dtype)


def _swa(sinks, qa, ka2, va2, B, S):
    N = qa.shape[0]
    tq = TQ_SWA
    nq = S // tq
    wpt = tq // WINDOW
    wps = S // WINDOW
    cur = lambda b, i: (b * nq + i, 0)
    prev = lambda b, i: (b * wps + jnp.maximum(i * wpt - 1, 0), 0)
    return pl.pallas_call(
        _swa_kernel,
        out_shape=jax.ShapeDtypeStruct((N, qa.shape[1]), BF16),
        grid=(B, nq),
        in_specs=[pl.BlockSpec(memory_space=pltpu.SMEM),
                  pl.BlockSpec((tq, qa.shape[1]), cur),
                  pl.BlockSpec((tq, ka2.shape[1]), cur),
                  pl.BlockSpec((WINDOW, ka2.shape[1]), prev),
                  pl.BlockSpec((tq, va2.shape[1]), cur),
                  pl.BlockSpec((WINDOW, va2.shape[1]), prev)],
        out_specs=pl.BlockSpec((tq, qa.shape[1]), cur),
        compiler_params=_cparams(("arbitrary", "arbitrary")),
        name="swa",
    )(sinks, qa, ka2, ka2, va2, va2)


def _diff_kernel(lam_ref, g_ref, qt_ref, k_ref, vt_ref, o_ref, sa_ref, sb_ref, m_ref, acc_ref, *, lambda_init):
    i = pl.program_id(2)
    tq = qt_ref.shape[3]
    tk = vt_ref.shape[3]
    qt = qt_ref[0, 0]
    lane = lax.broadcasted_iota(jnp.int32, (1, LANES), 1)
    lo = lane < HEAD_DIM
    m_ref[...] = jnp.full(m_ref.shape, NEG_INF, F32)
    acc_ref[...] = jnp.zeros(acc_ref.shape, F32)

    def scores(c, s_ref):
        k = k_ref[pl.ds(pl.multiple_of(c * tk, tk), tk), :]
        zero = jnp.zeros_like(k)
        s_ref[0] = jnp.dot(jnp.where(lo, k, zero), qt, preferred_element_type=F32)
        s_ref[1] = jnp.dot(jnp.where(lo, zero, k), qt, preferred_element_type=F32)

    def consume(c, s_ref, first_key):
        vt = vt_ref[0, c]
        for mp in range(2):
            s = s_ref[mp]
            if first_key is not None:
                kpos = lax.broadcasted_iota(jnp.int32, (tk, tq), 0) + first_key
                qpos = lax.broadcasted_iota(jnp.int32, (tk, tq), 1)
                s = jnp.where(kpos <= qpos, s, NEG_INF)
            m_prev = m_ref[mp]
            m_new = jnp.maximum(m_prev, jnp.max(s, axis=0, keepdims=True))
            alpha = jnp.exp2(m_prev - m_new)
            p = jnp.exp2(s - m_new).astype(BF16)
            acc_ref[mp] = alpha * acc_ref[mp] + jnp.dot(vt, p, preferred_element_type=F32)
            m_ref[mp] = m_new

    scores(0, sa_ref)

    def pair(jj, carry):
        c = 2 * jj
        scores(c + 1, sb_ref)
        consume(c, sa_ref, None)
        scores(c + 2, sa_ref)
        consume(c + 1, sb_ref, None)
        return carry

    lax.fori_loop(0, lax.shift_right_logical(i, 1), pair, 0)

    @pl.when(i % 2 == 0)
    def _():
        consume(i, sa_ref, 0)

    @pl.when(i % 2 == 1)
    def _():
        scores(i, sb_ref)
        consume(i - 1, sa_ref, None)
        consume(i, sb_ref, 0)

    lq1, lk1, lq2, lk2 = (lam_ref[r:r + 1, :] for r in range(4))
    lam = (jnp.exp(jnp.sum(lq1 * lk1, axis=1, keepdims=True))
           - jnp.exp(jnp.sum(lq2 * lk2, axis=1, keepdims=True)) + lambda_init)
    d = DIFF_V_DIM
    ot = (acc_ref[0, 0:d, :] / acc_ref[0, d:d + 1, :]
          - lam * (acc_ref[1, 0:d, :] / acc_ref[1, d:d + 1, :]))
    ot = ot * lax.rsqrt(jnp.mean(ot * ot, axis=0, keepdims=True) + EPS)
    o_ref[...] = (ot.T * g_ref[...] * (1.0 - lambda_init)).astype(o_ref.dtype)


def _diffattn(lam_vecs, g_subln, qdt, kd, vdt, B, S, lambda_init):
    N, C = kd.shape
    tq = qdt.shape[3]
    nq = S // tq
    return pl.pallas_call(
        functools.partial(_diff_kernel, lambda_init=lambda_init),
        out_shape=jax.ShapeDtypeStruct((N, C), BF16),
        grid=(B, DIFF_HEADS, nq),
        in_specs=[pl.BlockSpec((4, HEAD_DIM), lambda b, h, i: (0, 0)),
                  pl.BlockSpec((1, DIFF_V_DIM), lambda b, h, i: (0, 0)),
                  pl.BlockSpec((1, 1, LANES, tq), lambda b, h, i: (b, i, h, 0)),
                  pl.BlockSpec((S, LANES), lambda b, h, i: (b, h)),
                  pl.BlockSpec((1, nq, VT_ROWS, tq), lambda b, h, i: (b, 0, h, 0))],
        out_specs=pl.BlockSpec((tq, LANES), lambda b, h, i: (b * nq + i, h)),
        scratch_shapes=[pltpu.VMEM((2, tq, tq), F32), pltpu.VMEM((2, tq, tq), F32),
                        pltpu.VMEM((2, 1, tq), F32), pltpu.VMEM((2, VT_ROWS, tq), F32)],
        compiler_params=_cparams(("arbitrary", "arbitrary", "arbitrary")),
        name="diffattn",
    )(lam_vecs, g_subln, qdt, kd, vdt)


def _outproj_kernel(oa_ref, ob_ref, x_ref, mod_ref, wo_ref, bo_ref, g_ref, wr_ref, br_ref,
                    x1_ref, h2_ref, idx_ref, gate_ref, rank_ref, cnt_ref, carry_ref):
    i = pl.program_id(0)
    tm = x_ref.shape[0]
    half = oa_ref.shape[1]

    @pl.when(i == 0)
    def _():
        carry_ref[...] = jnp.zeros(carry_ref.shape, F32)

    gt1 = mod_ref[0, 2:3, :]
    sh2 = mod_ref[0, 3:4, :]
    sc2 = mod_ref[0, 4:5, :]
    mixed = (jnp.dot(oa_ref[...], wo_ref[0:half, :], preferred_element_type=F32)
             + jnp.dot(ob_ref[...], wo_ref[half:, :], preferred_element_type=F32) + bo_ref[...])
    x1 = x_ref[...] + gt1 * mixed
    x1_ref[...] = x1
    h2 = _rms(x1) * g_ref[...] * (1.0 + sc2) + sh2
    dn = (((1,), (1,)), ((), ()))
    h_hi = h2.astype(BF16)
    h2_ref[...] = _pack_bf16_pairs(h2)
    h_lo = (h2 - h_hi.astype(F32)).astype(BF16)
    w = wr_ref[...]
    w_hi = w.astype(BF16)
    w_lo = (w - w_hi.astype(F32)).astype(BF16)
    logits = (lax.dot_general(w_hi, h_hi, dn, preferred_element_type=F32)
              + lax.dot_general(w_hi, h_lo, dn, preferred_element_type=F32)
              + lax.dot_general(w_lo, h_hi, dn, preferred_element_type=F32)
              + br_ref[...])

    eidx = lax.broadcasted_iota(jnp.int32, logits.shape, 0)
    vals = logits
    onehots, top_vals, top_idx = [], [], []
    for _k in range(TOP_K):
        mx = jnp.max(vals, axis=0, keepdims=True)
        sel = jnp.min(jnp.where(vals == mx, eidx, N_EXPERTS), axis=0, keepdims=True)
        oh = eidx == sel
        onehots.append(oh)
        top_vals.append(mx)
        top_idx.append(sel)
        vals = jnp.where(oh, NEG_INF, vals)
    exps = [jnp.exp(v - top_vals[0]) for v in top_vals]
    denom = exps[0] + exps[1] + exps[2] + exps[3]
    gate_ref[...] = jnp.concatenate([e / denom for e in exps], axis=0)
    idx_ref[...] = jnp.concatenate(top_idx, axis=0)

    member = (onehots[0] | onehots[1] | onehots[2] | onehots[3])
    member_f = member.astype(F32)
    t_src = lax.broadcasted_iota(jnp.int32, (tm, tm), 0)
    t_dst = lax.broadcasted_iota(jnp.int32, (tm, tm), 1)
    before = (t_src < t_dst).astype(BF16)
    prefix = jnp.dot(member.astype(BF16), before, preferred_element_type=F32) + carry_ref[...]
    ranks = [jnp.sum(jnp.where(oh, prefix, 0.0), axis=0, keepdims=True) for oh in onehots]
    rank_ref[...] = jnp.concatenate(ranks, axis=0).astype(jnp.int32)
    carry_ref[...] = carry_ref[...] + jnp.sum(member_f, axis=1, keepdims=True)
    cnt_ref[...] = jnp.broadcast_to(carry_ref[...], cnt_ref.shape)


def _outproj(out_a, out_b, x2, mod3, w_out, b_out, g_ffn, wr_t, br_col, S):
    N, D = x2.shape
    tm = TM_OUT
    tiles_per_seq = S // tm
    row = lambda i: (i, 0)
    colb = lambda i: (0, i)
    const = lambda i: (0, 0)
    return pl.pallas_call(
        _outproj_kernel,
        out_shape=[jax.ShapeDtypeStruct((N, D), F32), jax.ShapeDtypeStruct((N, D // 2), jnp.int32),
                   jax.ShapeDtypeStruct((TOP_K, N), jnp.int32), jax.ShapeDtypeStruct((TOP_K, N), F32),
                   jax.ShapeDtypeStruct((TOP_K, N), jnp.int32), jax.ShapeDtypeStruct((N_EXPERTS, LANES), F32)],
        grid=(N // tm,),
        in_specs=[pl.BlockSpec((tm, out_a.shape[1]), row),
                  pl.BlockSpec((tm, out_b.shape[1]), row),
                  pl.BlockSpec((tm, D), row),
                  pl.BlockSpec((1, N_MOD, D), lambda i: (i // tiles_per_seq, 0, 0)),
                  pl.BlockSpec(w_out.shape, const),
                  pl.BlockSpec((1, D), const),
                  pl.BlockSpec((1, D), const),
                  pl.BlockSpec(wr_t.shape, const),
                  pl.BlockSpec((N_EXPERTS, 1), const)],
        out_specs=[pl.BlockSpec((tm, D), row), pl.BlockSpec((tm, D // 2), row),
                   pl.BlockSpec((TOP_K, tm), colb), pl.BlockSpec((TOP_K, tm), colb),
                   pl.BlockSpec((TOP_K, tm), colb), pl.BlockSpec((N_EXPERTS, LANES), const)],
        scratch_shapes=[pltpu.VMEM((N_EXPERTS, 1), F32)],
        compiler_params=_cparams(("arbitrary",)),
        name="outproj_router",
    )(out_a, out_b, x2, mod3, w_out, b_out, g_ffn, wr_t, br_col)


def _experts_kernel(first_ref, nblk_ref, x_ref, w1_ref, b1g_ref, b1l_ref, w2_ref, b2_ref, y_ref,
                    w1g_s, w1l_s, w2_s, xbuf, ybuf, xsem, ysem):
    e = pl.program_id(0)
    first = first_ref[e]
    n_blk = nblk_ref[e]
    bm = xbuf.shape[1]

    def x_copy(j, slot, first_blk=None):
        blk = (first if first_blk is None else first_blk) + j
        rows = pl.ds(pl.multiple_of(blk * bm, bm), bm)
        return pltpu.make_async_copy(x_ref.at[rows], xbuf.at[slot], xsem.at[slot])

    def y_copy(j, slot):
        rows = pl.ds(pl.multiple_of((first + j) * bm, bm), bm)
        return pltpu.make_async_copy(ybuf.at[slot], y_ref.at[rows], ysem.at[slot])

    def swiglu(ug, ul):
        glu = jnp.minimum(ug, SWIGLU_LIMIT)
        lin = jnp.clip(ul, -SWIGLU_LIMIT, SWIGLU_LIMIT)
        return (glu * jax.nn.sigmoid(SWIGLU_ALPHA * glu) * (lin + 1.0)).astype(BF16)

    @pl.when((n_blk > 0) & (e == 0))
    def _():
        x_copy(0, 0).start(priority=1)

    @pl.when(n_blk > 0)
    def _():
        ch = 256
        half = ch // 2
        for c in range(w1_ref.shape[2] // ch):
            t = w1_ref[0, :, c * ch:(c + 1) * ch].astype(BF16).T
            pairs = pltpu.bitcast(t, jnp.int32)
            cols = slice(c * half, (c + 1) * half)
            w1g_s[:, cols] = lax.bitcast_convert_type(lax.shift_left(pairs, 16), F32).astype(BF16).T
            w1l_s[:, cols] = lax.bitcast_convert_type(pairs & jnp.int32(HIGH_HALF), F32).astype(BF16).T
        w2_s[...] = w2_ref[0].astype(BF16)

    def block(j, carry):
        slot = j & 1
        x_copy(j, slot).wait()

        @pl.when(j + 1 < n_blk)
        def _():
            x_copy(j + 1, 1 - slot).start(priority=1)

        @pl.when(j >= 2)
        def _():
            y_copy(j - 2, slot).wait()

        x = _unpack_bf16_pairs(xbuf[slot]).astype(BF16)
        ug = jnp.dot(x, w1g_s[...], preferred_element_type=F32) + b1g_ref[0]
        ul = jnp.dot(x, w1l_s[...], preferred_element_type=F32) + b1l_ref[0]
        y = jnp.dot(swiglu(ug, ul), w2_s[...], preferred_element_type=F32) + b2_ref[0]
        ybuf[slot] = _pack_bf16_pairs(y)
        y_copy(j, slot).start(priority=1)
        return carry

    lax.fori_loop(0, n_blk, block, 0)

    nxt = jnp.minimum(e + 1, pl.num_programs(0) - 1)

    @pl.when((e + 1 < pl.num_programs(0)) & (nblk_ref[nxt] > 0))
    def _():
        x_copy(0, 0, first_ref[nxt]).start(priority=1)

    @pl.when(n_blk >= 2)
    def _():
        y_copy(n_blk - 2, n_blk & 1).wait()

    @pl.when(n_blk >= 1)
    def _():
        y_copy(n_blk - 1, (n_blk - 1) & 1).wait()


def _experts(first_blk, n_blk, x_rows, w1, b1g, b1l, w2, b2):
    E, Fh, D = w2.shape
    bm = BLOCK_ROWS
    n_rows = x_rows.shape[0]
    wsel = lambda e, fb, nb: (e, 0, 0)
    return pl.pallas_call(
        _experts_kernel,
        out_shape=jax.ShapeDtypeStruct((n_rows, D // 2), jnp.int32),
        grid_spec=pltpu.PrefetchScalarGridSpec(
            num_scalar_prefetch=2,
            grid=(E,),
            in_specs=[pl.BlockSpec(memory_space=pl.ANY),
                      pl.BlockSpec((1, D, 2 * Fh), wsel),
                      pl.BlockSpec((1, 1, Fh), wsel),
                      pl.BlockSpec((1, 1, Fh), wsel),
                      pl.BlockSpec((1, Fh, D), wsel),
                      pl.BlockSpec((1, 1, D), wsel)],
            out_specs=pl.BlockSpec(memory_space=pl.ANY),
            scratch_shapes=[pltpu.VMEM((D, Fh), BF16), pltpu.VMEM((D, Fh), BF16), pltpu.VMEM((Fh, D), BF16),
                            pltpu.VMEM((2, bm, D // 2), jnp.int32), pltpu.VMEM((2, bm, D // 2), jnp.int32),
                            pltpu.SemaphoreType.DMA((2,)), pltpu.SemaphoreType.DMA((2,))]),
        compiler_params=_cparams(("arbitrary",), 56 * 1024 * 1024),
        name="experts",
    )(first_blk, n_blk, x_rows, w1, b1g, b1l, w2, b2)


SC_CORES = 2
SC_SUBCORES = 16
SC_CHUNK = 64


def _sc_gather_rows(table, idx):
    M = idx.shape[0]
    D = table.shape[1]
    workers = SC_CORES * SC_SUBCORES
    per_worker = M // workers
    n_chunks = per_worker // SC_CHUNK
    assert M % workers == 0 and per_worker % (2 * SC_CHUNK) == 0
    mesh = plsc.VectorSubcoreMesh(core_axis_name="c", subcore_axis_name="s")

    @functools.partial(
        pl.kernel, mesh=mesh,
        out_type=jax.ShapeDtypeStruct((M, D), table.dtype),
        scratch_types=[pltpu.VMEM((per_worker,), jnp.int32),
                       pltpu.VMEM((SC_CHUNK, D), table.dtype), pltpu.VMEM((SC_CHUNK, D), table.dtype),
                       pltpu.SemaphoreType.DMA, pltpu.SemaphoreType.DMA],
        name="sc_gather_rows")
    def gather(table_hbm, idx_hbm, out_hbm, idx_v, rows0, rows1, sem0, sem1):
        wid = lax.axis_index("s") * SC_CORES + lax.axis_index("c")
        base = wid * per_worker
        pltpu.sync_copy(idx_hbm.at[pl.ds(base, per_worker)], idx_v)

        def fetch(c, buf, sem):
            off = pl.multiple_of(c * SC_CHUNK, SC_CHUNK)
            return pltpu.make_async_copy(table_hbm.at[idx_v.at[pl.ds(off, SC_CHUNK)]], buf, sem)

        def flush(c, buf):
            off = pl.multiple_of(c * SC_CHUNK, SC_CHUNK)
            pltpu.sync_copy(buf, out_hbm.at[pl.ds(base + off, SC_CHUNK)])

        fetch(0, rows0, sem0).start()

        @pl.loop(0, n_chunks // 2)
        def _(jj):
            c = 2 * jj
            fetch(c + 1, rows1, sem1).start()
            fetch(c, rows0, sem0).wait()
            flush(c, rows0)

            @pl.when(c + 2 < n_chunks)
            def _():
                fetch(c + 2, rows0, sem0).start()

            fetch(c + 1, rows1, sem1).wait()
            flush(c + 1, rows1)

    return gather(table, idx)


SC_SCATTER_CHUNK = 128


def _sc_scatter_rows(rows, dest, n_rows):
    N, W = rows.shape
    workers = SC_CORES * SC_SUBCORES
    per_worker = N // workers
    chunks = per_worker // SC_SCATTER_CHUNK
    assert N % workers == 0 and per_worker % SC_SCATTER_CHUNK == 0
    dest3 = dest.reshape(TOP_K, N // SC_SCATTER_CHUNK, SC_SCATTER_CHUNK)
    mesh = plsc.VectorSubcoreMesh(core_axis_name="c", subcore_axis_name="s")

    @functools.partial(
        pl.kernel, mesh=mesh,
        out_type=jax.ShapeDtypeStruct((n_rows, W), rows.dtype),
        scratch_types=[pltpu.VMEM((TOP_K, chunks, SC_SCATTER_CHUNK), jnp.int32),
                       pltpu.VMEM((SC_SCATTER_CHUNK, W), rows.dtype)],
        name="sc_scatter_rows")
    def scatter(rows_hbm, dest_hbm, out_hbm, idx_v, rows_v):
        wid = lax.axis_index("s") * SC_CORES + lax.axis_index("c")
        for k in range(TOP_K):
            pltpu.sync_copy(dest_hbm.at[k, pl.ds(wid * chunks, chunks)], idx_v.at[k])

        @pl.loop(0, chunks)
        def _(j):
            start = pl.multiple_of(wid * per_worker + j * SC_SCATTER_CHUNK, SC_SCATTER_CHUNK)
            pltpu.sync_copy(rows_hbm.at[pl.ds(start, SC_SCATTER_CHUNK)], rows_v)
            for k in range(TOP_K):
                pltpu.sync_copy(rows_v, out_hbm.at[idx_v.at[k, j]])

    return scatter(rows, dest3)


def _combine_kernel(x1_ref, gate_ref, mod_ref, g_ref, y_ref, o_ref, *, final_norm):
    gate = gate_ref[...]
    moe = gate[:, 0:1] * _unpack_bf16_pairs(y_ref[0])
    for k in range(1, TOP_K):
        moe = moe + gate[:, k:k + 1] * _unpack_bf16_pairs(y_ref[k])
    gt2 = mod_ref[0, 5:6, :]
    x2 = x1_ref[...] + gt2 * moe
    o_ref[...] = _rms(x2) * g_ref[...] if final_norm else x2


def _combine(x1, gate_t, mod3, g_final, y_tok, S, final_norm):
    N, D = x1.shape
    tm = TM_ROWS
    tiles_per_seq = S // tm
    row = lambda i: (i, 0)
    return pl.pallas_call(
        functools.partial(_combine_kernel, final_norm=final_norm),
        out_shape=jax.ShapeDtypeStruct((N, D), F32),
        grid=(N // tm,),
        in_specs=[pl.BlockSpec((tm, D), row),
                  pl.BlockSpec((tm, TOP_K), row),
                  pl.BlockSpec((1, N_MOD, D), lambda i: (i // tiles_per_seq, 0, 0)),
                  pl.BlockSpec((1, D), lambda i: (0, 0)),
                  pl.BlockSpec((TOP_K, tm, D // 2), lambda i: (0, i, 0))],
        out_specs=pl.BlockSpec((tm, D), row),
        compiler_params=_cparams(("arbitrary",)),
        name="combine",
    )(x1, gate_t, mod3, g_final, y_tok)


def _routing_tables(counts, idx, rank):
    bm = BLOCK_ROWS
    counts = counts.astype(jnp.int32)
    padded = (counts + bm - 1) // bm * bm
    pends = jnp.cumsum(padded)
    pstarts = pends - padded
    experts = jnp.arange(N_EXPERTS, dtype=jnp.int32)
    dest = jnp.sum(jnp.where(idx[..., None] == experts, pstarts, 0), axis=-1) + rank
    return dest.astype(jnp.int32), (pstarts // bm).astype(jnp.int32), (padded // bm).astype(jnp.int32)


def _extended_in_weights(w_in, b_in):
    a_q = SWA_Q_HEADS * HEAD_DIM
    a_kv = SWA_KV_HEADS * HEAD_DIM
    b_w = DIFF_HEADS * DIFF_V_DIM
    spans = [(0, a_q)]
    for base in (a_q, a_q + a_kv):
        for j in range(SWA_KV_HEADS):
            spans += [(base + j * HEAD_DIM, base + (j + 1) * HEAD_DIM)] * 2
    spans.append((a_q + 2 * a_kv, a_q + 2 * a_kv + 3 * b_w))
    w_ext = jnp.concatenate([w_in[:, lo:hi] for lo, hi in spans], axis=1).astype(BF16)
    b_ext = jnp.concatenate([b_in[lo:hi] for lo, hi in spans]).reshape(1, -1)
    widths = (a_q, 2 * a_kv, 2 * a_kv, b_w, b_w, b_w)
    return w_ext, b_ext, widths


def kernel(x, c, positions, w_ada, b_ada, g_mix, w_in, b_in, attn_sinks, lambda_q1, lambda_k1, lambda_q2,
           lambda_k2, g_subln, w_out, b_out, g_ffn, w_router, b_router, w1, b1, w2, b2, g_final):
    B, S, D = x.shape
    N = B * S
    depth = w_ada.shape[0]
    n_rows = (N * TOP_K + N_EXPERTS * (BLOCK_ROWS - 1) + BLOCK_ROWS - 1) // BLOCK_ROWS * BLOCK_ROWS

    inv = 1.0 / (ROPE_THETA ** (jnp.arange(0, HEAD_DIM, 2, dtype=F32) / HEAD_DIM))
    inv_lane = jnp.tile(inv, LANES // (HEAD_DIM // 2)).reshape(1, LANES)
    pos2 = positions.reshape(N, 1)
    xcur = x.reshape(N, D)

    for layer in range(depth):
        last = layer == depth - 1
        lambda_init = 0.8 - 0.6 * math.exp(-0.3 * layer)
        mod3 = _adaln(c, w_ada[layer], b_ada[layer]).reshape(B, N_MOD, D)

        w_ext, b_ext, widths = _extended_in_weights(w_in[layer], b_in[layer])
        qa, ka2, va2, qdt, kd, vdt = _inproj(xcur, pos2, inv_lane, mod3, g_mix[layer].reshape(1, D),
                                             w_ext, b_ext, S, widths)
        out_a = _swa(attn_sinks[layer], qa, ka2, va2, B, S)
        lam_vecs = jnp.stack([lambda_q1[layer], lambda_k1[layer], lambda_q2[layer], lambda_k2[layer]])
        out_b = _diffattn(lam_vecs, g_subln[layer].reshape(1, DIFF_V_DIM), qdt, kd, vdt, B, S, lambda_init)

        x1, h2, idx, gate, rank, counts = _outproj(
            out_a, out_b, xcur, mod3, w_out[layer].astype(BF16), b_out[layer].reshape(1, D),
            g_ffn[layer].reshape(1, D), w_router[layer].T, b_router[layer].reshape(N_EXPERTS, 1), S)

        dest, first_blk, n_blk = _routing_tables(counts[:, 0], idx, rank)
        x_rows = _sc_scatter_rows(h2, dest, n_rows)
        y_rows = _experts(first_blk, n_blk, x_rows, w1[layer],
                          b1[layer][:, None, 0::2], b1[layer][:, None, 1::2],
                          w2[layer], b2[layer][:, None, :])
        y_tok = _sc_gather_rows(y_rows, dest.reshape(-1)).reshape(TOP_K, N, D // 2)
        xcur = _combine(x1, gate.T, mod3, g_final.reshape(1, D), y_tok, S, final_norm=last)
    return xcur.reshape(B, S, D)
```

```python
import functools
import math

import jax
import jax.numpy as jnp
from jax import lax
from jax.experimental import pallas as pl
from jax.experimental.pallas import tpu as pltpu
from jax.experimental.pallas import tpu_sc as plsc

HEAD_DIM = 64
SWA_Q_HEADS = 8
SWA_KV_HEADS = 2
SWA_GROUP = SWA_Q_HEADS // SWA_KV_HEADS
WINDOW = 128
DIFF_HEADS = 4
DIFF_V_DIM = 2 * HEAD_DIM
ROPE_THETA = 10000.0
N_EXPERTS = 32
TOP_K = 4
SWIGLU_ALPHA = 1.702
SWIGLU_LIMIT = 7.0
EPS = 1e-5
N_MOD = 6

LANES = 128
F32 = jnp.float32
BF16 = jnp.bfloat16
NEG_INF = float("-inf")

TM_PROJ = 1024
TQ_SWA = 512
VT_ROWS = DIFF_V_DIM + 16
TM_OUT = 512
TM_ROWS = 256
BLOCK_ROWS = 512
VMEM_LIMIT = 48 * 1024 * 1024


def _cparams(sem, vmem=VMEM_LIMIT):
    return pltpu.CompilerParams(dimension_semantics=sem, vmem_limit_bytes=vmem)


def _adaln_kernel(ct_ref, w_ref, b_ref, o_ref):
    c = ct_ref[...]
    cond = c * jax.nn.sigmoid(c)
    w = w_ref[...]
    rows = [jnp.sum(w * cond[:, b:b + 1], axis=0, keepdims=True) for b in range(c.shape[1])]
    o_ref[...] = jnp.concatenate(rows, axis=0) + b_ref[...]


def _adaln(c, w_ada, b_ada):
    B, D = c.shape
    n_out = w_ada.shape[1]
    tn = 1024
    return pl.pallas_call(
        _adaln_kernel,
        out_shape=jax.ShapeDtypeStruct((B, n_out), F32),
        grid=(n_out // tn,),
        in_specs=[pl.BlockSpec((D, B), lambda j: (0, 0)),
                  pl.BlockSpec((D, tn), lambda j: (0, j)),
                  pl.BlockSpec((1, tn), lambda j: (0, j))],
        out_specs=pl.BlockSpec((B, tn), lambda j: (0, j)),
        compiler_params=_cparams(("arbitrary",)),
        name="adaln",
    )(c.T, w_ada, b_ada.reshape(1, n_out))


def _rms(x):
    return x * lax.rsqrt(jnp.mean(x * x, axis=-1, keepdims=True) + EPS)


HIGH_HALF = -65536


def _pack_bf16_pairs(v):
    bits = lax.bitcast_convert_type(v.astype(BF16).astype(F32), jnp.int32)
    half = v.shape[1] // 2
    return lax.shift_right_logical(bits[:, :half], 16) | (bits[:, half:] & jnp.int32(HIGH_HALF))


def _unpack_bf16_pairs(w):
    return jnp.concatenate([lax.bitcast_convert_type(lax.shift_left(w, 16), F32),
                            lax.bitcast_convert_type(w & jnp.int32(HIGH_HALF), F32)], axis=1)


def _inproj_kernel(x_ref, pos_ref, inv_ref, mod_ref, g_ref, w_ref, b_ref,
                   qa_ref, ka_ref, va_ref, qd_ref, kd_ref, vd_ref):
    x = x_ref[...]
    sh = mod_ref[0, 0:1, :]
    sc = mod_ref[0, 1:2, :]
    h = _rms(x) * g_ref[...] * (1.0 + sc) + sh
    proj = jnp.dot(h.astype(BF16), w_ref[...], preferred_element_type=F32) + b_ref[...]

    lane = lax.broadcasted_iota(jnp.int32, (1, LANES), 1)
    first_half = (lane & (HEAD_DIM - 1)) < (HEAD_DIM // 2)
    n_freq = HEAD_DIM // 2
    groups = LANES // n_freq
    tm = x.shape[0]
    rows = tm // groups
    group = lax.shift_right_logical(lane, n_freq.bit_length() - 1)
    pos = pos_ref[...].astype(F32)
    pos_q = pos[0:rows]
    for g in range(1, groups):
        pos_q = jnp.where(group == g, pos[g * rows:(g + 1) * rows], pos_q)
    ang_q = pos_q * inv_ref[...]

    def spread(table_q):
        blocks = []
        for g in range(groups):
            only = jnp.where(group == g, table_q, 0.0)
            full = only
            for r in range(1, groups):
                full = full + pltpu.roll(only, r * n_freq, axis=1)
            blocks.append(full)
        return jnp.concatenate(blocks, axis=0)

    cos = spread(jnp.cos(ang_q))
    sin = spread(jnp.sin(ang_q))
    sin_signed = jnp.where(first_half, -sin, sin)

    def rope(t):
        partner = jnp.where(first_half,
                            pltpu.roll(t, LANES - HEAD_DIM // 2, axis=1),
                            pltpu.roll(t, HEAD_DIM // 2, axis=1))
        return t * cos + partner * sin_signed

    def emit(out_ref, col0, width, rotary, scale, transposed):
        for j in range(width // LANES):
            t = proj[:, col0 + j * LANES: col0 + (j + 1) * LANES]
            if rotary:
                t = rope(t)
            if scale != 1.0:
                t = t * scale
            if transposed:
                rows = out_ref.shape[2] // (width // LANES)
                out_ref[0, 0, j * rows:j * rows + LANES, :] = t.T.astype(out_ref.dtype)
                if rows > LANES:
                    fill = lax.broadcasted_iota(jnp.int32, (rows - LANES, t.shape[0]), 0) == 0
                    out_ref[0, 0, j * rows + LANES:(j + 1) * rows, :] = fill.astype(out_ref.dtype)
            else:
                out_ref[:, j * LANES:(j + 1) * LANES] = t.astype(out_ref.dtype)

    swa_scale = 1.0 / math.sqrt(HEAD_DIM)
    diff_scale = math.log2(math.e) / math.sqrt(HEAD_DIM)
    col = 0
    for out_ref, width, rotary, scale, transposed in (
            (qa_ref, qa_ref.shape[1], True, swa_scale, False), (ka_ref, ka_ref.shape[1], True, 1.0, False),
            (va_ref, va_ref.shape[1], False, 1.0, False), (qd_ref, qd_ref.shape[2], True, diff_scale, True),
            (kd_ref, kd_ref.shape[1], True, 1.0, False),
            (vd_ref, vd_ref.shape[2] // VT_ROWS * LANES, False, 1.0, True)):
        emit(out_ref, col, width, rotary, scale, transposed)
        col += width


def _inproj(x2, pos2, inv_lane, mod3, g_mix, w_ext, b_ext, S, widths):
    N, D = x2.shape
    tm = TM_PROJ
    C = w_ext.shape[1]
    tiles_per_seq = S // tm
    row = lambda i: (i, 0)
    t_rows = (0, 0, 0, widths[3], 0, widths[5] // LANES * VT_ROWS)
    out_shape, out_specs = [], []
    for w, tr in zip(widths, t_rows):
        if tr:
            out_shape.append(jax.ShapeDtypeStruct((N // S, tiles_per_seq, tr, tm), BF16))
            out_specs.append(pl.BlockSpec((1, 1, tr, tm), lambda i: (i // tiles_per_seq, i % tiles_per_seq, 0, 0)))
        else:
            out_shape.append(jax.ShapeDtypeStruct((N, w), BF16))
            out_specs.append(pl.BlockSpec((tm, w), row))
    return pl.pallas_call(
        _inproj_kernel,
        out_shape=out_shape,
        grid=(N // tm,),
        in_specs=[pl.BlockSpec((tm, D), row),
                  pl.BlockSpec((tm, 1), row),
                  pl.BlockSpec((1, LANES), lambda i: (0, 0)),
                  pl.BlockSpec((1, N_MOD, D), lambda i: (i // tiles_per_seq, 0, 0)),
                  pl.BlockSpec((1, D), lambda i: (0, 0)),
                  pl.BlockSpec((D, C), lambda i: (0, 0)),
                  pl.BlockSpec((1, C), lambda i: (0, 0))],
        out_specs=out_specs,
        compiler_params=_cparams(("arbitrary",)),
        name="inproj",
    )(x2, pos2, inv_lane, mod3, g_mix, w_ext, b_ext)


def _swa_kernel(sink_ref, q_ref, kc_ref, kp_ref, vc_ref, vp_ref, o_ref):
    i = pl.program_id(1)
    tq = q_ref.shape[0]
    lane = lax.broadcasted_iota(jnp.int32, (1, LANES), 1)
    lo = lane < HEAD_DIM
    qi = lax.broadcasted_iota(jnp.int32, (WINDOW, 2 * WINDOW), 0) + WINDOW
    kj = lax.broadcasted_iota(jnp.int32, (WINDOW, 2 * WINDOW), 1)
    band = (qi - kj >= 0) & (qi - kj < WINDOW)
    dn = (((1,), (1,)), ((), ()))
    for c in range(tq // WINDOW):
        if c == 0:
            kcat = jnp.concatenate([kp_ref[...], kc_ref[0:WINDOW, :]], axis=0)
            vcat = jnp.concatenate([vp_ref[...], vc_ref[0:WINDOW, :]], axis=0)
            mask = band & (kj >= jnp.where(i > 0, 0, WINDOW))
        else:
            kcat = kc_ref[(c - 1) * WINDOW:(c + 1) * WINDOW, :]
            vcat = vc_ref[(c - 1) * WINDOW:(c + 1) * WINDOW, :]
            mask = band
        for j in range(SWA_KV_HEADS):
            kj2 = kcat[:, j * LANES:(j + 1) * LANES]
            vj2 = vcat[:, j * LANES:(j + 1) * LANES]
            zero = jnp.zeros_like(kj2)
            k_halves = (jnp.where(lo, kj2, zero), jnp.where(lo, zero, kj2))
            v_halves = (jnp.where(lo, vj2, zero), jnp.where(lo, zero, vj2))
            for p in range(SWA_GROUP // 2):
                g = j * (SWA_GROUP // 2) + p
                q = q_ref[c * WINDOW:(c + 1) * WINDOW, g * LANES:(g + 1) * LANES]
                out = jnp.zeros((WINDOW, LANES), F32)
                for half in range(2):
                    sink = sink_ref[2 * g + half]
                    s = lax.dot_general(q, k_halves[half], dn, preferred_element_type=F32)
                    s = jnp.where(mask, s, NEG_INF)
                    m = jnp.maximum(jnp.max(s, axis=1, keepdims=True), sink)
                    e = jnp.exp(s - m)
                    denom = jnp.sum(e, axis=1, keepdims=True) + jnp.exp(sink - m)
                    pv = jnp.dot(e.astype(BF16), v_halves[half], preferred_element_type=F32)
                    out = out + pv / denom
                o_ref[c * WINDOW:(c + 1) * WINDOW, g * LANES:(g + 1) * LANES] = out.astype(o_ref.dtype)


def _swa(sinks, qa, ka2, va2, B, S):
    N = qa.shape[0]
    tq = TQ_SWA
    nq = S // tq
    wpt = tq // WINDOW
    wps = S // WINDOW
    cur = lambda b, i: (b * nq + i, 0)
    prev = lambda b, i: (b * wps + jnp.maximum(i * wpt - 1, 0), 0)
    return pl.pallas_call(
        _swa_kernel,
        out_shape=jax.ShapeDtypeStruct((N, qa.shape[1]), BF16),
        grid=(B, nq),
        in_specs=[pl.BlockSpec(memory_space=pltpu.SMEM),
                  pl.BlockSpec((tq, qa.shape[1]), cur),
                  pl.BlockSpec((tq, ka2.shape[1]), cur),
                  pl.BlockSpec((WINDOW, ka2.shape[1]), prev),
                  pl.BlockSpec((tq, va2.shape[1]), cur),
                  pl.BlockSpec((WINDOW, va2.shape[1]), prev)],
        out_specs=pl.BlockSpec((tq, qa.shape[1]), cur),
        compiler_params=_cparams(("arbitrary", "arbitrary")),
        name="swa",
    )(sinks, qa, ka2, ka2, va2, va2)


def _diff_kernel(lam_ref, g_ref, qt_ref, k_ref, vt_ref, o_ref, sa_ref, sb_ref, m_ref, acc_ref, *, lambda_init):
    i = pl.program_id(2)
    tq = qt_ref.shape[3]
    tk = vt_ref.shape[3]
    qt = qt_ref[0, 0]
    lane = lax.broadcasted_iota(jnp.int32, (1, LANES), 1)
    lo = lane < HEAD_DIM
    m_ref[...] = jnp.full(m_ref.shape, NEG_INF, F32)
    acc_ref[...] = jnp.zeros(acc_ref.shape, F32)

    def scores(c, s_ref):
        k = k_ref[pl.ds(pl.multiple_of(c * tk, tk), tk), :]
        zero = jnp.zeros_like(k)
        s_ref[0] = jnp.dot(jnp.where(lo, k, zero), qt, preferred_element_type=F32)
        s_ref[1] = jnp.dot(jnp.where(lo, zero, k), qt, preferred_element_type=F32)

    def consume(c, s_ref, first_key):
        vt = vt_ref[0, c]
        for mp in range(2):
            s = s_ref[mp]
            if first_key is not None:
                kpos = lax.broadcasted_iota(jnp.int32, (tk, tq), 0) + first_key
                qpos = lax.broadcasted_iota(jnp.int32, (tk, tq), 1)
                s = jnp.where(kpos <= qpos, s, NEG_INF)
            m_prev = m_ref[mp]
            m_new = jnp.maximum(m_prev, jnp.max(s, axis=0, keepdims=True))
            alpha = jnp.exp2(m_prev - m_new)
            p = jnp.exp2(s - m_new).astype(BF16)
            acc_ref[mp] = alpha * acc_ref[mp] + jnp.dot(vt, p, preferred_element_type=F32)
            m_ref[mp] = m_new

    scores(0, sa_ref)

    def pair(jj, carry):
        c = 2 * jj
        scores(c + 1, sb_ref)
        consume(c, sa_ref, None)
        scores(c + 2, sa_ref)
        consume(c + 1, sb_ref, None)
        return carry

    lax.fori_loop(0, lax.shift_right_logical(i, 1), pair, 0)

    @pl.when(i % 2 == 0)
    def _():
        consume(i, sa_ref, 0)

    @pl.when(i % 2 == 1)
    def _():
        scores(i, sb_ref)
        consume(i - 1, sa_ref, None)
        consume(i, sb_ref, 0)

    lq1, lk1, lq2, lk2 = (lam_ref[r:r + 1, :] for r in range(4))
    lam = (jnp.exp(jnp.sum(lq1 * lk1, axis=1, keepdims=True))
           - jnp.exp(jnp.sum(lq2 * lk2, axis=1, keepdims=True)) + lambda_init)
    d = DIFF_V_DIM
    ot = (acc_ref[0, 0:d, :] / acc_ref[0, d:d + 1, :]
          - lam * (acc_ref[1, 0:d, :] / acc_ref[1, d:d + 1, :]))
    ot = ot * lax.rsqrt(jnp.mean(ot * ot, axis=0, keepdims=True) + EPS)
    o_ref[...] = (ot.T * g_ref[...] * (1.0 - lambda_init)).astype(o_ref.dtype)


def _diffattn(lam_vecs, g_subln, qdt, kd, vdt, B, S, lambda_init):
    N, C = kd.shape
    tq = qdt.shape[3]
    nq = S // tq
    return pl.pallas_call(
        functools.partial(_diff_kernel, lambda_init=lambda_init),
        out_shape=jax.ShapeDtypeStruct((N, C), BF16),
        grid=(B, DIFF_HEADS, nq),
        in_specs=[pl.BlockSpec((4, HEAD_DIM), lambda b, h, i: (0, 0)),
                  pl.BlockSpec((1, DIFF_V_DIM), lambda b, h, i: (0, 0)),
                  pl.BlockSpec((1, 1, LANES, tq), lambda b, h, i: (b, i, h, 0)),
                  pl.BlockSpec((S, LANES), lambda b, h, i: (b, h)),
                  pl.BlockSpec((1, nq, VT_ROWS, tq), lambda b, h, i: (b, 0, h, 0))],
        out_specs=pl.BlockSpec((tq, LANES), lambda b, h, i: (b * nq + i, h)),
        scratch_shapes=[pltpu.VMEM((2, tq, tq), F32), pltpu.VMEM((2, tq, tq), F32),
                        pltpu.VMEM((2, 1, tq), F32), pltpu.VMEM((2, VT_ROWS, tq), F32)],
        compiler_params=_cparams(("arbitrary", "arbitrary", "arbitrary")),
        name="diffattn",
    )(lam_vecs, g_subln, qdt, kd, vdt)


def _outproj_kernel(oa_ref, ob_ref, x_ref, mod_ref, wo_ref, bo_ref, g_ref, wr_ref, br_ref,
                    x1_ref, h2_ref, idx_ref, gate_ref, rank_ref, cnt_ref, carry_ref):
    i = pl.program_id(0)
    tm = x_ref.shape[0]
    half = oa_ref.shape[1]

    @pl.when(i == 0)
    def _():
        carry_ref[...] = jnp.zeros(carry_ref.shape, F32)

    gt1 = mod_ref[0, 2:3, :]
    sh2 = mod_ref[0, 3:4, :]
    sc2 = mod_ref[0, 4:5, :]
    mixed = (jnp.dot(oa_ref[...], wo_ref[0:half, :], preferred_element_type=F32)
             + jnp.dot(ob_ref[...], wo_ref[half:, :], preferred_element_type=F32) + bo_ref[...])
    x1 = x_ref[...] + gt1 * mixed
    x1_ref[...] = x1
    h2 = _rms(x1) * g_ref[...] * (1.0 + sc2) + sh2
    dn = (((1,), (1,)), ((), ()))
    h_hi = h2.astype(BF16)
    h2_ref[...] = _pack_bf16_pairs(h2)
    h_lo = (h2 - h_hi.astype(F32)).astype(BF16)
    w = wr_ref[...]
    w_hi = w.astype(BF16)
    w_lo = (w - w_hi.astype(F32)).astype(BF16)
    logits = (lax.dot_general(w_hi, h_hi, dn, preferred_element_type=F32)
              + lax.dot_general(w_hi, h_lo, dn, preferred_element_type=F32)
              + lax.dot_general(w_lo, h_hi, dn, preferred_element_type=F32)
              + br_ref[...])

    eidx = lax.broadcasted_iota(jnp.int32, logits.shape, 0)
    vals = logits
    onehots, top_vals, top_idx = [], [], []
    for _k in range(TOP_K):
        mx = jnp.max(vals, axis=0, keepdims=True)
        sel = jnp.min(jnp.where(vals == mx, eidx, N_EXPERTS), axis=0, keepdims=True)
        oh = eidx == sel
        onehots.append(oh)
        top_vals.append(mx)
        top_idx.append(sel)
        vals = jnp.where(oh, NEG_INF, vals)
    exps = [jnp.exp(v - top_vals[0]) for v in top_vals]
    denom = exps[0] + exps[1] + exps[2] + exps[3]
    gate_ref[...] = jnp.concatenate([e / denom for e in exps], axis=0)
    idx_ref[...] = jnp.concatenate(top_idx, axis=0)

    member = (onehots[0] | onehots[1] | onehots[2] | onehots[3])
    member_f = member.astype(F32)
    t_src = lax.broadcasted_iota(jnp.int32, (tm, tm), 0)
    t_dst = lax.broadcasted_iota(jnp.int32, (tm, tm), 1)
    before = (t_src < t_dst).astype(BF16)
    prefix = jnp.dot(member.astype(BF16), before, preferred_element_type=F32) + carry_ref[...]
    ranks = [jnp.sum(jnp.where(oh, prefix, 0.0), axis=0, keepdims=True) for oh in onehots]
    rank_ref[...] = jnp.concatenate(ranks, axis=0).astype(jnp.int32)
    carry_ref[...] = carry_ref[...] + jnp.sum(member_f, axis=1, keepdims=True)
    cnt_ref[...] = jnp.broadcast_to(carry_ref[...], cnt_ref.shape)


def _outproj(out_a, out_b, x2, mod3, w_out, b_out, g_ffn, wr_t, br_col, S):
    N, D = x2.shape
    tm = TM_OUT
    tiles_per_seq = S // tm
    row = lambda i: (i, 0)
    colb = lambda i: (0, i)
    const = lambda i: (0, 0)
    return pl.pallas_call(
        _outproj_kernel,
        out_shape=[jax.ShapeDtypeStruct((N, D), F32), jax.ShapeDtypeStruct((N, D // 2), jnp.int32),
                   jax.ShapeDtypeStruct((TOP_K, N), jnp.int32), jax.ShapeDtypeStruct((TOP_K, N), F32),
                   jax.ShapeDtypeStruct((TOP_K, N), jnp.int32), jax.ShapeDtypeStruct((N_EXPERTS, LANES), F32)],
        grid=(N // tm,),
        in_specs=[pl.BlockSpec((tm, out_a.shape[1]), row),
                  pl.BlockSpec((tm, out_b.shape[1]), row),
                  pl.BlockSpec((tm, D), row),
                  pl.BlockSpec((1, N_MOD, D), lambda i: (i // tiles_per_seq, 0, 0)),
                  pl.BlockSpec(w_out.shape, const),
                  pl.BlockSpec((1, D), const),
                  pl.BlockSpec((1, D), const),
                  pl.BlockSpec(wr_t.shape, const),
                  pl.BlockSpec((N_EXPERTS, 1), const)],
        out_specs=[pl.BlockSpec((tm, D), row), pl.BlockSpec((tm, D // 2), row),
                   pl.BlockSpec((TOP_K, tm), colb), pl.BlockSpec((TOP_K, tm), colb),
                   pl.BlockSpec((TOP_K, tm), colb), pl.BlockSpec((N_EXPERTS, LANES), const)],
        scratch_shapes=[pltpu.VMEM((N_EXPERTS, 1), F32)],
        compiler_params=_cparams(("arbitrary",)),
        name="outproj_router",
    )(out_a, out_b, x2, mod3, w_out, b_out, g_ffn, wr_t, br_col)


def _experts_kernel(first_ref, nblk_ref, x_ref, w1_ref, b1g_ref, b1l_ref, w2_ref, b2_ref, y_ref,
                    w1g_a, w1l_a, w2_a, w1g_b, w1l_b, w2_b, xbuf, ybuf, xsem, ysem):
    s = pl.program_id(0)
    n_exp = pl.num_programs(0) - 1
    e = jnp.maximum(s - 1, 0)
    first = first_ref[e]
    n_blk = jnp.where(s >= 1, nblk_ref[e], 0)
    nxt = jnp.minimum(s, n_exp - 1)
    do_prep = (s < n_exp) & (nblk_ref[nxt] > 0)
    bm = xbuf.shape[1]

    def x_copy(j, slot, first_blk=None):
        blk = (first if first_blk is None else first_blk) + j
        rows = pl.ds(pl.multiple_of(blk * bm, bm), bm)
        return pltpu.make_async_copy(x_ref.at[rows], xbuf.at[slot], xsem.at[slot])

    def y_copy(j, slot):
        rows = pl.ds(pl.multiple_of((first + j) * bm, bm), bm)
        return pltpu.make_async_copy(ybuf.at[slot], y_ref.at[rows], ysem.at[slot])

    def swiglu(ug, ul):
        glu = jnp.minimum(ug, SWIGLU_LIMIT)
        lin = jnp.clip(ul, -SWIGLU_LIMIT, SWIGLU_LIMIT)
        return (glu * jax.nn.sigmoid(SWIGLU_ALPHA * glu) * (lin + 1.0)).astype(BF16)

    def relayout(new):
        w1g_s, w1l_s, w2_s = new
        ch = 256
        half = ch // 2
        for c in range(w1_ref.shape[2] // ch):
            t = w1_ref[0, :, c * ch:(c + 1) * ch].astype(BF16).T
            pairs = pltpu.bitcast(t, jnp.int32)
            cols = slice(c * half, (c + 1) * half)
            w1g_s[:, cols] = lax.bitcast_convert_type(lax.shift_left(pairs, 16), F32).astype(BF16).T
            w1l_s[:, cols] = lax.bitcast_convert_type(pairs & jnp.int32(HIGH_HALF), F32).astype(BF16).T
        w2_s[...] = w2_ref[0].astype(BF16)

    def block(j, cur, new=None):
        w1g_s, w1l_s, w2_s = cur
        slot = j & 1
        x_copy(j, slot).wait()

        @pl.when(j + 1 < n_blk)
        def _():
            x_copy(j + 1, 1 - slot).start(priority=1)

        @pl.when(j >= 2)
        def _():
            y_copy(j - 2, slot).wait()

        if new is not None:
            relayout(new)
        x = _unpack_bf16_pairs(xbuf[slot]).astype(BF16)
        ug = jnp.dot(x, w1g_s[...], preferred_element_type=F32) + b1g_ref[0]
        ul = jnp.dot(x, w1l_s[...], preferred_element_type=F32) + b1l_ref[0]
        y = jnp.dot(swiglu(ug, ul), w2_s[...], preferred_element_type=F32) + b2_ref[0]
        ybuf[slot] = _pack_bf16_pairs(y)
        y_copy(j, slot).start(priority=1)

    def step(new, cur):
        @pl.when((n_blk > 0) & do_prep)
        def _():
            block(0, cur, new)

        @pl.when((n_blk > 0) & jnp.logical_not(do_prep))
        def _():
            block(0, cur)

        @pl.when((n_blk == 0) & do_prep)
        def _():
            relayout(new)

        def later_block(j, carry):
            block(j, cur)
            return carry

        lax.fori_loop(1, n_blk, later_block, 0)

    set_a, set_b = (w1g_a, w1l_a, w2_a), (w1g_b, w1l_b, w2_b)

    @pl.when(s % 2 == 0)
    def _():
        step(set_a, set_b)

    @pl.when(s % 2 == 1)
    def _():
        step(set_b, set_a)

    @pl.when(do_prep)
    def _():
        x_copy(0, 0, first_ref[nxt]).start(priority=1)

    @pl.when(n_blk >= 2)
    def _():
        y_copy(n_blk - 2, n_blk & 1).wait()

    @pl.when(n_blk >= 1)
    def _():
        y_copy(n_blk - 1, (n_blk - 1) & 1).wait()


def _experts(first_blk, n_blk, x_rows, w1, b1g, b1l, w2, b2):
    E, Fh, D = w2.shape
    bm = BLOCK_ROWS
    n_rows = x_rows.shape[0]
    wsel = lambda s, fb, nb: (jnp.minimum(s, E - 1), 0, 0)
    bsel = lambda s, fb, nb: (jnp.maximum(s - 1, 0), 0, 0)
    return pl.pallas_call(
        _experts_kernel,
        out_shape=jax.ShapeDtypeStruct((n_rows, D // 2), jnp.int32),
        grid_spec=pltpu.PrefetchScalarGridSpec(
            num_scalar_prefetch=2,
            grid=(E + 1,),
            in_specs=[pl.BlockSpec(memory_space=pl.ANY),
                      pl.BlockSpec((1, D, 2 * Fh), wsel),
                      pl.BlockSpec((1, 1, Fh), bsel),
                      pl.BlockSpec((1, 1, Fh), bsel),
                      pl.BlockSpec((1, Fh, D), wsel),
                      pl.BlockSpec((1, 1, D), bsel)],
            out_specs=pl.BlockSpec(memory_space=pl.ANY),
            scratch_shapes=[pltpu.VMEM((D, Fh), BF16), pltpu.VMEM((D, Fh), BF16), pltpu.VMEM((Fh, D), BF16),
                            pltpu.VMEM((D, Fh), BF16), pltpu.VMEM((D, Fh), BF16), pltpu.VMEM((Fh, D), BF16),
                            pltpu.VMEM((2, bm, D // 2), jnp.int32), pltpu.VMEM((2, bm, D // 2), jnp.int32),
                            pltpu.SemaphoreType.DMA((2,)), pltpu.SemaphoreType.DMA((2,))]),
        compiler_params=_cparams(("arbitrary",), 56 * 1024 * 1024),
        name="experts",
    )(first_blk, n_blk, x_rows, w1, b1g, b1l, w2, b2)


SC_CORES = 2
SC_SUBCORES = 16
SC_CHUNK = 64


def _sc_gather_rows(table, idx):
    M = idx.shape[0]
    D = table.shape[1]
    workers = SC_CORES * SC_SUBCORES
    per_worker = M // workers
    n_chunks = per_worker // SC_CHUNK
    assert M % workers == 0 and per_worker % (2 * SC_CHUNK) == 0
    mesh = plsc.VectorSubcoreMesh(core_axis_name="c", subcore_axis_name="s")

    @functools.partial(
        pl.kernel, mesh=mesh,
        out_type=jax.ShapeDtypeStruct((M, D), table.dtype),
        scratch_types=[pltpu.VMEM((per_worker,), jnp.int32),
                       pltpu.VMEM((SC_CHUNK, D), table.dtype), pltpu.VMEM((SC_CHUNK, D), table.dtype),
                       pltpu.SemaphoreType.DMA, pltpu.SemaphoreType.DMA],
        name="sc_gather_rows")
    def gather(table_hbm, idx_hbm, out_hbm, idx_v, rows0, rows1, sem0, sem1):
        wid = lax.axis_index("s") * SC_CORES + lax.axis_index("c")
        base = wid * per_worker
        pltpu.sync_copy(idx_hbm.at[pl.ds(base, per_worker)], idx_v)

        def fetch(c, buf, sem):
            off = pl.multiple_of(c * SC_CHUNK, SC_CHUNK)
            return pltpu.make_async_copy(table_hbm.at[idx_v.at[pl.ds(off, SC_CHUNK)]], buf, sem)

        def flush(c, buf):
            off = pl.multiple_of(c * SC_CHUNK, SC_CHUNK)
            pltpu.sync_copy(buf, out_hbm.at[pl.ds(base + off, SC_CHUNK)])

        fetch(0, rows0, sem0).start()

        @pl.loop(0, n_chunks // 2)
        def _(jj):
            c = 2 * jj
            fetch(c + 1, rows1, sem1).start()
            fetch(c, rows0, sem0).wait()
            flush(c, rows0)

            @pl.when(c + 2 < n_chunks)
            def _():
                fetch(c + 2, rows0, sem0).start()

            fetch(c + 1, rows1, sem1).wait()
            flush(c + 1, rows1)

    return gather(table, idx)


SC_SCATTER_CHUNK = 128


def _sc_scatter_rows(rows, dest, n_rows):
    N, W = rows.shape
    workers = SC_CORES * SC_SUBCORES
    per_worker = N // workers
    chunks = per_worker // SC_SCATTER_CHUNK
    assert N % workers == 0 and per_worker % SC_SCATTER_CHUNK == 0
    dest3 = dest.reshape(TOP_K, N // SC_SCATTER_CHUNK, SC_SCATTER_CHUNK)
    mesh = plsc.VectorSubcoreMesh(core_axis_name="c", subcore_axis_name="s")

    @functools.partial(
        pl.kernel, mesh=mesh,
        out_type=jax.ShapeDtypeStruct((n_rows, W), rows.dtype),
        scratch_types=[pltpu.VMEM((TOP_K, chunks, SC_SCATTER_CHUNK), jnp.int32),
                       pltpu.VMEM((SC_SCATTER_CHUNK, W), rows.dtype)],
        name="sc_scatter_rows")
    def scatter(rows_hbm, dest_hbm, out_hbm, idx_v, rows_v):
        wid = lax.axis_index("s") * SC_CORES + lax.axis_index("c")
        for k in range(TOP_K):
            pltpu.sync_copy(dest_hbm.at[k, pl.ds(wid * chunks, chunks)], idx_v.at[k])

        @pl.loop(0, chunks)
        def _(j):
            start = pl.multiple_of(wid * per_worker + j * SC_SCATTER_CHUNK, SC_SCATTER_CHUNK)
            pltpu.sync_copy(rows_hbm.at[pl.ds(start, SC_SCATTER_CHUNK)], rows_v)
            for k in range(TOP_K):
                pltpu.sync_copy(rows_v, out_hbm.at[idx_v.at[k, j]])

    return scatter(rows, dest3)


def _combine_kernel(x1_ref, gate_ref, mod_ref, g_ref, y_ref, o_ref, *, final_norm):
    gate = gate_ref[...]
    moe = gate[:, 0:1] * _unpack_bf16_pairs(y_ref[0])
    for k in range(1, TOP_K):
        moe = moe + gate[:, k:k + 1] * _unpack_bf16_pairs(y_ref[k])
    gt2 = mod_ref[0, 5:6, :]
    x2 = x1_ref[...] + gt2 * moe
    o_ref[...] = _rms(x2) * g_ref[...] if final_norm else x2


def _combine(x1, gate_t, mod3, g_final, y_tok, S, final_norm):
    N, D = x1.shape
    tm = TM_ROWS
    tiles_per_seq = S // tm
    row = lambda i: (i, 0)
    return pl.pallas_call(
        functools.partial(_combine_kernel, final_norm=final_norm),
        out_shape=jax.ShapeDtypeStruct((N, D), F32),
        grid=(N // tm,),
        in_specs=[pl.BlockSpec((tm, D), row),
                  pl.BlockSpec((tm, TOP_K), row),
                  pl.BlockSpec((1, N_MOD, D), lambda i: (i // tiles_per_seq, 0, 0)),
                  pl.BlockSpec((1, D), lambda i: (0, 0)),
                  pl.BlockSpec((TOP_K, tm, D // 2), lambda i: (0, i, 0))],
        out_specs=pl.BlockSpec((tm, D), row),
        compiler_params=_cparams(("arbitrary",)),
        name="combine",
    )(x1, gate_t, mod3, g_final, y_tok)


def _routing_tables(counts, idx, rank):
    bm = BLOCK_ROWS
    counts = counts.astype(jnp.int32)
    padded = (counts + bm - 1) // bm * bm
    pends = jnp.cumsum(padded)
    pstarts = pends - padded
    experts = jnp.arange(N_EXPERTS, dtype=jnp.int32)
    dest = jnp.sum(jnp.where(idx[..., None] == experts, pstarts, 0), axis=-1) + rank
    return dest.astype(jnp.int32), (pstarts // bm).astype(jnp.int32), (padded // bm).astype(jnp.int32)


def _extended_in_weights(w_in, b_in):
    a_q = SWA_Q_HEADS * HEAD_DIM
    a_kv = SWA_KV_HEADS * HEAD_DIM
    b_w = DIFF_HEADS * DIFF_V_DIM
    spans = [(0, a_q)]
    for base in (a_q, a_q + a_kv):
        for j in range(SWA_KV_HEADS):
            spans += [(base + j * HEAD_DIM, base + (j + 1) * HEAD_DIM)] * 2
    spans.append((a_q + 2 * a_kv, a_q + 2 * a_kv + 3 * b_w))
    w_ext = jnp.concatenate([w_in[:, lo:hi] for lo, hi in spans], axis=1).astype(BF16)
    b_ext = jnp.concatenate([b_in[lo:hi] for lo, hi in spans]).reshape(1, -1)
    widths = (a_q, 2 * a_kv, 2 * a_kv, b_w, b_w, b_w)
    return w_ext, b_ext, widths


def kernel(x, c, positions, w_ada, b_ada, g_mix, w_in, b_in, attn_sinks, lambda_q1, lambda_k1, lambda_q2,
           lambda_k2, g_subln, w_out, b_out, g_ffn, w_router, b_router, w1, b1, w2, b2, g_final):
    B, S, D = x.shape
    N = B * S
    depth = w_ada.shape[0]
    n_rows = (N * TOP_K + N_EXPERTS * (BLOCK_ROWS - 1) + BLOCK_ROWS - 1) // BLOCK_ROWS * BLOCK_ROWS

    inv = 1.0 / (ROPE_THETA ** (jnp.arange(0, HEAD_DIM, 2, dtype=F32) / HEAD_DIM))
    inv_lane = jnp.tile(inv, LANES // (HEAD_DIM // 2)).reshape(1, LANES)
    pos2 = positions.reshape(N, 1)
    xcur = x.reshape(N, D)

    for layer in range(depth):
        last = layer == depth - 1
        lambda_init = 0.8 - 0.6 * math.exp(-0.3 * layer)
        mod3 = _adaln(c, w_ada[layer], b_ada[layer]).reshape(B, N_MOD, D)

        w_ext, b_ext, widths = _extended_in_weights(w_in[layer], b_in[layer])
        qa, ka2, va2, qdt, kd, vdt = _inproj(xcur, pos2, inv_lane, mod3, g_mix[layer].reshape(1, D),
                                             w_ext, b_ext, S, widths)
        out_a = _swa(attn_sinks[layer], qa, ka2, va2, B, S)
        lam_vecs = jnp.stack([lambda_q1[layer], lambda_k1[layer], lambda_q2[layer], lambda_k2[layer]])
        out_b = _diffattn(lam_vecs, g_subln[layer].reshape(1, DIFF_V_DIM), qdt, kd, vdt, B, S, lambda_init)

        x1, h2, idx, gate, rank, counts = _outproj(
            out_a, out_b, xcur, mod3, w_out[layer].astype(BF16), b_out[layer].reshape(1, D),
            g_ffn[layer].reshape(1, D), w_router[layer].T, b_router[layer].reshape(N_EXPERTS, 1), S)

        dest, first_blk, n_blk = _routing_tables(counts[:, 0], idx, rank)
        x_rows = _sc_scatter_rows(h2, dest, n_rows)
        y_rows = _experts(first_blk, n_blk, x_rows, w1[layer],
                          b1[layer][:, None, 0::2], b1[layer][:, None, 1::2],
                          w2[layer], b2[layer][:, None, :])
        y_tok = _sc_gather_rows(y_rows, dest.reshape(-1)).reshape(TOP_K, N, D // 2)
        xcur = _combine(x1, gate.T, mod3, g_final.reshape(1, D), y_tok, S, final_norm=last)
    return xcur.reshape(B, S, D)
```

```python
import functools
import math

import jax
import jax.numpy as jnp
from jax import lax
from jax.experimental import pallas as pl
from jax.experimental.pallas import tpu as pltpu
from jax.experimental.pallas import tpu_sc as plsc

HEAD_DIM = 64
SWA_Q_HEADS = 8
SWA_KV_HEADS = 2
SWA_GROUP = SWA_Q_HEADS // SWA_KV_HEADS
WINDOW = 128
DIFF_HEADS = 4
DIFF_V_DIM = 2 * HEAD_DIM
ROPE_THETA = 10000.0
N_EXPERTS = 32
TOP_K = 4
SWIGLU_ALPHA = 1.702
SWIGLU_LIMIT = 7.0
EPS = 1e-5
N_MOD = 6

LANES = 128
F32 = jnp.float32
BF16 = jnp.bfloat16
NEG_INF = float("-inf")

TM_PROJ = 1024
TQ_SWA = 512
VT_ROWS = DIFF_V_DIM + 16
TM_OUT = 512
TM_ROWS = 256
BLOCKS_PER_EXPERT = 4
BF16_ROWS = 16
VMEM_LIMIT = 48 * 1024 * 1024


def _block_rows(n_tokens):
    balanced = n_tokens * TOP_K // N_EXPERTS
    step = BLOCKS_PER_EXPERT * BF16_ROWS
    return max(LANES, -(-(balanced * 105 // 100) // step) * BF16_ROWS)


def _cparams(sem, vmem=VMEM_LIMIT):
    return pltpu.CompilerParams(dimension_semantics=sem, vmem_limit_bytes=vmem)


def _adaln_kernel(ct_ref, w_ref, b_ref, o_ref):
    c = ct_ref[...]
    cond = c * jax.nn.sigmoid(c)
    w = w_ref[...]
    rows = [jnp.sum(w * cond[:, b:b + 1], axis=0, keepdims=True) for b in range(c.shape[1])]
    o_ref[...] = jnp.concatenate(rows, axis=0) + b_ref[...]


def _adaln(c, w_ada, b_ada):
    B, D = c.shape
    n_out = w_ada.shape[1]
    tn = 1024
    return pl.pallas_call(
        _adaln_kernel,
        out_shape=jax.ShapeDtypeStruct((B, n_out), F32),
        grid=(n_out // tn,),
        in_specs=[pl.BlockSpec((D, B), lambda j: (0, 0)),
                  pl.BlockSpec((D, tn), lambda j: (0, j)),
                  pl.BlockSpec((1, tn), lambda j: (0, j))],
        out_specs=pl.BlockSpec((B, tn), lambda j: (0, j)),
        compiler_params=_cparams(("arbitrary",)),
        name="adaln",
    )(c.T, w_ada, b_ada.reshape(1, n_out))


def _rms(x):
    return x * lax.rsqrt(jnp.mean(x * x, axis=-1, keepdims=True) + EPS)


HIGH_HALF = -65536


def _pack_bf16_pairs(v):
    bits = lax.bitcast_convert_type(v.astype(BF16).astype(F32), jnp.int32)
    half = v.shape[1] // 2
    return lax.shift_right_logical(bits[:, :half], 16) | (bits[:, half:] & jnp.int32(HIGH_HALF))


def _unpack_bf16_pairs(w):
    return jnp.concatenate([lax.bitcast_convert_type(lax.shift_left(w, 16), F32),
                            lax.bitcast_convert_type(w & jnp.int32(HIGH_HALF), F32)], axis=1)


def _inproj_kernel(x_ref, pos_ref, inv_ref, mod_ref, g_ref, w_ref, b_ref,
                   qa_ref, ka_ref, va_ref, qd_ref, kd_ref, vd_ref):
    x = x_ref[...]
    sh = mod_ref[0, 0:1, :]
    sc = mod_ref[0, 1:2, :]
    h = _rms(x) * g_ref[...] * (1.0 + sc) + sh
    proj = jnp.dot(h.astype(BF16), w_ref[...], preferred_element_type=F32) + b_ref[...]

    lane = lax.broadcasted_iota(jnp.int32, (1, LANES), 1)
    first_half = (lane & (HEAD_DIM - 1)) < (HEAD_DIM // 2)
    n_freq = HEAD_DIM // 2
    groups = LANES // n_freq
    tm = x.shape[0]
    rows = tm // groups
    group = lax.shift_right_logical(lane, n_freq.bit_length() - 1)
    pos = pos_ref[...].astype(F32)
    pos_q = pos[0:rows]
    for g in range(1, groups):
        pos_q = jnp.where(group == g, pos[g * rows:(g + 1) * rows], pos_q)
    ang_q = pos_q * inv_ref[...]

    def spread(table_q):
        blocks = []
        for g in range(groups):
            only = jnp.where(group == g, table_q, 0.0)
            full = only
            for r in range(1, groups):
                full = full + pltpu.roll(only, r * n_freq, axis=1)
            blocks.append(full)
        return jnp.concatenate(blocks, axis=0)

    cos = spread(jnp.cos(ang_q))
    sin = spread(jnp.sin(ang_q))
    sin_signed = jnp.where(first_half, -sin, sin)

    def rope(t):
        partner = jnp.where(first_half,
                            pltpu.roll(t, LANES - HEAD_DIM // 2, axis=1),
                            pltpu.roll(t, HEAD_DIM // 2, axis=1))
        return t * cos + partner * sin_signed

    def emit(out_ref, col0, width, rotary, scale, transposed):
        for j in range(width // LANES):
            t = proj[:, col0 + j * LANES: col0 + (j + 1) * LANES]
            if rotary:
                t = rope(t)
            if scale != 1.0:
                t = t * scale
            if transposed:
                rows = out_ref.shape[2] // (width // LANES)
                out_ref[0, 0, j * rows:j * rows + LANES, :] = t.T.astype(out_ref.dtype)
                if rows > LANES:
                    fill = lax.broadcasted_iota(jnp.int32, (rows - LANES, t.shape[0]), 0) == 0
                    out_ref[0, 0, j * rows + LANES:(j + 1) * rows, :] = fill.astype(out_ref.dtype)
            else:
                out_ref[:, j * LANES:(j + 1) * LANES] = t.astype(out_ref.dtype)

    swa_scale = 1.0 / math.sqrt(HEAD_DIM)
    diff_scale = math.log2(math.e) / math.sqrt(HEAD_DIM)
    col = 0
    for out_ref, width, rotary, scale, transposed in (
            (qa_ref, qa_ref.shape[1], True, swa_scale, False), (ka_ref, ka_ref.shape[1], True, 1.0, False),
            (va_ref, va_ref.shape[1], False, 1.0, False), (qd_ref, qd_ref.shape[2], True, diff_scale, True),
            (kd_ref, kd_ref.shape[1], True, 1.0, False),
            (vd_ref, vd_ref.shape[2] // VT_ROWS * LANES, False, 1.0, True)):
        emit(out_ref, col, width, rotary, scale, transposed)
        col += width


def _inproj(x2, pos2, inv_lane, mod3, g_mix, w_ext, b_ext, S, widths):
    N, D = x2.shape
    tm = TM_PROJ
    C = w_ext.shape[1]
    tiles_per_seq = S // tm
    row = lambda i: (i, 0)
    t_rows = (0, 0, 0, widths[3], 0, widths[5] // LANES * VT_ROWS)
    out_shape, out_specs = [], []
    for w, tr in zip(widths, t_rows):
        if tr:
            out_shape.append(jax.ShapeDtypeStruct((N // S, tiles_per_seq, tr, tm), BF16))
            out_specs.append(pl.BlockSpec((1, 1, tr, tm), lambda i: (i // tiles_per_seq, i % tiles_per_seq, 0, 0)))
        else:
            out_shape.append(jax.ShapeDtypeStruct((N, w), BF16))
            out_specs.append(pl.BlockSpec((tm, w), row))
    return pl.pallas_call(
        _inproj_kernel,
        out_shape=out_shape,
        grid=(N // tm,),
        in_specs=[pl.BlockSpec((tm, D), row),
                  pl.BlockSpec((tm, 1), row),
                  pl.BlockSpec((1, LANES), lambda i: (0, 0)),
                  pl.BlockSpec((1, N_MOD, D), lambda i: (i // tiles_per_seq, 0, 0)),
                  pl.BlockSpec((1, D), lambda i: (0, 0)),
                  pl.BlockSpec((D, C), lambda i: (0, 0)),
                  pl.BlockSpec((1, C), lambda i: (0, 0))],
        out_specs=out_specs,
        compiler_params=_cparams(("arbitrary",)),
        name="inproj",
    )(x2, pos2, inv_lane, mod3, g_mix, w_ext, b_ext)


def _swa_kernel(sink_ref, q_ref, kc_ref, kp_ref, vc_ref, vp_ref, o_ref):
    i = pl.program_id(1)
    tq = q_ref.shape[0]
    lane = lax.broadcasted_iota(jnp.int32, (1, LANES), 1)
    lo = lane < HEAD_DIM
    qi = lax.broadcasted_iota(jnp.int32, (WINDOW, 2 * WINDOW), 0) + WINDOW
    kj = lax.broadcasted_iota(jnp.int32, (WINDOW, 2 * WINDOW), 1)
    band = (qi - kj >= 0) & (qi - kj < WINDOW)
    dn = (((1,), (1,)), ((), ()))
    for c in range(tq // WINDOW):
        if c == 0:
            kcat = jnp.concatenate([kp_ref[...], kc_ref[0:WINDOW, :]], axis=0)
            vcat = jnp.concatenate([vp_ref[...], vc_ref[0:WINDOW, :]], axis=0)
            mask = band & (kj >= jnp.where(i > 0, 0, WINDOW))
        else:
            kcat = kc_ref[(c - 1) * WINDOW:(c + 1) * WINDOW, :]
            vcat = vc_ref[(c - 1) * WINDOW:(c + 1) * WINDOW, :]
            mask = band
        for j in range(SWA_KV_HEADS):
            kj2 = kcat[:, j * LANES:(j + 1) * LANES]
            vj2 = vcat[:, j * LANES:(j + 1) * LANES]
            zero = jnp.zeros_like(kj2)
            k_halves = (jnp.where(lo, kj2, zero), jnp.where(lo, zero, kj2))
            v_halves = (jnp.where(lo, vj2, zero), jnp.where(lo, zero, vj2))
            for p in range(SWA_GROUP // 2):
                g = j * (SWA_GROUP // 2) + p
                q = q_ref[c * WINDOW:(c + 1) * WINDOW, g * LANES:(g + 1) * LANES]
                out = jnp.zeros((WINDOW, LANES), F32)
                for half in range(2):
                    sink = sink_ref[2 * g + half]
                    s = lax.dot_general(q, k_halves[half], dn, preferred_element_type=F32)
                    s = jnp.where(mask, s, NEG_INF)
                    m = jnp.maximum(jnp.max(s, axis=1, keepdims=True), sink)
                    e = jnp.exp(s - m)
                    denom = jnp.sum(e, axis=1, keepdims=True) + jnp.exp(sink - m)
                    pv = jnp.dot(e.astype(BF16), v_halves[half], preferred_element_type=F32)
                    out = out + pv / denom
                o_ref[c * WINDOW:(c + 1) * WINDOW, g * LANES:(g + 1) * LANES] = out.astype(o_ref.dtype)


def _swa(sinks, qa, ka2, va2, B, S):
    N = qa.shape[0]
    tq = TQ_SWA
    nq = S // tq
    wpt = tq // WINDOW
    wps = S // WINDOW
    cur = lambda b, i: (b * nq + i, 0)
    prev = lambda b, i: (b * wps + jnp.maximum(i * wpt - 1, 0), 0)
    return pl.pallas_call(
        _swa_kernel,
        out_shape=jax.ShapeDtypeStruct((N, qa.shape[1]), BF16),
        grid=(B, nq),
        in_specs=[pl.BlockSpec(memory_space=pltpu.SMEM),
                  pl.BlockSpec((tq, qa.shape[1]), cur),
                  pl.BlockSpec((tq, ka2.shape[1]), cur),
                  pl.BlockSpec((WINDOW, ka2.shape[1]), prev),
                  pl.BlockSpec((tq, va2.shape[1]), cur),
                  pl.BlockSpec((WINDOW, va2.shape[1]), prev)],
        out_specs=pl.BlockSpec((tq, qa.shape[1]), cur),
        compiler_params=_cparams(("arbitrary", "arbitrary")),
        name="swa",
    )(sinks, qa, ka2, ka2, va2, va2)


def _diff_kernel(lam_ref, g_ref, qt_ref, k_ref, vt_ref, o_ref, sa_ref, sb_ref, m_ref, acc_ref, *, lambda_init):
    i = pl.program_id(2)
    tq = qt_ref.shape[3]
    tk = vt_ref.shape[3]
    qt = qt_ref[0, 0]
    lane = lax.broadcasted_iota(jnp.int32, (1, LANES), 1)
    lo = lane < HEAD_DIM
    m_ref[...] = jnp.full(m_ref.shape, NEG_INF, F32)
    acc_ref[...] = jnp.zeros(acc_ref.shape, F32)

    def scores(c, s_ref):
        k = k_ref[pl.ds(pl.multiple_of(c * tk, tk), tk), :]
        zero = jnp.zeros_like(k)
        s_ref[0] = jnp.dot(jnp.where(lo, k, zero), qt, preferred_element_type=F32)
        s_ref[1] = jnp.dot(jnp.where(lo, zero, k), qt, preferred_element_type=F32)

    def consume(c, s_ref, first_key):
        vt = vt_ref[0, c]
        for mp in range(2):
            s = s_ref[mp]
            if first_key is not None:
                kpos = lax.broadcasted_iota(jnp.int32, (tk, tq), 0) + first_key
                qpos = lax.broadcasted_iota(jnp.int32, (tk, tq), 1)
                s = jnp.where(kpos <= qpos, s, NEG_INF)
            m_prev = m_ref[mp]
            m_new = jnp.maximum(m_prev, jnp.max(s, axis=0, keepdims=True))
            alpha = jnp.exp2(m_prev - m_new)
            p = jnp.exp2(s - m_new).astype(BF16)
            acc_ref[mp] = alpha * acc_ref[mp] + jnp.dot(vt, p, preferred_element_type=F32)
            m_ref[mp] = m_new

    scores(0, sa_ref)

    def pair(jj, carry):
        c = 2 * jj
        scores(c + 1, sb_ref)
        consume(c, sa_ref, None)
        scores(c + 2, sa_ref)
        consume(c + 1, sb_ref, None)
        return carry

    lax.fori_loop(0, lax.shift_right_logical(i, 1), pair, 0)

    @pl.when(i % 2 == 0)
    def _():
        consume(i, sa_ref, 0)

    @pl.when(i % 2 == 1)
    def _():
        scores(i, sb_ref)
        consume(i - 1, sa_ref, None)
        consume(i, sb_ref, 0)

    lq1, lk1, lq2, lk2 = (lam_ref[r:r + 1, :] for r in range(4))
    lam = (jnp.exp(jnp.sum(lq1 * lk1, axis=1, keepdims=True))
           - jnp.exp(jnp.sum(lq2 * lk2, axis=1, keepdims=True)) + lambda_init)
    d = DIFF_V_DIM
    ot = (acc_ref[0, 0:d, :] / acc_ref[0, d:d + 1, :]
          - lam * (acc_ref[1, 0:d, :] / acc_ref[1, d:d + 1, :]))
    ot = ot * lax.rsqrt(jnp.mean(ot * ot, axis=0, keepdims=True) + EPS)
    o_ref[...] = (ot.T * g_ref[...] * (1.0 - lambda_init)).astype(o_ref.dtype)


def _diffattn(lam_vecs, g_subln, qdt, kd, vdt, B, S, lambda_init):
    N, C = kd.shape
    tq = qdt.shape[3]
    nq = S // tq
    return pl.pallas_call(
        functools.partial(_diff_kernel, lambda_init=lambda_init),
        out_shape=jax.ShapeDtypeStruct((N, C), BF16),
        grid=(B, DIFF_HEADS, nq),
        in_specs=[pl.BlockSpec((4, HEAD_DIM), lambda b, h, i: (0, 0)),
                  pl.BlockSpec((1, DIFF_V_DIM), lambda b, h, i: (0, 0)),
                  pl.BlockSpec((1, 1, LANES, tq), lambda b, h, i: (b, i, h, 0)),
                  pl.BlockSpec((S, LANES), lambda b, h, i: (b, h)),
                  pl.BlockSpec((1, nq, VT_ROWS, tq), lambda b, h, i: (b, 0, h, 0))],
        out_specs=pl.BlockSpec((tq, LANES), lambda b, h, i: (b * nq + i, h)),
        scratch_shapes=[pltpu.VMEM((2, tq, tq), F32), pltpu.VMEM((2, tq, tq), F32),
                        pltpu.VMEM((2, 1, tq), F32), pltpu.VMEM((2, VT_ROWS, tq), F32)],
        compiler_params=_cparams(("arbitrary", "arbitrary", "arbitrary")),
        name="diffattn",
    )(lam_vecs, g_subln, qdt, kd, vdt)


def _outproj_kernel(oa_ref, ob_ref, x_ref, mod_ref, wo_ref, bo_ref, g_ref, wr_ref, br_ref,
                    x1_ref, h2_ref, idx_ref, gate_ref, rank_ref, cnt_ref, carry_ref):
    i = pl.program_id(0)
    tm = x_ref.shape[0]
    half = oa_ref.shape[1]

    @pl.when(i == 0)
    def _():
        carry_ref[...] = jnp.zeros(carry_ref.shape, F32)

    gt1 = mod_ref[0, 2:3, :]
    sh2 = mod_ref[0, 3:4, :]
    sc2 = mod_ref[0, 4:5, :]
    mixed = (jnp.dot(oa_ref[...], wo_ref[0:half, :], preferred_element_type=F32)
             + jnp.dot(ob_ref[...], wo_ref[half:, :], preferred_element_type=F32) + bo_ref[...])
    x1 = x_ref[...] + gt1 * mixed
    x1_ref[...] = x1
    h2 = _rms(x1) * g_ref[...] * (1.0 + sc2) + sh2
    dn = (((1,), (1,)), ((), ()))
    h_hi = h2.astype(BF16)
    h2_ref[...] = _pack_bf16_pairs(h2)
    h_lo = (h2 - h_hi.astype(F32)).astype(BF16)
    w = wr_ref[...]
    w_hi = w.astype(BF16)
    w_lo = (w - w_hi.astype(F32)).astype(BF16)
    logits = (lax.dot_general(w_hi, h_hi, dn, preferred_element_type=F32)
              + lax.dot_general(w_hi, h_lo, dn, preferred_element_type=F32)
              + lax.dot_general(w_lo, h_hi, dn, preferred_element_type=F32)
              + br_ref[...])

    eidx = lax.broadcasted_iota(jnp.int32, logits.shape, 0)
    vals = logits
    onehots, top_vals, top_idx = [], [], []
    for _k in range(TOP_K):
        mx = jnp.max(vals, axis=0, keepdims=True)
        sel = jnp.min(jnp.where(vals == mx, eidx, N_EXPERTS), axis=0, keepdims=True)
        oh = eidx == sel
        onehots.append(oh)
        top_vals.append(mx)
        top_idx.append(sel)
        vals = jnp.where(oh, NEG_INF, vals)
    exps = [jnp.exp(v - top_vals[0]) for v in top_vals]
    denom = exps[0] + exps[1] + exps[2] + exps[3]
    gate_ref[...] = jnp.concatenate([e / denom for e in exps], axis=0)
    idx_ref[...] = jnp.concatenate(top_idx, axis=0)

    member = (onehots[0] | onehots[1] | onehots[2] | onehots[3])
    member_f = member.astype(F32)
    t_src = lax.broadcasted_iota(jnp.int32, (tm, tm), 0)
    t_dst = lax.broadcasted_iota(jnp.int32, (tm, tm), 1)
    before = (t_src < t_dst).astype(BF16)
    prefix = jnp.dot(member.astype(BF16), before, preferred_element_type=F32) + carry_ref[...]
    ranks = [jnp.sum(jnp.where(oh, prefix, 0.0), axis=0, keepdims=True) for oh in onehots]
    rank_ref[...] = jnp.concatenate(ranks, axis=0).astype(jnp.int32)
    carry_ref[...] = carry_ref[...] + jnp.sum(member_f, axis=1, keepdims=True)
    cnt_ref[...] = jnp.broadcast_to(carry_ref[...], cnt_ref.shape)


def _outproj(out_a, out_b, x2, mod3, w_out, b_out, g_ffn, wr_t, br_col, S):
    N, D = x2.shape
    tm = TM_OUT
    tiles_per_seq = S // tm
    row = lambda i: (i, 0)
    colb = lambda i: (0, i)
    const = lambda i: (0, 0)
    return pl.pallas_call(
        _outproj_kernel,
        out_shape=[jax.ShapeDtypeStruct((N, D), F32), jax.ShapeDtypeStruct((N, D // 2), jnp.int32),
                   jax.ShapeDtypeStruct((TOP_K, N), jnp.int32), jax.ShapeDtypeStruct((TOP_K, N), F32),
                   jax.ShapeDtypeStruct((TOP_K, N), jnp.int32), jax.ShapeDtypeStruct((N_EXPERTS, LANES), F32)],
        grid=(N // tm,),
        in_specs=[pl.BlockSpec((tm, out_a.shape[1]), row),
                  pl.BlockSpec((tm, out_b.shape[1]), row),
                  pl.BlockSpec((tm, D), row),
                  pl.BlockSpec((1, N_MOD, D), lambda i: (i // tiles_per_seq, 0, 0)),
                  pl.BlockSpec(w_out.shape, const),
                  pl.BlockSpec((1, D), const),
                  pl.BlockSpec((1, D), const),
                  pl.BlockSpec(wr_t.shape, const),
                  pl.BlockSpec((N_EXPERTS, 1), const)],
        out_specs=[pl.BlockSpec((tm, D), row), pl.BlockSpec((tm, D // 2), row),
                   pl.BlockSpec((TOP_K, tm), colb), pl.BlockSpec((TOP_K, tm), colb),
                   pl.BlockSpec((TOP_K, tm), colb), pl.BlockSpec((N_EXPERTS, LANES), const)],
        scratch_shapes=[pltpu.VMEM((N_EXPERTS, 1), F32)],
        compiler_params=_cparams(("arbitrary",)),
        name="outproj_router",
    )(out_a, out_b, x2, mod3, w_out, b_out, g_ffn, wr_t, br_col)


def _experts_kernel(first_ref, nblk_ref, x_ref, w1_ref, b1g_ref, b1l_ref, w2_ref, b2_ref, y_ref,
                    w1g_a, w1l_a, w2_a, w1g_b, w1l_b, w2_b, xbuf, ybuf, xsem, ysem):
    s = pl.program_id(0)
    n_exp = pl.num_programs(0) - 1
    e = jnp.maximum(s - 1, 0)
    first = first_ref[e]
    n_blk = jnp.where(s >= 1, nblk_ref[e], 0)
    nxt = jnp.minimum(s, n_exp - 1)
    do_prep = (s < n_exp) & (nblk_ref[nxt] > 0)
    bm = xbuf.shape[1]

    def x_copy(j, slot, first_blk=None):
        blk = (first if first_blk is None else first_blk) + j
        rows = pl.ds(pl.multiple_of(blk * bm, bm), bm)
        return pltpu.make_async_copy(x_ref.at[rows], xbuf.at[slot], xsem.at[slot])

    def y_copy(j, slot):
        rows = pl.ds(pl.multiple_of((first + j) * bm, bm), bm)
        return pltpu.make_async_copy(ybuf.at[slot], y_ref.at[rows], ysem.at[slot])

    def swiglu(ug, ul):
        glu = jnp.minimum(ug, SWIGLU_LIMIT)
        lin = jnp.clip(ul, -SWIGLU_LIMIT, SWIGLU_LIMIT)
        return (glu * jax.nn.sigmoid(SWIGLU_ALPHA * glu) * (lin + 1.0)).astype(BF16)

    def relayout(new):
        w1g_s, w1l_s, w2_s = new
        ch = 256
        half = ch // 2
        for c in range(w1_ref.shape[2] // ch):
            t = w1_ref[0, :, c * ch:(c + 1) * ch].astype(BF16).T
            pairs = pltpu.bitcast(t, jnp.int32)
            cols = slice(c * half, (c + 1) * half)
            w1g_s[:, cols] = lax.bitcast_convert_type(lax.shift_left(pairs, 16), F32).astype(BF16).T
            w1l_s[:, cols] = lax.bitcast_convert_type(pairs & jnp.int32(HIGH_HALF), F32).astype(BF16).T
        w2_s[...] = w2_ref[0].astype(BF16)

    def block(j, cur, new=None):
        w1g_s, w1l_s, w2_s = cur
        slot = j & 1
        x_copy(j, slot).wait()

        @pl.when(j + 1 < n_blk)
        def _():
            x_copy(j + 1, 1 - slot).start(priority=1)

        @pl.when(j >= 2)
        def _():
            y_copy(j - 2, slot).wait()

        if new is not None:
            relayout(new)
        x = _unpack_bf16_pairs(xbuf[slot]).astype(BF16)
        ug = jnp.dot(x, w1g_s[...], preferred_element_type=F32) + b1g_ref[0]
        ul = jnp.dot(x, w1l_s[...], preferred_element_type=F32) + b1l_ref[0]
        y = jnp.dot(swiglu(ug, ul), w2_s[...], preferred_element_type=F32) + b2_ref[0]
        ybuf[slot] = _pack_bf16_pairs(y)
        y_copy(j, slot).start(priority=1)

    def step(new, cur):
        @pl.when((n_blk > 0) & do_prep)
        def _():
            block(0, cur, new)

        @pl.when((n_blk > 0) & jnp.logical_not(do_prep))
        def _():
            block(0, cur)

        @pl.when((n_blk == 0) & do_prep)
        def _():
            relayout(new)

        def later_block(j, carry):
            block(j, cur)
            return carry

        lax.fori_loop(1, n_blk, later_block, 0)

    set_a, set_b = (w1g_a, w1l_a, w2_a), (w1g_b, w1l_b, w2_b)

    @pl.when(s % 2 == 0)
    def _():
        step(set_a, set_b)

    @pl.when(s % 2 == 1)
    def _():
        step(set_b, set_a)

    @pl.when(do_prep)
    def _():
        x_copy(0, 0, first_ref[nxt]).start(priority=1)

    @pl.when(n_blk >= 2)
    def _():
        y_copy(n_blk - 2, n_blk & 1).wait()

    @pl.when(n_blk >= 1)
    def _():
        y_copy(n_blk - 1, (n_blk - 1) & 1).wait()


def _experts(first_blk, n_blk, x_rows, w1, b1g, b1l, w2, b2, bm):
    E, Fh, D = w2.shape
    n_rows = x_rows.shape[0]
    wsel = lambda s, fb, nb: (jnp.minimum(s, E - 1), 0, 0)
    bsel = lambda s, fb, nb: (jnp.maximum(s - 1, 0), 0, 0)
    return pl.pallas_call(
        _experts_kernel,
        out_shape=jax.ShapeDtypeStruct((n_rows, D // 2), jnp.int32),
        grid_spec=pltpu.PrefetchScalarGridSpec(
            num_scalar_prefetch=2,
            grid=(E + 1,),
            in_specs=[pl.BlockSpec(memory_space=pl.ANY),
                      pl.BlockSpec((1, D, 2 * Fh), wsel),
                      pl.BlockSpec((1, 1, Fh), bsel),
                      pl.BlockSpec((1, 1, Fh), bsel),
                      pl.BlockSpec((1, Fh, D), wsel),
                      pl.BlockSpec((1, 1, D), bsel)],
            out_specs=pl.BlockSpec(memory_space=pl.ANY),
            scratch_shapes=[pltpu.VMEM((D, Fh), BF16), pltpu.VMEM((D, Fh), BF16), pltpu.VMEM((Fh, D), BF16),
                            pltpu.VMEM((D, Fh), BF16), pltpu.VMEM((D, Fh), BF16), pltpu.VMEM((Fh, D), BF16),
                            pltpu.VMEM((2, bm, D // 2), jnp.int32), pltpu.VMEM((2, bm, D // 2), jnp.int32),
                            pltpu.SemaphoreType.DMA((2,)), pltpu.SemaphoreType.DMA((2,))]),
        compiler_params=_cparams(("arbitrary",), 56 * 1024 * 1024),
        name="experts",
    )(first_blk, n_blk, x_rows, w1, b1g, b1l, w2, b2)


SC_CORES = 2
SC_SUBCORES = 16
SC_CHUNK = 64


def _sc_gather_rows(table, idx):
    M = idx.shape[0]
    D = table.shape[1]
    workers = SC_CORES * SC_SUBCORES
    per_worker = M // workers
    n_chunks = per_worker // SC_CHUNK
    assert M % workers == 0 and per_worker % (2 * SC_CHUNK) == 0
    mesh = plsc.VectorSubcoreMesh(core_axis_name="c", subcore_axis_name="s")

    @functools.partial(
        pl.kernel, mesh=mesh,
        out_type=jax.ShapeDtypeStruct((M, D), table.dtype),
        scratch_types=[pltpu.VMEM((per_worker,), jnp.int32),
                       pltpu.VMEM((SC_CHUNK, D), table.dtype), pltpu.VMEM((SC_CHUNK, D), table.dtype),
                       pltpu.SemaphoreType.DMA, pltpu.SemaphoreType.DMA],
        name="sc_gather_rows")
    def gather(table_hbm, idx_hbm, out_hbm, idx_v, rows0, rows1, sem0, sem1):
        wid = lax.axis_index("s") * SC_CORES + lax.axis_index("c")
        base = wid * per_worker
        pltpu.sync_copy(idx_hbm.at[pl.ds(base, per_worker)], idx_v)

        def fetch(c, buf, sem):
            off = pl.multiple_of(c * SC_CHUNK, SC_CHUNK)
            return pltpu.make_async_copy(table_hbm.at[idx_v.at[pl.ds(off, SC_CHUNK)]], buf, sem)

        def flush(c, buf):
            off = pl.multiple_of(c * SC_CHUNK, SC_CHUNK)
            pltpu.sync_copy(buf, out_hbm.at[pl.ds(base + off, SC_CHUNK)])

        fetch(0, rows0, sem0).start()

        @pl.loop(0, n_chunks // 2)
        def _(jj):
            c = 2 * jj
            fetch(c + 1, rows1, sem1).start()
            fetch(c, rows0, sem0).wait()
            flush(c, rows0)

            @pl.when(c + 2 < n_chunks)
            def _():
                fetch(c + 2, rows0, sem0).start()

            fetch(c + 1, rows1, sem1).wait()
            flush(c + 1, rows1)

    return gather(table, idx)


SC_SCATTER_CHUNK = 128


def _sc_scatter_rows(rows, dest, n_rows):
    N, W = rows.shape
    workers = SC_CORES * SC_SUBCORES
    per_worker = N // workers
    chunks = per_worker // SC_SCATTER_CHUNK
    assert N % workers == 0 and per_worker % SC_SCATTER_CHUNK == 0
    dest3 = dest.reshape(TOP_K, N // SC_SCATTER_CHUNK, SC_SCATTER_CHUNK)
    mesh = plsc.VectorSubcoreMesh(core_axis_name="c", subcore_axis_name="s")

    @functools.partial(
        pl.kernel, mesh=mesh,
        out_type=jax.ShapeDtypeStruct((n_rows, W), rows.dtype),
        scratch_types=[pltpu.VMEM((TOP_K, chunks, SC_SCATTER_CHUNK), jnp.int32),
                       pltpu.VMEM((SC_SCATTER_CHUNK, W), rows.dtype)],
        name="sc_scatter_rows")
    def scatter(rows_hbm, dest_hbm, out_hbm, idx_v, rows_v):
        wid = lax.axis_index("s") * SC_CORES + lax.axis_index("c")
        for k in range(TOP_K):
            pltpu.sync_copy(dest_hbm.at[k, pl.ds(wid * chunks, chunks)], idx_v.at[k])

        @pl.loop(0, chunks)
        def _(j):
            start = pl.multiple_of(wid * per_worker + j * SC_SCATTER_CHUNK, SC_SCATTER_CHUNK)
            pltpu.sync_copy(rows_hbm.at[pl.ds(start, SC_SCATTER_CHUNK)], rows_v)
            for k in range(TOP_K):
                pltpu.sync_copy(rows_v, out_hbm.at[idx_v.at[k, j]])

    return scatter(rows, dest3)


def _combine_kernel(x1_ref, gate_ref, mod_ref, g_ref, y_ref, o_ref, *, final_norm):
    gate = gate_ref[...]
    moe = gate[:, 0:1] * _unpack_bf16_pairs(y_ref[0])
    for k in range(1, TOP_K):
        moe = moe + gate[:, k:k + 1] * _unpack_bf16_pairs(y_ref[k])
    gt2 = mod_ref[0, 5:6, :]
    x2 = x1_ref[...] + gt2 * moe
    o_ref[...] = _rms(x2) * g_ref[...] if final_norm else x2


def _combine(x1, gate_t, mod3, g_final, y_tok, S, final_norm):
    N, D = x1.shape
    tm = TM_ROWS
    tiles_per_seq = S // tm
    row = lambda i: (i, 0)
    return pl.pallas_call(
        functools.partial(_combine_kernel, final_norm=final_norm),
        out_shape=jax.ShapeDtypeStruct((N, D), F32),
        grid=(N // tm,),
        in_specs=[pl.BlockSpec((tm, D), row),
                  pl.BlockSpec((tm, TOP_K), row),
                  pl.BlockSpec((1, N_MOD, D), lambda i: (i // tiles_per_seq, 0, 0)),
                  pl.BlockSpec((1, D), lambda i: (0, 0)),
                  pl.BlockSpec((TOP_K, tm, D // 2), lambda i: (0, i, 0))],
        out_specs=pl.BlockSpec((tm, D), row),
        compiler_params=_cparams(("arbitrary",)),
        name="combine",
    )(x1, gate_t, mod3, g_final, y_tok)


def _routing_tables(counts, idx, rank, bm):
    counts = counts.astype(jnp.int32)
    padded = (counts + bm - 1) // bm * bm
    pends = jnp.cumsum(padded)
    pstarts = pends - padded
    experts = jnp.arange(N_EXPERTS, dtype=jnp.int32)
    dest = jnp.sum(jnp.where(idx[..., None] == experts, pstarts, 0), axis=-1) + rank
    return dest.astype(jnp.int32), (pstarts // bm).astype(jnp.int32), (padded // bm).astype(jnp.int32)


def _extended_in_weights(w_in, b_in):
    a_q = SWA_Q_HEADS * HEAD_DIM
    a_kv = SWA_KV_HEADS * HEAD_DIM
    b_w = DIFF_HEADS * DIFF_V_DIM
    spans = [(0, a_q)]
    for base in (a_q, a_q + a_kv):
        for j in range(SWA_KV_HEADS):
            spans += [(base + j * HEAD_DIM, base + (j + 1) * HEAD_DIM)] * 2
    spans.append((a_q + 2 * a_kv, a_q + 2 * a_kv + 3 * b_w))
    w_ext = jnp.concatenate([w_in[:, lo:hi] for lo, hi in spans], axis=1).astype(BF16)
    b_ext = jnp.concatenate([b_in[lo:hi] for lo, hi in spans]).reshape(1, -1)
    widths = (a_q, 2 * a_kv, 2 * a_kv, b_w, b_w, b_w)
    return w_ext, b_ext, widths


def kernel(x, c, positions, w_ada, b_ada, g_mix, w_in, b_in, attn_sinks, lambda_q1, lambda_k1, lambda_q2,
           lambda_k2, g_subln, w_out, b_out, g_ffn, w_router, b_router, w1, b1, w2, b2, g_final):
    B, S, D = x.shape
    N = B * S
    depth = w_ada.shape[0]
    bm = _block_rows(N)
    n_rows = (N * TOP_K + N_EXPERTS * (bm - 1) + bm - 1) // bm * bm

    inv = 1.0 / (ROPE_THETA ** (jnp.arange(0, HEAD_DIM, 2, dtype=F32) / HEAD_DIM))
    inv_lane = jnp.tile(inv, LANES // (HEAD_DIM // 2)).reshape(1, LANES)
    pos2 = positions.reshape(N, 1)
    xcur = x.reshape(N, D)

    for layer in range(depth):
        last = layer == depth - 1
        lambda_init = 0.8 - 0.6 * math.exp(-0.3 * layer)
        mod3 = _adaln(c, w_ada[layer], b_ada[layer]).reshape(B, N_MOD, D)

        w_ext, b_ext, widths = _extended_in_weights(w_in[layer], b_in[layer])
        qa, ka2, va2, qdt, kd, vdt = _inproj(xcur, pos2, inv_lane, mod3, g_mix[layer].reshape(1, D),
                                             w_ext, b_ext, S, widths)
        out_a = _swa(attn_sinks[layer], qa, ka2, va2, B, S)
        lam_vecs = jnp.stack([lambda_q1[layer], lambda_k1[layer], lambda_q2[layer], lambda_k2[layer]])
        out_b = _diffattn(lam_vecs, g_subln[layer].reshape(1, DIFF_V_DIM), qdt, kd, vdt, B, S, lambda_init)

        x1, h2, idx, gate, rank, counts = _outproj(
            out_a, out_b, xcur, mod3, w_out[layer].astype(BF16), b_out[layer].reshape(1, D),
            g_ffn[layer].reshape(1, D), w_router[layer].T, b_router[layer].reshape(N_EXPERTS, 1), S)

        dest, first_blk, n_blk = _routing_tables(counts[:, 0], idx, rank, bm)
        x_rows = _sc_scatter_rows(h2, dest, n_rows)
        y_rows = _experts(first_blk, n_blk, x_rows, w1[layer],
                          b1[layer][:, None, 0::2], b1[layer][:, None, 1::2],
                          w2[layer], b2[layer][:, None, :], bm)
        y_tok = _sc_gather_rows(y_rows, dest.reshape(-1)).reshape(TOP_K, N, D // 2)
        xcur = _combine(x1, gate.T, mod3, g_final.reshape(1, D), y_tok, S, final_norm=last)
    return xcur.reshape(B, S, D)
```

```python
import functools
import math

import jax
import jax.numpy as jnp
from jax import lax
from jax.experimental import pallas as pl
from jax.experimental.pallas import tpu as pltpu
from jax.experimental.pallas import tpu_sc as plsc

HEAD_DIM = 64
SWA_Q_HEADS = 8
SWA_KV_HEADS = 2
SWA_GROUP = SWA_Q_HEADS // SWA_KV_HEADS
WINDOW = 128
DIFF_HEADS = 4
DIFF_V_DIM = 2 * HEAD_DIM
ROPE_THETA = 10000.0
N_EXPERTS = 32
TOP_K = 4
SWIGLU_ALPHA = 1.702
SWIGLU_LIMIT = 7.0
EPS = 1e-5
N_MOD = 6

LANES = 128
F32 = jnp.float32
BF16 = jnp.bfloat16
NEG_INF = float("-inf")

TM_PROJ = 1024
TQ_SWA = 512
VT_ROWS = DIFF_V_DIM + 16
TM_OUT = 512
TM_ROWS = 256
BLOCKS_PER_EXPERT = 4
BF16_ROWS = 16
VMEM_LIMIT = 48 * 1024 * 1024


def _block_rows(n_tokens):
    balanced = n_tokens * TOP_K // N_EXPERTS
    step = BLOCKS_PER_EXPERT * BF16_ROWS
    return max(LANES, -(-(balanced * 105 // 100) // step) * BF16_ROWS)


def _cparams(sem, vmem=VMEM_LIMIT):
    return pltpu.CompilerParams(dimension_semantics=sem, vmem_limit_bytes=vmem)


def _adaln_kernel(ct_ref, w_ref, b_ref, o_ref):
    c = ct_ref[...]
    cond = c * jax.nn.sigmoid(c)
    w = w_ref[...]
    rows = [jnp.sum(w * cond[:, b:b + 1], axis=0, keepdims=True) for b in range(c.shape[1])]
    o_ref[...] = jnp.concatenate(rows, axis=0) + b_ref[...]


def _adaln(c, w_ada, b_ada):
    B, D = c.shape
    n_out = w_ada.shape[1]
    tn = 1024
    return pl.pallas_call(
        _adaln_kernel,
        out_shape=jax.ShapeDtypeStruct((B, n_out), F32),
        grid=(n_out // tn,),
        in_specs=[pl.BlockSpec((D, B), lambda j: (0, 0)),
                  pl.BlockSpec((D, tn), lambda j: (0, j)),
                  pl.BlockSpec((1, tn), lambda j: (0, j))],
        out_specs=pl.BlockSpec((B, tn), lambda j: (0, j)),
        compiler_params=_cparams(("arbitrary",)),
        name="adaln",
    )(c.T, w_ada, b_ada.reshape(1, n_out))


def _rms(x):
    return x * lax.rsqrt(jnp.mean(x * x, axis=-1, keepdims=True) + EPS)


HIGH_HALF = -65536


def _pack_bf16_pairs(v):
    bits = lax.bitcast_convert_type(v.astype(BF16).astype(F32), jnp.int32)
    half = v.shape[1] // 2
    return lax.shift_right_logical(bits[:, :half], 16) | (bits[:, half:] & jnp.int32(HIGH_HALF))


def _unpack_bf16_pairs(w):
    return jnp.concatenate([lax.bitcast_convert_type(lax.shift_left(w, 16), F32),
                            lax.bitcast_convert_type(w & jnp.int32(HIGH_HALF), F32)], axis=1)


def _inproj_kernel(x_ref, pos_ref, inv_ref, mod_ref, g_ref, w_ref, b_ref,
                   qa_ref, ka_ref, va_ref, qd_ref, kd_ref, vd_ref):
    x = x_ref[...]
    sh = mod_ref[0, 0:1, :]
    sc = mod_ref[0, 1:2, :]
    h = _rms(x) * g_ref[...] * (1.0 + sc) + sh
    proj = jnp.dot(h.astype(BF16), w_ref[...], preferred_element_type=F32) + b_ref[...]

    lane = lax.broadcasted_iota(jnp.int32, (1, LANES), 1)
    first_half = (lane & (HEAD_DIM - 1)) < (HEAD_DIM // 2)
    n_freq = HEAD_DIM // 2
    groups = LANES // n_freq
    tm = x.shape[0]
    rows = tm // groups
    group = lax.shift_right_logical(lane, n_freq.bit_length() - 1)
    pos = pos_ref[...].astype(F32)
    pos_q = pos[0:rows]
    for g in range(1, groups):
        pos_q = jnp.where(group == g, pos[g * rows:(g + 1) * rows], pos_q)
    ang_q = pos_q * inv_ref[...]

    def spread(table_q):
        blocks = []
        for g in range(groups):
            only = jnp.where(group == g, table_q, 0.0)
            full = only
            for r in range(1, groups):
                full = full + pltpu.roll(only, r * n_freq, axis=1)
            blocks.append(full)
        return jnp.concatenate(blocks, axis=0)

    cos = spread(jnp.cos(ang_q))
    sin = spread(jnp.sin(ang_q))
    sin_signed = jnp.where(first_half, -sin, sin)

    def rope(t):
        partner = jnp.where(first_half,
                            pltpu.roll(t, LANES - HEAD_DIM // 2, axis=1),
                            pltpu.roll(t, HEAD_DIM // 2, axis=1))
        return t * cos + partner * sin_signed

    def emit(out_ref, col0, width, rotary, scale, transposed):
        for j in range(width // LANES):
            t = proj[:, col0 + j * LANES: col0 + (j + 1) * LANES]
            if rotary:
                t = rope(t)
            if scale != 1.0:
                t = t * scale
            if transposed:
                rows = out_ref.shape[2] // (width // LANES)
                out_ref[0, 0, j * rows:j * rows + LANES, :] = t.T.astype(out_ref.dtype)
                if rows > LANES:
                    fill = lax.broadcasted_iota(jnp.int32, (rows - LANES, t.shape[0]), 0) == 0
                    out_ref[0, 0, j * rows + LANES:(j + 1) * rows, :] = fill.astype(out_ref.dtype)
            else:
                out_ref[:, j * LANES:(j + 1) * LANES] = t.astype(out_ref.dtype)

    swa_scale = 1.0 / math.sqrt(HEAD_DIM)
    diff_scale = math.log2(math.e) / math.sqrt(HEAD_DIM)
    col = 0
    for out_ref, width, rotary, scale, transposed in (
            (qa_ref, qa_ref.shape[1], True, swa_scale, False), (ka_ref, ka_ref.shape[1], True, 1.0, False),
            (va_ref, va_ref.shape[1], False, 1.0, False), (qd_ref, qd_ref.shape[2], True, diff_scale, True),
            (kd_ref, kd_ref.shape[1], True, 1.0, False),
            (vd_ref, vd_ref.shape[2] // VT_ROWS * LANES, False, 1.0, True)):
        emit(out_ref, col, width, rotary, scale, transposed)
        col += width


def _inproj(x2, pos2, inv_lane, mod3, g_mix, w_ext, b_ext, S, widths):
    N, D = x2.shape
    tm = TM_PROJ
    C = w_ext.shape[1]
    tiles_per_seq = S // tm
    row = lambda i: (i, 0)
    t_rows = (0, 0, 0, widths[3], 0, widths[5] // LANES * VT_ROWS)
    out_shape, out_specs = [], []
    for w, tr in zip(widths, t_rows):
        if tr:
            out_shape.append(jax.ShapeDtypeStruct((N // S, tiles_per_seq, tr, tm), BF16))
            out_specs.append(pl.BlockSpec((1, 1, tr, tm), lambda i: (i // tiles_per_seq, i % tiles_per_seq, 0, 0)))
        else:
            out_shape.append(jax.ShapeDtypeStruct((N, w), BF16))
            out_specs.append(pl.BlockSpec((tm, w), row))
    return pl.pallas_call(
        _inproj_kernel,
        out_shape=out_shape,
        grid=(N // tm,),
        in_specs=[pl.BlockSpec((tm, D), row),
                  pl.BlockSpec((tm, 1), row),
                  pl.BlockSpec((1, LANES), lambda i: (0, 0)),
                  pl.BlockSpec((1, N_MOD, D), lambda i: (i // tiles_per_seq, 0, 0)),
                  pl.BlockSpec((1, D), lambda i: (0, 0)),
                  pl.BlockSpec((D, C), lambda i: (0, 0)),
                  pl.BlockSpec((1, C), lambda i: (0, 0))],
        out_specs=out_specs,
        compiler_params=_cparams(("arbitrary",)),
        name="inproj",
    )(x2, pos2, inv_lane, mod3, g_mix, w_ext, b_ext)


def _swa_kernel(sink_ref, q_ref, kc_ref, kp_ref, vc_ref, vp_ref, o_ref):
    i = pl.program_id(1)
    tq = q_ref.shape[0]
    lane = lax.broadcasted_iota(jnp.int32, (1, LANES), 1)
    lo = lane < HEAD_DIM
    qi = lax.broadcasted_iota(jnp.int32, (WINDOW, 2 * WINDOW), 0) + WINDOW
    kj = lax.broadcasted_iota(jnp.int32, (WINDOW, 2 * WINDOW), 1)
    band = (qi - kj >= 0) & (qi - kj < WINDOW)
    dn = (((1,), (1,)), ((), ()))
    for c in range(tq // WINDOW):
        if c == 0:
            kcat = jnp.concatenate([kp_ref[...], kc_ref[0:WINDOW, :]], axis=0)
            vcat = jnp.concatenate([vp_ref[...], vc_ref[0:WINDOW, :]], axis=0)
            mask = band & (kj >= jnp.where(i > 0, 0, WINDOW))
        else:
            kcat = kc_ref[(c - 1) * WINDOW:(c + 1) * WINDOW, :]
            vcat = vc_ref[(c - 1) * WINDOW:(c + 1) * WINDOW, :]
            mask = band
        for j in range(SWA_KV_HEADS):
            kj2 = kcat[:, j * LANES:(j + 1) * LANES]
            vj2 = vcat[:, j * LANES:(j + 1) * LANES]
            zero = jnp.zeros_like(kj2)
            k_halves = (jnp.where(lo, kj2, zero), jnp.where(lo, zero, kj2))
            v_halves = (jnp.where(lo, vj2, zero), jnp.where(lo, zero, vj2))
            for p in range(SWA_GROUP // 2):
                g = j * (SWA_GROUP // 2) + p
                q = q_ref[c * WINDOW:(c + 1) * WINDOW, g * LANES:(g + 1) * LANES]
                out = jnp.zeros((WINDOW, LANES), F32)
                for half in range(2):
                    sink = sink_ref[2 * g + half]
                    s = lax.dot_general(q, k_halves[half], dn, preferred_element_type=F32)
                    s = jnp.where(mask, s, NEG_INF)
                    m = jnp.maximum(jnp.max(s, axis=1, keepdims=True), sink)
                    e = jnp.exp(s - m)
                    denom = jnp.sum(e, axis=1, keepdims=True) + jnp.exp(sink - m)
                    pv = jnp.dot(e.astype(BF16), v_halves[half], preferred_element_type=F32)
                    out = out + pv / denom
                o_ref[c * WINDOW:(c + 1) * WINDOW, g * LANES:(g + 1) * LANES] = out.astype(o_ref.dtype)


def _swa(sinks, qa, ka2, va2, B, S):
    N = qa.shape[0]
    tq = TQ_SWA
    nq = S // tq
    wpt = tq // WINDOW
    wps = S // WINDOW
    cur = lambda b, i: (b * nq + i, 0)
    prev = lambda b, i: (b * wps + jnp.maximum(i * wpt - 1, 0), 0)
    return pl.pallas_call(
        _swa_kernel,
        out_shape=jax.ShapeDtypeStruct((N, qa.shape[1]), BF16),
        grid=(B, nq),
        in_specs=[pl.BlockSpec(memory_space=pltpu.SMEM),
                  pl.BlockSpec((tq, qa.shape[1]), cur),
                  pl.BlockSpec((tq, ka2.shape[1]), cur),
                  pl.BlockSpec((WINDOW, ka2.shape[1]), prev),
                  pl.BlockSpec((tq, va2.shape[1]), cur),
                  pl.BlockSpec((WINDOW, va2.shape[1]), prev)],
        out_specs=pl.BlockSpec((tq, qa.shape[1]), cur),
        compiler_params=_cparams(("arbitrary", "arbitrary")),
        name="swa",
    )(sinks, qa, ka2, ka2, va2, va2)


def _diff_kernel(lam_ref, g_ref, qt_ref, k_ref, vt_ref, o_ref, sa_ref, sb_ref, m_ref, acc_ref, *, lambda_init):
    i = pl.program_id(2)
    tq = qt_ref.shape[3]
    tk = vt_ref.shape[3]
    qt = qt_ref[0, 0]
    lane = lax.broadcasted_iota(jnp.int32, (1, LANES), 1)
    lo = lane < HEAD_DIM
    m_ref[...] = jnp.full(m_ref.shape, NEG_INF, F32)
    acc_ref[...] = jnp.zeros(acc_ref.shape, F32)

    def scores(c, s_ref):
        k = k_ref[pl.ds(pl.multiple_of(c * tk, tk), tk), :]
        zero = jnp.zeros_like(k)
        s_ref[0] = jnp.dot(jnp.where(lo, k, zero), qt, preferred_element_type=F32)
        s_ref[1] = jnp.dot(jnp.where(lo, zero, k), qt, preferred_element_type=F32)

    def consume(c, s_ref, first_key):
        vt = vt_ref[0, c]
        for mp in range(2):
            s = s_ref[mp]
            if first_key is not None:
                kpos = lax.broadcasted_iota(jnp.int32, (tk, tq), 0) + first_key
                qpos = lax.broadcasted_iota(jnp.int32, (tk, tq), 1)
                s = jnp.where(kpos <= qpos, s, NEG_INF)
            m_prev = m_ref[mp]
            m_new = jnp.maximum(m_prev, jnp.max(s, axis=0, keepdims=True))
            alpha = jnp.exp2(m_prev - m_new)
            p = jnp.exp2(s - m_new).astype(BF16)
            acc_ref[mp] = alpha * acc_ref[mp] + jnp.dot(vt, p, preferred_element_type=F32)
            m_ref[mp] = m_new

    scores(0, sa_ref)

    def pair(jj, carry):
        c = 2 * jj
        scores(c + 1, sb_ref)
        consume(c, sa_ref, None)
        scores(c + 2, sa_ref)
        consume(c + 1, sb_ref, None)
        return carry

    lax.fori_loop(0, lax.shift_right_logical(i, 1), pair, 0)

    @pl.when(i % 2 == 0)
    def _():
        consume(i, sa_ref, 0)

    @pl.when(i % 2 == 1)
    def _():
        scores(i, sb_ref)
        consume(i - 1, sa_ref, None)
        consume(i, sb_ref, 0)

    lq1, lk1, lq2, lk2 = (lam_ref[r:r + 1, :] for r in range(4))
    lam = (jnp.exp(jnp.sum(lq1 * lk1, axis=1, keepdims=True))
           - jnp.exp(jnp.sum(lq2 * lk2, axis=1, keepdims=True)) + lambda_init)
    d = DIFF_V_DIM
    ot = (acc_ref[0, 0:d, :] / acc_ref[0, d:d + 1, :]
          - lam * (acc_ref[1, 0:d, :] / acc_ref[1, d:d + 1, :]))
    ot = ot * lax.rsqrt(jnp.mean(ot * ot, axis=0, keepdims=True) + EPS)
    o_ref[...] = (ot.T * g_ref[...] * (1.0 - lambda_init)).astype(o_ref.dtype)


def _diffattn(lam_vecs, g_subln, qdt, kd, vdt, B, S, lambda_init):
    N, C = kd.shape
    tq = qdt.shape[3]
    nq = S // tq
    return pl.pallas_call(
        functools.partial(_diff_kernel, lambda_init=lambda_init),
        out_shape=jax.ShapeDtypeStruct((N, C), BF16),
        grid=(B, DIFF_HEADS, nq),
        in_specs=[pl.BlockSpec((4, HEAD_DIM), lambda b, h, i: (0, 0)),
                  pl.BlockSpec((1, DIFF_V_DIM), lambda b, h, i: (0, 0)),
                  pl.BlockSpec((1, 1, LANES, tq), lambda b, h, i: (b, i, h, 0)),
                  pl.BlockSpec((S, LANES), lambda b, h, i: (b, h)),
                  pl.BlockSpec((1, nq, VT_ROWS, tq), lambda b, h, i: (b, 0, h, 0))],
        out_specs=pl.BlockSpec((tq, LANES), lambda b, h, i: (b * nq + i, h)),
        scratch_shapes=[pltpu.VMEM((2, tq, tq), F32), pltpu.VMEM((2, tq, tq), F32),
                        pltpu.VMEM((2, 1, tq), F32), pltpu.VMEM((2, VT_ROWS, tq), F32)],
        compiler_params=_cparams(("arbitrary", "arbitrary", "arbitrary")),
        name="diffattn",
    )(lam_vecs, g_subln, qdt, kd, vdt)


def _outproj_kernel(oa_ref, ob_ref, x_ref, mod_ref, wo_ref, bo_ref, g_ref, wr_ref, br_ref,
                    x1_ref, h2_ref, idx_ref, gate_ref, rank_ref, cnt_ref, carry_ref):
    i = pl.program_id(0)
    tm = x_ref.shape[0]
    half = oa_ref.shape[1]

    @pl.when(i == 0)
    def _():
        carry_ref[...] = jnp.zeros(carry_ref.shape, F32)

    gt1 = mod_ref[0, 2:3, :]
    sh2 = mod_ref[0, 3:4, :]
    sc2 = mod_ref[0, 4:5, :]
    mixed = (jnp.dot(oa_ref[...], wo_ref[0:half, :], preferred_element_type=F32)
             + jnp.dot(ob_ref[...], wo_ref[half:, :], preferred_element_type=F32) + bo_ref[...])
    x1 = x_ref[...] + gt1 * mixed
    x1_ref[...] = x1
    h2 = _rms(x1) * g_ref[...] * (1.0 + sc2) + sh2
    dn = (((1,), (1,)), ((), ()))
    h_hi = h2.astype(BF16)
    h2_ref[...] = _pack_bf16_pairs(h2)
    h_lo = (h2 - h_hi.astype(F32)).astype(BF16)
    w = wr_ref[...]
    w_hi = w.astype(BF16)
    w_lo = (w - w_hi.astype(F32)).astype(BF16)
    logits = (lax.dot_general(w_hi, h_hi, dn, preferred_element_type=F32)
              + lax.dot_general(w_hi, h_lo, dn, preferred_element_type=F32)
              + lax.dot_general(w_lo, h_hi, dn, preferred_element_type=F32)
              + br_ref[...])

    eidx = lax.broadcasted_iota(jnp.int32, logits.shape, 0)
    vals = logits
    onehots, top_vals, top_idx = [], [], []
    for _k in range(TOP_K):
        mx = jnp.max(vals, axis=0, keepdims=True)
        sel = jnp.min(jnp.where(vals == mx, eidx, N_EXPERTS), axis=0, keepdims=True)
        oh = eidx == sel
        onehots.append(oh)
        top_vals.append(mx)
        top_idx.append(sel)
        vals = jnp.where(oh, NEG_INF, vals)
    exps = [jnp.exp(v - top_vals[0]) for v in top_vals]
    denom = exps[0] + exps[1] + exps[2] + exps[3]
    gate_ref[...] = jnp.concatenate([e / denom for e in exps], axis=0)
    idx_ref[...] = jnp.concatenate(top_idx, axis=0)

    member = (onehots[0] | onehots[1] | onehots[2] | onehots[3])
    member_f = member.astype(F32)
    t_src = lax.broadcasted_iota(jnp.int32, (tm, tm), 0)
    t_dst = lax.broadcasted_iota(jnp.int32, (tm, tm), 1)
    before = (t_src < t_dst).astype(BF16)
    prefix = jnp.dot(member.astype(BF16), before, preferred_element_type=F32) + carry_ref[...]
    ranks = [jnp.sum(jnp.where(oh, prefix, 0.0), axis=0, keepdims=True) for oh in onehots]
    rank_ref[...] = jnp.concatenate(ranks, axis=0).astype(jnp.int32)
    carry_ref[...] = carry_ref[...] + jnp.sum(member_f, axis=1, keepdims=True)
    cnt_ref[...] = jnp.broadcast_to(carry_ref[...], cnt_ref.shape)


def _outproj(out_a, out_b, x2, mod3, w_out, b_out, g_ffn, wr_t, br_col, S):
    N, D = x2.shape
    tm = TM_OUT
    tiles_per_seq = S // tm
    row = lambda i: (i, 0)
    colb = lambda i: (0, i)
    const = lambda i: (0, 0)
    return pl.pallas_call(
        _outproj_kernel,
        out_shape=[jax.ShapeDtypeStruct((N, D), F32), jax.ShapeDtypeStruct((N, D // 2), jnp.int32),
                   jax.ShapeDtypeStruct((TOP_K, N), jnp.int32), jax.ShapeDtypeStruct((TOP_K, N), F32),
                   jax.ShapeDtypeStruct((TOP_K, N), jnp.int32), jax.ShapeDtypeStruct((N_EXPERTS, LANES), F32)],
        grid=(N // tm,),
        in_specs=[pl.BlockSpec((tm, out_a.shape[1]), row),
                  pl.BlockSpec((tm, out_b.shape[1]), row),
                  pl.BlockSpec((tm, D), row),
                  pl.BlockSpec((1, N_MOD, D), lambda i: (i // tiles_per_seq, 0, 0)),
                  pl.BlockSpec(w_out.shape, const),
                  pl.BlockSpec((1, D), const),
                  pl.BlockSpec((1, D), const),
                  pl.BlockSpec(wr_t.shape, const),
                  pl.BlockSpec((N_EXPERTS, 1), const)],
        out_specs=[pl.BlockSpec((tm, D), row), pl.BlockSpec((tm, D // 2), row),
                   pl.BlockSpec((TOP_K, tm), colb), pl.BlockSpec((TOP_K, tm), colb),
                   pl.BlockSpec((TOP_K, tm), colb), pl.BlockSpec((N_EXPERTS, LANES), const)],
        scratch_shapes=[pltpu.VMEM((N_EXPERTS, 1), F32)],
        compiler_params=_cparams(("arbitrary",)),
        name="outproj_router",
    )(out_a, out_b, x2, mod3, w_out, b_out, g_ffn, wr_t, br_col)


def _experts_kernel(first_ref, nblk_ref, x_ref, w1_ref, b1g_ref, b1l_ref, w2_ref, b2_ref, y_ref,
                    w1g_a, w1l_a, w2_a, w1g_b, w1l_b, w2_b, xbuf, ybuf, xsem, ysem):
    s = pl.program_id(0)
    n_exp = pl.num_programs(0) - 1
    e = jnp.maximum(s - 1, 0)
    first = first_ref[e]
    n_blk = jnp.where(s >= 1, nblk_ref[e], 0)
    nxt = jnp.minimum(s, n_exp - 1)
    do_prep = (s < n_exp) & (nblk_ref[nxt] > 0)
    bm = xbuf.shape[1]

    def x_copy(j, slot, first_blk=None):
        blk = (first if first_blk is None else first_blk) + j
        rows = pl.ds(pl.multiple_of(blk * bm, bm), bm)
        return pltpu.make_async_copy(x_ref.at[rows], xbuf.at[slot], xsem.at[slot])

    def y_copy(j, slot):
        rows = pl.ds(pl.multiple_of((first + j) * bm, bm), bm)
        return pltpu.make_async_copy(ybuf.at[slot], y_ref.at[rows], ysem.at[slot])

    def swiglu(ug, ul):
        glu = jnp.minimum(ug, SWIGLU_LIMIT)
        lin = jnp.clip(ul, -SWIGLU_LIMIT, SWIGLU_LIMIT)
        return (glu * jax.nn.sigmoid(SWIGLU_ALPHA * glu) * (lin + 1.0)).astype(BF16)

    def relayout(new):
        w1g_s, w1l_s, w2_s = new
        ch = 256
        half = ch // 2
        for c in range(w1_ref.shape[2] // ch):
            t = w1_ref[0, :, c * ch:(c + 1) * ch].astype(BF16).T
            pairs = pltpu.bitcast(t, jnp.int32)
            cols = slice(c * half, (c + 1) * half)
            w1g_s[:, cols] = lax.bitcast_convert_type(lax.shift_left(pairs, 16), F32).astype(BF16).T
            w1l_s[:, cols] = lax.bitcast_convert_type(pairs & jnp.int32(HIGH_HALF), F32).astype(BF16).T
        w2_s[...] = w2_ref[0].astype(BF16)

    def block(j, cur, new=None):
        w1g_s, w1l_s, w2_s = cur
        slot = j & 1
        x_copy(j, slot).wait()

        @pl.when(j + 1 < n_blk)
        def _():
            x_copy(j + 1, 1 - slot).start(priority=1)

        @pl.when(j >= 2)
        def _():
            y_copy(j - 2, slot).wait()

        if new is not None:
            relayout(new)
        x = _unpack_bf16_pairs(xbuf[slot]).astype(BF16)
        ug = jnp.dot(x, w1g_s[...], preferred_element_type=F32) + b1g_ref[0]
        ul = jnp.dot(x, w1l_s[...], preferred_element_type=F32) + b1l_ref[0]
        y = jnp.dot(swiglu(ug, ul), w2_s[...], preferred_element_type=F32) + b2_ref[0]
        ybuf[slot] = _pack_bf16_pairs(y)
        y_copy(j, slot).start(priority=1)

    def step(new, cur):
        @pl.when((n_blk > 0) & do_prep)
        def _():
            block(0, cur, new)

        @pl.when((n_blk > 0) & jnp.logical_not(do_prep))
        def _():
            block(0, cur)

        @pl.when((n_blk == 0) & do_prep)
        def _():
            relayout(new)

        def later_block(j, carry):
            block(j, cur)
            return carry

        lax.fori_loop(1, n_blk, later_block, 0)

    set_a, set_b = (w1g_a, w1l_a, w2_a), (w1g_b, w1l_b, w2_b)

    @pl.when(s % 2 == 0)
    def _():
        step(set_a, set_b)

    @pl.when(s % 2 == 1)
    def _():
        step(set_b, set_a)

    @pl.when(do_prep)
    def _():
        x_copy(0, 0, first_ref[nxt]).start(priority=1)

    @pl.when(n_blk >= 2)
    def _():
        y_copy(n_blk - 2, n_blk & 1).wait()

    @pl.when(n_blk >= 1)
    def _():
        y_copy(n_blk - 1, (n_blk - 1) & 1).wait()


def _experts(first_blk, n_blk, x_rows, w1, b1g, b1l, w2, b2, bm):
    E, Fh, D = w2.shape
    n_rows = x_rows.shape[0]
    wsel = lambda s, fb, nb: (jnp.minimum(s, E - 1), 0, 0)
    bsel = lambda s, fb, nb: (jnp.maximum(s - 1, 0), 0, 0)
    return pl.pallas_call(
        _experts_kernel,
        out_shape=jax.ShapeDtypeStruct((n_rows, D // 2), jnp.int32),
        grid_spec=pltpu.PrefetchScalarGridSpec(
            num_scalar_prefetch=2,
            grid=(E + 1,),
            in_specs=[pl.BlockSpec(memory_space=pl.ANY),
                      pl.BlockSpec((1, D, 2 * Fh), wsel),
                      pl.BlockSpec((1, 1, Fh), bsel),
                      pl.BlockSpec((1, 1, Fh), bsel),
                      pl.BlockSpec((1, Fh, D), wsel),
                      pl.BlockSpec((1, 1, D), bsel)],
            out_specs=pl.BlockSpec(memory_space=pl.ANY),
            scratch_shapes=[pltpu.VMEM((D, Fh), BF16), pltpu.VMEM((D, Fh), BF16), pltpu.VMEM((Fh, D), BF16),
                            pltpu.VMEM((D, Fh), BF16), pltpu.VMEM((D, Fh), BF16), pltpu.VMEM((Fh, D), BF16),
                            pltpu.VMEM((2, bm, D // 2), jnp.int32), pltpu.VMEM((2, bm, D // 2), jnp.int32),
                            pltpu.SemaphoreType.DMA((2,)), pltpu.SemaphoreType.DMA((2,))]),
        compiler_params=_cparams(("arbitrary",), 56 * 1024 * 1024),
        name="experts",
    )(first_blk, n_blk, x_rows, w1, b1g, b1l, w2, b2)


SC_CORES = 2
SC_SUBCORES = 16
SC_CHUNK = 64


def _sc_gather_rows(table, idx):
    M = idx.shape[0]
    D = table.shape[1]
    workers = SC_CORES * SC_SUBCORES
    per_worker = M // workers
    n_chunks = per_worker // SC_CHUNK
    assert M % workers == 0 and per_worker % (2 * SC_CHUNK) == 0
    mesh = plsc.VectorSubcoreMesh(core_axis_name="c", subcore_axis_name="s")

    @functools.partial(
        pl.kernel, mesh=mesh,
        out_type=jax.ShapeDtypeStruct((M, D), table.dtype),
        scratch_types=[pltpu.VMEM((per_worker,), jnp.int32),
                       pltpu.VMEM((SC_CHUNK, D), table.dtype), pltpu.VMEM((SC_CHUNK, D), table.dtype),
                       pltpu.SemaphoreType.DMA, pltpu.SemaphoreType.DMA],
        name="sc_gather_rows")
    def gather(table_hbm, idx_hbm, out_hbm, idx_v, rows0, rows1, sem0, sem1):
        wid = lax.axis_index("s") * SC_CORES + lax.axis_index("c")
        base = wid * per_worker
        pltpu.sync_copy(idx_hbm.at[pl.ds(base, per_worker)], idx_v)

        def fetch(c, buf, sem):
            off = pl.multiple_of(c * SC_CHUNK, SC_CHUNK)
            return pltpu.make_async_copy(table_hbm.at[idx_v.at[pl.ds(off, SC_CHUNK)]], buf, sem)

        def flush(c, buf):
            off = pl.multiple_of(c * SC_CHUNK, SC_CHUNK)
            pltpu.sync_copy(buf, out_hbm.at[pl.ds(base + off, SC_CHUNK)])

        fetch(0, rows0, sem0).start()

        @pl.loop(0, n_chunks // 2)
        def _(jj):
            c = 2 * jj
            fetch(c + 1, rows1, sem1).start()
            fetch(c, rows0, sem0).wait()
            flush(c, rows0)

            @pl.when(c + 2 < n_chunks)
            def _():
                fetch(c + 2, rows0, sem0).start()

            fetch(c + 1, rows1, sem1).wait()
            flush(c + 1, rows1)

    return gather(table, idx)


SC_SCATTER_CHUNK = 128


def _sc_scatter_rows(rows, dest, n_rows):
    N, W = rows.shape
    workers = SC_CORES * SC_SUBCORES
    per_worker = N // workers
    chunks = per_worker // SC_SCATTER_CHUNK
    assert N % workers == 0 and per_worker % SC_SCATTER_CHUNK == 0
    dest3 = dest.reshape(TOP_K, N // SC_SCATTER_CHUNK, SC_SCATTER_CHUNK)
    mesh = plsc.VectorSubcoreMesh(core_axis_name="c", subcore_axis_name="s")

    @functools.partial(
        pl.kernel, mesh=mesh,
        out_type=jax.ShapeDtypeStruct((n_rows, W), rows.dtype),
        scratch_types=[pltpu.VMEM((TOP_K, chunks, SC_SCATTER_CHUNK), jnp.int32),
                       pltpu.VMEM((SC_SCATTER_CHUNK, W), rows.dtype)],
        name="sc_scatter_rows")
    def scatter(rows_hbm, dest_hbm, out_hbm, idx_v, rows_v):
        wid = lax.axis_index("s") * SC_CORES + lax.axis_index("c")
        for k in range(TOP_K):
            pltpu.sync_copy(dest_hbm.at[k, pl.ds(wid * chunks, chunks)], idx_v.at[k])

        @pl.loop(0, chunks)
        def _(j):
            start = pl.multiple_of(wid * per_worker + j * SC_SCATTER_CHUNK, SC_SCATTER_CHUNK)
            pltpu.sync_copy(rows_hbm.at[pl.ds(start, SC_SCATTER_CHUNK)], rows_v)
            for k in range(TOP_K):
                pltpu.sync_copy(rows_v, out_hbm.at[idx_v.at[k, j]])

    return scatter(rows, dest3)


def _combine_kernel(x1_ref, gate_ref, mod_ref, g_ref, y_ref, o_ref, *, final_norm):
    gate = gate_ref[...]
    moe = gate[:, 0:1] * _unpack_bf16_pairs(y_ref[0])
    for k in range(1, TOP_K):
        moe = moe + gate[:, k:k + 1] * _unpack_bf16_pairs(y_ref[k])
    gt2 = mod_ref[0, 5:6, :]
    x2 = x1_ref[...] + gt2 * moe
    o_ref[...] = _rms(x2) * g_ref[...] if final_norm else x2


def _combine(x1, gate_t, mod3, g_final, y_tok, S, final_norm):
    N, D = x1.shape
    tm = TM_ROWS
    tiles_per_seq = S // tm
    row = lambda i: (i, 0)
    return pl.pallas_call(
        functools.partial(_combine_kernel, final_norm=final_norm),
        out_shape=jax.ShapeDtypeStruct((N, D), F32),
        grid=(N // tm,),
        in_specs=[pl.BlockSpec((tm, D), row),
                  pl.BlockSpec((tm, TOP_K), row),
                  pl.BlockSpec((1, N_MOD, D), lambda i: (i // tiles_per_seq, 0, 0)),
                  pl.BlockSpec((1, D), lambda i: (0, 0)),
                  pl.BlockSpec((TOP_K, tm, D // 2), lambda i: (0, i, 0))],
        out_specs=pl.BlockSpec((tm, D), row),
        compiler_params=_cparams(("arbitrary",)),
        name="combine",
    )(x1, gate_t, mod3, g_final, y_tok)


def _routing_tables(counts, idx, rank, bm):
    counts = counts.astype(jnp.int32)
    padded = (counts + bm - 1) // bm * bm
    pends = jnp.cumsum(padded)
    pstarts = pends - padded
    experts = jnp.arange(N_EXPERTS, dtype=jnp.int32)
    dest = jnp.sum(jnp.where(idx[..., None] == experts, pstarts, 0), axis=-1) + rank
    return dest.astype(jnp.int32), (pstarts // bm).astype(jnp.int32), (padded // bm).astype(jnp.int32)


def _extended_in_weights(w_in, b_in):
    a_q = SWA_Q_HEADS * HEAD_DIM
    a_kv = SWA_KV_HEADS * HEAD_DIM
    b_w = DIFF_HEADS * DIFF_V_DIM
    spans = [(0, a_q)]
    for base in (a_q, a_q + a_kv):
        for j in range(SWA_KV_HEADS):
            spans += [(base + j * HEAD_DIM, base + (j + 1) * HEAD_DIM)] * 2
    spans.append((a_q + 2 * a_kv, a_q + 2 * a_kv + 3 * b_w))
    w_ext = jnp.concatenate([w_in[:, lo:hi] for lo, hi in spans], axis=1).astype(BF16)
    b_ext = jnp.concatenate([b_in[lo:hi] for lo, hi in spans]).reshape(1, -1)
    widths = (a_q, 2 * a_kv, 2 * a_kv, b_w, b_w, b_w)
    return w_ext, b_ext, widths


def kernel(x, c, positions, w_ada, b_ada, g_mix, w_in, b_in, attn_sinks, lambda_q1, lambda_k1, lambda_q2,
           lambda_k2, g_subln, w_out, b_out, g_ffn, w_router, b_router, w1, b1, w2, b2, g_final):
    B, S, D = x.shape
    N = B * S
    depth = w_ada.shape[0]
    bm = _block_rows(N)
    n_rows = (N * TOP_K + N_EXPERTS * (bm - 1) + bm - 1) // bm * bm

    inv = 1.0 / (ROPE_THETA ** (jnp.arange(0, HEAD_DIM, 2, dtype=F32) / HEAD_DIM))
    inv_lane = jnp.tile(inv, LANES // (HEAD_DIM // 2)).reshape(1, LANES)
    pos2 = positions.reshape(N, 1)
    xcur = x.reshape(N, D)

    for layer in range(depth):
        last = layer == depth - 1
        lambda_init = 0.8 - 0.6 * math.exp(-0.3 * layer)
        mod3 = _adaln(c, w_ada[layer], b_ada[layer]).reshape(B, N_MOD, D)

        w_ext, b_ext, widths = _extended_in_weights(w_in[layer], b_in[layer])
        qa, ka2, va2, qdt, kd, vdt = _inproj(xcur, pos2, inv_lane, mod3, g_mix[layer].reshape(1, D),
                                             w_ext, b_ext, S, widths)
        out_a = _swa(attn_sinks[layer], qa, ka2, va2, B, S)
        lam_vecs = jnp.stack([lambda_q1[layer], lambda_k1[layer], lambda_q2[layer], lambda_k2[layer]])
        out_b = _diffattn(lam_vecs, g_subln[layer].reshape(1, DIFF_V_DIM), qdt, kd, vdt, B, S, lambda_init)

        x1, h2, idx, gate, rank, counts = _outproj(
            out_a, out_b, xcur, mod3, w_out[layer].astype(BF16), b_out[layer].reshape(1, D),
            g_ffn[layer].reshape(1, D), w_router[layer].T, b_router[layer].reshape(N_EXPERTS, 1), S)

        dest, first_blk, n_blk = _routing_tables(counts[:, 0], idx, rank, bm)
        x_rows = _sc_scatter_rows(h2, dest, n_rows)
        y_rows = _experts(first_blk, n_blk, x_rows, w1[layer].astype(BF16),
                          b1[layer][:, None, 0::2], b1[layer][:, None, 1::2],
                          w2[layer].astype(BF16), b2[layer][:, None, :], bm)
        y_tok = _sc_gather_rows(y_rows, dest.reshape(-1)).reshape(TOP_K, N, D // 2)
        xcur = _combine(x1, gate.T, mod3, g_final.reshape(1, D), y_tok, S, final_norm=last)
    return xcur.reshape(B, S, D)
```

```python
import functools
import math

import jax
import jax.numpy as jnp
from jax import lax
from jax.experimental import pallas as pl
from jax.experimental.pallas import tpu as pltpu
from jax.experimental.pallas import tpu_sc as plsc

HEAD_DIM = 64
SWA_Q_HEADS = 8
SWA_KV_HEADS = 2
SWA_GROUP = SWA_Q_HEADS // SWA_KV_HEADS
WINDOW = 128
DIFF_HEADS = 4
DIFF_V_DIM = 2 * HEAD_DIM
ROPE_THETA = 10000.0
N_EXPERTS = 32
TOP_K = 4
SWIGLU_ALPHA = 1.702
SWIGLU_LIMIT = 7.0
EPS = 1e-5
N_MOD = 6

LANES = 128
F32 = jnp.float32
BF16 = jnp.bfloat16
NEG_INF = float("-inf")

TM_PROJ = 1024
TQ_SWA = 512
VT_ROWS = DIFF_V_DIM + 16
TM_OUT = 512
TM_ROWS = 256
BLOCK_ROWS = 512
VMEM_LIMIT = 48 * 1024 * 1024


def _cparams(sem, vmem=VMEM_LIMIT):
    return pltpu.CompilerParams(dimension_semantics=sem, vmem_limit_bytes=vmem)


def _adaln_kernel(ct_ref, w_ref, b_ref, o_ref):
    c = ct_ref[...]
    cond = c * jax.nn.sigmoid(c)
    w = w_ref[...]
    rows = [jnp.sum(w * cond[:, b:b + 1], axis=0, keepdims=True) for b in range(c.shape[1])]
    o_ref[...] = jnp.concatenate(rows, axis=0) + b_ref[...]


def _adaln(c, w_ada, b_ada):
    B, D = c.shape
    n_out = w_ada.shape[1]
    tn = 1024
    return pl.pallas_call(
        _adaln_kernel,
        out_shape=jax.ShapeDtypeStruct((B, n_out), F32),
        grid=(n_out // tn,),
        in_specs=[pl.BlockSpec((D, B), lambda j: (0, 0)),
                  pl.BlockSpec((D, tn), lambda j: (0, j)),
                  pl.BlockSpec((1, tn), lambda j: (0, j))],
        out_specs=pl.BlockSpec((B, tn), lambda j: (0, j)),
        compiler_params=_cparams(("arbitrary",)),
        name="adaln",
    )(c.T, w_ada, b_ada.reshape(1, n_out))


def _rms(x):
    return x * lax.rsqrt(jnp.mean(x * x, axis=-1, keepdims=True) + EPS)


HIGH_HALF = -65536


def _pack_bf16_pairs(v):
    bits = lax.bitcast_convert_type(v.astype(BF16).astype(F32), jnp.int32)
    half = v.shape[1] // 2
    return lax.shift_right_logical(bits[:, :half], 16) | (bits[:, half:] & jnp.int32(HIGH_HALF))


def _unpack_bf16_pairs(w):
    return jnp.concatenate([lax.bitcast_convert_type(lax.shift_left(w, 16), F32),
                            lax.bitcast_convert_type(w & jnp.int32(HIGH_HALF), F32)], axis=1)


def _inproj_kernel(x_ref, pos_ref, inv_ref, mod_ref, g_ref, w_ref, b_ref,
                   qa_ref, ka_ref, va_ref, qd_ref, kd_ref, vd_ref):
    x = x_ref[...]
    sh = mod_ref[0, 0:1, :]
    sc = mod_ref[0, 1:2, :]
    h = _rms(x) * g_ref[...] * (1.0 + sc) + sh
    proj = jnp.dot(h.astype(BF16), w_ref[...], preferred_element_type=F32) + b_ref[...]

    lane = lax.broadcasted_iota(jnp.int32, (1, LANES), 1)
    first_half = (lane & (HEAD_DIM - 1)) < (HEAD_DIM // 2)
    n_freq = HEAD_DIM // 2
    groups = LANES // n_freq
    tm = x.shape[0]
    rows = tm // groups
    group = lax.shift_right_logical(lane, n_freq.bit_length() - 1)
    pos = pos_ref[...].astype(F32)
    pos_q = pos[0:rows]
    for g in range(1, groups):
        pos_q = jnp.where(group == g, pos[g * rows:(g + 1) * rows], pos_q)
    ang_q = pos_q * inv_ref[...]

    def spread(table_q):
        blocks = []
        for g in range(groups):
            only = jnp.where(group == g, table_q, 0.0)
            full = only
            for r in range(1, groups):
                full = full + pltpu.roll(only, r * n_freq, axis=1)
            blocks.append(full)
        return jnp.concatenate(blocks, axis=0)

    cos = spread(jnp.cos(ang_q))
    sin = spread(jnp.sin(ang_q))
    sin_signed = jnp.where(first_half, -sin, sin)

    def rope(t):
        partner = jnp.where(first_half,
                            pltpu.roll(t, LANES - HEAD_DIM // 2, axis=1),
                            pltpu.roll(t, HEAD_DIM // 2, axis=1))
        return t * cos + partner * sin_signed

    lo_half = lane < HEAD_DIM

    def emit(out_ref, col0, width, rotary, scale, transposed, doubled=False):
        for j in range(width // LANES):
            t = proj[:, col0 + j * LANES: col0 + (j + 1) * LANES]
            if rotary:
                t = rope(t)
            if scale != 1.0:
                t = t * scale
            if doubled:
                swapped = pltpu.roll(t, HEAD_DIM, axis=1)
                out_ref[:, 2 * j * LANES:(2 * j + 1) * LANES] = jnp.where(lo_half, t, swapped).astype(out_ref.dtype)
                out_ref[:, (2 * j + 1) * LANES:(2 * j + 2) * LANES] = jnp.where(lo_half, swapped, t).astype(out_ref.dtype)
            elif transposed:
                rows = out_ref.shape[2] // (width // LANES)
                out_ref[0, 0, j * rows:j * rows + LANES, :] = t.T.astype(out_ref.dtype)
                if rows > LANES:
                    fill = lax.broadcasted_iota(jnp.int32, (rows - LANES, t.shape[0]), 0) == 0
                    out_ref[0, 0, j * rows + LANES:(j + 1) * rows, :] = fill.astype(out_ref.dtype)
            else:
                out_ref[:, j * LANES:(j + 1) * LANES] = t.astype(out_ref.dtype)

    swa_scale = 1.0 / math.sqrt(HEAD_DIM)
    diff_scale = math.log2(math.e) / math.sqrt(HEAD_DIM)
    col = 0
    for out_ref, width, rotary, scale, transposed, doubled in (
            (qa_ref, qa_ref.shape[1], True, swa_scale, False, False),
            (ka_ref, ka_ref.shape[1] // 2, True, 1.0, False, True),
            (va_ref, va_ref.shape[1] // 2, False, 1.0, False, True),
            (qd_ref, qd_ref.shape[2], True, diff_scale, True, False),
            (kd_ref, kd_ref.shape[1], True, 1.0, False, False),
            (vd_ref, vd_ref.shape[2] // VT_ROWS * LANES, False, 1.0, True, False)):
        emit(out_ref, col, width, rotary, scale, transposed, doubled)
        col += width


def _inproj(x2, pos2, inv_lane, mod3, g_mix, w_ext, b_ext, S, widths):
    N, D = x2.shape
    tm = TM_PROJ
    C = w_ext.shape[1]
    tiles_per_seq = S // tm
    row = lambda i: (i, 0)
    t_rows = (0, 0, 0, widths[3], 0, widths[5] // LANES * VT_ROWS)
    out_shape, out_specs = [], []
    for w, tr in zip(widths, t_rows):
        if tr:
            out_shape.append(jax.ShapeDtypeStruct((N // S, tiles_per_seq, tr, tm), BF16))
            out_specs.append(pl.BlockSpec((1, 1, tr, tm), lambda i: (i // tiles_per_seq, i % tiles_per_seq, 0, 0)))
        else:
            out_shape.append(jax.ShapeDtypeStruct((N, w), BF16))
            out_specs.append(pl.BlockSpec((tm, w), row))
    return pl.pallas_call(
        _inproj_kernel,
        out_shape=out_shape,
        grid=(N // tm,),
        in_specs=[pl.BlockSpec((tm, D), row),
                  pl.BlockSpec((tm, 1), row),
                  pl.BlockSpec((1, LANES), lambda i: (0, 0)),
                  pl.BlockSpec((1, N_MOD, D), lambda i: (i // tiles_per_seq, 0, 0)),
                  pl.BlockSpec((1, D), lambda i: (0, 0)),
                  pl.BlockSpec((D, C), lambda i: (0, 0)),
                  pl.BlockSpec((1, C), lambda i: (0, 0))],
        out_specs=out_specs,
        compiler_params=_cparams(("arbitrary",)),
        name="inproj",
    )(x2, pos2, inv_lane, mod3, g_mix, w_ext, b_ext)


def _swa_kernel(sink_ref, q_ref, kc_ref, kp_ref, vc_ref, vp_ref, o_ref):
    i = pl.program_id(1)
    tq = q_ref.shape[0]
    lane = lax.broadcasted_iota(jnp.int32, (1, LANES), 1)
    lo = lane < HEAD_DIM
    qi = lax.broadcasted_iota(jnp.int32, (WINDOW, 2 * WINDOW), 0) + WINDOW
    kj = lax.broadcasted_iota(jnp.int32, (WINDOW, 2 * WINDOW), 1)
    band = (qi - kj >= 0) & (qi - kj < WINDOW)
    dn = (((1,), (1,)), ((), ()))
    for c in range(tq // WINDOW):
        if c == 0:
            kcat = jnp.concatenate([kp_ref[...], kc_ref[0:WINDOW, :]], axis=0)
            vcat = jnp.concatenate([vp_ref[...], vc_ref[0:WINDOW, :]], axis=0)
            mask = band & (kj >= jnp.where(i > 0, 0, WINDOW))
        else:
            kcat = kc_ref[(c - 1) * WINDOW:(c + 1) * WINDOW, :]
            vcat = vc_ref[(c - 1) * WINDOW:(c + 1) * WINDOW, :]
            mask = band
        for j in range(SWA_KV_HEADS):
            kj2 = kcat[:, j * LANES:(j + 1) * LANES]
            vj2 = vcat[:, j * LANES:(j + 1) * LANES]
            zero = jnp.zeros_like(kj2)
            k_halves = (jnp.where(lo, kj2, zero), jnp.where(lo, zero, kj2))
            v_halves = (jnp.where(lo, vj2, zero), jnp.where(lo, zero, vj2))
            for p in range(SWA_GROUP // 2):
                g = j * (SWA_GROUP // 2) + p
                q = q_ref[c * WINDOW:(c + 1) * WINDOW, g * LANES:(g + 1) * LANES]
                out = jnp.zeros((WINDOW, LANES), F32)
                for half in range(2):
                    sink = sink_ref[2 * g + half]
                    s = lax.dot_general(q, k_halves[half], dn, preferred_element_type=F32)
                    s = jnp.where(mask, s, NEG_INF)
                    m = jnp.maximum(jnp.max(s, axis=1, keepdims=True), sink)
                    e = jnp.exp(s - m)
                    denom = jnp.sum(e, axis=1, keepdims=True) + jnp.exp(sink - m)
                    pv = jnp.dot(e.astype(BF16), v_halves[half], preferred_element_type=F32)
                    out = out + pv / denom
                o_ref[c * WINDOW:(c + 1) * WINDOW, g * LANES:(g + 1) * LANES] = out.astype(o_ref.dtype)


def _swa(sinks, qa, ka2, va2, B, S):
    N = qa.shape[0]
    tq = TQ_SWA
    nq = S // tq
    wpt = tq // WINDOW
    wps = S // WINDOW
    cur = lambda b, i: (b * nq + i, 0)
    prev = lambda b, i: (b * wps + jnp.maximum(i * wpt - 1, 0), 0)
    return pl.pallas_call(
        _swa_kernel,
        out_shape=jax.ShapeDtypeStruct((N, qa.shape[1]), BF16),
        grid=(B, nq),
        in_specs=[pl.BlockSpec(memory_space=pltpu.SMEM),
                  pl.BlockSpec((tq, qa.shape[1]), cur),
                  pl.BlockSpec((tq, ka2.shape[1]), cur),
                  pl.BlockSpec((WINDOW, ka2.shape[1]), prev),
                  pl.BlockSpec((tq, va2.shape[1]), cur),
                  pl.BlockSpec((WINDOW, va2.shape[1]), prev)],
        out_specs=pl.BlockSpec((tq, qa.shape[1]), cur),
        compiler_params=_cparams(("arbitrary", "arbitrary")),
        name="swa",
    )(sinks, qa, ka2, ka2, va2, va2)


def _diff_kernel(lam_ref, g_ref, qt_ref, k_ref, vt_ref, o_ref, sa_ref, sb_ref, m_ref, acc_ref, *, lambda_init):
    i = pl.program_id(2)
    tq = qt_ref.shape[3]
    tk = vt_ref.shape[3]
    qt = qt_ref[0, 0]
    lane = lax.broadcasted_iota(jnp.int32, (1, LANES), 1)
    lo = lane < HEAD_DIM
    m_ref[...] = jnp.full(m_ref.shape, NEG_INF, F32)
    acc_ref[...] = jnp.zeros(acc_ref.shape, F32)

    def scores(c, s_ref):
        k = k_ref[pl.ds(pl.multiple_of(c * tk, tk), tk), :]
        zero = jnp.zeros_like(k)
        s_ref[0] = jnp.dot(jnp.where(lo, k, zero), qt, preferred_element_type=F32)
        s_ref[1] = jnp.dot(jnp.where(lo, zero, k), qt, preferred_element_type=F32)

    def consume(c, s_ref, first_key):
        vt = vt_ref[0, c]
        for mp in range(2):
            s = s_ref[mp]
            if first_key is not None:
                kpos = lax.broadcasted_iota(jnp.int32, (tk, tq), 0) + first_key
                qpos = lax.broadcasted_iota(jnp.int32, (tk, tq), 1)
                s = jnp.where(kpos <= qpos, s, NEG_INF)
            m_prev = m_ref[mp]
            m_new = jnp.maximum(m_prev, jnp.max(s, axis=0, keepdims=True))
            alpha = jnp.exp2(m_prev - m_new)
            p = jnp.exp2(s - m_new).astype(BF16)
            acc_ref[mp] = alpha * acc_ref[mp] + jnp.dot(vt, p, preferred_element_type=F32)
            m_ref[mp] = m_new

    scores(0, sa_ref)

    def pair(jj, carry):
        c = 2 * jj
        scores(c + 1, sb_ref)
        consume(c, sa_ref, None)
        scores(c + 2, sa_ref)
        consume(c + 1, sb_ref, None)
        return carry

    lax.fori_loop(0, lax.shift_right_logical(i, 1), pair, 0)

    @pl.when(i % 2 == 0)
    def _():
        consume(i, sa_ref, 0)

    @pl.when(i % 2 == 1)
    def _():
        scores(i, sb_ref)
        consume(i - 1, sa_ref, None)
        consume(i, sb_ref, 0)

    lq1, lk1, lq2, lk2 = (lam_ref[r:r + 1, :] for r in range(4))
    lam = (jnp.exp(jnp.sum(lq1 * lk1, axis=1, keepdims=True))
           - jnp.exp(jnp.sum(lq2 * lk2, axis=1, keepdims=True)) + lambda_init)
    d = DIFF_V_DIM
    ot = (acc_ref[0, 0:d, :] / acc_ref[0, d:d + 1, :]
          - lam * (acc_ref[1, 0:d, :] / acc_ref[1, d:d + 1, :]))
    ot = ot * lax.rsqrt(jnp.mean(ot * ot, axis=0, keepdims=True) + EPS)
    o_ref[...] = (ot.T * g_ref[...] * (1.0 - lambda_init)).astype(o_ref.dtype)


def _diffattn(lam_vecs, g_subln, qdt, kd, vdt, B, S, lambda_init):
    N, C = kd.shape
    tq = qdt.shape[3]
    nq = S // tq
    return pl.pallas_call(
        functools.partial(_diff_kernel, lambda_init=lambda_init),
        out_shape=jax.ShapeDtypeStruct((N, C), BF16),
        grid=(B, DIFF_HEADS, nq),
        in_specs=[pl.BlockSpec((4, HEAD_DIM), lambda b, h, i: (0, 0)),
                  pl.BlockSpec((1, DIFF_V_DIM), lambda b, h, i: (0, 0)),
                  pl.BlockSpec((1, 1, LANES, tq), lambda b, h, i: (b, i, h, 0)),
                  pl.BlockSpec((S, LANES), lambda b, h, i: (b, h)),
                  pl.BlockSpec((1, nq, VT_ROWS, tq), lambda b, h, i: (b, 0, h, 0))],
        out_specs=pl.BlockSpec((tq, LANES), lambda b, h, i: (b * nq + i, h)),
        scratch_shapes=[pltpu.VMEM((2, tq, tq), F32), pltpu.VMEM((2, tq, tq), F32),
                        pltpu.VMEM((2, 1, tq), F32), pltpu.VMEM((2, VT_ROWS, tq), F32)],
        compiler_params=_cparams(("arbitrary", "arbitrary", "arbitrary")),
        name="diffattn",
    )(lam_vecs, g_subln, qdt, kd, vdt)


def _outproj_kernel(oa_ref, ob_ref, x_ref, mod_ref, wo_ref, bo_ref, g_ref, wr_ref, br_ref,
                    x1_ref, h2_ref, idx_ref, gate_ref, rank_ref, cnt_ref, carry_ref):
    i = pl.program_id(0)
    tm = x_ref.shape[0]
    half = oa_ref.shape[1]

    @pl.when(i == 0)
    def _():
        carry_ref[...] = jnp.zeros(carry_ref.shape, F32)

    gt1 = mod_ref[0, 2:3, :]
    sh2 = mod_ref[0, 3:4, :]
    sc2 = mod_ref[0, 4:5, :]
    mixed = (jnp.dot(oa_ref[...], wo_ref[0:half, :], preferred_element_type=F32)
             + jnp.dot(ob_ref[...], wo_ref[half:, :], preferred_element_type=F32) + bo_ref[...])
    x1 = x_ref[...] + gt1 * mixed
    x1_ref[...] = x1
    h2 = _rms(x1) * g_ref[...] * (1.0 + sc2) + sh2
    dn = (((1,), (1,)), ((), ()))
    h_hi = h2.astype(BF16)
    h2_ref[...] = _pack_bf16_pairs(h2)
    h_lo = (h2 - h_hi.astype(F32)).astype(BF16)
    w = wr_ref[...]
    w_hi = w.astype(BF16)
    w_lo = (w - w_hi.astype(F32)).astype(BF16)
    logits = (lax.dot_general(w_hi, h_hi, dn, preferred_element_type=F32)
              + lax.dot_general(w_hi, h_lo, dn, preferred_element_type=F32)
              + lax.dot_general(w_lo, h_hi, dn, preferred_element_type=F32)
              + br_ref[...])

    eidx = lax.broadcasted_iota(jnp.int32, logits.shape, 0)
    vals = logits
    onehots, top_vals, top_idx = [], [], []
    for _k in range(TOP_K):
        mx = jnp.max(vals, axis=0, keepdims=True)
        sel = jnp.min(jnp.where(vals == mx, eidx, N_EXPERTS), axis=0, keepdims=True)
        oh = eidx == sel
        onehots.append(oh)
        top_vals.append(mx)
        top_idx.append(sel)
        vals = jnp.where(oh, NEG_INF, vals)
    exps = [jnp.exp(v - top_vals[0]) for v in top_vals]
    denom = exps[0] + exps[1] + exps[2] + exps[3]
    gate_ref[...] = jnp.concatenate([e / denom for e in exps], axis=0)
    idx_ref[...] = jnp.concatenate(top_idx, axis=0)

    member = (onehots[0] | onehots[1] | onehots[2] | onehots[3])
    member_f = member.astype(F32)
    t_src = lax.broadcasted_iota(jnp.int32, (tm, tm), 0)
    t_dst = lax.broadcasted_iota(jnp.int32, (tm, tm), 1)
    before = (t_src < t_dst).astype(BF16)
    prefix = jnp.dot(member.astype(BF16), before, preferred_element_type=F32) + carry_ref[...]
    ranks = [jnp.sum(jnp.where(oh, prefix, 0.0), axis=0, keepdims=True) for oh in onehots]
    rank_ref[...] = jnp.concatenate(ranks, axis=0).astype(jnp.int32)
    carry_ref[...] = carry_ref[...] + jnp.sum(member_f, axis=1, keepdims=True)
    cnt_ref[...] = jnp.broadcast_to(carry_ref[...], cnt_ref.shape)


def _outproj(out_a, out_b, x2, mod3, w_out, b_out, g_ffn, wr_t, br_col, S):
    N, D = x2.shape
    tm = TM_OUT
    tiles_per_seq = S // tm
    row = lambda i: (i, 0)
    colb = lambda i: (0, i)
    const = lambda i: (0, 0)
    return pl.pallas_call(
        _outproj_kernel,
        out_shape=[jax.ShapeDtypeStruct((N, D), F32), jax.ShapeDtypeStruct((N, D // 2), jnp.int32),
                   jax.ShapeDtypeStruct((TOP_K, N), jnp.int32), jax.ShapeDtypeStruct((TOP_K, N), F32),
                   jax.ShapeDtypeStruct((TOP_K, N), jnp.int32), jax.ShapeDtypeStruct((N_EXPERTS, LANES), F32)],
        grid=(N // tm,),
        in_specs=[pl.BlockSpec((tm, out_a.shape[1]), row),
                  pl.BlockSpec((tm, out_b.shape[1]), row),
                  pl.BlockSpec((tm, D), row),
                  pl.BlockSpec((1, N_MOD, D), lambda i: (i // tiles_per_seq, 0, 0)),
                  pl.BlockSpec(w_out.shape, const),
                  pl.BlockSpec((1, D), const),
                  pl.BlockSpec((1, D), const),
                  pl.BlockSpec(wr_t.shape, const),
                  pl.BlockSpec((N_EXPERTS, 1), const)],
        out_specs=[pl.BlockSpec((tm, D), row), pl.BlockSpec((tm, D // 2), row),
                   pl.BlockSpec((TOP_K, tm), colb), pl.BlockSpec((TOP_K, tm), colb),
                   pl.BlockSpec((TOP_K, tm), colb), pl.BlockSpec((N_EXPERTS, LANES), const)],
        scratch_shapes=[pltpu.VMEM((N_EXPERTS, 1), F32)],
        compiler_params=_cparams(("arbitrary",)),
        name="outproj_router",
    )(out_a, out_b, x2, mod3, w_out, b_out, g_ffn, wr_t, br_col)


def _experts_kernel(first_ref, nblk_ref, x_ref, w1_ref, b1g_ref, b1l_ref, w2_ref, b2_ref, y_ref,
                    w1g_a, w1l_a, w2_a, w1g_b, w1l_b, w2_b, xbuf, ybuf, xsem, ysem):
    s = pl.program_id(0)
    n_exp = pl.num_programs(0) - 1
    e = jnp.maximum(s - 1, 0)
    first = first_ref[e]
    n_blk = jnp.where(s >= 1, nblk_ref[e], 0)
    nxt = jnp.minimum(s, n_exp - 1)
    do_prep = (s < n_exp) & (nblk_ref[nxt] > 0)
    bm = xbuf.shape[1]

    def x_copy(j, slot, first_blk=None):
        blk = (first if first_blk is None else first_blk) + j
        rows = pl.ds(pl.multiple_of(blk * bm, bm), bm)
        return pltpu.make_async_copy(x_ref.at[rows], xbuf.at[slot], xsem.at[slot])

    def y_copy(j, slot):
        rows = pl.ds(pl.multiple_of((first + j) * bm, bm), bm)
        return pltpu.make_async_copy(ybuf.at[slot], y_ref.at[rows], ysem.at[slot])

    def swiglu(ug, ul):
        glu = jnp.minimum(ug, SWIGLU_LIMIT)
        lin = jnp.clip(ul, -SWIGLU_LIMIT, SWIGLU_LIMIT)
        return (glu * jax.nn.sigmoid(SWIGLU_ALPHA * glu) * (lin + 1.0)).astype(BF16)

    def relayout(new):
        w1g_s, w1l_s, w2_s = new
        ch = 256
        half = ch // 2
        for c in range(w1_ref.shape[2] // ch):
            t = w1_ref[0, :, c * ch:(c + 1) * ch].astype(BF16).T
            pairs = pltpu.bitcast(t, jnp.int32)
            cols = slice(c * half, (c + 1) * half)
            w1g_s[:, cols] = lax.bitcast_convert_type(lax.shift_left(pairs, 16), F32).astype(BF16).T
            w1l_s[:, cols] = lax.bitcast_convert_type(pairs & jnp.int32(HIGH_HALF), F32).astype(BF16).T
        w2_s[...] = w2_ref[0].astype(BF16)

    def block(j, cur, new=None):
        w1g_s, w1l_s, w2_s = cur
        slot = j & 1
        x_copy(j, slot).wait()

        @pl.when(j + 1 < n_blk)
        def _():
            x_copy(j + 1, 1 - slot).start(priority=1)

        @pl.when(j >= 2)
        def _():
            y_copy(j - 2, slot).wait()

        if new is not None:
            relayout(new)
        x = _unpack_bf16_pairs(xbuf[slot]).astype(BF16)
        ug = jnp.dot(x, w1g_s[...], preferred_element_type=F32) + b1g_ref[0]
        ul = jnp.dot(x, w1l_s[...], preferred_element_type=F32) + b1l_ref[0]
        y = jnp.dot(swiglu(ug, ul), w2_s[...], preferred_element_type=F32) + b2_ref[0]
        ybuf[slot] = _pack_bf16_pairs(y)
        y_copy(j, slot).start(priority=1)

    def step(new, cur):
        @pl.when((n_blk > 0) & do_prep)
        def _():
            block(0, cur, new)

        @pl.when((n_blk > 0) & jnp.logical_not(do_prep))
        def _():
            block(0, cur)

        @pl.when((n_blk == 0) & do_prep)
        def _():
            relayout(new)

        def later_block(j, carry):
            block(j, cur)
            return carry

        lax.fori_loop(1, n_blk, later_block, 0)

    set_a, set_b = (w1g_a, w1l_a, w2_a), (w1g_b, w1l_b, w2_b)

    @pl.when(s % 2 == 0)
    def _():
        step(set_a, set_b)

    @pl.when(s % 2 == 1)
    def _():
        step(set_b, set_a)

    @pl.when(do_prep)
    def _():
        x_copy(0, 0, first_ref[nxt]).start(priority=1)

    @pl.when(n_blk >= 2)
    def _():
        y_copy(n_blk - 2, n_blk & 1).wait()

    @pl.when(n_blk >= 1)
    def _():
        y_copy(n_blk - 1, (n_blk - 1) & 1).wait()


def _experts(first_blk, n_blk, x_rows, w1, b1g, b1l, w2, b2):
    E, Fh, D = w2.shape
    bm = BLOCK_ROWS
    n_rows = x_rows.shape[0]
    wsel = lambda s, fb, nb: (jnp.minimum(s, E - 1), 0, 0)
    bsel = lambda s, fb, nb: (jnp.maximum(s - 1, 0), 0, 0)
    return pl.pallas_call(
        _experts_kernel,
        out_shape=jax.ShapeDtypeStruct((n_rows, D // 2), jnp.int32),
        grid_spec=pltpu.PrefetchScalarGridSpec(
            num_scalar_prefetch=2,
            grid=(E + 1,),
            in_specs=[pl.BlockSpec(memory_space=pl.ANY),
                      pl.BlockSpec((1, D, 2 * Fh), wsel),
                      pl.BlockSpec((1, 1, Fh), bsel),
                      pl.BlockSpec((1, 1, Fh), bsel),
                      pl.BlockSpec((1, Fh, D), wsel),
                      pl.BlockSpec((1, 1, D), bsel)],
            out_specs=pl.BlockSpec(memory_space=pl.ANY),
            scratch_shapes=[pltpu.VMEM((D, Fh), BF16), pltpu.VMEM((D, Fh), BF16), pltpu.VMEM((Fh, D), BF16),
                            pltpu.VMEM((D, Fh), BF16), pltpu.VMEM((D, Fh), BF16), pltpu.VMEM((Fh, D), BF16),
                            pltpu.VMEM((2, bm, D // 2), jnp.int32), pltpu.VMEM((2, bm, D // 2), jnp.int32),
                            pltpu.SemaphoreType.DMA((2,)), pltpu.SemaphoreType.DMA((2,))]),
        compiler_params=_cparams(("arbitrary",), 56 * 1024 * 1024),
        name="experts",
    )(first_blk, n_blk, x_rows, w1, b1g, b1l, w2, b2)


SC_CORES = 2
SC_SUBCORES = 16
SC_CHUNK = 64


def _sc_gather_rows(table, idx):
    M = idx.shape[0]
    D = table.shape[1]
    workers = SC_CORES * SC_SUBCORES
    per_worker = M // workers
    n_chunks = per_worker // SC_CHUNK
    assert M % workers == 0 and per_worker % (2 * SC_CHUNK) == 0
    mesh = plsc.VectorSubcoreMesh(core_axis_name="c", subcore_axis_name="s")

    @functools.partial(
        pl.kernel, mesh=mesh,
        out_type=jax.ShapeDtypeStruct((M, D), table.dtype),
        scratch_types=[pltpu.VMEM((per_worker,), jnp.int32),
                       pltpu.VMEM((SC_CHUNK, D), table.dtype), pltpu.VMEM((SC_CHUNK, D), table.dtype),
                       pltpu.SemaphoreType.DMA, pltpu.SemaphoreType.DMA],
        name="sc_gather_rows")
    def gather(table_hbm, idx_hbm, out_hbm, idx_v, rows0, rows1, sem0, sem1):
        wid = lax.axis_index("s") * SC_CORES + lax.axis_index("c")
        base = wid * per_worker
        pltpu.sync_copy(idx_hbm.at[pl.ds(base, per_worker)], idx_v)

        def fetch(c, buf, sem):
            off = pl.multiple_of(c * SC_CHUNK, SC_CHUNK)
            return pltpu.make_async_copy(table_hbm.at[idx_v.at[pl.ds(off, SC_CHUNK)]], buf, sem)

        def flush(c, buf):
            off = pl.multiple_of(c * SC_CHUNK, SC_CHUNK)
            pltpu.sync_copy(buf, out_hbm.at[pl.ds(base + off, SC_CHUNK)])

        fetch(0, rows0, sem0).start()

        @pl.loop(0, n_chunks // 2)
        def _(jj):
            c = 2 * jj
            fetch(c + 1, rows1, sem1).start()
            fetch(c, rows0, sem0).wait()
            flush(c, rows0)

            @pl.when(c + 2 < n_chunks)
            def _():
                fetch(c + 2, rows0, sem0).start()

            fetch(c + 1, rows1, sem1).wait()
            flush(c + 1, rows1)

    return gather(table, idx)


SC_SCATTER_CHUNK = 128


def _sc_scatter_rows(rows, dest, n_rows):
    N, W = rows.shape
    workers = SC_CORES * SC_SUBCORES
    per_worker = N // workers
    chunks = per_worker // SC_SCATTER_CHUNK
    assert N % workers == 0 and per_worker % SC_SCATTER_CHUNK == 0
    dest3 = dest.reshape(TOP_K, N // SC_SCATTER_CHUNK, SC_SCATTER_CHUNK)
    mesh = plsc.VectorSubcoreMesh(core_axis_name="c", subcore_axis_name="s")

    @functools.partial(
        pl.kernel, mesh=mesh,
        out_type=jax.ShapeDtypeStruct((n_rows, W), rows.dtype),
        scratch_types=[pltpu.VMEM((TOP_K, chunks, SC_SCATTER_CHUNK), jnp.int32),
                       pltpu.VMEM((SC_SCATTER_CHUNK, W), rows.dtype)],
        name="sc_scatter_rows")
    def scatter(rows_hbm, dest_hbm, out_hbm, idx_v, rows_v):
        wid = lax.axis_index("s") * SC_CORES + lax.axis_index("c")
        for k in range(TOP_K):
            pltpu.sync_copy(dest_hbm.at[k, pl.ds(wid * chunks, chunks)], idx_v.at[k])

        @pl.loop(0, chunks)
        def _(j):
            start = pl.multiple_of(wid * per_worker + j * SC_SCATTER_CHUNK, SC_SCATTER_CHUNK)
            pltpu.sync_copy(rows_hbm.at[pl.ds(start, SC_SCATTER_CHUNK)], rows_v)
            for k in range(TOP_K):
                pltpu.sync_copy(rows_v, out_hbm.at[idx_v.at[k, j]])

    return scatter(rows, dest3)


def _combine_kernel(x1_ref, gate_ref, mod_ref, g_ref, y_ref, o_ref, *, final_norm):
    gate = gate_ref[...]
    moe = gate[:, 0:1] * _unpack_bf16_pairs(y_ref[0])
    for k in range(1, TOP_K):
        moe = moe + gate[:, k:k + 1] * _unpack_bf16_pairs(y_ref[k])
    gt2 = mod_ref[0, 5:6, :]
    x2 = x1_ref[...] + gt2 * moe
    o_ref[...] = _rms(x2) * g_ref[...] if final_norm else x2


def _combine(x1, gate_t, mod3, g_final, y_tok, S, final_norm):
    N, D = x1.shape
    tm = TM_ROWS
    tiles_per_seq = S // tm
    row = lambda i: (i, 0)
    return pl.pallas_call(
        functools.partial(_combine_kernel, final_norm=final_norm),
        out_shape=jax.ShapeDtypeStruct((N, D), F32),
        grid=(N // tm,),
        in_specs=[pl.BlockSpec((tm, D), row),
                  pl.BlockSpec((tm, TOP_K), row),
                  pl.BlockSpec((1, N_MOD, D), lambda i: (i // tiles_per_seq, 0, 0)),
                  pl.BlockSpec((1, D), lambda i: (0, 0)),
                  pl.BlockSpec((TOP_K, tm, D // 2), lambda i: (0, i, 0))],
        out_specs=pl.BlockSpec((tm, D), row),
        compiler_params=_cparams(("arbitrary",)),
        name="combine",
    )(x1, gate_t, mod3, g_final, y_tok)


def _routing_tables(counts, idx, rank):
    bm = BLOCK_ROWS
    counts = counts.astype(jnp.int32)
    padded = (counts + bm - 1) // bm * bm
    pends = jnp.cumsum(padded)
    pstarts = pends - padded
    experts = jnp.arange(N_EXPERTS, dtype=jnp.int32)
    dest = jnp.sum(jnp.where(idx[..., None] == experts, pstarts, 0), axis=-1) + rank
    return dest.astype(jnp.int32), (pstarts // bm).astype(jnp.int32), (padded // bm).astype(jnp.int32)


def _extended_in_weights(w_in, b_in):
    a_q = SWA_Q_HEADS * HEAD_DIM
    a_kv = SWA_KV_HEADS * HEAD_DIM
    b_w = DIFF_HEADS * DIFF_V_DIM
    widths = (a_q, 2 * a_kv, 2 * a_kv, b_w, b_w, b_w)
    return w_in.astype(BF16), b_in.reshape(1, -1), widths


def kernel(x, c, positions, w_ada, b_ada, g_mix, w_in, b_in, attn_sinks, lambda_q1, lambda_k1, lambda_q2,
           lambda_k2, g_subln, w_out, b_out, g_ffn, w_router, b_router, w1, b1, w2, b2, g_final):
    B, S, D = x.shape
    N = B * S
    depth = w_ada.shape[0]
    n_rows = (N * TOP_K + N_EXPERTS * (BLOCK_ROWS - 1) + BLOCK_ROWS - 1) // BLOCK_ROWS * BLOCK_ROWS

    inv = 1.0 / (ROPE_THETA ** (jnp.arange(0, HEAD_DIM, 2, dtype=F32) / HEAD_DIM))
    inv_lane = jnp.tile(inv, LANES // (HEAD_DIM // 2)).reshape(1, LANES)
    pos2 = positions.reshape(N, 1)
    xcur = x.reshape(N, D)

    for layer in range(depth):
        last = layer == depth - 1
        lambda_init = 0.8 - 0.6 * math.exp(-0.3 * layer)
        mod3 = _adaln(c, w_ada[layer], b_ada[layer]).reshape(B, N_MOD, D)

        w_ext, b_ext, widths = _extended_in_weights(w_in[layer], b_in[layer])
        qa, ka2, va2, qdt, kd, vdt = _inproj(xcur, pos2, inv_lane, mod3, g_mix[layer].reshape(1, D),
                                             w_ext, b_ext, S, widths)
        out_a = _swa(attn_sinks[layer], qa, ka2, va2, B, S)
        lam_vecs = jnp.stack([lambda_q1[layer], lambda_k1[layer], lambda_q2[layer], lambda_k2[layer]])
        out_b = _diffattn(lam_vecs, g_subln[layer].reshape(1, DIFF_V_DIM), qdt, kd, vdt, B, S, lambda_init)

        x1, h2, idx, gate, rank, counts = _outproj(
            out_a, out_b, xcur, mod3, w_out[layer].astype(BF16), b_out[layer].reshape(1, D),
            g_ffn[layer].reshape(1, D), w_router[layer].T, b_router[layer].reshape(N_EXPERTS, 1), S)

        dest, first_blk, n_blk = _routing_tables(counts[:, 0], idx, rank)
        x_rows = _sc_scatter_rows(h2, dest, n_rows)
        y_rows = _experts(first_blk, n_blk, x_rows, w1[layer],
                          b1[layer][:, None, 0::2], b1[layer][:, None, 1::2],
                          w2[layer], b2[layer][:, None, :])
        y_tok = _sc_gather_rows(y_rows, dest.reshape(-1)).reshape(TOP_K, N, D // 2)
        xcur = _combine(x1, gate.T, mod3, g_final.reshape(1, D), y_tok, S, final_norm=last)
    return xcur.reshape(B, S, D)
```

```python
import functools
import math

import jax
import jax.numpy as jnp
from jax import lax
from jax.experimental import pallas as pl
from jax.experimental.pallas import tpu as pltpu
from jax.experimental.pallas import tpu_sc as plsc

HEAD_DIM = 64
SWA_Q_HEADS = 8
SWA_KV_HEADS = 2
SWA_GROUP = SWA_Q_HEADS // SWA_KV_HEADS
WINDOW = 128
DIFF_HEADS = 4
DIFF_V_DIM = 2 * HEAD_DIM
ROPE_THETA = 10000.0
N_EXPERTS = 32
TOP_K = 4
SWIGLU_ALPHA = 1.702
SWIGLU_LIMIT = 7.0
EPS = 1e-5
N_MOD = 6

LANES = 128
F32 = jnp.float32
BF16 = jnp.bfloat16
NEG_INF = float("-inf")

TM_PROJ = 1024
TQ_SWA = 512
VT_ROWS = DIFF_V_DIM + 16
TM_OUT = 512
TM_ROWS = 512
BLOCK_ROWS = 512
VMEM_LIMIT = 48 * 1024 * 1024


def _cparams(sem, vmem=VMEM_LIMIT):
    return pltpu.CompilerParams(dimension_semantics=sem, vmem_limit_bytes=vmem)


def _adaln_kernel(ct_ref, w_ref, b_ref, o_ref):
    c = ct_ref[...]
    cond = c * jax.nn.sigmoid(c)
    w = w_ref[...]
    rows = [jnp.sum(w * cond[:, b:b + 1], axis=0, keepdims=True) for b in range(c.shape[1])]
    o_ref[...] = jnp.concatenate(rows, axis=0) + b_ref[...]


def _adaln(c, w_ada, b_ada):
    B, D = c.shape
    n_out = w_ada.shape[1]
    tn = 1024
    return pl.pallas_call(
        _adaln_kernel,
        out_shape=jax.ShapeDtypeStruct((B, n_out), F32),
        grid=(n_out // tn,),
        in_specs=[pl.BlockSpec((D, B), lambda j: (0, 0)),
                  pl.BlockSpec((D, tn), lambda j: (0, j)),
                  pl.BlockSpec((1, tn), lambda j: (0, j))],
        out_specs=pl.BlockSpec((B, tn), lambda j: (0, j)),
        compiler_params=_cparams(("arbitrary",)),
        name="adaln",
    )(c.T, w_ada, b_ada.reshape(1, n_out))


def _rms(x):
    return x * lax.rsqrt(jnp.mean(x * x, axis=-1, keepdims=True) + EPS)


HIGH_HALF = -65536


def _pack_bf16_pairs(v):
    bits = lax.bitcast_convert_type(v.astype(BF16).astype(F32), jnp.int32)
    half = v.shape[1] // 2
    return lax.shift_right_logical(bits[:, :half], 16) | (bits[:, half:] & jnp.int32(HIGH_HALF))


def _unpack_bf16_pairs(w):
    return jnp.concatenate([lax.bitcast_convert_type(lax.shift_left(w, 16), F32),
                            lax.bitcast_convert_type(w & jnp.int32(HIGH_HALF), F32)], axis=1)


def _inproj_kernel(x_ref, pos_ref, inv_ref, mod_ref, g_ref, w_ref, b_ref,
                   qa_ref, ka_ref, va_ref, qd_ref, kd_ref, vd_ref):
    x = x_ref[...]
    sh = mod_ref[0, 0:1, :]
    sc = mod_ref[0, 1:2, :]
    h = _rms(x) * g_ref[...] * (1.0 + sc) + sh
    proj = jnp.dot(h.astype(BF16), w_ref[...], preferred_element_type=F32) + b_ref[...]

    lane = lax.broadcasted_iota(jnp.int32, (1, LANES), 1)
    first_half = (lane & (HEAD_DIM - 1)) < (HEAD_DIM // 2)
    n_freq = HEAD_DIM // 2
    groups = LANES // n_freq
    tm = x.shape[0]
    rows = tm // groups
    group = lax.shift_right_logical(lane, n_freq.bit_length() - 1)
    pos = pos_ref[...].astype(F32)
    pos_q = pos[0:rows]
    for g in range(1, groups):
        pos_q = jnp.where(group == g, pos[g * rows:(g + 1) * rows], pos_q)
    ang_q = pos_q * inv_ref[...]

    def spread(table_q):
        blocks = []
        for g in range(groups):
            only = jnp.where(group == g, table_q, 0.0)
            full = only
            for r in range(1, groups):
                full = full + pltpu.roll(only, r * n_freq, axis=1)
            blocks.append(full)
        return jnp.concatenate(blocks, axis=0)

    cos = spread(jnp.cos(ang_q))
    sin = spread(jnp.sin(ang_q))
    sin_signed = jnp.where(first_half, -sin, sin)

    def rope(t):
        partner = jnp.where(first_half,
                            pltpu.roll(t, LANES - HEAD_DIM // 2, axis=1),
                            pltpu.roll(t, HEAD_DIM // 2, axis=1))
        return t * cos + partner * sin_signed

    lo_half = lane < HEAD_DIM

    def emit(out_ref, col0, width, rotary, scale, transposed, doubled=False):
        for j in range(width // LANES):
            t = proj[:, col0 + j * LANES: col0 + (j + 1) * LANES]
            if rotary:
                t = rope(t)
            if scale != 1.0:
                t = t * scale
            if doubled:
                swapped = pltpu.roll(t, HEAD_DIM, axis=1)
                out_ref[:, 2 * j * LANES:(2 * j + 1) * LANES] = jnp.where(lo_half, t, swapped).astype(out_ref.dtype)
                out_ref[:, (2 * j + 1) * LANES:(2 * j + 2) * LANES] = jnp.where(lo_half, swapped, t).astype(out_ref.dtype)
            elif transposed:
                rows = out_ref.shape[2] // (width // LANES)
                out_ref[0, 0, j * rows:j * rows + LANES, :] = t.T.astype(out_ref.dtype)
                if rows > LANES:
                    fill = lax.broadcasted_iota(jnp.int32, (rows - LANES, t.shape[0]), 0) == 0
                    out_ref[0, 0, j * rows + LANES:(j + 1) * rows, :] = fill.astype(out_ref.dtype)
            else:
                out_ref[:, j * LANES:(j + 1) * LANES] = t.astype(out_ref.dtype)

    swa_scale = 1.0 / math.sqrt(HEAD_DIM)
    diff_scale = math.log2(math.e) / math.sqrt(HEAD_DIM)
    col = 0
    for out_ref, width, rotary, scale, transposed, doubled in (
            (qa_ref, qa_ref.shape[1], True, swa_scale, False, False),
            (ka_ref, ka_ref.shape[1] // 2, True, 1.0, False, True),
            (va_ref, va_ref.shape[1] // 2, False, 1.0, False, True),
            (qd_ref, qd_ref.shape[2], True, diff_scale, True, False),
            (kd_ref, kd_ref.shape[1], True, 1.0, False, False),
            (vd_ref, vd_ref.shape[2] // VT_ROWS * LANES, False, 1.0, True, False)):
        emit(out_ref, col, width, rotary, scale, transposed, doubled)
        col += width


def _inproj(x2, pos2, inv_lane, mod3, g_mix, w_ext, b_ext, S, widths):
    N, D = x2.shape
    tm = TM_PROJ
    C = w_ext.shape[1]
    tiles_per_seq = S // tm
    row = lambda i: (i, 0)
    t_rows = (0, 0, 0, widths[3], 0, widths[5] // LANES * VT_ROWS)
    out_shape, out_specs = [], []
    for w, tr in zip(widths, t_rows):
        if tr:
            out_shape.append(jax.ShapeDtypeStruct((N // S, tiles_per_seq, tr, tm), BF16))
            out_specs.append(pl.BlockSpec((1, 1, tr, tm), lambda i: (i // tiles_per_seq, i % tiles_per_seq, 0, 0)))
        else:
            out_shape.append(jax.ShapeDtypeStruct((N, w), BF16))
            out_specs.append(pl.BlockSpec((tm, w), row))
    return pl.pallas_call(
        _inproj_kernel,
        out_shape=out_shape,
        grid=(N // tm,),
        in_specs=[pl.BlockSpec((tm, D), row),
                  pl.BlockSpec((tm, 1), row),
                  pl.BlockSpec((1, LANES), lambda i: (0, 0)),
                  pl.BlockSpec((1, N_MOD, D), lambda i: (i // tiles_per_seq, 0, 0)),
                  pl.BlockSpec((1, D), lambda i: (0, 0)),
                  pl.BlockSpec((D, C), lambda i: (0, 0)),
                  pl.BlockSpec((1, C), lambda i: (0, 0))],
        out_specs=out_specs,
        compiler_params=_cparams(("arbitrary",)),
        name="inproj",
    )(x2, pos2, inv_lane, mod3, g_mix, w_ext, b_ext)


def _swa_kernel(sink_ref, q_ref, kc_ref, kp_ref, vc_ref, vp_ref, o_ref):
    i = pl.program_id(1)
    tq = q_ref.shape[0]
    lane = lax.broadcasted_iota(jnp.int32, (1, LANES), 1)
    lo = lane < HEAD_DIM
    qi = lax.broadcasted_iota(jnp.int32, (WINDOW, 2 * WINDOW), 0) + WINDOW
    kj = lax.broadcasted_iota(jnp.int32, (WINDOW, 2 * WINDOW), 1)
    band = (qi - kj >= 0) & (qi - kj < WINDOW)
    dn = (((1,), (1,)), ((), ()))
    for c in range(tq // WINDOW):
        if c == 0:
            kcat = jnp.concatenate([kp_ref[...], kc_ref[0:WINDOW, :]], axis=0)
            vcat = jnp.concatenate([vp_ref[...], vc_ref[0:WINDOW, :]], axis=0)
            mask = band & (kj >= jnp.where(i > 0, 0, WINDOW))
        else:
            kcat = kc_ref[(c - 1) * WINDOW:(c + 1) * WINDOW, :]
            vcat = vc_ref[(c - 1) * WINDOW:(c + 1) * WINDOW, :]
            mask = band
        for j in range(SWA_KV_HEADS):
            kj2 = kcat[:, j * LANES:(j + 1) * LANES]
            vj2 = vcat[:, j * LANES:(j + 1) * LANES]
            zero = jnp.zeros_like(kj2)
            k_halves = (jnp.where(lo, kj2, zero), jnp.where(lo, zero, kj2))
            v_halves = (jnp.where(lo, vj2, zero), jnp.where(lo, zero, vj2))
            for p in range(SWA_GROUP // 2):
                g = j * (SWA_GROUP // 2) + p
                q = q_ref[c * WINDOW:(c + 1) * WINDOW, g * LANES:(g + 1) * LANES]
                out = jnp.zeros((WINDOW, LANES), F32)
                for half in range(2):
                    sink = sink_ref[2 * g + half]
                    s = lax.dot_general(q, k_halves[half], dn, preferred_element_type=F32)
                    s = jnp.where(mask, s, NEG_INF)
                    m = jnp.maximum(jnp.max(s, axis=1, keepdims=True), sink)
                    e = jnp.exp(s - m)
                    denom = jnp.sum(e, axis=1, keepdims=True) + jnp.exp(sink - m)
                    pv = jnp.dot(e.astype(BF16), v_halves[half], preferred_element_type=F32)
                    out = out + pv / denom
                o_ref[c * WINDOW:(c + 1) * WINDOW, g * LANES:(g + 1) * LANES] = out.astype(o_ref.dtype)


def _swa(sinks, qa, ka2, va2, B, S):
    N = qa.shape[0]
    tq = TQ_SWA
    nq = S // tq
    wpt = tq // WINDOW
    wps = S // WINDOW
    cur = lambda b, i: (b * nq + i, 0)
    prev = lambda b, i: (b * wps + jnp.maximum(i * wpt - 1, 0), 0)
    return pl.pallas_call(
        _swa_kernel,
        out_shape=jax.ShapeDtypeStruct((N, qa.shape[1]), BF16),
        grid=(B, nq),
        in_specs=[pl.BlockSpec(memory_space=pltpu.SMEM),
                  pl.BlockSpec((tq, qa.shape[1]), cur),
                  pl.BlockSpec((tq, ka2.shape[1]), cur),
                  pl.BlockSpec((WINDOW, ka2.shape[1]), prev),
                  pl.BlockSpec((tq, va2.shape[1]), cur),
                  pl.BlockSpec((WINDOW, va2.shape[1]), prev)],
        out_specs=pl.BlockSpec((tq, qa.shape[1]), cur),
        compiler_params=_cparams(("arbitrary", "arbitrary")),
        name="swa",
    )(sinks, qa, ka2, ka2, va2, va2)


def _diff_kernel(lam_ref, g_ref, qt_ref, k_ref, vt_ref, o_ref, sa_ref, sb_ref, m_ref, acc_ref, *, lambda_init):
    i = pl.program_id(2)
    tq = qt_ref.shape[3]
    tk = vt_ref.shape[3]
    qt = qt_ref[0, 0]
    lane = lax.broadcasted_iota(jnp.int32, (1, LANES), 1)
    lo = lane < HEAD_DIM
    m_ref[...] = jnp.full(m_ref.shape, NEG_INF, F32)
    acc_ref[...] = jnp.zeros(acc_ref.shape, F32)

    def scores(c, s_ref):
        k = k_ref[pl.ds(pl.multiple_of(c * tk, tk), tk), :]
        zero = jnp.zeros_like(k)
        s_ref[0] = jnp.dot(jnp.where(lo, k, zero), qt, preferred_element_type=F32)
        s_ref[1] = jnp.dot(jnp.where(lo, zero, k), qt, preferred_element_type=F32)

    def consume(c, s_ref, first_key):
        vt = vt_ref[0, c]
        for mp in range(2):
            s = s_ref[mp]
            if first_key is not None:
                kpos = lax.broadcasted_iota(jnp.int32, (tk, tq), 0) + first_key
                qpos = lax.broadcasted_iota(jnp.int32, (tk, tq), 1)
                s = jnp.where(kpos <= qpos, s, NEG_INF)
            m_prev = m_ref[mp]
            m_new = jnp.maximum(m_prev, jnp.max(s, axis=0, keepdims=True))
            alpha = jnp.exp2(m_prev - m_new)
            p = jnp.exp2(s - m_new).astype(BF16)
            acc_ref[mp] = alpha * acc_ref[mp] + jnp.dot(vt, p, preferred_element_type=F32)
            m_ref[mp] = m_new

    scores(0, sa_ref)

    def pair(jj, carry):
        c = 2 * jj
        scores(c + 1, sb_ref)
        consume(c, sa_ref, None)
        scores(c + 2, sa_ref)
        consume(c + 1, sb_ref, None)
        return carry

    lax.fori_loop(0, lax.shift_right_logical(i, 1), pair, 0)

    @pl.when(i % 2 == 0)
    def _():
        consume(i, sa_ref, 0)

    @pl.when(i % 2 == 1)
    def _():
        scores(i, sb_ref)
        consume(i - 1, sa_ref, None)
        consume(i, sb_ref, 0)

    lq1, lk1, lq2, lk2 = (lam_ref[r:r + 1, :] for r in range(4))
    lam = (jnp.exp(jnp.sum(lq1 * lk1, axis=1, keepdims=True))
           - jnp.exp(jnp.sum(lq2 * lk2, axis=1, keepdims=True)) + lambda_init)
    d = DIFF_V_DIM
    ot = (acc_ref[0, 0:d, :] / acc_ref[0, d:d + 1, :]
          - lam * (acc_ref[1, 0:d, :] / acc_ref[1, d:d + 1, :]))
    ot = ot * lax.rsqrt(jnp.mean(ot * ot, axis=0, keepdims=True) + EPS)
    o_ref[...] = (ot.T * g_ref[...] * (1.0 - lambda_init)).astype(o_ref.dtype)


def _diffattn(lam_vecs, g_subln, qdt, kd, vdt, B, S, lambda_init):
    N, C = kd.shape
    tq = qdt.shape[3]
    nq = S // tq
    return pl.pallas_call(
        functools.partial(_diff_kernel, lambda_init=lambda_init),
        out_shape=jax.ShapeDtypeStruct((N, C), BF16),
        grid=(B, DIFF_HEADS, nq),
        in_specs=[pl.BlockSpec((4, HEAD_DIM), lambda b, h, i: (0, 0)),
                  pl.BlockSpec((1, DIFF_V_DIM), lambda b, h, i: (0, 0)),
                  pl.BlockSpec((1, 1, LANES, tq), lambda b, h, i: (b, i, h, 0)),
                  pl.BlockSpec((S, LANES), lambda b, h, i: (b, h)),
                  pl.BlockSpec((1, nq, VT_ROWS, tq), lambda b, h, i: (b, 0, h, 0))],
        out_specs=pl.BlockSpec((tq, LANES), lambda b, h, i: (b * nq + i, h)),
        scratch_shapes=[pltpu.VMEM((2, tq, tq), F32), pltpu.VMEM((2, tq, tq), F32),
                        pltpu.VMEM((2, 1, tq), F32), pltpu.VMEM((2, VT_ROWS, tq), F32)],
        compiler_params=_cparams(("arbitrary", "arbitrary", "arbitrary")),
        name="diffattn",
    )(lam_vecs, g_subln, qdt, kd, vdt)


def _outproj_kernel(oa_ref, ob_ref, x_ref, mod_ref, wo_ref, bo_ref, g_ref, wr_ref, br_ref,
                    x1_ref, h2_ref, idx_ref, gate_ref, rank_ref, cnt_ref, carry_ref):
    i = pl.program_id(0)
    tm = x_ref.shape[0]
    half = oa_ref.shape[1]

    @pl.when(i == 0)
    def _():
        carry_ref[...] = jnp.zeros(carry_ref.shape, F32)

    gt1 = mod_ref[0, 2:3, :]
    sh2 = mod_ref[0, 3:4, :]
    sc2 = mod_ref[0, 4:5, :]
    mixed = (jnp.dot(oa_ref[...], wo_ref[0:half, :], preferred_element_type=F32)
             + jnp.dot(ob_ref[...], wo_ref[half:, :], preferred_element_type=F32) + bo_ref[...])
    x1 = x_ref[...] + gt1 * mixed
    x1_ref[...] = x1
    h2 = _rms(x1) * g_ref[...] * (1.0 + sc2) + sh2
    dn = (((1,), (1,)), ((), ()))
    h_hi = h2.astype(BF16)
    h2_ref[...] = _pack_bf16_pairs(h2)
    h_lo = (h2 - h_hi.astype(F32)).astype(BF16)
    w = wr_ref[...]
    w_hi = w.astype(BF16)
    w_lo = (w - w_hi.astype(F32)).astype(BF16)
    logits = (lax.dot_general(w_hi, h_hi, dn, preferred_element_type=F32)
              + lax.dot_general(w_hi, h_lo, dn, preferred_element_type=F32)
              + lax.dot_general(w_lo, h_hi, dn, preferred_element_type=F32)
              + br_ref[...])

    eidx = lax.broadcasted_iota(jnp.int32, logits.shape, 0)
    vals = logits
    onehots, top_vals, top_idx = [], [], []
    for _k in range(TOP_K):
        mx = jnp.max(vals, axis=0, keepdims=True)
        sel = jnp.min(jnp.where(vals == mx, eidx, N_EXPERTS), axis=0, keepdims=True)
        oh = eidx == sel
        onehots.append(oh)
        top_vals.append(mx)
        top_idx.append(sel)
        vals = jnp.where(oh, NEG_INF, vals)
    exps = [jnp.exp(v - top_vals[0]) for v in top_vals]
    denom = exps[0] + exps[1] + exps[2] + exps[3]
    gate_ref[...] = jnp.concatenate([e / denom for e in exps], axis=0)
    idx_ref[...] = jnp.concatenate(top_idx, axis=0)

    member = (onehots[0] | onehots[1] | onehots[2] | onehots[3])
    member_f = member.astype(F32)
    t_src = lax.broadcasted_iota(jnp.int32, (tm, tm), 0)
    t_dst = lax.broadcasted_iota(jnp.int32, (tm, tm), 1)
    before = (t_src < t_dst).astype(BF16)
    prefix = jnp.dot(member.astype(BF16), before, preferred_element_type=F32) + carry_ref[...]
    ranks = [jnp.sum(jnp.where(oh, prefix, 0.0), axis=0, keepdims=True) for oh in onehots]
    rank_ref[...] = jnp.concatenate(ranks, axis=0).astype(jnp.int32)
    carry_ref[...] = carry_ref[...] + jnp.sum(member_f, axis=1, keepdims=True)
    cnt_ref[...] = jnp.broadcast_to(carry_ref[...], cnt_ref.shape)


def _outproj(out_a, out_b, x2, mod3, w_out, b_out, g_ffn, wr_t, br_col, S):
    N, D = x2.shape
    tm = TM_OUT
    tiles_per_seq = S // tm
    row = lambda i: (i, 0)
    colb = lambda i: (0, i)
    const = lambda i: (0, 0)
    return pl.pallas_call(
        _outproj_kernel,
        out_shape=[jax.ShapeDtypeStruct((N, D), F32), jax.ShapeDtypeStruct((N, D // 2), jnp.int32),
                   jax.ShapeDtypeStruct((TOP_K, N), jnp.int32), jax.ShapeDtypeStruct((TOP_K, N), F32),
                   jax.ShapeDtypeStruct((TOP_K, N), jnp.int32), jax.ShapeDtypeStruct((N_EXPERTS, LANES), F32)],
        grid=(N // tm,),
        in_specs=[pl.BlockSpec((tm, out_a.shape[1]), row),
                  pl.BlockSpec((tm, out_b.shape[1]), row),
                  pl.BlockSpec((tm, D), row),
                  pl.BlockSpec((1, N_MOD, D), lambda i: (i // tiles_per_seq, 0, 0)),
                  pl.BlockSpec(w_out.shape, const),
                  pl.BlockSpec((1, D), const),
                  pl.BlockSpec((1, D), const),
                  pl.BlockSpec(wr_t.shape, const),
                  pl.BlockSpec((N_EXPERTS, 1), const)],
        out_specs=[pl.BlockSpec((tm, D), row), pl.BlockSpec((tm, D // 2), row),
                   pl.BlockSpec((TOP_K, tm), colb), pl.BlockSpec((TOP_K, tm), colb),
                   pl.BlockSpec((TOP_K, tm), colb), pl.BlockSpec((N_EXPERTS, LANES), const)],
        scratch_shapes=[pltpu.VMEM((N_EXPERTS, 1), F32)],
        compiler_params=_cparams(("arbitrary",)),
        name="outproj_router",
    )(out_a, out_b, x2, mod3, w_out, b_out, g_ffn, wr_t, br_col)


def _experts_kernel(first_ref, nblk_ref, x_ref, w1_ref, b1g_ref, b1l_ref, w2_ref, b2_ref, y_ref,
                    w1g_a, w1l_a, w2_a, w1g_b, w1l_b, w2_b, xbuf, ybuf, xsem, ysem):
    s = pl.program_id(0)
    n_exp = pl.num_programs(0) - 1
    e = jnp.maximum(s - 1, 0)
    first = first_ref[e]
    n_blk = jnp.where(s >= 1, nblk_ref[e], 0)
    nxt = jnp.minimum(s, n_exp - 1)
    do_prep = (s < n_exp) & (nblk_ref[nxt] > 0)
    bm = xbuf.shape[1]

    def x_copy(j, slot, first_blk=None):
        blk = (first if first_blk is None else first_blk) + j
        rows = pl.ds(pl.multiple_of(blk * bm, bm), bm)
        return pltpu.make_async_copy(x_ref.at[rows], xbuf.at[slot], xsem.at[slot])

    def y_copy(j, slot):
        rows = pl.ds(pl.multiple_of((first + j) * bm, bm), bm)
        return pltpu.make_async_copy(ybuf.at[slot], y_ref.at[rows], ysem.at[slot])

    def swiglu(ug, ul):
        glu = jnp.minimum(ug, SWIGLU_LIMIT)
        lin = jnp.clip(ul, -SWIGLU_LIMIT, SWIGLU_LIMIT)
        return (glu * jax.nn.sigmoid(SWIGLU_ALPHA * glu) * (lin + 1.0)).astype(BF16)

    def relayout(new):
        w1g_s, w1l_s, w2_s = new
        ch = 256
        half = ch // 2
        for c in range(w1_ref.shape[2] // ch):
            t = w1_ref[0, :, c * ch:(c + 1) * ch].astype(BF16).T
            pairs = pltpu.bitcast(t, jnp.int32)
            cols = slice(c * half, (c + 1) * half)
            w1g_s[:, cols] = lax.bitcast_convert_type(lax.shift_left(pairs, 16), F32).astype(BF16).T
            w1l_s[:, cols] = lax.bitcast_convert_type(pairs & jnp.int32(HIGH_HALF), F32).astype(BF16).T
        w2_s[...] = w2_ref[0].astype(BF16)

    def block(j, cur, new=None):
        w1g_s, w1l_s, w2_s = cur
        slot = j & 1
        x_copy(j, slot).wait()

        @pl.when(j + 1 < n_blk)
        def _():
            x_copy(j + 1, 1 - slot).start(priority=1)

        @pl.when(j >= 2)
        def _():
            y_copy(j - 2, slot).wait()

        if new is not None:
            relayout(new)
        x = _unpack_bf16_pairs(xbuf[slot]).astype(BF16)
        ug = jnp.dot(x, w1g_s[...], preferred_element_type=F32) + b1g_ref[0]
        ul = jnp.dot(x, w1l_s[...], preferred_element_type=F32) + b1l_ref[0]
        y = jnp.dot(swiglu(ug, ul), w2_s[...], preferred_element_type=F32) + b2_ref[0]
        ybuf[slot] = _pack_bf16_pairs(y)
        y_copy(j, slot).start(priority=1)

    def step(new, cur):
        @pl.when((n_blk > 0) & do_prep)
        def _():
            block(0, cur, new)

        @pl.when((n_blk > 0) & jnp.logical_not(do_prep))
        def _():
            block(0, cur)

        @pl.when((n_blk == 0) & do_prep)
        def _():
            relayout(new)

        def later_block(j, carry):
            block(j, cur)
            return carry

        lax.fori_loop(1, n_blk, later_block, 0)

    set_a, set_b = (w1g_a, w1l_a, w2_a), (w1g_b, w1l_b, w2_b)

    @pl.when(s % 2 == 0)
    def _():
        step(set_a, set_b)

    @pl.when(s % 2 == 1)
    def _():
        step(set_b, set_a)

    @pl.when(do_prep)
    def _():
        x_copy(0, 0, first_ref[nxt]).start(priority=1)

    @pl.when(n_blk >= 2)
    def _():
        y_copy(n_blk - 2, n_blk & 1).wait()

    @pl.when(n_blk >= 1)
    def _():
        y_copy(n_blk - 1, (n_blk - 1) & 1).wait()


def _experts(first_blk, n_blk, x_rows, w1, b1g, b1l, w2, b2):
    E, Fh, D = w2.shape
    bm = BLOCK_ROWS
    n_rows = x_rows.shape[0]
    wsel = lambda s, fb, nb: (jnp.minimum(s, E - 1), 0, 0)
    bsel = lambda s, fb, nb: (jnp.maximum(s - 1, 0), 0, 0)
    return pl.pallas_call(
        _experts_kernel,
        out_shape=jax.ShapeDtypeStruct((n_rows, D // 2), jnp.int32),
        grid_spec=pltpu.PrefetchScalarGridSpec(
            num_scalar_prefetch=2,
            grid=(E + 1,),
            in_specs=[pl.BlockSpec(memory_space=pl.ANY),
                      pl.BlockSpec((1, D, 2 * Fh), wsel),
                      pl.BlockSpec((1, 1, Fh), bsel),
                      pl.BlockSpec((1, 1, Fh), bsel),
                      pl.BlockSpec((1, Fh, D), wsel),
                      pl.BlockSpec((1, 1, D), bsel)],
            out_specs=pl.BlockSpec(memory_space=pl.ANY),
            scratch_shapes=[pltpu.VMEM((D, Fh), BF16), pltpu.VMEM((D, Fh), BF16), pltpu.VMEM((Fh, D), BF16),
                            pltpu.VMEM((D, Fh), BF16), pltpu.VMEM((D, Fh), BF16), pltpu.VMEM((Fh, D), BF16),
                            pltpu.VMEM((2, bm, D // 2), jnp.int32), pltpu.VMEM((2, bm, D // 2), jnp.int32),
                            pltpu.SemaphoreType.DMA((2,)), pltpu.SemaphoreType.DMA((2,))]),
        compiler_params=_cparams(("arbitrary",), 56 * 1024 * 1024),
        name="experts",
    )(first_blk, n_blk, x_rows, w1, b1g, b1l, w2, b2)


SC_CORES = 2
SC_SUBCORES = 16
SC_CHUNK = 64


def _sc_gather_rows(table, idx):
    M = idx.shape[0]
    D = table.shape[1]
    workers = SC_CORES * SC_SUBCORES
    per_worker = M // workers
    n_chunks = per_worker // SC_CHUNK
    assert M % workers == 0 and per_worker % (2 * SC_CHUNK) == 0
    mesh = plsc.VectorSubcoreMesh(core_axis_name="c", subcore_axis_name="s")

    @functools.partial(
        pl.kernel, mesh=mesh,
        out_type=jax.ShapeDtypeStruct((M, D), table.dtype),
        scratch_types=[pltpu.VMEM((per_worker,), jnp.int32),
                       pltpu.VMEM((SC_CHUNK, D), table.dtype), pltpu.VMEM((SC_CHUNK, D), table.dtype),
                       pltpu.SemaphoreType.DMA, pltpu.SemaphoreType.DMA],
        name="sc_gather_rows")
    def gather(table_hbm, idx_hbm, out_hbm, idx_v, rows0, rows1, sem0, sem1):
        wid = lax.axis_index("s") * SC_CORES + lax.axis_index("c")
        base = wid * per_worker
        pltpu.sync_copy(idx_hbm.at[pl.ds(base, per_worker)], idx_v)

        def fetch(c, buf, sem):
            off = pl.multiple_of(c * SC_CHUNK, SC_CHUNK)
            return pltpu.make_async_copy(table_hbm.at[idx_v.at[pl.ds(off, SC_CHUNK)]], buf, sem)

        def flush(c, buf):
            off = pl.multiple_of(c * SC_CHUNK, SC_CHUNK)
            pltpu.sync_copy(buf, out_hbm.at[pl.ds(base + off, SC_CHUNK)])

        fetch(0, rows0, sem0).start()

        @pl.loop(0, n_chunks // 2)
        def _(jj):
            c = 2 * jj
            fetch(c + 1, rows1, sem1).start()
            fetch(c, rows0, sem0).wait()
            flush(c, rows0)

            @pl.when(c + 2 < n_chunks)
            def _():
                fetch(c + 2, rows0, sem0).start()

            fetch(c + 1, rows1, sem1).wait()
            flush(c + 1, rows1)

    return gather(table, idx)


SC_SCATTER_CHUNK = 128


def _sc_scatter_rows(rows, dest, n_rows):
    N, W = rows.shape
    workers = SC_CORES * SC_SUBCORES
    per_worker = N // workers
    chunks = per_worker // SC_SCATTER_CHUNK
    assert N % workers == 0 and per_worker % SC_SCATTER_CHUNK == 0
    dest3 = dest.reshape(TOP_K, N // SC_SCATTER_CHUNK, SC_SCATTER_CHUNK)
    mesh = plsc.VectorSubcoreMesh(core_axis_name="c", subcore_axis_name="s")

    @functools.partial(
        pl.kernel, mesh=mesh,
        out_type=jax.ShapeDtypeStruct((n_rows, W), rows.dtype),
        scratch_types=[pltpu.VMEM((TOP_K, chunks, SC_SCATTER_CHUNK), jnp.int32),
                       pltpu.VMEM((SC_SCATTER_CHUNK, W), rows.dtype)],
        name="sc_scatter_rows")
    def scatter(rows_hbm, dest_hbm, out_hbm, idx_v, rows_v):
        wid = lax.axis_index("s") * SC_CORES + lax.axis_index("c")
        for k in range(TOP_K):
            pltpu.sync_copy(dest_hbm.at[k, pl.ds(wid * chunks, chunks)], idx_v.at[k])

        @pl.loop(0, chunks)
        def _(j):
            start = pl.multiple_of(wid * per_worker + j * SC_SCATTER_CHUNK, SC_SCATTER_CHUNK)
            pltpu.sync_copy(rows_hbm.at[pl.ds(start, SC_SCATTER_CHUNK)], rows_v)
            for k in range(TOP_K):
                pltpu.sync_copy(rows_v, out_hbm.at[idx_v.at[k, j]])

    return scatter(rows, dest3)


def _combine_kernel(x1_ref, gate_ref, mod_ref, g_ref, y_ref, o_ref, *, final_norm):
    gate = gate_ref[...]
    moe = gate[:, 0:1] * _unpack_bf16_pairs(y_ref[0])
    for k in range(1, TOP_K):
        moe = moe + gate[:, k:k + 1] * _unpack_bf16_pairs(y_ref[k])
    gt2 = mod_ref[0, 5:6, :]
    x2 = x1_ref[...] + gt2 * moe
    o_ref[...] = _rms(x2) * g_ref[...] if final_norm else x2


def _combine(x1, gate_t, mod3, g_final, y_tok, S, final_norm):
    N, D = x1.shape
    tm = TM_ROWS
    tiles_per_seq = S // tm
    row = lambda i: (i, 0)
    return pl.pallas_call(
        functools.partial(_combine_kernel, final_norm=final_norm),
        out_shape=jax.ShapeDtypeStruct((N, D), F32),
        grid=(N // tm,),
        in_specs=[pl.BlockSpec((tm, D), row),
                  pl.BlockSpec((tm, TOP_K), row),
                  pl.BlockSpec((1, N_MOD, D), lambda i: (i // tiles_per_seq, 0, 0)),
                  pl.BlockSpec((1, D), lambda i: (0, 0)),
                  pl.BlockSpec((TOP_K, tm, D // 2), lambda i: (0, i, 0))],
        out_specs=pl.BlockSpec((tm, D), row),
        compiler_params=_cparams(("arbitrary",)),
        name="combine",
    )(x1, gate_t, mod3, g_final, y_tok)


def _routing_tables(counts, idx, rank):
    bm = BLOCK_ROWS
    counts = counts.astype(jnp.int32)
    padded = (counts + bm - 1) // bm * bm
    pends = jnp.cumsum(padded)
    pstarts = pends - padded
    experts = jnp.arange(N_EXPERTS, dtype=jnp.int32)
    dest = jnp.sum(jnp.where(idx[..., None] == experts, pstarts, 0), axis=-1) + rank
    return dest.astype(jnp.int32), (pstarts // bm).astype(jnp.int32), (padded // bm).astype(jnp.int32)


def _extended_in_weights(w_in, b_in):
    a_q = SWA_Q_HEADS * HEAD_DIM
    a_kv = SWA_KV_HEADS * HEAD_DIM
    b_w = DIFF_HEADS * DIFF_V_DIM
    widths = (a_q, 2 * a_kv, 2 * a_kv, b_w, b_w, b_w)
    return w_in.astype(BF16), b_in.reshape(1, -1), widths


def kernel(x, c, positions, w_ada, b_ada, g_mix, w_in, b_in, attn_sinks, lambda_q1, lambda_k1, lambda_q2,
           lambda_k2, g_subln, w_out, b_out, g_ffn, w_router, b_router, w1, b1, w2, b2, g_final):
    B, S, D = x.shape
    N = B * S
    depth = w_ada.shape[0]
    n_rows = (N * TOP_K + N_EXPERTS * (BLOCK_ROWS - 1) + BLOCK_ROWS - 1) // BLOCK_ROWS * BLOCK_ROWS

    inv = 1.0 / (ROPE_THETA ** (jnp.arange(0, HEAD_DIM, 2, dtype=F32) / HEAD_DIM))
    inv_lane = jnp.tile(inv, LANES // (HEAD_DIM // 2)).reshape(1, LANES)
    pos2 = positions.reshape(N, 1)
    xcur = x.reshape(N, D)

    for layer in range(depth):
        last = layer == depth - 1
        lambda_init = 0.8 - 0.6 * math.exp(-0.3 * layer)
        mod3 = _adaln(c, w_ada[layer], b_ada[layer]).reshape(B, N_MOD, D)

        w_ext, b_ext, widths = _extended_in_weights(w_in[layer], b_in[layer])
        qa, ka2, va2, qdt, kd, vdt = _inproj(xcur, pos2, inv_lane, mod3, g_mix[layer].reshape(1, D),
                                             w_ext, b_ext, S, widths)
        out_a = _swa(attn_sinks[layer], qa, ka2, va2, B, S)
        lam_vecs = jnp.stack([lambda_q1[layer], lambda_k1[layer], lambda_q2[layer], lambda_k2[layer]])
        out_b = _diffattn(lam_vecs, g_subln[layer].reshape(1, DIFF_V_DIM), qdt, kd, vdt, B, S, lambda_init)

        x1, h2, idx, gate, rank, counts = _outproj(
            out_a, out_b, xcur, mod3, w_out[layer].astype(BF16), b_out[layer].reshape(1, D),
            g_ffn[layer].reshape(1, D), w_router[layer].T, b_router[layer].reshape(N_EXPERTS, 1), S)

        dest, first_blk, n_blk = _routing_tables(counts[:, 0], idx, rank)
        x_rows = _sc_scatter_rows(h2, dest, n_rows)
        y_rows = _experts(first_blk, n_blk, x_rows, w1[layer],
                          b1[layer][:, None, 0::2], b1[layer][:, None, 1::2],
                          w2[layer], b2[layer][:, None, :])
        y_tok = _sc_gather_rows(y_rows, dest.reshape(-1)).reshape(TOP_K, N, D // 2)
        xcur = _combine(x1, gate.T, mod3, g_final.reshape(1, D), y_tok, S, final_norm=last)
    return xcur.reshape(B, S, D)
```

```python
import functools
import math

import jax
import jax.numpy as jnp
from jax import lax
from jax.experimental import pallas as pl
from jax.experimental.pallas import tpu as pltpu
from jax.experimental.pallas import tpu_sc as plsc

HEAD_DIM = 64
SWA_Q_HEADS = 8
SWA_KV_HEADS = 2
SWA_GROUP = SWA_Q_HEADS // SWA_KV_HEADS
WINDOW = 128
DIFF_HEADS = 4
DIFF_V_DIM = 2 * HEAD_DIM
ROPE_THETA = 10000.0
N_EXPERTS = 32
TOP_K = 4
SWIGLU_ALPHA = 1.702
SWIGLU_LIMIT = 7.0
EPS = 1e-5
N_MOD = 6

LANES = 128
F32 = jnp.float32
BF16 = jnp.bfloat16
NEG_INF = float("-inf")

TM_PROJ = 1024
TQ_SWA = 512
VT_ROWS = DIFF_V_DIM + 16
TM_OUT = 512
TM_ROWS = 1024
BLOCK_ROWS = 512
VMEM_LIMIT = 48 * 1024 * 1024


def _cparams(sem, vmem=VMEM_LIMIT):
    return pltpu.CompilerParams(dimension_semantics=sem, vmem_limit_bytes=vmem)


def _adaln_kernel(ct_ref, w_ref, b_ref, o_ref):
    c = ct_ref[...]
    cond = c * jax.nn.sigmoid(c)
    w = w_ref[...]
    rows = [jnp.sum(w * cond[:, b:b + 1], axis=0, keepdims=True) for b in range(c.shape[1])]
    o_ref[...] = jnp.concatenate(rows, axis=0) + b_ref[...]


def _adaln(c, w_ada, b_ada):
    B, D = c.shape
    n_out = w_ada.shape[1]
    tn = 1024
    return pl.pallas_call(
        _adaln_kernel,
        out_shape=jax.ShapeDtypeStruct((B, n_out), F32),
        grid=(n_out // tn,),
        in_specs=[pl.BlockSpec((D, B), lambda j: (0, 0)),
                  pl.BlockSpec((D, tn), lambda j: (0, j)),
                  pl.BlockSpec((1, tn), lambda j: (0, j))],
        out_specs=pl.BlockSpec((B, tn), lambda j: (0, j)),
        compiler_params=_cparams(("arbitrary",)),
        name="adaln",
    )(c.T, w_ada, b_ada.reshape(1, n_out))


def _rms(x):
    return x * lax.rsqrt(jnp.mean(x * x, axis=-1, keepdims=True) + EPS)


HIGH_HALF = -65536


def _pack_bf16_pairs(v):
    bits = lax.bitcast_convert_type(v.astype(BF16).astype(F32), jnp.int32)
    half = v.shape[1] // 2
    return lax.shift_right_logical(bits[:, :half], 16) | (bits[:, half:] & jnp.int32(HIGH_HALF))


def _unpack_bf16_pairs(w):
    return jnp.concatenate([lax.bitcast_convert_type(lax.shift_left(w, 16), F32),
                            lax.bitcast_convert_type(w & jnp.int32(HIGH_HALF), F32)], axis=1)


def _inproj_kernel(x_ref, pos_ref, inv_ref, mod_ref, g_ref, w_ref, b_ref,
                   qa_ref, ka_ref, va_ref, qd_ref, kd_ref, vd_ref):
    x = x_ref[...]
    sh = mod_ref[0, 0:1, :]
    sc = mod_ref[0, 1:2, :]
    h = _rms(x) * g_ref[...] * (1.0 + sc) + sh
    proj = jnp.dot(h.astype(BF16), w_ref[...], preferred_element_type=F32) + b_ref[...]

    lane = lax.broadcasted_iota(jnp.int32, (1, LANES), 1)
    first_half = (lane & (HEAD_DIM - 1)) < (HEAD_DIM // 2)
    n_freq = HEAD_DIM // 2
    groups = LANES // n_freq
    tm = x.shape[0]
    rows = tm // groups
    group = lax.shift_right_logical(lane, n_freq.bit_length() - 1)
    pos = pos_ref[...].astype(F32)
    pos_q = pos[0:rows]
    for g in range(1, groups):
        pos_q = jnp.where(group == g, pos[g * rows:(g + 1) * rows], pos_q)
    ang_q = pos_q * inv_ref[...]

    def spread(table_q):
        blocks = []
        for g in range(groups):
            only = jnp.where(group == g, table_q, 0.0)
            full = only
            for r in range(1, groups):
                full = full + pltpu.roll(only, r * n_freq, axis=1)
            blocks.append(full)
        return jnp.concatenate(blocks, axis=0)

    cos = spread(jnp.cos(ang_q))
    sin = spread(jnp.sin(ang_q))
    sin_signed = jnp.where(first_half, -sin, sin)

    def rope(t):
        partner = jnp.where(first_half,
                            pltpu.roll(t, LANES - HEAD_DIM // 2, axis=1),
                            pltpu.roll(t, HEAD_DIM // 2, axis=1))
        return t * cos + partner * sin_signed

    lo_half = lane < HEAD_DIM

    def emit(out_ref, col0, width, rotary, scale, transposed, doubled=False):
        for j in range(width // LANES):
            t = proj[:, col0 + j * LANES: col0 + (j + 1) * LANES]
            if rotary:
                t = rope(t)
            if scale != 1.0:
                t = t * scale
            if doubled:
                swapped = pltpu.roll(t, HEAD_DIM, axis=1)
                out_ref[:, 2 * j * LANES:(2 * j + 1) * LANES] = jnp.where(lo_half, t, swapped).astype(out_ref.dtype)
                out_ref[:, (2 * j + 1) * LANES:(2 * j + 2) * LANES] = jnp.where(lo_half, swapped, t).astype(out_ref.dtype)
            elif transposed:
                rows = out_ref.shape[2] // (width // LANES)
                out_ref[0, 0, j * rows:j * rows + LANES, :] = t.T.astype(out_ref.dtype)
                if rows > LANES:
                    fill = lax.broadcasted_iota(jnp.int32, (rows - LANES, t.shape[0]), 0) == 0
                    out_ref[0, 0, j * rows + LANES:(j + 1) * rows, :] = fill.astype(out_ref.dtype)
            else:
                out_ref[:, j * LANES:(j + 1) * LANES] = t.astype(out_ref.dtype)

    swa_scale = 1.0 / math.sqrt(HEAD_DIM)
    diff_scale = math.log2(math.e) / math.sqrt(HEAD_DIM)
    col = 0
    for out_ref, width, rotary, scale, transposed, doubled in (
            (qa_ref, qa_ref.shape[1], True, swa_scale, False, False),
            (ka_ref, ka_ref.shape[1] // 2, True, 1.0, False, True),
            (va_ref, va_ref.shape[1] // 2, False, 1.0, False, True),
            (qd_ref, qd_ref.shape[2], True, diff_scale, True, False),
            (kd_ref, kd_ref.shape[1], True, 1.0, False, False),
            (vd_ref, vd_ref.shape[2] // VT_ROWS * LANES, False, 1.0, True, False)):
        emit(out_ref, col, width, rotary, scale, transposed, doubled)
        col += width


def _inproj(x2, pos2, inv_lane, mod3, g_mix, w_ext, b_ext, S, widths):
    N, D = x2.shape
    tm = TM_PROJ
    C = w_ext.shape[1]
    tiles_per_seq = S // tm
    row = lambda i: (i, 0)
    t_rows = (0, 0, 0, widths[3], 0, widths[5] // LANES * VT_ROWS)
    out_shape, out_specs = [], []
    for w, tr in zip(widths, t_rows):
        if tr:
            out_shape.append(jax.ShapeDtypeStruct((N // S, tiles_per_seq, tr, tm), BF16))
            out_specs.append(pl.BlockSpec((1, 1, tr, tm), lambda i: (i // tiles_per_seq, i % tiles_per_seq, 0, 0)))
        else:
            out_shape.append(jax.ShapeDtypeStruct((N, w), BF16))
            out_specs.append(pl.BlockSpec((tm, w), row))
    return pl.pallas_call(
        _inproj_kernel,
        out_shape=out_shape,
        grid=(N // tm,),
        in_specs=[pl.BlockSpec((tm, D), row),
                  pl.BlockSpec((tm, 1), row),
                  pl.BlockSpec((1, LANES), lambda i: (0, 0)),
                  pl.BlockSpec((1, N_MOD, D), lambda i: (i // tiles_per_seq, 0, 0)),
                  pl.BlockSpec((1, D), lambda i: (0, 0)),
                  pl.BlockSpec((D, C), lambda i: (0, 0)),
                  pl.BlockSpec((1, C), lambda i: (0, 0))],
        out_specs=out_specs,
        compiler_params=_cparams(("arbitrary",)),
        name="inproj",
    )(x2, pos2, inv_lane, mod3, g_mix, w_ext, b_ext)


def _swa_kernel(sink_ref, q_ref, kc_ref, kp_ref, vc_ref, vp_ref, o_ref):
    i = pl.program_id(1)
    tq = q_ref.shape[0]
    lane = lax.broadcasted_iota(jnp.int32, (1, LANES), 1)
    lo = lane < HEAD_DIM
    qi = lax.broadcasted_iota(jnp.int32, (WINDOW, 2 * WINDOW), 0) + WINDOW
    kj = lax.broadcasted_iota(jnp.int32, (WINDOW, 2 * WINDOW), 1)
    band = (qi - kj >= 0) & (qi - kj < WINDOW)
    dn = (((1,), (1,)), ((), ()))
    for c in range(tq // WINDOW):
        if c == 0:
            kcat = jnp.concatenate([kp_ref[...], kc_ref[0:WINDOW, :]], axis=0)
            vcat = jnp.concatenate([vp_ref[...], vc_ref[0:WINDOW, :]], axis=0)
            mask = band & (kj >= jnp.where(i > 0, 0, WINDOW))
        else:
            kcat = kc_ref[(c - 1) * WINDOW:(c + 1) * WINDOW, :]
            vcat = vc_ref[(c - 1) * WINDOW:(c + 1) * WINDOW, :]
            mask = band
        for j in range(SWA_KV_HEADS):
            kj2 = kcat[:, j * LANES:(j + 1) * LANES]
            vj2 = vcat[:, j * LANES:(j + 1) * LANES]
            zero = jnp.zeros_like(kj2)
            k_halves = (jnp.where(lo, kj2, zero), jnp.where(lo, zero, kj2))
            v_halves = (jnp.where(lo, vj2, zero), jnp.where(lo, zero, vj2))
            for p in range(SWA_GROUP // 2):
                g = j * (SWA_GROUP // 2) + p
                q = q_ref[c * WINDOW:(c + 1) * WINDOW, g * LANES:(g + 1) * LANES]
                out = jnp.zeros((WINDOW, LANES), F32)
                for half in range(2):
                    sink = sink_ref[2 * g + half]
                    s = lax.dot_general(q, k_halves[half], dn, preferred_element_type=F32)
                    s = jnp.where(mask, s, NEG_INF)
                    m = jnp.maximum(jnp.max(s, axis=1, keepdims=True), sink)
                    e = jnp.exp(s - m)
                    denom = jnp.sum(e, axis=1, keepdims=True) + jnp.exp(sink - m)
                    pv = jnp.dot(e.astype(BF16), v_halves[half], preferred_element_type=F32)
                    out = out + pv / denom
                o_ref[c * WINDOW:(c + 1) * WINDOW, g * LANES:(g + 1) * LANES] = out.astype(o_ref.dtype)


def _swa(sinks, qa, ka2, va2, B, S):
    N = qa.shape[0]
    tq = TQ_SWA
    nq = S // tq
    wpt = tq // WINDOW
    wps = S // WINDOW
    cur = lambda b, i: (b * nq + i, 0)
    prev = lambda b, i: (b * wps + jnp.maximum(i * wpt - 1, 0), 0)
    return pl.pallas_call(
        _swa_kernel,
        out_shape=jax.ShapeDtypeStruct((N, qa.shape[1]), BF16),
        grid=(B, nq),
        in_specs=[pl.BlockSpec(memory_space=pltpu.SMEM),
                  pl.BlockSpec((tq, qa.shape[1]), cur),
                  pl.BlockSpec((tq, ka2.shape[1]), cur),
                  pl.BlockSpec((WINDOW, ka2.shape[1]), prev),
                  pl.BlockSpec((tq, va2.shape[1]), cur),
                  pl.BlockSpec((WINDOW, va2.shape[1]), prev)],
        out_specs=pl.BlockSpec((tq, qa.shape[1]), cur),
        compiler_params=_cparams(("arbitrary", "arbitrary")),
        name="swa",
    )(sinks, qa, ka2, ka2, va2, va2)


def _diff_kernel(lam_ref, g_ref, qt_ref, k_ref, vt_ref, o_ref, sa_ref, sb_ref, m_ref, acc_ref, *, lambda_init):
    i = pl.program_id(2)
    tq = qt_ref.shape[3]
    tk = vt_ref.shape[3]
    qt = qt_ref[0, 0]
    lane = lax.broadcasted_iota(jnp.int32, (1, LANES), 1)
    lo = lane < HEAD_DIM
    m_ref[...] = jnp.full(m_ref.shape, NEG_INF, F32)
    acc_ref[...] = jnp.zeros(acc_ref.shape, F32)

    def scores(c, s_ref):
        k = k_ref[pl.ds(pl.multiple_of(c * tk, tk), tk), :]
        zero = jnp.zeros_like(k)
        s_ref[0] = jnp.dot(jnp.where(lo, k, zero), qt, preferred_element_type=F32)
        s_ref[1] = jnp.dot(jnp.where(lo, zero, k), qt, preferred_element_type=F32)

    def consume(c, s_ref, first_key):
        vt = vt_ref[0, c]
        for mp in range(2):
            s = s_ref[mp]
            if first_key is not None:
                kpos = lax.broadcasted_iota(jnp.int32, (tk, tq), 0) + first_key
                qpos = lax.broadcasted_iota(jnp.int32, (tk, tq), 1)
                s = jnp.where(kpos <= qpos, s, NEG_INF)
            m_prev = m_ref[mp]
            m_new = jnp.maximum(m_prev, jnp.max(s, axis=0, keepdims=True))
            alpha = jnp.exp2(m_prev - m_new)
            p = jnp.exp2(s - m_new).astype(BF16)
            acc_ref[mp] = alpha * acc_ref[mp] + jnp.dot(vt, p, preferred_element_type=F32)
            m_ref[mp] = m_new

    scores(0, sa_ref)

    def pair(jj, carry):
        c = 2 * jj
        scores(c + 1, sb_ref)
        consume(c, sa_ref, None)
        scores(c + 2, sa_ref)
        consume(c + 1, sb_ref, None)
        return carry

    lax.fori_loop(0, lax.shift_right_logical(i, 1), pair, 0)

    @pl.when(i % 2 == 0)
    def _():
        consume(i, sa_ref, 0)

    @pl.when(i % 2 == 1)
    def _():
        scores(i, sb_ref)
        consume(i - 1, sa_ref, None)
        consume(i, sb_ref, 0)

    lq1, lk1, lq2, lk2 = (lam_ref[r:r + 1, :] for r in range(4))
    lam = (jnp.exp(jnp.sum(lq1 * lk1, axis=1, keepdims=True))
           - jnp.exp(jnp.sum(lq2 * lk2, axis=1, keepdims=True)) + lambda_init)
    d = DIFF_V_DIM
    ot = (acc_ref[0, 0:d, :] / acc_ref[0, d:d + 1, :]
          - lam * (acc_ref[1, 0:d, :] / acc_ref[1, d:d + 1, :]))
    ot = ot * lax.rsqrt(jnp.mean(ot * ot, axis=0, keepdims=True) + EPS)
    o_ref[...] = (ot.T * g_ref[...] * (1.0 - lambda_init)).astype(o_ref.dtype)


def _diffattn(lam_vecs, g_subln, qdt, kd, vdt, B, S, lambda_init):
    N, C = kd.shape
    tq = qdt.shape[3]
    nq = S // tq
    return pl.pallas_call(
        functools.partial(_diff_kernel, lambda_init=lambda_init),
        out_shape=jax.ShapeDtypeStruct((N, C), BF16),
        grid=(B, DIFF_HEADS, nq),
        in_specs=[pl.BlockSpec((4, HEAD_DIM), lambda b, h, i: (0, 0)),
                  pl.BlockSpec((1, DIFF_V_DIM), lambda b, h, i: (0, 0)),
                  pl.BlockSpec((1, 1, LANES, tq), lambda b, h, i: (b, i, h, 0)),
                  pl.BlockSpec((S, LANES), lambda b, h, i: (b, h)),
                  pl.BlockSpec((1, nq, VT_ROWS, tq), lambda b, h, i: (b, 0, h, 0))],
        out_specs=pl.BlockSpec((tq, LANES), lambda b, h, i: (b * nq + i, h)),
        scratch_shapes=[pltpu.VMEM((2, tq, tq), F32), pltpu.VMEM((2, tq, tq), F32),
                        pltpu.VMEM((2, 1, tq), F32), pltpu.VMEM((2, VT_ROWS, tq), F32)],
        compiler_params=_cparams(("arbitrary", "arbitrary", "arbitrary")),
        name="diffattn",
    )(lam_vecs, g_subln, qdt, kd, vdt)


def _outproj_kernel(oa_ref, ob_ref, x_ref, mod_ref, wo_ref, bo_ref, g_ref, wr_ref, br_ref,
                    x1_ref, h2_ref, idx_ref, gate_ref, rank_ref, cnt_ref, carry_ref):
    i = pl.program_id(0)
    tm = x_ref.shape[0]
    half = oa_ref.shape[1]

    @pl.when(i == 0)
    def _():
        carry_ref[...] = jnp.zeros(carry_ref.shape, F32)

    gt1 = mod_ref[0, 2:3, :]
    sh2 = mod_ref[0, 3:4, :]
    sc2 = mod_ref[0, 4:5, :]
    mixed = (jnp.dot(oa_ref[...], wo_ref[0:half, :], preferred_element_type=F32)
             + jnp.dot(ob_ref[...], wo_ref[half:, :], preferred_element_type=F32) + bo_ref[...])
    x1 = x_ref[...] + gt1 * mixed
    x1_ref[...] = x1
    h2 = _rms(x1) * g_ref[...] * (1.0 + sc2) + sh2
    dn = (((1,), (1,)), ((), ()))
    h_hi = h2.astype(BF16)
    h2_ref[...] = _pack_bf16_pairs(h2)
    h_lo = (h2 - h_hi.astype(F32)).astype(BF16)
    w = wr_ref[...]
    w_hi = w.astype(BF16)
    w_lo = (w - w_hi.astype(F32)).astype(BF16)
    logits = (lax.dot_general(w_hi, h_hi, dn, preferred_element_type=F32)
              + lax.dot_general(w_hi, h_lo, dn, preferred_element_type=F32)
              + lax.dot_general(w_lo, h_hi, dn, preferred_element_type=F32)
              + br_ref[...])

    eidx = lax.broadcasted_iota(jnp.int32, logits.shape, 0)
    vals = logits
    onehots, top_vals, top_idx = [], [], []
    for _k in range(TOP_K):
        mx = jnp.max(vals, axis=0, keepdims=True)
        sel = jnp.min(jnp.where(vals == mx, eidx, N_EXPERTS), axis=0, keepdims=True)
        oh = eidx == sel
        onehots.append(oh)
        top_vals.append(mx)
        top_idx.append(sel)
        vals = jnp.where(oh, NEG_INF, vals)
    exps = [jnp.exp(v - top_vals[0]) for v in top_vals]
    denom = exps[0] + exps[1] + exps[2] + exps[3]
    gate_ref[...] = jnp.concatenate([e / denom for e in exps], axis=0)
    idx_ref[...] = jnp.concatenate(top_idx, axis=0)

    member = (onehots[0] | onehots[1] | onehots[2] | onehots[3])
    member_f = member.astype(F32)
    t_src = lax.broadcasted_iota(jnp.int32, (tm, tm), 0)
    t_dst = lax.broadcasted_iota(jnp.int32, (tm, tm), 1)
    before = (t_src < t_dst).astype(BF16)
    prefix = jnp.dot(member.astype(BF16), before, preferred_element_type=F32) + carry_ref[...]
    ranks = [jnp.sum(jnp.where(oh, prefix, 0.0), axis=0, keepdims=True) for oh in onehots]
    rank_ref[...] = jnp.concatenate(ranks, axis=0).astype(jnp.int32)
    carry_ref[...] = carry_ref[...] + jnp.sum(member_f, axis=1, keepdims=True)
    cnt_ref[...] = jnp.broadcast_to(carry_ref[...], cnt_ref.shape)


def _outproj(out_a, out_b, x2, mod3, w_out, b_out, g_ffn, wr_t, br_col, S):
    N, D = x2.shape
    tm = TM_OUT
    tiles_per_seq = S // tm
    row = lambda i: (i, 0)
    colb = lambda i: (0, i)
    const = lambda i: (0, 0)
    return pl.pallas_call(
        _outproj_kernel,
        out_shape=[jax.ShapeDtypeStruct((N, D), F32), jax.ShapeDtypeStruct((N, D // 2), jnp.int32),
                   jax.ShapeDtypeStruct((TOP_K, N), jnp.int32), jax.ShapeDtypeStruct((TOP_K, N), F32),
                   jax.ShapeDtypeStruct((TOP_K, N), jnp.int32), jax.ShapeDtypeStruct((N_EXPERTS, LANES), F32)],
        grid=(N // tm,),
        in_specs=[pl.BlockSpec((tm, out_a.shape[1]), row),
                  pl.BlockSpec((tm, out_b.shape[1]), row),
                  pl.BlockSpec((tm, D), row),
                  pl.BlockSpec((1, N_MOD, D), lambda i: (i // tiles_per_seq, 0, 0)),
                  pl.BlockSpec(w_out.shape, const),
                  pl.BlockSpec((1, D), const),
                  pl.BlockSpec((1, D), const),
                  pl.BlockSpec(wr_t.shape, const),
                  pl.BlockSpec((N_EXPERTS, 1), const)],
        out_specs=[pl.BlockSpec((tm, D), row), pl.BlockSpec((tm, D // 2), row),
                   pl.BlockSpec((TOP_K, tm), colb), pl.BlockSpec((TOP_K, tm), colb),
                   pl.BlockSpec((TOP_K, tm), colb), pl.BlockSpec((N_EXPERTS, LANES), const)],
        scratch_shapes=[pltpu.VMEM((N_EXPERTS, 1), F32)],
        compiler_params=_cparams(("arbitrary",)),
        name="outproj_router",
    )(out_a, out_b, x2, mod3, w_out, b_out, g_ffn, wr_t, br_col)


def _experts_kernel(first_ref, nblk_ref, x_ref, w1_ref, b1g_ref, b1l_ref, w2_ref, b2_ref, y_ref,
                    w1g_a, w1l_a, w2_a, w1g_b, w1l_b, w2_b, xbuf, ybuf, xsem, ysem):
    s = pl.program_id(0)
    n_exp = pl.num_programs(0) - 1
    e = jnp.maximum(s - 1, 0)
    first = first_ref[e]
    n_blk = jnp.where(s >= 1, nblk_ref[e], 0)
    nxt = jnp.minimum(s, n_exp - 1)
    do_prep = (s < n_exp) & (nblk_ref[nxt] > 0)
    bm = xbuf.shape[1]

    def x_copy(j, slot, first_blk=None):
        blk = (first if first_blk is None else first_blk) + j
        rows = pl.ds(pl.multiple_of(blk * bm, bm), bm)
        return pltpu.make_async_copy(x_ref.at[rows], xbuf.at[slot], xsem.at[slot])

    def y_copy(j, slot):
        rows = pl.ds(pl.multiple_of((first + j) * bm, bm), bm)
        return pltpu.make_async_copy(ybuf.at[slot], y_ref.at[rows], ysem.at[slot])

    def swiglu(ug, ul):
        glu = jnp.minimum(ug, SWIGLU_LIMIT)
        lin = jnp.clip(ul, -SWIGLU_LIMIT, SWIGLU_LIMIT)
        return (glu * jax.nn.sigmoid(SWIGLU_ALPHA * glu) * (lin + 1.0)).astype(BF16)

    def relayout(new):
        w1g_s, w1l_s, w2_s = new
        ch = 256
        half = ch // 2
        for c in range(w1_ref.shape[2] // ch):
            t = w1_ref[0, :, c * ch:(c + 1) * ch].astype(BF16).T
            pairs = pltpu.bitcast(t, jnp.int32)
            cols = slice(c * half, (c + 1) * half)
            w1g_s[:, cols] = lax.bitcast_convert_type(lax.shift_left(pairs, 16), F32).astype(BF16).T
            w1l_s[:, cols] = lax.bitcast_convert_type(pairs & jnp.int32(HIGH_HALF), F32).astype(BF16).T
        w2_s[...] = w2_ref[0].astype(BF16)

    def block(j, cur, new=None):
        w1g_s, w1l_s, w2_s = cur
        slot = j & 1
        x_copy(j, slot).wait()

        @pl.when(j + 1 < n_blk)
        def _():
            x_copy(j + 1, 1 - slot).start(priority=1)

        @pl.when(j >= 2)
        def _():
            y_copy(j - 2, slot).wait()

        if new is not None:
            relayout(new)
        x = _unpack_bf16_pairs(xbuf[slot]).astype(BF16)
        ug = jnp.dot(x, w1g_s[...], preferred_element_type=F32) + b1g_ref[0]
        ul = jnp.dot(x, w1l_s[...], preferred_element_type=F32) + b1l_ref[0]
        y = jnp.dot(swiglu(ug, ul), w2_s[...], preferred_element_type=F32) + b2_ref[0]
        ybuf[slot] = _pack_bf16_pairs(y)
        y_copy(j, slot).start(priority=1)

    def step(new, cur):
        @pl.when((n_blk > 0) & do_prep)
        def _():
            block(0, cur, new)

        @pl.when((n_blk > 0) & jnp.logical_not(do_prep))
        def _():
            block(0, cur)

        @pl.when((n_blk == 0) & do_prep)
        def _():
            relayout(new)

        def later_block(j, carry):
            block(j, cur)
            return carry

        lax.fori_loop(1, n_blk, later_block, 0)

    set_a, set_b = (w1g_a, w1l_a, w2_a), (w1g_b, w1l_b, w2_b)

    @pl.when(s % 2 == 0)
    def _():
        step(set_a, set_b)

    @pl.when(s % 2 == 1)
    def _():
        step(set_b, set_a)

    @pl.when(do_prep)
    def _():
        x_copy(0, 0, first_ref[nxt]).start(priority=1)

    @pl.when(n_blk >= 2)
    def _():
        y_copy(n_blk - 2, n_blk & 1).wait()

    @pl.when(n_blk >= 1)
    def _():
        y_copy(n_blk - 1, (n_blk - 1) & 1).wait()


def _experts(first_blk, n_blk, x_rows, w1, b1g, b1l, w2, b2):
    E, Fh, D = w2.shape
    bm = BLOCK_ROWS
    n_rows = x_rows.shape[0]
    wsel = lambda s, fb, nb: (jnp.minimum(s, E - 1), 0, 0)
    bsel = lambda s, fb, nb: (jnp.maximum(s - 1, 0), 0, 0)
    return pl.pallas_call(
        _experts_kernel,
        out_shape=jax.ShapeDtypeStruct((n_rows, D // 2), jnp.int32),
        grid_spec=pltpu.PrefetchScalarGridSpec(
            num_scalar_prefetch=2,
            grid=(E + 1,),
            in_specs=[pl.BlockSpec(memory_space=pl.ANY),
                      pl.BlockSpec((1, D, 2 * Fh), wsel),
                      pl.BlockSpec((1, 1, Fh), bsel),
                      pl.BlockSpec((1, 1, Fh), bsel),
                      pl.BlockSpec((1, Fh, D), wsel),
                      pl.BlockSpec((1, 1, D), bsel)],
            out_specs=pl.BlockSpec(memory_space=pl.ANY),
            scratch_shapes=[pltpu.VMEM((D, Fh), BF16), pltpu.VMEM((D, Fh), BF16), pltpu.VMEM((Fh, D), BF16),
                            pltpu.VMEM((D, Fh), BF16), pltpu.VMEM((D, Fh), BF16), pltpu.VMEM((Fh, D), BF16),
                            pltpu.VMEM((2, bm, D // 2), jnp.int32), pltpu.VMEM((2, bm, D // 2), jnp.int32),
                            pltpu.SemaphoreType.DMA((2,)), pltpu.SemaphoreType.DMA((2,))]),
        compiler_params=_cparams(("arbitrary",), 56 * 1024 * 1024),
        name="experts",
    )(first_blk, n_blk, x_rows, w1, b1g, b1l, w2, b2)


SC_CORES = 2
SC_SUBCORES = 16
SC_CHUNK = 64


def _sc_gather_rows(table, idx):
    M = idx.shape[0]
    D = table.shape[1]
    workers = SC_CORES * SC_SUBCORES
    per_worker = M // workers
    n_chunks = per_worker // SC_CHUNK
    assert M % workers == 0 and per_worker % (2 * SC_CHUNK) == 0
    mesh = plsc.VectorSubcoreMesh(core_axis_name="c", subcore_axis_name="s")

    @functools.partial(
        pl.kernel, mesh=mesh,
        out_type=jax.ShapeDtypeStruct((M, D), table.dtype),
        scratch_types=[pltpu.VMEM((per_worker,), jnp.int32),
                       pltpu.VMEM((SC_CHUNK, D), table.dtype), pltpu.VMEM((SC_CHUNK, D), table.dtype),
                       pltpu.SemaphoreType.DMA, pltpu.SemaphoreType.DMA],
        name="sc_gather_rows")
    def gather(table_hbm, idx_hbm, out_hbm, idx_v, rows0, rows1, sem0, sem1):
        wid = lax.axis_index("s") * SC_CORES + lax.axis_index("c")
        base = wid * per_worker
        pltpu.sync_copy(idx_hbm.at[pl.ds(base, per_worker)], idx_v)

        def fetch(c, buf, sem):
            off = pl.multiple_of(c * SC_CHUNK, SC_CHUNK)
            return pltpu.make_async_copy(table_hbm.at[idx_v.at[pl.ds(off, SC_CHUNK)]], buf, sem)

        def flush(c, buf):
            off = pl.multiple_of(c * SC_CHUNK, SC_CHUNK)
            pltpu.sync_copy(buf, out_hbm.at[pl.ds(base + off, SC_CHUNK)])

        fetch(0, rows0, sem0).start()

        @pl.loop(0, n_chunks // 2)
        def _(jj):
            c = 2 * jj
            fetch(c + 1, rows1, sem1).start()
            fetch(c, rows0, sem0).wait()
            flush(c, rows0)

            @pl.when(c + 2 < n_chunks)
            def _():
                fetch(c + 2, rows0, sem0).start()

            fetch(c + 1, rows1, sem1).wait()
            flush(c + 1, rows1)

    return gather(table, idx)


SC_SCATTER_CHUNK = 128


def _sc_scatter_rows(rows, dest, n_rows):
    N, W = rows.shape
    workers = SC_CORES * SC_SUBCORES
    per_worker = N // workers
    chunks = per_worker // SC_SCATTER_CHUNK
    assert N % workers == 0 and per_worker % SC_SCATTER_CHUNK == 0
    dest3 = dest.reshape(TOP_K, N // SC_SCATTER_CHUNK, SC_SCATTER_CHUNK)
    mesh = plsc.VectorSubcoreMesh(core_axis_name="c", subcore_axis_name="s")

    @functools.partial(
        pl.kernel, mesh=mesh,
        out_type=jax.ShapeDtypeStruct((n_rows, W), rows.dtype),
        scratch_types=[pltpu.VMEM((TOP_K, chunks, SC_SCATTER_CHUNK), jnp.int32),
                       pltpu.VMEM((SC_SCATTER_CHUNK, W), rows.dtype)],
        name="sc_scatter_rows")
    def scatter(rows_hbm, dest_hbm, out_hbm, idx_v, rows_v):
        wid = lax.axis_index("s") * SC_CORES + lax.axis_index("c")
        for k in range(TOP_K):
            pltpu.sync_copy(dest_hbm.at[k, pl.ds(wid * chunks, chunks)], idx_v.at[k])

        @pl.loop(0, chunks)
        def _(j):
            start = pl.multiple_of(wid * per_worker + j * SC_SCATTER_CHUNK, SC_SCATTER_CHUNK)
            pltpu.sync_copy(rows_hbm.at[pl.ds(start, SC_SCATTER_CHUNK)], rows_v)
            for k in range(TOP_K):
                pltpu.sync_copy(rows_v, out_hbm.at[idx_v.at[k, j]])

    return scatter(rows, dest3)


def _combine_kernel(x1_ref, gate_ref, mod_ref, g_ref, y_ref, o_ref, *, final_norm):
    gate = gate_ref[...]
    moe = gate[:, 0:1] * _unpack_bf16_pairs(y_ref[0])
    for k in range(1, TOP_K):
        moe = moe + gate[:, k:k + 1] * _unpack_bf16_pairs(y_ref[k])
    gt2 = mod_ref[0, 5:6, :]
    x2 = x1_ref[...] + gt2 * moe
    o_ref[...] = _rms(x2) * g_ref[...] if final_norm else x2


def _combine(x1, gate_t, mod3, g_final, y_tok, S, final_norm):
    N, D = x1.shape
    tm = TM_ROWS
    tiles_per_seq = S // tm
    row = lambda i: (i, 0)
    return pl.pallas_call(
        functools.partial(_combine_kernel, final_norm=final_norm),
        out_shape=jax.ShapeDtypeStruct((N, D), F32),
        grid=(N // tm,),
        in_specs=[pl.BlockSpec((tm, D), row),
                  pl.BlockSpec((tm, TOP_K), row),
                  pl.BlockSpec((1, N_MOD, D), lambda i: (i // tiles_per_seq, 0, 0)),
                  pl.BlockSpec((1, D), lambda i: (0, 0)),
                  pl.BlockSpec((TOP_K, tm, D // 2), lambda i: (0, i, 0))],
        out_specs=pl.BlockSpec((tm, D), row),
        compiler_params=_cparams(("arbitrary",)),
        name="combine",
    )(x1, gate_t, mod3, g_final, y_tok)


def _routing_tables(counts, idx, rank):
    bm = BLOCK_ROWS
    counts = counts.astype(jnp.int32)
    padded = (counts + bm - 1) // bm * bm
    pends = jnp.cumsum(padded)
    pstarts = pends - padded
    experts = jnp.arange(N_EXPERTS, dtype=jnp.int32)
    dest = jnp.sum(jnp.where(idx[..., None] == experts, pstarts, 0), axis=-1) + rank
    return dest.astype(jnp.int32), (pstarts // bm).astype(jnp.int32), (padded // bm).astype(jnp.int32)


def _extended_in_weights(w_in, b_in):
    a_q = SWA_Q_HEADS * HEAD_DIM
    a_kv = SWA_KV_HEADS * HEAD_DIM
    b_w = DIFF_HEADS * DIFF_V_DIM
    widths = (a_q, 2 * a_kv, 2 * a_kv, b_w, b_w, b_w)
    return w_in.astype(BF16), b_in.reshape(1, -1), widths


def kernel(x, c, positions, w_ada, b_ada, g_mix, w_in, b_in, attn_sinks, lambda_q1, lambda_k1, lambda_q2,
           lambda_k2, g_subln, w_out, b_out, g_ffn, w_router, b_router, w1, b1, w2, b2, g_final):
    B, S, D = x.shape
    N = B * S
    depth = w_ada.shape[0]
    n_rows = (N * TOP_K + N_EXPERTS * (BLOCK_ROWS - 1) + BLOCK_ROWS - 1) // BLOCK_ROWS * BLOCK_ROWS

    inv = 1.0 / (ROPE_THETA ** (jnp.arange(0, HEAD_DIM, 2, dtype=F32) / HEAD_DIM))
    inv_lane = jnp.tile(inv, LANES // (HEAD_DIM // 2)).reshape(1, LANES)
    pos2 = positions.reshape(N, 1)
    xcur = x.reshape(N, D)

    for layer in range(depth):
        last = layer == depth - 1
        lambda_init = 0.8 - 0.6 * math.exp(-0.3 * layer)
        mod3 = _adaln(c, w_ada[layer], b_ada[layer]).reshape(B, N_MOD, D)

        w_ext, b_ext, widths = _extended_in_weights(w_in[layer], b_in[layer])
        qa, ka2, va2, qdt, kd, vdt = _inproj(xcur, pos2, inv_lane, mod3, g_mix[layer].reshape(1, D),
                                             w_ext, b_ext, S, widths)
        out_a = _swa(attn_sinks[layer], qa, ka2, va2, B, S)
        lam_vecs = jnp.stack([lambda_q1[layer], lambda_k1[layer], lambda_q2[layer], lambda_k2[layer]])
        out_b = _diffattn(lam_vecs, g_subln[layer].reshape(1, DIFF_V_DIM), qdt, kd, vdt, B, S, lambda_init)

        x1, h2, idx, gate, rank, counts = _outproj(
            out_a, out_b, xcur, mod3, w_out[layer].astype(BF16), b_out[layer].reshape(1, D),
            g_ffn[layer].reshape(1, D), w_router[layer].T, b_router[layer].reshape(N_EXPERTS, 1), S)

        dest, first_blk, n_blk = _routing_tables(counts[:, 0], idx, rank)
        x_rows = _sc_scatter_rows(h2, dest, n_rows)
        y_rows = _experts(first_blk, n_blk, x_rows, w1[layer],
                          b1[layer][:, None, 0::2], b1[layer][:, None, 1::2],
                          w2[layer], b2[layer][:, None, :])
        y_tok = _sc_gather_rows(y_rows, dest.reshape(-1)).reshape(TOP_K, N, D // 2)
        xcur = _combine(x1, gate.T, mod3, g_final.reshape(1, D), y_tok, S, final_norm=last)
    return xcur.reshape(B, S, D)
```

```python
import functools
import math

import jax
import jax.numpy as jnp
from jax import lax
from jax.experimental import pallas as pl
from jax.experimental.pallas import tpu as pltpu
from jax.experimental.pallas import tpu_sc as plsc

HEAD_DIM = 64
SWA_Q_HEADS = 8
SWA_KV_HEADS = 2
SWA_GROUP = SWA_Q_HEADS // SWA_KV_HEADS
WINDOW = 128
DIFF_HEADS = 4
DIFF_V_DIM = 2 * HEAD_DIM
ROPE_THETA = 10000.0
N_EXPERTS = 32
TOP_K = 4
SWIGLU_ALPHA = 1.702
SWIGLU_LIMIT = 7.0
EPS = 1e-5
N_MOD = 6

LANES = 128
F32 = jnp.float32
BF16 = jnp.bfloat16
NEG_INF = float("-inf")

TM_PROJ = 1024
TQ_SWA = 512
VT_ROWS = DIFF_V_DIM + 16
TM_OUT = 1024
TM_ROWS = 1024
BLOCK_ROWS = 512
VMEM_LIMIT = 48 * 1024 * 1024


def _cparams(sem, vmem=VMEM_LIMIT):
    return pltpu.CompilerParams(dimension_semantics=sem, vmem_limit_bytes=vmem)


def _adaln_kernel(ct_ref, w_ref, b_ref, o_ref):
    c = ct_ref[...]
    cond = c * jax.nn.sigmoid(c)
    w = w_ref[...]
    rows = [jnp.sum(w * cond[:, b:b + 1], axis=0, keepdims=True) for b in range(c.shape[1])]
    o_ref[...] = jnp.concatenate(rows, axis=0) + b_ref[...]


def _adaln(c, w_ada, b_ada):
    B, D = c.shape
    n_out = w_ada.shape[1]
    tn = 1024
    return pl.pallas_call(
        _adaln_kernel,
        out_shape=jax.ShapeDtypeStruct((B, n_out), F32),
        grid=(n_out // tn,),
        in_specs=[pl.BlockSpec((D, B), lambda j: (0, 0)),
                  pl.BlockSpec((D, tn), lambda j: (0, j)),
                  pl.BlockSpec((1, tn), lambda j: (0, j))],
        out_specs=pl.BlockSpec((B, tn), lambda j: (0, j)),
        compiler_params=_cparams(("arbitrary",)),
        name="adaln",
    )(c.T, w_ada, b_ada.reshape(1, n_out))


def _rms(x):
    return x * lax.rsqrt(jnp.mean(x * x, axis=-1, keepdims=True) + EPS)


HIGH_HALF = -65536


def _pack_bf16_pairs(v):
    bits = lax.bitcast_convert_type(v.astype(BF16).astype(F32), jnp.int32)
    half = v.shape[1] // 2
    return lax.shift_right_logical(bits[:, :half], 16) | (bits[:, half:] & jnp.int32(HIGH_HALF))


def _unpack_bf16_pairs(w):
    return jnp.concatenate([lax.bitcast_convert_type(lax.shift_left(w, 16), F32),
                            lax.bitcast_convert_type(w & jnp.int32(HIGH_HALF), F32)], axis=1)


def _inproj_kernel(x_ref, pos_ref, inv_ref, mod_ref, g_ref, w_ref, b_ref,
                   qa_ref, ka_ref, va_ref, qd_ref, kd_ref, vd_ref):
    x = x_ref[...]
    sh = mod_ref[0, 0:1, :]
    sc = mod_ref[0, 1:2, :]
    h = _rms(x) * g_ref[...] * (1.0 + sc) + sh
    proj = jnp.dot(h.astype(BF16), w_ref[...], preferred_element_type=F32) + b_ref[...]

    lane = lax.broadcasted_iota(jnp.int32, (1, LANES), 1)
    first_half = (lane & (HEAD_DIM - 1)) < (HEAD_DIM // 2)
    n_freq = HEAD_DIM // 2
    groups = LANES // n_freq
    tm = x.shape[0]
    rows = tm // groups
    group = lax.shift_right_logical(lane, n_freq.bit_length() - 1)
    pos = pos_ref[...].astype(F32)
    pos_q = pos[0:rows]
    for g in range(1, groups):
        pos_q = jnp.where(group == g, pos[g * rows:(g + 1) * rows], pos_q)
    ang_q = pos_q * inv_ref[...]

    def spread(table_q):
        blocks = []
        for g in range(groups):
            only = jnp.where(group == g, table_q, 0.0)
            full = only
            for r in range(1, groups):
                full = full + pltpu.roll(only, r * n_freq, axis=1)
            blocks.append(full)
        return jnp.concatenate(blocks, axis=0)

    cos = spread(jnp.cos(ang_q))
    sin = spread(jnp.sin(ang_q))
    sin_signed = jnp.where(first_half, -sin, sin)

    def rope(t):
        partner = jnp.where(first_half,
                            pltpu.roll(t, LANES - HEAD_DIM // 2, axis=1),
                            pltpu.roll(t, HEAD_DIM // 2, axis=1))
        return t * cos + partner * sin_signed

    lo_half = lane < HEAD_DIM

    def emit(out_ref, col0, width, rotary, scale, transposed, doubled=False):
        for j in range(width // LANES):
            t = proj[:, col0 + j * LANES: col0 + (j + 1) * LANES]
            if rotary:
                t = rope(t)
            if scale != 1.0:
                t = t * scale
            if doubled:
                swapped = pltpu.roll(t, HEAD_DIM, axis=1)
                out_ref[:, 2 * j * LANES:(2 * j + 1) * LANES] = jnp.where(lo_half, t, swapped).astype(out_ref.dtype)
                out_ref[:, (2 * j + 1) * LANES:(2 * j + 2) * LANES] = jnp.where(lo_half, swapped, t).astype(out_ref.dtype)
            elif transposed:
                rows = out_ref.shape[2] // (width // LANES)
                out_ref[0, 0, j * rows:j * rows + LANES, :] = t.T.astype(out_ref.dtype)
                if rows > LANES:
                    fill = lax.broadcasted_iota(jnp.int32, (rows - LANES, t.shape[0]), 0) == 0
                    out_ref[0, 0, j * rows + LANES:(j + 1) * rows, :] = fill.astype(out_ref.dtype)
            else:
                out_ref[:, j * LANES:(j + 1) * LANES] = t.astype(out_ref.dtype)

    swa_scale = 1.0 / math.sqrt(HEAD_DIM)
    diff_scale = math.log2(math.e) / math.sqrt(HEAD_DIM)
    col = 0
    for out_ref, width, rotary, scale, transposed, doubled in (
            (qa_ref, qa_ref.shape[1], True, swa_scale, False, False),
            (ka_ref, ka_ref.shape[1] // 2, True, 1.0, False, True),
            (va_ref, va_ref.shape[1] // 2, False, 1.0, False, True),
            (qd_ref, qd_ref.shape[2], True, diff_scale, True, False),
            (kd_ref, kd_ref.shape[1], True, 1.0, False, False),
            (vd_ref, vd_ref.shape[2] // VT_ROWS * LANES, False, 1.0, True, False)):
        emit(out_ref, col, width, rotary, scale, transposed, doubled)
        col += width


def _inproj(x2, pos2, inv_lane, mod3, g_mix, w_ext, b_ext, S, widths):
    N, D = x2.shape
    tm = TM_PROJ
    C = w_ext.shape[1]
    tiles_per_seq = S // tm
    row = lambda i: (i, 0)
    t_rows = (0, 0, 0, widths[3], 0, widths[5] // LANES * VT_ROWS)
    out_shape, out_specs = [], []
    for w, tr in zip(widths, t_rows):
        if tr:
            out_shape.append(jax.ShapeDtypeStruct((N // S, tiles_per_seq, tr, tm), BF16))
            out_specs.append(pl.BlockSpec((1, 1, tr, tm), lambda i: (i // tiles_per_seq, i % tiles_per_seq, 0, 0)))
        else:
            out_shape.append(jax.ShapeDtypeStruct((N, w), BF16))
            out_specs.append(pl.BlockSpec((tm, w), row))
    return pl.pallas_call(
        _inproj_kernel,
        out_shape=out_shape,
        grid=(N // tm,),
        in_specs=[pl.BlockSpec((tm, D), row),
                  pl.BlockSpec((tm, 1), row),
                  pl.BlockSpec((1, LANES), lambda i: (0, 0)),
                  pl.BlockSpec((1, N_MOD, D), lambda i: (i // tiles_per_seq, 0, 0)),
                  pl.BlockSpec((1, D), lambda i: (0, 0)),
                  pl.BlockSpec((D, C), lambda i: (0, 0)),
                  pl.BlockSpec((1, C), lambda i: (0, 0))],
        out_specs=out_specs,
        compiler_params=_cparams(("arbitrary",)),
        name="inproj",
    )(x2, pos2, inv_lane, mod3, g_mix, w_ext, b_ext)


def _swa_kernel(sink_ref, q_ref, kc_ref, kp_ref, vc_ref, vp_ref, o_ref):
    i = pl.program_id(1)
    tq = q_ref.shape[0]
    lane = lax.broadcasted_iota(jnp.int32, (1, LANES), 1)
    lo = lane < HEAD_DIM
    qi = lax.broadcasted_iota(jnp.int32, (WINDOW, 2 * WINDOW), 0) + WINDOW
    kj = lax.broadcasted_iota(jnp.int32, (WINDOW, 2 * WINDOW), 1)
    band = (qi - kj >= 0) & (qi - kj < WINDOW)
    dn = (((1,), (1,)), ((), ()))
    for c in range(tq // WINDOW):
        if c == 0:
            kcat = jnp.concatenate([kp_ref[...], kc_ref[0:WINDOW, :]], axis=0)
            vcat = jnp.concatenate([vp_ref[...], vc_ref[0:WINDOW, :]], axis=0)
            mask = band & (kj >= jnp.where(i > 0, 0, WINDOW))
        else:
            kcat = kc_ref[(c - 1) * WINDOW:(c + 1) * WINDOW, :]
            vcat = vc_ref[(c - 1) * WINDOW:(c + 1) * WINDOW, :]
            mask = band
        for j in range(SWA_KV_HEADS):
            kj2 = kcat[:, j * LANES:(j + 1) * LANES]
            vj2 = vcat[:, j * LANES:(j + 1) * LANES]
            zero = jnp.zeros_like(kj2)
            k_halves = (jnp.where(lo, kj2, zero), jnp.where(lo, zero, kj2))
            v_halves = (jnp.where(lo, vj2, zero), jnp.where(lo, zero, vj2))
            for p in range(SWA_GROUP // 2):
                g = j * (SWA_GROUP // 2) + p
                q = q_ref[c * WINDOW:(c + 1) * WINDOW, g * LANES:(g + 1) * LANES]
                out = jnp.zeros((WINDOW, LANES), F32)
                for half in range(2):
                    sink = sink_ref[2 * g + half]
                    s = lax.dot_general(q, k_halves[half], dn, preferred_element_type=F32)
                    s = jnp.where(mask, s, NEG_INF)
                    m = jnp.maximum(jnp.max(s, axis=1, keepdims=True), sink)
                    e = jnp.exp(s - m)
                    denom = jnp.sum(e, axis=1, keepdims=True) + jnp.exp(sink - m)
                    pv = jnp.dot(e.astype(BF16), v_halves[half], preferred_element_type=F32)
                    out = out + pv / denom
                o_ref[c * WINDOW:(c + 1) * WINDOW, g * LANES:(g + 1) * LANES] = out.astype(o_ref.dtype)


def _swa(sinks, qa, ka2, va2, B, S):
    N = qa.shape[0]
    tq = TQ_SWA
    nq = S // tq
    wpt = tq // WINDOW
    wps = S // WINDOW
    cur = lambda b, i: (b * nq + i, 0)
    prev = lambda b, i: (b * wps + jnp.maximum(i * wpt - 1, 0), 0)
    return pl.pallas_call(
        _swa_kernel,
        out_shape=jax.ShapeDtypeStruct((N, qa.shape[1]), BF16),
        grid=(B, nq),
        in_specs=[pl.BlockSpec(memory_space=pltpu.SMEM),
                  pl.BlockSpec((tq, qa.shape[1]), cur),
                  pl.BlockSpec((tq, ka2.shape[1]), cur),
                  pl.BlockSpec((WINDOW, ka2.shape[1]), prev),
                  pl.BlockSpec((tq, va2.shape[1]), cur),
                  pl.BlockSpec((WINDOW, va2.shape[1]), prev)],
        out_specs=pl.BlockSpec((tq, qa.shape[1]), cur),
        compiler_params=_cparams(("arbitrary", "arbitrary")),
        name="swa",
    )(sinks, qa, ka2, ka2, va2, va2)


def _diff_kernel(lam_ref, g_ref, qt_ref, k_ref, vt_ref, o_ref, sa_ref, sb_ref, m_ref, acc_ref, *, lambda_init):
    i = pl.program_id(2)
    tq = qt_ref.shape[3]
    tk = vt_ref.shape[3]
    qt = qt_ref[0, 0]
    lane = lax.broadcasted_iota(jnp.int32, (1, LANES), 1)
    lo = lane < HEAD_DIM
    m_ref[...] = jnp.full(m_ref.shape, NEG_INF, F32)
    acc_ref[...] = jnp.zeros(acc_ref.shape, F32)

    def scores(c, s_ref):
        k = k_ref[pl.ds(pl.multiple_of(c * tk, tk), tk), :]
        zero = jnp.zeros_like(k)
        s_ref[0] = jnp.dot(jnp.where(lo, k, zero), qt, preferred_element_type=F32)
        s_ref[1] = jnp.dot(jnp.where(lo, zero, k), qt, preferred_element_type=F32)

    def consume(c, s_ref, first_key):
        vt = vt_ref[0, c]
        for mp in range(2):
            s = s_ref[mp]
            if first_key is not None:
                kpos = lax.broadcasted_iota(jnp.int32, (tk, tq), 0) + first_key
                qpos = lax.broadcasted_iota(jnp.int32, (tk, tq), 1)
                s = jnp.where(kpos <= qpos, s, NEG_INF)
            m_prev = m_ref[mp]
            m_new = jnp.maximum(m_prev, jnp.max(s, axis=0, keepdims=True))
            alpha = jnp.exp2(m_prev - m_new)
            p = jnp.exp2(s - m_new).astype(BF16)
            acc_ref[mp] = alpha * acc_ref[mp] + jnp.dot(vt, p, preferred_element_type=F32)
            m_ref[mp] = m_new

    scores(0, sa_ref)

    def pair(jj, carry):
        c = 2 * jj
        scores(c + 1, sb_ref)
        consume(c, sa_ref, None)
        scores(c + 2, sa_ref)
        consume(c + 1, sb_ref, None)
        return carry

    lax.fori_loop(0, lax.shift_right_logical(i, 1), pair, 0)

    @pl.when(i % 2 == 0)
    def _():
        consume(i, sa_ref, 0)

    @pl.when(i % 2 == 1)
    def _():
        scores(i, sb_ref)
        consume(i - 1, sa_ref, None)
        consume(i, sb_ref, 0)

    lq1, lk1, lq2, lk2 = (lam_ref[r:r + 1, :] for r in range(4))
    lam = (jnp.exp(jnp.sum(lq1 * lk1, axis=1, keepdims=True))
           - jnp.exp(jnp.sum(lq2 * lk2, axis=1, keepdims=True)) + lambda_init)
    d = DIFF_V_DIM
    ot = (acc_ref[0, 0:d, :] / acc_ref[0, d:d + 1, :]
          - lam * (acc_ref[1, 0:d, :] / acc_ref[1, d:d + 1, :]))
    ot = ot * lax.rsqrt(jnp.mean(ot * ot, axis=0, keepdims=True) + EPS)
    o_ref[...] = (ot.T * g_ref[...] * (1.0 - lambda_init)).astype(o_ref.dtype)


def _diffattn(lam_vecs, g_subln, qdt, kd, vdt, B, S, lambda_init):
    N, C = kd.shape
    tq = qdt.shape[3]
    nq = S // tq
    return pl.pallas_call(
        functools.partial(_diff_kernel, lambda_init=lambda_init),
        out_shape=jax.ShapeDtypeStruct((N, C), BF16),
        grid=(B, DIFF_HEADS, nq),
        in_specs=[pl.BlockSpec((4, HEAD_DIM), lambda b, h, i: (0, 0)),
                  pl.BlockSpec((1, DIFF_V_DIM), lambda b, h, i: (0, 0)),
                  pl.BlockSpec((1, 1, LANES, tq), lambda b, h, i: (b, i, h, 0)),
                  pl.BlockSpec((S, LANES), lambda b, h, i: (b, h)),
                  pl.BlockSpec((1, nq, VT_ROWS, tq), lambda b, h, i: (b, 0, h, 0))],
        out_specs=pl.BlockSpec((tq, LANES), lambda b, h, i: (b * nq + i, h)),
        scratch_shapes=[pltpu.VMEM((2, tq, tq), F32), pltpu.VMEM((2, tq, tq), F32),
                        pltpu.VMEM((2, 1, tq), F32), pltpu.VMEM((2, VT_ROWS, tq), F32)],
        compiler_params=_cparams(("arbitrary", "arbitrary", "arbitrary")),
        name="diffattn",
    )(lam_vecs, g_subln, qdt, kd, vdt)


def _outproj_kernel(oa_ref, ob_ref, x_ref, mod_ref, wo_ref, bo_ref, g_ref, wr_ref, br_ref,
                    x1_ref, h2_ref, idx_ref, gate_ref, rank_ref, cnt_ref, carry_ref):
    i = pl.program_id(0)
    tm = x_ref.shape[0]
    half = oa_ref.shape[1]

    @pl.when(i == 0)
    def _():
        carry_ref[...] = jnp.zeros(carry_ref.shape, F32)

    gt1 = mod_ref[0, 2:3, :]
    sh2 = mod_ref[0, 3:4, :]
    sc2 = mod_ref[0, 4:5, :]
    mixed = (jnp.dot(oa_ref[...], wo_ref[0:half, :], preferred_element_type=F32)
             + jnp.dot(ob_ref[...], wo_ref[half:, :], preferred_element_type=F32) + bo_ref[...])
    x1 = x_ref[...] + gt1 * mixed
    x1_ref[...] = x1
    h2 = _rms(x1) * g_ref[...] * (1.0 + sc2) + sh2
    dn = (((1,), (1,)), ((), ()))
    h_hi = h2.astype(BF16)
    h2_ref[...] = _pack_bf16_pairs(h2)
    h_lo = (h2 - h_hi.astype(F32)).astype(BF16)
    w = wr_ref[...]
    w_hi = w.astype(BF16)
    w_lo = (w - w_hi.astype(F32)).astype(BF16)
    logits = (lax.dot_general(w_hi, h_hi, dn, preferred_element_type=F32)
              + lax.dot_general(w_hi, h_lo, dn, preferred_element_type=F32)
              + lax.dot_general(w_lo, h_hi, dn, preferred_element_type=F32)
              + br_ref[...])

    eidx = lax.broadcasted_iota(jnp.int32, logits.shape, 0)
    vals = logits
    onehots, top_vals, top_idx = [], [], []
    for _k in range(TOP_K):
        mx = jnp.max(vals, axis=0, keepdims=True)
        sel = jnp.min(jnp.where(vals == mx, eidx, N_EXPERTS), axis=0, keepdims=True)
        oh = eidx == sel
        onehots.append(oh)
        top_vals.append(mx)
        top_idx.append(sel)
        vals = jnp.where(oh, NEG_INF, vals)
    exps = [jnp.exp(v - top_vals[0]) for v in top_vals]
    denom = exps[0] + exps[1] + exps[2] + exps[3]
    gate_ref[...] = jnp.concatenate([e / denom for e in exps], axis=0)
    idx_ref[...] = jnp.concatenate(top_idx, axis=0)

    member = (onehots[0] | onehots[1] | onehots[2] | onehots[3])
    member_f = member.astype(F32)
    t_src = lax.broadcasted_iota(jnp.int32, (tm, tm), 0)
    t_dst = lax.broadcasted_iota(jnp.int32, (tm, tm), 1)
    before = (t_src < t_dst).astype(BF16)
    prefix = jnp.dot(member.astype(BF16), before, preferred_element_type=F32) + carry_ref[...]
    ranks = [jnp.sum(jnp.where(oh, prefix, 0.0), axis=0, keepdims=True) for oh in onehots]
    rank_ref[...] = jnp.concatenate(ranks, axis=0).astype(jnp.int32)
    carry_ref[...] = carry_ref[...] + jnp.sum(member_f, axis=1, keepdims=True)
    cnt_ref[...] = jnp.broadcast_to(carry_ref[...], cnt_ref.shape)


def _outproj(out_a, out_b, x2, mod3, w_out, b_out, g_ffn, wr_t, br_col, S):
    N, D = x2.shape
    tm = TM_OUT
    tiles_per_seq = S // tm
    row = lambda i: (i, 0)
    colb = lambda i: (0, i)
    const = lambda i: (0, 0)
    return pl.pallas_call(
        _outproj_kernel,
        out_shape=[jax.ShapeDtypeStruct((N, D), F32), jax.ShapeDtypeStruct((N, D // 2), jnp.int32),
                   jax.ShapeDtypeStruct((TOP_K, N), jnp.int32), jax.ShapeDtypeStruct((TOP_K, N), F32),
                   jax.ShapeDtypeStruct((TOP_K, N), jnp.int32), jax.ShapeDtypeStruct((N_EXPERTS, LANES), F32)],
        grid=(N // tm,),
        in_specs=[pl.BlockSpec((tm, out_a.shape[1]), row),
                  pl.BlockSpec((tm, out_b.shape[1]), row),
                  pl.BlockSpec((tm, D), row),
                  pl.BlockSpec((1, N_MOD, D), lambda i: (i // tiles_per_seq, 0, 0)),
                  pl.BlockSpec(w_out.shape, const),
                  pl.BlockSpec((1, D), const),
                  pl.BlockSpec((1, D), const),
                  pl.BlockSpec(wr_t.shape, const),
                  pl.BlockSpec((N_EXPERTS, 1), const)],
        out_specs=[pl.BlockSpec((tm, D), row), pl.BlockSpec((tm, D // 2), row),
                   pl.BlockSpec((TOP_K, tm), colb), pl.BlockSpec((TOP_K, tm), colb),
                   pl.BlockSpec((TOP_K, tm), colb), pl.BlockSpec((N_EXPERTS, LANES), const)],
        scratch_shapes=[pltpu.VMEM((N_EXPERTS, 1), F32)],
        compiler_params=_cparams(("arbitrary",)),
        name="outproj_router",
    )(out_a, out_b, x2, mod3, w_out, b_out, g_ffn, wr_t, br_col)


def _experts_kernel(first_ref, nblk_ref, x_ref, w1_ref, b1g_ref, b1l_ref, w2_ref, b2_ref, y_ref,
                    w1g_a, w1l_a, w2_a, w1g_b, w1l_b, w2_b, xbuf, ybuf, xsem, ysem):
    s = pl.program_id(0)
    n_exp = pl.num_programs(0) - 1
    e = jnp.maximum(s - 1, 0)
    first = first_ref[e]
    n_blk = jnp.where(s >= 1, nblk_ref[e], 0)
    nxt = jnp.minimum(s, n_exp - 1)
    do_prep = (s < n_exp) & (nblk_ref[nxt] > 0)
    bm = xbuf.shape[1]

    def x_copy(j, slot, first_blk=None):
        blk = (first if first_blk is None else first_blk) + j
        rows = pl.ds(pl.multiple_of(blk * bm, bm), bm)
        return pltpu.make_async_copy(x_ref.at[rows], xbuf.at[slot], xsem.at[slot])

    def y_copy(j, slot):
        rows = pl.ds(pl.multiple_of((first + j) * bm, bm), bm)
        return pltpu.make_async_copy(ybuf.at[slot], y_ref.at[rows], ysem.at[slot])

    def swiglu(ug, ul):
        glu = jnp.minimum(ug, SWIGLU_LIMIT)
        lin = jnp.clip(ul, -SWIGLU_LIMIT, SWIGLU_LIMIT)
        return (glu * jax.nn.sigmoid(SWIGLU_ALPHA * glu) * (lin + 1.0)).astype(BF16)

    def relayout(new):
        w1g_s, w1l_s, w2_s = new
        ch = 256
        half = ch // 2
        for c in range(w1_ref.shape[2] // ch):
            t = w1_ref[0, :, c * ch:(c + 1) * ch].astype(BF16).T
            pairs = pltpu.bitcast(t, jnp.int32)
            cols = slice(c * half, (c + 1) * half)
            w1g_s[:, cols] = lax.bitcast_convert_type(lax.shift_left(pairs, 16), F32).astype(BF16).T
            w1l_s[:, cols] = lax.bitcast_convert_type(pairs & jnp.int32(HIGH_HALF), F32).astype(BF16).T
        w2_s[...] = w2_ref[0].astype(BF16)

    def block(j, cur, new=None):
        w1g_s, w1l_s, w2_s = cur
        slot = j & 1
        x_copy(j, slot).wait()

        @pl.when(j + 1 < n_blk)
        def _():
            x_copy(j + 1, 1 - slot).start(priority=1)

        @pl.when(j >= 2)
        def _():
            y_copy(j - 2, slot).wait()

        if new is not None:
            relayout(new)
        x = _unpack_bf16_pairs(xbuf[slot]).astype(BF16)
        ug = jnp.dot(x, w1g_s[...], preferred_element_type=F32) + b1g_ref[0]
        ul = jnp.dot(x, w1l_s[...], preferred_element_type=F32) + b1l_ref[0]
        y = jnp.dot(swiglu(ug, ul), w2_s[...], preferred_element_type=F32) + b2_ref[0]
        ybuf[slot] = _pack_bf16_pairs(y)
        y_copy(j, slot).start(priority=1)

    def step(new, cur):
        @pl.when((n_blk > 0) & do_prep)
        def _():
            block(0, cur, new)

        @pl.when((n_blk > 0) & jnp.logical_not(do_prep))
        def _():
            block(0, cur)

        @pl.when((n_blk == 0) & do_prep)
        def _():
            relayout(new)

        def later_block(j, carry):
            block(j, cur)
            return carry

        lax.fori_loop(1, n_blk, later_block, 0)

    set_a, set_b = (w1g_a, w1l_a, w2_a), (w1g_b, w1l_b, w2_b)

    @pl.when(s % 2 == 0)
    def _():
        step(set_a, set_b)

    @pl.when(s % 2 == 1)
    def _():
        step(set_b, set_a)

    @pl.when(do_prep)
    def _():
        x_copy(0, 0, first_ref[nxt]).start(priority=1)

    @pl.when(n_blk >= 2)
    def _():
        y_copy(n_blk - 2, n_blk & 1).wait()

    @pl.when(n_blk >= 1)
    def _():
        y_copy(n_blk - 1, (n_blk - 1) & 1).wait()


def _experts(first_blk, n_blk, x_rows, w1, b1g, b1l, w2, b2):
    E, Fh, D = w2.shape
    bm = BLOCK_ROWS
    n_rows = x_rows.shape[0]
    wsel = lambda s, fb, nb: (jnp.minimum(s, E - 1), 0, 0)
    bsel = lambda s, fb, nb: (jnp.maximum(s - 1, 0), 0, 0)
    return pl.pallas_call(
        _experts_kernel,
        out_shape=jax.ShapeDtypeStruct((n_rows, D // 2), jnp.int32),
        grid_spec=pltpu.PrefetchScalarGridSpec(
            num_scalar_prefetch=2,
            grid=(E + 1,),
            in_specs=[pl.BlockSpec(memory_space=pl.ANY),
                      pl.BlockSpec((1, D, 2 * Fh), wsel),
                      pl.BlockSpec((1, 1, Fh), bsel),
                      pl.BlockSpec((1, 1, Fh), bsel),
                      pl.BlockSpec((1, Fh, D), wsel),
                      pl.BlockSpec((1, 1, D), bsel)],
            out_specs=pl.BlockSpec(memory_space=pl.ANY),
            scratch_shapes=[pltpu.VMEM((D, Fh), BF16), pltpu.VMEM((D, Fh), BF16), pltpu.VMEM((Fh, D), BF16),
                            pltpu.VMEM((D, Fh), BF16), pltpu.VMEM((D, Fh), BF16), pltpu.VMEM((Fh, D), BF16),
                            pltpu.VMEM((2, bm, D // 2), jnp.int32), pltpu.VMEM((2, bm, D // 2), jnp.int32),
                            pltpu.SemaphoreType.DMA((2,)), pltpu.SemaphoreType.DMA((2,))]),
        compiler_params=_cparams(("arbitrary",), 56 * 1024 * 1024),
        name="experts",
    )(first_blk, n_blk, x_rows, w1, b1g, b1l, w2, b2)


SC_CORES = 2
SC_SUBCORES = 16
SC_CHUNK = 64


def _sc_gather_rows(table, idx):
    M = idx.shape[0]
    D = table.shape[1]
    workers = SC_CORES * SC_SUBCORES
    per_worker = M // workers
    n_chunks = per_worker // SC_CHUNK
    assert M % workers == 0 and per_worker % (2 * SC_CHUNK) == 0
    mesh = plsc.VectorSubcoreMesh(core_axis_name="c", subcore_axis_name="s")

    @functools.partial(
        pl.kernel, mesh=mesh,
        out_type=jax.ShapeDtypeStruct((M, D), table.dtype),
        scratch_types=[pltpu.VMEM((per_worker,), jnp.int32),
                       pltpu.VMEM((SC_CHUNK, D), table.dtype), pltpu.VMEM((SC_CHUNK, D), table.dtype),
                       pltpu.SemaphoreType.DMA, pltpu.SemaphoreType.DMA],
        name="sc_gather_rows")
    def gather(table_hbm, idx_hbm, out_hbm, idx_v, rows0, rows1, sem0, sem1):
        wid = lax.axis_index("s") * SC_CORES + lax.axis_index("c")
        base = wid * per_worker
        pltpu.sync_copy(idx_hbm.at[pl.ds(base, per_worker)], idx_v)

        def fetch(c, buf, sem):
            off = pl.multiple_of(c * SC_CHUNK, SC_CHUNK)
            return pltpu.make_async_copy(table_hbm.at[idx_v.at[pl.ds(off, SC_CHUNK)]], buf, sem)

        def flush(c, buf):
            off = pl.multiple_of(c * SC_CHUNK, SC_CHUNK)
            pltpu.sync_copy(buf, out_hbm.at[pl.ds(base + off, SC_CHUNK)])

        fetch(0, rows0, sem0).start()

        @pl.loop(0, n_chunks // 2)
        def _(jj):
            c = 2 * jj
            fetch(c + 1, rows1, sem1).start()
            fetch(c, rows0, sem0).wait()
            flush(c, rows0)

            @pl.when(c + 2 < n_chunks)
            def _():
                fetch(c + 2, rows0, sem0).start()

            fetch(c + 1, rows1, sem1).wait()
            flush(c + 1, rows1)

    return gather(table, idx)


SC_SCATTER_CHUNK = 128


def _sc_scatter_rows(rows, dest, n_rows):
    N, W = rows.shape
    workers = SC_CORES * SC_SUBCORES
    per_worker = N // workers
    chunks = per_worker // SC_SCATTER_CHUNK
    assert N % workers == 0 and per_worker % SC_SCATTER_CHUNK == 0
    dest3 = dest.reshape(TOP_K, N // SC_SCATTER_CHUNK, SC_SCATTER_CHUNK)
    mesh = plsc.VectorSubcoreMesh(core_axis_name="c", subcore_axis_name="s")

    @functools.partial(
        pl.kernel, mesh=mesh,
        out_type=jax.ShapeDtypeStruct((n_rows, W), rows.dtype),
        scratch_types=[pltpu.VMEM((TOP_K, chunks, SC_SCATTER_CHUNK), jnp.int32),
                       pltpu.VMEM((SC_SCATTER_CHUNK, W), rows.dtype)],
        name="sc_scatter_rows")
    def scatter(rows_hbm, dest_hbm, out_hbm, idx_v, rows_v):
        wid = lax.axis_index("s") * SC_CORES + lax.axis_index("c")
        for k in range(TOP_K):
            pltpu.sync_copy(dest_hbm.at[k, pl.ds(wid * chunks, chunks)], idx_v.at[k])

        @pl.loop(0, chunks)
        def _(j):
            start = pl.multiple_of(wid * per_worker + j * SC_SCATTER_CHUNK, SC_SCATTER_CHUNK)
            pltpu.sync_copy(rows_hbm.at[pl.ds(start, SC_SCATTER_CHUNK)], rows_v)
            for k in range(TOP_K):
                pltpu.sync_copy(rows_v, out_hbm.at[idx_v.at[k, j]])

    return scatter(rows, dest3)


def _combine_kernel(x1_ref, gate_ref, mod_ref, g_ref, y_ref, o_ref, *, final_norm):
    gate = gate_ref[...]
    moe = gate[:, 0:1] * _unpack_bf16_pairs(y_ref[0])
    for k in range(1, TOP_K):
        moe = moe + gate[:, k:k + 1] * _unpack_bf16_pairs(y_ref[k])
    gt2 = mod_ref[0, 5:6, :]
    x2 = x1_ref[...] + gt2 * moe
    o_ref[...] = _rms(x2) * g_ref[...] if final_norm else x2


def _combine(x1, gate_t, mod3, g_final, y_tok, S, final_norm):
    N, D = x1.shape
    tm = TM_ROWS
    tiles_per_seq = S // tm
    row = lambda i: (i, 0)
    return pl.pallas_call(
        functools.partial(_combine_kernel, final_norm=final_norm),
        out_shape=jax.ShapeDtypeStruct((N, D), F32),
        grid=(N // tm,),
        in_specs=[pl.BlockSpec((tm, D), row),
                  pl.BlockSpec((tm, TOP_K), row),
                  pl.BlockSpec((1, N_MOD, D), lambda i: (i // tiles_per_seq, 0, 0)),
                  pl.BlockSpec((1, D), lambda i: (0, 0)),
                  pl.BlockSpec((TOP_K, tm, D // 2), lambda i: (0, i, 0))],
        out_specs=pl.BlockSpec((tm, D), row),
        compiler_params=_cparams(("arbitrary",)),
        name="combine",
    )(x1, gate_t, mod3, g_final, y_tok)


def _routing_tables(counts, idx, rank):
    bm = BLOCK_ROWS
    counts = counts.astype(jnp.int32)
    padded = (counts + bm - 1) // bm * bm
    pends = jnp.cumsum(padded)
    pstarts = pends - padded
    experts = jnp.arange(N_EXPERTS, dtype=jnp.int32)
    dest = jnp.sum(jnp.where(idx[..., None] == experts, pstarts, 0), axis=-1) + rank
    return dest.astype(jnp.int32), (pstarts // bm).astype(jnp.int32), (padded // bm).astype(jnp.int32)


def _extended_in_weights(w_in, b_in):
    a_q = SWA_Q_HEADS * HEAD_DIM
    a_kv = SWA_KV_HEADS * HEAD_DIM
    b_w = DIFF_HEADS * DIFF_V_DIM
    widths = (a_q, 2 * a_kv, 2 * a_kv, b_w, b_w, b_w)
    return w_in.astype(BF16), b_in.reshape(1, -1), widths


def kernel(x, c, positions, w_ada, b_ada, g_mix, w_in, b_in, attn_sinks, lambda_q1, lambda_k1, lambda_q2,
           lambda_k2, g_subln, w_out, b_out, g_ffn, w_router, b_router, w1, b1, w2, b2, g_final):
    B, S, D = x.shape
    N = B * S
    depth = w_ada.shape[0]
    n_rows = (N * TOP_K + N_EXPERTS * (BLOCK_ROWS - 1) + BLOCK_ROWS - 1) // BLOCK_ROWS * BLOCK_ROWS

    inv = 1.0 / (ROPE_THETA ** (jnp.arange(0, HEAD_DIM, 2, dtype=F32) / HEAD_DIM))
    inv_lane = jnp.tile(inv, LANES // (HEAD_DIM // 2)).reshape(1, LANES)
    pos2 = positions.reshape(N, 1)
    xcur = x.reshape(N, D)

    for layer in range(depth):
        last = layer == depth - 1
        lambda_init = 0.8 - 0.6 * math.exp(-0.3 * layer)
        mod3 = _adaln(c, w_ada[layer], b_ada[layer]).reshape(B, N_MOD, D)

        w_ext, b_ext, widths = _extended_in_weights(w_in[layer], b_in[layer])
        qa, ka2, va2, qdt, kd, vdt = _inproj(xcur, pos2, inv_lane, mod3, g_mix[layer].reshape(1, D),
                                             w_ext, b_ext, S, widths)
        out_a = _swa(attn_sinks[layer], qa, ka2, va2, B, S)
        lam_vecs = jnp.stack([lambda_q1[layer], lambda_k1[layer], lambda_q2[layer], lambda_k2[layer]])
        out_b = _diffattn(lam_vecs, g_subln[layer].reshape(1, DIFF_V_DIM), qdt, kd, vdt, B, S, lambda_init)

        x1, h2, idx, gate, rank, counts = _outproj(
            out_a, out_b, xcur, mod3, w_out[layer].astype(BF16), b_out[layer].reshape(1, D),
            g_ffn[layer].reshape(1, D), w_router[layer].T, b_router[layer].reshape(N_EXPERTS, 1), S)

        dest, first_blk, n_blk = _routing_tables(counts[:, 0], idx, rank)
        x_rows = _sc_scatter_rows(h2, dest, n_rows)
        y_rows = _experts(first_blk, n_blk, x_rows, w1[layer],
                          b1[layer][:, None, 0::2], b1[layer][:, None, 1::2],
                          w2[layer], b2[layer][:, None, :])
        y_tok = _sc_gather_rows(y_rows, dest.reshape(-1)).reshape(TOP_K, N, D // 2)
        xcur = _combine(x1, gate.T, mod3, g_final.reshape(1, D), y_tok, S, final_norm=last)
    return xcur.reshape(B, S, D)
```
